```python
import jax
import jax.numpy as jnp
from jax import lax
import numpy as np

D_MODEL = 1024
BATCH = 8
SEQ = 2048
DEPTH = 2

HEAD_DIM = 64
N_MIX_HEADS = D_MODEL // HEAD_DIM
MLSTM_HEADS = (3 * N_MIX_HEADS) // 8
ATTN_Q_HEADS = (3 * N_MIX_HEADS) // 8
ATTN_KV_HEADS = ATTN_Q_HEADS // 3
ATTN_GROUP = ATTN_Q_HEADS // ATTN_KV_HEADS
SGU_GROUPS = N_MIX_HEADS - MLSTM_HEADS - ATTN_Q_HEADS
MLSTM_W = MLSTM_HEADS * HEAD_DIM
ATTN_W = ATTN_Q_HEADS * HEAD_DIM
ATTN_KV_W = ATTN_KV_HEADS * HEAD_DIM
SGU_W = SGU_GROUPS * HEAD_DIM
D_MIX = MLSTM_W + ATTN_W + SGU_W
MLSTM_CHUNK = 128
CONV_WIDTH = 4
WINDOW = 128
ROPE_DIM = HEAD_DIM // 4
ROPE_THETA = 500000.0
SGU_CHUNK = 128
D_FF = 256 * ((8 * D_MODEL // 3 + 255) // 256)
N_EXPERTS = 8
TOP_K = 2
D_FF_EXPERT = 7 * D_MODEL // 2
N_DENSE = (DEPTH + 1) // 2
N_MOE = DEPTH // 2
DN_ALPHA = (2.0 * DEPTH) ** 0.25
DN_BETA = (8.0 * DEPTH) ** -0.25
LN_EPS = 1e-5
PROJ_SIZES = (MLSTM_W, MLSTM_W, MLSTM_W, MLSTM_W, MLSTM_HEADS, MLSTM_HEADS,
              ATTN_W, ATTN_KV_W, ATTN_KV_W, SGU_W, SGU_W)
PROJ_SPLITS = tuple(int(s) for s in np.cumsum(PROJ_SIZES)[:-1])
D_PROJ = int(sum(PROJ_SIZES))

kernel_name = 'hybrid_mlstm_swa_gmlp_moe_block'


def layer_norm(x, g, b=None, eps=LN_EPS):
    xf = x.astype(jnp.float32)
    xc = xf - jnp.mean(xf, -1, keepdims=True)
    y = xc * lax.rsqrt(jnp.mean(xc * xc, -1, keepdims=True) + eps) * g.astype(jnp.float32)
    if b is not None:
        y = y + b.astype(jnp.float32)
    return y.astype(x.dtype)


def causal_conv(x, w):
    K = w.shape[0]
    S = x.shape[1]
    xp = jnp.pad(x, ((0, 0), (K - 1, 0), (0, 0)))
    return sum(xp[:, j:j + S, :] * w[j] for j in range(K))


def partial_rope(x, cos, sin):
    half = ROPE_DIM // 2
    x1, x2, xp = x[..., :half], x[..., half:ROPE_DIM], x[..., ROPE_DIM:]
    return jnp.concatenate([x1 * cos - x2 * sin, x2 * cos + x1 * sin, xp], -1)


def mlstm_chunkwise(q, k, v, log_i, log_f):
    B, H, S, d = q.shape
    L = MLSTM_CHUNK
    NC = S // L
    q = q.reshape(B, H, NC, L, d) * (d ** -0.5)
    k = k.reshape(B, H, NC, L, d)
    v = v.reshape(B, H, NC, L, d)
    li = log_i.reshape(B, H, NC, L)
    bcum = jnp.cumsum(log_f.reshape(B, H, NC, L), axis=-1)
    b_tot = bcum[..., -1]
    a = b_tot[..., None] - bcum + li
    m_loc = jnp.max(a, -1)
    w_loc = jnp.exp(a - m_loc[..., None])
    C_loc = jnp.einsum('bhcl,bhclv,bhclk->bhcvk', w_loc, v, k)
    n_loc = jnp.einsum('bhcl,bhclk->bhck', w_loc, k)

    def step(carry, inp):
        C, n, m = carry
        bt, Cl, nl, ml = inp
        m_new = jnp.maximum(bt + m, ml)
        s_old = jnp.exp(bt + m - m_new)
        s_loc = jnp.exp(ml - m_new)
        C_new = s_old[..., None, None] * C + s_loc[..., None, None] * Cl
        n_new = s_old[..., None] * n + s_loc[..., None] * nl
        return (C_new, n_new, m_new), (C, n, m)

    init = (jnp.zeros((B, H, d, d), jnp.float32), jnp.zeros((B, H, d), jnp.float32),
            jnp.zeros((B, H), jnp.float32))
    xs = (jnp.moveaxis(b_tot, 2, 0), jnp.moveaxis(C_loc, 2, 0),
          jnp.moveaxis(n_loc, 2, 0), jnp.moveaxis(m_loc, 2, 0))
    _, (C_prev, n_prev, m_prev) = lax.scan(step, init, xs)
    C_prev = jnp.moveaxis(C_prev, 0, 2)
    n_prev = jnp.moveaxis(n_prev, 0, 2)
    m_prev = jnp.moveaxis(m_prev, 0, 2)
    D = bcum[..., :, None] - bcum[..., None, :] + li[..., None, :]
    D = jnp.where(jnp.tril(jnp.ones((L, L), bool)), D, -jnp.inf)
    m_inter = bcum + m_prev[..., None]
    m = jnp.maximum(m_inter, jnp.max(D, -1))
    Sm = jnp.einsum('bhcld,bhcsd->bhcls', q, k) * jnp.exp(D - m[..., None])
    w_inter = jnp.exp(m_inter - m)
    num = (jnp.einsum('bhcls,bhcsd->bhcld', Sm, v)
           + w_inter[..., None] * jnp.einsum('bhcvk,bhclk->bhclv', C_prev, q))
    den = jnp.sum(Sm, -1) + w_inter * jnp.einsum('bhck,bhclk->bhcl', n_prev, q)
    h = num / jnp.maximum(jnp.abs(den), jnp.exp(-m))[..., None]
    return h.reshape(B, H, S, d)


def swa_with_sinks(q, k, v, sinks):
    B, S, _, d = q.shape
    Lb = WINDOW
    NB = S // Lb
    q = q.reshape(B, NB, Lb, ATTN_KV_HEADS, ATTN_GROUP, d)
    k = k.reshape(B, NB, Lb, ATTN_KV_HEADS, d)
    v = v.reshape(B, NB, Lb, ATTN_KV_HEADS, d)
    prev = lambda t: jnp.concatenate([jnp.zeros_like(t[:, :1]), t[:, :-1]], axis=1)
    kk = jnp.concatenate([prev(k), k], axis=2)
    vv = jnp.concatenate([prev(v), v], axis=2)
    s = jnp.einsum('bnqhgd,bnkhd->bnhgqk', q, kk) * (d ** -0.5)
    blk = jnp.arange(NB)[:, None] * Lb
    q_pos = blk + jnp.arange(Lb)[None, :]
    k_pos = blk - Lb + jnp.arange(2 * Lb)[None, :]
    rel = q_pos[:, :, None] - k_pos[:, None, :]
    mask = (rel >= 0) & (rel < WINDOW) & (k_pos[:, None, :] >= 0)
    s = jnp.where(mask[None, :, None, None], s, -jnp.inf)
    sink = sinks.astype(jnp.float32).reshape(1, 1, ATTN_KV_HEADS, ATTN_GROUP, 1, 1)
    mx = jnp.maximum(jnp.max(s, -1, keepdims=True), sink)
    p = jnp.exp(s - mx)
    denom = jnp.sum(p, -1, keepdims=True) + jnp.exp(sink - mx)
    o = jnp.einsum('bnhgqk,bnkhd->bnqhgd', p / denom, vv)
    return o.reshape(B, S, ATTN_W)


def spatial_gating(u, v, w_s, b_s, g, bn):
    B, S, _ = u.shape
    NCk = S // SGU_CHUNK
    v = layer_norm(v.reshape(B, S, SGU_GROUPS, HEAD_DIM),
                   g.reshape(SGU_GROUPS, HEAD_DIM), bn.reshape(SGU_GROUPS, HEAD_DIM))
    v = v.reshape(B, NCk, SGU_CHUNK, SGU_GROUPS, HEAD_DIM)
    w = jnp.where(jnp.tril(jnp.ones((SGU_CHUNK, SGU_CHUNK), bool)), w_s, 0.0)
    mixed = jnp.einsum('gts,bcsgd->bctgd', w, v) + b_s.T[None, None, :, :, None]
    return u * mixed.reshape(B, S, SGU_W).astype(u.dtype)


def hybrid_mixer(x, cos, sin, w_in, b_in, conv_w, mlstm_g, sinks, w_s, b_s, sgu_g, sgu_b, w_out):
    B, S, _ = x.shape
    f32 = jnp.float32
    proj = x @ w_in + b_in
    mq, mk, mv, mo, mi, mf, aq, ak, av, su, sv = jnp.split(proj, PROJ_SPLITS, axis=-1)
    qk = jax.nn.silu(causal_conv(jnp.concatenate([mq, mk], -1), conv_w))
    mq, mk = jnp.split(qk, 2, axis=-1)
    bhsd = lambda t: t.reshape(B, S, MLSTM_HEADS, HEAD_DIM).transpose(0, 2, 1, 3).astype(f32)
    h = mlstm_chunkwise(bhsd(mq), bhsd(mk), bhsd(mv),
                        mi.astype(f32).transpose(0, 2, 1),
                        jax.nn.log_sigmoid(mf.astype(f32)).transpose(0, 2, 1))
    h = layer_norm(h.transpose(0, 2, 1, 3), mlstm_g.reshape(MLSTM_HEADS, HEAD_DIM))
    h_a = (h.reshape(B, S, MLSTM_W) * jax.nn.sigmoid(mo.astype(f32))).astype(x.dtype)
    q = partial_rope(aq.reshape(B, S, ATTN_Q_HEADS, HEAD_DIM).astype(f32), cos, sin)
    k = partial_rope(ak.reshape(B, S, ATTN_KV_HEADS, HEAD_DIM).astype(f32), cos, sin)
    vv = av.reshape(B, S, ATTN_KV_HEADS, HEAD_DIM).astype(f32)
    h_b = swa_with_sinks(q, k, vv, sinks).astype(x.dtype)
    h_c = spatial_gating(jax.nn.gelu(su), jax.nn.gelu(sv), w_s, b_s, sgu_g, sgu_b).astype(x.dtype)
    return jnp.concatenate([h_a, h_b, h_c], -1) @ w_out


def swiglu(x, wg, wu, wd):
    return (jax.nn.silu(x @ wg) * (x @ wu)) @ wd


def moe_swiglu(x, w_router, b_router, wg, wu, wd):
    B, S, D = x.shape
    xt = x.reshape(B * S, D)
    logits = (xt @ w_router).astype(jnp.float32) + b_router.astype(jnp.float32)
    top_v, top_i = lax.top_k(logits, TOP_K)
    gates = jax.nn.softmax(top_v, axis=-1)
    combine = jnp.einsum('nk,nke->ne', gates,
                         jax.nn.one_hot(top_i, N_EXPERTS, dtype=jnp.float32)).astype(x.dtype)
    y = jnp.zeros_like(xt)
    for e in range(N_EXPERTS):
        y = y + combine[:, e:e + 1] * swiglu(xt, wg[e], wu[e], wd[e])
    return y.reshape(B, S, D)


def setup_inputs(seed: int = 0) -> dict:
    key = jax.random.key(seed)
    ks = jax.random.split(key, 32)
    nrm = lambda k, shape: jax.random.normal(k, shape, jnp.float32)
    x = nrm(ks[0], (BATCH, SEQ, D_MODEL))
    offsets = jax.random.randint(ks[1], (BATCH, 1), 0, 4096, dtype=jnp.int32)
    positions = offsets + jnp.arange(SEQ, dtype=jnp.int32)[None, :]
    w_in = nrm(ks[2], (DEPTH, D_MODEL, D_PROJ)) * D_MODEL ** -0.5
    f0 = PROJ_SPLITS[4]
    b_in = (0.02 * nrm(ks[3], (DEPTH, D_PROJ))).at[:, f0:f0 + MLSTM_HEADS].add(
        jnp.linspace(3.0, 6.0, MLSTM_HEADS, dtype=jnp.float32))
    conv_w = nrm(ks[4], (DEPTH, CONV_WIDTH, 2 * MLSTM_W)) * CONV_WIDTH ** -0.5
    mlstm_norm_g = 1.0 + 0.1 * nrm(ks[5], (DEPTH, MLSTM_W))
    attn_sinks = 0.5 * nrm(ks[6], (DEPTH, ATTN_Q_HEADS))
    sgu_w_s = nrm(ks[7], (DEPTH, SGU_GROUPS, SGU_CHUNK, SGU_CHUNK)) * SGU_CHUNK ** -0.5
    sgu_b_s = 1.0 + 0.1 * nrm(ks[8], (DEPTH, SGU_GROUPS, SGU_CHUNK))
    sgu_norm_g = 1.0 + 0.1 * nrm(ks[9], (DEPTH, SGU_W))
    sgu_norm_b = 0.02 * nrm(ks[10], (DEPTH, SGU_W))
    w_out = nrm(ks[11], (DEPTH, D_MIX, D_MODEL)) * D_MIX ** -0.5 * DN_BETA
    ln1_g = 1.0 + 0.1 * nrm(ks[12], (DEPTH, D_MODEL))
    ln1_b = 0.02 * nrm(ks[13], (DEPTH, D_MODEL))
    ln2_g = 1.0 + 0.1 * nrm(ks[14], (DEPTH, D_MODEL))
    ln2_b = 0.02 * nrm(ks[15], (DEPTH, D_MODEL))
    ffn_w_gate = nrm(ks[16], (N_DENSE, D_MODEL, D_FF)) * D_MODEL ** -0.5 * DN_BETA
    ffn_w_up = nrm(ks[17], (N_DENSE, D_MODEL, D_FF)) * D_MODEL ** -0.5 * DN_BETA
    ffn_w_down = nrm(ks[18], (N_DENSE, D_FF, D_MODEL)) * D_FF ** -0.5 * DN_BETA
    moe_w_router = nrm(ks[19], (N_MOE, D_MODEL, N_EXPERTS)) * D_MODEL ** -0.5
    moe_b_router = 0.01 * nrm(ks[20], (N_MOE, N_EXPERTS))
    moe_w_gate = nrm(ks[21], (N_MOE, N_EXPERTS, D_MODEL, D_FF_EXPERT)) * D_MODEL ** -0.5 * DN_BETA
    moe_w_up = nrm(ks[22], (N_MOE, N_EXPERTS, D_MODEL, D_FF_EXPERT)) * D_MODEL ** -0.5 * DN_BETA
    moe_w_down = nrm(ks[23], (N_MOE, N_EXPERTS, D_FF_EXPERT, D_MODEL)) * D_FF_EXPERT ** -0.5 * DN_BETA
    return {'x': x, 'positions': positions, 'w_in': w_in, 'b_in': b_in, 'conv_w': conv_w,
            'mlstm_norm_g': mlstm_norm_g, 'attn_sinks': attn_sinks, 'sgu_w_s': sgu_w_s,
            'sgu_b_s': sgu_b_s, 'sgu_norm_g': sgu_norm_g, 'sgu_norm_b': sgu_norm_b,
            'w_out': w_out, 'ln1_g': ln1_g, 'ln1_b': ln1_b, 'ln2_g': ln2_g, 'ln2_b': ln2_b,
            'ffn_w_gate': ffn_w_gate, 'ffn_w_up': ffn_w_up, 'ffn_w_down': ffn_w_down,
            'moe_w_router': moe_w_router, 'moe_b_router': moe_b_router,
            'moe_w_gate': moe_w_gate, 'moe_w_up': moe_w_up, 'moe_w_down': moe_w_down}


def reference(x, positions, w_in, b_in, conv_w, mlstm_norm_g, attn_sinks, sgu_w_s, sgu_b_s,
              sgu_norm_g, sgu_norm_b, w_out, ln1_g, ln1_b, ln2_g, ln2_b,
              ffn_w_gate, ffn_w_up, ffn_w_down, moe_w_router, moe_b_router,
              moe_w_gate, moe_w_up, moe_w_down):
    inv_freq = ROPE_THETA ** (-jnp.arange(0, ROPE_DIM, 2, dtype=jnp.float32) / ROPE_DIM)
    ang = positions.astype(jnp.float32)[..., None] * inv_freq
    cos = jnp.cos(ang)[:, :, None, :]
    sin = jnp.sin(ang)[:, :, None, :]
    for layer in range(DEPTH):
        mix = hybrid_mixer(x, cos, sin, w_in[layer], b_in[layer], conv_w[layer],
                           mlstm_norm_g[layer], attn_sinks[layer], sgu_w_s[layer],
                           sgu_b_s[layer], sgu_norm_g[layer], sgu_norm_b[layer], w_out[layer])
        x = layer_norm(DN_ALPHA * x + mix, ln1_g[layer], ln1_b[layer])
        j = layer // 2
        if layer % 2 == 0:
            ff = swiglu(x, ffn_w_gate[j], ffn_w_up[j], ffn_w_down[j])
        else:
            ff = moe_swiglu(x, moe_w_router[j], moe_b_router[j], moe_w_gate[j],
                            moe_w_up[j], moe_w_down[j])
        x = layer_norm(DN_ALPHA * x + ff, ln2_g[layer], ln2_b[layer])
    return x
```

```python
import functools

import jax
import jax.numpy as jnp
import numpy as np
from jax import lax
from jax.experimental import pallas as pl
from jax.experimental.pallas import tpu as pltpu

F32 = jnp.float32
BF16 = jnp.bfloat16

D_MODEL = 1024
HEAD_DIM = 64
LANES = 128
MLSTM_HEADS = 6
ATTN_Q_HEADS = 6
ATTN_KV_HEADS = 2
SGU_GROUPS = 4
MLSTM_W = MLSTM_HEADS * HEAD_DIM
ATTN_W = ATTN_Q_HEADS * HEAD_DIM
ATTN_KV_W = ATTN_KV_HEADS * HEAD_DIM
SGU_W = SGU_GROUPS * HEAD_DIM
CHUNK = 128
CONV_WIDTH = 4
ROPE_DIM = HEAD_DIM // 4
ROPE_THETA = 500000.0
N_EXPERTS = 8
TOP_K = 2
DEPTH = 2
DN_ALPHA = (2.0 * DEPTH) ** 0.25
LN_EPS = 1e-5

QK_OFF, V_OFF, O_OFF, AQ_OFF = 0, 768, 1152, 1536
GATE_OFF, AK_OFF, AV_OFF, SU_OFF, SV_OFF = 1920, 2048, 2176, 2304, 2560
D_PROJ_PAD = 2816
ATTN_HEAD_ORDER = (0, 3, 1, 4, 2, 5)

VMEM_LIMIT = 56 * 1024 * 1024

ROW_TILE = 512
FF_TILE = 256
MOE_SUB = 256
MOE_SUPER = 2048
MOE_FF_TILE = 512


def _params(*sem):
    return pltpu.CompilerParams(dimension_semantics=sem, vmem_limit_bytes=VMEM_LIMIT)


def _lane_lo(shape):
    return lax.broadcasted_iota(jnp.int32, shape, len(shape) - 1) < HEAD_DIM


def _layer_norm_rows(z, g, b):
    mu = jnp.mean(z, axis=-1, keepdims=True)
    zc = z - mu
    var = jnp.mean(zc * zc, axis=-1, keepdims=True)
    return zc * lax.rsqrt(var + LN_EPS) * g + b


def _half_layer_norm(x, lo):
    inv = 1.0 / HEAD_DIM
    s_lo = jnp.sum(jnp.where(lo, x, 0.0), axis=-1, keepdims=True)
    s_all = jnp.sum(x, axis=-1, keepdims=True)
    mu = jnp.where(lo, s_lo, s_all - s_lo) * inv
    xc = x - mu
    sq = xc * xc
    q_lo = jnp.sum(jnp.where(lo, sq, 0.0), axis=-1, keepdims=True)
    q_all = jnp.sum(sq, axis=-1, keepdims=True)
    var = jnp.where(lo, q_lo, q_all - q_lo) * inv
    return xc * lax.rsqrt(var + LN_EPS)


def _split3(a):
    h1 = a.astype(BF16)
    r1 = a - h1.astype(F32)
    h2 = r1.astype(BF16)
    r2 = r1 - h2.astype(F32)
    return h1, h2, r2.astype(BF16)


def _dot(a, b):
    return jnp.dot(a, b, preferred_element_type=F32)


def _inproj_kernel(x_ref, w_ref, b_ref, o_ref):
    o_ref[...] = _dot(x_ref[...].astype(BF16), w_ref[...]) + b_ref[...]


def _inproj(x2d, w, b):
    n = x2d.shape[0]
    return pl.pallas_call(
        _inproj_kernel,
        grid=(n // ROW_TILE,),
        in_specs=[pl.BlockSpec((ROW_TILE, D_MODEL), lambda i: (i, 0)),
                  pl.BlockSpec((D_MODEL, D_PROJ_PAD), lambda i: (0, 0)),
                  pl.BlockSpec((1, D_PROJ_PAD), lambda i: (0, 0))],
        out_specs=pl.BlockSpec((ROW_TILE, D_PROJ_PAD), lambda i: (i, 0)),
        out_shape=jax.ShapeDtypeStruct((n, D_PROJ_PAD), F32),
        compiler_params=_params("parallel"),
        name="inproj",
    )(x2d, w, b)


def _mlstm_kernel(qk_ref, v_ref, o_ref, gate_ref, cw_ref, g_ref, out_ref, buf_ref, ct_ref, m_ref):
    c = pl.program_id(1)

    @pl.when(c == 0)
    def _():
        buf_ref[0:8, :] = jnp.zeros((8, 2 * MLSTM_W), F32)
        ct_ref[...] = jnp.zeros_like(ct_ref)
        m_ref[...] = jnp.zeros_like(m_ref)

    buf_ref[8:8 + CHUNK, :] = qk_ref[...]
    cw = cw_ref[...]
    base = 8 - (CONV_WIDTH - 1)
    acc = cw[0:1, :] * buf_ref[base:base + CHUNK, :]
    for j in range(1, CONV_WIDTH):
        acc = acc + cw[j:j + 1, :] * buf_ref[base + j:base + j + CHUNK, :]
    buf_ref[0:8, :] = qk_ref[CHUNK - 8:CHUNK, :]
    qk = jax.nn.silu(acc)

    gates = gate_ref[...]
    lf = jax.nn.log_sigmoid(gates)
    row = lax.broadcasted_iota(jnp.int32, (CHUNK, CHUNK), 0)
    col = lax.broadcasted_iota(jnp.int32, (CHUNK, CHUNK), 1)
    causal = col <= row
    tri = causal.astype(BF16)
    l1, l2, l3 = _split3(lf)
    cum = _dot(tri, l1) + _dot(tri, l2) + _dot(tri, l3)
    gates_t = gates.T
    cum_t = cum.T

    lo = _lane_lo((CHUNK, LANES))
    for p in range(MLSTM_HEADS // 2):
        sl = slice(p * LANES, (p + 1) * LANES)
        q_slab = qk[:, sl] * (HEAD_DIM ** -0.5)
        k_slab = qk[:, MLSTM_W + p * LANES:MLSTM_W + (p + 1) * LANES]
        kt_slab = k_slab.T
        v_slab = v_ref[:, sl]
        ct_pair = ct_ref[p]
        ct_pair_b = ct_pair.astype(BF16)
        halves = []
        new_ct = []
        for half in range(2):
            h = 2 * p + half
            sel = lo if half == 0 else jnp.logical_not(lo)
            li_row = gates_t[h:h + 1, :]
            bc_row = cum_t[MLSTM_HEADS + h:MLSTM_HEADS + h + 1, :]
            bc_col = cum[:, MLSTM_HEADS + h:MLSTM_HEADS + h + 1]
            b_tot = bc_row[:, CHUNK - 1:CHUNK]
            m_prev = m_ref[h:h + 1, 0:1]

            dmat = jnp.where(causal, bc_col - bc_row + li_row, -jnp.inf)
            m_inter = bc_col + m_prev
            m_row = jnp.maximum(m_inter, jnp.max(dmat, axis=-1, keepdims=True))
            q_m = jnp.where(sel, q_slab, 0.0).astype(BF16)
            s = _dot(q_m, kt_slab.astype(BF16))
            pmat = (s * jnp.exp(dmat - m_row)).astype(BF16)
            w_inter = jnp.exp(m_inter - m_row)
            v_aug = jnp.where(sel, v_slab, 1.0).astype(BF16)
            r = _dot(pmat, v_aug) + w_inter * _dot(q_m, ct_pair_b)
            den_lane = HEAD_DIM if half == 0 else 0
            den = r[:, den_lane:den_lane + 1]
            halves.append(r / jnp.maximum(jnp.abs(den), jnp.exp(-m_row)))

            a_row = b_tot - bc_row + li_row
            m_loc = jnp.max(a_row, axis=-1, keepdims=True)
            w_row = jnp.exp(a_row - m_loc)
            kt_h = kt_slab[half * HEAD_DIM:(half + 1) * HEAD_DIM, :]
            ct_loc = _dot((kt_h * w_row).astype(BF16), v_aug)
            m_new = jnp.maximum(b_tot + m_prev, m_loc)
            s_old = jnp.exp(b_tot + m_prev - m_new)
            s_loc = jnp.exp(m_loc - m_new)
            new_ct.append(s_old * ct_pair[half * HEAD_DIM:(half + 1) * HEAD_DIM, :] + s_loc * ct_loc)
            m_ref[h:h + 1, :] = jnp.broadcast_to(m_new, (1, LANES))

        ct_ref[p] = jnp.concatenate(new_ct, axis=0)
        hh = jnp.where(lo, halves[0], halves[1])
        hn = _half_layer_norm(hh, lo) * g_ref[:, sl]
        out_ref[:, sl] = (hn * jax.nn.sigmoid(o_ref[:, sl])).astype(out_ref.dtype)


def _mlstm(proj3, conv_w, norm_g):
    b, s, _ = proj3.shape
    blk = lambda w, off: pl.BlockSpec((None, CHUNK, w), lambda i, c: (i, c, off // w))
    return pl.pallas_call(
        _mlstm_kernel,
        grid=(b, s // CHUNK),
        in_specs=[blk(2 * MLSTM_W, QK_OFF), blk(MLSTM_W, V_OFF), blk(MLSTM_W, O_OFF), blk(LANES, GATE_OFF),
                  pl.BlockSpec((CONV_WIDTH, 2 * MLSTM_W), lambda i, c: (0, 0)),
                  pl.BlockSpec((1, MLSTM_W), lambda i, c: (0, 0))],
        out_specs=pl.BlockSpec((None, CHUNK, MLSTM_W), lambda i, c: (i, c, 0)),
        out_shape=jax.ShapeDtypeStruct((b, s, MLSTM_W), BF16),
        scratch_shapes=[pltpu.VMEM((8 + CHUNK, 2 * MLSTM_W), F32),
                        pltpu.VMEM((MLSTM_HEADS // 2, LANES, LANES), F32),
                        pltpu.VMEM((8, LANES), F32)],
        compiler_params=_params("parallel", "arbitrary"),
        name="mlstm",
    )(proj3, proj3, proj3, proj3, conv_w, norm_g)


def _rope(x, cos_t, sin_t, first):
    return x * cos_t + jnp.where(first, pltpu.roll(x, LANES - ROPE_DIM // 2, 1),
                                 pltpu.roll(x, ROPE_DIM // 2, 1)) * sin_t


def _swa_kernel(sink_ref, q_ref, k_ref, v_ref, cos_ref, sin_ref, out_ref, kt_ref, vv_ref):
    c = pl.program_id(1)

    @pl.when(c == 0)
    def _():
        kt_ref[...] = jnp.zeros_like(kt_ref)
        vv_ref[...] = jnp.zeros_like(vv_ref)

    cos_t = cos_ref[...]
    sin_t = sin_ref[...]
    lane = lax.broadcasted_iota(jnp.int32, (CHUNK, LANES), 1)
    lo = lane < HEAD_DIM
    first = (lane % HEAD_DIM) < ROPE_DIM // 2

    k_cur = _rope(k_ref[...], cos_t, sin_t, first)
    kt_ref[:, CHUNK:2 * CHUNK] = k_cur.T.astype(BF16)
    vv_ref[CHUNK:2 * CHUNK, :] = v_ref[...].astype(BF16)
    kt = kt_ref[...]
    vv = vv_ref[...]

    row = lax.broadcasted_iota(jnp.int32, (CHUNK, 2 * CHUNK), 0)
    col = lax.broadcasted_iota(jnp.int32, (CHUNK, 2 * CHUNK), 1)
    first_block_shift = jnp.where(c > 0, 0, 2 * CHUNK)
    visible = jnp.logical_or(jnp.logical_and(col < CHUNK, col > row + first_block_shift),
                             jnp.logical_and(col >= CHUNK, col - CHUNK <= row))

    for j in range(ATTN_Q_HEADS // 2):
        sl = slice(j * LANES, (j + 1) * LANES)
        q_slab = _rope(q_ref[:, sl], cos_t, sin_t, first) * (HEAD_DIM ** -0.5)
        outs = []
        for half in range(2):
            sel = lo if half == 0 else jnp.logical_not(lo)
            sink = sink_ref[ATTN_HEAD_ORDER[2 * j + half]]
            q_m = jnp.where(sel, q_slab, 0.0).astype(BF16)
            s = jnp.where(visible, _dot(q_m, kt), -jnp.inf)
            mx = jnp.maximum(jnp.max(s, axis=-1, keepdims=True), sink)
            pexp = jnp.exp(s - mx)
            denom = jnp.sum(pexp, axis=-1, keepdims=True) + jnp.exp(sink - mx)
            outs.append(_dot(pexp.astype(BF16), vv) / denom)
        out_ref[:, sl] = jnp.where(lo, outs[0], outs[1]).astype(out_ref.dtype)

    kt_ref[:, 0:CHUNK] = kt_ref[:, CHUNK:2 * CHUNK]
    vv_ref[0:CHUNK, :] = vv_ref[CHUNK:2 * CHUNK, :]


def _swa(proj3, cos_t, sin_t, sinks):
    b, s, _ = proj3.shape
    blk = lambda w, off: pl.BlockSpec((None, CHUNK, w), lambda i, c: (i, c, off // w))
    tab = pl.BlockSpec((None, CHUNK, LANES), lambda i, c: (i, c, 0))
    return pl.pallas_call(
        _swa_kernel,
        grid=(b, s // CHUNK),
        in_specs=[pl.BlockSpec(memory_space=pltpu.SMEM),
                  blk(ATTN_W, AQ_OFF), blk(ATTN_KV_W, AK_OFF), blk(ATTN_KV_W, AV_OFF), tab, tab],
        out_specs=pl.BlockSpec((None, CHUNK, ATTN_W), lambda i, c: (i, c, 0)),
        out_shape=jax.ShapeDtypeStruct((b, s, ATTN_W), BF16),
        scratch_shapes=[pltpu.VMEM((LANES, 2 * CHUNK), BF16), pltpu.VMEM((2 * CHUNK, LANES), BF16)],
        compiler_params=_params("parallel", "arbitrary"),
        name="swa",
    )(sinks, proj3, proj3, proj3, cos_t, sin_t)


def _sgu_kernel(u_ref, v_ref, w_ref, bias_ref, g_ref, b_ref, out_ref):
    lo = _lane_lo((CHUNK, LANES))
    for j in range(SGU_GROUPS // 2):
        sl = slice(j * LANES, (j + 1) * LANES)
        u = jax.nn.gelu(u_ref[:, sl])
        v = jax.nn.gelu(v_ref[:, sl])
        vn = (_half_layer_norm(v, lo) * g_ref[:, sl] + b_ref[:, sl]).astype(BF16)
        mixed = jnp.where(lo, _dot(w_ref[2 * j], vn), _dot(w_ref[2 * j + 1], vn)) + bias_ref[:, sl]
        out_ref[:, sl] = (u * mixed).astype(out_ref.dtype)


def _sgu(proj3, w_tril, bias_tok, norm_g, norm_b):
    b, s, _ = proj3.shape
    blk = lambda w, off: pl.BlockSpec((None, CHUNK, w), lambda i, c: (i, c, off // w))
    const2 = lambda shape: pl.BlockSpec(shape, lambda i, c: (0,) * len(shape))
    return pl.pallas_call(
        _sgu_kernel,
        grid=(b, s // CHUNK),
        in_specs=[blk(SGU_W, SU_OFF), blk(SGU_W, SV_OFF), const2((SGU_GROUPS, CHUNK, CHUNK)),
                  const2((CHUNK, SGU_W)), const2((1, SGU_W)), const2((1, SGU_W))],
        out_specs=pl.BlockSpec((None, CHUNK, SGU_W), lambda i, c: (i, c, 0)),
        out_shape=jax.ShapeDtypeStruct((b, s, SGU_W), BF16),
        compiler_params=_params("parallel", "parallel"),
        name="sgu",
    )(proj3, proj3, w_tril, bias_tok, norm_g, norm_b)


def _outproj_kernel(ha_ref, hb_ref, hc_ref, x_ref, wa_ref, wb_ref, wc_ref, g_ref, b_ref, o_ref):
    mix = _dot(ha_ref[...], wa_ref[...]) + _dot(hb_ref[...], wb_ref[...]) + _dot(hc_ref[...], wc_ref[...])
    o_ref[...] = _layer_norm_rows(DN_ALPHA * x_ref[...] + mix, g_ref[...], b_ref[...])


def _outproj_ln(ha, hb, hc, x2d, wa, wb, wc, g, b):
    n = x2d.shape[0]
    rows = lambda w: pl.BlockSpec((ROW_TILE, w), lambda i: (i, 0))
    const = lambda shape: pl.BlockSpec(shape, lambda i: (0, 0))
    return pl.pallas_call(
        _outproj_kernel,
        grid=(n // ROW_TILE,),
        in_specs=[rows(MLSTM_W), rows(ATTN_W), rows(SGU_W), rows(D_MODEL),
                  const((MLSTM_W, D_MODEL)), const((ATTN_W, D_MODEL)), const((SGU_W, D_MODEL)),
                  const((1, D_MODEL)), const((1, D_MODEL))],
        out_specs=rows(D_MODEL),
        out_shape=jax.ShapeDtypeStruct((n, D_MODEL), F32),
        compiler_params=_params("parallel"),
        name="outproj_ln",
    )(ha, hb, hc, x2d, wa, wb, wc, g, b)


def _ffn_kernel(x_ref, wg_ref, wu_ref, wd_ref, g_ref, b_ref, o_ref, acc_ref):
    j = pl.program_id(1)
    xb = x_ref[...].astype(BF16)
    hidden = (jax.nn.silu(_dot(xb, wg_ref[...])) * _dot(xb, wu_ref[...])).astype(BF16)
    part = _dot(hidden, wd_ref[...])

    @pl.when(j == 0)
    def _():
        acc_ref[...] = part

    @pl.when(j > 0)
    def _():
        acc_ref[...] += part

    @pl.when(j == pl.num_programs(1) - 1)
    def _():
        o_ref[...] = _layer_norm_rows(DN_ALPHA * x_ref[...] + acc_ref[...], g_ref[...], b_ref[...])


def _ffn_ln(x2d, wg, wu, wd, g, b):
    n = x2d.shape[0]
    d_ff = wg.shape[1]
    tm = 2 * ROW_TILE
    return pl.pallas_call(
        _ffn_kernel,
        grid=(n // tm, d_ff // FF_TILE),
        in_specs=[pl.BlockSpec((tm, D_MODEL), lambda i, j: (i, 0)),
                  pl.BlockSpec((D_MODEL, FF_TILE), lambda i, j: (0, j)),
                  pl.BlockSpec((D_MODEL, FF_TILE), lambda i, j: (0, j)),
                  pl.BlockSpec((FF_TILE, D_MODEL), lambda i, j: (j, 0)),
                  pl.BlockSpec((1, D_MODEL), lambda i, j: (0, 0)),
                  pl.BlockSpec((1, D_MODEL), lambda i, j: (0, 0))],
        out_specs=pl.BlockSpec((tm, D_MODEL), lambda i, j: (i, 0)),
        out_shape=jax.ShapeDtypeStruct((n, D_MODEL), F32),
        scratch_shapes=[pltpu.VMEM((tm, D_MODEL), F32)],
        compiler_params=_params("parallel", "arbitrary"),
        name="ffn_ln",
    )(x2d, wg, wu, wd, g, b)


def _router_kernel(x_ref, wh_ref, wl_ref, b_ref, o_ref):
    x = x_ref[...]
    xh = x.astype(BF16)
    xl = (x - xh.astype(F32)).astype(BF16)
    logits = _dot(xh, wh_ref[...]) + _dot(xl, wh_ref[...]) + _dot(xh, wl_ref[...]) + b_ref[...]
    lane = lax.broadcasted_iota(jnp.int32, logits.shape, 1)
    logits = jnp.where(lane < N_EXPERTS, logits, -jnp.inf)
    m1 = jnp.max(logits, axis=-1, keepdims=True)
    i1 = jnp.min(jnp.where(logits == m1, lane, LANES), axis=-1, keepdims=True)
    rest = jnp.where(lane == i1, -jnp.inf, logits)
    m2 = jnp.max(rest, axis=-1, keepdims=True)
    i2 = jnp.min(jnp.where(rest == m2, lane, LANES), axis=-1, keepdims=True)
    e2 = jnp.exp(m2 - m1)
    g1 = 1.0 / (1.0 + e2)
    g2 = e2 / (1.0 + e2)
    o_ref[...] = jnp.where(lane == 0, g1, jnp.where(lane == 1, g2, jnp.where(
        lane == 2, i1.astype(F32), jnp.where(lane == 3, i2.astype(F32), 0.0))))


def _router(x2d, w_hi, w_lo, b):
    n = x2d.shape[0]
    const = lambda shape: pl.BlockSpec(shape, lambda i: (0, 0))
    return pl.pallas_call(
        _router_kernel,
        grid=(n // ROW_TILE,),
        in_specs=[pl.BlockSpec((ROW_TILE, D_MODEL), lambda i: (i, 0)),
                  const((D_MODEL, LANES)), const((D_MODEL, LANES)), const((1, LANES))],
        out_specs=pl.BlockSpec((ROW_TILE, LANES), lambda i: (i, 0)),
        out_shape=jax.ShapeDtypeStruct((n, LANES), F32),
        compiler_params=_params("parallel"),
        name="router",
    )(x2d, w_hi, w_lo, b)


def _gather_kernel(src_ref, x_hbm, o_ref, buf_ref, sem):
    t = pl.program_id(0)

    def row_copy(r):
        tok = src_ref[t * MOE_SUB + r]
        return pltpu.make_async_copy(x_hbm.at[pl.ds(tok, 1), :], buf_ref.at[pl.ds(r, 1), :], sem)

    def issue(r, carry):
        row_copy(r).start()
        return carry

    def drain(r, carry):
        row_copy(r).wait()
        return carry

    lax.fori_loop(0, MOE_SUB, issue, 0)
    lax.fori_loop(0, MOE_SUB, drain, 0)
    o_ref[...] = buf_ref[...].astype(o_ref.dtype)


def _gather_rows(x2d, src_tok, n_rows):
    return pl.pallas_call(
        _gather_kernel,
        grid_spec=pltpu.PrefetchScalarGridSpec(
            num_scalar_prefetch=1,
            grid=(n_rows // MOE_SUB,),
            in_specs=[pl.BlockSpec(memory_space=pl.ANY)],
            out_specs=pl.BlockSpec((MOE_SUB, D_MODEL), lambda t, src: (t, 0)),
            scratch_shapes=[pltpu.VMEM((MOE_SUB, D_MODEL), F32), pltpu.SemaphoreType.DMA(())]),
        out_shape=jax.ShapeDtypeStruct((n_rows, D_MODEL), BF16),
        compiler_params=_params("arbitrary"),
        name="moe_gather",
    )(src_tok, x2d)


def _moe_kernel(tile_ref, exp_ref, lo_ref, hi_ref, first_ref,
                x_ref, wg_ref, wu_ref, wd_ref, o_ref, wgb_ref, wub_ref, wdb_ref):
    v = pl.program_id(0)
    j = pl.program_id(1)

    @pl.when(jnp.logical_and(first_ref[v] == 1, j == 0))
    def _():
        o_ref[...] = jnp.zeros_like(o_ref)

    lo = lo_ref[v]
    hi = hi_ref[v]

    @pl.when(hi > lo)
    def _():
        wgb_ref[...] = wg_ref[...].astype(BF16)
        wub_ref[...] = wu_ref[...].astype(BF16)
        wdb_ref[...] = wd_ref[...].astype(BF16)

        def body(m, carry):
            rows = pl.ds(pl.multiple_of(m * MOE_SUB, MOE_SUB), MOE_SUB)
            xs = x_ref[rows, :]
            hidden = (jax.nn.silu(_dot(xs, wgb_ref[...])) * _dot(xs, wub_ref[...])).astype(BF16)
            o_ref[rows, :] += _dot(hidden, wdb_ref[...])
            return carry

        lax.fori_loop(lo, hi, body, 0)


def _moe_grouped(xs, wg, wu, wd, visit_tile, visit_exp, visit_lo, visit_hi, visit_first):
    n_rows = xs.shape[0]
    n_visits = visit_tile.shape[0]
    d_ff = wg.shape[2]
    return pl.pallas_call(
        _moe_kernel,
        grid_spec=pltpu.PrefetchScalarGridSpec(
            num_scalar_prefetch=5,
            grid=(n_visits, d_ff // MOE_FF_TILE),
            in_specs=[pl.BlockSpec((MOE_SUPER, D_MODEL), lambda v, j, t, e, lo, hi, f: (t[v], 0)),
                      pl.BlockSpec((None, D_MODEL, MOE_FF_TILE), lambda v, j, t, e, lo, hi, f: (e[v], 0, j)),
                      pl.BlockSpec((None, D_MODEL, MOE_FF_TILE), lambda v, j, t, e, lo, hi, f: (e[v], 0, j)),
                      pl.BlockSpec((None, MOE_FF_TILE, D_MODEL), lambda v, j, t, e, lo, hi, f: (e[v], j, 0))],
            out_specs=pl.BlockSpec((MOE_SUPER, D_MODEL), lambda v, j, t, e, lo, hi, f: (t[v], 0)),
            scratch_shapes=[pltpu.VMEM((D_MODEL, MOE_FF_TILE), BF16), pltpu.VMEM((D_MODEL, MOE_FF_TILE), BF16),
                            pltpu.VMEM((MOE_FF_TILE, D_MODEL), BF16)]),
        out_shape=jax.ShapeDtypeStruct((n_rows, D_MODEL), F32),
        compiler_params=_params("arbitrary", "arbitrary"),
        name="moe_grouped",
    )(visit_tile, visit_exp, visit_lo, visit_hi, visit_first, xs, wg, wu, wd)


def _combine_kernel(slot_ref, x_ref, gate_ref, y_hbm, g_ref, b_ref, o_ref, buf_ref, sem):
    t = pl.program_id(0)

    def row_copy(r, k):
        slot = slot_ref[(t * MOE_SUB + r) * TOP_K + k]
        return pltpu.make_async_copy(y_hbm.at[pl.ds(slot, 1), :], buf_ref.at[k, pl.ds(r, 1), :], sem)

    def issue(r, carry):
        row_copy(r, 0).start()
        row_copy(r, 1).start()
        return carry

    def drain(r, carry):
        row_copy(r, 0).wait()
        row_copy(r, 1).wait()
        return carry

    lax.fori_loop(0, MOE_SUB, issue, 0)
    lax.fori_loop(0, MOE_SUB, drain, 0)
    gate = gate_ref[...]
    ff = gate[:, 0:1] * buf_ref[0] + gate[:, 1:2] * buf_ref[1]
    o_ref[...] = _layer_norm_rows(DN_ALPHA * x_ref[...] + ff, g_ref[...], b_ref[...])


def _combine_ln(x2d, route, slots, ys, g, b):
    n = x2d.shape[0]
    return pl.pallas_call(
        _combine_kernel,
        grid_spec=pltpu.PrefetchScalarGridSpec(
            num_scalar_prefetch=1,
            grid=(n // MOE_SUB,),
            in_specs=[pl.BlockSpec((MOE_SUB, D_MODEL), lambda t, s: (t, 0)),
                      pl.BlockSpec((MOE_SUB, LANES), lambda t, s: (t, 0)),
                      pl.BlockSpec(memory_space=pl.ANY),
                      pl.BlockSpec((1, D_MODEL), lambda t, s: (0, 0)),
                      pl.BlockSpec((1, D_MODEL), lambda t, s: (0, 0))],
            out_specs=pl.BlockSpec((MOE_SUB, D_MODEL), lambda t, s: (t, 0)),
            scratch_shapes=[pltpu.VMEM((TOP_K, MOE_SUB, D_MODEL), F32), pltpu.SemaphoreType.DMA(())]),
        out_shape=jax.ShapeDtypeStruct((n, D_MODEL), F32),
        compiler_params=_params("arbitrary"),
        name="moe_combine_ln",
    )(slots, x2d, route, ys, g, b)


def _routing_tables(route, n):
    idx = route[:, 2:4].astype(jnp.int32)
    onehot = (idx[:, :, None] == jnp.arange(N_EXPERTS, dtype=jnp.int32)).astype(jnp.int32).sum(1)
    rank = jnp.cumsum(onehot, axis=0) - onehot
    counts = jnp.sum(onehot, axis=0)
    padded = ((counts + MOE_SUB - 1) // MOE_SUB) * MOE_SUB
    ends = jnp.cumsum(padded)
    starts = ends - padded
    slot = starts[idx] + jnp.take_along_axis(rank, idx, axis=1)

    n_rows = ((n * TOP_K + N_EXPERTS * (MOE_SUB - 1)) // MOE_SUPER + 1) * MOE_SUPER
    tok = jnp.broadcast_to(jnp.arange(n, dtype=jnp.int32)[:, None], (n, TOP_K))
    src_tok = jnp.zeros((n_rows,), jnp.int32).at[slot.reshape(-1)].set(tok.reshape(-1))

    per = MOE_SUPER // MOE_SUB
    n_super = n_rows // MOE_SUPER
    tile0 = jnp.arange(n_super, dtype=jnp.int32)[:, None] * per
    lo = jnp.clip(starts[None, :] // MOE_SUB - tile0, 0, per)
    hi = jnp.clip(ends[None, :] // MOE_SUB - tile0, 0, per)
    valid = (hi > lo).reshape(-1)
    n_visits = n_super + N_EXPERTS - 1
    order = jnp.argsort(jnp.logical_not(valid), stable=True)[:n_visits]
    n_valid = jnp.sum(valid.astype(jnp.int32))
    live = jnp.arange(n_visits) < n_valid
    last = order[jnp.maximum(n_valid - 1, 0)]
    order = jnp.where(live, order, last)
    v_tile = (order // N_EXPERTS).astype(jnp.int32)
    v_exp = (order % N_EXPERTS).astype(jnp.int32)
    v_lo = jnp.where(live, lo.reshape(-1)[order], 0).astype(jnp.int32)
    v_hi = jnp.where(live, hi.reshape(-1)[order], 0).astype(jnp.int32)
    prev_tile = jnp.concatenate([jnp.full((1,), -1, jnp.int32), v_tile[:-1]])
    v_first = jnp.logical_and(live, v_tile != prev_tile).astype(jnp.int32)
    return slot.reshape(-1).astype(jnp.int32), src_tok, n_rows, (v_tile, v_exp, v_lo, v_hi, v_first)


def _moe_ln(x2d, w_router, b_router, wg, wu, wd, g, b):
    n = x2d.shape[0]
    w_pad = jnp.zeros((D_MODEL, LANES), F32).at[:, :N_EXPERTS].set(w_router)
    w_hi = w_pad.astype(BF16)
    w_lo = (w_pad - w_hi.astype(F32)).astype(BF16)
    b_pad = jnp.zeros((1, LANES), F32).at[0, :N_EXPERTS].set(b_router)
    route = _router(x2d, w_hi, w_lo, b_pad)
    slots, src_tok, n_rows, visits = _routing_tables(route, n)
    xs = _gather_rows(x2d, src_tok, n_rows)
    ys = _moe_grouped(xs, wg, wu, wd, *visits)
    return _combine_ln(x2d, route, slots, ys, g, b)


def _layout_in_proj(w_in, b_in):
    sizes = (MLSTM_W, MLSTM_W, MLSTM_W, MLSTM_W, MLSTM_HEADS, MLSTM_HEADS,
             ATTN_W, ATTN_KV_W, ATTN_KV_W, SGU_W, SGU_W)
    offs = np.concatenate([[0], np.cumsum(sizes)])
    seg = lambda a, i: a[..., int(offs[i]):int(offs[i + 1])]

    def build(a):
        lead = a.shape[:-1]
        aq = seg(a, 6).reshape(lead + (ATTN_Q_HEADS, HEAD_DIM))[..., np.array(ATTN_HEAD_ORDER), :]
        gates = jnp.concatenate([seg(a, 4), seg(a, 5),
                                 jnp.zeros(lead + (LANES - 2 * MLSTM_HEADS,), a.dtype)], -1)
        return jnp.concatenate([seg(a, 0), seg(a, 1), seg(a, 2), seg(a, 3), aq.reshape(lead + (ATTN_W,)),
                                gates, seg(a, 7), seg(a, 8), seg(a, 9), seg(a, 10)], -1)

    return build(w_in).astype(BF16), build(b_in)[None, :]


def _rope_tables(positions):
    inv_freq = ROPE_THETA ** (-jnp.arange(0, ROPE_DIM, 2, dtype=F32) / ROPE_DIM)
    ang = positions.astype(F32)[..., None] * inv_freq
    cos, sin = jnp.cos(ang), jnp.sin(ang)
    half = ROPE_DIM // 2
    ones = jnp.ones(ang.shape[:-1] + (HEAD_DIM - ROPE_DIM,), F32)
    cos_head = jnp.concatenate([cos, cos, ones], -1)
    sin_head = jnp.concatenate([-sin, sin, 0.0 * ones], -1)
    del half
    return jnp.concatenate([cos_head, cos_head], -1), jnp.concatenate([sin_head, sin_head], -1)


def kernel(x, positions, w_in, b_in, conv_w, mlstm_norm_g, attn_sinks, sgu_w_s, sgu_b_s, sgu_norm_g, sgu_norm_b, w_out, ln1_g, ln1_b, ln2_g, ln2_b, ffn_w_gate, ffn_w_up, ffn_w_down, moe_w_router, moe_b_router, moe_w_gate, moe_w_up, moe_w_down):
    bsz, seq, _ = x.shape
    n = bsz * seq
    cos_t, sin_t = _rope_tables(positions)
    tril = jnp.tril(jnp.ones((CHUNK, CHUNK), bool))
    x2d = x.reshape(n, D_MODEL)
    for layer in range(DEPTH):
        w_p, b_p = _layout_in_proj(w_in[layer], b_in[layer])
        proj3 = _inproj(x2d, w_p, b_p).reshape(bsz, seq, D_PROJ_PAD)
        h_a = _mlstm(proj3, conv_w[layer], mlstm_norm_g[layer][None, :])
        h_b = _swa(proj3, cos_t, sin_t, attn_sinks[layer])
        w_tril = jnp.where(tril, sgu_w_s[layer], 0.0).astype(BF16)
        bias_tok = jnp.repeat(sgu_b_s[layer].T, HEAD_DIM, axis=1)
        h_c = _sgu(proj3, w_tril, bias_tok, sgu_norm_g[layer][None, :], sgu_norm_b[layer][None, :])
        wo = w_out[layer]
        wa = wo[:MLSTM_W].astype(BF16)
        wb = wo[MLSTM_W:MLSTM_W + ATTN_W].reshape(ATTN_Q_HEADS, HEAD_DIM, D_MODEL)[np.array(ATTN_HEAD_ORDER)]
        wb = wb.reshape(ATTN_W, D_MODEL).astype(BF16)
        wc = wo[MLSTM_W + ATTN_W:].astype(BF16)
        x2d = _outproj_ln(h_a.reshape(n, MLSTM_W), h_b.reshape(n, ATTN_W), h_c.reshape(n, SGU_W), x2d,
                          wa, wb, wc, ln1_g[layer][None, :], ln1_b[layer][None, :])
        j = layer // 2
        g2, b2 = ln2_g[layer][None, :], ln2_b[layer][None, :]
        if layer % 2 == 0:
            x2d = _ffn_ln(x2d, ffn_w_gate[j].astype(BF16), ffn_w_up[j].astype(BF16),
                          ffn_w_down[j].astype(BF16), g2, b2)
        else:
            x2d = _moe_ln(x2d, moe_w_router[j], moe_b_router[j], moe_w_gate[j], moe_w_up[j],
                          moe_w_down[j], g2, b2)
    return x2d.reshape(bsz, seq, D_MODEL)
```

```python
import functools

import jax
import jax.numpy as jnp
import numpy as np
from jax import lax
from jax.experimental import pallas as pl
from jax.experimental.pallas import tpu as pltpu

F32 = jnp.float32
BF16 = jnp.bfloat16

D_MODEL = 1024
HEAD_DIM = 64
LANES = 128
MLSTM_HEADS = 6
ATTN_Q_HEADS = 6
ATTN_KV_HEADS = 2
SGU_GROUPS = 4
MLSTM_W = MLSTM_HEADS * HEAD_DIM
ATTN_W = ATTN_Q_HEADS * HEAD_DIM
ATTN_KV_W = ATTN_KV_HEADS * HEAD_DIM
SGU_W = SGU_GROUPS * HEAD_DIM
CHUNK = 128
CONV_WIDTH = 4
ROPE_DIM = HEAD_DIM // 4
ROPE_THETA = 500000.0
N_EXPERTS = 8
TOP_K = 2
DEPTH = 2
DN_ALPHA = (2.0 * DEPTH) ** 0.25
LN_EPS = 1e-5

QK_OFF, V_OFF, O_OFF, AQ_OFF = 0, 768, 1152, 1536
GATE_OFF, AK_OFF, AV_OFF, SU_OFF, SV_OFF = 1920, 2048, 2176, 2304, 2560
D_PROJ_PAD = 2816
ATTN_HEAD_ORDER = (0, 3, 1, 4, 2, 5)

VMEM_LIMIT = 56 * 1024 * 1024

ROW_TILE = 512
FFN_ROWS = 2048
FF_TILE = 256
MOE_SUB = 256
MOE_SUPER = 2048
MOE_FF_TILE = 512


def _params(*sem):
    return pltpu.CompilerParams(dimension_semantics=sem, vmem_limit_bytes=VMEM_LIMIT)


def _lane_lo(shape):
    return lax.broadcasted_iota(jnp.int32, shape, len(shape) - 1) < HEAD_DIM


def _layer_norm_rows(z, g, b):
    mu = jnp.mean(z, axis=-1, keepdims=True)
    zc = z - mu
    var = jnp.mean(zc * zc, axis=-1, keepdims=True)
    return zc * lax.rsqrt(var + LN_EPS) * g + b


def _half_layer_norm(x, lo):
    inv = 1.0 / HEAD_DIM
    s_lo = jnp.sum(jnp.where(lo, x, 0.0), axis=-1, keepdims=True)
    s_all = jnp.sum(x, axis=-1, keepdims=True)
    mu = jnp.where(lo, s_lo, s_all - s_lo) * inv
    xc = x - mu
    sq = xc * xc
    q_lo = jnp.sum(jnp.where(lo, sq, 0.0), axis=-1, keepdims=True)
    q_all = jnp.sum(sq, axis=-1, keepdims=True)
    var = jnp.where(lo, q_lo, q_all - q_lo) * inv
    return xc * lax.rsqrt(var + LN_EPS)


def _split3(a):
    h1 = a.astype(BF16)
    r1 = a - h1.astype(F32)
    h2 = r1.astype(BF16)
    r2 = r1 - h2.astype(F32)
    return h1, h2, r2.astype(BF16)


def _dot(a, b):
    return jnp.dot(a, b, preferred_element_type=F32)


def _inproj_kernel(x_ref, w_ref, b_ref, o_ref):
    o_ref[...] = _dot(x_ref[...].astype(BF16), w_ref[...]) + b_ref[...]


def _inproj(x2d, w, b):
    n = x2d.shape[0]
    return pl.pallas_call(
        _inproj_kernel,
        grid=(n // ROW_TILE,),
        in_specs=[pl.BlockSpec((ROW_TILE, D_MODEL), lambda i: (i, 0)),
                  pl.BlockSpec((D_MODEL, D_PROJ_PAD), lambda i: (0, 0)),
                  pl.BlockSpec((1, D_PROJ_PAD), lambda i: (0, 0))],
        out_specs=pl.BlockSpec((ROW_TILE, D_PROJ_PAD), lambda i: (i, 0)),
        out_shape=jax.ShapeDtypeStruct((n, D_PROJ_PAD), F32),
        compiler_params=_params("parallel"),
        name="inproj",
    )(x2d, w, b)


def _mlstm_kernel(qk_ref, v_ref, o_ref, gate_ref, cw_ref, g_ref, out_ref, buf_ref, ct_ref, m_ref):
    c = pl.program_id(1)

    @pl.when(c == 0)
    def _():
        buf_ref[0:8, :] = jnp.zeros((8, 2 * MLSTM_W), F32)
        ct_ref[...] = jnp.zeros_like(ct_ref)
        m_ref[...] = jnp.zeros_like(m_ref)

    buf_ref[8:8 + CHUNK, :] = qk_ref[...]
    cw = cw_ref[...]
    base = 8 - (CONV_WIDTH - 1)
    acc = cw[0:1, :] * buf_ref[base:base + CHUNK, :]
    for j in range(1, CONV_WIDTH):
        acc = acc + cw[j:j + 1, :] * buf_ref[base + j:base + j + CHUNK, :]
    buf_ref[0:8, :] = qk_ref[CHUNK - 8:CHUNK, :]
    qk = jax.nn.silu(acc)

    gates = gate_ref[...]
    lf = jax.nn.log_sigmoid(gates)
    row = lax.broadcasted_iota(jnp.int32, (CHUNK, CHUNK), 0)
    col = lax.broadcasted_iota(jnp.int32, (CHUNK, CHUNK), 1)
    causal = col <= row
    tri = causal.astype(BF16)
    l1, l2, l3 = _split3(lf)
    cum = _dot(tri, l1) + _dot(tri, l2) + _dot(tri, l3)
    gates_t = gates.T
    cum_t = cum.T

    lo = _lane_lo((CHUNK, LANES))
    for p in range(MLSTM_HEADS // 2):
        sl = slice(p * LANES, (p + 1) * LANES)
        q_slab = qk[:, sl] * (HEAD_DIM ** -0.5)
        k_slab = qk[:, MLSTM_W + p * LANES:MLSTM_W + (p + 1) * LANES]
        kt_slab = k_slab.T
        v_slab = v_ref[:, sl]
        ct_pair = ct_ref[p]
        ct_pair_b = ct_pair.astype(BF16)
        halves = []
        new_ct = []
        for half in range(2):
            h = 2 * p + half
            sel = lo if half == 0 else jnp.logical_not(lo)
            li_row = gates_t[h:h + 1, :]
            bc_row = cum_t[MLSTM_HEADS + h:MLSTM_HEADS + h + 1, :]
            bc_col = cum[:, MLSTM_HEADS + h:MLSTM_HEADS + h + 1]
            b_tot = bc_row[:, CHUNK - 1:CHUNK]
            m_prev = m_ref[h:h + 1, 0:1]

            dmat = jnp.where(causal, bc_col - bc_row + li_row, -jnp.inf)
            m_inter = bc_col + m_prev
            m_row = jnp.maximum(m_inter, jnp.max(dmat, axis=-1, keepdims=True))
            q_m = jnp.where(sel, q_slab, 0.0).astype(BF16)
            s = _dot(q_m, kt_slab.astype(BF16))
            pmat = (s * jnp.exp(dmat - m_row)).astype(BF16)
            w_inter = jnp.exp(m_inter - m_row)
            v_aug = jnp.where(sel, v_slab, 1.0).astype(BF16)
            r = _dot(pmat, v_aug) + w_inter * _dot(q_m, ct_pair_b)
            den_lane = HEAD_DIM if half == 0 else 0
            den = r[:, den_lane:den_lane + 1]
            halves.append(r / jnp.maximum(jnp.abs(den), jnp.exp(-m_row)))

            a_row = b_tot - bc_row + li_row
            m_loc = jnp.max(a_row, axis=-1, keepdims=True)
            w_row = jnp.exp(a_row - m_loc)
            kt_h = kt_slab[half * HEAD_DIM:(half + 1) * HEAD_DIM, :]
            ct_loc = _dot((kt_h * w_row).astype(BF16), v_aug)
            m_new = jnp.maximum(b_tot + m_prev, m_loc)
            s_old = jnp.exp(b_tot + m_prev - m_new)
            s_loc = jnp.exp(m_loc - m_new)
            new_ct.append(s_old * ct_pair[half * HEAD_DIM:(half + 1) * HEAD_DIM, :] + s_loc * ct_loc)
            m_ref[h:h + 1, :] = jnp.broadcast_to(m_new, (1, LANES))

        ct_ref[p] = jnp.concatenate(new_ct, axis=0)
        hh = jnp.where(lo, halves[0], halves[1])
        hn = _half_layer_norm(hh, lo) * g_ref[:, sl]
        out_ref[:, sl] = (hn * jax.nn.sigmoid(o_ref[:, sl])).astype(out_ref.dtype)


def _mlstm(proj3, conv_w, norm_g):
    b, s, _ = proj3.shape
    blk = lambda w, off: pl.BlockSpec((None, CHUNK, w), lambda i, c: (i, c, off // w))
    return pl.pallas_call(
        _mlstm_kernel,
        grid=(b, s // CHUNK),
        in_specs=[blk(2 * MLSTM_W, QK_OFF), blk(MLSTM_W, V_OFF), blk(MLSTM_W, O_OFF), blk(LANES, GATE_OFF),
                  pl.BlockSpec((CONV_WIDTH, 2 * MLSTM_W), lambda i, c: (0, 0)),
                  pl.BlockSpec((1, MLSTM_W), lambda i, c: (0, 0))],
        out_specs=pl.BlockSpec((None, CHUNK, MLSTM_W), lambda i, c: (i, c, 0)),
        out_shape=jax.ShapeDtypeStruct((b, s, MLSTM_W), BF16),
        scratch_shapes=[pltpu.VMEM((8 + CHUNK, 2 * MLSTM_W), F32),
                        pltpu.VMEM((MLSTM_HEADS // 2, LANES, LANES), F32),
                        pltpu.VMEM((8, LANES), F32)],
        compiler_params=_params("parallel", "arbitrary"),
        name="mlstm",
    )(proj3, proj3, proj3, proj3, conv_w, norm_g)


def _rope(x, cos_t, sin_t, first):
    return x * cos_t + jnp.where(first, pltpu.roll(x, LANES - ROPE_DIM // 2, 1),
                                 pltpu.roll(x, ROPE_DIM // 2, 1)) * sin_t


def _swa_kernel(sink_ref, q_ref, k_ref, v_ref, cos_ref, sin_ref, out_ref, kt_ref, vv_ref):
    c = pl.program_id(1)

    @pl.when(c == 0)
    def _():
        kt_ref[...] = jnp.zeros_like(kt_ref)
        vv_ref[...] = jnp.zeros_like(vv_ref)

    cos_t = cos_ref[...]
    sin_t = sin_ref[...]
    lane = lax.broadcasted_iota(jnp.int32, (CHUNK, LANES), 1)
    lo = lane < HEAD_DIM
    first = (lane % HEAD_DIM) < ROPE_DIM // 2

    k_cur = _rope(k_ref[...], cos_t, sin_t, first)
    kt_ref[:, CHUNK:2 * CHUNK] = k_cur.T.astype(BF16)
    vv_ref[CHUNK:2 * CHUNK, :] = v_ref[...].astype(BF16)
    kt = kt_ref[...]
    vv = vv_ref[...]

    row = lax.broadcasted_iota(jnp.int32, (CHUNK, 2 * CHUNK), 0)
    col = lax.broadcasted_iota(jnp.int32, (CHUNK, 2 * CHUNK), 1)
    first_block_shift = jnp.where(c > 0, 0, 2 * CHUNK)
    visible = jnp.logical_or(jnp.logical_and(col < CHUNK, col > row + first_block_shift),
                             jnp.logical_and(col >= CHUNK, col - CHUNK <= row))

    for j in range(ATTN_Q_HEADS // 2):
        sl = slice(j * LANES, (j + 1) * LANES)
        q_slab = _rope(q_ref[:, sl], cos_t, sin_t, first) * (HEAD_DIM ** -0.5)
        outs = []
        for half in range(2):
            sel = lo if half == 0 else jnp.logical_not(lo)
            sink = sink_ref[ATTN_HEAD_ORDER[2 * j + half]]
            q_m = jnp.where(sel, q_slab, 0.0).astype(BF16)
            s = jnp.where(visible, _dot(q_m, kt), -jnp.inf)
            mx = jnp.maximum(jnp.max(s, axis=-1, keepdims=True), sink)
            pexp = jnp.exp(s - mx)
            denom = jnp.sum(pexp, axis=-1, keepdims=True) + jnp.exp(sink - mx)
            outs.append(_dot(pexp.astype(BF16), vv) / denom)
        out_ref[:, sl] = jnp.where(lo, outs[0], outs[1]).astype(out_ref.dtype)

    kt_ref[:, 0:CHUNK] = kt_ref[:, CHUNK:2 * CHUNK]
    vv_ref[0:CHUNK, :] = vv_ref[CHUNK:2 * CHUNK, :]


def _swa(proj3, cos_t, sin_t, sinks):
    b, s, _ = proj3.shape
    blk = lambda w, off: pl.BlockSpec((None, CHUNK, w), lambda i, c: (i, c, off // w))
    tab = pl.BlockSpec((None, CHUNK, LANES), lambda i, c: (i, c, 0))
    return pl.pallas_call(
        _swa_kernel,
        grid=(b, s // CHUNK),
        in_specs=[pl.BlockSpec(memory_space=pltpu.SMEM),
                  blk(ATTN_W, AQ_OFF), blk(ATTN_KV_W, AK_OFF), blk(ATTN_KV_W, AV_OFF), tab, tab],
        out_specs=pl.BlockSpec((None, CHUNK, ATTN_W), lambda i, c: (i, c, 0)),
        out_shape=jax.ShapeDtypeStruct((b, s, ATTN_W), BF16),
        scratch_shapes=[pltpu.VMEM((LANES, 2 * CHUNK), BF16), pltpu.VMEM((2 * CHUNK, LANES), BF16)],
        compiler_params=_params("parallel", "arbitrary"),
        name="swa",
    )(sinks, proj3, proj3, proj3, cos_t, sin_t)


def _sgu_kernel(u_ref, v_ref, w_ref, bias_ref, g_ref, b_ref, out_ref):
    lo = _lane_lo((CHUNK, LANES))
    for j in range(SGU_GROUPS // 2):
        sl = slice(j * LANES, (j + 1) * LANES)
        u = jax.nn.gelu(u_ref[:, sl])
        v = jax.nn.gelu(v_ref[:, sl])
        vn = (_half_layer_norm(v, lo) * g_ref[:, sl] + b_ref[:, sl]).astype(BF16)
        mixed = jnp.where(lo, _dot(w_ref[2 * j], vn), _dot(w_ref[2 * j + 1], vn)) + bias_ref[:, sl]
        out_ref[:, sl] = (u * mixed).astype(out_ref.dtype)


def _sgu(proj3, w_tril, bias_tok, norm_g, norm_b):
    b, s, _ = proj3.shape
    blk = lambda w, off: pl.BlockSpec((None, CHUNK, w), lambda i, c: (i, c, off // w))
    const2 = lambda shape: pl.BlockSpec(shape, lambda i, c: (0,) * len(shape))
    return pl.pallas_call(
        _sgu_kernel,
        grid=(b, s // CHUNK),
        in_specs=[blk(SGU_W, SU_OFF), blk(SGU_W, SV_OFF), const2((SGU_GROUPS, CHUNK, CHUNK)),
                  const2((CHUNK, SGU_W)), const2((1, SGU_W)), const2((1, SGU_W))],
        out_specs=pl.BlockSpec((None, CHUNK, SGU_W), lambda i, c: (i, c, 0)),
        out_shape=jax.ShapeDtypeStruct((b, s, SGU_W), BF16),
        compiler_params=_params("parallel", "parallel"),
        name="sgu",
    )(proj3, proj3, w_tril, bias_tok, norm_g, norm_b)


def _outproj_kernel(ha_ref, hb_ref, hc_ref, x_ref, wa_ref, wb_ref, wc_ref, g_ref, b_ref, o_ref):
    mix = _dot(ha_ref[...], wa_ref[...]) + _dot(hb_ref[...], wb_ref[...]) + _dot(hc_ref[...], wc_ref[...])
    o_ref[...] = _layer_norm_rows(DN_ALPHA * x_ref[...] + mix, g_ref[...], b_ref[...])


def _outproj_ln(ha, hb, hc, x2d, wa, wb, wc, g, b):
    n = x2d.shape[0]
    rows = lambda w: pl.BlockSpec((ROW_TILE, w), lambda i: (i, 0))
    const = lambda shape: pl.BlockSpec(shape, lambda i: (0, 0))
    return pl.pallas_call(
        _outproj_kernel,
        grid=(n // ROW_TILE,),
        in_specs=[rows(MLSTM_W), rows(ATTN_W), rows(SGU_W), rows(D_MODEL),
                  const((MLSTM_W, D_MODEL)), const((ATTN_W, D_MODEL)), const((SGU_W, D_MODEL)),
                  const((1, D_MODEL)), const((1, D_MODEL))],
        out_specs=rows(D_MODEL),
        out_shape=jax.ShapeDtypeStruct((n, D_MODEL), F32),
        compiler_params=_params("parallel"),
        name="outproj_ln",
    )(ha, hb, hc, x2d, wa, wb, wc, g, b)


def _ffn_kernel(x_ref, wg_ref, wu_ref, wd_ref, g_ref, b_ref, o_ref):
    j = pl.program_id(1)
    last = pl.num_programs(1) - 1

    def chunk(r, carry):
        rows = pl.ds(pl.multiple_of(r * ROW_TILE, ROW_TILE), ROW_TILE)
        xb = x_ref[rows, :].astype(BF16)
        hidden = (jax.nn.silu(_dot(xb, wg_ref[...])) * _dot(xb, wu_ref[...])).astype(BF16)
        part = _dot(hidden, wd_ref[...])

        @pl.when(j == 0)
        def _():
            o_ref[rows, :] = part

        @pl.when(jnp.logical_and(j > 0, j < last))
        def _():
            o_ref[rows, :] += part

        @pl.when(j == last)
        def _():
            o_ref[rows, :] = _layer_norm_rows(DN_ALPHA * x_ref[rows, :] + (o_ref[rows, :] + part),
                                              g_ref[...], b_ref[...])

        return carry

    lax.fori_loop(0, FFN_ROWS // ROW_TILE, chunk, 0)


def _ffn_ln(x2d, wg, wu, wd, g, b):
    n = x2d.shape[0]
    d_ff = wg.shape[1]
    tm = FFN_ROWS
    assert d_ff // FF_TILE >= 2
    return pl.pallas_call(
        _ffn_kernel,
        grid=(n // tm, d_ff // FF_TILE),
        in_specs=[pl.BlockSpec((tm, D_MODEL), lambda i, j: (i, 0)),
                  pl.BlockSpec((D_MODEL, FF_TILE), lambda i, j: (0, j)),
                  pl.BlockSpec((D_MODEL, FF_TILE), lambda i, j: (0, j)),
                  pl.BlockSpec((FF_TILE, D_MODEL), lambda i, j: (j, 0)),
                  pl.BlockSpec((1, D_MODEL), lambda i, j: (0, 0)),
                  pl.BlockSpec((1, D_MODEL), lambda i, j: (0, 0))],
        out_specs=pl.BlockSpec((tm, D_MODEL), lambda i, j: (i, 0)),
        out_shape=jax.ShapeDtypeStruct((n, D_MODEL), F32),
        compiler_params=_params("parallel", "arbitrary"),
        name="ffn_ln",
    )(x2d, wg, wu, wd, g, b)


def _router_kernel(x_ref, wh_ref, wl_ref, b_ref, o_ref):
    x = x_ref[...]
    xh = x.astype(BF16)
    xl = (x - xh.astype(F32)).astype(BF16)
    logits = _dot(xh, wh_ref[...]) + _dot(xl, wh_ref[...]) + _dot(xh, wl_ref[...]) + b_ref[...]
    lane = lax.broadcasted_iota(jnp.int32, logits.shape, 1)
    logits = jnp.where(lane < N_EXPERTS, logits, -jnp.inf)
    m1 = jnp.max(logits, axis=-1, keepdims=True)
    i1 = jnp.min(jnp.where(logits == m1, lane, LANES), axis=-1, keepdims=True)
    rest = jnp.where(lane == i1, -jnp.inf, logits)
    m2 = jnp.max(rest, axis=-1, keepdims=True)
    i2 = jnp.min(jnp.where(rest == m2, lane, LANES), axis=-1, keepdims=True)
    e2 = jnp.exp(m2 - m1)
    g1 = 1.0 / (1.0 + e2)
    g2 = e2 / (1.0 + e2)
    o_ref[...] = jnp.where(lane == 0, g1, jnp.where(lane == 1, g2, jnp.where(
        lane == 2, i1.astype(F32), jnp.where(lane == 3, i2.astype(F32), 0.0))))


def _router(x2d, w_hi, w_lo, b):
    n = x2d.shape[0]
    const = lambda shape: pl.BlockSpec(shape, lambda i: (0, 0))
    return pl.pallas_call(
        _router_kernel,
        grid=(n // ROW_TILE,),
        in_specs=[pl.BlockSpec((ROW_TILE, D_MODEL), lambda i: (i, 0)),
                  const((D_MODEL, LANES)), const((D_MODEL, LANES)), const((1, LANES))],
        out_specs=pl.BlockSpec((ROW_TILE, LANES), lambda i: (i, 0)),
        out_shape=jax.ShapeDtypeStruct((n, LANES), F32),
        compiler_params=_params("parallel"),
        name="router",
    )(x2d, w_hi, w_lo, b)


def _dispatch_kernel(slot_ref, ends_ref, x_ref, o_hbm, stage_ref, zero_ref, sem, zero_sem):
    t = pl.program_id(0)
    last = pl.num_programs(0) - 1
    s = t % 2

    def wait_tile(ss):
        for _ in range(TOP_K):
            pltpu.make_async_copy(stage_ref.at[ss], o_hbm.at[pl.ds(0, MOE_SUB), :], sem.at[ss]).wait()

    @pl.when(t == 0)
    def _():
        zero_ref[...] = jnp.zeros_like(zero_ref)

        def zero_sub_tile(m, carry):
            dst = o_hbm.at[pl.ds(pl.multiple_of(m * MOE_SUB, MOE_SUB), MOE_SUB), :]
            cp = pltpu.make_async_copy(zero_ref, dst, zero_sem)
            cp.start()
            cp.wait()
            return carry

        for e in range(N_EXPERTS):
            @pl.when(ends_ref[e] >= MOE_SUB)
            def _():
                zero_sub_tile(ends_ref[e] // MOE_SUB - 1, 0)

        lax.fori_loop(ends_ref[N_EXPERTS - 1] // MOE_SUB, o_hbm.shape[0] // MOE_SUB, zero_sub_tile, 0)

    @pl.when(t >= 2)
    def _():
        wait_tile(s)

    stage_ref[s] = x_ref[...]

    def issue(r, carry):
        base = (t * MOE_SUB + r) * TOP_K
        for k in range(TOP_K):
            pltpu.make_async_copy(stage_ref.at[s, pl.ds(r, 1), :],
                                  o_hbm.at[pl.ds(slot_ref[base + k], 1), :], sem.at[s]).start()
        return carry

    lax.fori_loop(0, MOE_SUB, issue, 0, unroll=8)

    @pl.when(t == last)
    def _():
        wait_tile(s)
        wait_tile(1 - s)


def _dispatch_rows(x2d, slots, ends, n_rows):
    n = x2d.shape[0]
    assert n // MOE_SUB >= 2
    return pl.pallas_call(
        _dispatch_kernel,
        grid_spec=pltpu.PrefetchScalarGridSpec(
            num_scalar_prefetch=2,
            grid=(n // MOE_SUB,),
            in_specs=[pl.BlockSpec((MOE_SUB, D_MODEL), lambda t, sl, en: (t, 0))],
            out_specs=pl.BlockSpec(memory_space=pl.ANY),
            scratch_shapes=[pltpu.VMEM((2, MOE_SUB, D_MODEL), F32), pltpu.VMEM((MOE_SUB, D_MODEL), F32),
                            pltpu.SemaphoreType.DMA((2,)), pltpu.SemaphoreType.DMA(())]),
        out_shape=jax.ShapeDtypeStruct((n_rows, D_MODEL), F32),
        compiler_params=_params("arbitrary"),
        name="moe_dispatch",
    )(slots, ends, x2d)


def _moe_kernel(tile_ref, exp_ref, lo_ref, hi_ref, first_ref,
                x_ref, wg_ref, wu_ref, wd_ref, o_ref, wgb_ref, wub_ref, wdb_ref):
    v = pl.program_id(0)
    j = pl.program_id(1)

    @pl.when(jnp.logical_and(first_ref[v] == 1, j == 0))
    def _():
        o_ref[...] = jnp.zeros_like(o_ref)

    lo = lo_ref[v]
    hi = hi_ref[v]

    @pl.when(hi > lo)
    def _():
        wgb_ref[...] = wg_ref[...].astype(BF16)
        wub_ref[...] = wu_ref[...].astype(BF16)
        wdb_ref[...] = wd_ref[...].astype(BF16)

        def body(m, carry):
            rows = pl.ds(pl.multiple_of(m * MOE_SUB, MOE_SUB), MOE_SUB)
            xs = x_ref[rows, :].astype(BF16)
            hidden = (jax.nn.silu(_dot(xs, wgb_ref[...])) * _dot(xs, wub_ref[...])).astype(BF16)
            o_ref[rows, :] += _dot(hidden, wdb_ref[...])
            return carry

        lax.fori_loop(lo, hi, body, 0)


def _moe_grouped(xs, wg, wu, wd, visit_tile, visit_exp, visit_lo, visit_hi, visit_first):
    n_rows = xs.shape[0]
    n_visits = visit_tile.shape[0]
    d_ff = wg.shape[2]
    return pl.pallas_call(
        _moe_kernel,
        grid_spec=pltpu.PrefetchScalarGridSpec(
            num_scalar_prefetch=5,
            grid=(n_visits, d_ff // MOE_FF_TILE),
            in_specs=[pl.BlockSpec((MOE_SUPER, D_MODEL), lambda v, j, t, e, lo, hi, f: (t[v], 0)),
                      pl.BlockSpec((None, D_MODEL, MOE_FF_TILE), lambda v, j, t, e, lo, hi, f: (e[v], 0, j)),
                      pl.BlockSpec((None, D_MODEL, MOE_FF_TILE), lambda v, j, t, e, lo, hi, f: (e[v], 0, j)),
                      pl.BlockSpec((None, MOE_FF_TILE, D_MODEL), lambda v, j, t, e, lo, hi, f: (e[v], j, 0))],
            out_specs=pl.BlockSpec((MOE_SUPER, D_MODEL), lambda v, j, t, e, lo, hi, f: (t[v], 0)),
            scratch_shapes=[pltpu.VMEM((D_MODEL, MOE_FF_TILE), BF16), pltpu.VMEM((D_MODEL, MOE_FF_TILE), BF16),
                            pltpu.VMEM((MOE_FF_TILE, D_MODEL), BF16)]),
        out_shape=jax.ShapeDtypeStruct((n_rows, D_MODEL), F32),
        compiler_params=_params("arbitrary", "arbitrary"),
        name="moe_grouped",
    )(visit_tile, visit_exp, visit_lo, visit_hi, visit_first, xs, wg, wu, wd)


def _combine_kernel(slot_ref, x_ref, gate_ref, y_hbm, g_ref, b_ref, o_ref, buf_ref, sem):
    t = pl.program_id(0)
    s = t % 2

    def issue_tile(tt, ss):
        def issue(r, carry):
            base = (tt * MOE_SUB + r) * TOP_K
            for k in range(TOP_K):
                pltpu.make_async_copy(y_hbm.at[pl.ds(slot_ref[base + k], 1), :],
                                      buf_ref.at[ss, k, pl.ds(r, 1), :], sem.at[ss]).start()
            return carry

        lax.fori_loop(0, MOE_SUB, issue, 0, unroll=8)

    @pl.when(t == 0)
    def _():
        issue_tile(0, 0)

    @pl.when(t + 1 < pl.num_programs(0))
    def _():
        issue_tile(t + 1, 1 - s)

    for k in range(TOP_K):
        pltpu.make_async_copy(y_hbm.at[pl.ds(0, MOE_SUB), :], buf_ref.at[s, k], sem.at[s]).wait()
    gate = gate_ref[...]
    ff = gate[:, 0:1] * buf_ref[s, 0] + gate[:, 1:2] * buf_ref[s, 1]
    o_ref[...] = _layer_norm_rows(DN_ALPHA * x_ref[...] + ff, g_ref[...], b_ref[...])


def _combine_ln(x2d, route, slots, ys, g, b):
    n = x2d.shape[0]
    return pl.pallas_call(
        _combine_kernel,
        grid_spec=pltpu.PrefetchScalarGridSpec(
            num_scalar_prefetch=1,
            grid=(n // MOE_SUB,),
            in_specs=[pl.BlockSpec((MOE_SUB, D_MODEL), lambda t, s: (t, 0)),
                      pl.BlockSpec((MOE_SUB, LANES), lambda t, s: (t, 0)),
                      pl.BlockSpec(memory_space=pl.ANY),
                      pl.BlockSpec((1, D_MODEL), lambda t, s: (0, 0)),
                      pl.BlockSpec((1, D_MODEL), lambda t, s: (0, 0))],
            out_specs=pl.BlockSpec((MOE_SUB, D_MODEL), lambda t, s: (t, 0)),
            scratch_shapes=[pltpu.VMEM((2, TOP_K, MOE_SUB, D_MODEL), F32), pltpu.SemaphoreType.DMA((2,))]),
        out_shape=jax.ShapeDtypeStruct((n, D_MODEL), F32),
        compiler_params=_params("arbitrary"),
        name="moe_combine_ln",
    )(slots, x2d, route, ys, g, b)


def _routing_tables(route, n):
    idx = route[:, 2:4].astype(jnp.int32)
    onehot = (idx[:, :, None] == jnp.arange(N_EXPERTS, dtype=jnp.int32)).astype(jnp.int32).sum(1)
    rank = jnp.cumsum(onehot, axis=0) - onehot
    counts = jnp.sum(onehot, axis=0)
    padded = ((counts + MOE_SUB - 1) // MOE_SUB) * MOE_SUB
    ends = jnp.cumsum(padded)
    starts = ends - padded
    slot = starts[idx] + jnp.take_along_axis(rank, idx, axis=1)

    n_rows = ((n * TOP_K + N_EXPERTS * (MOE_SUB - 1)) // MOE_SUPER + 1) * MOE_SUPER

    per = MOE_SUPER // MOE_SUB
    n_super = n_rows // MOE_SUPER
    tile0 = jnp.arange(n_super, dtype=jnp.int32)[:, None] * per
    lo = jnp.clip(starts[None, :] // MOE_SUB - tile0, 0, per)
    hi = jnp.clip(ends[None, :] // MOE_SUB - tile0, 0, per)
    valid = (hi > lo).reshape(-1)
    n_visits = n_super + N_EXPERTS - 1
    order = jnp.argsort(jnp.logical_not(valid), stable=True)[:n_visits]
    n_valid = jnp.sum(valid.astype(jnp.int32))
    live = jnp.arange(n_visits) < n_valid
    last = order[jnp.maximum(n_valid - 1, 0)]
    order = jnp.where(live, order, last)
    v_tile = (order // N_EXPERTS).astype(jnp.int32)
    v_exp = (order % N_EXPERTS).astype(jnp.int32)
    v_lo = jnp.where(live, lo.reshape(-1)[order], 0).astype(jnp.int32)
    v_hi = jnp.where(live, hi.reshape(-1)[order], 0).astype(jnp.int32)
    prev_tile = jnp.concatenate([jnp.full((1,), -1, jnp.int32), v_tile[:-1]])
    v_first = jnp.logical_and(live, v_tile != prev_tile).astype(jnp.int32)
    return slot.reshape(-1).astype(jnp.int32), ends.astype(jnp.int32), n_rows, (v_tile, v_exp, v_lo, v_hi, v_first)


def _moe_ln(x2d, w_router, b_router, wg, wu, wd, g, b):
    n = x2d.shape[0]
    w_pad = jnp.zeros((D_MODEL, LANES), F32).at[:, :N_EXPERTS].set(w_router)
    w_hi = w_pad.astype(BF16)
    w_lo = (w_pad - w_hi.astype(F32)).astype(BF16)
    b_pad = jnp.zeros((1, LANES), F32).at[0, :N_EXPERTS].set(b_router)
    route = _router(x2d, w_hi, w_lo, b_pad)
    slots, ends, n_rows, visits = _routing_tables(route, n)
    xs = _dispatch_rows(x2d, slots, ends, n_rows)
    ys = _moe_grouped(xs, wg, wu, wd, *visits)
    return _combine_ln(x2d, route, slots, ys, g, b)


def _layout_in_proj(w_in, b_in):
    sizes = (MLSTM_W, MLSTM_W, MLSTM_W, MLSTM_W, MLSTM_HEADS, MLSTM_HEADS,
             ATTN_W, ATTN_KV_W, ATTN_KV_W, SGU_W, SGU_W)
    offs = np.concatenate([[0], np.cumsum(sizes)])
    seg = lambda a, i: a[..., int(offs[i]):int(offs[i + 1])]

    def build(a):
        lead = a.shape[:-1]
        aq = seg(a, 6).reshape(lead + (ATTN_Q_HEADS, HEAD_DIM))[..., np.array(ATTN_HEAD_ORDER), :]
        gates = jnp.concatenate([seg(a, 4), seg(a, 5),
                                 jnp.zeros(lead + (LANES - 2 * MLSTM_HEADS,), a.dtype)], -1)
        return jnp.concatenate([seg(a, 0), seg(a, 1), seg(a, 2), seg(a, 3), aq.reshape(lead + (ATTN_W,)),
                                gates, seg(a, 7), seg(a, 8), seg(a, 9), seg(a, 10)], -1)

    return build(w_in).astype(BF16), build(b_in)[None, :]


def _rope_tables(positions):
    inv_freq = ROPE_THETA ** (-jnp.arange(0, ROPE_DIM, 2, dtype=F32) / ROPE_DIM)
    ang = positions.astype(F32)[..., None] * inv_freq
    cos, sin = jnp.cos(ang), jnp.sin(ang)
    half = ROPE_DIM // 2
    ones = jnp.ones(ang.shape[:-1] + (HEAD_DIM - ROPE_DIM,), F32)
    cos_head = jnp.concatenate([cos, cos, ones], -1)
    sin_head = jnp.concatenate([-sin, sin, 0.0 * ones], -1)
    del half
    return jnp.concatenate([cos_head, cos_head], -1), jnp.concatenate([sin_head, sin_head], -1)


def kernel(x, positions, w_in, b_in, conv_w, mlstm_norm_g, attn_sinks, sgu_w_s, sgu_b_s, sgu_norm_g, sgu_norm_b, w_out, ln1_g, ln1_b, ln2_g, ln2_b, ffn_w_gate, ffn_w_up, ffn_w_down, moe_w_router, moe_b_router, moe_w_gate, moe_w_up, moe_w_down):
    bsz, seq, _ = x.shape
    n = bsz * seq
    cos_t, sin_t = _rope_tables(positions)
    tril = jnp.tril(jnp.ones((CHUNK, CHUNK), bool))
    x2d = x.reshape(n, D_MODEL)
    for layer in range(DEPTH):
        w_p, b_p = _layout_in_proj(w_in[layer], b_in[layer])
        proj3 = _inproj(x2d, w_p, b_p).reshape(bsz, seq, D_PROJ_PAD)
        h_a = _mlstm(proj3, conv_w[layer], mlstm_norm_g[layer][None, :])
        h_b = _swa(proj3, cos_t, sin_t, attn_sinks[layer])
        w_tril = jnp.where(tril, sgu_w_s[layer], 0.0).astype(BF16)
        bias_tok = jnp.repeat(sgu_b_s[layer].T, HEAD_DIM, axis=1)
        h_c = _sgu(proj3, w_tril, bias_tok, sgu_norm_g[layer][None, :], sgu_norm_b[layer][None, :])
        wo = w_out[layer]
        wa = wo[:MLSTM_W].astype(BF16)
        wb = wo[MLSTM_W:MLSTM_W + ATTN_W].reshape(ATTN_Q_HEADS, HEAD_DIM, D_MODEL)[np.array(ATTN_HEAD_ORDER)]
        wb = wb.reshape(ATTN_W, D_MODEL).astype(BF16)
        wc = wo[MLSTM_W + ATTN_W:].astype(BF16)
        x2d = _outproj_ln(h_a.reshape(n, MLSTM_W), h_b.reshape(n, ATTN_W), h_c.reshape(n, SGU_W), x2d,
                          wa, wb, wc, ln1_g[layer][None, :], ln1_b[layer][None, :])
        j = layer // 2
        g2, b2 = ln2_g[layer][None, :], ln2_b[layer][None, :]
        if layer % 2 == 0:
            x2d = _ffn_ln(x2d, ffn_w_gate[j].astype(BF16), ffn_w_up[j].astype(BF16),
                          ffn_w_down[j].astype(BF16), g2, b2)
        else:
            x2d = _moe_ln(x2d, moe_w_router[j], moe_b_router[j], moe_w_gate[j], moe_w_up[j],
                          moe_w_down[j], g2, b2)
    return x2d.reshape(bsz, seq, D_MODEL)
```

```python
import functools

import jax
import jax.numpy as jnp
import numpy as np
from jax import lax
from jax.experimental import pallas as pl
from jax.experimental.pallas import tpu as pltpu

F32 = jnp.float32
BF16 = jnp.bfloat16

D_MODEL = 1024
HEAD_DIM = 64
LANES = 128
MLSTM_HEADS = 6
ATTN_Q_HEADS = 6
ATTN_KV_HEADS = 2
SGU_GROUPS = 4
MLSTM_W = MLSTM_HEADS * HEAD_DIM
ATTN_W = ATTN_Q_HEADS * HEAD_DIM
ATTN_KV_W = ATTN_KV_HEADS * HEAD_DIM
SGU_W = SGU_GROUPS * HEAD_DIM
CHUNK = 128
CONV_WIDTH = 4
ROPE_DIM = HEAD_DIM // 4
ROPE_THETA = 500000.0
N_EXPERTS = 8
TOP_K = 2
DEPTH = 2
DN_ALPHA = (2.0 * DEPTH) ** 0.25
LN_EPS = 1e-5

QK_OFF, V_OFF, O_OFF, AQ_OFF = 0, 768, 1152, 1536
GATE_OFF, AK_OFF, AV_OFF, SU_OFF, SV_OFF = 1920, 2048, 2176, 2304, 2560
D_PROJ_PAD = 2816
ATTN_HEAD_ORDER = (0, 3, 1, 4, 2, 5)

VMEM_LIMIT = 56 * 1024 * 1024

ROW_TILE = 512
FFN_ROWS = 1024
FF_TILE = 256
MOE_SUB = 256
MOE_VISIT_SUBS = 18
MOE_FF_TILE = 512


def _params(*sem):
    return pltpu.CompilerParams(dimension_semantics=sem, vmem_limit_bytes=VMEM_LIMIT)


def _lane_lo(shape):
    return lax.broadcasted_iota(jnp.int32, shape, len(shape) - 1) < HEAD_DIM


def _layer_norm_rows(z, g, b):
    mu = jnp.mean(z, axis=-1, keepdims=True)
    zc = z - mu
    var = jnp.mean(zc * zc, axis=-1, keepdims=True)
    return zc * lax.rsqrt(var + LN_EPS) * g + b


def _half_layer_norm(x, lo):
    inv = 1.0 / HEAD_DIM
    s_lo = jnp.sum(jnp.where(lo, x, 0.0), axis=-1, keepdims=True)
    s_all = jnp.sum(x, axis=-1, keepdims=True)
    mu = jnp.where(lo, s_lo, s_all - s_lo) * inv
    xc = x - mu
    sq = xc * xc
    q_lo = jnp.sum(jnp.where(lo, sq, 0.0), axis=-1, keepdims=True)
    q_all = jnp.sum(sq, axis=-1, keepdims=True)
    var = jnp.where(lo, q_lo, q_all - q_lo) * inv
    return xc * lax.rsqrt(var + LN_EPS)


def _split3(a):
    h1 = a.astype(BF16)
    r1 = a - h1.astype(F32)
    h2 = r1.astype(BF16)
    r2 = r1 - h2.astype(F32)
    return h1, h2, r2.astype(BF16)


def _dot(a, b):
    return jnp.dot(a, b, preferred_element_type=F32)


def _inproj_kernel(x_ref, w_ref, b_ref, o_ref):
    o_ref[...] = _dot(x_ref[...].astype(BF16), w_ref[...]) + b_ref[...]


def _inproj(x2d, w, b):
    n = x2d.shape[0]
    return pl.pallas_call(
        _inproj_kernel,
        grid=(n // ROW_TILE,),
        in_specs=[pl.BlockSpec((ROW_TILE, D_MODEL), lambda i: (i, 0)),
                  pl.BlockSpec((D_MODEL, D_PROJ_PAD), lambda i: (0, 0)),
                  pl.BlockSpec((1, D_PROJ_PAD), lambda i: (0, 0))],
        out_specs=pl.BlockSpec((ROW_TILE, D_PROJ_PAD), lambda i: (i, 0)),
        out_shape=jax.ShapeDtypeStruct((n, D_PROJ_PAD), F32),
        compiler_params=_params("parallel"),
        name="inproj",
    )(x2d, w, b)


def _mlstm_kernel(qk_ref, v_ref, o_ref, gate_ref, cw_ref, g_ref, out_ref, buf_ref, ct_ref, m_ref):
    c = pl.program_id(1)

    @pl.when(c == 0)
    def _():
        buf_ref[0:8, :] = jnp.zeros((8, 2 * MLSTM_W), F32)
        ct_ref[...] = jnp.zeros_like(ct_ref)
        m_ref[...] = jnp.zeros_like(m_ref)

    buf_ref[8:8 + CHUNK, :] = qk_ref[...]
    cw = cw_ref[...]
    base = 8 - (CONV_WIDTH - 1)
    acc = cw[0:1, :] * buf_ref[base:base + CHUNK, :]
    for j in range(1, CONV_WIDTH):
        acc = acc + cw[j:j + 1, :] * buf_ref[base + j:base + j + CHUNK, :]
    buf_ref[0:8, :] = qk_ref[CHUNK - 8:CHUNK, :]
    qk = jax.nn.silu(acc)

    gates = gate_ref[...]
    lf = jax.nn.log_sigmoid(gates)
    row = lax.broadcasted_iota(jnp.int32, (CHUNK, CHUNK), 0)
    col = lax.broadcasted_iota(jnp.int32, (CHUNK, CHUNK), 1)
    causal = col <= row
    tri = causal.astype(BF16)
    l1, l2, l3 = _split3(lf)
    cum = _dot(tri, l1) + _dot(tri, l2) + _dot(tri, l3)
    gates_t = gates.T
    cum_t = cum.T

    lo = _lane_lo((CHUNK, LANES))
    for p in range(MLSTM_HEADS // 2):
        sl = slice(p * LANES, (p + 1) * LANES)
        q_slab = qk[:, sl] * (HEAD_DIM ** -0.5)
        k_slab = qk[:, MLSTM_W + p * LANES:MLSTM_W + (p + 1) * LANES]
        kt_slab = k_slab.T
        v_slab = v_ref[:, sl]
        ct_pair = ct_ref[p]
        ct_pair_b = ct_pair.astype(BF16)
        halves = []
        new_ct = []
        for half in range(2):
            h = 2 * p + half
            sel = lo if half == 0 else jnp.logical_not(lo)
            li_row = gates_t[h:h + 1, :]
            bc_row = cum_t[MLSTM_HEADS + h:MLSTM_HEADS + h + 1, :]
            bc_col = cum[:, MLSTM_HEADS + h:MLSTM_HEADS + h + 1]
            b_tot = bc_row[:, CHUNK - 1:CHUNK]
            m_prev = m_ref[h:h + 1, 0:1]

            dmat = jnp.where(causal, bc_col - bc_row + li_row, -jnp.inf)
            m_inter = bc_col + m_prev
            m_row = jnp.maximum(m_inter, jnp.max(dmat, axis=-1, keepdims=True))
            q_m = jnp.where(sel, q_slab, 0.0).astype(BF16)
            s = _dot(q_m, kt_slab.astype(BF16))
            pmat = (s * jnp.exp(dmat - m_row)).astype(BF16)
            w_inter = jnp.exp(m_inter - m_row)
            v_aug = jnp.where(sel, v_slab, 1.0).astype(BF16)
            r = _dot(pmat, v_aug) + w_inter * _dot(q_m, ct_pair_b)
            den_lane = HEAD_DIM if half == 0 else 0
            den = r[:, den_lane:den_lane + 1]
            halves.append(r / jnp.maximum(jnp.abs(den), jnp.exp(-m_row)))

            a_row = b_tot - bc_row + li_row
            m_loc = jnp.max(a_row, axis=-1, keepdims=True)
            w_row = jnp.exp(a_row - m_loc)
            kt_h = kt_slab[half * HEAD_DIM:(half + 1) * HEAD_DIM, :]
            ct_loc = _dot((kt_h * w_row).astype(BF16), v_aug)
            m_new = jnp.maximum(b_tot + m_prev, m_loc)
            s_old = jnp.exp(b_tot + m_prev - m_new)
            s_loc = jnp.exp(m_loc - m_new)
            new_ct.append(s_old * ct_pair[half * HEAD_DIM:(half + 1) * HEAD_DIM, :] + s_loc * ct_loc)
            m_ref[h:h + 1, :] = jnp.broadcast_to(m_new, (1, LANES))

        ct_ref[p] = jnp.concatenate(new_ct, axis=0)
        hh = jnp.where(lo, halves[0], halves[1])
        hn = _half_layer_norm(hh, lo) * g_ref[:, sl]
        out_ref[:, sl] = (hn * jax.nn.sigmoid(o_ref[:, sl])).astype(out_ref.dtype)


def _mlstm(proj3, conv_w, norm_g):
    b, s, _ = proj3.shape
    blk = lambda w, off: pl.BlockSpec((None, CHUNK, w), lambda i, c: (i, c, off // w))
    return pl.pallas_call(
        _mlstm_kernel,
        grid=(b, s // CHUNK),
        in_specs=[blk(2 * MLSTM_W, QK_OFF), blk(MLSTM_W, V_OFF), blk(MLSTM_W, O_OFF), blk(LANES, GATE_OFF),
                  pl.BlockSpec((CONV_WIDTH, 2 * MLSTM_W), lambda i, c: (0, 0)),
                  pl.BlockSpec((1, MLSTM_W), lambda i, c: (0, 0))],
        out_specs=pl.BlockSpec((None, CHUNK, MLSTM_W), lambda i, c: (i, c, 0)),
        out_shape=jax.ShapeDtypeStruct((b, s, MLSTM_W), BF16),
        scratch_shapes=[pltpu.VMEM((8 + CHUNK, 2 * MLSTM_W), F32),
                        pltpu.VMEM((MLSTM_HEADS // 2, LANES, LANES), F32),
                        pltpu.VMEM((8, LANES), F32)],
        compiler_params=_params("parallel", "arbitrary"),
        name="mlstm",
    )(proj3, proj3, proj3, proj3, conv_w, norm_g)


def _rope(x, cos_t, sin_t, first):
    return x * cos_t + jnp.where(first, pltpu.roll(x, LANES - ROPE_DIM // 2, 1),
                                 pltpu.roll(x, ROPE_DIM // 2, 1)) * sin_t


def _swa_kernel(sink_ref, q_ref, k_ref, v_ref, cos_ref, sin_ref, out_ref, kt_ref, vv_ref):
    c = pl.program_id(1)

    @pl.when(c == 0)
    def _():
        kt_ref[...] = jnp.zeros_like(kt_ref)
        vv_ref[...] = jnp.zeros_like(vv_ref)

    cos_t = cos_ref[...]
    sin_t = sin_ref[...]
    lane = lax.broadcasted_iota(jnp.int32, (CHUNK, LANES), 1)
    lo = lane < HEAD_DIM
    first = (lane % HEAD_DIM) < ROPE_DIM // 2

    k_cur = _rope(k_ref[...], cos_t, sin_t, first)
    kt_ref[:, CHUNK:2 * CHUNK] = k_cur.T.astype(BF16)
    vv_ref[CHUNK:2 * CHUNK, :] = v_ref[...].astype(BF16)
    kt = kt_ref[...]
    vv = vv_ref[...]

    row = lax.broadcasted_iota(jnp.int32, (CHUNK, 2 * CHUNK), 0)
    col = lax.broadcasted_iota(jnp.int32, (CHUNK, 2 * CHUNK), 1)
    first_block_shift = jnp.where(c > 0, 0, 2 * CHUNK)
    visible = jnp.logical_or(jnp.logical_and(col < CHUNK, col > row + first_block_shift),
                             jnp.logical_and(col >= CHUNK, col - CHUNK <= row))

    for j in range(ATTN_Q_HEADS // 2):
        sl = slice(j * LANES, (j + 1) * LANES)
        q_slab = _rope(q_ref[:, sl], cos_t, sin_t, first) * (HEAD_DIM ** -0.5)
        outs = []
        for half in range(2):
            sel = lo if half == 0 else jnp.logical_not(lo)
            sink = sink_ref[ATTN_HEAD_ORDER[2 * j + half]]
            q_m = jnp.where(sel, q_slab, 0.0).astype(BF16)
            s = jnp.where(visible, _dot(q_m, kt), -jnp.inf)
            mx = jnp.maximum(jnp.max(s, axis=-1, keepdims=True), sink)
            pexp = jnp.exp(s - mx)
            denom = jnp.sum(pexp, axis=-1, keepdims=True) + jnp.exp(sink - mx)
            outs.append(_dot(pexp.astype(BF16), vv) / denom)
        out_ref[:, sl] = jnp.where(lo, outs[0], outs[1]).astype(out_ref.dtype)

    kt_ref[:, 0:CHUNK] = kt_ref[:, CHUNK:2 * CHUNK]
    vv_ref[0:CHUNK, :] = vv_ref[CHUNK:2 * CHUNK, :]


def _swa(proj3, cos_t, sin_t, sinks):
    b, s, _ = proj3.shape
    blk = lambda w, off: pl.BlockSpec((None, CHUNK, w), lambda i, c: (i, c, off // w))
    tab = pl.BlockSpec((None, CHUNK, LANES), lambda i, c: (i, c, 0))
    return pl.pallas_call(
        _swa_kernel,
        grid=(b, s // CHUNK),
        in_specs=[pl.BlockSpec(memory_space=pltpu.SMEM),
                  blk(ATTN_W, AQ_OFF), blk(ATTN_KV_W, AK_OFF), blk(ATTN_KV_W, AV_OFF), tab, tab],
        out_specs=pl.BlockSpec((None, CHUNK, ATTN_W), lambda i, c: (i, c, 0)),
        out_shape=jax.ShapeDtypeStruct((b, s, ATTN_W), BF16),
        scratch_shapes=[pltpu.VMEM((LANES, 2 * CHUNK), BF16), pltpu.VMEM((2 * CHUNK, LANES), BF16)],
        compiler_params=_params("parallel", "arbitrary"),
        name="swa",
    )(sinks, proj3, proj3, proj3, cos_t, sin_t)


def _sgu_kernel(u_ref, v_ref, w_ref, bias_ref, g_ref, b_ref, out_ref):
    lo = _lane_lo((CHUNK, LANES))
    for j in range(SGU_GROUPS // 2):
        sl = slice(j * LANES, (j + 1) * LANES)
        u = jax.nn.gelu(u_ref[:, sl])
        v = jax.nn.gelu(v_ref[:, sl])
        vn = (_half_layer_norm(v, lo) * g_ref[:, sl] + b_ref[:, sl]).astype(BF16)
        mixed = jnp.where(lo, _dot(w_ref[2 * j], vn), _dot(w_ref[2 * j + 1], vn)) + bias_ref[:, sl]
        out_ref[:, sl] = (u * mixed).astype(out_ref.dtype)


def _sgu(proj3, w_tril, bias_tok, norm_g, norm_b):
    b, s, _ = proj3.shape
    blk = lambda w, off: pl.BlockSpec((None, CHUNK, w), lambda i, c: (i, c, off // w))
    const2 = lambda shape: pl.BlockSpec(shape, lambda i, c: (0,) * len(shape))
    return pl.pallas_call(
        _sgu_kernel,
        grid=(b, s // CHUNK),
        in_specs=[blk(SGU_W, SU_OFF), blk(SGU_W, SV_OFF), const2((SGU_GROUPS, CHUNK, CHUNK)),
                  const2((CHUNK, SGU_W)), const2((1, SGU_W)), const2((1, SGU_W))],
        out_specs=pl.BlockSpec((None, CHUNK, SGU_W), lambda i, c: (i, c, 0)),
        out_shape=jax.ShapeDtypeStruct((b, s, SGU_W), BF16),
        compiler_params=_params("parallel", "parallel"),
        name="sgu",
    )(proj3, proj3, w_tril, bias_tok, norm_g, norm_b)


def _outproj_kernel(ha_ref, hb_ref, hc_ref, x_ref, wa_ref, wb_ref, wc_ref, g_ref, b_ref, o_ref):
    mix = _dot(ha_ref[...], wa_ref[...]) + _dot(hb_ref[...], wb_ref[...]) + _dot(hc_ref[...], wc_ref[...])
    o_ref[...] = _layer_norm_rows(DN_ALPHA * x_ref[...] + mix, g_ref[...], b_ref[...])


def _outproj_ln(ha, hb, hc, x2d, wa, wb, wc, g, b):
    n = x2d.shape[0]
    rows = lambda w: pl.BlockSpec((ROW_TILE, w), lambda i: (i, 0))
    const = lambda shape: pl.BlockSpec(shape, lambda i: (0, 0))
    return pl.pallas_call(
        _outproj_kernel,
        grid=(n // ROW_TILE,),
        in_specs=[rows(MLSTM_W), rows(ATTN_W), rows(SGU_W), rows(D_MODEL),
                  const((MLSTM_W, D_MODEL)), const((ATTN_W, D_MODEL)), const((SGU_W, D_MODEL)),
                  const((1, D_MODEL)), const((1, D_MODEL))],
        out_specs=rows(D_MODEL),
        out_shape=jax.ShapeDtypeStruct((n, D_MODEL), F32),
        compiler_params=_params("parallel"),
        name="outproj_ln",
    )(ha, hb, hc, x2d, wa, wb, wc, g, b)


def _ffn_kernel(x_ref, wg_ref, wu_ref, wd_ref, g_ref, b_ref, o_ref, acc_ref):
    j = pl.program_id(1)
    xb = x_ref[...].astype(BF16)
    hidden = (jax.nn.silu(_dot(xb, wg_ref[...])) * _dot(xb, wu_ref[...])).astype(BF16)
    part = _dot(hidden, wd_ref[...])

    @pl.when(j == 0)
    def _():
        acc_ref[...] = part

    @pl.when(j > 0)
    def _():
        acc_ref[...] += part

    @pl.when(j == pl.num_programs(1) - 1)
    def _():
        o_ref[...] = _layer_norm_rows(DN_ALPHA * x_ref[...] + acc_ref[...], g_ref[...], b_ref[...])


def _ffn_ln(x2d, wg, wu, wd, g, b):
    n = x2d.shape[0]
    d_ff = wg.shape[1]
    tm = FFN_ROWS
    return pl.pallas_call(
        _ffn_kernel,
        grid=(n // tm, d_ff // FF_TILE),
        in_specs=[pl.BlockSpec((tm, D_MODEL), lambda i, j: (i, 0)),
                  pl.BlockSpec((D_MODEL, FF_TILE), lambda i, j: (0, j)),
                  pl.BlockSpec((D_MODEL, FF_TILE), lambda i, j: (0, j)),
                  pl.BlockSpec((FF_TILE, D_MODEL), lambda i, j: (j, 0)),
                  pl.BlockSpec((1, D_MODEL), lambda i, j: (0, 0)),
                  pl.BlockSpec((1, D_MODEL), lambda i, j: (0, 0))],
        out_specs=pl.BlockSpec((tm, D_MODEL), lambda i, j: (i, 0)),
        out_shape=jax.ShapeDtypeStruct((n, D_MODEL), F32),
        scratch_shapes=[pltpu.VMEM((tm, D_MODEL), F32)],
        compiler_params=_params("parallel", "arbitrary"),
        name="ffn_ln",
    )(x2d, wg, wu, wd, g, b)


def _router_kernel(x_ref, wh_ref, wl_ref, b_ref, o_ref):
    x = x_ref[...]
    xh = x.astype(BF16)
    xl = (x - xh.astype(F32)).astype(BF16)
    logits = _dot(xh, wh_ref[...]) + _dot(xl, wh_ref[...]) + _dot(xh, wl_ref[...]) + b_ref[...]
    lane = lax.broadcasted_iota(jnp.int32, logits.shape, 1)
    logits = jnp.where(lane < N_EXPERTS, logits, -jnp.inf)
    m1 = jnp.max(logits, axis=-1, keepdims=True)
    i1 = jnp.min(jnp.where(logits == m1, lane, LANES), axis=-1, keepdims=True)
    rest = jnp.where(lane == i1, -jnp.inf, logits)
    m2 = jnp.max(rest, axis=-1, keepdims=True)
    i2 = jnp.min(jnp.where(rest == m2, lane, LANES), axis=-1, keepdims=True)
    e2 = jnp.exp(m2 - m1)
    g1 = 1.0 / (1.0 + e2)
    g2 = e2 / (1.0 + e2)
    o_ref[...] = jnp.where(lane == 0, g1, jnp.where(lane == 1, g2, jnp.where(
        lane == 2, i1.astype(F32), jnp.where(lane == 3, i2.astype(F32), 0.0))))


def _router(x2d, w_hi, w_lo, b):
    n = x2d.shape[0]
    const = lambda shape: pl.BlockSpec(shape, lambda i: (0, 0))
    return pl.pallas_call(
        _router_kernel,
        grid=(n // ROW_TILE,),
        in_specs=[pl.BlockSpec((ROW_TILE, D_MODEL), lambda i: (i, 0)),
                  const((D_MODEL, LANES)), const((D_MODEL, LANES)), const((1, LANES))],
        out_specs=pl.BlockSpec((ROW_TILE, LANES), lambda i: (i, 0)),
        out_shape=jax.ShapeDtypeStruct((n, LANES), F32),
        compiler_params=_params("parallel"),
        name="router",
    )(x2d, w_hi, w_lo, b)


def _dispatch_kernel(slot_ref, ends_ref, x_ref, o_hbm, stage_ref, zero_ref, sem, zero_sem):
    t = pl.program_id(0)
    last = pl.num_programs(0) - 1
    s = t % 2

    def wait_tile(ss):
        for _ in range(TOP_K):
            pltpu.make_async_copy(stage_ref.at[ss], o_hbm.at[pl.ds(0, MOE_SUB), :], sem.at[ss]).wait()

    @pl.when(t == 0)
    def _():
        zero_ref[...] = jnp.zeros_like(zero_ref)

        def zero_sub_tile(m, carry):
            dst = o_hbm.at[pl.ds(pl.multiple_of(m * MOE_SUB, MOE_SUB), MOE_SUB), :]
            cp = pltpu.make_async_copy(zero_ref, dst, zero_sem)
            cp.start()
            cp.wait()
            return carry

        for e in range(N_EXPERTS):
            @pl.when(ends_ref[e] >= MOE_SUB)
            def _():
                zero_sub_tile(ends_ref[e] // MOE_SUB - 1, 0)

        lax.fori_loop(ends_ref[N_EXPERTS - 1] // MOE_SUB, o_hbm.shape[0] // MOE_SUB, zero_sub_tile, 0)

    @pl.when(t >= 2)
    def _():
        wait_tile(s)

    stage_ref[s] = x_ref[...]

    def issue(r, carry):
        base = (t * MOE_SUB + r) * TOP_K
        for k in range(TOP_K):
            pltpu.make_async_copy(stage_ref.at[s, pl.ds(r, 1), :],
                                  o_hbm.at[pl.ds(slot_ref[base + k], 1), :], sem.at[s]).start()
        return carry

    lax.fori_loop(0, MOE_SUB, issue, 0, unroll=8)

    @pl.when(t == last)
    def _():
        wait_tile(s)
        wait_tile(1 - s)


def _dispatch_rows(x2d, slots, ends, n_rows):
    n = x2d.shape[0]
    assert n // MOE_SUB >= 2
    return pl.pallas_call(
        _dispatch_kernel,
        grid_spec=pltpu.PrefetchScalarGridSpec(
            num_scalar_prefetch=2,
            grid=(n // MOE_SUB,),
            in_specs=[pl.BlockSpec((MOE_SUB, D_MODEL), lambda t, sl, en: (t, 0))],
            out_specs=pl.BlockSpec(memory_space=pl.ANY),
            scratch_shapes=[pltpu.VMEM((2, MOE_SUB, D_MODEL), F32), pltpu.VMEM((MOE_SUB, D_MODEL), F32),
                            pltpu.SemaphoreType.DMA((2,)), pltpu.SemaphoreType.DMA(())]),
        out_shape=jax.ShapeDtypeStruct((n_rows, D_MODEL), F32),
        compiler_params=_params("arbitrary"),
        name="moe_dispatch",
    )(slots, ends, x2d)


def _moe_kernel(exp_ref, row0_ref, nsub_ref, tail_ref, x_hbm, wg_ref, wu_ref, wd_ref, y_hbm,
                xb_ref, acc_ref, wgb_ref, wub_ref, wdb_ref, stage_ref, in_sem, out_sem):
    v = pl.program_id(0)
    j = pl.program_id(1)
    last = pl.num_programs(1) - 1
    n_sub = nsub_ref[v]
    row0 = row0_ref[v]

    def sub_rows(m):
        return pl.ds(pl.multiple_of(m * MOE_SUB, MOE_SUB), MOE_SUB)

    def hbm_rows(m):
        return pl.ds(pl.multiple_of(row0 + m * MOE_SUB, MOE_SUB), MOE_SUB)

    @pl.when(jnp.logical_and(v == 0, j == 0))
    def _():
        stage_ref[0] = jnp.zeros((MOE_SUB, D_MODEL), F32)

        def zero_sub_tile(m, carry):
            cp = pltpu.make_async_copy(stage_ref.at[0], y_hbm.at[sub_rows(m), :], out_sem)
            cp.start()
            cp.wait()
            return carry

        lax.fori_loop(tail_ref[0], y_hbm.shape[0] // MOE_SUB, zero_sub_tile, 0)

    @pl.when(n_sub > 0)
    def _():
        @pl.when(j == 0)
        def _():
            def x_copy(m):
                return pltpu.make_async_copy(x_hbm.at[hbm_rows(m), :], stage_ref.at[m % 2], in_sem.at[m % 2])

            x_copy(0).start()

            def load(m, carry):
                @pl.when(m + 1 < n_sub)
                def _():
                    x_copy(m + 1).start()

                x_copy(m).wait()
                xb_ref[sub_rows(m), :] = stage_ref[m % 2].astype(BF16)
                return carry

            lax.fori_loop(0, n_sub, load, 0)

        wgb_ref[...] = wg_ref[...].astype(BF16)
        wub_ref[...] = wu_ref[...].astype(BF16)
        wdb_ref[...] = wd_ref[...].astype(BF16)

        def ffn_tile(m):
            xs = xb_ref[sub_rows(m), :]
            hidden = (jax.nn.silu(_dot(xs, wgb_ref[...])) * _dot(xs, wub_ref[...])).astype(BF16)
            return _dot(hidden, wdb_ref[...])

        def y_copy(m):
            return pltpu.make_async_copy(acc_ref.at[sub_rows(m), :], y_hbm.at[hbm_rows(m), :], out_sem)

        @pl.when(j == 0)
        def _():
            def body(m, carry):
                acc_ref[sub_rows(m), :] = ffn_tile(m)
                return carry

            lax.fori_loop(0, n_sub, body, 0)

        @pl.when(jnp.logical_and(j > 0, j < last))
        def _():
            def body(m, carry):
                acc_ref[sub_rows(m), :] += ffn_tile(m)
                return carry

            lax.fori_loop(0, n_sub, body, 0)

        @pl.when(j == last)
        def _():
            def body(m, carry):
                acc_ref[sub_rows(m), :] += ffn_tile(m)
                y_copy(m).start()
                return carry

            def drain(m, carry):
                y_copy(m).wait()
                return carry

            lax.fori_loop(0, n_sub, body, 0)
            lax.fori_loop(0, n_sub, drain, 0)


def _moe_grouped(xs, wg, wu, wd, visit_exp, visit_row0, visit_nsub, tail_sub):
    n_rows = xs.shape[0]
    n_visits = visit_exp.shape[0]
    d_ff = wg.shape[2]
    assert d_ff // MOE_FF_TILE >= 2
    rows = MOE_VISIT_SUBS * MOE_SUB
    w_in = lambda v, j, e, r, ns, tl: (e[v], 0, j)
    w_out = lambda v, j, e, r, ns, tl: (e[v], j, 0)
    return pl.pallas_call(
        _moe_kernel,
        grid_spec=pltpu.PrefetchScalarGridSpec(
            num_scalar_prefetch=4,
            grid=(n_visits, d_ff // MOE_FF_TILE),
            in_specs=[pl.BlockSpec(memory_space=pl.ANY),
                      pl.BlockSpec((None, D_MODEL, MOE_FF_TILE), w_in),
                      pl.BlockSpec((None, D_MODEL, MOE_FF_TILE), w_in),
                      pl.BlockSpec((None, MOE_FF_TILE, D_MODEL), w_out)],
            out_specs=pl.BlockSpec(memory_space=pl.ANY),
            scratch_shapes=[pltpu.VMEM((rows, D_MODEL), BF16), pltpu.VMEM((rows, D_MODEL), F32),
                            pltpu.VMEM((D_MODEL, MOE_FF_TILE), BF16), pltpu.VMEM((D_MODEL, MOE_FF_TILE), BF16),
                            pltpu.VMEM((MOE_FF_TILE, D_MODEL), BF16),
                            pltpu.VMEM((2, MOE_SUB, D_MODEL), F32),
                            pltpu.SemaphoreType.DMA((2,)), pltpu.SemaphoreType.DMA(())]),
        out_shape=jax.ShapeDtypeStruct((n_rows, D_MODEL), F32),
        compiler_params=_params("arbitrary", "arbitrary"),
        name="moe_grouped",
    )(visit_exp, visit_row0, visit_nsub, tail_sub, xs, wg, wu, wd)


def _combine_kernel(slot_ref, x_ref, gate_ref, y_hbm, g_ref, b_ref, o_ref, buf_ref, sem):
    t = pl.program_id(0)
    s = t % 2

    def issue_tile(tt, ss):
        def issue(r, carry):
            base = (tt * MOE_SUB + r) * TOP_K
            for k in range(TOP_K):
                pltpu.make_async_copy(y_hbm.at[pl.ds(slot_ref[base + k], 1), :],
                                      buf_ref.at[ss, k, pl.ds(r, 1), :], sem.at[ss]).start()
            return carry

        lax.fori_loop(0, MOE_SUB, issue, 0, unroll=8)

    @pl.when(t == 0)
    def _():
        issue_tile(0, 0)

    @pl.when(t + 1 < pl.num_programs(0))
    def _():
        issue_tile(t + 1, 1 - s)

    for k in range(TOP_K):
        pltpu.make_async_copy(y_hbm.at[pl.ds(0, MOE_SUB), :], buf_ref.at[s, k], sem.at[s]).wait()
    gate = gate_ref[...]
    ff = gate[:, 0:1] * buf_ref[s, 0] + gate[:, 1:2] * buf_ref[s, 1]
    o_ref[...] = _layer_norm_rows(DN_ALPHA * x_ref[...] + ff, g_ref[...], b_ref[...])


def _combine_ln(x2d, route, slots, ys, g, b):
    n = x2d.shape[0]
    return pl.pallas_call(
        _combine_kernel,
        grid_spec=pltpu.PrefetchScalarGridSpec(
            num_scalar_prefetch=1,
            grid=(n // MOE_SUB,),
            in_specs=[pl.BlockSpec((MOE_SUB, D_MODEL), lambda t, s: (t, 0)),
                      pl.BlockSpec((MOE_SUB, LANES), lambda t, s: (t, 0)),
                      pl.BlockSpec(memory_space=pl.ANY),
                      pl.BlockSpec((1, D_MODEL), lambda t, s: (0, 0)),
                      pl.BlockSpec((1, D_MODEL), lambda t, s: (0, 0))],
            out_specs=pl.BlockSpec((MOE_SUB, D_MODEL), lambda t, s: (t, 0)),
            scratch_shapes=[pltpu.VMEM((2, TOP_K, MOE_SUB, D_MODEL), F32), pltpu.SemaphoreType.DMA((2,))]),
        out_shape=jax.ShapeDtypeStruct((n, D_MODEL), F32),
        compiler_params=_params("arbitrary"),
        name="moe_combine_ln",
    )(slots, x2d, route, ys, g, b)


def _routing_tables(route, n):
    idx = route[:, 2:4].astype(jnp.int32)
    onehot = (idx[:, :, None] == jnp.arange(N_EXPERTS, dtype=jnp.int32)).astype(jnp.int32).sum(1)
    rank = jnp.cumsum(onehot, axis=0) - onehot
    counts = jnp.sum(onehot, axis=0)
    padded = ((counts + MOE_SUB - 1) // MOE_SUB) * MOE_SUB
    ends = jnp.cumsum(padded)
    starts = ends - padded
    slot = starts[idx] + jnp.take_along_axis(rank, idx, axis=1)

    n_rows = -(-(n * TOP_K + N_EXPERTS * (MOE_SUB - 1)) // MOE_SUB) * MOE_SUB

    visit_rows = MOE_VISIT_SUBS * MOE_SUB
    max_chunks = -(-n_rows // visit_rows)
    chunk = jnp.arange(max_chunks, dtype=jnp.int32)[None, :]
    left = padded[:, None] - chunk * visit_rows
    valid = (left > 0).reshape(-1)
    n_visits = n_rows // visit_rows + N_EXPERTS
    order = jnp.argsort(jnp.logical_not(valid), stable=True)[:n_visits]
    n_valid = jnp.sum(valid.astype(jnp.int32))
    live = jnp.arange(n_visits) < n_valid
    order = jnp.where(live, order, order[jnp.maximum(n_valid - 1, 0)])
    v_exp = (order // max_chunks).astype(jnp.int32)
    v_chunk = (order % max_chunks).astype(jnp.int32)
    v_row0 = jnp.where(live, starts[v_exp] + v_chunk * visit_rows, 0).astype(jnp.int32)
    v_nsub = jnp.where(live, jnp.minimum(left.reshape(-1)[order], visit_rows) // MOE_SUB, 0).astype(jnp.int32)
    tail_sub = (ends[N_EXPERTS - 1:] // MOE_SUB).astype(jnp.int32)
    return slot.reshape(-1).astype(jnp.int32), ends.astype(jnp.int32), n_rows, (v_exp, v_row0, v_nsub, tail_sub)


def _moe_ln(x2d, w_router, b_router, wg, wu, wd, g, b):
    n = x2d.shape[0]
    w_pad = jnp.zeros((D_MODEL, LANES), F32).at[:, :N_EXPERTS].set(w_router)
    w_hi = w_pad.astype(BF16)
    w_lo = (w_pad - w_hi.astype(F32)).astype(BF16)
    b_pad = jnp.zeros((1, LANES), F32).at[0, :N_EXPERTS].set(b_router)
    route = _router(x2d, w_hi, w_lo, b_pad)
    slots, ends, n_rows, visits = _routing_tables(route, n)
    xs = _dispatch_rows(x2d, slots, ends, n_rows)
    ys = _moe_grouped(xs, wg, wu, wd, *visits)
    return _combine_ln(x2d, route, slots, ys, g, b)


def _layout_in_proj(w_in, b_in):
    sizes = (MLSTM_W, MLSTM_W, MLSTM_W, MLSTM_W, MLSTM_HEADS, MLSTM_HEADS,
             ATTN_W, ATTN_KV_W, ATTN_KV_W, SGU_W, SGU_W)
    offs = np.concatenate([[0], np.cumsum(sizes)])
    seg = lambda a, i: a[..., int(offs[i]):int(offs[i + 1])]

    def build(a):
        lead = a.shape[:-1]
        aq = seg(a, 6).reshape(lead + (ATTN_Q_HEADS, HEAD_DIM))[..., np.array(ATTN_HEAD_ORDER), :]
        gates = jnp.concatenate([seg(a, 4), seg(a, 5),
                                 jnp.zeros(lead + (LANES - 2 * MLSTM_HEADS,), a.dtype)], -1)
        return jnp.concatenate([seg(a, 0), seg(a, 1), seg(a, 2), seg(a, 3), aq.reshape(lead + (ATTN_W,)),
                                gates, seg(a, 7), seg(a, 8), seg(a, 9), seg(a, 10)], -1)

    return build(w_in).astype(BF16), build(b_in)[None, :]


def _rope_tables(positions):
    inv_freq = ROPE_THETA ** (-jnp.arange(0, ROPE_DIM, 2, dtype=F32) / ROPE_DIM)
    ang = positions.astype(F32)[..., None] * inv_freq
    cos, sin = jnp.cos(ang), jnp.sin(ang)
    half = ROPE_DIM // 2
    ones = jnp.ones(ang.shape[:-1] + (HEAD_DIM - ROPE_DIM,), F32)
    cos_head = jnp.concatenate([cos, cos, ones], -1)
    sin_head = jnp.concatenate([-sin, sin, 0.0 * ones], -1)
    del half
    return jnp.concatenate([cos_head, cos_head], -1), jnp.concatenate([sin_head, sin_head], -1)


def kernel(x, positions, w_in, b_in, conv_w, mlstm_norm_g, attn_sinks, sgu_w_s, sgu_b_s, sgu_norm_g, sgu_norm_b, w_out, ln1_g, ln1_b, ln2_g, ln2_b, ffn_w_gate, ffn_w_up, ffn_w_down, moe_w_router, moe_b_router, moe_w_gate, moe_w_up, moe_w_down):
    bsz, seq, _ = x.shape
    n = bsz * seq
    cos_t, sin_t = _rope_tables(positions)
    tril = jnp.tril(jnp.ones((CHUNK, CHUNK), bool))
    x2d = x.reshape(n, D_MODEL)
    for layer in range(DEPTH):
        w_p, b_p = _layout_in_proj(w_in[layer], b_in[layer])
        proj3 = _inproj(x2d, w_p, b_p).reshape(bsz, seq, D_PROJ_PAD)
        h_a = _mlstm(proj3, conv_w[layer], mlstm_norm_g[layer][None, :])
        h_b = _swa(proj3, cos_t, sin_t, attn_sinks[layer])
        w_tril = jnp.where(tril, sgu_w_s[layer], 0.0).astype(BF16)
        bias_tok = jnp.repeat(sgu_b_s[layer].T, HEAD_DIM, axis=1)
        h_c = _sgu(proj3, w_tril, bias_tok, sgu_norm_g[layer][None, :], sgu_norm_b[layer][None, :])
        wo = w_out[layer]
        wa = wo[:MLSTM_W].astype(BF16)
        wb = wo[MLSTM_W:MLSTM_W + ATTN_W].reshape(ATTN_Q_HEADS, HEAD_DIM, D_MODEL)[np.array(ATTN_HEAD_ORDER)]
        wb = wb.reshape(ATTN_W, D_MODEL).astype(BF16)
        wc = wo[MLSTM_W + ATTN_W:].astype(BF16)
        x2d = _outproj_ln(h_a.reshape(n, MLSTM_W), h_b.reshape(n, ATTN_W), h_c.reshape(n, SGU_W), x2d,
                          wa, wb, wc, ln1_g[layer][None, :], ln1_b[layer][None, :])
        j = layer // 2
        g2, b2 = ln2_g[layer][None, :], ln2_b[layer][None, :]
        if layer % 2 == 0:
            x2d = _ffn_ln(x2d, ffn_w_gate[j].astype(BF16), ffn_w_up[j].astype(BF16),
                          ffn_w_down[j].astype(BF16), g2, b2)
        else:
            x2d = _moe_ln(x2d, moe_w_router[j], moe_b_router[j], moe_w_gate[j], moe_w_up[j],
                          moe_w_down[j], g2, b2)
    return x2d.reshape(bsz, seq, D_MODEL)
```

```python
import functools

import jax
import jax.numpy as jnp
import numpy as np
from jax import lax
from jax.experimental import pallas as pl
from jax.experimental.pallas import tpu as pltpu

F32 = jnp.float32
BF16 = jnp.bfloat16

D_MODEL = 1024
HEAD_DIM = 64
LANES = 128
MLSTM_HEADS = 6
ATTN_Q_HEADS = 6
ATTN_KV_HEADS = 2
SGU_GROUPS = 4
MLSTM_W = MLSTM_HEADS * HEAD_DIM
ATTN_W = ATTN_Q_HEADS * HEAD_DIM
ATTN_KV_W = ATTN_KV_HEADS * HEAD_DIM
SGU_W = SGU_GROUPS * HEAD_DIM
CHUNK = 128
CONV_WIDTH = 4
ROPE_DIM = HEAD_DIM // 4
ROPE_THETA = 500000.0
N_EXPERTS = 8
TOP_K = 2
DEPTH = 2
DN_ALPHA = (2.0 * DEPTH) ** 0.25
LN_EPS = 1e-5

QK_OFF, V_OFF, O_OFF, AQ_OFF = 0, 768, 1152, 1536
GATE_OFF, AK_OFF, AV_OFF, SU_OFF, SV_OFF = 1920, 2048, 2176, 2304, 2560
D_PROJ_PAD = 2816
ATTN_HEAD_ORDER = (0, 3, 1, 4, 2, 5)

VMEM_LIMIT = 56 * 1024 * 1024

ROW_TILE = 512
FFN_ROWS = 1024
FF_TILE = 256
MOE_SUB = 512
MOE_VISIT_SUBS = 9
MOE_FF_TILE = 512


def _params(*sem):
    return pltpu.CompilerParams(dimension_semantics=sem, vmem_limit_bytes=VMEM_LIMIT)


def _lane_lo(shape):
    return lax.broadcasted_iota(jnp.int32, shape, len(shape) - 1) < HEAD_DIM


def _layer_norm_rows(z, g, b):
    mu = jnp.mean(z, axis=-1, keepdims=True)
    zc = z - mu
    var = jnp.mean(zc * zc, axis=-1, keepdims=True)
    return zc * lax.rsqrt(var + LN_EPS) * g + b


def _half_layer_norm(x, lo):
    inv = 1.0 / HEAD_DIM
    s_lo = jnp.sum(jnp.where(lo, x, 0.0), axis=-1, keepdims=True)
    s_all = jnp.sum(x, axis=-1, keepdims=True)
    mu = jnp.where(lo, s_lo, s_all - s_lo) * inv
    xc = x - mu
    sq = xc * xc
    q_lo = jnp.sum(jnp.where(lo, sq, 0.0), axis=-1, keepdims=True)
    q_all = jnp.sum(sq, axis=-1, keepdims=True)
    var = jnp.where(lo, q_lo, q_all - q_lo) * inv
    return xc * lax.rsqrt(var + LN_EPS)


def _split3(a):
    h1 = a.astype(BF16)
    r1 = a - h1.astype(F32)
    h2 = r1.astype(BF16)
    r2 = r1 - h2.astype(F32)
    return h1, h2, r2.astype(BF16)


def _dot(a, b):
    return jnp.dot(a, b, preferred_element_type=F32)


def _inproj_kernel(x_ref, w_ref, b_ref, o_ref):
    o_ref[...] = _dot(x_ref[...].astype(BF16), w_ref[...]) + b_ref[...]


def _inproj(x2d, w, b):
    n = x2d.shape[0]
    return pl.pallas_call(
        _inproj_kernel,
        grid=(n // ROW_TILE,),
        in_specs=[pl.BlockSpec((ROW_TILE, D_MODEL), lambda i: (i, 0)),
                  pl.BlockSpec((D_MODEL, D_PROJ_PAD), lambda i: (0, 0)),
                  pl.BlockSpec((1, D_PROJ_PAD), lambda i: (0, 0))],
        out_specs=pl.BlockSpec((ROW_TILE, D_PROJ_PAD), lambda i: (i, 0)),
        out_shape=jax.ShapeDtypeStruct((n, D_PROJ_PAD), F32),
        compiler_params=_params("parallel"),
        name="inproj",
    )(x2d, w, b)


def _mlstm_kernel(qk_ref, v_ref, o_ref, gate_ref, cw_ref, g_ref, out_ref, buf_ref, ct_ref, m_ref):
    c = pl.program_id(1)

    @pl.when(c == 0)
    def _():
        buf_ref[0:8, :] = jnp.zeros((8, 2 * MLSTM_W), F32)
        ct_ref[...] = jnp.zeros_like(ct_ref)
        m_ref[...] = jnp.zeros_like(m_ref)

    buf_ref[8:8 + CHUNK, :] = qk_ref[...]
    cw = cw_ref[...]
    base = 8 - (CONV_WIDTH - 1)
    acc = cw[0:1, :] * buf_ref[base:base + CHUNK, :]
    for j in range(1, CONV_WIDTH):
        acc = acc + cw[j:j + 1, :] * buf_ref[base + j:base + j + CHUNK, :]
    buf_ref[0:8, :] = qk_ref[CHUNK - 8:CHUNK, :]
    qk = jax.nn.silu(acc)

    gates = gate_ref[...]
    lf = jax.nn.log_sigmoid(gates)
    row = lax.broadcasted_iota(jnp.int32, (CHUNK, CHUNK), 0)
    col = lax.broadcasted_iota(jnp.int32, (CHUNK, CHUNK), 1)
    causal = col <= row
    tri = causal.astype(BF16)
    l1, l2, l3 = _split3(lf)
    cum = _dot(tri, l1) + _dot(tri, l2) + _dot(tri, l3)
    gates_t = gates.T
    cum_t = cum.T

    lo = _lane_lo((CHUNK, LANES))
    for p in range(MLSTM_HEADS // 2):
        sl = slice(p * LANES, (p + 1) * LANES)
        q_slab = qk[:, sl] * (HEAD_DIM ** -0.5)
        k_slab = qk[:, MLSTM_W + p * LANES:MLSTM_W + (p + 1) * LANES]
        kt_slab = k_slab.T
        v_slab = v_ref[:, sl]
        ct_pair = ct_ref[p]
        ct_pair_b = ct_pair.astype(BF16)
        halves = []
        new_ct = []
        for half in range(2):
            h = 2 * p + half
            sel = lo if half == 0 else jnp.logical_not(lo)
            li_row = gates_t[h:h + 1, :]
            bc_row = cum_t[MLSTM_HEADS + h:MLSTM_HEADS + h + 1, :]
            bc_col = cum[:, MLSTM_HEADS + h:MLSTM_HEADS + h + 1]
            b_tot = bc_row[:, CHUNK - 1:CHUNK]
            m_prev = m_ref[h:h + 1, 0:1]

            dmat = jnp.where(causal, bc_col - bc_row + li_row, -jnp.inf)
            m_inter = bc_col + m_prev
            m_row = jnp.maximum(m_inter, jnp.max(dmat, axis=-1, keepdims=True))
            q_m = jnp.where(sel, q_slab, 0.0).astype(BF16)
            s = _dot(q_m, kt_slab.astype(BF16))
            pmat = (s * jnp.exp(dmat - m_row)).astype(BF16)
            w_inter = jnp.exp(m_inter - m_row)
            v_aug = jnp.where(sel, v_slab, 1.0).astype(BF16)
            r = _dot(pmat, v_aug) + w_inter * _dot(q_m, ct_pair_b)
            den_lane = HEAD_DIM if half == 0 else 0
            den = r[:, den_lane:den_lane + 1]
            halves.append(r / jnp.maximum(jnp.abs(den), jnp.exp(-m_row)))

            a_row = b_tot - bc_row + li_row
            m_loc = jnp.max(a_row, axis=-1, keepdims=True)
            w_row = jnp.exp(a_row - m_loc)
            kt_h = kt_slab[half * HEAD_DIM:(half + 1) * HEAD_DIM, :]
            ct_loc = _dot((kt_h * w_row).astype(BF16), v_aug)
            m_new = jnp.maximum(b_tot + m_prev, m_loc)
            s_old = jnp.exp(b_tot + m_prev - m_new)
            s_loc = jnp.exp(m_loc - m_new)
            new_ct.append(s_old * ct_pair[half * HEAD_DIM:(half + 1) * HEAD_DIM, :] + s_loc * ct_loc)
            m_ref[h:h + 1, :] = jnp.broadcast_to(m_new, (1, LANES))

        ct_ref[p] = jnp.concatenate(new_ct, axis=0)
        hh = jnp.where(lo, halves[0], halves[1])
        hn = _half_layer_norm(hh, lo) * g_ref[:, sl]
        out_ref[:, sl] = (hn * jax.nn.sigmoid(o_ref[:, sl])).astype(out_ref.dtype)


def _mlstm(proj3, conv_w, norm_g):
    b, s, _ = proj3.shape
    blk = lambda w, off: pl.BlockSpec((None, CHUNK, w), lambda i, c: (i, c, off // w))
    return pl.pallas_call(
        _mlstm_kernel,
        grid=(b, s // CHUNK),
        in_specs=[blk(2 * MLSTM_W, QK_OFF), blk(MLSTM_W, V_OFF), blk(MLSTM_W, O_OFF), blk(LANES, GATE_OFF),
                  pl.BlockSpec((CONV_WIDTH, 2 * MLSTM_W), lambda i, c: (0, 0)),
                  pl.BlockSpec((1, MLSTM_W), lambda i, c: (0, 0))],
        out_specs=pl.BlockSpec((None, CHUNK, MLSTM_W), lambda i, c: (i, c, 0)),
        out_shape=jax.ShapeDtypeStruct((b, s, MLSTM_W), BF16),
        scratch_shapes=[pltpu.VMEM((8 + CHUNK, 2 * MLSTM_W), F32),
                        pltpu.VMEM((MLSTM_HEADS // 2, LANES, LANES), F32),
                        pltpu.VMEM((8, LANES), F32)],
        compiler_params=_params("parallel", "arbitrary"),
        name="mlstm",
    )(proj3, proj3, proj3, proj3, conv_w, norm_g)


def _rope(x, cos_t, sin_t, first):
    return x * cos_t + jnp.where(first, pltpu.roll(x, LANES - ROPE_DIM // 2, 1),
                                 pltpu.roll(x, ROPE_DIM // 2, 1)) * sin_t


def _swa_kernel(sink_ref, q_ref, k_ref, v_ref, cos_ref, sin_ref, out_ref, kt_ref, vv_ref):
    c = pl.program_id(1)

    @pl.when(c == 0)
    def _():
        kt_ref[...] = jnp.zeros_like(kt_ref)
        vv_ref[...] = jnp.zeros_like(vv_ref)

    cos_t = cos_ref[...]
    sin_t = sin_ref[...]
    lane = lax.broadcasted_iota(jnp.int32, (CHUNK, LANES), 1)
    lo = lane < HEAD_DIM
    first = (lane % HEAD_DIM) < ROPE_DIM // 2

    k_cur = _rope(k_ref[...], cos_t, sin_t, first)
    kt_ref[:, CHUNK:2 * CHUNK] = k_cur.T.astype(BF16)
    vv_ref[CHUNK:2 * CHUNK, :] = v_ref[...].astype(BF16)
    kt = kt_ref[...]
    vv = vv_ref[...]

    row = lax.broadcasted_iota(jnp.int32, (CHUNK, 2 * CHUNK), 0)
    col = lax.broadcasted_iota(jnp.int32, (CHUNK, 2 * CHUNK), 1)
    first_block_shift = jnp.where(c > 0, 0, 2 * CHUNK)
    visible = jnp.logical_or(jnp.logical_and(col < CHUNK, col > row + first_block_shift),
                             jnp.logical_and(col >= CHUNK, col - CHUNK <= row))

    for j in range(ATTN_Q_HEADS // 2):
        sl = slice(j * LANES, (j + 1) * LANES)
        q_slab = _rope(q_ref[:, sl], cos_t, sin_t, first) * (HEAD_DIM ** -0.5)
        outs = []
        for half in range(2):
            sel = lo if half == 0 else jnp.logical_not(lo)
            sink = sink_ref[ATTN_HEAD_ORDER[2 * j + half]]
            q_m = jnp.where(sel, q_slab, 0.0).astype(BF16)
            s = jnp.where(visible, _dot(q_m, kt), -jnp.inf)
            mx = jnp.maximum(jnp.max(s, axis=-1, keepdims=True), sink)
            pexp = jnp.exp(s - mx)
            denom = jnp.sum(pexp, axis=-1, keepdims=True) + jnp.exp(sink - mx)
            outs.append(_dot(pexp.astype(BF16), vv) / denom)
        out_ref[:, sl] = jnp.where(lo, outs[0], outs[1]).astype(out_ref.dtype)

    kt_ref[:, 0:CHUNK] = kt_ref[:, CHUNK:2 * CHUNK]
    vv_ref[0:CHUNK, :] = vv_ref[CHUNK:2 * CHUNK, :]


def _swa(proj3, cos_t, sin_t, sinks):
    b, s, _ = proj3.shape
    blk = lambda w, off: pl.BlockSpec((None, CHUNK, w), lambda i, c: (i, c, off // w))
    tab = pl.BlockSpec((None, CHUNK, LANES), lambda i, c: (i, c, 0))
    return pl.pallas_call(
        _swa_kernel,
        grid=(b, s // CHUNK),
        in_specs=[pl.BlockSpec(memory_space=pltpu.SMEM),
                  blk(ATTN_W, AQ_OFF), blk(ATTN_KV_W, AK_OFF), blk(ATTN_KV_W, AV_OFF), tab, tab],
        out_specs=pl.BlockSpec((None, CHUNK, ATTN_W), lambda i, c: (i, c, 0)),
        out_shape=jax.ShapeDtypeStruct((b, s, ATTN_W), BF16),
        scratch_shapes=[pltpu.VMEM((LANES, 2 * CHUNK), BF16), pltpu.VMEM((2 * CHUNK, LANES), BF16)],
        compiler_params=_params("parallel", "arbitrary"),
        name="swa",
    )(sinks, proj3, proj3, proj3, cos_t, sin_t)


def _sgu_kernel(u_ref, v_ref, w_ref, bias_ref, g_ref, b_ref, out_ref):
    lo = _lane_lo((CHUNK, LANES))
    for j in range(SGU_GROUPS // 2):
        sl = slice(j * LANES, (j + 1) * LANES)
        u = jax.nn.gelu(u_ref[:, sl])
        v = jax.nn.gelu(v_ref[:, sl])
        vn = (_half_layer_norm(v, lo) * g_ref[:, sl] + b_ref[:, sl]).astype(BF16)
        mixed = jnp.where(lo, _dot(w_ref[2 * j], vn), _dot(w_ref[2 * j + 1], vn)) + bias_ref[:, sl]
        out_ref[:, sl] = (u * mixed).astype(out_ref.dtype)


def _sgu(proj3, w_tril, bias_tok, norm_g, norm_b):
    b, s, _ = proj3.shape
    blk = lambda w, off: pl.BlockSpec((None, CHUNK, w), lambda i, c: (i, c, off // w))
    const2 = lambda shape: pl.BlockSpec(shape, lambda i, c: (0,) * len(shape))
    return pl.pallas_call(
        _sgu_kernel,
        grid=(b, s // CHUNK),
        in_specs=[blk(SGU_W, SU_OFF), blk(SGU_W, SV_OFF), const2((SGU_GROUPS, CHUNK, CHUNK)),
                  const2((CHUNK, SGU_W)), const2((1, SGU_W)), const2((1, SGU_W))],
        out_specs=pl.BlockSpec((None, CHUNK, SGU_W), lambda i, c: (i, c, 0)),
        out_shape=jax.ShapeDtypeStruct((b, s, SGU_W), BF16),
        compiler_params=_params("parallel", "parallel"),
        name="sgu",
    )(proj3, proj3, w_tril, bias_tok, norm_g, norm_b)


def _outproj_kernel(ha_ref, hb_ref, hc_ref, x_ref, wa_ref, wb_ref, wc_ref, g_ref, b_ref, o_ref):
    mix = _dot(ha_ref[...], wa_ref[...]) + _dot(hb_ref[...], wb_ref[...]) + _dot(hc_ref[...], wc_ref[...])
    o_ref[...] = _layer_norm_rows(DN_ALPHA * x_ref[...] + mix, g_ref[...], b_ref[...])


def _outproj_ln(ha, hb, hc, x2d, wa, wb, wc, g, b):
    n = x2d.shape[0]
    rows = lambda w: pl.BlockSpec((ROW_TILE, w), lambda i: (i, 0))
    const = lambda shape: pl.BlockSpec(shape, lambda i: (0, 0))
    return pl.pallas_call(
        _outproj_kernel,
        grid=(n // ROW_TILE,),
        in_specs=[rows(MLSTM_W), rows(ATTN_W), rows(SGU_W), rows(D_MODEL),
                  const((MLSTM_W, D_MODEL)), const((ATTN_W, D_MODEL)), const((SGU_W, D_MODEL)),
                  const((1, D_MODEL)), const((1, D_MODEL))],
        out_specs=rows(D_MODEL),
        out_shape=jax.ShapeDtypeStruct((n, D_MODEL), F32),
        compiler_params=_params("parallel"),
        name="outproj_ln",
    )(ha, hb, hc, x2d, wa, wb, wc, g, b)


def _ffn_kernel(x_ref, wg_ref, wu_ref, wd_ref, g_ref, b_ref, o_ref, acc_ref):
    j = pl.program_id(1)
    xb = x_ref[...].astype(BF16)
    hidden = (jax.nn.silu(_dot(xb, wg_ref[...])) * _dot(xb, wu_ref[...])).astype(BF16)
    part = _dot(hidden, wd_ref[...])

    @pl.when(j == 0)
    def _():
        acc_ref[...] = part

    @pl.when(j > 0)
    def _():
        acc_ref[...] += part

    @pl.when(j == pl.num_programs(1) - 1)
    def _():
        o_ref[...] = _layer_norm_rows(DN_ALPHA * x_ref[...] + acc_ref[...], g_ref[...], b_ref[...])


def _ffn_ln(x2d, wg, wu, wd, g, b):
    n = x2d.shape[0]
    d_ff = wg.shape[1]
    tm = FFN_ROWS
    return pl.pallas_call(
        _ffn_kernel,
        grid=(n // tm, d_ff // FF_TILE),
        in_specs=[pl.BlockSpec((tm, D_MODEL), lambda i, j: (i, 0)),
                  pl.BlockSpec((D_MODEL, FF_TILE), lambda i, j: (0, j)),
                  pl.BlockSpec((D_MODEL, FF_TILE), lambda i, j: (0, j)),
                  pl.BlockSpec((FF_TILE, D_MODEL), lambda i, j: (j, 0)),
                  pl.BlockSpec((1, D_MODEL), lambda i, j: (0, 0)),
                  pl.BlockSpec((1, D_MODEL), lambda i, j: (0, 0))],
        out_specs=pl.BlockSpec((tm, D_MODEL), lambda i, j: (i, 0)),
        out_shape=jax.ShapeDtypeStruct((n, D_MODEL), F32),
        scratch_shapes=[pltpu.VMEM((tm, D_MODEL), F32)],
        compiler_params=_params("parallel", "arbitrary"),
        name="ffn_ln",
    )(x2d, wg, wu, wd, g, b)


def _router_kernel(x_ref, wh_ref, wl_ref, b_ref, o_ref):
    x = x_ref[...]
    xh = x.astype(BF16)
    xl = (x - xh.astype(F32)).astype(BF16)
    logits = _dot(xh, wh_ref[...]) + _dot(xl, wh_ref[...]) + _dot(xh, wl_ref[...]) + b_ref[...]
    lane = lax.broadcasted_iota(jnp.int32, logits.shape, 1)
    logits = jnp.where(lane < N_EXPERTS, logits, -jnp.inf)
    m1 = jnp.max(logits, axis=-1, keepdims=True)
    i1 = jnp.min(jnp.where(logits == m1, lane, LANES), axis=-1, keepdims=True)
    rest = jnp.where(lane == i1, -jnp.inf, logits)
    m2 = jnp.max(rest, axis=-1, keepdims=True)
    i2 = jnp.min(jnp.where(rest == m2, lane, LANES), axis=-1, keepdims=True)
    e2 = jnp.exp(m2 - m1)
    g1 = 1.0 / (1.0 + e2)
    g2 = e2 / (1.0 + e2)
    o_ref[...] = jnp.where(lane == 0, g1, jnp.where(lane == 1, g2, jnp.where(
        lane == 2, i1.astype(F32), jnp.where(lane == 3, i2.astype(F32), 0.0))))


def _router(x2d, w_hi, w_lo, b):
    n = x2d.shape[0]
    const = lambda shape: pl.BlockSpec(shape, lambda i: (0, 0))
    return pl.pallas_call(
        _router_kernel,
        grid=(n // ROW_TILE,),
        in_specs=[pl.BlockSpec((ROW_TILE, D_MODEL), lambda i: (i, 0)),
                  const((D_MODEL, LANES)), const((D_MODEL, LANES)), const((1, LANES))],
        out_specs=pl.BlockSpec((ROW_TILE, LANES), lambda i: (i, 0)),
        out_shape=jax.ShapeDtypeStruct((n, LANES), F32),
        compiler_params=_params("parallel"),
        name="router",
    )(x2d, w_hi, w_lo, b)


def _dispatch_kernel(slot_ref, ends_ref, x_ref, o_hbm, stage_ref, zero_ref, sem, zero_sem):
    t = pl.program_id(0)
    last = pl.num_programs(0) - 1
    s = t % 2

    def wait_tile(ss):
        for _ in range(TOP_K):
            pltpu.make_async_copy(stage_ref.at[ss], o_hbm.at[pl.ds(0, MOE_SUB), :], sem.at[ss]).wait()

    @pl.when(t == 0)
    def _():
        zero_ref[...] = jnp.zeros_like(zero_ref)

        def zero_sub_tile(m, carry):
            dst = o_hbm.at[pl.ds(pl.multiple_of(m * MOE_SUB, MOE_SUB), MOE_SUB), :]
            cp = pltpu.make_async_copy(zero_ref, dst, zero_sem)
            cp.start()
            cp.wait()
            return carry

        for e in range(N_EXPERTS):
            @pl.when(ends_ref[e] >= MOE_SUB)
            def _():
                zero_sub_tile(ends_ref[e] // MOE_SUB - 1, 0)

        lax.fori_loop(ends_ref[N_EXPERTS - 1] // MOE_SUB, o_hbm.shape[0] // MOE_SUB, zero_sub_tile, 0)

    @pl.when(t >= 2)
    def _():
        wait_tile(s)

    stage_ref[s] = x_ref[...]

    def issue(r, carry):
        base = (t * MOE_SUB + r) * TOP_K
        for k in range(TOP_K):
            pltpu.make_async_copy(stage_ref.at[s, pl.ds(r, 1), :],
                                  o_hbm.at[pl.ds(slot_ref[base + k], 1), :], sem.at[s]).start()
        return carry

    lax.fori_loop(0, MOE_SUB, issue, 0, unroll=8)

    @pl.when(t == last)
    def _():
        wait_tile(s)
        wait_tile(1 - s)


def _dispatch_rows(x2d, slots, ends, n_rows):
    n = x2d.shape[0]
    assert n // MOE_SUB >= 2
    return pl.pallas_call(
        _dispatch_kernel,
        grid_spec=pltpu.PrefetchScalarGridSpec(
            num_scalar_prefetch=2,
            grid=(n // MOE_SUB,),
            in_specs=[pl.BlockSpec((MOE_SUB, D_MODEL), lambda t, sl, en: (t, 0))],
            out_specs=pl.BlockSpec(memory_space=pl.ANY),
            scratch_shapes=[pltpu.VMEM((2, MOE_SUB, D_MODEL), F32), pltpu.VMEM((MOE_SUB, D_MODEL), F32),
                            pltpu.SemaphoreType.DMA((2,)), pltpu.SemaphoreType.DMA(())]),
        out_shape=jax.ShapeDtypeStruct((n_rows, D_MODEL), F32),
        compiler_params=_params("arbitrary"),
        name="moe_dispatch",
    )(slots, ends, x2d)


def _moe_kernel(exp_ref, row0_ref, nsub_ref, tail_ref, x_hbm, wg_ref, wu_ref, wd_ref, y_hbm,
                xb_ref, acc_ref, wgu_ref, wdb_ref, stage_ref, in_sem, out_sem):
    v = pl.program_id(0)
    j = pl.program_id(1)
    last = pl.num_programs(1) - 1
    n_sub = nsub_ref[v]
    row0 = row0_ref[v]

    def sub_rows(m):
        return pl.ds(pl.multiple_of(m * MOE_SUB, MOE_SUB), MOE_SUB)

    def hbm_rows(m):
        return pl.ds(pl.multiple_of(row0 + m * MOE_SUB, MOE_SUB), MOE_SUB)

    @pl.when(jnp.logical_and(v == 0, j == 0))
    def _():
        stage_ref[0] = jnp.zeros((MOE_SUB, D_MODEL), F32)

        def zero_sub_tile(m, carry):
            cp = pltpu.make_async_copy(stage_ref.at[0], y_hbm.at[sub_rows(m), :], out_sem)
            cp.start()
            cp.wait()
            return carry

        lax.fori_loop(tail_ref[0], y_hbm.shape[0] // MOE_SUB, zero_sub_tile, 0)

    @pl.when(n_sub > 0)
    def _():
        @pl.when(j == 0)
        def _():
            def x_copy(m):
                return pltpu.make_async_copy(x_hbm.at[hbm_rows(m), :], stage_ref.at[m % 2], in_sem.at[m % 2])

            x_copy(0).start()

            def load(m, carry):
                @pl.when(m + 1 < n_sub)
                def _():
                    x_copy(m + 1).start()

                x_copy(m).wait()
                xb_ref[sub_rows(m), :] = stage_ref[m % 2].astype(BF16)
                return carry

            lax.fori_loop(0, n_sub, load, 0)

        wgu_ref[:, :MOE_FF_TILE] = wg_ref[...].astype(BF16)
        wgu_ref[:, MOE_FF_TILE:] = wu_ref[...].astype(BF16)
        wdb_ref[...] = wd_ref[...].astype(BF16)

        def ffn_tile(m):
            xs = xb_ref[sub_rows(m), :]
            gu = _dot(xs, wgu_ref[...])
            hidden = (jax.nn.silu(gu[:, :MOE_FF_TILE]) * gu[:, MOE_FF_TILE:]).astype(BF16)
            return _dot(hidden, wdb_ref[...])

        def y_copy(m):
            return pltpu.make_async_copy(acc_ref.at[sub_rows(m), :], y_hbm.at[hbm_rows(m), :], out_sem)

        @pl.when(j == 0)
        def _():
            def body(m, carry):
                acc_ref[sub_rows(m), :] = ffn_tile(m)
                return carry

            lax.fori_loop(0, n_sub, body, 0)

        @pl.when(jnp.logical_and(j > 0, j < last))
        def _():
            def body(m, carry):
                acc_ref[sub_rows(m), :] += ffn_tile(m)
                return carry

            lax.fori_loop(0, n_sub, body, 0)

        @pl.when(j == last)
        def _():
            def body(m, carry):
                acc_ref[sub_rows(m), :] += ffn_tile(m)
                y_copy(m).start()
                return carry

            def drain(m, carry):
                y_copy(m).wait()
                return carry

            lax.fori_loop(0, n_sub, body, 0)
            lax.fori_loop(0, n_sub, drain, 0)


def _moe_grouped(xs, wg, wu, wd, visit_exp, visit_row0, visit_nsub, tail_sub):
    n_rows = xs.shape[0]
    n_visits = visit_exp.shape[0]
    d_ff = wg.shape[2]
    assert d_ff // MOE_FF_TILE >= 2
    rows = MOE_VISIT_SUBS * MOE_SUB
    w_in = lambda v, j, e, r, ns, tl: (e[v], 0, j)
    w_out = lambda v, j, e, r, ns, tl: (e[v], j, 0)
    return pl.pallas_call(
        _moe_kernel,
        grid_spec=pltpu.PrefetchScalarGridSpec(
            num_scalar_prefetch=4,
            grid=(n_visits, d_ff // MOE_FF_TILE),
            in_specs=[pl.BlockSpec(memory_space=pl.ANY),
                      pl.BlockSpec((None, D_MODEL, MOE_FF_TILE), w_in),
                      pl.BlockSpec((None, D_MODEL, MOE_FF_TILE), w_in),
                      pl.BlockSpec((None, MOE_FF_TILE, D_MODEL), w_out)],
            out_specs=pl.BlockSpec(memory_space=pl.ANY),
            scratch_shapes=[pltpu.VMEM((rows, D_MODEL), BF16), pltpu.VMEM((rows, D_MODEL), F32),
                            pltpu.VMEM((D_MODEL, 2 * MOE_FF_TILE), BF16),
                            pltpu.VMEM((MOE_FF_TILE, D_MODEL), BF16),
                            pltpu.VMEM((2, MOE_SUB, D_MODEL), F32),
                            pltpu.SemaphoreType.DMA((2,)), pltpu.SemaphoreType.DMA(())]),
        out_shape=jax.ShapeDtypeStruct((n_rows, D_MODEL), F32),
        compiler_params=_params("arbitrary", "arbitrary"),
        name="moe_grouped",
    )(visit_exp, visit_row0, visit_nsub, tail_sub, xs, wg, wu, wd)


def _combine_kernel(slot_ref, x_ref, gate_ref, y_hbm, g_ref, b_ref, o_ref, buf_ref, sem):
    t = pl.program_id(0)
    s = t % 2

    def issue_tile(tt, ss):
        def issue(r, carry):
            base = (tt * MOE_SUB + r) * TOP_K
            for k in range(TOP_K):
                pltpu.make_async_copy(y_hbm.at[pl.ds(slot_ref[base + k], 1), :],
                                      buf_ref.at[ss, k, pl.ds(r, 1), :], sem.at[ss]).start()
            return carry

        lax.fori_loop(0, MOE_SUB, issue, 0, unroll=8)

    @pl.when(t == 0)
    def _():
        issue_tile(0, 0)

    @pl.when(t + 1 < pl.num_programs(0))
    def _():
        issue_tile(t + 1, 1 - s)

    for k in range(TOP_K):
        pltpu.make_async_copy(y_hbm.at[pl.ds(0, MOE_SUB), :], buf_ref.at[s, k], sem.at[s]).wait()
    gate = gate_ref[...]
    ff = gate[:, 0:1] * buf_ref[s, 0] + gate[:, 1:2] * buf_ref[s, 1]
    o_ref[...] = _layer_norm_rows(DN_ALPHA * x_ref[...] + ff, g_ref[...], b_ref[...])


def _combine_ln(x2d, route, slots, ys, g, b):
    n = x2d.shape[0]
    return pl.pallas_call(
        _combine_kernel,
        grid_spec=pltpu.PrefetchScalarGridSpec(
            num_scalar_prefetch=1,
            grid=(n // MOE_SUB,),
            in_specs=[pl.BlockSpec((MOE_SUB, D_MODEL), lambda t, s: (t, 0)),
                      pl.BlockSpec((MOE_SUB, LANES), lambda t, s: (t, 0)),
                      pl.BlockSpec(memory_space=pl.ANY),
                      pl.BlockSpec((1, D_MODEL), lambda t, s: (0, 0)),
                      pl.BlockSpec((1, D_MODEL), lambda t, s: (0, 0))],
            out_specs=pl.BlockSpec((MOE_SUB, D_MODEL), lambda t, s: (t, 0)),
            scratch_shapes=[pltpu.VMEM((2, TOP_K, MOE_SUB, D_MODEL), F32), pltpu.SemaphoreType.DMA((2,))]),
        out_shape=jax.ShapeDtypeStruct((n, D_MODEL), F32),
        compiler_params=_params("arbitrary"),
        name="moe_combine_ln",
    )(slots, x2d, route, ys, g, b)


def _routing_tables(route, n):
    idx = route[:, 2:4].astype(jnp.int32)
    onehot = (idx[:, :, None] == jnp.arange(N_EXPERTS, dtype=jnp.int32)).astype(jnp.int32).sum(1)
    rank = jnp.cumsum(onehot, axis=0) - onehot
    counts = jnp.sum(onehot, axis=0)
    padded = ((counts + MOE_SUB - 1) // MOE_SUB) * MOE_SUB
    ends = jnp.cumsum(padded)
    starts = ends - padded
    slot = starts[idx] + jnp.take_along_axis(rank, idx, axis=1)

    n_rows = -(-(n * TOP_K + N_EXPERTS * (MOE_SUB - 1)) // MOE_SUB) * MOE_SUB

    visit_rows = MOE_VISIT_SUBS * MOE_SUB
    max_chunks = -(-n_rows // visit_rows)
    chunk = jnp.arange(max_chunks, dtype=jnp.int32)[None, :]
    left = padded[:, None] - chunk * visit_rows
    valid = (left > 0).reshape(-1)
    n_visits = n_rows // visit_rows + N_EXPERTS
    order = jnp.argsort(jnp.logical_not(valid), stable=True)[:n_visits]
    n_valid = jnp.sum(valid.astype(jnp.int32))
    live = jnp.arange(n_visits) < n_valid
    order = jnp.where(live, order, order[jnp.maximum(n_valid - 1, 0)])
    v_exp = (order // max_chunks).astype(jnp.int32)
    v_chunk = (order % max_chunks).astype(jnp.int32)
    v_row0 = jnp.where(live, starts[v_exp] + v_chunk * visit_rows, 0).astype(jnp.int32)
    v_nsub = jnp.where(live, jnp.minimum(left.reshape(-1)[order], visit_rows) // MOE_SUB, 0).astype(jnp.int32)
    tail_sub = (ends[N_EXPERTS - 1:] // MOE_SUB).astype(jnp.int32)
    return slot.reshape(-1).astype(jnp.int32), ends.astype(jnp.int32), n_rows, (v_exp, v_row0, v_nsub, tail_sub)


def _moe_ln(x2d, w_router, b_router, wg, wu, wd, g, b):
    n = x2d.shape[0]
    w_pad = jnp.zeros((D_MODEL, LANES), F32).at[:, :N_EXPERTS].set(w_router)
    w_hi = w_pad.astype(BF16)
    w_lo = (w_pad - w_hi.astype(F32)).astype(BF16)
    b_pad = jnp.zeros((1, LANES), F32).at[0, :N_EXPERTS].set(b_router)
    route = _router(x2d, w_hi, w_lo, b_pad)
    slots, ends, n_rows, visits = _routing_tables(route, n)
    xs = _dispatch_rows(x2d, slots, ends, n_rows)
    ys = _moe_grouped(xs, wg, wu, wd, *visits)
    return _combine_ln(x2d, route, slots, ys, g, b)


def _layout_in_proj(w_in, b_in):
    sizes = (MLSTM_W, MLSTM_W, MLSTM_W, MLSTM_W, MLSTM_HEADS, MLSTM_HEADS,
             ATTN_W, ATTN_KV_W, ATTN_KV_W, SGU_W, SGU_W)
    offs = np.concatenate([[0], np.cumsum(sizes)])
    seg = lambda a, i: a[..., int(offs[i]):int(offs[i + 1])]

    def build(a):
        lead = a.shape[:-1]
        aq = seg(a, 6).reshape(lead + (ATTN_Q_HEADS, HEAD_DIM))[..., np.array(ATTN_HEAD_ORDER), :]
        gates = jnp.concatenate([seg(a, 4), seg(a, 5),
                                 jnp.zeros(lead + (LANES - 2 * MLSTM_HEADS,), a.dtype)], -1)
        return jnp.concatenate([seg(a, 0), seg(a, 1), seg(a, 2), seg(a, 3), aq.reshape(lead + (ATTN_W,)),
                                gates, seg(a, 7), seg(a, 8), seg(a, 9), seg(a, 10)], -1)

    return build(w_in).astype(BF16), build(b_in)[None, :]


def _rope_tables(positions):
    inv_freq = ROPE_THETA ** (-jnp.arange(0, ROPE_DIM, 2, dtype=F32) / ROPE_DIM)
    ang = positions.astype(F32)[..., None] * inv_freq
    cos, sin = jnp.cos(ang), jnp.sin(ang)
    half = ROPE_DIM // 2
    ones = jnp.ones(ang.shape[:-1] + (HEAD_DIM - ROPE_DIM,), F32)
    cos_head = jnp.concatenate([cos, cos, ones], -1)
    sin_head = jnp.concatenate([-sin, sin, 0.0 * ones], -1)
    del half
    return jnp.concatenate([cos_head, cos_head], -1), jnp.concatenate([sin_head, sin_head], -1)


def kernel(x, positions, w_in, b_in, conv_w, mlstm_norm_g, attn_sinks, sgu_w_s, sgu_b_s, sgu_norm_g, sgu_norm_b, w_out, ln1_g, ln1_b, ln2_g, ln2_b, ffn_w_gate, ffn_w_up, ffn_w_down, moe_w_router, moe_b_router, moe_w_gate, moe_w_up, moe_w_down):
    bsz, seq, _ = x.shape
    n = bsz * seq
    cos_t, sin_t = _rope_tables(positions)
    tril = jnp.tril(jnp.ones((CHUNK, CHUNK), bool))
    x2d = x.reshape(n, D_MODEL)
    for layer in range(DEPTH):
        w_p, b_p = _layout_in_proj(w_in[layer], b_in[layer])
        proj3 = _inproj(x2d, w_p, b_p).reshape(bsz, seq, D_PROJ_PAD)
        h_a = _mlstm(proj3, conv_w[layer], mlstm_norm_g[layer][None, :])
        h_b = _swa(proj3, cos_t, sin_t, attn_sinks[layer])
        w_tril = jnp.where(tril, sgu_w_s[layer], 0.0).astype(BF16)
        bias_tok = jnp.repeat(sgu_b_s[layer].T, HEAD_DIM, axis=1)
        h_c = _sgu(proj3, w_tril, bias_tok, sgu_norm_g[layer][None, :], sgu_norm_b[layer][None, :])
        wo = w_out[layer]
        wa = wo[:MLSTM_W].astype(BF16)
        wb = wo[MLSTM_W:MLSTM_W + ATTN_W].reshape(ATTN_Q_HEADS, HEAD_DIM, D_MODEL)[np.array(ATTN_HEAD_ORDER)]
        wb = wb.reshape(ATTN_W, D_MODEL).astype(BF16)
        wc = wo[MLSTM_W + ATTN_W:].astype(BF16)
        x2d = _outproj_ln(h_a.reshape(n, MLSTM_W), h_b.reshape(n, ATTN_W), h_c.reshape(n, SGU_W), x2d,
                          wa, wb, wc, ln1_g[layer][None, :], ln1_b[layer][None, :])
        j = layer // 2
        g2, b2 = ln2_g[layer][None, :], ln2_b[layer][None, :]
        if layer % 2 == 0:
            x2d = _ffn_ln(x2d, ffn_w_gate[j].astype(BF16), ffn_w_up[j].astype(BF16),
                          ffn_w_down[j].astype(BF16), g2, b2)
        else:
            x2d = _moe_ln(x2d, moe_w_router[j], moe_b_router[j], moe_w_gate[j], moe_w_up[j],
                          moe_w_down[j], g2, b2)
    return x2d.reshape(bsz, seq, D_MODEL)
```

```python
import functools

import jax
import jax.numpy as jnp
import numpy as np
from jax import lax
from jax.experimental import pallas as pl
from jax.experimental.pallas import tpu as pltpu

F32 = jnp.float32
BF16 = jnp.bfloat16

D_MODEL = 1024
HEAD_DIM = 64
LANES = 128
MLSTM_HEADS = 6
ATTN_Q_HEADS = 6
ATTN_KV_HEADS = 2
SGU_GROUPS = 4
MLSTM_W = MLSTM_HEADS * HEAD_DIM
ATTN_W = ATTN_Q_HEADS * HEAD_DIM
ATTN_KV_W = ATTN_KV_HEADS * HEAD_DIM
SGU_W = SGU_GROUPS * HEAD_DIM
CHUNK = 128
CONV_WIDTH = 4
ROPE_DIM = HEAD_DIM // 4
ROPE_THETA = 500000.0
N_EXPERTS = 8
TOP_K = 2
DEPTH = 2
DN_ALPHA = (2.0 * DEPTH) ** 0.25
LN_EPS = 1e-5

QK_OFF, V_OFF, O_OFF, AQ_OFF = 0, 768, 1152, 1536
GATE_OFF, AK_OFF, AV_OFF, SU_OFF, SV_OFF = 1920, 2048, 2176, 2304, 2560
D_PROJ_PAD = 2816
ATTN_HEAD_ORDER = (0, 3, 1, 4, 2, 5)

VMEM_LIMIT = 56 * 1024 * 1024

ROW_TILE = 512
FFN_ROWS = 1024
FF_TILE = 256
MOE_SUB = 512
MOE_VISIT_SUBS = 9
MOE_FF_TILE = 512


def _params(*sem):
    return pltpu.CompilerParams(dimension_semantics=sem, vmem_limit_bytes=VMEM_LIMIT)


def _lane_lo(shape):
    return lax.broadcasted_iota(jnp.int32, shape, len(shape) - 1) < HEAD_DIM


def _layer_norm_rows(z, g, b):
    mu = jnp.mean(z, axis=-1, keepdims=True)
    zc = z - mu
    var = jnp.mean(zc * zc, axis=-1, keepdims=True)
    return zc * lax.rsqrt(var + LN_EPS) * g + b


def _half_layer_norm(x, lo):
    inv = 1.0 / HEAD_DIM
    s_lo = jnp.sum(jnp.where(lo, x, 0.0), axis=-1, keepdims=True)
    s_all = jnp.sum(x, axis=-1, keepdims=True)
    mu = jnp.where(lo, s_lo, s_all - s_lo) * inv
    xc = x - mu
    sq = xc * xc
    q_lo = jnp.sum(jnp.where(lo, sq, 0.0), axis=-1, keepdims=True)
    q_all = jnp.sum(sq, axis=-1, keepdims=True)
    var = jnp.where(lo, q_lo, q_all - q_lo) * inv
    return xc * lax.rsqrt(var + LN_EPS)


def _split3(a):
    h1 = a.astype(BF16)
    r1 = a - h1.astype(F32)
    h2 = r1.astype(BF16)
    r2 = r1 - h2.astype(F32)
    return h1, h2, r2.astype(BF16)


def _dot(a, b):
    return jnp.dot(a, b, preferred_element_type=F32)


def _inproj_kernel(x_ref, w_ref, b_ref, o_ref):
    o_ref[...] = lax.dot_general(x_ref[...].astype(BF16), w_ref[...], (((1,), (1,)), ((), ())),
                                 preferred_element_type=F32) + b_ref[...]


def _inproj(x2d, w, b):
    n = x2d.shape[0]
    return pl.pallas_call(
        _inproj_kernel,
        grid=(n // ROW_TILE,),
        in_specs=[pl.BlockSpec((ROW_TILE, D_MODEL), lambda i: (i, 0)),
                  pl.BlockSpec((D_PROJ_PAD, D_MODEL), lambda i: (0, 0)),
                  pl.BlockSpec((1, D_PROJ_PAD), lambda i: (0, 0))],
        out_specs=pl.BlockSpec((ROW_TILE, D_PROJ_PAD), lambda i: (i, 0)),
        out_shape=jax.ShapeDtypeStruct((n, D_PROJ_PAD), F32),
        compiler_params=_params("parallel"),
        name="inproj",
    )(x2d, w, b)


def _mlstm_kernel(qk_ref, v_ref, o_ref, gate_ref, cw_ref, g_ref, out_ref, buf_ref, ct_ref, m_ref):
    c = pl.program_id(1)

    @pl.when(c == 0)
    def _():
        buf_ref[0:8, :] = jnp.zeros((8, 2 * MLSTM_W), F32)
        ct_ref[...] = jnp.zeros_like(ct_ref)
        m_ref[...] = jnp.zeros_like(m_ref)

    buf_ref[8:8 + CHUNK, :] = qk_ref[...]
    cw = cw_ref[...]
    base = 8 - (CONV_WIDTH - 1)
    acc = cw[0:1, :] * buf_ref[base:base + CHUNK, :]
    for j in range(1, CONV_WIDTH):
        acc = acc + cw[j:j + 1, :] * buf_ref[base + j:base + j + CHUNK, :]
    buf_ref[0:8, :] = qk_ref[CHUNK - 8:CHUNK, :]
    qk = jax.nn.silu(acc)

    gates = gate_ref[...]
    lf = jax.nn.log_sigmoid(gates)
    row = lax.broadcasted_iota(jnp.int32, (CHUNK, CHUNK), 0)
    col = lax.broadcasted_iota(jnp.int32, (CHUNK, CHUNK), 1)
    causal = col <= row
    tri = causal.astype(BF16)
    l1, l2, l3 = _split3(lf)
    cum = _dot(tri, l1) + _dot(tri, l2) + _dot(tri, l3)
    gates_t = gates.T
    cum_t = cum.T

    lo = _lane_lo((CHUNK, LANES))
    for p in range(MLSTM_HEADS // 2):
        sl = slice(p * LANES, (p + 1) * LANES)
        q_slab = qk[:, sl] * (HEAD_DIM ** -0.5)
        k_slab = qk[:, MLSTM_W + p * LANES:MLSTM_W + (p + 1) * LANES]
        kt_slab = k_slab.T
        v_slab = v_ref[:, sl]
        ct_pair = ct_ref[p]
        ct_pair_b = ct_pair.astype(BF16)
        halves = []
        new_ct = []
        for half in range(2):
            h = 2 * p + half
            sel = lo if half == 0 else jnp.logical_not(lo)
            li_row = gates_t[h:h + 1, :]
            bc_row = cum_t[MLSTM_HEADS + h:MLSTM_HEADS + h + 1, :]
            bc_col = cum[:, MLSTM_HEADS + h:MLSTM_HEADS + h + 1]
            b_tot = bc_row[:, CHUNK - 1:CHUNK]
            m_prev = m_ref[h:h + 1, 0:1]

            dmat = jnp.where(causal, bc_col - bc_row + li_row, -jnp.inf)
            m_inter = bc_col + m_prev
            m_row = jnp.maximum(m_inter, jnp.max(dmat, axis=-1, keepdims=True))
            q_m = jnp.where(sel, q_slab, 0.0).astype(BF16)
            s = _dot(q_m, kt_slab.astype(BF16))
            pmat = (s * jnp.exp(dmat - m_row)).astype(BF16)
            w_inter = jnp.exp(m_inter - m_row)
            v_aug = jnp.where(sel, v_slab, 1.0).astype(BF16)
            r = _dot(pmat, v_aug) + w_inter * _dot(q_m, ct_pair_b)
            den_lane = HEAD_DIM if half == 0 else 0
            den = r[:, den_lane:den_lane + 1]
            halves.append(r / jnp.maximum(jnp.abs(den), jnp.exp(-m_row)))

            a_row = b_tot - bc_row + li_row
            m_loc = jnp.max(a_row, axis=-1, keepdims=True)
            w_row = jnp.exp(a_row - m_loc)
            kt_h = kt_slab[half * HEAD_DIM:(half + 1) * HEAD_DIM, :]
            ct_loc = _dot((kt_h * w_row).astype(BF16), v_aug)
            m_new = jnp.maximum(b_tot + m_prev, m_loc)
            s_old = jnp.exp(b_tot + m_prev - m_new)
            s_loc = jnp.exp(m_loc - m_new)
            new_ct.append(s_old * ct_pair[half * HEAD_DIM:(half + 1) * HEAD_DIM, :] + s_loc * ct_loc)
            m_ref[h:h + 1, :] = jnp.broadcast_to(m_new, (1, LANES))

        ct_ref[p] = jnp.concatenate(new_ct, axis=0)
        hh = jnp.where(lo, halves[0], halves[1])
        hn = _half_layer_norm(hh, lo) * g_ref[:, sl]
        out_ref[:, sl] = (hn * jax.nn.sigmoid(o_ref[:, sl])).astype(out_ref.dtype)


def _mlstm(proj3, conv_w, norm_g):
    b, s, _ = proj3.shape
    blk = lambda w, off: pl.BlockSpec((None, CHUNK, w), lambda i, c: (i, c, off // w))
    return pl.pallas_call(
        _mlstm_kernel,
        grid=(b, s // CHUNK),
        in_specs=[blk(2 * MLSTM_W, QK_OFF), blk(MLSTM_W, V_OFF), blk(MLSTM_W, O_OFF), blk(LANES, GATE_OFF),
                  pl.BlockSpec((CONV_WIDTH, 2 * MLSTM_W), lambda i, c: (0, 0)),
                  pl.BlockSpec((1, MLSTM_W), lambda i, c: (0, 0))],
        out_specs=pl.BlockSpec((None, CHUNK, MLSTM_W), lambda i, c: (i, c, 0)),
        out_shape=jax.ShapeDtypeStruct((b, s, MLSTM_W), BF16),
        scratch_shapes=[pltpu.VMEM((8 + CHUNK, 2 * MLSTM_W), F32),
                        pltpu.VMEM((MLSTM_HEADS // 2, LANES, LANES), F32),
                        pltpu.VMEM((8, LANES), F32)],
        compiler_params=_params("parallel", "arbitrary"),
        name="mlstm",
    )(proj3, proj3, proj3, proj3, conv_w, norm_g)


def _rope(x, cos_t, sin_t, first):
    return x * cos_t + jnp.where(first, pltpu.roll(x, LANES - ROPE_DIM // 2, 1),
                                 pltpu.roll(x, ROPE_DIM // 2, 1)) * sin_t


def _swa_kernel(sink_ref, q_ref, k_ref, v_ref, cos_ref, sin_ref, out_ref, kt_ref, vv_ref):
    c = pl.program_id(1)

    @pl.when(c == 0)
    def _():
        kt_ref[...] = jnp.zeros_like(kt_ref)
        vv_ref[...] = jnp.zeros_like(vv_ref)

    cos_t = cos_ref[...].T
    sin_t = sin_ref[...].T
    lane =lax.broadcasted_iota(jnp.int32, (CHUNK, LANES), 1)
    lo = lane < HEAD_DIM
    first = (lane % HEAD_DIM) < ROPE_DIM // 2

    k_cur = _rope(k_ref[...], cos_t, sin_t, first)
    kt_ref[:, CHUNK:2 * CHUNK] = k_cur.T.astype(BF16)
    vv_ref[CHUNK:2 * CHUNK, :] = v_ref[...].astype(BF16)
    kt = kt_ref[...]
    vv = vv_ref[...]

    row = lax.broadcasted_iota(jnp.int32, (CHUNK, 2 * CHUNK), 0)
    col = lax.broadcasted_iota(jnp.int32, (CHUNK, 2 * CHUNK), 1)
    first_block_shift = jnp.where(c > 0, 0, 2 * CHUNK)
    visible = jnp.logical_or(jnp.logical_and(col < CHUNK, col > row + first_block_shift),
                             jnp.logical_and(col >= CHUNK, col - CHUNK <= row))

    for j in range(ATTN_Q_HEADS // 2):
        sl = slice(j * LANES, (j + 1) * LANES)
        q_slab = _rope(q_ref[:, sl], cos_t, sin_t, first) * (HEAD_DIM ** -0.5)
        outs = []
        for half in range(2):
            sel = lo if half == 0 else jnp.logical_not(lo)
            sink = sink_ref[ATTN_HEAD_ORDER[2 * j + half]]
            q_m = jnp.where(sel, q_slab, 0.0).astype(BF16)
            s = jnp.where(visible, _dot(q_m, kt), -jnp.inf)
            mx = jnp.maximum(jnp.max(s, axis=-1, keepdims=True), sink)
            pexp = jnp.exp(s - mx)
            denom = jnp.sum(pexp, axis=-1, keepdims=True) + jnp.exp(sink - mx)
            outs.append(_dot(pexp.astype(BF16), vv) / denom)
        out_ref[:, sl] = jnp.where(lo, outs[0], outs[1]).astype(out_ref.dtype)

    kt_ref[:, 0:CHUNK] = kt_ref[:, CHUNK:2 * CHUNK]
    vv_ref[0:CHUNK, :] = vv_ref[CHUNK:2 * CHUNK, :]


def _swa(proj3, cos_t, sin_t, sinks):
    b, s, _ = proj3.shape
    blk = lambda w, off: pl.BlockSpec((None, CHUNK, w), lambda i, c: (i, c, off // w))
    tab = pl.BlockSpec((None, LANES, CHUNK), lambda i, c: (i, 0, c))
    return pl.pallas_call(
        _swa_kernel,
        grid=(b, s // CHUNK),
        in_specs=[pl.BlockSpec(memory_space=pltpu.SMEM),
                  blk(ATTN_W, AQ_OFF), blk(ATTN_KV_W, AK_OFF), blk(ATTN_KV_W, AV_OFF), tab, tab],
        out_specs=pl.BlockSpec((None, CHUNK, ATTN_W), lambda i, c: (i, c, 0)),
        out_shape=jax.ShapeDtypeStruct((b, s, ATTN_W), BF16),
        scratch_shapes=[pltpu.VMEM((LANES, 2 * CHUNK), BF16), pltpu.VMEM((2 * CHUNK, LANES), BF16)],
        compiler_params=_params("parallel", "arbitrary"),
        name="swa",
    )(sinks, proj3, proj3, proj3, cos_t, sin_t)


def _sgu_kernel(u_ref, v_ref, w_ref, bias_ref, g_ref, b_ref, out_ref):
    lo = _lane_lo((CHUNK, LANES))
    for j in range(SGU_GROUPS // 2):
        sl = slice(j * LANES, (j + 1) * LANES)
        u = jax.nn.gelu(u_ref[:, sl])
        v = jax.nn.gelu(v_ref[:, sl])
        vn = (_half_layer_norm(v, lo) * g_ref[:, sl] + b_ref[:, sl]).astype(BF16)
        mixed = jnp.where(lo, _dot(w_ref[2 * j], vn), _dot(w_ref[2 * j + 1], vn)) + bias_ref[:, sl]
        out_ref[:, sl] = (u * mixed).astype(out_ref.dtype)


def _sgu(proj3, w_tril, bias_tok, norm_g, norm_b):
    b, s, _ = proj3.shape
    blk = lambda w, off: pl.BlockSpec((None, CHUNK, w), lambda i, c: (i, c, off // w))
    const2 = lambda shape: pl.BlockSpec(shape, lambda i, c: (0,) * len(shape))
    return pl.pallas_call(
        _sgu_kernel,
        grid=(b, s // CHUNK),
        in_specs=[blk(SGU_W, SU_OFF), blk(SGU_W, SV_OFF), const2((SGU_GROUPS, CHUNK, CHUNK)),
                  const2((CHUNK, SGU_W)), const2((1, SGU_W)), const2((1, SGU_W))],
        out_specs=pl.BlockSpec((None, CHUNK, SGU_W), lambda i, c: (i, c, 0)),
        out_shape=jax.ShapeDtypeStruct((b, s, SGU_W), BF16),
        compiler_params=_params("parallel", "parallel"),
        name="sgu",
    )(proj3, proj3, w_tril, bias_tok, norm_g, norm_b)


def _outproj_kernel(ha_ref, hb_ref, hc_ref, x_ref, wa_ref, wb_ref, wc_ref, g_ref, b_ref, o_ref):
    mix = _dot(ha_ref[...], wa_ref[...]) + _dot(hb_ref[...], wb_ref[...]) + _dot(hc_ref[...], wc_ref[...])
    o_ref[...] = _layer_norm_rows(DN_ALPHA * x_ref[...] + mix, g_ref[...], b_ref[...])


def _outproj_ln(ha, hb, hc, x2d, wa, wb, wc, g, b):
    n = x2d.shape[0]
    rows = lambda w: pl.BlockSpec((ROW_TILE, w), lambda i: (i, 0))
    const = lambda shape: pl.BlockSpec(shape, lambda i: (0, 0))
    return pl.pallas_call(
        _outproj_kernel,
        grid=(n // ROW_TILE,),
        in_specs=[rows(MLSTM_W), rows(ATTN_W), rows(SGU_W), rows(D_MODEL),
                  const((MLSTM_W, D_MODEL)), const((ATTN_W, D_MODEL)), const((SGU_W, D_MODEL)),
                  const((1, D_MODEL)), const((1, D_MODEL))],
        out_specs=rows(D_MODEL),
        out_shape=jax.ShapeDtypeStruct((n, D_MODEL), F32),
        compiler_params=_params("parallel"),
        name="outproj_ln",
    )(ha, hb, hc, x2d, wa, wb, wc, g, b)


def _ffn_kernel(x_ref, wg_ref, wu_ref, wd_ref, g_ref, b_ref, o_ref, acc_ref):
    j = pl.program_id(1)
    xb = x_ref[...].astype(BF16)
    hidden = (jax.nn.silu(_dot(xb, wg_ref[...])) * _dot(xb, wu_ref[...])).astype(BF16)
    part = _dot(hidden, wd_ref[...])

    @pl.when(j == 0)
    def _():
        acc_ref[...] = part

    @pl.when(j > 0)
    def _():
        acc_ref[...] += part

    @pl.when(j == pl.num_programs(1) - 1)
    def _():
        o_ref[...] = _layer_norm_rows(DN_ALPHA * x_ref[...] + acc_ref[...], g_ref[...], b_ref[...])


def _ffn_ln(x2d, wg, wu, wd, g, b):
    n = x2d.shape[0]
    d_ff = wg.shape[1]
    tm = FFN_ROWS
    return pl.pallas_call(
        _ffn_kernel,
        grid=(n // tm, d_ff // FF_TILE),
        in_specs=[pl.BlockSpec((tm, D_MODEL), lambda i, j: (i, 0)),
                  pl.BlockSpec((D_MODEL, FF_TILE), lambda i, j: (0, j)),
                  pl.BlockSpec((D_MODEL, FF_TILE), lambda i, j: (0, j)),
                  pl.BlockSpec((FF_TILE, D_MODEL), lambda i, j: (j, 0)),
                  pl.BlockSpec((1, D_MODEL), lambda i, j: (0, 0)),
                  pl.BlockSpec((1, D_MODEL), lambda i, j: (0, 0))],
        out_specs=pl.BlockSpec((tm, D_MODEL), lambda i, j: (i, 0)),
        out_shape=jax.ShapeDtypeStruct((n, D_MODEL), F32),
        scratch_shapes=[pltpu.VMEM((tm, D_MODEL), F32)],
        compiler_params=_params("parallel", "arbitrary"),
        name="ffn_ln",
    )(x2d, wg, wu, wd, g, b)


def _router_kernel(x_ref, wh_ref, wl_ref, b_ref, o_ref, ot_ref):
    x = x_ref[...]
    xh = x.astype(BF16)
    xl = (x - xh.astype(F32)).astype(BF16)
    logits = _dot(xh, wh_ref[...]) + _dot(xl, wh_ref[...]) + _dot(xh, wl_ref[...]) + b_ref[...]
    lane = lax.broadcasted_iota(jnp.int32, logits.shape, 1)
    logits = jnp.where(lane < N_EXPERTS, logits, -jnp.inf)
    m1 = jnp.max(logits, axis=-1, keepdims=True)
    i1 = jnp.min(jnp.where(logits == m1, lane, LANES), axis=-1, keepdims=True)
    rest = jnp.where(lane == i1, -jnp.inf, logits)
    m2 = jnp.max(rest, axis=-1, keepdims=True)
    i2 = jnp.min(jnp.where(rest == m2, lane, LANES), axis=-1, keepdims=True)
    e2 = jnp.exp(m2 - m1)
    g1 = 1.0 / (1.0 + e2)
    g2 = e2 / (1.0 + e2)
    route = jnp.where(lane == 0, g1, jnp.where(lane == 1, g2, jnp.where(
        lane == 2, i1.astype(F32), jnp.where(lane == 3, i2.astype(F32), 0.0))))
    o_ref[...] = route
    ot_ref[...] = route.T[0:8, :]


def _router(x2d, w_hi, w_lo, b):
    n = x2d.shape[0]
    const = lambda shape: pl.BlockSpec(shape, lambda i: (0, 0))
    return pl.pallas_call(
        _router_kernel,
        grid=(n // ROW_TILE,),
        in_specs=[pl.BlockSpec((ROW_TILE, D_MODEL), lambda i: (i, 0)),
                  const((D_MODEL, LANES)), const((D_MODEL, LANES)), const((1, LANES))],
        out_specs=[pl.BlockSpec((ROW_TILE, LANES), lambda i: (i, 0)), pl.BlockSpec((8, ROW_TILE), lambda i: (0, i))],
        out_shape=[jax.ShapeDtypeStruct((n, LANES), F32), jax.ShapeDtypeStruct((8, n), F32)],
        compiler_params=_params("parallel"),
        name="router",
    )(x2d, w_hi, w_lo, b)


def _dispatch_kernel(slot0_ref, slot1_ref, ends_ref, x_ref, o_hbm, stage_ref, zero_ref, sem, zero_sem):
    t = pl.program_id(0)
    last = pl.num_programs(0) - 1
    s = t % 2

    def wait_tile(ss):
        for _ in range(TOP_K):
            pltpu.make_async_copy(stage_ref.at[ss], o_hbm.at[pl.ds(0, MOE_SUB), :], sem.at[ss]).wait()

    @pl.when(t == 0)
    def _():
        zero_ref[...] = jnp.zeros_like(zero_ref)

        def zero_sub_tile(m, carry):
            dst = o_hbm.at[pl.ds(pl.multiple_of(m * MOE_SUB, MOE_SUB), MOE_SUB), :]
            cp = pltpu.make_async_copy(zero_ref, dst, zero_sem)
            cp.start()
            cp.wait()
            return carry

        for e in range(N_EXPERTS):
            @pl.when(ends_ref[e] >= MOE_SUB)
            def _():
                zero_sub_tile(ends_ref[e] // MOE_SUB - 1, 0)

        lax.fori_loop(ends_ref[N_EXPERTS - 1] // MOE_SUB, o_hbm.shape[0] // MOE_SUB, zero_sub_tile, 0)

    @pl.when(t >= 2)
    def _():
        wait_tile(s)

    stage_ref[s] = x_ref[...]

    def issue(r, carry):
        for slot_ref in (slot0_ref, slot1_ref):
            pltpu.make_async_copy(stage_ref.at[s, pl.ds(r, 1), :],
                                  o_hbm.at[pl.ds(slot_ref[t * MOE_SUB + r], 1), :], sem.at[s]).start()
        return carry

    lax.fori_loop(0, MOE_SUB, issue, 0, unroll=8)

    @pl.when(t == last)
    def _():
        wait_tile(s)
        wait_tile(1 - s)


def _dispatch_rows(x2d, slots, ends, n_rows):
    n = x2d.shape[0]
    assert n // MOE_SUB >= 2
    return pl.pallas_call(
        _dispatch_kernel,
        grid_spec=pltpu.PrefetchScalarGridSpec(
            num_scalar_prefetch=3,
            grid=(n // MOE_SUB,),
            in_specs=[pl.BlockSpec((MOE_SUB, D_MODEL), lambda t, s0, s1, en: (t, 0))],
            out_specs=pl.BlockSpec(memory_space=pl.ANY),
            scratch_shapes=[pltpu.VMEM((2, MOE_SUB, D_MODEL), F32), pltpu.VMEM((MOE_SUB, D_MODEL), F32),
                            pltpu.SemaphoreType.DMA((2,)), pltpu.SemaphoreType.DMA(())]),
        out_shape=jax.ShapeDtypeStruct((n_rows, D_MODEL), F32),
        compiler_params=_params("arbitrary"),
        name="moe_dispatch",
    )(slots[0], slots[1], ends, x2d)


def _moe_kernel(exp_ref, row0_ref, nsub_ref, tail_ref, x_hbm, wg_ref, wu_ref, wd_ref, y_hbm,
                xb_ref, acc_ref, wgu_ref, wdb_ref, stage_ref, in_sem, out_sem):
    v = pl.program_id(0)
    j = pl.program_id(1)
    last = pl.num_programs(1) - 1
    n_sub = nsub_ref[v]
    row0 = row0_ref[v]

    def sub_rows(m):
        return pl.ds(pl.multiple_of(m * MOE_SUB, MOE_SUB), MOE_SUB)

    def hbm_rows(m):
        return pl.ds(pl.multiple_of(row0 + m * MOE_SUB, MOE_SUB), MOE_SUB)

    @pl.when(jnp.logical_and(v == 0, j == 0))
    def _():
        stage_ref[0] = jnp.zeros((MOE_SUB, D_MODEL), F32)

        def zero_sub_tile(m, carry):
            cp = pltpu.make_async_copy(stage_ref.at[0], y_hbm.at[sub_rows(m), :], out_sem)
            cp.start()
            cp.wait()
            return carry

        lax.fori_loop(tail_ref[0], y_hbm.shape[0] // MOE_SUB, zero_sub_tile, 0)

    @pl.when(n_sub > 0)
    def _():
        wgu_ref[:, :MOE_FF_TILE] = wg_ref[...].astype(BF16)
        wgu_ref[:, MOE_FF_TILE:] = wu_ref[...].astype(BF16)
        wdb_ref[...] = wd_ref[...].astype(BF16)

        def ffn_tile(m):
            xs = xb_ref[sub_rows(m), :]
            gu = _dot(xs, wgu_ref[...])
            hidden = (jax.nn.silu(gu[:, :MOE_FF_TILE]) * gu[:, MOE_FF_TILE:]).astype(BF16)
            return _dot(hidden, wdb_ref[...])

        def y_copy(m):
            return pltpu.make_async_copy(acc_ref.at[sub_rows(m), :], y_hbm.at[hbm_rows(m), :], out_sem)

        @pl.when(j == 0)
        def _():
            def x_copy(m):
                return pltpu.make_async_copy(x_hbm.at[hbm_rows(m), :], stage_ref.at[m % 2], in_sem.at[m % 2])

            x_copy(0).start()

            def body(m, carry):
                @pl.when(m + 1 < n_sub)
                def _():
                    x_copy(m + 1).start()

                x_copy(m).wait()
                xb_ref[sub_rows(m), :] = stage_ref[m % 2].astype(BF16)
                acc_ref[sub_rows(m), :] = ffn_tile(m)
                return carry

            lax.fori_loop(0, n_sub, body, 0)

        @pl.when(jnp.logical_and(j > 0, j < last))
        def _():
            def body(m, carry):
                acc_ref[sub_rows(m), :] += ffn_tile(m)
                return carry

            lax.fori_loop(0, n_sub, body, 0)

        @pl.when(j == last)
        def _():
            def body(m, carry):
                acc_ref[sub_rows(m), :] += ffn_tile(m)
                y_copy(m).start()
                return carry

            def drain(m, carry):
                y_copy(m).wait()
                return carry

            lax.fori_loop(0, n_sub, body, 0)
            lax.fori_loop(0, n_sub, drain, 0)


def _moe_grouped(xs, wg, wu, wd, visit_exp, visit_row0, visit_nsub, tail_sub):
    n_rows = xs.shape[0]
    n_visits = visit_exp.shape[0]
    d_ff = wg.shape[2]
    assert d_ff // MOE_FF_TILE >= 2
    rows = MOE_VISIT_SUBS * MOE_SUB
    w_in = lambda v, j, e, r, ns, tl: (e[v], 0, j)
    w_out = lambda v, j, e, r, ns, tl: (e[v], j, 0)
    return pl.pallas_call(
        _moe_kernel,
        grid_spec=pltpu.PrefetchScalarGridSpec(
            num_scalar_prefetch=4,
            grid=(n_visits, d_ff // MOE_FF_TILE),
            in_specs=[pl.BlockSpec(memory_space=pl.ANY),
                      pl.BlockSpec((None, D_MODEL, MOE_FF_TILE), w_in),
                      pl.BlockSpec((None, D_MODEL, MOE_FF_TILE), w_in),
                      pl.BlockSpec((None, MOE_FF_TILE, D_MODEL), w_out)],
            out_specs=pl.BlockSpec(memory_space=pl.ANY),
            scratch_shapes=[pltpu.VMEM((rows, D_MODEL), BF16), pltpu.VMEM((rows, D_MODEL), F32),
                            pltpu.VMEM((D_MODEL, 2 * MOE_FF_TILE), BF16),
                            pltpu.VMEM((MOE_FF_TILE, D_MODEL), BF16),
                            pltpu.VMEM((2, MOE_SUB, D_MODEL), F32),
                            pltpu.SemaphoreType.DMA((2,)), pltpu.SemaphoreType.DMA(())]),
        out_shape=jax.ShapeDtypeStruct((n_rows, D_MODEL), F32),
        compiler_params=_params("arbitrary", "arbitrary"),
        name="moe_grouped",
    )(visit_exp, visit_row0, visit_nsub, tail_sub, xs, wg, wu, wd)


def _combine_kernel(slot0_ref, slot1_ref, x_ref, gate_ref, y_hbm, g_ref, b_ref, o_ref, buf_ref, sem):
    t = pl.program_id(0)
    s = t % 2

    def issue_tile(tt, ss):
        def issue(r, carry):
            for k, slot_ref in enumerate((slot0_ref, slot1_ref)):
                pltpu.make_async_copy(y_hbm.at[pl.ds(slot_ref[tt * MOE_SUB + r], 1), :],
                                      buf_ref.at[ss, k, pl.ds(r, 1), :], sem.at[ss]).start()
            return carry

        lax.fori_loop(0, MOE_SUB, issue, 0, unroll=8)

    @pl.when(t == 0)
    def _():
        issue_tile(0, 0)

    @pl.when(t + 1 < pl.num_programs(0))
    def _():
        issue_tile(t + 1, 1 - s)

    for k in range(TOP_K):
        pltpu.make_async_copy(y_hbm.at[pl.ds(0, MOE_SUB), :], buf_ref.at[s, k], sem.at[s]).wait()
    gate = gate_ref[...]
    ff = gate[:, 0:1] * buf_ref[s, 0] + gate[:, 1:2] * buf_ref[s, 1]
    o_ref[...] = _layer_norm_rows(DN_ALPHA * x_ref[...] + ff, g_ref[...], b_ref[...])


def _combine_ln(x2d, route, slots, ys, g, b):
    n = x2d.shape[0]
    return pl.pallas_call(
        _combine_kernel,
        grid_spec=pltpu.PrefetchScalarGridSpec(
            num_scalar_prefetch=2,
            grid=(n // MOE_SUB,),
            in_specs=[pl.BlockSpec((MOE_SUB, D_MODEL), lambda t, s0, s1: (t, 0)),
                      pl.BlockSpec((MOE_SUB, LANES), lambda t, s0, s1: (t, 0)),
                      pl.BlockSpec(memory_space=pl.ANY),
                      pl.BlockSpec((1, D_MODEL), lambda t, s0, s1: (0, 0)),
                      pl.BlockSpec((1, D_MODEL), lambda t, s0, s1: (0, 0))],
            out_specs=pl.BlockSpec((MOE_SUB, D_MODEL), lambda t, s0, s1: (t, 0)),
            scratch_shapes=[pltpu.VMEM((2, TOP_K, MOE_SUB, D_MODEL), F32), pltpu.SemaphoreType.DMA((2,))]),
        out_shape=jax.ShapeDtypeStruct((n, D_MODEL), F32),
        compiler_params=_params("arbitrary"),
        name="moe_combine_ln",
    )(slots[0], slots[1], x2d, route, ys, g, b)


def _routing_tables(route_t, n):
    idx = route_t[2:4].astype(jnp.int32)
    expert = jnp.arange(N_EXPERTS, dtype=jnp.int32)[:, None]
    chosen = [idx[k][None, :] == expert for k in range(TOP_K)]
    onehot = jnp.logical_or(chosen[0], chosen[1]).astype(jnp.int32)
    rank = jnp.cumsum(onehot, axis=1) - onehot
    counts = jnp.sum(onehot, axis=1)
    padded = ((counts + MOE_SUB - 1) // MOE_SUB) * MOE_SUB
    ends = jnp.cumsum(padded)
    starts = ends - padded
    place = starts[:, None] + rank
    slot = [jnp.sum(jnp.where(chosen[k], place, 0), axis=0).astype(jnp.int32) for k in range(TOP_K)]

    n_rows = -(-(n * TOP_K + N_EXPERTS * (MOE_SUB - 1)) // MOE_SUB) * MOE_SUB

    visit_rows = MOE_VISIT_SUBS * MOE_SUB
    max_chunks = -(-n_rows // visit_rows)
    chunk = jnp.arange(max_chunks, dtype=jnp.int32)[None, :]
    left = padded[:, None] - chunk * visit_rows
    valid = (left > 0).reshape(-1)
    n_visits = n_rows // visit_rows + N_EXPERTS
    order = jnp.argsort(jnp.logical_not(valid), stable=True)[:n_visits]
    n_valid = jnp.sum(valid.astype(jnp.int32))
    live = jnp.arange(n_visits) < n_valid
    order = jnp.where(live, order, order[jnp.maximum(n_valid - 1, 0)])
    v_exp = (order // max_chunks).astype(jnp.int32)
    v_chunk = (order % max_chunks).astype(jnp.int32)
    v_row0 = jnp.where(live, starts[v_exp] + v_chunk * visit_rows, 0).astype(jnp.int32)
    v_nsub = jnp.where(live, jnp.minimum(left.reshape(-1)[order], visit_rows) // MOE_SUB, 0).astype(jnp.int32)
    tail_sub = (ends[N_EXPERTS - 1:] // MOE_SUB).astype(jnp.int32)
    return slot, ends.astype(jnp.int32), n_rows, (v_exp, v_row0, v_nsub, tail_sub)


def _moe_ln(x2d, w_router, b_router, wg, wu, wd, g, b):
    n = x2d.shape[0]
    w_pad = jnp.zeros((D_MODEL, LANES), F32).at[:, :N_EXPERTS].set(w_router)
    w_hi = w_pad.astype(BF16)
    w_lo = (w_pad - w_hi.astype(F32)).astype(BF16)
    b_pad = jnp.zeros((1, LANES), F32).at[0, :N_EXPERTS].set(b_router)
    route, route_t = _router(x2d, w_hi, w_lo, b_pad)
    slots, ends, n_rows, visits = _routing_tables(route_t, n)
    xs = _dispatch_rows(x2d, slots, ends, n_rows)
    ys = _moe_grouped(xs, wg, wu, wd, *visits)
    return _combine_ln(x2d, route, slots, ys, g, b)


def _layout_in_proj(w_in, b_in):
    sizes = (MLSTM_W, MLSTM_W, MLSTM_W, MLSTM_W, MLSTM_HEADS, MLSTM_HEADS,
             ATTN_W, ATTN_KV_W, ATTN_KV_W, SGU_W, SGU_W)
    offs = np.concatenate([[0], np.cumsum(sizes)])
    seg = lambda a, i: a[int(offs[i]):int(offs[i + 1])]

    def build(a):
        tail = a.shape[1:]
        aq = seg(a, 6).reshape((ATTN_Q_HEADS, HEAD_DIM) + tail)[np.array(ATTN_HEAD_ORDER)]
        gates = jnp.concatenate([seg(a, 4), seg(a, 5), jnp.zeros((LANES - 2 * MLSTM_HEADS,) + tail, a.dtype)], 0)
        return jnp.concatenate([seg(a, 0), seg(a, 1), seg(a, 2), seg(a, 3), aq.reshape((ATTN_W,) + tail),
                                gates, seg(a, 7), seg(a, 8), seg(a, 9), seg(a, 10)], 0)

    return build(w_in.T).astype(BF16), build(b_in)[None, :]


def _rope_tables(positions):
    inv_freq = ROPE_THETA ** (-jnp.arange(0, ROPE_DIM, 2, dtype=F32) / ROPE_DIM)
    ang = inv_freq[None, :, None] * positions.astype(F32)[:, None, :]
    cos, sin = jnp.cos(ang), jnp.sin(ang)
    ones = jnp.ones((ang.shape[0], HEAD_DIM - ROPE_DIM, ang.shape[2]), F32)
    cos_head = jnp.concatenate([cos, cos, ones], 1)
    sin_head = jnp.concatenate([-sin, sin, 0.0 * ones], 1)
    return jnp.concatenate([cos_head, cos_head], 1), jnp.concatenate([sin_head, sin_head], 1)


def kernel(x, positions, w_in, b_in, conv_w, mlstm_norm_g, attn_sinks, sgu_w_s, sgu_b_s, sgu_norm_g, sgu_norm_b, w_out, ln1_g, ln1_b, ln2_g, ln2_b, ffn_w_gate, ffn_w_up, ffn_w_down, moe_w_router, moe_b_router, moe_w_gate, moe_w_up, moe_w_down):
    bsz, seq, _ = x.shape
    n = bsz * seq
    cos_t, sin_t = _rope_tables(positions)
    tril = jnp.tril(jnp.ones((CHUNK, CHUNK), bool))
    x2d = x.reshape(n, D_MODEL)
    for layer in range(DEPTH):
        w_p, b_p = _layout_in_proj(w_in[layer], b_in[layer])
        proj3 = _inproj(x2d, w_p, b_p).reshape(bsz, seq, D_PROJ_PAD)
        h_a = _mlstm(proj3, conv_w[layer], mlstm_norm_g[layer][None, :])
        h_b = _swa(proj3, cos_t, sin_t, attn_sinks[layer])
        w_tril = jnp.where(tril, sgu_w_s[layer], 0.0).astype(BF16)
        bias_tok = jnp.repeat(sgu_b_s[layer].T, HEAD_DIM, axis=1)
        h_c = _sgu(proj3, w_tril, bias_tok, sgu_norm_g[layer][None, :], sgu_norm_b[layer][None, :])
        wo = w_out[layer]
        wa = wo[:MLSTM_W].astype(BF16)
        wb = wo[MLSTM_W:MLSTM_W + ATTN_W].reshape(ATTN_Q_HEADS, HEAD_DIM, D_MODEL)[np.array(ATTN_HEAD_ORDER)]
        wb = wb.reshape(ATTN_W, D_MODEL).astype(BF16)
        wc = wo[MLSTM_W + ATTN_W:].astype(BF16)
        x2d = _outproj_ln(h_a.reshape(n, MLSTM_W), h_b.reshape(n, ATTN_W), h_c.reshape(n, SGU_W), x2d,
                          wa, wb, wc, ln1_g[layer][None, :], ln1_b[layer][None, :])
        j = layer // 2
        g2, b2 = ln2_g[layer][None, :], ln2_b[layer][None, :]
        if layer % 2 == 0:
            x2d = _ffn_ln(x2d, ffn_w_gate[j].astype(BF16), ffn_w_up[j].astype(BF16),
                          ffn_w_down[j].astype(BF16), g2, b2)
        else:
            x2d = _moe_ln(x2d, moe_w_router[j], moe_b_router[j], moe_w_gate[j], moe_w_up[j],
                          moe_w_down[j], g2, b2)
    return x2d.reshape(bsz, seq, D_MODEL)
```

```python
import functools

import jax
import jax.numpy as jnp
import numpy as np
from jax import lax
from jax.experimental import pallas as pl
from jax.experimental.pallas import tpu as pltpu

F32 = jnp.float32
BF16 = jnp.bfloat16

D_MODEL = 1024
HEAD_DIM = 64
LANES = 128
MLSTM_HEADS = 6
ATTN_Q_HEADS = 6
ATTN_KV_HEADS = 2
SGU_GROUPS = 4
MLSTM_W = MLSTM_HEADS * HEAD_DIM
ATTN_W = ATTN_Q_HEADS * HEAD_DIM
ATTN_KV_W = ATTN_KV_HEADS * HEAD_DIM
SGU_W = SGU_GROUPS * HEAD_DIM
CHUNK = 128
CONV_WIDTH = 4
ROPE_DIM = HEAD_DIM // 4
ROPE_THETA = 500000.0
N_EXPERTS = 8
TOP_K = 2
DEPTH = 2
DN_ALPHA = (2.0 * DEPTH) ** 0.25
LN_EPS = 1e-5

QK_OFF, V_OFF, O_OFF, AQ_OFF = 0, 768, 1152, 1536
GATE_OFF, AK_OFF, AV_OFF, SU_OFF, SV_OFF = 1920, 2048, 2176, 2304, 2560
D_PROJ_PAD = 2816
ATTN_HEAD_ORDER = (0, 3, 1, 4, 2, 5)

VMEM_LIMIT = 56 * 1024 * 1024

SGU_CHUNKS_PER_STEP = 4
SEQS_PER_STEP = 1
ROW_TILE = 512
FFN_ROWS = 2048
FF_TILE = 256
MOE_SUB = 512
MOE_VISIT_SUBS = 9
MOE_FF_TILE = 512


def _params(*sem):
    return pltpu.CompilerParams(dimension_semantics=sem, vmem_limit_bytes=VMEM_LIMIT)


def _lane_lo(shape):
    return lax.broadcasted_iota(jnp.int32, shape, len(shape) - 1) < HEAD_DIM


def _layer_norm_rows(z, g, b):
    mu = jnp.mean(z, axis=-1, keepdims=True)
    zc = z - mu
    var = jnp.mean(zc * zc, axis=-1, keepdims=True)
    return zc * lax.rsqrt(var + LN_EPS) * g + b


def _half_layer_norm(x, lo):
    inv = 1.0 / HEAD_DIM
    s_lo = jnp.sum(jnp.where(lo, x, 0.0), axis=-1, keepdims=True)
    s_all = jnp.sum(x, axis=-1, keepdims=True)
    mu = jnp.where(lo, s_lo, s_all - s_lo) * inv
    xc = x - mu
    sq = xc * xc
    q_lo = jnp.sum(jnp.where(lo, sq, 0.0), axis=-1, keepdims=True)
    q_all = jnp.sum(sq, axis=-1, keepdims=True)
    var = jnp.where(lo, q_lo, q_all - q_lo) * inv
    return xc * lax.rsqrt(var + LN_EPS)


def _split3(a):
    h1 = a.astype(BF16)
    r1 = a - h1.astype(F32)
    h2 = r1.astype(BF16)
    r2 = r1 - h2.astype(F32)
    return h1, h2, r2.astype(BF16)


def _dot(a, b):
    return jnp.dot(a, b, preferred_element_type=F32)


def _inproj_kernel(x_ref, w_ref, b_ref, o_ref):
    o_ref[...] = lax.dot_general(x_ref[...].astype(BF16), w_ref[...], (((1,), (1,)), ((), ())),
                                 preferred_element_type=F32) + b_ref[...]


def _inproj(x2d, w, b):
    n = x2d.shape[0]
    return pl.pallas_call(
        _inproj_kernel,
        grid=(n // ROW_TILE,),
        in_specs=[pl.BlockSpec((ROW_TILE, D_MODEL), lambda i: (i, 0)),
                  pl.BlockSpec((D_PROJ_PAD, D_MODEL), lambda i: (0, 0)),
                  pl.BlockSpec((1, D_PROJ_PAD), lambda i: (0, 0))],
        out_specs=pl.BlockSpec((ROW_TILE, D_PROJ_PAD), lambda i: (i, 0)),
        out_shape=jax.ShapeDtypeStruct((n, D_PROJ_PAD), F32),
        compiler_params=_params("parallel"),
        name="inproj",
    )(x2d, w, b)


def _mlstm_kernel(qk_ref, v_ref, o_ref, gate_ref, cw_ref, g_ref, out_ref, buf_ref, ct_ref, m_ref):
    @pl.when(pl.program_id(1) == 0)
    def _():
        buf_ref[:, 0:8, :] = jnp.zeros((SEQS_PER_STEP, 8, 2 * MLSTM_W), F32)
        ct_ref[...] = jnp.zeros_like(ct_ref)
        m_ref[...] = jnp.zeros_like(m_ref)

    for i in range(SEQS_PER_STEP):
        _mlstm_chunk(qk_ref.at[i], v_ref.at[i], o_ref.at[i], gate_ref.at[i], cw_ref, g_ref, out_ref.at[i],
                     buf_ref.at[i], ct_ref.at[i], m_ref.at[i])


def _mlstm_chunk(qk_ref, v_ref, o_ref, gate_ref, cw_ref, g_ref, out_ref, buf_ref, ct_ref, m_ref):
    buf_ref[8:8 + CHUNK, :] = qk_ref[...]
    cw = cw_ref[...]
    base = 8 - (CONV_WIDTH - 1)
    acc = cw[0:1, :] * buf_ref[base:base + CHUNK, :]
    for j in range(1, CONV_WIDTH):
        acc = acc + cw[j:j + 1, :] * buf_ref[base + j:base + j + CHUNK, :]
    buf_ref[0:8, :] = qk_ref[CHUNK - 8:CHUNK, :]
    qk = jax.nn.silu(acc)

    gates = gate_ref[...]
    lf = jax.nn.log_sigmoid(gates)
    row = lax.broadcasted_iota(jnp.int32, (CHUNK, CHUNK), 0)
    col = lax.broadcasted_iota(jnp.int32, (CHUNK, CHUNK), 1)
    causal = col <= row
    tri = causal.astype(BF16)
    l1, l2, l3 = _split3(lf)
    cum = _dot(tri, l1) + _dot(tri, l2) + _dot(tri, l3)
    gates_t = gates.T
    cum_t = cum.T

    lo = _lane_lo((CHUNK, LANES))
    for p in range(MLSTM_HEADS // 2):
        sl = slice(p * LANES, (p + 1) * LANES)
        q_slab = qk[:, sl] * (HEAD_DIM ** -0.5)
        k_slab = qk[:, MLSTM_W + p * LANES:MLSTM_W + (p + 1) * LANES]
        kt_slab = k_slab.T
        v_slab = v_ref[:, sl]
        ct_pair = ct_ref[p]
        ct_pair_b = ct_pair.astype(BF16)
        halves = []
        new_ct = []
        for half in range(2):
            h = 2 * p + half
            sel = lo if half == 0 else jnp.logical_not(lo)
            li_row = gates_t[h:h + 1, :]
            bc_row = cum_t[MLSTM_HEADS + h:MLSTM_HEADS + h + 1, :]
            bc_col = cum[:, MLSTM_HEADS + h:MLSTM_HEADS + h + 1]
            b_tot = bc_row[:, CHUNK - 1:CHUNK]
            m_prev = m_ref[h:h + 1, 0:1]

            dmat = jnp.where(causal, bc_col - bc_row + li_row, -jnp.inf)
            m_inter = bc_col + m_prev
            m_row = jnp.maximum(m_inter, jnp.max(dmat, axis=-1, keepdims=True))
            q_m = jnp.where(sel, q_slab, 0.0).astype(BF16)
            s = _dot(q_m, kt_slab.astype(BF16))
            pmat = (s * jnp.exp(dmat - m_row)).astype(BF16)
            w_inter = jnp.exp(m_inter - m_row)
            v_aug = jnp.where(sel, v_slab, 1.0).astype(BF16)
            r = _dot(pmat, v_aug) + w_inter * _dot(q_m, ct_pair_b)
            den_lane = HEAD_DIM if half == 0 else 0
            den = r[:, den_lane:den_lane + 1]
            halves.append(r / jnp.maximum(jnp.abs(den), jnp.exp(-m_row)))

            a_row = b_tot - bc_row + li_row
            m_loc = jnp.max(a_row, axis=-1, keepdims=True)
            w_row = jnp.exp(a_row - m_loc)
            kt_h = kt_slab[half * HEAD_DIM:(half + 1) * HEAD_DIM, :]
            ct_loc = _dot((kt_h * w_row).astype(BF16), v_aug)
            m_new = jnp.maximum(b_tot + m_prev, m_loc)
            s_old = jnp.exp(b_tot + m_prev - m_new)
            s_loc = jnp.exp(m_loc - m_new)
            new_ct.append(s_old * ct_pair[half * HEAD_DIM:(half + 1) * HEAD_DIM, :] + s_loc * ct_loc)
            m_ref[h:h + 1, :] = jnp.broadcast_to(m_new, (1, LANES))

        ct_ref[p] = jnp.concatenate(new_ct, axis=0)
        hh = jnp.where(lo, halves[0], halves[1])
        hn = _half_layer_norm(hh, lo) * g_ref[:, sl]
        out_ref[:, sl] = (hn * jax.nn.sigmoid(o_ref[:, sl])).astype(out_ref.dtype)


def _mlstm(proj3, conv_w, norm_g):
    b, s, _ = proj3.shape
    assert b % SEQS_PER_STEP == 0
    blk = lambda w, off: pl.BlockSpec((SEQS_PER_STEP, CHUNK, w), lambda i, c: (i, c, off // w))
    return pl.pallas_call(
        _mlstm_kernel,
        grid=(b // SEQS_PER_STEP, s // CHUNK),
        in_specs=[blk(2 * MLSTM_W, QK_OFF), blk(MLSTM_W, V_OFF), blk(MLSTM_W, O_OFF), blk(LANES, GATE_OFF),
                  pl.BlockSpec((CONV_WIDTH, 2 * MLSTM_W), lambda i, c: (0, 0)),
                  pl.BlockSpec((1, MLSTM_W), lambda i, c: (0, 0))],
        out_specs=pl.BlockSpec((SEQS_PER_STEP, CHUNK, MLSTM_W), lambda i, c: (i, c, 0)),
        out_shape=jax.ShapeDtypeStruct((b, s, MLSTM_W), BF16),
        scratch_shapes=[pltpu.VMEM((SEQS_PER_STEP, 8 + CHUNK, 2 * MLSTM_W), F32),
                        pltpu.VMEM((SEQS_PER_STEP, MLSTM_HEADS // 2, LANES, LANES), F32),
                        pltpu.VMEM((SEQS_PER_STEP, 8, LANES), F32)],
        compiler_params=_params("parallel", "arbitrary"),
        name="mlstm",
    )(proj3, proj3, proj3, proj3, conv_w, norm_g)


def _rope(x, cos_t, sin_t, first):
    return x * cos_t + jnp.where(first, pltpu.roll(x, LANES - ROPE_DIM // 2, 1),
                                 pltpu.roll(x, ROPE_DIM // 2, 1)) * sin_t


def _swa_kernel(sink_ref, q_ref, k_ref, v_ref, cos_ref, sin_ref, out_ref, kt_ref, vv_ref):
    @pl.when(pl.program_id(1) == 0)
    def _():
        kt_ref[...] = jnp.zeros_like(kt_ref)
        vv_ref[...] = jnp.zeros_like(vv_ref)

    for i in range(SEQS_PER_STEP):
        _swa_block(sink_ref, q_ref.at[i], k_ref.at[i], v_ref.at[i], cos_ref.at[i], sin_ref.at[i], out_ref.at[i],
                   kt_ref.at[i], vv_ref.at[i])


def _swa_block(sink_ref, q_ref, k_ref, v_ref, cos_ref, sin_ref, out_ref, kt_ref, vv_ref):
    c = pl.program_id(1)
    cos_t = cos_ref[...].T
    sin_t = sin_ref[...].T
    lane =lax.broadcasted_iota(jnp.int32, (CHUNK, LANES), 1)
    lo = lane < HEAD_DIM
    first = (lane % HEAD_DIM) < ROPE_DIM // 2

    k_cur = _rope(k_ref[...], cos_t, sin_t, first)
    kt_ref[:, CHUNK:2 * CHUNK] = k_cur.T.astype(BF16)
    vv_ref[CHUNK:2 * CHUNK, :] = v_ref[...].astype(BF16)
    kt = kt_ref[...]
    vv = vv_ref[...]

    row = lax.broadcasted_iota(jnp.int32, (CHUNK, 2 * CHUNK), 0)
    col = lax.broadcasted_iota(jnp.int32, (CHUNK, 2 * CHUNK), 1)
    first_block_shift = jnp.where(c > 0, 0, 2 * CHUNK)
    visible = jnp.logical_or(jnp.logical_and(col < CHUNK, col > row + first_block_shift),
                             jnp.logical_and(col >= CHUNK, col - CHUNK <= row))

    for j in range(ATTN_Q_HEADS // 2):
        sl = slice(j * LANES, (j + 1) * LANES)
        q_slab = _rope(q_ref[:, sl], cos_t, sin_t, first) * (HEAD_DIM ** -0.5)
        outs = []
        for half in range(2):
            sel = lo if half == 0 else jnp.logical_not(lo)
            sink = sink_ref[ATTN_HEAD_ORDER[2 * j + half]]
            q_m = jnp.where(sel, q_slab, 0.0).astype(BF16)
            s = jnp.where(visible, _dot(q_m, kt), -jnp.inf)
            mx = jnp.maximum(jnp.max(s, axis=-1, keepdims=True), sink)
            pexp = jnp.exp(s - mx)
            denom = jnp.sum(pexp, axis=-1, keepdims=True) + jnp.exp(sink - mx)
            outs.append(_dot(pexp.astype(BF16), vv) / denom)
        out_ref[:, sl] = jnp.where(lo, outs[0], outs[1]).astype(out_ref.dtype)

    kt_ref[:, 0:CHUNK] = kt_ref[:, CHUNK:2 * CHUNK]
    vv_ref[0:CHUNK, :] = vv_ref[CHUNK:2 * CHUNK, :]


def _swa(proj3, cos_t, sin_t, sinks):
    b, s, _ = proj3.shape
    assert b % SEQS_PER_STEP == 0
    blk = lambda w, off: pl.BlockSpec((SEQS_PER_STEP, CHUNK, w), lambda i, c: (i, c, off // w))
    tab = pl.BlockSpec((SEQS_PER_STEP, LANES, CHUNK), lambda i, c: (i, 0, c))
    return pl.pallas_call(
        _swa_kernel,
        grid=(b // SEQS_PER_STEP, s // CHUNK),
        in_specs=[pl.BlockSpec(memory_space=pltpu.SMEM),
                  blk(ATTN_W, AQ_OFF), blk(ATTN_KV_W, AK_OFF), blk(ATTN_KV_W, AV_OFF), tab, tab],
        out_specs=pl.BlockSpec((SEQS_PER_STEP, CHUNK, ATTN_W), lambda i, c: (i, c, 0)),
        out_shape=jax.ShapeDtypeStruct((b, s, ATTN_W), BF16),
        scratch_shapes=[pltpu.VMEM((SEQS_PER_STEP, LANES, 2 * CHUNK), BF16),
                        pltpu.VMEM((SEQS_PER_STEP, 2 * CHUNK, LANES), BF16)],
        compiler_params=_params("parallel", "arbitrary"),
        name="swa",
    )(sinks, proj3, proj3, proj3, cos_t, sin_t)


def _sgu_kernel(u_ref, v_ref, w_ref, bias_ref, g_ref, b_ref, out_ref):
    lo = _lane_lo((CHUNK, LANES))
    for c in range(SGU_CHUNKS_PER_STEP):
        rows = slice(c * CHUNK, (c + 1) * CHUNK)
        for j in range(SGU_GROUPS // 2):
            sl = slice(j * LANES, (j + 1) * LANES)
            u = jax.nn.gelu(u_ref[rows, sl])
            v = jax.nn.gelu(v_ref[rows, sl])
            vn = (_half_layer_norm(v, lo) * g_ref[:, sl] + b_ref[:, sl]).astype(BF16)
            mixed = jnp.where(lo, _dot(w_ref[2 * j], vn), _dot(w_ref[2 * j + 1], vn)) + bias_ref[:, sl]
            out_ref[rows, sl] = (u * mixed).astype(out_ref.dtype)


def _sgu(proj3, w_tril, bias_tok, norm_g, norm_b):
    b, s, _ = proj3.shape
    rows = SGU_CHUNKS_PER_STEP * CHUNK
    assert s % rows == 0
    blk = lambda w, off: pl.BlockSpec((None, rows, w), lambda i, c: (i, c, off // w))
    const2 = lambda shape: pl.BlockSpec(shape, lambda i, c: (0,) * len(shape))
    return pl.pallas_call(
        _sgu_kernel,
        grid=(b, s // rows),
        in_specs=[blk(SGU_W, SU_OFF), blk(SGU_W, SV_OFF), const2((SGU_GROUPS, CHUNK, CHUNK)),
                  const2((CHUNK, SGU_W)), const2((1, SGU_W)), const2((1, SGU_W))],
        out_specs=pl.BlockSpec((None, rows, SGU_W), lambda i, c: (i, c, 0)),
        out_shape=jax.ShapeDtypeStruct((b, s, SGU_W), BF16),
        compiler_params=_params("parallel", "parallel"),
        name="sgu",
    )(proj3, proj3, w_tril, bias_tok, norm_g, norm_b)


def _outproj_kernel(ha_ref, hb_ref, hc_ref, x_ref, wa_ref, wb_ref, wc_ref, g_ref, b_ref, o_ref):
    mix = _dot(ha_ref[...], wa_ref[...]) + _dot(hb_ref[...], wb_ref[...]) + _dot(hc_ref[...], wc_ref[...])
    o_ref[...] = _layer_norm_rows(DN_ALPHA * x_ref[...] + mix, g_ref[...], b_ref[...])


def _outproj_ln(ha, hb, hc, x2d, wa, wb, wc, g, b):
    n = x2d.shape[0]
    rows = lambda w: pl.BlockSpec((ROW_TILE, w), lambda i: (i, 0))
    const = lambda shape: pl.BlockSpec(shape, lambda i: (0, 0))
    return pl.pallas_call(
        _outproj_kernel,
        grid=(n // ROW_TILE,),
        in_specs=[rows(MLSTM_W), rows(ATTN_W), rows(SGU_W), rows(D_MODEL),
                  const((MLSTM_W, D_MODEL)), const((ATTN_W, D_MODEL)), const((SGU_W, D_MODEL)),
                  const((1, D_MODEL)), const((1, D_MODEL))],
        out_specs=rows(D_MODEL),
        out_shape=jax.ShapeDtypeStruct((n, D_MODEL), F32),
        compiler_params=_params("parallel"),
        name="outproj_ln",
    )(ha, hb, hc, x2d, wa, wb, wc, g, b)


def _ffn_kernel(x_ref, wgu_ref, wd_ref, g_ref, b_ref, o_ref, xb_ref):
    j = pl.program_id(1)
    last = pl.num_programs(1) - 1
    n_sub = FFN_ROWS // ROW_TILE

    def sub_rows(m):
        return pl.ds(pl.multiple_of(m * ROW_TILE, ROW_TILE), ROW_TILE)

    def ffn_tile(m):
        gu = _dot(xb_ref[sub_rows(m), :], wgu_ref[...])
        hidden = (jax.nn.silu(gu[:, :FF_TILE]) * gu[:, FF_TILE:]).astype(BF16)
        return _dot(hidden, wd_ref[...])

    @pl.when(j == 0)
    def _():
        def body(m, carry):
            xb_ref[sub_rows(m), :] = x_ref[sub_rows(m), :].astype(BF16)
            o_ref[sub_rows(m), :] = ffn_tile(m)
            return carry

        lax.fori_loop(0, n_sub, body, 0)

    @pl.when(jnp.logical_and(j > 0, j < last))
    def _():
        def body(m, carry):
            o_ref[sub_rows(m), :] += ffn_tile(m)
            return carry

        lax.fori_loop(0, n_sub, body, 0)

    @pl.when(j == last)
    def _():
        def body(m, carry):
            rows = sub_rows(m)
            ff = o_ref[rows, :] + ffn_tile(m)
            o_ref[rows, :] = _layer_norm_rows(DN_ALPHA * x_ref[rows, :] + ff, g_ref[...], b_ref[...])
            return carry

        lax.fori_loop(0, n_sub, body, 0)


def _ffn_ln(x2d, wgu, wd, g, b):
    n = x2d.shape[0]
    d_ff = wd.shape[0]
    assert d_ff // FF_TILE >= 2
    tm = FFN_ROWS
    return pl.pallas_call(
        _ffn_kernel,
        grid=(n // tm, d_ff // FF_TILE),
        in_specs=[pl.BlockSpec((tm, D_MODEL), lambda i, j: (i, 0)),
                  pl.BlockSpec((D_MODEL, 2 * FF_TILE), lambda i, j: (0, j)),
                  pl.BlockSpec((FF_TILE, D_MODEL), lambda i, j: (j, 0)),
                  pl.BlockSpec((1, D_MODEL), lambda i, j: (0, 0)),
                  pl.BlockSpec((1, D_MODEL), lambda i, j: (0, 0))],
        out_specs=pl.BlockSpec((tm, D_MODEL), lambda i, j: (i, 0)),
        out_shape=jax.ShapeDtypeStruct((n, D_MODEL), F32),
        scratch_shapes=[pltpu.VMEM((tm, D_MODEL), BF16)],
        compiler_params=_params("parallel", "arbitrary"),
        name="ffn_ln",
    )(x2d, wgu, wd, g, b)


def _interleave_gate_up(wg, wu):
    d_ff = wg.shape[1]
    tiles = d_ff // FF_TILE
    both = jnp.stack([wg.reshape(D_MODEL, tiles, FF_TILE), wu.reshape(D_MODEL, tiles, FF_TILE)], axis=2)
    return both.reshape(D_MODEL, 2 * d_ff).astype(BF16)


def _router_kernel(x_ref, wh_ref, wl_ref, b_ref, o_ref, ot_ref):
    x = x_ref[...]
    xh = x.astype(BF16)
    xl = (x - xh.astype(F32)).astype(BF16)
    logits = _dot(xh, wh_ref[...]) + _dot(xl, wh_ref[...]) + _dot(xh, wl_ref[...]) + b_ref[...]
    lane = lax.broadcasted_iota(jnp.int32, logits.shape, 1)
    logits = jnp.where(lane < N_EXPERTS, logits, -jnp.inf)
    m1 = jnp.max(logits, axis=-1, keepdims=True)
    i1 = jnp.min(jnp.where(logits == m1, lane, LANES), axis=-1, keepdims=True)
    rest = jnp.where(lane == i1, -jnp.inf, logits)
    m2 = jnp.max(rest, axis=-1, keepdims=True)
    i2 = jnp.min(jnp.where(rest == m2, lane, LANES), axis=-1, keepdims=True)
    e2 = jnp.exp(m2 - m1)
    g1 = 1.0 / (1.0 + e2)
    g2 = e2 / (1.0 + e2)
    route = jnp.where(lane == 0, g1, jnp.where(lane == 1, g2, jnp.where(
        lane == 2, i1.astype(F32), jnp.where(lane == 3, i2.astype(F32), 0.0))))
    o_ref[...] = route
    ot_ref[...] = route.T[0:8, :]


def _router(x2d, w_hi, w_lo, b):
    n = x2d.shape[0]
    const = lambda shape: pl.BlockSpec(shape, lambda i: (0, 0))
    return pl.pallas_call(
        _router_kernel,
        grid=(n // ROW_TILE,),
        in_specs=[pl.BlockSpec((ROW_TILE, D_MODEL), lambda i: (i, 0)),
                  const((D_MODEL, LANES)), const((D_MODEL, LANES)), const((1, LANES))],
        out_specs=[pl.BlockSpec((ROW_TILE, LANES), lambda i: (i, 0)), pl.BlockSpec((8, ROW_TILE), lambda i: (0, i))],
        out_shape=[jax.ShapeDtypeStruct((n, LANES), F32), jax.ShapeDtypeStruct((8, n), F32)],
        compiler_params=_params("parallel"),
        name="router",
    )(x2d, w_hi, w_lo, b)


def _dispatch_kernel(slot0_ref, slot1_ref, ends_ref, x_ref, o_hbm, stage_ref, zero_ref, sem, zero_sem):
    t = pl.program_id(0)
    last = pl.num_programs(0) - 1
    s = t % 2

    def wait_tile(ss):
        for _ in range(TOP_K):
            pltpu.make_async_copy(stage_ref.at[ss], o_hbm.at[pl.ds(0, MOE_SUB), :], sem.at[ss]).wait()

    @pl.when(t == 0)
    def _():
        zero_ref[...] = jnp.zeros_like(zero_ref)

        def zero_sub_tile(m, carry):
            dst = o_hbm.at[pl.ds(pl.multiple_of(m * MOE_SUB, MOE_SUB), MOE_SUB), :]
            cp = pltpu.make_async_copy(zero_ref, dst, zero_sem)
            cp.start()
            cp.wait()
            return carry

        for e in range(N_EXPERTS):
            @pl.when(ends_ref[e] >= MOE_SUB)
            def _():
                zero_sub_tile(ends_ref[e] // MOE_SUB - 1, 0)

        lax.fori_loop(ends_ref[N_EXPERTS - 1] // MOE_SUB, o_hbm.shape[0] // MOE_SUB, zero_sub_tile, 0)

    @pl.when(t >= 2)
    def _():
        wait_tile(s)

    stage_ref[s] = x_ref[...]

    def issue(r, carry):
        for slot_ref in (slot0_ref, slot1_ref):
            pltpu.make_async_copy(stage_ref.at[s, pl.ds(r, 1), :],
                                  o_hbm.at[pl.ds(slot_ref[t * MOE_SUB + r], 1), :], sem.at[s]).start()
        return carry

    lax.fori_loop(0, MOE_SUB, issue, 0, unroll=8)

    @pl.when(t == last)
    def _():
        wait_tile(s)
        wait_tile(1 - s)


def _dispatch_rows(x2d, slots, ends, n_rows):
    n = x2d.shape[0]
    assert n // MOE_SUB >= 2
    return pl.pallas_call(
        _dispatch_kernel,
        grid_spec=pltpu.PrefetchScalarGridSpec(
            num_scalar_prefetch=3,
            grid=(n // MOE_SUB,),
            in_specs=[pl.BlockSpec((MOE_SUB, D_MODEL), lambda t, s0, s1, en: (t, 0))],
            out_specs=pl.BlockSpec(memory_space=pl.ANY),
            scratch_shapes=[pltpu.VMEM((2, MOE_SUB, D_MODEL), F32), pltpu.VMEM((MOE_SUB, D_MODEL), F32),
                            pltpu.SemaphoreType.DMA((2,)), pltpu.SemaphoreType.DMA(())]),
        out_shape=jax.ShapeDtypeStruct((n_rows, D_MODEL), F32),
        compiler_params=_params("arbitrary"),
        name="moe_dispatch",
    )(slots[0], slots[1], ends, x2d)


def _moe_kernel(exp_ref, row0_ref, nsub_ref, tail_ref, x_hbm, wg_ref, wu_ref, wd_ref, y_hbm,
                xb_ref, acc_ref, wgu_ref, wdb_ref, stage_ref, in_sem, out_sem):
    v = pl.program_id(0)
    j = pl.program_id(1)
    last = pl.num_programs(1) - 1
    n_sub = nsub_ref[v]
    row0 = row0_ref[v]

    def sub_rows(m):
        return pl.ds(pl.multiple_of(m * MOE_SUB, MOE_SUB), MOE_SUB)

    def hbm_rows(m):
        return pl.ds(pl.multiple_of(row0 + m * MOE_SUB, MOE_SUB), MOE_SUB)

    @pl.when(jnp.logical_and(v == 0, j == 0))
    def _():
        stage_ref[0] = jnp.zeros((MOE_SUB, D_MODEL), F32)

        def zero_sub_tile(m, carry):
            cp = pltpu.make_async_copy(stage_ref.at[0], y_hbm.at[sub_rows(m), :], out_sem)
            cp.start()
            cp.wait()
            return carry

        lax.fori_loop(tail_ref[0], y_hbm.shape[0] // MOE_SUB, zero_sub_tile, 0)

    @pl.when(n_sub > 0)
    def _():
        wgu_ref[:, :MOE_FF_TILE] = wg_ref[...].astype(BF16)
        wgu_ref[:, MOE_FF_TILE:] = wu_ref[...].astype(BF16)
        wdb_ref[...] = wd_ref[...].astype(BF16)

        def ffn_tile(m):
            xs = xb_ref[sub_rows(m), :]
            gu = _dot(xs, wgu_ref[...])
            hidden = (jax.nn.silu(gu[:, :MOE_FF_TILE]) * gu[:, MOE_FF_TILE:]).astype(BF16)
            return _dot(hidden, wdb_ref[...])

        def y_copy(m):
            return pltpu.make_async_copy(acc_ref.at[sub_rows(m), :], y_hbm.at[hbm_rows(m), :], out_sem)

        @pl.when(j == 0)
        def _():
            def x_copy(m):
                return pltpu.make_async_copy(x_hbm.at[hbm_rows(m), :], stage_ref.at[m % 2], in_sem.at[m % 2])

            x_copy(0).start()

            def body(m, carry):
                @pl.when(m + 1 < n_sub)
                def _():
                    x_copy(m + 1).start()

                x_copy(m).wait()
                xb_ref[sub_rows(m), :] = stage_ref[m % 2].astype(BF16)
                acc_ref[sub_rows(m), :] = ffn_tile(m)
                return carry

            lax.fori_loop(0, n_sub, body, 0)

        @pl.when(jnp.logical_and(j > 0, j < last))
        def _():
            def body(m, carry):
                acc_ref[sub_rows(m), :] += ffn_tile(m)
                return carry

            lax.fori_loop(0, n_sub, body, 0)

        @pl.when(j == last)
        def _():
            def body(m, carry):
                acc_ref[sub_rows(m), :] += ffn_tile(m)
                y_copy(m).start()
                return carry

            def drain(m, carry):
                y_copy(m).wait()
                return carry

            lax.fori_loop(0, n_sub, body, 0)
            lax.fori_loop(0, n_sub, drain, 0)


def _moe_grouped(xs, wg, wu, wd, visit_exp, visit_row0, visit_nsub, tail_sub):
    n_rows = xs.shape[0]
    n_visits = visit_exp.shape[0]
    d_ff = wg.shape[2]
    assert d_ff // MOE_FF_TILE >= 2
    rows = MOE_VISIT_SUBS * MOE_SUB
    w_in = lambda v, j, e, r, ns, tl: (e[v], 0, j)
    w_out = lambda v, j, e, r, ns, tl: (e[v], j, 0)
    return pl.pallas_call(
        _moe_kernel,
        grid_spec=pltpu.PrefetchScalarGridSpec(
            num_scalar_prefetch=4,
            grid=(n_visits, d_ff // MOE_FF_TILE),
            in_specs=[pl.BlockSpec(memory_space=pl.ANY),
                      pl.BlockSpec((None, D_MODEL, MOE_FF_TILE), w_in),
                      pl.BlockSpec((None, D_MODEL, MOE_FF_TILE), w_in),
                      pl.BlockSpec((None, MOE_FF_TILE, D_MODEL), w_out)],
            out_specs=pl.BlockSpec(memory_space=pl.ANY),
            scratch_shapes=[pltpu.VMEM((rows, D_MODEL), BF16), pltpu.VMEM((rows, D_MODEL), F32),
                            pltpu.VMEM((D_MODEL, 2 * MOE_FF_TILE), BF16),
                            pltpu.VMEM((MOE_FF_TILE, D_MODEL), BF16),
                            pltpu.VMEM((2, MOE_SUB, D_MODEL), F32),
                            pltpu.SemaphoreType.DMA((2,)), pltpu.SemaphoreType.DMA(())]),
        out_shape=jax.ShapeDtypeStruct((n_rows, D_MODEL), F32),
        compiler_params=_params("arbitrary", "arbitrary"),
        name="moe_grouped",
    )(visit_exp, visit_row0, visit_nsub, tail_sub, xs, wg, wu, wd)


def _combine_kernel(slot0_ref, slot1_ref, x_ref, gate_ref, y_hbm, g_ref, b_ref, o_ref, buf_ref, sem):
    t = pl.program_id(0)
    s = t % 2

    def issue_tile(tt, ss):
        def issue(r, carry):
            for k, slot_ref in enumerate((slot0_ref, slot1_ref)):
                pltpu.make_async_copy(y_hbm.at[pl.ds(slot_ref[tt * MOE_SUB + r], 1), :],
                                      buf_ref.at[ss, k, pl.ds(r, 1), :], sem.at[ss]).start()
            return carry

        lax.fori_loop(0, MOE_SUB, issue, 0, unroll=8)

    @pl.when(t == 0)
    def _():
        issue_tile(0, 0)

    @pl.when(t + 1 < pl.num_programs(0))
    def _():
        issue_tile(t + 1, 1 - s)

    for k in range(TOP_K):
        pltpu.make_async_copy(y_hbm.at[pl.ds(0, MOE_SUB), :], buf_ref.at[s, k], sem.at[s]).wait()
    gate = gate_ref[...]
    ff = gate[:, 0:1] * buf_ref[s, 0] + gate[:, 1:2] * buf_ref[s, 1]
    o_ref[...] = _layer_norm_rows(DN_ALPHA * x_ref[...] + ff, g_ref[...], b_ref[...])


def _combine_ln(x2d, route, slots, ys, g, b):
    n = x2d.shape[0]
    return pl.pallas_call(
        _combine_kernel,
        grid_spec=pltpu.PrefetchScalarGridSpec(
            num_scalar_prefetch=2,
            grid=(n // MOE_SUB,),
            in_specs=[pl.BlockSpec((MOE_SUB, D_MODEL), lambda t, s0, s1: (t, 0)),
                      pl.BlockSpec((MOE_SUB, LANES), lambda t, s0, s1: (t, 0)),
                      pl.BlockSpec(memory_space=pl.ANY),
                      pl.BlockSpec((1, D_MODEL), lambda t, s0, s1: (0, 0)),
                      pl.BlockSpec((1, D_MODEL), lambda t, s0, s1: (0, 0))],
            out_specs=pl.BlockSpec((MOE_SUB, D_MODEL), lambda t, s0, s1: (t, 0)),
            scratch_shapes=[pltpu.VMEM((2, TOP_K, MOE_SUB, D_MODEL), F32), pltpu.SemaphoreType.DMA((2,))]),
        out_shape=jax.ShapeDtypeStruct((n, D_MODEL), F32),
        compiler_params=_params("arbitrary"),
        name="moe_combine_ln",
    )(slots[0], slots[1], x2d, route, ys, g, b)


def _routing_tables(route_t, n):
    idx = route_t[2:4].astype(jnp.int32)
    expert = jnp.arange(N_EXPERTS, dtype=jnp.int32)[:, None]
    chosen = [idx[k][None, :] == expert for k in range(TOP_K)]
    onehot = jnp.logical_or(chosen[0], chosen[1]).astype(jnp.int32)
    rank = jnp.cumsum(onehot, axis=1) - onehot
    counts = jnp.sum(onehot, axis=1)
    padded = ((counts + MOE_SUB - 1) // MOE_SUB) * MOE_SUB
    ends = jnp.cumsum(padded)
    starts = ends - padded
    place = starts[:, None] + rank
    slot = [jnp.sum(jnp.where(chosen[k], place, 0), axis=0).astype(jnp.int32) for k in range(TOP_K)]

    n_rows = -(-(n * TOP_K + N_EXPERTS * (MOE_SUB - 1)) // MOE_SUB) * MOE_SUB

    visit_rows = MOE_VISIT_SUBS * MOE_SUB
    max_chunks = -(-n_rows // visit_rows)
    chunk = jnp.arange(max_chunks, dtype=jnp.int32)[None, :]
    left = padded[:, None] - chunk * visit_rows
    valid = (left > 0).reshape(-1)
    n_visits = n_rows // visit_rows + N_EXPERTS
    order = jnp.argsort(jnp.logical_not(valid), stable=True)[:n_visits]
    n_valid = jnp.sum(valid.astype(jnp.int32))
    live = jnp.arange(n_visits) < n_valid
    order = jnp.where(live, order, order[jnp.maximum(n_valid - 1, 0)])
    v_exp = (order // max_chunks).astype(jnp.int32)
    v_chunk = (order % max_chunks).astype(jnp.int32)
    v_row0 = jnp.where(live, starts[v_exp] + v_chunk * visit_rows, 0).astype(jnp.int32)
    v_nsub = jnp.where(live, jnp.minimum(left.reshape(-1)[order], visit_rows) // MOE_SUB, 0).astype(jnp.int32)
    tail_sub = (ends[N_EXPERTS - 1:] // MOE_SUB).astype(jnp.int32)
    return slot, ends.astype(jnp.int32), n_rows, (v_exp, v_row0, v_nsub, tail_sub)


def _moe_ln(x2d, w_router, b_router, wg, wu, wd, g, b):
    n = x2d.shape[0]
    w_pad = jnp.zeros((D_MODEL, LANES), F32).at[:, :N_EXPERTS].set(w_router)
    w_hi = w_pad.astype(BF16)
    w_lo = (w_pad - w_hi.astype(F32)).astype(BF16)
    b_pad = jnp.zeros((1, LANES), F32).at[0, :N_EXPERTS].set(b_router)
    route, route_t = _router(x2d, w_hi, w_lo, b_pad)
    slots, ends, n_rows, visits = _routing_tables(route_t, n)
    xs = _dispatch_rows(x2d, slots, ends, n_rows)
    ys = _moe_grouped(xs, wg, wu, wd, *visits)
    return _combine_ln(x2d, route, slots, ys, g, b)


def _layout_in_proj(w_in, b_in):
    sizes = (MLSTM_W, MLSTM_W, MLSTM_W, MLSTM_W, MLSTM_HEADS, MLSTM_HEADS,
             ATTN_W, ATTN_KV_W, ATTN_KV_W, SGU_W, SGU_W)
    offs = np.concatenate([[0], np.cumsum(sizes)])
    seg = lambda a, i: a[int(offs[i]):int(offs[i + 1])]

    def build(a):
        tail = a.shape[1:]
        aq = seg(a, 6).reshape((ATTN_Q_HEADS, HEAD_DIM) + tail)[np.array(ATTN_HEAD_ORDER)]
        gates = jnp.concatenate([seg(a, 4), seg(a, 5), jnp.zeros((LANES - 2 * MLSTM_HEADS,) + tail, a.dtype)], 0)
        return jnp.concatenate([seg(a, 0), seg(a, 1), seg(a, 2), seg(a, 3), aq.reshape((ATTN_W,) + tail),
                                gates, seg(a, 7), seg(a, 8), seg(a, 9), seg(a, 10)], 0)

    return build(w_in.T).astype(BF16), build(b_in)[None, :]


def _rope_tables(positions):
    inv_freq = ROPE_THETA ** (-jnp.arange(0, ROPE_DIM, 2, dtype=F32) / ROPE_DIM)
    ang = inv_freq[None, :, None] * positions.astype(F32)[:, None, :]
    cos, sin = jnp.cos(ang), jnp.sin(ang)
    ones = jnp.ones((ang.shape[0], HEAD_DIM - ROPE_DIM, ang.shape[2]), F32)
    cos_head = jnp.concatenate([cos, cos, ones], 1)
    sin_head = jnp.concatenate([-sin, sin, 0.0 * ones], 1)
    return jnp.concatenate([cos_head, cos_head], 1), jnp.concatenate([sin_head, sin_head], 1)


def kernel(x, positions, w_in, b_in, conv_w, mlstm_norm_g, attn_sinks, sgu_w_s, sgu_b_s, sgu_norm_g, sgu_norm_b, w_out, ln1_g, ln1_b, ln2_g, ln2_b, ffn_w_gate, ffn_w_up, ffn_w_down, moe_w_router, moe_b_router, moe_w_gate, moe_w_up, moe_w_down):
    bsz, seq, _ = x.shape
    n = bsz * seq
    cos_t, sin_t = _rope_tables(positions)
    tril = jnp.tril(jnp.ones((CHUNK, CHUNK), bool))
    x2d = x.reshape(n, D_MODEL)
    for layer in range(DEPTH):
        w_p, b_p = _layout_in_proj(w_in[layer], b_in[layer])
        proj3 = _inproj(x2d, w_p, b_p).reshape(bsz, seq, D_PROJ_PAD)
        h_a = _mlstm(proj3, conv_w[layer], mlstm_norm_g[layer][None, :])
        h_b = _swa(proj3, cos_t, sin_t, attn_sinks[layer])
        w_tril = jnp.where(tril, sgu_w_s[layer], 0.0).astype(BF16)
        bias_tok = jnp.repeat(sgu_b_s[layer].T, HEAD_DIM, axis=1)
        h_c = _sgu(proj3, w_tril, bias_tok, sgu_norm_g[layer][None, :], sgu_norm_b[layer][None, :])
        wo = w_out[layer]
        wa = wo[:MLSTM_W].astype(BF16)
        wb = wo[MLSTM_W:MLSTM_W + ATTN_W].reshape(ATTN_Q_HEADS, HEAD_DIM, D_MODEL)[np.array(ATTN_HEAD_ORDER)]
        wb = wb.reshape(ATTN_W, D_MODEL).astype(BF16)
        wc = wo[MLSTM_W + ATTN_W:].astype(BF16)
        x2d = _outproj_ln(h_a.reshape(n, MLSTM_W), h_b.reshape(n, ATTN_W), h_c.reshape(n, SGU_W), x2d,
                          wa, wb, wc, ln1_g[layer][None, :], ln1_b[layer][None, :])
        j = layer // 2
        g2, b2 = ln2_g[layer][None, :], ln2_b[layer][None, :]
        if layer % 2 == 0:
            x2d = _ffn_ln(x2d, _interleave_gate_up(ffn_w_gate[j], ffn_w_up[j]),
                          ffn_w_down[j].astype(BF16), g2, b2)
        else:
            x2d = _moe_ln(x2d, moe_w_router[j], moe_b_router[j], moe_w_gate[j], moe_w_up[j],
                          moe_w_down[j], g2, b2)
    return x2d.reshape(bsz, seq, D_MODEL)
```

```python
import functools

import jax
import jax.numpy as jnp
import numpy as np
from jax import lax
from jax.experimental import pallas as pl
from jax.experimental.pallas import tpu as pltpu

F32 = jnp.float32
BF16 = jnp.bfloat16

D_MODEL = 1024
HEAD_DIM = 64
LANES = 128
MLSTM_HEADS = 6
ATTN_Q_HEADS = 6
ATTN_KV_HEADS = 2
SGU_GROUPS = 4
MLSTM_W = MLSTM_HEADS * HEAD_DIM
ATTN_W = ATTN_Q_HEADS * HEAD_DIM
ATTN_KV_W = ATTN_KV_HEADS * HEAD_DIM
SGU_W = SGU_GROUPS * HEAD_DIM
CHUNK = 128
CONV_WIDTH = 4
ROPE_DIM = HEAD_DIM // 4
ROPE_THETA = 500000.0
N_EXPERTS = 8
TOP_K = 2
DEPTH = 2
DN_ALPHA = (2.0 * DEPTH) ** 0.25
LN_EPS = 1e-5

QK_OFF, V_OFF, O_OFF, AQ_OFF = 0, 768, 1152, 1536
GATE_OFF, AK_OFF, AV_OFF, SU_OFF, SV_OFF = 1920, 2048, 2176, 2304, 2560
D_PROJ_PAD = 2816
ATTN_HEAD_ORDER = (0, 3, 1, 4, 2, 5)

VMEM_LIMIT = 56 * 1024 * 1024

SGU_CHUNKS_PER_STEP = 4
SEQS_PER_STEP = 1
ROW_TILE = 512
FFN_ROWS = 2048
FF_TILE = 256
MOE_SUB = 512
MOE_VISIT_SUBS = 9
MOE_FF_TILE = 512


def _params(*sem):
    return pltpu.CompilerParams(dimension_semantics=sem, vmem_limit_bytes=VMEM_LIMIT)


def _lane_lo(shape):
    return lax.broadcasted_iota(jnp.int32, shape, len(shape) - 1) < HEAD_DIM


def _layer_norm_rows(z, g, b):
    mu = jnp.mean(z, axis=-1, keepdims=True)
    zc = z - mu
    var = jnp.mean(zc * zc, axis=-1, keepdims=True)
    return zc * lax.rsqrt(var + LN_EPS) * g + b


def _half_layer_norm(x, lo):
    inv = 1.0 / HEAD_DIM
    s_lo = jnp.sum(jnp.where(lo, x, 0.0), axis=-1, keepdims=True)
    s_all = jnp.sum(x, axis=-1, keepdims=True)
    mu = jnp.where(lo, s_lo, s_all - s_lo) * inv
    xc = x - mu
    sq = xc * xc
    q_lo = jnp.sum(jnp.where(lo, sq, 0.0), axis=-1, keepdims=True)
    q_all = jnp.sum(sq, axis=-1, keepdims=True)
    var = jnp.where(lo, q_lo, q_all - q_lo) * inv
    return xc * lax.rsqrt(var + LN_EPS)


def _split3(a):
    h1 = a.astype(BF16)
    r1 = a - h1.astype(F32)
    h2 = r1.astype(BF16)
    r2 = r1 - h2.astype(F32)
    return h1, h2, r2.astype(BF16)


def _dot(a, b):
    return jnp.dot(a, b, preferred_element_type=F32)


def _inproj_kernel(x_ref, w_ref, b_ref, o_ref):
    o_ref[...] = lax.dot_general(x_ref[...].astype(BF16), w_ref[...], (((1,), (1,)), ((), ())),
                                 preferred_element_type=F32) + b_ref[...]


def _inproj(x2d, w, b):
    n = x2d.shape[0]
    return pl.pallas_call(
        _inproj_kernel,
        grid=(n // ROW_TILE,),
        in_specs=[pl.BlockSpec((ROW_TILE, D_MODEL), lambda i: (i, 0)),
                  pl.BlockSpec((D_PROJ_PAD, D_MODEL), lambda i: (0, 0)),
                  pl.BlockSpec((1, D_PROJ_PAD), lambda i: (0, 0))],
        out_specs=pl.BlockSpec((ROW_TILE, D_PROJ_PAD), lambda i: (i, 0)),
        out_shape=jax.ShapeDtypeStruct((n, D_PROJ_PAD), F32),
        compiler_params=_params("parallel"),
        name="inproj",
    )(x2d, w, b)


def _mlstm_kernel(qk_ref, v_ref, o_ref, gate_ref, cw_ref, g_ref, out_ref, buf_ref, ct_ref, m_ref):
    @pl.when(pl.program_id(1) == 0)
    def _():
        buf_ref[:, 0:8, :] = jnp.zeros((SEQS_PER_STEP, 8, 2 * MLSTM_W), F32)
        ct_ref[...] = jnp.zeros_like(ct_ref)
        m_ref[...] = jnp.zeros_like(m_ref)

    for i in range(SEQS_PER_STEP):
        _mlstm_chunk(qk_ref.at[i], v_ref.at[i], o_ref.at[i], gate_ref.at[i], cw_ref, g_ref, out_ref.at[i],
                     buf_ref.at[i], ct_ref.at[i], m_ref.at[i])


def _mlstm_chunk(qk_ref, v_ref, o_ref, gate_ref, cw_ref, g_ref, out_ref, buf_ref, ct_ref, m_ref):
    buf_ref[8:8 + CHUNK, :] = qk_ref[...]
    cw = cw_ref[...]
    base = 8 - (CONV_WIDTH - 1)
    acc = cw[0:1, :] * buf_ref[base:base + CHUNK, :]
    for j in range(1, CONV_WIDTH):
        acc = acc + cw[j:j + 1, :] * buf_ref[base + j:base + j + CHUNK, :]
    buf_ref[0:8, :] = qk_ref[CHUNK - 8:CHUNK, :]
    qk = jax.nn.silu(acc)

    gates = gate_ref[...]
    lf = jax.nn.log_sigmoid(gates)
    row = lax.broadcasted_iota(jnp.int32, (CHUNK, CHUNK), 0)
    col = lax.broadcasted_iota(jnp.int32, (CHUNK, CHUNK), 1)
    causal = col <= row
    tri = causal.astype(BF16)
    l1, l2, l3 = _split3(lf)
    cum = _dot(tri, l1) + _dot(tri, l2) + _dot(tri, l3)
    gates_t = gates.T
    cum_t = cum.T

    lo = _lane_lo((CHUNK, LANES))
    for p in range(MLSTM_HEADS // 2):
        sl = slice(p * LANES, (p + 1) * LANES)
        q_slab = qk[:, sl] * (HEAD_DIM ** -0.5)
        k_slab = qk[:, MLSTM_W + p * LANES:MLSTM_W + (p + 1) * LANES]
        kt_slab = k_slab.T
        v_slab = v_ref[:, sl]
        ct_pair = ct_ref[p]
        ct_pair_b = ct_pair.astype(BF16)
        halves = []
        new_ct = []
        for half in range(2):
            h = 2 * p + half
            sel = lo if half == 0 else jnp.logical_not(lo)
            li_row = gates_t[h:h + 1, :]
            bc_row = cum_t[MLSTM_HEADS + h:MLSTM_HEADS + h + 1, :]
            bc_col = cum[:, MLSTM_HEADS + h:MLSTM_HEADS + h + 1]
            b_tot = bc_row[:, CHUNK - 1:CHUNK]
            m_prev = m_ref[h:h + 1, 0:1]

            dmat = jnp.where(causal, bc_col - bc_row + li_row, -jnp.inf)
            m_inter = bc_col + m_prev
            m_row = jnp.maximum(m_inter, jnp.max(dmat, axis=-1, keepdims=True))
            q_m = jnp.where(sel, q_slab, 0.0).astype(BF16)
            s = _dot(q_m, kt_slab.astype(BF16))
            pmat = (s * jnp.exp(dmat - m_row)).astype(BF16)
            w_inter = jnp.exp(m_inter - m_row)
            v_aug = jnp.where(sel, v_slab, 1.0).astype(BF16)
            r = _dot(pmat, v_aug) + w_inter * _dot(q_m, ct_pair_b)
            den_lane = HEAD_DIM if half == 0 else 0
            den = r[:, den_lane:den_lane + 1]
            halves.append(r / jnp.maximum(jnp.abs(den), jnp.exp(-m_row)))

            a_row = b_tot - bc_row + li_row
            m_loc = jnp.max(a_row, axis=-1, keepdims=True)
            w_row = jnp.exp(a_row - m_loc)
            kt_h = kt_slab[half * HEAD_DIM:(half + 1) * HEAD_DIM, :]
            ct_loc = _dot((kt_h * w_row).astype(BF16), v_aug)
            m_new = jnp.maximum(b_tot + m_prev, m_loc)
            s_old = jnp.exp(b_tot + m_prev - m_new)
            s_loc = jnp.exp(m_loc - m_new)
            new_ct.append(s_old * ct_pair[half * HEAD_DIM:(half + 1) * HEAD_DIM, :] + s_loc * ct_loc)
            m_ref[h:h + 1, :] = jnp.broadcast_to(m_new, (1, LANES))

        ct_ref[p] = jnp.concatenate(new_ct, axis=0)
        hh = jnp.where(lo, halves[0], halves[1])
        hn = _half_layer_norm(hh, lo) * g_ref[:, sl]
        out_ref[:, sl] = (hn * jax.nn.sigmoid(o_ref[:, sl])).astype(out_ref.dtype)


def _mlstm(proj3, conv_w, norm_g):
    b, s, _ = proj3.shape
    assert b % SEQS_PER_STEP == 0
    blk = lambda w, off: pl.BlockSpec((SEQS_PER_STEP, CHUNK, w), lambda i, c: (i, c, off // w))
    return pl.pallas_call(
        _mlstm_kernel,
        grid=(b // SEQS_PER_STEP, s // CHUNK),
        in_specs=[blk(2 * MLSTM_W, QK_OFF), blk(MLSTM_W, V_OFF), blk(MLSTM_W, O_OFF), blk(LANES, GATE_OFF),
                  pl.BlockSpec((CONV_WIDTH, 2 * MLSTM_W), lambda i, c: (0, 0)),
                  pl.BlockSpec((1, MLSTM_W), lambda i, c: (0, 0))],
        out_specs=pl.BlockSpec((SEQS_PER_STEP, CHUNK, MLSTM_W), lambda i, c: (i, c, 0)),
        out_shape=jax.ShapeDtypeStruct((b, s, MLSTM_W), BF16),
        scratch_shapes=[pltpu.VMEM((SEQS_PER_STEP, 8 + CHUNK, 2 * MLSTM_W), F32),
                        pltpu.VMEM((SEQS_PER_STEP, MLSTM_HEADS // 2, LANES, LANES), F32),
                        pltpu.VMEM((SEQS_PER_STEP, 8, LANES), F32)],
        compiler_params=_params("parallel", "arbitrary"),
        name="mlstm",
    )(proj3, proj3, proj3, proj3, conv_w, norm_g)


def _rope(x, cos_t, sin_t, first):
    return x * cos_t + jnp.where(first, pltpu.roll(x, LANES - ROPE_DIM // 2, 1),
                                 pltpu.roll(x, ROPE_DIM // 2, 1)) * sin_t


def _swa_kernel(sink_ref, q_ref, k_ref, v_ref, cos_ref, sin_ref, out_ref, kt_ref, vv_ref):
    @pl.when(pl.program_id(1) == 0)
    def _():
        kt_ref[...] = jnp.zeros_like(kt_ref)
        vv_ref[...] = jnp.zeros_like(vv_ref)

    for i in range(SEQS_PER_STEP):
        _swa_block(sink_ref, q_ref.at[i], k_ref.at[i], v_ref.at[i], cos_ref.at[i], sin_ref.at[i], out_ref.at[i],
                   kt_ref.at[i], vv_ref.at[i])


def _swa_block(sink_ref, q_ref, k_ref, v_ref, cos_ref, sin_ref, out_ref, kt_ref, vv_ref):
    c = pl.program_id(1)
    cos_t = cos_ref[...].T
    sin_t = sin_ref[...].T
    lane =lax.broadcasted_iota(jnp.int32, (CHUNK, LANES), 1)
    lo = lane < HEAD_DIM
    first = (lane % HEAD_DIM) < ROPE_DIM // 2

    k_cur = _rope(k_ref[...], cos_t, sin_t, first)
    kt_ref[:, CHUNK:2 * CHUNK] = k_cur.T.astype(BF16)
    vv_ref[CHUNK:2 * CHUNK, :] = v_ref[...].astype(BF16)
    kt = kt_ref[...]
    vv = vv_ref[...]

    row = lax.broadcasted_iota(jnp.int32, (CHUNK, 2 * CHUNK), 0)
    col = lax.broadcasted_iota(jnp.int32, (CHUNK, 2 * CHUNK), 1)
    first_block_shift = jnp.where(c > 0, 0, 2 * CHUNK)
    visible = jnp.logical_or(jnp.logical_and(col < CHUNK, col > row + first_block_shift),
                             jnp.logical_and(col >= CHUNK, col - CHUNK <= row))

    for j in range(ATTN_Q_HEADS // 2):
        sl = slice(j * LANES, (j + 1) * LANES)
        q_slab = _rope(q_ref[:, sl], cos_t, sin_t, first) * (HEAD_DIM ** -0.5)
        outs = []
        for half in range(2):
            sel = lo if half == 0 else jnp.logical_not(lo)
            sink = sink_ref[ATTN_HEAD_ORDER[2 * j + half]]
            q_m = jnp.where(sel, q_slab, 0.0).astype(BF16)
            s = jnp.where(visible, _dot(q_m, kt), -jnp.inf)
            mx = jnp.maximum(jnp.max(s, axis=-1, keepdims=True), sink)
            pexp = jnp.exp(s - mx)
            denom = jnp.sum(pexp, axis=-1, keepdims=True) + jnp.exp(sink - mx)
            outs.append(_dot(pexp.astype(BF16), vv) / denom)
        out_ref[:, sl] = jnp.where(lo, outs[0], outs[1]).astype(out_ref.dtype)

    kt_ref[:, 0:CHUNK] = kt_ref[:, CHUNK:2 * CHUNK]
    vv_ref[0:CHUNK, :] = vv_ref[CHUNK:2 * CHUNK, :]


def _swa(proj3, cos_t, sin_t, sinks):
    b, s, _ = proj3.shape
    assert b % SEQS_PER_STEP == 0
    blk = lambda w, off: pl.BlockSpec((SEQS_PER_STEP, CHUNK, w), lambda i, c: (i, c, off // w))
    tab = pl.BlockSpec((SEQS_PER_STEP, LANES, CHUNK), lambda i, c: (i, 0, c))
    return pl.pallas_call(
        _swa_kernel,
        grid=(b // SEQS_PER_STEP, s // CHUNK),
        in_specs=[pl.BlockSpec(memory_space=pltpu.SMEM),
                  blk(ATTN_W, AQ_OFF), blk(ATTN_KV_W, AK_OFF), blk(ATTN_KV_W, AV_OFF), tab, tab],
        out_specs=pl.BlockSpec((SEQS_PER_STEP, CHUNK, ATTN_W), lambda i, c: (i, c, 0)),
        out_shape=jax.ShapeDtypeStruct((b, s, ATTN_W), BF16),
        scratch_shapes=[pltpu.VMEM((SEQS_PER_STEP, LANES, 2 * CHUNK), BF16),
                        pltpu.VMEM((SEQS_PER_STEP, 2 * CHUNK, LANES), BF16)],
        compiler_params=_params("parallel", "arbitrary"),
        name="swa",
    )(sinks, proj3, proj3, proj3, cos_t, sin_t)


def _sgu_kernel(u_ref, v_ref, w_ref, bias_ref, g_ref, b_ref, out_ref):
    lo = _lane_lo((CHUNK, LANES))
    for c in range(SGU_CHUNKS_PER_STEP):
        rows = slice(c * CHUNK, (c + 1) * CHUNK)
        for j in range(SGU_GROUPS // 2):
            sl = slice(j * LANES, (j + 1) * LANES)
            u = jax.nn.gelu(u_ref[rows, sl])
            v = jax.nn.gelu(v_ref[rows, sl])
            vn = (_half_layer_norm(v, lo) * g_ref[:, sl] + b_ref[:, sl]).astype(BF16)
            mixed = jnp.where(lo, _dot(w_ref[2 * j], vn), _dot(w_ref[2 * j + 1], vn)) + bias_ref[:, sl]
            out_ref[rows, sl] = (u * mixed).astype(out_ref.dtype)


def _sgu(proj3, w_tril, bias_tok, norm_g, norm_b):
    b, s, _ = proj3.shape
    rows = SGU_CHUNKS_PER_STEP * CHUNK
    assert s % rows == 0
    blk = lambda w, off: pl.BlockSpec((None, rows, w), lambda i, c: (i, c, off // w))
    const2 = lambda shape: pl.BlockSpec(shape, lambda i, c: (0,) * len(shape))
    return pl.pallas_call(
        _sgu_kernel,
        grid=(b, s // rows),
        in_specs=[blk(SGU_W, SU_OFF), blk(SGU_W, SV_OFF), const2((SGU_GROUPS, CHUNK, CHUNK)),
                  const2((CHUNK, SGU_W)), const2((1, SGU_W)), const2((1, SGU_W))],
        out_specs=pl.BlockSpec((None, rows, SGU_W), lambda i, c: (i, c, 0)),
        out_shape=jax.ShapeDtypeStruct((b, s, SGU_W), BF16),
        compiler_params=_params("parallel", "parallel"),
        name="sgu",
    )(proj3, proj3, w_tril, bias_tok, norm_g, norm_b)


def _outproj_kernel(ha_ref, hb_ref, hc_ref, x_ref, wa_ref, wb_ref, wc_ref, g_ref, b_ref, o_ref):
    mix = _dot(ha_ref[...], wa_ref[...]) + _dot(hb_ref[...], wb_ref[...]) + _dot(hc_ref[...], wc_ref[...])
    o_ref[...] = _layer_norm_rows(DN_ALPHA * x_ref[...] + mix, g_ref[...], b_ref[...])


def _outproj_ln(ha, hb, hc, x2d, wa, wb, wc, g, b):
    n = x2d.shape[0]
    rows = lambda w: pl.BlockSpec((ROW_TILE, w), lambda i: (i, 0))
    const = lambda shape: pl.BlockSpec(shape, lambda i: (0, 0))
    return pl.pallas_call(
        _outproj_kernel,
        grid=(n // ROW_TILE,),
        in_specs=[rows(MLSTM_W), rows(ATTN_W), rows(SGU_W), rows(D_MODEL),
                  const((MLSTM_W, D_MODEL)), const((ATTN_W, D_MODEL)), const((SGU_W, D_MODEL)),
                  const((1, D_MODEL)), const((1, D_MODEL))],
        out_specs=rows(D_MODEL),
        out_shape=jax.ShapeDtypeStruct((n, D_MODEL), F32),
        compiler_params=_params("parallel"),
        name="outproj_ln",
    )(ha, hb, hc, x2d, wa, wb, wc, g, b)


def _ffn_kernel(x_ref, wg_ref, wu_ref, wd_ref, g_ref, b_ref, o_ref, xb_ref, wgu_ref):
    j = pl.program_id(1)
    last = pl.num_programs(1) - 1
    n_sub = FFN_ROWS // ROW_TILE
    wgu_ref[:, :FF_TILE] = wg_ref[...]
    wgu_ref[:, FF_TILE:] = wu_ref[...]

    def sub_rows(m):
        return pl.ds(pl.multiple_of(m * ROW_TILE, ROW_TILE), ROW_TILE)

    def ffn_tile(m):
        gu = _dot(xb_ref[sub_rows(m), :], wgu_ref[...])
        hidden = (jax.nn.silu(gu[:, :FF_TILE]) * gu[:, FF_TILE:]).astype(BF16)
        return _dot(hidden, wd_ref[...])

    @pl.when(j == 0)
    def _():
        def body(m, carry):
            xb_ref[sub_rows(m), :] = x_ref[sub_rows(m), :].astype(BF16)
            o_ref[sub_rows(m), :] = ffn_tile(m)
            return carry

        lax.fori_loop(0, n_sub, body, 0)

    @pl.when(jnp.logical_and(j > 0, j < last))
    def _():
        def body(m, carry):
            o_ref[sub_rows(m), :] += ffn_tile(m)
            return carry

        lax.fori_loop(0, n_sub, body, 0)

    @pl.when(j == last)
    def _():
        def body(m, carry):
            rows = sub_rows(m)
            ff = o_ref[rows, :] + ffn_tile(m)
            o_ref[rows, :] = _layer_norm_rows(DN_ALPHA * x_ref[rows, :] + ff, g_ref[...], b_ref[...])
            return carry

        lax.fori_loop(0, n_sub, body, 0)


def _ffn_ln(x2d, wg, wu, wd, g, b):
    n = x2d.shape[0]
    d_ff = wd.shape[0]
    assert d_ff // FF_TILE >= 2
    tm = FFN_ROWS
    return pl.pallas_call(
        _ffn_kernel,
        grid=(n // tm, d_ff // FF_TILE),
        in_specs=[pl.BlockSpec((tm, D_MODEL), lambda i, j: (i, 0)),
                  pl.BlockSpec((D_MODEL, FF_TILE), lambda i, j: (0, j)),
                  pl.BlockSpec((D_MODEL, FF_TILE), lambda i, j: (0, j)),
                  pl.BlockSpec((FF_TILE, D_MODEL), lambda i, j: (j, 0)),
                  pl.BlockSpec((1, D_MODEL), lambda i, j: (0, 0)),
                  pl.BlockSpec((1, D_MODEL), lambda i, j: (0, 0))],
        out_specs=pl.BlockSpec((tm, D_MODEL), lambda i, j: (i, 0)),
        out_shape=jax.ShapeDtypeStruct((n, D_MODEL), F32),
        scratch_shapes=[pltpu.VMEM((tm, D_MODEL), BF16), pltpu.VMEM((D_MODEL, 2 * FF_TILE), BF16)],
        compiler_params=_params("parallel", "arbitrary"),
        name="ffn_ln",
    )(x2d, wg, wu, wd, g, b)


def _router_kernel(x_ref, wh_ref, wl_ref, b_ref, o_ref, ot_ref):
    x = x_ref[...]
    xh = x.astype(BF16)
    xl = (x - xh.astype(F32)).astype(BF16)
    logits = _dot(xh, wh_ref[...]) + _dot(xl, wh_ref[...]) + _dot(xh, wl_ref[...]) + b_ref[...]
    lane = lax.broadcasted_iota(jnp.int32, logits.shape, 1)
    logits = jnp.where(lane < N_EXPERTS, logits, -jnp.inf)
    m1 = jnp.max(logits, axis=-1, keepdims=True)
    i1 = jnp.min(jnp.where(logits == m1, lane, LANES), axis=-1, keepdims=True)
    rest = jnp.where(lane == i1, -jnp.inf, logits)
    m2 = jnp.max(rest, axis=-1, keepdims=True)
    i2 = jnp.min(jnp.where(rest == m2, lane, LANES), axis=-1, keepdims=True)
    e2 = jnp.exp(m2 - m1)
    g1 = 1.0 / (1.0 + e2)
    g2 = e2 / (1.0 + e2)
    route = jnp.where(lane == 0, g1, jnp.where(lane == 1, g2, jnp.where(
        lane == 2, i1.astype(F32), jnp.where(lane == 3, i2.astype(F32), 0.0))))
    o_ref[...] = route
    ot_ref[...] = route.T[0:8, :]


def _router(x2d, w_hi, w_lo, b):
    n = x2d.shape[0]
    const = lambda shape: pl.BlockSpec(shape, lambda i: (0, 0))
    return pl.pallas_call(
        _router_kernel,
        grid=(n // ROW_TILE,),
        in_specs=[pl.BlockSpec((ROW_TILE, D_MODEL), lambda i: (i, 0)),
                  const((D_MODEL, LANES)), const((D_MODEL, LANES)), const((1, LANES))],
        out_specs=[pl.BlockSpec((ROW_TILE, LANES), lambda i: (i, 0)), pl.BlockSpec((8, ROW_TILE), lambda i: (0, i))],
        out_shape=[jax.ShapeDtypeStruct((n, LANES), F32), jax.ShapeDtypeStruct((8, n), F32)],
        compiler_params=_params("parallel"),
        name="router",
    )(x2d, w_hi, w_lo, b)


def _dispatch_kernel(slot0_ref, slot1_ref, ends_ref, x_ref, o_hbm, stage_ref, zero_ref, sem, zero_sem):
    t = pl.program_id(0)
    last = pl.num_programs(0) - 1
    s = t % 2

    def wait_tile(ss):
        for _ in range(TOP_K):
            pltpu.make_async_copy(zero_ref, o_hbm.at[pl.ds(0, MOE_SUB), :], sem.at[ss]).wait()

    @pl.when(t == 0)
    def _():
        zero_ref[...] = jnp.zeros_like(zero_ref)

        def zero_sub_tile(m, carry):
            dst = o_hbm.at[pl.ds(pl.multiple_of(m * MOE_SUB, MOE_SUB), MOE_SUB), :]
            cp = pltpu.make_async_copy(zero_ref, dst, zero_sem)
            cp.start()
            cp.wait()
            return carry

        for e in range(N_EXPERTS):
            @pl.when(ends_ref[e] >= MOE_SUB)
            def _():
                zero_sub_tile(ends_ref[e] // MOE_SUB - 1, 0)

        lax.fori_loop(ends_ref[N_EXPERTS - 1] // MOE_SUB, o_hbm.shape[0] // MOE_SUB, zero_sub_tile, 0)

    @pl.when(t >= 2)
    def _():
        wait_tile(s)

    stage_ref[s] = x_ref[...].reshape(MOE_SUB // 8, 8, D_MODEL)

    def issue(i, carry):
        for u in range(8):
            for slot_ref in (slot0_ref, slot1_ref):
                pltpu.make_async_copy(stage_ref.at[s, i, pl.ds(u, 1), :],
                                      o_hbm.at[pl.ds(slot_ref[t * MOE_SUB + i * 8 + u], 1), :], sem.at[s]).start()
        return carry

    lax.fori_loop(0, MOE_SUB // 8, issue, 0)

    @pl.when(t == last)
    def _():
        wait_tile(s)
        wait_tile(1 - s)


def _dispatch_rows(x2d, slots, ends, n_rows):
    n = x2d.shape[0]
    assert n // MOE_SUB >= 2
    return pl.pallas_call(
        _dispatch_kernel,
        grid_spec=pltpu.PrefetchScalarGridSpec(
            num_scalar_prefetch=3,
            grid=(n // MOE_SUB,),
            in_specs=[pl.BlockSpec((MOE_SUB, D_MODEL), lambda t, s0, s1, en: (t, 0))],
            out_specs=pl.BlockSpec(memory_space=pl.ANY),
            scratch_shapes=[pltpu.VMEM((2, MOE_SUB // 8, 8, D_MODEL), F32), pltpu.VMEM((MOE_SUB, D_MODEL), F32),
                            pltpu.SemaphoreType.DMA((2,)), pltpu.SemaphoreType.DMA(())]),
        out_shape=jax.ShapeDtypeStruct((n_rows, D_MODEL), F32),
        compiler_params=_params("arbitrary"),
        name="moe_dispatch",
    )(slots[0], slots[1], ends, x2d)


def _moe_kernel(exp_ref, row0_ref, nsub_ref, tail_ref, x_hbm, wg_ref, wu_ref, wd_ref, y_hbm,
                xb_ref, acc_ref, wgu_ref, wdb_ref, stage_ref, in_sem, out_sem):
    v = pl.program_id(0)
    j = pl.program_id(1)
    last = pl.num_programs(1) - 1
    n_sub = nsub_ref[v]
    row0 = row0_ref[v]

    def sub_rows(m):
        return pl.ds(pl.multiple_of(m * MOE_SUB, MOE_SUB), MOE_SUB)

    def hbm_rows(m):
        return pl.ds(pl.multiple_of(row0 + m * MOE_SUB, MOE_SUB), MOE_SUB)

    @pl.when(jnp.logical_and(v == 0, j == 0))
    def _():
        stage_ref[0] = jnp.zeros((MOE_SUB, D_MODEL), F32)

        def zero_sub_tile(m, carry):
            cp = pltpu.make_async_copy(stage_ref.at[0], y_hbm.at[sub_rows(m), :], out_sem)
            cp.start()
            cp.wait()
            return carry

        lax.fori_loop(tail_ref[0], y_hbm.shape[0] // MOE_SUB, zero_sub_tile, 0)

    @pl.when(n_sub > 0)
    def _():
        wgu_ref[:, :MOE_FF_TILE] = wg_ref[...].astype(BF16)
        wgu_ref[:, MOE_FF_TILE:] = wu_ref[...].astype(BF16)
        wdb_ref[...] = wd_ref[...].astype(BF16)

        def ffn_tile(m):
            xs = xb_ref[sub_rows(m), :]
            gu = _dot(xs, wgu_ref[...])
            hidden = (jax.nn.silu(gu[:, :MOE_FF_TILE]) * gu[:, MOE_FF_TILE:]).astype(BF16)
            return _dot(hidden, wdb_ref[...])

        def y_copy(m):
            return pltpu.make_async_copy(acc_ref.at[sub_rows(m), :], y_hbm.at[hbm_rows(m), :], out_sem)

        @pl.when(j == 0)
        def _():
            def x_copy(m):
                return pltpu.make_async_copy(x_hbm.at[hbm_rows(m), :], stage_ref.at[m % 2], in_sem.at[m % 2])

            x_copy(0).start()

            def body(m, carry):
                @pl.when(m + 1 < n_sub)
                def _():
                    x_copy(m + 1).start()

                x_copy(m).wait()
                xb_ref[sub_rows(m), :] = stage_ref[m % 2].astype(BF16)
                acc_ref[sub_rows(m), :] = ffn_tile(m)
                return carry

            lax.fori_loop(0, n_sub, body, 0)

        @pl.when(jnp.logical_and(j > 0, j < last))
        def _():
            def body(m, carry):
                acc_ref[sub_rows(m), :] += ffn_tile(m)
                return carry

            lax.fori_loop(0, n_sub, body, 0)

        @pl.when(j == last)
        def _():
            def body(m, carry):
                acc_ref[sub_rows(m), :] += ffn_tile(m)
                y_copy(m).start()
                return carry

            def drain(m, carry):
                y_copy(m).wait()
                return carry

            lax.fori_loop(0, n_sub, body, 0)
            lax.fori_loop(0, n_sub, drain, 0)


def _moe_grouped(xs, wg, wu, wd, visit_exp, visit_row0, visit_nsub, tail_sub):
    n_rows = xs.shape[0]
    n_visits = visit_exp.shape[0]
    d_ff = wg.shape[2]
    assert d_ff // MOE_FF_TILE >= 2
    rows = MOE_VISIT_SUBS * MOE_SUB
    w_in = lambda v, j, e, r, ns, tl: (e[v], 0, j)
    w_out = lambda v, j, e, r, ns, tl: (e[v], j, 0)
    return pl.pallas_call(
        _moe_kernel,
        grid_spec=pltpu.PrefetchScalarGridSpec(
            num_scalar_prefetch=4,
            grid=(n_visits, d_ff // MOE_FF_TILE),
            in_specs=[pl.BlockSpec(memory_space=pl.ANY),
                      pl.BlockSpec((None, D_MODEL, MOE_FF_TILE), w_in),
                      pl.BlockSpec((None, D_MODEL, MOE_FF_TILE), w_in),
                      pl.BlockSpec((None, MOE_FF_TILE, D_MODEL), w_out)],
            out_specs=pl.BlockSpec(memory_space=pl.ANY),
            scratch_shapes=[pltpu.VMEM((rows, D_MODEL), BF16), pltpu.VMEM((rows, D_MODEL), F32),
                            pltpu.VMEM((D_MODEL, 2 * MOE_FF_TILE), BF16),
                            pltpu.VMEM((MOE_FF_TILE, D_MODEL), BF16),
                            pltpu.VMEM((2, MOE_SUB, D_MODEL), F32),
                            pltpu.SemaphoreType.DMA((2,)), pltpu.SemaphoreType.DMA(())]),
        out_shape=jax.ShapeDtypeStruct((n_rows, D_MODEL), F32),
        compiler_params=_params("arbitrary", "arbitrary"),
        name="moe_grouped",
    )(visit_exp, visit_row0, visit_nsub, tail_sub, xs, wg, wu, wd)


def _combine_kernel(slot0_ref, slot1_ref, x_ref, gate_ref, y_hbm, y_grouped_hbm, g_ref, b_ref, o_ref, buf_ref, sem):
    t = pl.program_id(0)
    s = t % 2

    def issue_tile(tt, ss):
        def issue(i, carry):
            for u in range(8):
                for k, slot_ref in enumerate((slot0_ref, slot1_ref)):
                    pltpu.make_async_copy(y_hbm.at[pl.ds(slot_ref[tt * MOE_SUB + i * 8 + u], 1), :],
                                          buf_ref.at[ss, k, i, pl.ds(u, 1), :], sem.at[ss]).start()
            return carry

        lax.fori_loop(0, MOE_SUB // 8, issue, 0)

    @pl.when(t == 0)
    def _():
        issue_tile(0, 0)

    @pl.when(t + 1 < pl.num_programs(0))
    def _():
        issue_tile(t + 1, 1 - s)

    for k in range(TOP_K):
        pltpu.make_async_copy(y_grouped_hbm.at[pl.ds(0, MOE_SUB // 8)], buf_ref.at[s, k], sem.at[s]).wait()
    gate = gate_ref[...]
    y0 = buf_ref[s, 0].reshape(MOE_SUB, D_MODEL)
    y1 = buf_ref[s, 1].reshape(MOE_SUB, D_MODEL)
    ff = gate[:, 0:1] * y0 + gate[:, 1:2] * y1
    o_ref[...] = _layer_norm_rows(DN_ALPHA * x_ref[...] + ff, g_ref[...], b_ref[...])


def _combine_ln(x2d, route, slots, ys, g, b):
    n = x2d.shape[0]
    return pl.pallas_call(
        _combine_kernel,
        grid_spec=pltpu.PrefetchScalarGridSpec(
            num_scalar_prefetch=2,
            grid=(n // MOE_SUB,),
            in_specs=[pl.BlockSpec((MOE_SUB, D_MODEL), lambda t, s0, s1: (t, 0)),
                      pl.BlockSpec((MOE_SUB, LANES), lambda t, s0, s1: (t, 0)),
                      pl.BlockSpec(memory_space=pl.ANY), pl.BlockSpec(memory_space=pl.ANY),
                      pl.BlockSpec((1, D_MODEL), lambda t, s0, s1: (0, 0)),
                      pl.BlockSpec((1, D_MODEL), lambda t, s0, s1: (0, 0))],
            out_specs=pl.BlockSpec((MOE_SUB, D_MODEL), lambda t, s0, s1: (t, 0)),
            scratch_shapes=[pltpu.VMEM((2, TOP_K, MOE_SUB // 8, 8, D_MODEL), F32),
                            pltpu.SemaphoreType.DMA((2,))]),
        out_shape=jax.ShapeDtypeStruct((n, D_MODEL), F32),
        compiler_params=_params("arbitrary"),
        name="moe_combine_ln",
    )(slots[0], slots[1], x2d, route, ys, ys.reshape(ys.shape[0] // 8, 8, D_MODEL), g, b)


def _routing_tables(route_t, n):
    idx = route_t[2:4].astype(jnp.int32)
    expert = jnp.arange(N_EXPERTS, dtype=jnp.int32)[:, None]
    chosen = [idx[k][None, :] == expert for k in range(TOP_K)]
    onehot = jnp.logical_or(chosen[0], chosen[1]).astype(jnp.int32)
    rank = jnp.cumsum(onehot, axis=1) - onehot
    counts = jnp.sum(onehot, axis=1)
    padded = ((counts + MOE_SUB - 1) // MOE_SUB) * MOE_SUB
    ends = jnp.cumsum(padded)
    starts = ends - padded
    place = starts[:, None] + rank
    slot = [jnp.sum(jnp.where(chosen[k], place, 0), axis=0).astype(jnp.int32) for k in range(TOP_K)]

    n_rows = -(-(n * TOP_K + N_EXPERTS * (MOE_SUB - 1)) // MOE_SUB) * MOE_SUB

    visit_rows = MOE_VISIT_SUBS * MOE_SUB
    max_chunks = -(-n_rows // visit_rows)
    chunk = jnp.arange(max_chunks, dtype=jnp.int32)[None, :]
    left = padded[:, None] - chunk * visit_rows
    valid = (left > 0).reshape(-1)
    n_visits = n_rows // visit_rows + N_EXPERTS
    order = jnp.argsort(jnp.logical_not(valid), stable=True)[:n_visits]
    n_valid = jnp.sum(valid.astype(jnp.int32))
    live = jnp.arange(n_visits) < n_valid
    order = jnp.where(live, order, order[jnp.maximum(n_valid - 1, 0)])
    v_exp = (order // max_chunks).astype(jnp.int32)
    v_chunk = (order % max_chunks).astype(jnp.int32)
    v_row0 = jnp.where(live, starts[v_exp] + v_chunk * visit_rows, 0).astype(jnp.int32)
    v_nsub = jnp.where(live, jnp.minimum(left.reshape(-1)[order], visit_rows) // MOE_SUB, 0).astype(jnp.int32)
    tail_sub = (ends[N_EXPERTS - 1:] // MOE_SUB).astype(jnp.int32)
    return slot, ends.astype(jnp.int32), n_rows, (v_exp, v_row0, v_nsub, tail_sub)


def _moe_ln(x2d, w_router, b_router, wg, wu, wd, g, b):
    n = x2d.shape[0]
    w_pad = jnp.zeros((D_MODEL, LANES), F32).at[:, :N_EXPERTS].set(w_router)
    w_hi = w_pad.astype(BF16)
    w_lo = (w_pad - w_hi.astype(F32)).astype(BF16)
    b_pad = jnp.zeros((1, LANES), F32).at[0, :N_EXPERTS].set(b_router)
    route, route_t = _router(x2d, w_hi, w_lo, b_pad)
    slots, ends, n_rows, visits = _routing_tables(route_t, n)
    xs = _dispatch_rows(x2d, slots, ends, n_rows)
    ys = _moe_grouped(xs, wg, wu, wd, *visits)
    return _combine_ln(x2d, route, slots, ys, g, b)


def _layout_in_proj(w_in, b_in):
    sizes = (MLSTM_W, MLSTM_W, MLSTM_W, MLSTM_W, MLSTM_HEADS, MLSTM_HEADS,
             ATTN_W, ATTN_KV_W, ATTN_KV_W, SGU_W, SGU_W)
    offs = np.concatenate([[0], np.cumsum(sizes)])
    seg = lambda a, i: a[int(offs[i]):int(offs[i + 1])]

    def build(a):
        tail = a.shape[1:]
        aq = seg(a, 6).reshape((ATTN_Q_HEADS, HEAD_DIM) + tail)[np.array(ATTN_HEAD_ORDER)]
        gates = jnp.concatenate([seg(a, 4), seg(a, 5), jnp.zeros((LANES - 2 * MLSTM_HEADS,) + tail, a.dtype)], 0)
        return jnp.concatenate([seg(a, 0), seg(a, 1), seg(a, 2), seg(a, 3), aq.reshape((ATTN_W,) + tail),
                                gates, seg(a, 7), seg(a, 8), seg(a, 9), seg(a, 10)], 0)

    return build(w_in.T).astype(BF16), build(b_in)[None, :]


def _rope_tables(positions):
    inv_freq = ROPE_THETA ** (-jnp.arange(0, ROPE_DIM, 2, dtype=F32) / ROPE_DIM)
    ang = inv_freq[None, :, None] * positions.astype(F32)[:, None, :]
    cos, sin = jnp.cos(ang), jnp.sin(ang)
    ones = jnp.ones((ang.shape[0], HEAD_DIM - ROPE_DIM, ang.shape[2]), F32)
    cos_head = jnp.concatenate([cos, cos, ones], 1)
    sin_head = jnp.concatenate([-sin, sin, 0.0 * ones], 1)
    return jnp.concatenate([cos_head, cos_head], 1), jnp.concatenate([sin_head, sin_head], 1)


def kernel(x, positions, w_in, b_in, conv_w, mlstm_norm_g, attn_sinks, sgu_w_s, sgu_b_s, sgu_norm_g, sgu_norm_b, w_out, ln1_g, ln1_b, ln2_g, ln2_b, ffn_w_gate, ffn_w_up, ffn_w_down, moe_w_router, moe_b_router, moe_w_gate, moe_w_up, moe_w_down):
    bsz, seq, _ = x.shape
    n = bsz * seq
    cos_t, sin_t = _rope_tables(positions)
    tril = jnp.tril(jnp.ones((CHUNK, CHUNK), bool))
    x2d = x.reshape(n, D_MODEL)
    for layer in range(DEPTH):
        w_p, b_p = _layout_in_proj(w_in[layer], b_in[layer])
        proj3 = _inproj(x2d, w_p, b_p).reshape(bsz, seq, D_PROJ_PAD)
        h_a = _mlstm(proj3, conv_w[layer], mlstm_norm_g[layer][None, :])
        h_b = _swa(proj3, cos_t, sin_t, attn_sinks[layer])
        w_tril = jnp.where(tril, sgu_w_s[layer], 0.0).astype(BF16)
        bias_tok = jnp.repeat(sgu_b_s[layer].T, HEAD_DIM, axis=1)
        h_c = _sgu(proj3, w_tril, bias_tok, sgu_norm_g[layer][None, :], sgu_norm_b[layer][None, :])
        wo = w_out[layer]
        wa = wo[:MLSTM_W].astype(BF16)
        wb = wo[MLSTM_W:MLSTM_W + ATTN_W].reshape(ATTN_Q_HEADS, HEAD_DIM, D_MODEL)[np.array(ATTN_HEAD_ORDER)]
        wb = wb.reshape(ATTN_W, D_MODEL).astype(BF16)
        wc = wo[MLSTM_W + ATTN_W:].astype(BF16)
        x2d = _outproj_ln(h_a.reshape(n, MLSTM_W), h_b.reshape(n, ATTN_W), h_c.reshape(n, SGU_W), x2d,
                          wa, wb, wc, ln1_g[layer][None, :], ln1_b[layer][None, :])
        j = layer // 2
        g2, b2 = ln2_g[layer][None, :], ln2_b[layer][None, :]
        if layer % 2 == 0:
            x2d = _ffn_ln(x2d, ffn_w_gate[j].astype(BF16), ffn_w_up[j].astype(BF16),
                          ffn_w_down[j].astype(BF16), g2, b2)
        else:
            x2d = _moe_ln(x2d, moe_w_router[j], moe_b_router[j], moe_w_gate[j], moe_w_up[j],
                          moe_w_down[j], g2, b2)
    return x2d.reshape(bsz, seq, D_MODEL)
```

```python
import functools

import jax
import jax.numpy as jnp
import numpy as np
from jax import lax
from jax.experimental import pallas as pl
from jax.experimental.pallas import tpu as pltpu

F32 = jnp.float32
BF16 = jnp.bfloat16

D_MODEL = 1024
HEAD_DIM = 64
LANES = 128
MLSTM_HEADS = 6
ATTN_Q_HEADS = 6
ATTN_KV_HEADS = 2
SGU_GROUPS = 4
MLSTM_W = MLSTM_HEADS * HEAD_DIM
ATTN_W = ATTN_Q_HEADS * HEAD_DIM
ATTN_KV_W = ATTN_KV_HEADS * HEAD_DIM
SGU_W = SGU_GROUPS * HEAD_DIM
CHUNK = 128
CONV_WIDTH = 4
ROPE_DIM = HEAD_DIM // 4
ROPE_THETA = 500000.0
N_EXPERTS = 8
TOP_K = 2
DEPTH = 2
DN_ALPHA = (2.0 * DEPTH) ** 0.25
LN_EPS = 1e-5

QK_OFF, V_OFF, O_OFF, AQ_OFF = 0, 768, 1152, 1536
GATE_OFF, AK_OFF, AV_OFF, SU_OFF, SV_OFF = 1920, 2048, 2176, 2304, 2560
D_PROJ_PAD = 2816
ATTN_HEAD_ORDER = (0, 3, 1, 4, 2, 5)

VMEM_LIMIT = 56 * 1024 * 1024

SGU_CHUNKS_PER_STEP = 4
MLSTM_CHUNKS_PER_STEP = 1
SWA_BLOCKS_PER_STEP = 1
ROW_TILE = 512
FFN_ROWS = 2048
FF_TILE = 256
MOE_SUB = 512
MOE_VISIT_SUBS = 9
MOE_FF_TILE = 512


def _params(*sem):
    return pltpu.CompilerParams(dimension_semantics=sem, vmem_limit_bytes=VMEM_LIMIT)


def _lane_lo(shape):
    return lax.broadcasted_iota(jnp.int32, shape, len(shape) - 1) < HEAD_DIM


def _layer_norm_rows(z, g, b):
    mu = jnp.mean(z, axis=-1, keepdims=True)
    zc = z - mu
    var = jnp.mean(zc * zc, axis=-1, keepdims=True)
    return zc * lax.rsqrt(var + LN_EPS) * g + b


def _half_layer_norm(x, lo):
    inv = 1.0 / HEAD_DIM
    s_lo = jnp.sum(jnp.where(lo, x, 0.0), axis=-1, keepdims=True)
    s_all = jnp.sum(x, axis=-1, keepdims=True)
    mu = jnp.where(lo, s_lo, s_all - s_lo) * inv
    xc = x - mu
    sq = xc * xc
    q_lo = jnp.sum(jnp.where(lo, sq, 0.0), axis=-1, keepdims=True)
    q_all = jnp.sum(sq, axis=-1, keepdims=True)
    var = jnp.where(lo, q_lo, q_all - q_lo) * inv
    return xc * lax.rsqrt(var + LN_EPS)


def _split3(a):
    h1 = a.astype(BF16)
    r1 = a - h1.astype(F32)
    h2 = r1.astype(BF16)
    r2 = r1 - h2.astype(F32)
    return h1, h2, r2.astype(BF16)


def _split3_masked(a):
    def top(x):
        bits = lax.bitcast_convert_type(x, jnp.int32) & jnp.int32(-65536)
        return lax.bitcast_convert_type(bits, F32)

    h1 = top(a)
    r1 = a - h1
    h2 = top(r1)
    return h1.astype(BF16), h2.astype(BF16), (r1 - h2).astype(BF16)


def _dot(a, b):
    return jnp.dot(a, b, preferred_element_type=F32)


def _inproj_kernel(x_ref, w_ref, b_ref, o_ref):
    o_ref[...] = lax.dot_general(x_ref[...].astype(BF16), w_ref[...], (((1,), (1,)), ((), ())),
                                 preferred_element_type=F32) + b_ref[...]


def _inproj(x2d, w, b):
    n = x2d.shape[0]
    return pl.pallas_call(
        _inproj_kernel,
        grid=(n // ROW_TILE,),
        in_specs=[pl.BlockSpec((ROW_TILE, D_MODEL), lambda i: (i, 0)),
                  pl.BlockSpec((D_PROJ_PAD, D_MODEL), lambda i: (0, 0)),
                  pl.BlockSpec((1, D_PROJ_PAD), lambda i: (0, 0))],
        out_specs=pl.BlockSpec((ROW_TILE, D_PROJ_PAD), lambda i: (i, 0)),
        out_shape=jax.ShapeDtypeStruct((n, D_PROJ_PAD), F32),
        compiler_params=_params("parallel"),
        name="inproj",
    )(x2d, w, b)


def _mlstm_kernel(qk_ref, v_ref, o_ref, gate_ref, cw_ref, g_ref, out_ref, buf_ref, ct_ref, m_ref):
    @pl.when(pl.program_id(1) == 0)
    def _():
        buf_ref[0:8, :] = jnp.zeros((8, 2 * MLSTM_W), F32)
        ct_ref[...] = jnp.zeros_like(ct_ref)
        m_ref[...] = jnp.zeros_like(m_ref)

    for i in range(MLSTM_CHUNKS_PER_STEP):
        rows = pl.ds(i * CHUNK, CHUNK)
        _mlstm_chunk(qk_ref.at[rows], v_ref.at[rows], o_ref.at[rows], gate_ref.at[rows], cw_ref, g_ref,
                     out_ref.at[rows], buf_ref, ct_ref, m_ref)


def _mlstm_chunk(qk_ref, v_ref, o_ref, gate_ref, cw_ref, g_ref, out_ref, buf_ref, ct_ref, m_ref):
    buf_ref[8:8 + CHUNK, :] = qk_ref[...]
    cw = cw_ref[...]
    base = 8 - (CONV_WIDTH - 1)
    acc = cw[0:1, :] * buf_ref[base:base + CHUNK, :]
    for j in range(1, CONV_WIDTH):
        acc = acc + cw[j:j + 1, :] * buf_ref[base + j:base + j + CHUNK, :]
    buf_ref[0:8, :] = qk_ref[CHUNK - 8:CHUNK, :]
    qk = jax.nn.silu(acc)

    gates = gate_ref[...]
    lf = jax.nn.log_sigmoid(gates)
    row = lax.broadcasted_iota(jnp.int32, (CHUNK, CHUNK), 0)
    col = lax.broadcasted_iota(jnp.int32, (CHUNK, CHUNK), 1)
    causal = col <= row
    tri = causal.astype(BF16)
    l1, l2, l3 = _split3(lf)
    cum = _dot(tri, l1) + _dot(tri, l2) + _dot(tri, l3)
    gates_t = gates.T
    cum_t = cum.T
    cum_parts = _split3_masked(cum)

    lo = _lane_lo((CHUNK, LANES))
    for p in range(MLSTM_HEADS // 2):
        sl = slice(p * LANES, (p + 1) * LANES)
        q_slab = qk[:, sl] * (HEAD_DIM ** -0.5)
        k_slab = qk[:, MLSTM_W + p * LANES:MLSTM_W + (p + 1) * LANES]
        kt_slab = k_slab.T
        v_slab = v_ref[:, sl]
        ct_pair = ct_ref[p]
        ct_pair_b = ct_pair.astype(BF16)
        halves = []
        new_ct = []
        for half in range(2):
            h = 2 * p + half
            sel = lo if half == 0 else jnp.logical_not(lo)
            li_row = gates_t[h:h + 1, :]
            bc_row = cum_t[MLSTM_HEADS + h:MLSTM_HEADS + h + 1, :]
            pick = (row == MLSTM_HEADS + h).astype(BF16)
            bc_col = _dot(cum_parts[0], pick) + _dot(cum_parts[1], pick) + _dot(cum_parts[2], pick)
            b_tot = bc_row[:, CHUNK - 1:CHUNK]
            m_prev = m_ref[h:h + 1, 0:1]

            dmat = jnp.where(causal, bc_col - bc_row + li_row, -jnp.inf)
            m_inter = bc_col + m_prev
            m_row = jnp.maximum(m_inter, jnp.max(dmat, axis=-1, keepdims=True))
            q_m = jnp.where(sel, q_slab, 0.0).astype(BF16)
            s = _dot(q_m, kt_slab.astype(BF16))
            pmat = (s * jnp.exp(dmat - m_row)).astype(BF16)
            w_inter = jnp.exp(m_inter - m_row)
            v_aug = jnp.where(sel, v_slab, 1.0).astype(BF16)
            halves.append((_dot(pmat, v_aug) + w_inter * _dot(q_m, ct_pair_b), jnp.exp(-m_row)))

            a_row = b_tot - bc_row + li_row
            m_loc = jnp.max(a_row, axis=-1, keepdims=True)
            w_row = jnp.exp(a_row - m_loc)
            kt_h = kt_slab[half * HEAD_DIM:(half + 1) * HEAD_DIM, :]
            ct_loc = _dot((kt_h * w_row).astype(BF16), v_aug)
            m_new = jnp.maximum(b_tot + m_prev, m_loc)
            s_old = jnp.exp(b_tot + m_prev - m_new)
            s_loc = jnp.exp(m_loc - m_new)
            new_ct.append(s_old * ct_pair[half * HEAD_DIM:(half + 1) * HEAD_DIM, :] + s_loc * ct_loc)
            m_ref[h:h + 1, :] = jnp.broadcast_to(m_new, (1, LANES))

        ct_ref[p] = jnp.concatenate(new_ct, axis=0)
        (r_even, stab_even), (r_odd, stab_odd) = halves
        num = jnp.where(lo, r_even, r_odd)
        den = pltpu.roll(jnp.where(lo, r_odd, r_even), HEAD_DIM, 1)
        hh = num / jnp.maximum(jnp.abs(den), jnp.where(lo, stab_even, stab_odd))
        hn = _half_layer_norm(hh, lo) * g_ref[:, sl]
        out_ref[:, sl] = (hn * jax.nn.sigmoid(o_ref[:, sl])).astype(out_ref.dtype)


def _mlstm(proj3, conv_w, norm_g):
    b, s, _ = proj3.shape
    rows = MLSTM_CHUNKS_PER_STEP * CHUNK
    assert s % rows == 0
    blk = lambda w, off: pl.BlockSpec((None, rows, w), lambda i, c: (i, c, off // w))
    return pl.pallas_call(
        _mlstm_kernel,
        grid=(b, s // rows),
        in_specs=[blk(2 * MLSTM_W, QK_OFF), blk(MLSTM_W, V_OFF), blk(MLSTM_W, O_OFF), blk(LANES, GATE_OFF),
                  pl.BlockSpec((CONV_WIDTH, 2 * MLSTM_W), lambda i, c: (0, 0)),
                  pl.BlockSpec((1, MLSTM_W), lambda i, c: (0, 0))],
        out_specs=pl.BlockSpec((None, rows, MLSTM_W), lambda i, c: (i, c, 0)),
        out_shape=jax.ShapeDtypeStruct((b, s, MLSTM_W), BF16),
        scratch_shapes=[pltpu.VMEM((8 + CHUNK, 2 * MLSTM_W), F32),
                        pltpu.VMEM((MLSTM_HEADS // 2, LANES, LANES), F32),
                        pltpu.VMEM((8, LANES), F32)],
        compiler_params=_params("parallel", "arbitrary"),
        name="mlstm",
    )(proj3, proj3, proj3, proj3, conv_w, norm_g)


def _rope(x, cos_t, sin_t, first):
    return x * cos_t + jnp.where(first, pltpu.roll(x, LANES - ROPE_DIM // 2, 1),
                                 pltpu.roll(x, ROPE_DIM // 2, 1)) * sin_t


def _swa_kernel(sink_ref, q_ref, k_ref, v_ref, cos_ref, sin_ref, out_ref, kt_ref, vv_ref):
    first_step = pl.program_id(1) == 0

    @pl.when(first_step)
    def _():
        kt_ref[...] = jnp.zeros_like(kt_ref)
        vv_ref[...] = jnp.zeros_like(vv_ref)

    for i in range(SWA_BLOCKS_PER_STEP):
        rows = pl.ds(i * CHUNK, CHUNK)
        hide_previous = jnp.where(first_step, 2 * CHUNK, 0) if i == 0 else 0
        _swa_block(hide_previous, sink_ref, q_ref.at[rows], k_ref.at[rows], v_ref.at[rows], cos_ref.at[rows],
                   sin_ref.at[rows], out_ref.at[rows], kt_ref, vv_ref)


def _swa_block(first_block_shift, sink_ref, q_ref, k_ref, v_ref, cos_ref, sin_ref, out_ref, kt_ref, vv_ref):
    cos_t = cos_ref[...]
    sin_t = sin_ref[...]
    lane =lax.broadcasted_iota(jnp.int32, (CHUNK, LANES), 1)
    lo = lane < HEAD_DIM
    first = (lane % HEAD_DIM) < ROPE_DIM // 2

    k_cur = _rope(k_ref[...], cos_t, sin_t, first)
    kt_ref[:, CHUNK:2 * CHUNK] = k_cur.T.astype(BF16)
    vv_ref[CHUNK:2 * CHUNK, :] = v_ref[...].astype(BF16)
    kt = kt_ref[...]
    vv = vv_ref[...]

    row = lax.broadcasted_iota(jnp.int32, (CHUNK, 2 * CHUNK), 0)
    col = lax.broadcasted_iota(jnp.int32, (CHUNK, 2 * CHUNK), 1)
    visible = jnp.logical_or(jnp.logical_and(col < CHUNK, col > row + first_block_shift),
                             jnp.logical_and(col >= CHUNK, col - CHUNK <= row))

    for j in range(ATTN_Q_HEADS // 2):
        sl = slice(j * LANES, (j + 1) * LANES)
        q_slab = _rope(q_ref[:, sl], cos_t, sin_t, first) * (HEAD_DIM ** -0.5)
        outs = []
        for half in range(2):
            sel = lo if half == 0 else jnp.logical_not(lo)
            sink = sink_ref[ATTN_HEAD_ORDER[2 * j + half]]
            q_m = jnp.where(sel, q_slab, 0.0).astype(BF16)
            s = jnp.where(visible, _dot(q_m, kt), -jnp.inf)
            mx = jnp.maximum(jnp.max(s, axis=-1, keepdims=True), sink)
            pexp = jnp.exp(s - mx)
            denom = jnp.sum(pexp, axis=-1, keepdims=True) + jnp.exp(sink - mx)
            outs.append(_dot(pexp.astype(BF16), vv) / denom)
        out_ref[:, sl] = jnp.where(lo, outs[0], outs[1]).astype(out_ref.dtype)

    kt_ref[:, 0:CHUNK] = kt_ref[:, CHUNK:2 * CHUNK]
    vv_ref[0:CHUNK, :] = vv_ref[CHUNK:2 * CHUNK, :]


def _swa(proj3, cos_t, sin_t, sinks):
    b, s, _ = proj3.shape
    rows = SWA_BLOCKS_PER_STEP * CHUNK
    assert s % rows == 0
    blk = lambda w, off: pl.BlockSpec((None, rows, w), lambda i, c: (i, c, off // w))
    tab =pl.BlockSpec((None, rows, LANES), lambda i, c: (i, c, 0))
    return pl.pallas_call(
        _swa_kernel,
        grid=(b, s // rows),
        in_specs=[pl.BlockSpec(memory_space=pltpu.SMEM),
                  blk(ATTN_W, AQ_OFF), blk(ATTN_KV_W, AK_OFF), blk(ATTN_KV_W, AV_OFF), tab, tab],
        out_specs=pl.BlockSpec((None, rows, ATTN_W), lambda i, c: (i, c, 0)),
        out_shape=jax.ShapeDtypeStruct((b, s, ATTN_W), BF16),
        scratch_shapes=[pltpu.VMEM((LANES, 2 * CHUNK), BF16), pltpu.VMEM((2 * CHUNK, LANES), BF16)],
        compiler_params=_params("parallel", "arbitrary"),
        name="swa",
    )(sinks, proj3, proj3, proj3, cos_t, sin_t)


def _sgu_kernel(u_ref, v_ref, w_ref, bias_ref, g_ref, b_ref, out_ref):
    lo = _lane_lo((CHUNK, LANES))
    for c in range(SGU_CHUNKS_PER_STEP):
        rows = slice(c * CHUNK, (c + 1) * CHUNK)
        for j in range(SGU_GROUPS // 2):
            sl = slice(j * LANES, (j + 1) * LANES)
            u = jax.nn.gelu(u_ref[rows, sl])
            v = jax.nn.gelu(v_ref[rows, sl])
            vn = (_half_layer_norm(v, lo) * g_ref[:, sl] + b_ref[:, sl]).astype(BF16)
            mixed = jnp.where(lo, _dot(w_ref[2 * j], vn), _dot(w_ref[2 * j + 1], vn)) + bias_ref[:, sl]
            out_ref[rows, sl] = (u * mixed).astype(out_ref.dtype)


def _sgu(proj3, w_tril, bias_tok, norm_g, norm_b):
    b, s, _ = proj3.shape
    rows = SGU_CHUNKS_PER_STEP * CHUNK
    assert s % rows == 0
    blk = lambda w, off: pl.BlockSpec((None, rows, w), lambda i, c: (i, c, off // w))
    const2 = lambda shape: pl.BlockSpec(shape, lambda i, c: (0,) * len(shape))
    return pl.pallas_call(
        _sgu_kernel,
        grid=(b, s // rows),
        in_specs=[blk(SGU_W, SU_OFF), blk(SGU_W, SV_OFF), const2((SGU_GROUPS, CHUNK, CHUNK)),
                  const2((CHUNK, SGU_W)), const2((1, SGU_W)), const2((1, SGU_W))],
        out_specs=pl.BlockSpec((None, rows, SGU_W), lambda i, c: (i, c, 0)),
        out_shape=jax.ShapeDtypeStruct((b, s, SGU_W), BF16),
        compiler_params=_params("parallel", "parallel"),
        name="sgu",
    )(proj3, proj3, w_tril, bias_tok, norm_g, norm_b)


def _outproj_kernel(ha_ref, hb_ref, hc_ref, x_ref, wa_ref, wb_ref, wc_ref, g_ref, b_ref, o_ref):
    mix = _dot(ha_ref[...], wa_ref[...]) + _dot(hb_ref[...], wb_ref[...]) + _dot(hc_ref[...], wc_ref[...])
    o_ref[...] = _layer_norm_rows(DN_ALPHA * x_ref[...] + mix, g_ref[...], b_ref[...])


def _outproj_ln(ha, hb, hc, x2d, wa, wb, wc, g, b):
    n = x2d.shape[0]
    rows = lambda w: pl.BlockSpec((ROW_TILE, w), lambda i: (i, 0))
    const = lambda shape: pl.BlockSpec(shape, lambda i: (0, 0))
    return pl.pallas_call(
        _outproj_kernel,
        grid=(n // ROW_TILE,),
        in_specs=[rows(MLSTM_W), rows(ATTN_W), rows(SGU_W), rows(D_MODEL),
                  const((MLSTM_W, D_MODEL)), const((ATTN_W, D_MODEL)), const((SGU_W, D_MODEL)),
                  const((1, D_MODEL)), const((1, D_MODEL))],
        out_specs=rows(D_MODEL),
        out_shape=jax.ShapeDtypeStruct((n, D_MODEL), F32),
        compiler_params=_params("parallel"),
        name="outproj_ln",
    )(ha, hb, hc, x2d, wa, wb, wc, g, b)


def _ffn_kernel(x_ref, wg_ref, wu_ref, wd_ref, g_ref, b_ref, o_ref, xb_ref, wgu_ref):
    j = pl.program_id(1)
    last = pl.num_programs(1) - 1
    n_sub = FFN_ROWS // ROW_TILE
    wgu_ref[:, :FF_TILE] = wg_ref[...]
    wgu_ref[:, FF_TILE:] = wu_ref[...]

    def sub_rows(m):
        return pl.ds(pl.multiple_of(m * ROW_TILE, ROW_TILE), ROW_TILE)

    def ffn_tile(m):
        gu = _dot(xb_ref[sub_rows(m), :], wgu_ref[...])
        hidden = (jax.nn.silu(gu[:, :FF_TILE]) * gu[:, FF_TILE:]).astype(BF16)
        return _dot(hidden, wd_ref[...])

    @pl.when(j == 0)
    def _():
        def body(m, carry):
            xb_ref[sub_rows(m), :] = x_ref[sub_rows(m), :].astype(BF16)
            o_ref[sub_rows(m), :] = ffn_tile(m)
            return carry

        lax.fori_loop(0, n_sub, body, 0)

    @pl.when(jnp.logical_and(j > 0, j < last))
    def _():
        def body(m, carry):
            o_ref[sub_rows(m), :] += ffn_tile(m)
            return carry

        lax.fori_loop(0, n_sub, body, 0)

    @pl.when(j == last)
    def _():
        def body(m, carry):
            rows = sub_rows(m)
            ff = o_ref[rows, :] + ffn_tile(m)
            o_ref[rows, :] = _layer_norm_rows(DN_ALPHA * x_ref[rows, :] + ff, g_ref[...], b_ref[...])
            return carry

        lax.fori_loop(0, n_sub, body, 0)


def _ffn_ln(x2d, wg, wu, wd, g, b):
    n = x2d.shape[0]
    d_ff = wd.shape[0]
    assert d_ff // FF_TILE >= 2
    tm = FFN_ROWS
    return pl.pallas_call(
        _ffn_kernel,
        grid=(n // tm, d_ff // FF_TILE),
        in_specs=[pl.BlockSpec((tm, D_MODEL), lambda i, j: (i, 0)),
                  pl.BlockSpec((D_MODEL, FF_TILE), lambda i, j: (0, j)),
                  pl.BlockSpec((D_MODEL, FF_TILE), lambda i, j: (0, j)),
                  pl.BlockSpec((FF_TILE, D_MODEL), lambda i, j: (j, 0)),
                  pl.BlockSpec((1, D_MODEL), lambda i, j: (0, 0)),
                  pl.BlockSpec((1, D_MODEL), lambda i, j: (0, 0))],
        out_specs=pl.BlockSpec((tm, D_MODEL), lambda i, j: (i, 0)),
        out_shape=jax.ShapeDtypeStruct((n, D_MODEL), F32),
        scratch_shapes=[pltpu.VMEM((tm, D_MODEL), BF16), pltpu.VMEM((D_MODEL, 2 * FF_TILE), BF16)],
        compiler_params=_params("parallel", "arbitrary"),
        name="ffn_ln",
    )(x2d, wg, wu, wd, g, b)


def _router_kernel(x_ref, wh_ref, wl_ref, b_ref, o_ref, ot_ref):
    x = x_ref[...]
    xh = x.astype(BF16)
    xl = (x - xh.astype(F32)).astype(BF16)
    logits = _dot(xh, wh_ref[...]) + _dot(xl, wh_ref[...]) + _dot(xh, wl_ref[...]) + b_ref[...]
    lane = lax.broadcasted_iota(jnp.int32, logits.shape, 1)
    logits = jnp.where(lane < N_EXPERTS, logits, -jnp.inf)
    m1 = jnp.max(logits, axis=-1, keepdims=True)
    i1 = jnp.min(jnp.where(logits == m1, lane, LANES), axis=-1, keepdims=True)
    rest = jnp.where(lane == i1, -jnp.inf, logits)
    m2 = jnp.max(rest, axis=-1, keepdims=True)
    i2 = jnp.min(jnp.where(rest == m2, lane, LANES), axis=-1, keepdims=True)
    e2 = jnp.exp(m2 - m1)
    g1 = 1.0 / (1.0 + e2)
    g2 = e2 / (1.0 + e2)
    route = jnp.where(lane == 0, g1, jnp.where(lane == 1, g2, jnp.where(
        lane == 2, i1.astype(F32), jnp.where(lane == 3, i2.astype(F32), 0.0))))
    o_ref[...] = route
    ot_ref[...] = route.T[0:8, :]


def _router(x2d, w_hi, w_lo, b):
    n = x2d.shape[0]
    const = lambda shape: pl.BlockSpec(shape, lambda i: (0, 0))
    return pl.pallas_call(
        _router_kernel,
        grid=(n // ROW_TILE,),
        in_specs=[pl.BlockSpec((ROW_TILE, D_MODEL), lambda i: (i, 0)),
                  const((D_MODEL, LANES)), const((D_MODEL, LANES)), const((1, LANES))],
        out_specs=[pl.BlockSpec((ROW_TILE, LANES), lambda i: (i, 0)), pl.BlockSpec((8, ROW_TILE), lambda i: (0, i))],
        out_shape=[jax.ShapeDtypeStruct((n, LANES), F32), jax.ShapeDtypeStruct((8, n), F32)],
        compiler_params=_params("parallel"),
        name="router",
    )(x2d, w_hi, w_lo, b)


def _dispatch_kernel(slot0_ref, slot1_ref, ends_ref, x_ref, o_hbm, stage_ref, zero_ref, sem, zero_sem):
    t = pl.program_id(0)
    last = pl.num_programs(0) - 1
    s = t % 2

    def wait_tile(ss):
        for _ in range(TOP_K):
            pltpu.make_async_copy(zero_ref, o_hbm.at[pl.ds(0, MOE_SUB), :], sem.at[ss]).wait()

    @pl.when(t == 0)
    def _():
        zero_ref[...] = jnp.zeros_like(zero_ref)

        def zero_sub_tile(m, carry):
            dst = o_hbm.at[pl.ds(pl.multiple_of(m * MOE_SUB, MOE_SUB), MOE_SUB), :]
            cp = pltpu.make_async_copy(zero_ref, dst, zero_sem)
            cp.start()
            cp.wait()
            return carry

        for e in range(N_EXPERTS):
            @pl.when(ends_ref[e] >= MOE_SUB)
            def _():
                zero_sub_tile(ends_ref[e] // MOE_SUB - 1, 0)

        lax.fori_loop(ends_ref[N_EXPERTS - 1] // MOE_SUB, o_hbm.shape[0] // MOE_SUB, zero_sub_tile, 0)

    @pl.when(t >= 2)
    def _():
        wait_tile(s)

    stage_ref[s] = x_ref[...].reshape(MOE_SUB // 8, 8, D_MODEL)

    def issue(i, carry):
        for u in range(8):
            for slot_ref in (slot0_ref, slot1_ref):
                pltpu.make_async_copy(stage_ref.at[s, i, pl.ds(u, 1), :],
                                      o_hbm.at[pl.ds(slot_ref[t * MOE_SUB + i * 8 + u], 1), :], sem.at[s]).start()
        return carry

    lax.fori_loop(0, MOE_SUB // 8, issue, 0)

    @pl.when(t == last)
    def _():
        wait_tile(s)
        wait_tile(1 - s)


def _dispatch_rows(x2d, slots, ends, n_rows):
    n = x2d.shape[0]
    assert n // MOE_SUB >= 2
    return pl.pallas_call(
        _dispatch_kernel,
        grid_spec=pltpu.PrefetchScalarGridSpec(
            num_scalar_prefetch=3,
            grid=(n // MOE_SUB,),
            in_specs=[pl.BlockSpec((MOE_SUB, D_MODEL), lambda t, s0, s1, en: (t, 0))],
            out_specs=pl.BlockSpec(memory_space=pl.ANY),
            scratch_shapes=[pltpu.VMEM((2, MOE_SUB // 8, 8, D_MODEL), F32), pltpu.VMEM((MOE_SUB, D_MODEL), F32),
                            pltpu.SemaphoreType.DMA((2,)), pltpu.SemaphoreType.DMA(())]),
        out_shape=jax.ShapeDtypeStruct((n_rows, D_MODEL), F32),
        compiler_params=_params("arbitrary"),
        name="moe_dispatch",
    )(slots[0], slots[1], ends, x2d)


def _moe_kernel(exp_ref, row0_ref, nsub_ref, tail_ref, x_hbm, wg_ref, wu_ref, wd_ref, y_hbm,
                xb_ref, acc_ref, wgu_ref, wdb_ref, stage_ref, in_sem, out_sem):
    v = pl.program_id(0)
    j = pl.program_id(1)
    last = pl.num_programs(1) - 1
    n_sub = nsub_ref[v]
    row0 = row0_ref[v]

    def sub_rows(m):
        return pl.ds(pl.multiple_of(m * MOE_SUB, MOE_SUB), MOE_SUB)

    def hbm_rows(m):
        return pl.ds(pl.multiple_of(row0 + m * MOE_SUB, MOE_SUB), MOE_SUB)

    @pl.when(jnp.logical_and(v == 0, j == 0))
    def _():
        stage_ref[0] = jnp.zeros((MOE_SUB, D_MODEL), F32)

        def zero_sub_tile(m, carry):
            cp = pltpu.make_async_copy(stage_ref.at[0], y_hbm.at[sub_rows(m), :], out_sem)
            cp.start()
            cp.wait()
            return carry

        lax.fori_loop(tail_ref[0], y_hbm.shape[0] // MOE_SUB, zero_sub_tile, 0)

    @pl.when(n_sub > 0)
    def _():
        wgu_ref[:, :MOE_FF_TILE] = wg_ref[...].astype(BF16)
        wgu_ref[:, MOE_FF_TILE:] = wu_ref[...].astype(BF16)
        wdb_ref[...] = wd_ref[...].astype(BF16)

        def ffn_tile(m):
            xs = xb_ref[sub_rows(m), :]
            gu = _dot(xs, wgu_ref[...])
            hidden = (jax.nn.silu(gu[:, :MOE_FF_TILE]) * gu[:, MOE_FF_TILE:]).astype(BF16)
            return _dot(hidden, wdb_ref[...])

        def y_copy(m):
            return pltpu.make_async_copy(acc_ref.at[sub_rows(m), :], y_hbm.at[hbm_rows(m), :], out_sem)

        @pl.when(j == 0)
        def _():
            def x_copy(m):
                return pltpu.make_async_copy(x_hbm.at[hbm_rows(m), :], stage_ref.at[m % 2], in_sem.at[m % 2])

            x_copy(0).start()

            def body(m, carry):
                @pl.when(m + 1 < n_sub)
                def _():
                    x_copy(m + 1).start()

                x_copy(m).wait()
                xb_ref[sub_rows(m), :] = stage_ref[m % 2].astype(BF16)
                acc_ref[sub_rows(m), :] = ffn_tile(m)
                return carry

            lax.fori_loop(0, n_sub, body, 0)

        @pl.when(jnp.logical_and(j > 0, j < last))
        def _():
            def body(m, carry):
                acc_ref[sub_rows(m), :] += ffn_tile(m)
                return carry

            lax.fori_loop(0, n_sub, body, 0)

        @pl.when(j == last)
        def _():
            def body(m, carry):
                acc_ref[sub_rows(m), :] += ffn_tile(m)
                y_copy(m).start()
                return carry

            def drain(m, carry):
                y_copy(m).wait()
                return carry

            lax.fori_loop(0, n_sub, body, 0)
            lax.fori_loop(0, n_sub, drain, 0)


def _moe_grouped(xs, wg, wu, wd, visit_exp, visit_row0, visit_nsub, tail_sub):
    n_rows = xs.shape[0]
    n_visits = visit_exp.shape[0]
    d_ff = wg.shape[2]
    assert d_ff // MOE_FF_TILE >= 2
    rows = MOE_VISIT_SUBS * MOE_SUB
    w_in = lambda v, j, e, r, ns, tl: (e[v], 0, j)
    w_out = lambda v, j, e, r, ns, tl: (e[v], j, 0)
    return pl.pallas_call(
        _moe_kernel,
        grid_spec=pltpu.PrefetchScalarGridSpec(
            num_scalar_prefetch=4,
            grid=(n_visits, d_ff // MOE_FF_TILE),
            in_specs=[pl.BlockSpec(memory_space=pl.ANY),
                      pl.BlockSpec((None, D_MODEL, MOE_FF_TILE), w_in),
                      pl.BlockSpec((None, D_MODEL, MOE_FF_TILE), w_in),
                      pl.BlockSpec((None, MOE_FF_TILE, D_MODEL), w_out)],
            out_specs=pl.BlockSpec(memory_space=pl.ANY),
            scratch_shapes=[pltpu.VMEM((rows, D_MODEL), BF16), pltpu.VMEM((rows, D_MODEL), F32),
                            pltpu.VMEM((D_MODEL, 2 * MOE_FF_TILE), BF16),
                            pltpu.VMEM((MOE_FF_TILE, D_MODEL), BF16),
                            pltpu.VMEM((2, MOE_SUB, D_MODEL), F32),
                            pltpu.SemaphoreType.DMA((2,)), pltpu.SemaphoreType.DMA(())]),
        out_shape=jax.ShapeDtypeStruct((n_rows, D_MODEL), F32),
        compiler_params=_params("arbitrary", "arbitrary"),
        name="moe_grouped",
    )(visit_exp, visit_row0, visit_nsub, tail_sub, xs, wg, wu, wd)


def _combine_kernel(slot0_ref, slot1_ref, x_ref, gate_ref, y_hbm, y_grouped_hbm, g_ref, b_ref, o_ref, buf_ref, sem):
    t = pl.program_id(0)
    s = t % 2

    def issue_tile(tt, ss):
        def issue(i, carry):
            for u in range(8):
                for k, slot_ref in enumerate((slot0_ref, slot1_ref)):
                    pltpu.make_async_copy(y_hbm.at[pl.ds(slot_ref[tt * MOE_SUB + i * 8 + u], 1), :],
                                          buf_ref.at[ss, k, i, pl.ds(u, 1), :], sem.at[ss]).start()
            return carry

        lax.fori_loop(0, MOE_SUB // 8, issue, 0)

    @pl.when(t == 0)
    def _():
        issue_tile(0, 0)

    @pl.when(t + 1 < pl.num_programs(0))
    def _():
        issue_tile(t + 1, 1 - s)

    for k in range(TOP_K):
        pltpu.make_async_copy(y_grouped_hbm.at[pl.ds(0, MOE_SUB // 8)], buf_ref.at[s, k], sem.at[s]).wait()
    gate = gate_ref[...]
    y0 = buf_ref[s, 0].reshape(MOE_SUB, D_MODEL)
    y1 = buf_ref[s, 1].reshape(MOE_SUB, D_MODEL)
    ff = gate[:, 0:1] * y0 + gate[:, 1:2] * y1
    o_ref[...] = _layer_norm_rows(DN_ALPHA * x_ref[...] + ff, g_ref[...], b_ref[...])


def _combine_ln(x2d, route, slots, ys, g, b):
    n = x2d.shape[0]
    return pl.pallas_call(
        _combine_kernel,
        grid_spec=pltpu.PrefetchScalarGridSpec(
            num_scalar_prefetch=2,
            grid=(n // MOE_SUB,),
            in_specs=[pl.BlockSpec((MOE_SUB, D_MODEL), lambda t, s0, s1: (t, 0)),
                      pl.BlockSpec((MOE_SUB, LANES), lambda t, s0, s1: (t, 0)),
                      pl.BlockSpec(memory_space=pl.ANY), pl.BlockSpec(memory_space=pl.ANY),
                      pl.BlockSpec((1, D_MODEL), lambda t, s0, s1: (0, 0)),
                      pl.BlockSpec((1, D_MODEL), lambda t, s0, s1: (0, 0))],
            out_specs=pl.BlockSpec((MOE_SUB, D_MODEL), lambda t, s0, s1: (t, 0)),
            scratch_shapes=[pltpu.VMEM((2, TOP_K, MOE_SUB // 8, 8, D_MODEL), F32),
                            pltpu.SemaphoreType.DMA((2,))]),
        out_shape=jax.ShapeDtypeStruct((n, D_MODEL), F32),
        compiler_params=_params("arbitrary"),
        name="moe_combine_ln",
    )(slots[0], slots[1], x2d, route, ys, ys.reshape(ys.shape[0] // 8, 8, D_MODEL), g, b)


def _routing_tables(route_t, n):
    idx = route_t[2:4].astype(jnp.int32)
    expert = jnp.arange(N_EXPERTS, dtype=jnp.int32)[:, None]
    chosen = [idx[k][None, :] == expert for k in range(TOP_K)]
    onehot = jnp.logical_or(chosen[0], chosen[1]).astype(jnp.int32)
    rank = jnp.cumsum(onehot, axis=1) - onehot
    counts = jnp.sum(onehot, axis=1)
    padded = ((counts + MOE_SUB - 1) // MOE_SUB) * MOE_SUB
    ends = jnp.cumsum(padded)
    starts = ends - padded
    place = starts[:, None] + rank
    slot = [jnp.sum(jnp.where(chosen[k], place, 0), axis=0).astype(jnp.int32) for k in range(TOP_K)]

    n_rows = -(-(n * TOP_K + N_EXPERTS * (MOE_SUB - 1)) // MOE_SUB) * MOE_SUB

    visit_rows = MOE_VISIT_SUBS * MOE_SUB
    max_chunks = -(-n_rows // visit_rows)
    chunk = jnp.arange(max_chunks, dtype=jnp.int32)[None, :]
    left = padded[:, None] - chunk * visit_rows
    valid = (left > 0).reshape(-1)
    n_visits = n_rows // visit_rows + N_EXPERTS
    order = jnp.argsort(jnp.logical_not(valid), stable=True)[:n_visits]
    n_valid = jnp.sum(valid.astype(jnp.int32))
    live = jnp.arange(n_visits) < n_valid
    order = jnp.where(live, order, order[jnp.maximum(n_valid - 1, 0)])
    v_exp = (order // max_chunks).astype(jnp.int32)
    v_chunk = (order % max_chunks).astype(jnp.int32)
    v_row0 = jnp.where(live, starts[v_exp] + v_chunk * visit_rows, 0).astype(jnp.int32)
    v_nsub = jnp.where(live, jnp.minimum(left.reshape(-1)[order], visit_rows) // MOE_SUB, 0).astype(jnp.int32)
    tail_sub = (ends[N_EXPERTS - 1:] // MOE_SUB).astype(jnp.int32)
    return slot, ends.astype(jnp.int32), n_rows, (v_exp, v_row0, v_nsub, tail_sub)


def _moe_ln(x2d, w_router, b_router, wg, wu, wd, g, b):
    n = x2d.shape[0]
    w_pad = jnp.zeros((D_MODEL, LANES), F32).at[:, :N_EXPERTS].set(w_router)
    w_hi = w_pad.astype(BF16)
    w_lo = (w_pad - w_hi.astype(F32)).astype(BF16)
    b_pad = jnp.zeros((1, LANES), F32).at[0, :N_EXPERTS].set(b_router)
    route, route_t = _router(x2d, w_hi, w_lo, b_pad)
    slots, ends, n_rows, visits = _routing_tables(route_t, n)
    xs = _dispatch_rows(x2d, slots, ends, n_rows)
    ys = _moe_grouped(xs, wg, wu, wd, *visits)
    return _combine_ln(x2d, route, slots, ys, g, b)


def _layout_in_proj(w_in, b_in):
    sizes = (MLSTM_W, MLSTM_W, MLSTM_W, MLSTM_W, MLSTM_HEADS, MLSTM_HEADS,
             ATTN_W, ATTN_KV_W, ATTN_KV_W, SGU_W, SGU_W)
    offs = np.concatenate([[0], np.cumsum(sizes)])
    seg = lambda a, i: a[int(offs[i]):int(offs[i + 1])]

    def build(a):
        tail = a.shape[1:]
        aq = seg(a, 6).reshape((ATTN_Q_HEADS, HEAD_DIM) + tail)[np.array(ATTN_HEAD_ORDER)]
        gates = jnp.concatenate([seg(a, 4), seg(a, 5), jnp.zeros((LANES - 2 * MLSTM_HEADS,) + tail, a.dtype)], 0)
        return jnp.concatenate([seg(a, 0), seg(a, 1), seg(a, 2), seg(a, 3), aq.reshape((ATTN_W,) + tail),
                                gates, seg(a, 7), seg(a, 8), seg(a, 9), seg(a, 10)], 0)

    return build(w_in.T).astype(BF16), build(b_in)[None, :]


def _rope_tables(positions):
    inv_freq = ROPE_THETA ** (-jnp.arange(0, ROPE_DIM, 2, dtype=F32) / ROPE_DIM)
    ang = inv_freq[None, :, None] * positions.astype(F32)[:, None, :]
    cos, sin = jnp.cos(ang), jnp.sin(ang)
    ones = jnp.ones((ang.shape[0], HEAD_DIM - ROPE_DIM, ang.shape[2]), F32)
    cos_head = jnp.concatenate([cos, cos, ones], 1)
    sin_head = jnp.concatenate([-sin, sin, 0.0 * ones], 1)
    cos_t = jnp.concatenate([cos_head, cos_head], 1)
    sin_t = jnp.concatenate([sin_head, sin_head], 1)
    return jnp.swapaxes(cos_t, 1, 2), jnp.swapaxes(sin_t, 1, 2)


def kernel(x, positions, w_in, b_in, conv_w, mlstm_norm_g, attn_sinks, sgu_w_s, sgu_b_s, sgu_norm_g, sgu_norm_b, w_out, ln1_g, ln1_b, ln2_g, ln2_b, ffn_w_gate, ffn_w_up, ffn_w_down, moe_w_router, moe_b_router, moe_w_gate, moe_w_up, moe_w_down):
    bsz, seq, _ = x.shape
    n = bsz * seq
    cos_t, sin_t = _rope_tables(positions)
    tril = jnp.tril(jnp.ones((CHUNK, CHUNK), bool))
    x2d = x.reshape(n, D_MODEL)
    for layer in range(DEPTH):
        w_p, b_p = _layout_in_proj(w_in[layer], b_in[layer])
        proj3 = _inproj(x2d, w_p, b_p).reshape(bsz, seq, D_PROJ_PAD)
        h_a = _mlstm(proj3, conv_w[layer], mlstm_norm_g[layer][None, :])
        h_b = _swa(proj3, cos_t, sin_t, attn_sinks[layer])
        w_tril = jnp.where(tril, sgu_w_s[layer], 0.0).astype(BF16)
        bias_tok = jnp.repeat(sgu_b_s[layer].T, HEAD_DIM, axis=1)
        h_c = _sgu(proj3, w_tril, bias_tok, sgu_norm_g[layer][None, :], sgu_norm_b[layer][None, :])
        wo = w_out[layer]
        wa = wo[:MLSTM_W].astype(BF16)
        wb = wo[MLSTM_W:MLSTM_W + ATTN_W].reshape(ATTN_Q_HEADS, HEAD_DIM, D_MODEL)[np.array(ATTN_HEAD_ORDER)]
        wb = wb.reshape(ATTN_W, D_MODEL).astype(BF16)
        wc = wo[MLSTM_W + ATTN_W:].astype(BF16)
        x2d = _outproj_ln(h_a.reshape(n, MLSTM_W), h_b.reshape(n, ATTN_W), h_c.reshape(n, SGU_W), x2d,
                          wa, wb, wc, ln1_g[layer][None, :], ln1_b[layer][None, :])
        j = layer // 2
        g2, b2 = ln2_g[layer][None, :], ln2_b[layer][None, :]
        if layer % 2 == 0:
            x2d = _ffn_ln(x2d, ffn_w_gate[j].astype(BF16), ffn_w_up[j].astype(BF16),
                          ffn_w_down[j].astype(BF16), g2, b2)
        else:
            x2d = _moe_ln(x2d, moe_w_router[j], moe_b_router[j], moe_w_gate[j], moe_w_up[j],
                          moe_w_down[j], g2, b2)
    return x2d.reshape(bsz, seq, D_MODEL)
```

```python
import functools

import jax
import jax.numpy as jnp
import numpy as np
from jax import lax
from jax.experimental import pallas as pl
from jax.experimental.pallas import tpu as pltpu

F32 = jnp.float32
BF16 = jnp.bfloat16

D_MODEL = 1024
HEAD_DIM = 64
LANES = 128
MLSTM_HEADS = 6
ATTN_Q_HEADS = 6
ATTN_KV_HEADS = 2
SGU_GROUPS = 4
MLSTM_W = MLSTM_HEADS * HEAD_DIM
ATTN_W = ATTN_Q_HEADS * HEAD_DIM
ATTN_KV_W = ATTN_KV_HEADS * HEAD_DIM
SGU_W = SGU_GROUPS * HEAD_DIM
CHUNK = 128
CONV_WIDTH = 4
ROPE_DIM = HEAD_DIM // 4
ROPE_THETA = 500000.0
N_EXPERTS = 8
TOP_K = 2
DEPTH = 2
DN_ALPHA = (2.0 * DEPTH) ** 0.25
LN_EPS = 1e-5

QK_OFF, V_OFF, O_OFF, AQ_OFF = 0, 768, 1152, 1536
GATE_OFF, AK_OFF, AV_OFF, SU_OFF, SV_OFF = 1920, 2048, 2176, 2304, 2560
D_PROJ_PAD = 2816
ATTN_HEAD_ORDER = (0, 3, 1, 4, 2, 5)

VMEM_LIMIT = 56 * 1024 * 1024

SGU_CHUNKS_PER_STEP = 4
MLSTM_CHUNKS_PER_STEP = 4
SWA_BLOCKS_PER_STEP = 2
ROW_TILE = 512
FFN_ROWS = 2048
FF_TILE = 256
MOE_SUB = 512
MOE_VISIT_SUBS = 9
MOE_FF_TILE = 512


def _params(*sem):
    return pltpu.CompilerParams(dimension_semantics=sem, vmem_limit_bytes=VMEM_LIMIT)


def _lane_lo(shape):
    return lax.broadcasted_iota(jnp.int32, shape, len(shape) - 1) < HEAD_DIM


def _layer_norm_rows(z, g, b):
    mu = jnp.mean(z, axis=-1, keepdims=True)
    zc = z - mu
    var = jnp.mean(zc * zc, axis=-1, keepdims=True)
    return zc * lax.rsqrt(var + LN_EPS) * g + b


def _half_layer_norm(x, lo):
    inv = 1.0 / HEAD_DIM
    s_lo = jnp.sum(jnp.where(lo, x, 0.0), axis=-1, keepdims=True)
    s_all = jnp.sum(x, axis=-1, keepdims=True)
    mu = jnp.where(lo, s_lo, s_all - s_lo) * inv
    xc = x - mu
    sq = xc * xc
    q_lo = jnp.sum(jnp.where(lo, sq, 0.0), axis=-1, keepdims=True)
    q_all = jnp.sum(sq, axis=-1, keepdims=True)
    var = jnp.where(lo, q_lo, q_all - q_lo) * inv
    return xc * lax.rsqrt(var + LN_EPS)


def _split3(a):
    h1 = a.astype(BF16)
    r1 = a - h1.astype(F32)
    h2 = r1.astype(BF16)
    r2 = r1 - h2.astype(F32)
    return h1, h2, r2.astype(BF16)


def _split3_masked(a):
    def top(x):
        bits = lax.bitcast_convert_type(x, jnp.int32) & jnp.int32(-65536)
        return lax.bitcast_convert_type(bits, F32)

    h1 = top(a)
    r1 = a - h1
    h2 = top(r1)
    return h1.astype(BF16), h2.astype(BF16), (r1 - h2).astype(BF16)


def _dot(a, b):
    return jnp.dot(a, b, preferred_element_type=F32)


def _inproj_kernel(x_ref, w_ref, b_ref, o_ref):
    o_ref[...] = lax.dot_general(x_ref[...].astype(BF16), w_ref[...], (((1,), (1,)), ((), ())),
                                 preferred_element_type=F32) + b_ref[...]


def _inproj(x2d, w, b):
    n = x2d.shape[0]
    return pl.pallas_call(
        _inproj_kernel,
        grid=(n // ROW_TILE,),
        in_specs=[pl.BlockSpec((ROW_TILE, D_MODEL), lambda i: (i, 0)),
                  pl.BlockSpec((D_PROJ_PAD, D_MODEL), lambda i: (0, 0)),
                  pl.BlockSpec((1, D_PROJ_PAD), lambda i: (0, 0))],
        out_specs=pl.BlockSpec((ROW_TILE, D_PROJ_PAD), lambda i: (i, 0)),
        out_shape=jax.ShapeDtypeStruct((n, D_PROJ_PAD), F32),
        compiler_params=_params("parallel"),
        name="inproj",
    )(x2d, w, b)


def _mlstm_kernel(qk_ref, v_ref, o_ref, gate_ref, cw_ref, g_ref, out_ref, buf_ref, ct_ref, m_ref):
    rows = MLSTM_CHUNKS_PER_STEP * CHUNK

    @pl.when(pl.program_id(1) == 0)
    def _():
        buf_ref[0:8, :] = jnp.zeros((8, 2 * MLSTM_W), F32)
        ct_ref[...] = jnp.zeros_like(ct_ref)
        m_ref[...] = jnp.zeros_like(m_ref)

    buf_ref[8:8 + rows, :] = qk_ref[...]
    ct = [ct_ref[p] for p in range(MLSTM_HEADS // 2)]
    m = [m_ref[h:h + 1, 0:1] for h in range(MLSTM_HEADS)]
    for i in range(MLSTM_CHUNKS_PER_STEP):
        blk = pl.ds(i * CHUNK, CHUNK)
        ct, m = _mlstm_chunk(i * CHUNK, buf_ref, v_ref.at[blk], o_ref.at[blk], gate_ref.at[blk], cw_ref, g_ref,
                             out_ref.at[blk], ct, m)
    buf_ref[0:8, :] = qk_ref[rows - 8:rows, :]
    for p in range(MLSTM_HEADS // 2):
        ct_ref[p] = ct[p]
    for h in range(MLSTM_HEADS):
        m_ref[h:h + 1, :] = jnp.broadcast_to(m[h], (1, LANES))


def _mlstm_chunk(row0, buf_ref, v_ref, o_ref, gate_ref, cw_ref, g_ref, out_ref, ct_in, m_in):
    cw = cw_ref[...]
    base = row0 + 8 - (CONV_WIDTH - 1)
    acc = cw[0:1, :] * buf_ref[base:base + CHUNK, :]
    for j in range(1, CONV_WIDTH):
        acc = acc + cw[j:j + 1, :] * buf_ref[base + j:base + j + CHUNK, :]
    qk = jax.nn.silu(acc)

    gates = gate_ref[...]
    lf = jax.nn.log_sigmoid(gates)
    row = lax.broadcasted_iota(jnp.int32, (CHUNK, CHUNK), 0)
    col = lax.broadcasted_iota(jnp.int32, (CHUNK, CHUNK), 1)
    causal = col <= row
    tri = causal.astype(BF16)
    l1, l2, l3 = _split3(lf)
    cum = _dot(tri, l1) + _dot(tri, l2) + _dot(tri, l3)
    gates_t = gates.T
    cum_t = cum.T
    cum_parts = _split3_masked(cum)

    lo = _lane_lo((CHUNK, LANES))
    ct_out = []
    m_out = []
    for p in range(MLSTM_HEADS // 2):
        sl = slice(p * LANES, (p + 1) * LANES)
        q_slab = qk[:, sl] * (HEAD_DIM ** -0.5)
        k_slab = qk[:, MLSTM_W + p * LANES:MLSTM_W + (p + 1) * LANES]
        kt_slab = k_slab.T
        v_slab = v_ref[:, sl]
        ct_pair = ct_in[p]
        ct_pair_b = ct_pair.astype(BF16)
        halves = []
        new_ct = []
        for half in range(2):
            h = 2 * p + half
            sel = lo if half == 0 else jnp.logical_not(lo)
            li_row = gates_t[h:h + 1, :]
            bc_row = cum_t[MLSTM_HEADS + h:MLSTM_HEADS + h + 1, :]
            pick = (row == MLSTM_HEADS + h).astype(BF16)
            bc_col = _dot(cum_parts[0], pick) + _dot(cum_parts[1], pick) + _dot(cum_parts[2], pick)
            b_tot = bc_row[:, CHUNK - 1:CHUNK]
            m_prev = m_in[h]

            dmat = jnp.where(causal, bc_col - bc_row + li_row, -jnp.inf)
            m_inter = bc_col + m_prev
            m_row = jnp.maximum(m_inter, jnp.max(dmat, axis=-1, keepdims=True))
            q_m = jnp.where(sel, q_slab, 0.0).astype(BF16)
            s = _dot(q_m, kt_slab.astype(BF16))
            pmat = (s * jnp.exp(dmat - m_row)).astype(BF16)
            w_inter = jnp.exp(m_inter - m_row)
            v_aug = jnp.where(sel, v_slab, 1.0).astype(BF16)
            halves.append((_dot(pmat, v_aug) + w_inter * _dot(q_m, ct_pair_b), jnp.exp(-m_row)))

            a_row = b_tot - bc_row + li_row
            m_loc = jnp.max(a_row, axis=-1, keepdims=True)
            w_row = jnp.exp(a_row - m_loc)
            kt_h = kt_slab[half * HEAD_DIM:(half + 1) * HEAD_DIM, :]
            ct_loc = _dot((kt_h * w_row).astype(BF16), v_aug)
            m_new = jnp.maximum(b_tot + m_prev, m_loc)
            s_old = jnp.exp(b_tot + m_prev - m_new)
            s_loc = jnp.exp(m_loc - m_new)
            new_ct.append(s_old * ct_pair[half * HEAD_DIM:(half + 1) * HEAD_DIM, :] + s_loc * ct_loc)
            m_out.append(m_new)

        ct_out.append(jnp.concatenate(new_ct, axis=0))
        (r_even, stab_even), (r_odd, stab_odd) = halves
        num = jnp.where(lo, r_even, r_odd)
        den = pltpu.roll(jnp.where(lo, r_odd, r_even), HEAD_DIM, 1)
        hh = num / jnp.maximum(jnp.abs(den), jnp.where(lo, stab_even, stab_odd))
        hn = _half_layer_norm(hh, lo) * g_ref[:, sl]
        out_ref[:, sl] = (hn * jax.nn.sigmoid(o_ref[:, sl])).astype(out_ref.dtype)
    return ct_out, m_out


def _mlstm(proj3, conv_w, norm_g):
    b, s, _ = proj3.shape
    rows = MLSTM_CHUNKS_PER_STEP * CHUNK
    assert s % rows == 0
    blk = lambda w, off: pl.BlockSpec((None, rows, w), lambda i, c: (i, c, off // w))
    return pl.pallas_call(
        _mlstm_kernel,
        grid=(b, s // rows),
        in_specs=[blk(2 * MLSTM_W, QK_OFF), blk(MLSTM_W, V_OFF), blk(MLSTM_W, O_OFF), blk(LANES, GATE_OFF),
                  pl.BlockSpec((CONV_WIDTH, 2 * MLSTM_W), lambda i, c: (0, 0)),
                  pl.BlockSpec((1, MLSTM_W), lambda i, c: (0, 0))],
        out_specs=pl.BlockSpec((None, rows, MLSTM_W), lambda i, c: (i, c, 0)),
        out_shape=jax.ShapeDtypeStruct((b, s, MLSTM_W), BF16),
        scratch_shapes=[pltpu.VMEM((8 + rows, 2 * MLSTM_W), F32),
                        pltpu.VMEM((MLSTM_HEADS // 2, LANES, LANES), F32),
                        pltpu.VMEM((8, LANES), F32)],
        compiler_params=_params("parallel", "arbitrary"),
        name="mlstm",
    )(proj3, proj3, proj3, proj3, conv_w, norm_g)


def _rope(x, cos_t, sin_t, first):
    return x * cos_t + jnp.where(first, pltpu.roll(x, LANES - ROPE_DIM // 2, 1),
                                 pltpu.roll(x, ROPE_DIM // 2, 1)) * sin_t


def _swa_kernel(sink_ref, q_ref, k_ref, v_ref, cos_ref, sin_ref, out_ref, kt_ref, vv_ref):
    first_step = pl.program_id(1) == 0

    @pl.when(first_step)
    def _():
        kt_ref[...] = jnp.zeros_like(kt_ref)
        vv_ref[...] = jnp.zeros_like(vv_ref)

    kt_prev = kt_ref[...]
    v_prev = vv_ref[...]
    for i in range(SWA_BLOCKS_PER_STEP):
        rows = pl.ds(i * CHUNK, CHUNK)
        hide_previous = jnp.where(first_step, 2 * CHUNK, 0) if i == 0 else 0
        kt_prev, v_prev = _swa_block(hide_previous, sink_ref, q_ref.at[rows], k_ref.at[rows], v_ref.at[rows],
                                     cos_ref.at[rows], sin_ref.at[rows], out_ref.at[rows], kt_prev, v_prev)
    kt_ref[...] = kt_prev
    vv_ref[...] = v_prev


def _swa_block(first_block_shift, sink_ref, q_ref, k_ref, v_ref, cos_ref, sin_ref, out_ref, kt_prev, v_prev):
    cos_t = cos_ref[...]
    sin_t = sin_ref[...]
    lane = lax.broadcasted_iota(jnp.int32, (CHUNK, LANES), 1)
    lo = lane < HEAD_DIM
    first = (lane % HEAD_DIM) < ROPE_DIM // 2

    kt_cur = _rope(k_ref[...], cos_t, sin_t, first).T.astype(BF16)
    v_cur = v_ref[...].astype(BF16)
    kt = jnp.concatenate([kt_prev, kt_cur], axis=1)
    vv = jnp.concatenate([v_prev, v_cur], axis=0)

    row = lax.broadcasted_iota(jnp.int32, (CHUNK, 2 * CHUNK), 0)
    col = lax.broadcasted_iota(jnp.int32, (CHUNK, 2 * CHUNK), 1)
    visible = jnp.logical_or(jnp.logical_and(col < CHUNK, col > row + first_block_shift),
                             jnp.logical_and(col >= CHUNK, col - CHUNK <= row))

    for j in range(ATTN_Q_HEADS // 2):
        sl = slice(j * LANES, (j + 1) * LANES)
        q_slab = _rope(q_ref[:, sl], cos_t, sin_t, first) * (HEAD_DIM ** -0.5)
        outs = []
        for half in range(2):
            sel = lo if half == 0 else jnp.logical_not(lo)
            sink = sink_ref[ATTN_HEAD_ORDER[2 * j + half]]
            q_m = jnp.where(sel, q_slab, 0.0).astype(BF16)
            s = jnp.where(visible, _dot(q_m, kt), -jnp.inf)
            mx = jnp.maximum(jnp.max(s, axis=-1, keepdims=True), sink)
            pexp = jnp.exp(s - mx)
            denom = jnp.sum(pexp, axis=-1, keepdims=True) + jnp.exp(sink - mx)
            outs.append(_dot(pexp.astype(BF16), vv) / denom)
        out_ref[:, sl] = jnp.where(lo, outs[0], outs[1]).astype(out_ref.dtype)

    return kt_cur, v_cur


def _swa(proj3, cos_t, sin_t, sinks):
    b, s, _ = proj3.shape
    rows = SWA_BLOCKS_PER_STEP * CHUNK
    assert s % rows == 0
    blk = lambda w, off: pl.BlockSpec((None, rows, w), lambda i, c: (i, c, off // w))
    tab =pl.BlockSpec((None, rows, LANES), lambda i, c: (i, c, 0))
    return pl.pallas_call(
        _swa_kernel,
        grid=(b, s // rows),
        in_specs=[pl.BlockSpec(memory_space=pltpu.SMEM),
                  blk(ATTN_W, AQ_OFF), blk(ATTN_KV_W, AK_OFF), blk(ATTN_KV_W, AV_OFF), tab, tab],
        out_specs=pl.BlockSpec((None, rows, ATTN_W), lambda i, c: (i, c, 0)),
        out_shape=jax.ShapeDtypeStruct((b, s, ATTN_W), BF16),
        scratch_shapes=[pltpu.VMEM((LANES, CHUNK), BF16), pltpu.VMEM((CHUNK, LANES), BF16)],
        compiler_params=_params("parallel", "arbitrary"),
        name="swa",
    )(sinks, proj3, proj3, proj3, cos_t, sin_t)


def _sgu_kernel(u_ref, v_ref, w_ref, bias_ref, g_ref, b_ref, out_ref):
    lo = _lane_lo((CHUNK, LANES))
    for c in range(SGU_CHUNKS_PER_STEP):
        rows = slice(c * CHUNK, (c + 1) * CHUNK)
        for j in range(SGU_GROUPS // 2):
            sl = slice(j * LANES, (j + 1) * LANES)
            u = jax.nn.gelu(u_ref[rows, sl])
            v = jax.nn.gelu(v_ref[rows, sl])
            vn = (_half_layer_norm(v, lo) * g_ref[:, sl] + b_ref[:, sl]).astype(BF16)
            mixed = jnp.where(lo, _dot(w_ref[2 * j], vn), _dot(w_ref[2 * j + 1], vn)) + bias_ref[:, sl]
            out_ref[rows, sl] = (u * mixed).astype(out_ref.dtype)


def _sgu(proj3, w_tril, bias_tok, norm_g, norm_b):
    b, s, _ = proj3.shape
    rows = SGU_CHUNKS_PER_STEP * CHUNK
    assert s % rows == 0
    blk = lambda w, off: pl.BlockSpec((None, rows, w), lambda i, c: (i, c, off // w))
    const2 = lambda shape: pl.BlockSpec(shape, lambda i, c: (0,) * len(shape))
    return pl.pallas_call(
        _sgu_kernel,
        grid=(b, s // rows),
        in_specs=[blk(SGU_W, SU_OFF), blk(SGU_W, SV_OFF), const2((SGU_GROUPS, CHUNK, CHUNK)),
                  const2((CHUNK, SGU_W)), const2((1, SGU_W)), const2((1, SGU_W))],
        out_specs=pl.BlockSpec((None, rows, SGU_W), lambda i, c: (i, c, 0)),
        out_shape=jax.ShapeDtypeStruct((b, s, SGU_W), BF16),
        compiler_params=_params("parallel", "parallel"),
        name="sgu",
    )(proj3, proj3, w_tril, bias_tok, norm_g, norm_b)


def _outproj_kernel(ha_ref, hb_ref, hc_ref, x_ref, wa_ref, wb_ref, wc_ref, g_ref, b_ref, o_ref):
    mix = _dot(ha_ref[...], wa_ref[...]) + _dot(hb_ref[...], wb_ref[...]) + _dot(hc_ref[...], wc_ref[...])
    o_ref[...] = _layer_norm_rows(DN_ALPHA * x_ref[...] + mix, g_ref[...], b_ref[...])


def _outproj_ln(ha, hb, hc, x2d, wa, wb, wc, g, b):
    n = x2d.shape[0]
    rows = lambda w: pl.BlockSpec((ROW_TILE, w), lambda i: (i, 0))
    const = lambda shape: pl.BlockSpec(shape, lambda i: (0, 0))
    return pl.pallas_call(
        _outproj_kernel,
        grid=(n // ROW_TILE,),
        in_specs=[rows(MLSTM_W), rows(ATTN_W), rows(SGU_W), rows(D_MODEL),
                  const((MLSTM_W, D_MODEL)), const((ATTN_W, D_MODEL)), const((SGU_W, D_MODEL)),
                  const((1, D_MODEL)), const((1, D_MODEL))],
        out_specs=rows(D_MODEL),
        out_shape=jax.ShapeDtypeStruct((n, D_MODEL), F32),
        compiler_params=_params("parallel"),
        name="outproj_ln",
    )(ha, hb, hc, x2d, wa, wb, wc, g, b)


def _ffn_kernel(x_ref, wg_ref, wu_ref, wd_ref, g_ref, b_ref, o_ref, xb_ref, wgu_ref):
    j = pl.program_id(1)
    last = pl.num_programs(1) - 1
    n_sub = FFN_ROWS // ROW_TILE
    wgu_ref[:, :FF_TILE] = wg_ref[...]
    wgu_ref[:, FF_TILE:] = wu_ref[...]

    def sub_rows(m):
        return pl.ds(pl.multiple_of(m * ROW_TILE, ROW_TILE), ROW_TILE)

    def ffn_tile(m):
        gu = _dot(xb_ref[sub_rows(m), :], wgu_ref[...])
        hidden = (jax.nn.silu(gu[:, :FF_TILE]) * gu[:, FF_TILE:]).astype(BF16)
        return _dot(hidden, wd_ref[...])

    @pl.when(j == 0)
    def _():
        def body(m, carry):
            xb_ref[sub_rows(m), :] = x_ref[sub_rows(m), :].astype(BF16)
            o_ref[sub_rows(m), :] = ffn_tile(m)
            return carry

        lax.fori_loop(0, n_sub, body, 0)

    @pl.when(jnp.logical_and(j > 0, j < last))
    def _():
        def body(m, carry):
            o_ref[sub_rows(m), :] += ffn_tile(m)
            return carry

        lax.fori_loop(0, n_sub, body, 0)

    @pl.when(j == last)
    def _():
        def body(m, carry):
            rows = sub_rows(m)
            ff = o_ref[rows, :] + ffn_tile(m)
            o_ref[rows, :] = _layer_norm_rows(DN_ALPHA * x_ref[rows, :] + ff, g_ref[...], b_ref[...])
            return carry

        lax.fori_loop(0, n_sub, body, 0)


def _ffn_ln(x2d, wg, wu, wd, g, b):
    n = x2d.shape[0]
    d_ff = wd.shape[0]
    assert d_ff // FF_TILE >= 2
    tm = FFN_ROWS
    return pl.pallas_call(
        _ffn_kernel,
        grid=(n // tm, d_ff // FF_TILE),
        in_specs=[pl.BlockSpec((tm, D_MODEL), lambda i, j: (i, 0)),
                  pl.BlockSpec((D_MODEL, FF_TILE), lambda i, j: (0, j)),
                  pl.BlockSpec((D_MODEL, FF_TILE), lambda i, j: (0, j)),
                  pl.BlockSpec((FF_TILE, D_MODEL), lambda i, j: (j, 0)),
                  pl.BlockSpec((1, D_MODEL), lambda i, j: (0, 0)),
                  pl.BlockSpec((1, D_MODEL), lambda i, j: (0, 0))],
        out_specs=pl.BlockSpec((tm, D_MODEL), lambda i, j: (i, 0)),
        out_shape=jax.ShapeDtypeStruct((n, D_MODEL), F32),
        scratch_shapes=[pltpu.VMEM((tm, D_MODEL), BF16), pltpu.VMEM((D_MODEL, 2 * FF_TILE), BF16)],
        compiler_params=_params("parallel", "arbitrary"),
        name="ffn_ln",
    )(x2d, wg, wu, wd, g, b)


def _router_kernel(x_ref, wh_ref, wl_ref, b_ref, o_ref, ot_ref):
    x = x_ref[...]
    xh = x.astype(BF16)
    xl = (x - xh.astype(F32)).astype(BF16)
    logits = _dot(xh, wh_ref[...]) + _dot(xl, wh_ref[...]) + _dot(xh, wl_ref[...]) + b_ref[...]
    lane = lax.broadcasted_iota(jnp.int32, logits.shape, 1)
    logits = jnp.where(lane < N_EXPERTS, logits, -jnp.inf)
    m1 = jnp.max(logits, axis=-1, keepdims=True)
    i1 = jnp.min(jnp.where(logits == m1, lane, LANES), axis=-1, keepdims=True)
    rest = jnp.where(lane == i1, -jnp.inf, logits)
    m2 = jnp.max(rest, axis=-1, keepdims=True)
    i2 = jnp.min(jnp.where(rest == m2, lane, LANES), axis=-1, keepdims=True)
    e2 = jnp.exp(m2 - m1)
    g1 = 1.0 / (1.0 + e2)
    g2 = e2 / (1.0 + e2)
    route = jnp.where(lane == 0, g1, jnp.where(lane == 1, g2, jnp.where(
        lane == 2, i1.astype(F32), jnp.where(lane == 3, i2.astype(F32), 0.0))))
    o_ref[...] = route
    ot_ref[...] = route.T[0:8, :]


def _router(x2d, w_hi, w_lo, b):
    n = x2d.shape[0]
    const = lambda shape: pl.BlockSpec(shape, lambda i: (0, 0))
    return pl.pallas_call(
        _router_kernel,
        grid=(n // ROW_TILE,),
        in_specs=[pl.BlockSpec((ROW_TILE, D_MODEL), lambda i: (i, 0)),
                  const((D_MODEL, LANES)), const((D_MODEL, LANES)), const((1, LANES))],
        out_specs=[pl.BlockSpec((ROW_TILE, LANES), lambda i: (i, 0)), pl.BlockSpec((8, ROW_TILE), lambda i: (0, i))],
        out_shape=[jax.ShapeDtypeStruct((n, LANES), F32), jax.ShapeDtypeStruct((8, n), F32)],
        compiler_params=_params("parallel"),
        name="router",
    )(x2d, w_hi, w_lo, b)


def _dispatch_kernel(slot0_ref, slot1_ref, ends_ref, x_ref, o_hbm, stage_ref, zero_ref, sem, zero_sem):
    t = pl.program_id(0)
    last = pl.num_programs(0) - 1
    s = t % 2

    def wait_tile(ss):
        for _ in range(TOP_K):
            pltpu.make_async_copy(zero_ref, o_hbm.at[pl.ds(0, MOE_SUB), :], sem.at[ss]).wait()

    @pl.when(t == 0)
    def _():
        zero_ref[...] = jnp.zeros_like(zero_ref)

        def zero_sub_tile(m, carry):
            dst = o_hbm.at[pl.ds(pl.multiple_of(m * MOE_SUB, MOE_SUB), MOE_SUB), :]
            cp = pltpu.make_async_copy(zero_ref, dst, zero_sem)
            cp.start()
            cp.wait()
            return carry

        for e in range(N_EXPERTS):
            @pl.when(ends_ref[e] >= MOE_SUB)
            def _():
                zero_sub_tile(ends_ref[e] // MOE_SUB - 1, 0)

        lax.fori_loop(ends_ref[N_EXPERTS - 1] // MOE_SUB, o_hbm.shape[0] // MOE_SUB, zero_sub_tile, 0)

    @pl.when(t >= 2)
    def _():
        wait_tile(s)

    stage_ref[s] = x_ref[...].reshape(MOE_SUB // 8, 8, D_MODEL)

    def issue(i, carry):
        for u in range(8):
            for slot_ref in (slot0_ref, slot1_ref):
                pltpu.make_async_copy(stage_ref.at[s, i, pl.ds(u, 1), :],
                                      o_hbm.at[pl.ds(slot_ref[t * MOE_SUB + i * 8 + u], 1), :], sem.at[s]).start()
        return carry

    lax.fori_loop(0, MOE_SUB // 8, issue, 0)

    @pl.when(t == last)
    def _():
        wait_tile(s)
        wait_tile(1 - s)


def _dispatch_rows(x2d, slots, ends, n_rows):
    n = x2d.shape[0]
    assert n // MOE_SUB >= 2
    return pl.pallas_call(
        _dispatch_kernel,
        grid_spec=pltpu.PrefetchScalarGridSpec(
            num_scalar_prefetch=3,
            grid=(n // MOE_SUB,),
            in_specs=[pl.BlockSpec((MOE_SUB, D_MODEL), lambda t, s0, s1, en: (t, 0))],
            out_specs=pl.BlockSpec(memory_space=pl.ANY),
            scratch_shapes=[pltpu.VMEM((2, MOE_SUB // 8, 8, D_MODEL), F32), pltpu.VMEM((MOE_SUB, D_MODEL), F32),
                            pltpu.SemaphoreType.DMA((2,)), pltpu.SemaphoreType.DMA(())]),
        out_shape=jax.ShapeDtypeStruct((n_rows, D_MODEL), F32),
        compiler_params=_params("arbitrary"),
        name="moe_dispatch",
    )(slots[0], slots[1], ends, x2d)


def _moe_kernel(exp_ref, row0_ref, nsub_ref, tail_ref, x_hbm, wg_ref, wu_ref, wd_ref, y_hbm,
                xb_ref, acc_ref, wgu_ref, wdb_ref, stage_ref, in_sem, out_sem):
    v = pl.program_id(0)
    j = pl.program_id(1)
    last = pl.num_programs(1) - 1
    n_sub = nsub_ref[v]
    row0 = row0_ref[v]

    def sub_rows(m):
        return pl.ds(pl.multiple_of(m * MOE_SUB, MOE_SUB), MOE_SUB)

    def hbm_rows(m):
        return pl.ds(pl.multiple_of(row0 + m * MOE_SUB, MOE_SUB), MOE_SUB)

    @pl.when(jnp.logical_and(v == 0, j == 0))
    def _():
        stage_ref[0] = jnp.zeros((MOE_SUB, D_MODEL), F32)

        def zero_sub_tile(m, carry):
            cp = pltpu.make_async_copy(stage_ref.at[0], y_hbm.at[sub_rows(m), :], out_sem)
            cp.start()
            cp.wait()
            return carry

        lax.fori_loop(tail_ref[0], y_hbm.shape[0] // MOE_SUB, zero_sub_tile, 0)

    @pl.when(n_sub > 0)
    def _():
        wgu_ref[:, :MOE_FF_TILE] = wg_ref[...].astype(BF16)
        wgu_ref[:, MOE_FF_TILE:] = wu_ref[...].astype(BF16)
        wdb_ref[...] = wd_ref[...].astype(BF16)

        def ffn_tile(m):
            xs = xb_ref[sub_rows(m), :]
            gu = _dot(xs, wgu_ref[...])
            hidden = (jax.nn.silu(gu[:, :MOE_FF_TILE]) * gu[:, MOE_FF_TILE:]).astype(BF16)
            return _dot(hidden, wdb_ref[...])

        def y_copy(m):
            return pltpu.make_async_copy(acc_ref.at[sub_rows(m), :], y_hbm.at[hbm_rows(m), :], out_sem)

        @pl.when(j == 0)
        def _():
            def x_copy(m):
                return pltpu.make_async_copy(x_hbm.at[hbm_rows(m), :], stage_ref.at[m % 2], in_sem.at[m % 2])

            x_copy(0).start()

            def body(m, carry):
                @pl.when(m + 1 < n_sub)
                def _():
                    x_copy(m + 1).start()

                x_copy(m).wait()
                xb_ref[sub_rows(m), :] = stage_ref[m % 2].astype(BF16)
                acc_ref[sub_rows(m), :] = ffn_tile(m)
                return carry

            lax.fori_loop(0, n_sub, body, 0)

        @pl.when(jnp.logical_and(j > 0, j < last))
        def _():
            def body(m, carry):
                acc_ref[sub_rows(m), :] += ffn_tile(m)
                return carry

            lax.fori_loop(0, n_sub, body, 0)

        @pl.when(j == last)
        def _():
            def body(m, carry):
                acc_ref[sub_rows(m), :] += ffn_tile(m)
                y_copy(m).start()
                return carry

            def drain(m, carry):
                y_copy(m).wait()
                return carry

            lax.fori_loop(0, n_sub, body, 0)
            lax.fori_loop(0, n_sub, drain, 0)


def _moe_grouped(xs, wg, wu, wd, visit_exp, visit_row0, visit_nsub, tail_sub):
    n_rows = xs.shape[0]
    n_visits = visit_exp.shape[0]
    d_ff = wg.shape[2]
    assert d_ff // MOE_FF_TILE >= 2
    rows = MOE_VISIT_SUBS * MOE_SUB
    w_in = lambda v, j, e, r, ns, tl: (e[v], 0, j)
    w_out = lambda v, j, e, r, ns, tl: (e[v], j, 0)
    return pl.pallas_call(
        _moe_kernel,
        grid_spec=pltpu.PrefetchScalarGridSpec(
            num_scalar_prefetch=4,
            grid=(n_visits, d_ff // MOE_FF_TILE),
            in_specs=[pl.BlockSpec(memory_space=pl.ANY),
                      pl.BlockSpec((None, D_MODEL, MOE_FF_TILE), w_in),
                      pl.BlockSpec((None, D_MODEL, MOE_FF_TILE), w_in),
                      pl.BlockSpec((None, MOE_FF_TILE, D_MODEL), w_out)],
            out_specs=pl.BlockSpec(memory_space=pl.ANY),
            scratch_shapes=[pltpu.VMEM((rows, D_MODEL), BF16), pltpu.VMEM((rows, D_MODEL), F32),
                            pltpu.VMEM((D_MODEL, 2 * MOE_FF_TILE), BF16),
                            pltpu.VMEM((MOE_FF_TILE, D_MODEL), BF16),
                            pltpu.VMEM((2, MOE_SUB, D_MODEL), F32),
                            pltpu.SemaphoreType.DMA((2,)), pltpu.SemaphoreType.DMA(())]),
        out_shape=jax.ShapeDtypeStruct((n_rows, D_MODEL), F32),
        compiler_params=_params("arbitrary", "arbitrary"),
        name="moe_grouped",
    )(visit_exp, visit_row0, visit_nsub, tail_sub, xs, wg, wu, wd)


def _combine_kernel(slot0_ref, slot1_ref, x_ref, gate_ref, y_hbm, y_grouped_hbm, g_ref, b_ref, o_ref, buf_ref, sem):
    t = pl.program_id(0)
    s = t % 2

    def issue_tile(tt, ss):
        def issue(i, carry):
            for u in range(8):
                for k, slot_ref in enumerate((slot0_ref, slot1_ref)):
                    pltpu.make_async_copy(y_hbm.at[pl.ds(slot_ref[tt * MOE_SUB + i * 8 + u], 1), :],
                                          buf_ref.at[ss, k, i, pl.ds(u, 1), :], sem.at[ss]).start()
            return carry

        lax.fori_loop(0, MOE_SUB // 8, issue, 0)

    @pl.when(t == 0)
    def _():
        issue_tile(0, 0)

    @pl.when(t + 1 < pl.num_programs(0))
    def _():
        issue_tile(t + 1, 1 - s)

    for k in range(TOP_K):
        pltpu.make_async_copy(y_grouped_hbm.at[pl.ds(0, MOE_SUB // 8)], buf_ref.at[s, k], sem.at[s]).wait()
    gate = gate_ref[...]
    y0 = buf_ref[s, 0].reshape(MOE_SUB, D_MODEL)
    y1 = buf_ref[s, 1].reshape(MOE_SUB, D_MODEL)
    ff = gate[:, 0:1] * y0 + gate[:, 1:2] * y1
    o_ref[...] = _layer_norm_rows(DN_ALPHA * x_ref[...] + ff, g_ref[...], b_ref[...])


def _combine_ln(x2d, route, slots, ys, g, b):
    n = x2d.shape[0]
    return pl.pallas_call(
        _combine_kernel,
        grid_spec=pltpu.PrefetchScalarGridSpec(
            num_scalar_prefetch=2,
            grid=(n // MOE_SUB,),
            in_specs=[pl.BlockSpec((MOE_SUB, D_MODEL), lambda t, s0, s1: (t, 0)),
                      pl.BlockSpec((MOE_SUB, LANES), lambda t, s0, s1: (t, 0)),
                      pl.BlockSpec(memory_space=pl.ANY), pl.BlockSpec(memory_space=pl.ANY),
                      pl.BlockSpec((1, D_MODEL), lambda t, s0, s1: (0, 0)),
                      pl.BlockSpec((1, D_MODEL), lambda t, s0, s1: (0, 0))],
            out_specs=pl.BlockSpec((MOE_SUB, D_MODEL), lambda t, s0, s1: (t, 0)),
            scratch_shapes=[pltpu.VMEM((2, TOP_K, MOE_SUB // 8, 8, D_MODEL), F32),
                            pltpu.SemaphoreType.DMA((2,))]),
        out_shape=jax.ShapeDtypeStruct((n, D_MODEL), F32),
        compiler_params=_params("arbitrary"),
        name="moe_combine_ln",
    )(slots[0], slots[1], x2d, route, ys, ys.reshape(ys.shape[0] // 8, 8, D_MODEL), g, b)


def _routing_tables(route_t, n):
    idx = route_t[2:4].astype(jnp.int32)
    expert = jnp.arange(N_EXPERTS, dtype=jnp.int32)[:, None]
    chosen = [idx[k][None, :] == expert for k in range(TOP_K)]
    onehot = jnp.logical_or(chosen[0], chosen[1]).astype(jnp.int32)
    rank = jnp.cumsum(onehot, axis=1) - onehot
    counts = jnp.sum(onehot, axis=1)
    padded = ((counts + MOE_SUB - 1) // MOE_SUB) * MOE_SUB
    ends = jnp.cumsum(padded)
    starts = ends - padded
    place = starts[:, None] + rank
    slot = [jnp.sum(jnp.where(chosen[k], place, 0), axis=0).astype(jnp.int32) for k in range(TOP_K)]

    n_rows = -(-(n * TOP_K + N_EXPERTS * (MOE_SUB - 1)) // MOE_SUB) * MOE_SUB

    visit_rows = MOE_VISIT_SUBS * MOE_SUB
    max_chunks = -(-n_rows // visit_rows)
    chunk = jnp.arange(max_chunks, dtype=jnp.int32)[None, :]
    left = padded[:, None] - chunk * visit_rows
    valid = (left > 0).reshape(-1)
    n_visits = n_rows // visit_rows + N_EXPERTS
    order = jnp.argsort(jnp.logical_not(valid), stable=True)[:n_visits]
    n_valid = jnp.sum(valid.astype(jnp.int32))
    live = jnp.arange(n_visits) < n_valid
    order = jnp.where(live, order, order[jnp.maximum(n_valid - 1, 0)])
    v_exp = (order // max_chunks).astype(jnp.int32)
    v_chunk = (order % max_chunks).astype(jnp.int32)
    v_row0 = jnp.where(live, starts[v_exp] + v_chunk * visit_rows, 0).astype(jnp.int32)
    v_nsub = jnp.where(live, jnp.minimum(left.reshape(-1)[order], visit_rows) // MOE_SUB, 0).astype(jnp.int32)
    tail_sub = (ends[N_EXPERTS - 1:] // MOE_SUB).astype(jnp.int32)
    return slot, ends.astype(jnp.int32), n_rows, (v_exp, v_row0, v_nsub, tail_sub)


def _moe_ln(x2d, w_router, b_router, wg, wu, wd, g, b):
    n = x2d.shape[0]
    w_pad = jnp.zeros((D_MODEL, LANES), F32).at[:, :N_EXPERTS].set(w_router)
    w_hi = w_pad.astype(BF16)
    w_lo = (w_pad - w_hi.astype(F32)).astype(BF16)
    b_pad = jnp.zeros((1, LANES), F32).at[0, :N_EXPERTS].set(b_router)
    route, route_t = _router(x2d, w_hi, w_lo, b_pad)
    slots, ends, n_rows, visits = _routing_tables(route_t, n)
    xs = _dispatch_rows(x2d, slots, ends, n_rows)
    ys = _moe_grouped(xs, wg, wu, wd, *visits)
    return _combine_ln(x2d, route, slots, ys, g, b)


def _layout_in_proj(w_in, b_in):
    sizes = (MLSTM_W, MLSTM_W, MLSTM_W, MLSTM_W, MLSTM_HEADS, MLSTM_HEADS,
             ATTN_W, ATTN_KV_W, ATTN_KV_W, SGU_W, SGU_W)
    offs = np.concatenate([[0], np.cumsum(sizes)])
    seg = lambda a, i: a[int(offs[i]):int(offs[i + 1])]

    def build(a):
        tail = a.shape[1:]
        aq = seg(a, 6).reshape((ATTN_Q_HEADS, HEAD_DIM) + tail)[np.array(ATTN_HEAD_ORDER)]
        gates = jnp.concatenate([seg(a, 4), seg(a, 5), jnp.zeros((LANES - 2 * MLSTM_HEADS,) + tail, a.dtype)], 0)
        return jnp.concatenate([seg(a, 0), seg(a, 1), seg(a, 2), seg(a, 3), aq.reshape((ATTN_W,) + tail),
                                gates, seg(a, 7), seg(a, 8), seg(a, 9), seg(a, 10)], 0)

    return build(w_in.T).astype(BF16), build(b_in)[None, :]


def _rope_tables(positions):
    inv_freq = ROPE_THETA ** (-jnp.arange(0, ROPE_DIM, 2, dtype=F32) / ROPE_DIM)
    ang = inv_freq[None, :, None] * positions.astype(F32)[:, None, :]
    cos, sin = jnp.cos(ang), jnp.sin(ang)
    ones = jnp.ones((ang.shape[0], HEAD_DIM - ROPE_DIM, ang.shape[2]), F32)
    cos_head = jnp.concatenate([cos, cos, ones], 1)
    sin_head = jnp.concatenate([-sin, sin, 0.0 * ones], 1)
    cos_t = jnp.concatenate([cos_head, cos_head], 1)
    sin_t = jnp.concatenate([sin_head, sin_head], 1)
    cos_t, sin_t = lax.optimization_barrier((cos_t, sin_t))
    return jnp.swapaxes(cos_t, 1, 2), jnp.swapaxes(sin_t, 1, 2)


def kernel(x, positions, w_in, b_in, conv_w, mlstm_norm_g, attn_sinks, sgu_w_s, sgu_b_s, sgu_norm_g, sgu_norm_b, w_out, ln1_g, ln1_b, ln2_g, ln2_b, ffn_w_gate, ffn_w_up, ffn_w_down, moe_w_router, moe_b_router, moe_w_gate, moe_w_up, moe_w_down):
    bsz, seq, _ = x.shape
    n = bsz * seq
    cos_t, sin_t = _rope_tables(positions)
    tril = jnp.tril(jnp.ones((CHUNK, CHUNK), bool))
    x2d = x.reshape(n, D_MODEL)
    for layer in range(DEPTH):
        w_p, b_p = _layout_in_proj(w_in[layer], b_in[layer])
        proj3 = _inproj(x2d, w_p, b_p).reshape(bsz, seq, D_PROJ_PAD)
        h_a = _mlstm(proj3, conv_w[layer], mlstm_norm_g[layer][None, :])
        h_b = _swa(proj3, cos_t, sin_t, attn_sinks[layer])
        w_tril = jnp.where(tril, sgu_w_s[layer], 0.0).astype(BF16)
        bias_tok = jnp.repeat(sgu_b_s[layer].T, HEAD_DIM, axis=1)
        h_c = _sgu(proj3, w_tril, bias_tok, sgu_norm_g[layer][None, :], sgu_norm_b[layer][None, :])
        wo = w_out[layer]
        wa = wo[:MLSTM_W].astype(BF16)
        wb = wo[MLSTM_W:MLSTM_W + ATTN_W].reshape(ATTN_Q_HEADS, HEAD_DIM, D_MODEL)[np.array(ATTN_HEAD_ORDER)]
        wb = wb.reshape(ATTN_W, D_MODEL).astype(BF16)
        wc = wo[MLSTM_W + ATTN_W:].astype(BF16)
        x2d = _outproj_ln(h_a.reshape(n, MLSTM_W), h_b.reshape(n, ATTN_W), h_c.reshape(n, SGU_W), x2d,
                          wa, wb, wc, ln1_g[layer][None, :], ln1_b[layer][None, :])
        j = layer // 2
        g2, b2 = ln2_g[layer][None, :], ln2_b[layer][None, :]
        if layer % 2 == 0:
            x2d = _ffn_ln(x2d, ffn_w_gate[j].astype(BF16), ffn_w_up[j].astype(BF16),
                          ffn_w_down[j].astype(BF16), g2, b2)
        else:
            x2d = _moe_ln(x2d, moe_w_router[j], moe_b_router[j], moe_w_gate[j], moe_w_up[j],
                          moe_w_down[j], g2, b2)
    return x2d.reshape(bsz, seq, D_MODEL)
```

```python
import functools

import jax
import jax.numpy as jnp
import numpy as np
from jax import lax
from jax.experimental import pallas as pl
from jax.experimental.pallas import tpu as pltpu

F32 = jnp.float32
BF16 = jnp.bfloat16

D_MODEL = 1024
HEAD_DIM = 64
LANES = 128
MLSTM_HEADS = 6
ATTN_Q_HEADS = 6
ATTN_KV_HEADS = 2
SGU_GROUPS = 4
MLSTM_W = MLSTM_HEADS * HEAD_DIM
ATTN_W = ATTN_Q_HEADS * HEAD_DIM
ATTN_KV_W = ATTN_KV_HEADS * HEAD_DIM
SGU_W = SGU_GROUPS * HEAD_DIM
CHUNK = 128
CONV_WIDTH = 4
ROPE_DIM = HEAD_DIM // 4
ROPE_THETA = 500000.0
N_EXPERTS = 8
TOP_K = 2
DEPTH = 2
DN_ALPHA = (2.0 * DEPTH) ** 0.25
LN_EPS = 1e-5

QK_OFF, V_OFF, O_OFF, AQ_OFF = 0, 768, 1152, 1536
GATE_OFF, AK_OFF, AV_OFF, SU_OFF, SV_OFF = 1920, 2048, 2176, 2304, 2560
D_PROJ_PAD = 2816
ATTN_HEAD_ORDER = (0, 3, 1, 4, 2, 5)

VMEM_LIMIT = 56 * 1024 * 1024

SGU_CHUNKS_PER_STEP = 4
MLSTM_CHUNKS_PER_STEP = 8
SWA_BLOCKS_PER_STEP = 2
ROW_TILE = 512
FFN_ROWS = 2048
FF_TILE = 256
MOE_SUB = 512
MOE_VISIT_SUBS = 9
MOE_FF_TILE = 512


def _params(*sem):
    return pltpu.CompilerParams(dimension_semantics=sem, vmem_limit_bytes=VMEM_LIMIT)


def _lane_lo(shape):
    return lax.broadcasted_iota(jnp.int32, shape, len(shape) - 1) < HEAD_DIM


def _layer_norm_rows(z, g, b):
    mu = jnp.mean(z, axis=-1, keepdims=True)
    zc = z - mu
    var = jnp.mean(zc * zc, axis=-1, keepdims=True)
    return zc * lax.rsqrt(var + LN_EPS) * g + b


def _half_layer_norm(x, lo):
    inv = 1.0 / HEAD_DIM
    s_lo = jnp.sum(jnp.where(lo, x, 0.0), axis=-1, keepdims=True)
    s_all = jnp.sum(x, axis=-1, keepdims=True)
    mu = jnp.where(lo, s_lo, s_all - s_lo) * inv
    xc = x - mu
    sq = xc * xc
    q_lo = jnp.sum(jnp.where(lo, sq, 0.0), axis=-1, keepdims=True)
    q_all = jnp.sum(sq, axis=-1, keepdims=True)
    var = jnp.where(lo, q_lo, q_all - q_lo) * inv
    return xc * lax.rsqrt(var + LN_EPS)


def _split3(a):
    h1 = a.astype(BF16)
    r1 = a - h1.astype(F32)
    h2 = r1.astype(BF16)
    r2 = r1 - h2.astype(F32)
    return h1, h2, r2.astype(BF16)


def _split3_masked(a):
    def top(x):
        bits = lax.bitcast_convert_type(x, jnp.int32) & jnp.int32(-65536)
        return lax.bitcast_convert_type(bits, F32)

    h1 = top(a)
    r1 = a - h1
    h2 = top(r1)
    return h1.astype(BF16), h2.astype(BF16), (r1 - h2).astype(BF16)


def _dot(a, b):
    return jnp.dot(a, b, preferred_element_type=F32)


def _inproj_kernel(x_ref, w_ref, b_ref, o_ref):
    o_ref[...] = lax.dot_general(x_ref[...].astype(BF16), w_ref[...], (((1,), (1,)), ((), ())),
                                 preferred_element_type=F32) + b_ref[...]


def _inproj(x2d, w, b):
    n = x2d.shape[0]
    return pl.pallas_call(
        _inproj_kernel,
        grid=(n // ROW_TILE,),
        in_specs=[pl.BlockSpec((ROW_TILE, D_MODEL), lambda i: (i, 0)),
                  pl.BlockSpec((D_PROJ_PAD, D_MODEL), lambda i: (0, 0)),
                  pl.BlockSpec((1, D_PROJ_PAD), lambda i: (0, 0))],
        out_specs=pl.BlockSpec((ROW_TILE, D_PROJ_PAD), lambda i: (i, 0)),
        out_shape=jax.ShapeDtypeStruct((n, D_PROJ_PAD), F32),
        compiler_params=_params("parallel"),
        name="inproj",
    )(x2d, w, b)


def _mlstm_kernel(qk_ref, v_ref, o_ref, gate_ref, cw_ref, g_ref, out_ref, buf_ref, ct_ref, m_ref):
    rows = MLSTM_CHUNKS_PER_STEP * CHUNK

    @pl.when(pl.program_id(1) == 0)
    def _():
        buf_ref[0:8, :] = jnp.zeros((8, 2 * MLSTM_W), F32)
        ct_ref[...] = jnp.zeros_like(ct_ref)
        m_ref[...] = jnp.zeros_like(m_ref)

    buf_ref[8:8 + rows, :] = qk_ref[...]
    ct = [ct_ref[p] for p in range(MLSTM_HEADS // 2)]
    m = [m_ref[h:h + 1, 0:1] for h in range(MLSTM_HEADS)]
    for i in range(MLSTM_CHUNKS_PER_STEP):
        blk = pl.ds(i * CHUNK, CHUNK)
        ct, m = _mlstm_chunk(i * CHUNK, buf_ref, v_ref.at[blk], o_ref.at[blk], gate_ref.at[blk], cw_ref, g_ref,
                             out_ref.at[blk], ct, m)
    buf_ref[0:8, :] = qk_ref[rows - 8:rows, :]
    for p in range(MLSTM_HEADS // 2):
        ct_ref[p] = ct[p]
    for h in range(MLSTM_HEADS):
        m_ref[h:h + 1, :] = jnp.broadcast_to(m[h], (1, LANES))


def _mlstm_chunk(row0, buf_ref, v_ref, o_ref, gate_ref, cw_ref, g_ref, out_ref, ct_in, m_in):
    cw = cw_ref[...]
    base = row0 + 8 - (CONV_WIDTH - 1)
    acc = cw[0:1, :] * buf_ref[base:base + CHUNK, :]
    for j in range(1, CONV_WIDTH):
        acc = acc + cw[j:j + 1, :] * buf_ref[base + j:base + j + CHUNK, :]
    qk = jax.nn.silu(acc)

    gates = gate_ref[...]
    lf = jax.nn.log_sigmoid(gates)
    row = lax.broadcasted_iota(jnp.int32, (CHUNK, CHUNK), 0)
    col = lax.broadcasted_iota(jnp.int32, (CHUNK, CHUNK), 1)
    causal = col <= row
    tri = causal.astype(BF16)
    l1, l2, l3 = _split3(lf)
    cum = _dot(tri, l1) + _dot(tri, l2) + _dot(tri, l3)
    gates_t = gates.T
    cum_t = cum.T
    cum_parts = _split3_masked(cum)

    lo = _lane_lo((CHUNK, LANES))
    ct_out = []
    m_out = []
    for p in range(MLSTM_HEADS // 2):
        sl = slice(p * LANES, (p + 1) * LANES)
        q_slab = qk[:, sl] * (HEAD_DIM ** -0.5)
        k_slab = qk[:, MLSTM_W + p * LANES:MLSTM_W + (p + 1) * LANES]
        kt_slab = k_slab.T
        v_slab = v_ref[:, sl]
        ct_pair = ct_in[p]
        ct_pair_b = ct_pair.astype(BF16)
        halves = []
        new_ct = []
        for half in range(2):
            h = 2 * p + half
            sel = lo if half == 0 else jnp.logical_not(lo)
            li_row = gates_t[h:h + 1, :]
            bc_row = cum_t[MLSTM_HEADS + h:MLSTM_HEADS + h + 1, :]
            pick = (row == MLSTM_HEADS + h).astype(BF16)
            bc_col = _dot(cum_parts[0], pick) + _dot(cum_parts[1], pick) + _dot(cum_parts[2], pick)
            b_tot = bc_row[:, CHUNK - 1:CHUNK]
            m_prev = m_in[h]

            dmat = jnp.where(causal, bc_col - bc_row + li_row, -jnp.inf)
            m_inter = bc_col + m_prev
            m_row = jnp.maximum(m_inter, jnp.max(dmat, axis=-1, keepdims=True))
            q_m = jnp.where(sel, q_slab, 0.0).astype(BF16)
            s = _dot(q_m, kt_slab.astype(BF16))
            pmat = (s * jnp.exp(dmat - m_row)).astype(BF16)
            w_inter = jnp.exp(m_inter - m_row)
            v_aug = jnp.where(sel, v_slab, 1.0).astype(BF16)
            halves.append((_dot(pmat, v_aug) + w_inter * _dot(q_m, ct_pair_b), jnp.exp(-m_row)))

            a_row = b_tot - bc_row + li_row
            m_loc = jnp.max(a_row, axis=-1, keepdims=True)
            w_row = jnp.exp(a_row - m_loc)
            kt_h = kt_slab[half * HEAD_DIM:(half + 1) * HEAD_DIM, :]
            ct_loc = _dot((kt_h * w_row).astype(BF16), v_aug)
            m_new = jnp.maximum(b_tot + m_prev, m_loc)
            s_old = jnp.exp(b_tot + m_prev - m_new)
            s_loc = jnp.exp(m_loc - m_new)
            new_ct.append(s_old * ct_pair[half * HEAD_DIM:(half + 1) * HEAD_DIM, :] + s_loc * ct_loc)
            m_out.append(m_new)

        ct_out.append(jnp.concatenate(new_ct, axis=0))
        (r_even, stab_even), (r_odd, stab_odd) = halves
        num = jnp.where(lo, r_even, r_odd)
        den = pltpu.roll(jnp.where(lo, r_odd, r_even), HEAD_DIM, 1)
        hh = num / jnp.maximum(jnp.abs(den), jnp.where(lo, stab_even, stab_odd))
        hn = _half_layer_norm(hh, lo) * g_ref[:, sl]
        out_ref[:, sl] = (hn * jax.nn.sigmoid(o_ref[:, sl])).astype(out_ref.dtype)
    return ct_out, m_out


def _mlstm(proj3, conv_w, norm_g):
    b, s, _ = proj3.shape
    rows = MLSTM_CHUNKS_PER_STEP * CHUNK
    assert s % rows == 0
    blk = lambda w, off: pl.BlockSpec((None, rows, w), lambda i, c: (i, c, off // w))
    return pl.pallas_call(
        _mlstm_kernel,
        grid=(b, s // rows),
        in_specs=[blk(2 * MLSTM_W, QK_OFF), blk(MLSTM_W, V_OFF), blk(MLSTM_W, O_OFF), blk(LANES, GATE_OFF),
                  pl.BlockSpec((CONV_WIDTH, 2 * MLSTM_W), lambda i, c: (0, 0)),
                  pl.BlockSpec((1, MLSTM_W), lambda i, c: (0, 0))],
        out_specs=pl.BlockSpec((None, rows, MLSTM_W), lambda i, c: (i, c, 0)),
        out_shape=jax.ShapeDtypeStruct((b, s, MLSTM_W), BF16),
        scratch_shapes=[pltpu.VMEM((8 + rows, 2 * MLSTM_W), F32),
                        pltpu.VMEM((MLSTM_HEADS // 2, LANES, LANES), F32),
                        pltpu.VMEM((8, LANES), F32)],
        compiler_params=_params("parallel", "arbitrary"),
        name="mlstm",
    )(proj3, proj3, proj3, proj3, conv_w, norm_g)


def _rope(x, cos_t, sin_t, first):
    return x * cos_t + jnp.where(first, pltpu.roll(x, LANES - ROPE_DIM // 2, 1),
                                 pltpu.roll(x, ROPE_DIM // 2, 1)) * sin_t


def _swa_kernel(sink_ref, q_ref, k_ref, v_ref, cos_ref, sin_ref, out_ref, kt_ref, vv_ref):
    first_step = pl.program_id(1) == 0

    @pl.when(first_step)
    def _():
        kt_ref[...] = jnp.zeros_like(kt_ref)
        vv_ref[...] = jnp.zeros_like(vv_ref)

    kt_prev = kt_ref[...]
    v_prev = vv_ref[...]
    for i in range(SWA_BLOCKS_PER_STEP):
        rows = pl.ds(i * CHUNK, CHUNK)
        hide_previous = jnp.where(first_step, 2 * CHUNK, 0) if i == 0 else 0
        kt_prev, v_prev = _swa_block(hide_previous, sink_ref, q_ref.at[rows], k_ref.at[rows], v_ref.at[rows],
                                     cos_ref.at[rows], sin_ref.at[rows], out_ref.at[rows], kt_prev, v_prev)
    kt_ref[...] = kt_prev
    vv_ref[...] = v_prev


def _swa_block(first_block_shift, sink_ref, q_ref, k_ref, v_ref, cos_ref, sin_ref, out_ref, kt_prev, v_prev):
    cos_t = cos_ref[...]
    sin_t = sin_ref[...]
    lane = lax.broadcasted_iota(jnp.int32, (CHUNK, LANES), 1)
    lo = lane < HEAD_DIM
    first = (lane % HEAD_DIM) < ROPE_DIM // 2

    kt_cur = _rope(k_ref[...], cos_t, sin_t, first).T.astype(BF16)
    v_cur = v_ref[...].astype(BF16)
    kt = jnp.concatenate([kt_prev, kt_cur], axis=1)
    vv = jnp.concatenate([v_prev, v_cur], axis=0)

    row = lax.broadcasted_iota(jnp.int32, (CHUNK, 2 * CHUNK), 0)
    col = lax.broadcasted_iota(jnp.int32, (CHUNK, 2 * CHUNK), 1)
    visible = jnp.logical_or(jnp.logical_and(col < CHUNK, col > row + first_block_shift),
                             jnp.logical_and(col >= CHUNK, col - CHUNK <= row))

    for j in range(ATTN_Q_HEADS // 2):
        sl = slice(j * LANES, (j + 1) * LANES)
        q_slab = _rope(q_ref[:, sl], cos_t, sin_t, first) * (HEAD_DIM ** -0.5)
        outs = []
        for half in range(2):
            sel = lo if half == 0 else jnp.logical_not(lo)
            sink = sink_ref[ATTN_HEAD_ORDER[2 * j + half]]
            q_m = jnp.where(sel, q_slab, 0.0).astype(BF16)
            s = jnp.where(visible, _dot(q_m, kt), -jnp.inf)
            mx = jnp.maximum(jnp.max(s, axis=-1, keepdims=True), sink)
            pexp = jnp.exp(s - mx)
            denom = jnp.sum(pexp, axis=-1, keepdims=True) + jnp.exp(sink - mx)
            outs.append(_dot(pexp.astype(BF16), vv) / denom)
        out_ref[:, sl] = jnp.where(lo, outs[0], outs[1]).astype(out_ref.dtype)

    return kt_cur, v_cur


def _swa(proj3, cos_t, sin_t, sinks):
    b, s, _ = proj3.shape
    rows = SWA_BLOCKS_PER_STEP * CHUNK
    assert s % rows == 0
    blk = lambda w, off: pl.BlockSpec((None, rows, w), lambda i, c: (i, c, off // w))
    tab =pl.BlockSpec((None, rows, LANES), lambda i, c: (i, c, 0))
    return pl.pallas_call(
        _swa_kernel,
        grid=(b, s // rows),
        in_specs=[pl.BlockSpec(memory_space=pltpu.SMEM),
                  blk(ATTN_W, AQ_OFF), blk(ATTN_KV_W, AK_OFF), blk(ATTN_KV_W, AV_OFF), tab, tab],
        out_specs=pl.BlockSpec((None, rows, ATTN_W), lambda i, c: (i, c, 0)),
        out_shape=jax.ShapeDtypeStruct((b, s, ATTN_W), BF16),
        scratch_shapes=[pltpu.VMEM((LANES, CHUNK), BF16), pltpu.VMEM((CHUNK, LANES), BF16)],
        compiler_params=_params("parallel", "arbitrary"),
        name="swa",
    )(sinks, proj3, proj3, proj3, cos_t, sin_t)


def _sgu_kernel(u_ref, v_ref, w_ref, bias_ref, g_ref, b_ref, out_ref):
    lo = _lane_lo((CHUNK, LANES))
    for c in range(SGU_CHUNKS_PER_STEP):
        rows = slice(c * CHUNK, (c + 1) * CHUNK)
        for j in range(SGU_GROUPS // 2):
            sl = slice(j * LANES, (j + 1) * LANES)
            u = jax.nn.gelu(u_ref[rows, sl])
            v = jax.nn.gelu(v_ref[rows, sl])
            vn = (_half_layer_norm(v, lo) * g_ref[:, sl] + b_ref[:, sl]).astype(BF16)
            mixed = jnp.where(lo, _dot(w_ref[2 * j], vn), _dot(w_ref[2 * j + 1], vn)) + bias_ref[:, sl]
            out_ref[rows, sl] = (u * mixed).astype(out_ref.dtype)


def _sgu(proj3, w_tril, bias_tok, norm_g, norm_b):
    b, s, _ = proj3.shape
    rows = SGU_CHUNKS_PER_STEP * CHUNK
    assert s % rows == 0
    blk = lambda w, off: pl.BlockSpec((None, rows, w), lambda i, c: (i, c, off // w))
    const2 = lambda shape: pl.BlockSpec(shape, lambda i, c: (0,) * len(shape))
    return pl.pallas_call(
        _sgu_kernel,
        grid=(b, s // rows),
        in_specs=[blk(SGU_W, SU_OFF), blk(SGU_W, SV_OFF), const2((SGU_GROUPS, CHUNK, CHUNK)),
                  const2((CHUNK, SGU_W)), const2((1, SGU_W)), const2((1, SGU_W))],
        out_specs=pl.BlockSpec((None, rows, SGU_W), lambda i, c: (i, c, 0)),
        out_shape=jax.ShapeDtypeStruct((b, s, SGU_W), BF16),
        compiler_params=_params("parallel", "parallel"),
        name="sgu",
    )(proj3, proj3, w_tril, bias_tok, norm_g, norm_b)


def _outproj_kernel(ha_ref, hb_ref, hc_ref, x_ref, wa_ref, wb_ref, wc_ref, g_ref, b_ref, *rest):
    mix = _dot(ha_ref[...], wa_ref[...]) + _dot(hb_ref[...], wb_ref[...]) + _dot(hc_ref[...], wc_ref[...])
    y = _layer_norm_rows(DN_ALPHA * x_ref[...] + mix, g_ref[...], b_ref[...])
    if len(rest) == 1:
        rest[0][...] = y
        return
    wh_ref, wl_ref, rb_ref, o_ref, route_ref, route_t_ref = rest
    o_ref[...] = y
    route = _route_rows(y, wh_ref[...], wl_ref[...], rb_ref[...])
    route_ref[...] = route
    route_t_ref[...] = route.T[0:8, :]


def _outproj_ln(ha, hb, hc, x2d, wa, wb, wc, g, b, router=None):
    n = x2d.shape[0]
    rows = lambda w: pl.BlockSpec((ROW_TILE, w), lambda i: (i, 0))
    const = lambda shape: pl.BlockSpec(shape, lambda i: (0, 0))
    in_specs = [rows(MLSTM_W), rows(ATTN_W), rows(SGU_W), rows(D_MODEL),
                const((MLSTM_W, D_MODEL)), const((ATTN_W, D_MODEL)), const((SGU_W, D_MODEL)),
                const((1, D_MODEL)), const((1, D_MODEL))]
    out_specs = [rows(D_MODEL)]
    out_shape = [jax.ShapeDtypeStruct((n, D_MODEL), F32)]
    operands = (ha, hb, hc, x2d, wa, wb, wc, g, b)
    if router is not None:
        in_specs += [const((D_MODEL, LANES)), const((D_MODEL, LANES)), const((1, LANES))]
        out_specs += [rows(LANES), pl.BlockSpec((8, ROW_TILE), lambda i: (0, i))]
        out_shape += [jax.ShapeDtypeStruct((n, LANES), F32), jax.ShapeDtypeStruct((8, n), F32)]
        operands += tuple(router)
    out = pl.pallas_call(
        _outproj_kernel,
        grid=(n // ROW_TILE,),
        in_specs=in_specs,
        out_specs=out_specs,
        out_shape=out_shape,
        compiler_params=_params("parallel"),
        name="outproj_ln" if router is None else "outproj_ln_route",
    )(*operands)
    return out[0] if router is None else tuple(out)


def _ffn_kernel(x_ref, wg_ref, wu_ref, wd_ref, g_ref, b_ref, o_ref, xb_ref, wgu_ref):
    j = pl.program_id(1)
    last = pl.num_programs(1) - 1
    n_sub = FFN_ROWS // ROW_TILE
    wgu_ref[:, :FF_TILE] = wg_ref[...]
    wgu_ref[:, FF_TILE:] = wu_ref[...]

    def sub_rows(m):
        return pl.ds(pl.multiple_of(m * ROW_TILE, ROW_TILE), ROW_TILE)

    def ffn_tile(m):
        gu = _dot(xb_ref[sub_rows(m), :], wgu_ref[...])
        hidden = (jax.nn.silu(gu[:, :FF_TILE]) * gu[:, FF_TILE:]).astype(BF16)
        return _dot(hidden, wd_ref[...])

    @pl.when(j == 0)
    def _():
        def body(m, carry):
            xb_ref[sub_rows(m), :] = x_ref[sub_rows(m), :].astype(BF16)
            o_ref[sub_rows(m), :] = ffn_tile(m)
            return carry

        lax.fori_loop(0, n_sub, body, 0)

    @pl.when(jnp.logical_and(j > 0, j < last))
    def _():
        def body(m, carry):
            o_ref[sub_rows(m), :] += ffn_tile(m)
            return carry

        lax.fori_loop(0, n_sub, body, 0)

    @pl.when(j == last)
    def _():
        def body(m, carry):
            rows = sub_rows(m)
            ff = o_ref[rows, :] + ffn_tile(m)
            o_ref[rows, :] = _layer_norm_rows(DN_ALPHA * x_ref[rows, :] + ff, g_ref[...], b_ref[...])
            return carry

        lax.fori_loop(0, n_sub, body, 0)


def _ffn_ln(x2d, wg, wu, wd, g, b):
    n = x2d.shape[0]
    d_ff = wd.shape[0]
    assert d_ff // FF_TILE >= 2
    tm = FFN_ROWS
    return pl.pallas_call(
        _ffn_kernel,
        grid=(n // tm, d_ff // FF_TILE),
        in_specs=[pl.BlockSpec((tm, D_MODEL), lambda i, j: (i, 0)),
                  pl.BlockSpec((D_MODEL, FF_TILE), lambda i, j: (0, j)),
                  pl.BlockSpec((D_MODEL, FF_TILE), lambda i, j: (0, j)),
                  pl.BlockSpec((FF_TILE, D_MODEL), lambda i, j: (j, 0)),
                  pl.BlockSpec((1, D_MODEL), lambda i, j: (0, 0)),
                  pl.BlockSpec((1, D_MODEL), lambda i, j: (0, 0))],
        out_specs=pl.BlockSpec((tm, D_MODEL), lambda i, j: (i, 0)),
        out_shape=jax.ShapeDtypeStruct((n, D_MODEL), F32),
        scratch_shapes=[pltpu.VMEM((tm, D_MODEL), BF16), pltpu.VMEM((D_MODEL, 2 * FF_TILE), BF16)],
        compiler_params=_params("parallel", "arbitrary"),
        name="ffn_ln",
    )(x2d, wg, wu, wd, g, b)


def _route_rows(x, w_hi, w_lo, bias):
    xh = x.astype(BF16)
    xl = (x - xh.astype(F32)).astype(BF16)
    logits = _dot(xh, w_hi) + _dot(xl, w_hi) + _dot(xh, w_lo) + bias
    lane = lax.broadcasted_iota(jnp.int32, logits.shape, 1)
    logits = jnp.where(lane < N_EXPERTS, logits, -jnp.inf)
    m1 = jnp.max(logits, axis=-1, keepdims=True)
    i1 = jnp.min(jnp.where(logits == m1, lane, LANES), axis=-1, keepdims=True)
    rest = jnp.where(lane == i1, -jnp.inf, logits)
    m2 = jnp.max(rest, axis=-1, keepdims=True)
    i2 = jnp.min(jnp.where(rest == m2, lane, LANES), axis=-1, keepdims=True)
    e2 = jnp.exp(m2 - m1)
    g1 = 1.0 / (1.0 + e2)
    g2 = e2 / (1.0 + e2)
    return jnp.where(lane == 0, g1, jnp.where(lane == 1, g2, jnp.where(
        lane == 2, i1.astype(F32), jnp.where(lane == 3, i2.astype(F32), 0.0))))


def _dispatch_kernel(slot0_ref, slot1_ref, ends_ref, x_ref, o_hbm, stage_ref, zero_ref, sem, zero_sem):
    t = pl.program_id(0)
    last = pl.num_programs(0) - 1
    s = t % 2

    def wait_tile(ss):
        for _ in range(TOP_K):
            pltpu.make_async_copy(zero_ref, o_hbm.at[pl.ds(0, MOE_SUB), :], sem.at[ss]).wait()

    @pl.when(t == 0)
    def _():
        zero_ref[...] = jnp.zeros_like(zero_ref)

        def zero_sub_tile(m, carry):
            dst = o_hbm.at[pl.ds(pl.multiple_of(m * MOE_SUB, MOE_SUB), MOE_SUB), :]
            cp = pltpu.make_async_copy(zero_ref, dst, zero_sem)
            cp.start()
            cp.wait()
            return carry

        for e in range(N_EXPERTS):
            @pl.when(ends_ref[e] >= MOE_SUB)
            def _():
                zero_sub_tile(ends_ref[e] // MOE_SUB - 1, 0)

        lax.fori_loop(ends_ref[N_EXPERTS - 1] // MOE_SUB, o_hbm.shape[0] // MOE_SUB, zero_sub_tile, 0)

    @pl.when(t >= 2)
    def _():
        wait_tile(s)

    stage_ref[s] = x_ref[...].reshape(MOE_SUB // 8, 8, D_MODEL)

    def issue(i, carry):
        for u in range(8):
            for slot_ref in (slot0_ref, slot1_ref):
                pltpu.make_async_copy(stage_ref.at[s, i, pl.ds(u, 1), :],
                                      o_hbm.at[pl.ds(slot_ref[t * MOE_SUB + i * 8 + u], 1), :], sem.at[s]).start()
        return carry

    lax.fori_loop(0, MOE_SUB // 8, issue, 0)

    @pl.when(t == last)
    def _():
        wait_tile(s)
        wait_tile(1 - s)


def _dispatch_rows(x2d, slots, ends, n_rows):
    n = x2d.shape[0]
    assert n // MOE_SUB >= 2
    return pl.pallas_call(
        _dispatch_kernel,
        grid_spec=pltpu.PrefetchScalarGridSpec(
            num_scalar_prefetch=3,
            grid=(n // MOE_SUB,),
            in_specs=[pl.BlockSpec((MOE_SUB, D_MODEL), lambda t, s0, s1, en: (t, 0))],
            out_specs=pl.BlockSpec(memory_space=pl.ANY),
            scratch_shapes=[pltpu.VMEM((2, MOE_SUB // 8, 8, D_MODEL), F32), pltpu.VMEM((MOE_SUB, D_MODEL), F32),
                            pltpu.SemaphoreType.DMA((2,)), pltpu.SemaphoreType.DMA(())]),
        out_shape=jax.ShapeDtypeStruct((n_rows, D_MODEL), F32),
        compiler_params=_params("arbitrary"),
        name="moe_dispatch",
    )(slots[0], slots[1], ends, x2d)


def _moe_kernel(exp_ref, row0_ref, nsub_ref, tail_ref, x_hbm, wg_ref, wu_ref, wd_ref, y_hbm,
                xb_ref, acc_ref, wgu_ref, wdb_ref, stage_ref, in_sem, out_sem):
    v = pl.program_id(0)
    j = pl.program_id(1)
    last = pl.num_programs(1) - 1
    n_sub = nsub_ref[v]
    row0 = row0_ref[v]

    def sub_rows(m):
        return pl.ds(pl.multiple_of(m * MOE_SUB, MOE_SUB), MOE_SUB)

    def hbm_rows(m):
        return pl.ds(pl.multiple_of(row0 + m * MOE_SUB, MOE_SUB), MOE_SUB)

    @pl.when(jnp.logical_and(v == 0, j == 0))
    def _():
        stage_ref[0] = jnp.zeros((MOE_SUB, D_MODEL), F32)

        def zero_sub_tile(m, carry):
            cp = pltpu.make_async_copy(stage_ref.at[0], y_hbm.at[sub_rows(m), :], out_sem)
            cp.start()
            cp.wait()
            return carry

        lax.fori_loop(tail_ref[0], y_hbm.shape[0] // MOE_SUB, zero_sub_tile, 0)

    @pl.when(n_sub > 0)
    def _():
        wgu_ref[:, :MOE_FF_TILE] = wg_ref[...].astype(BF16)
        wgu_ref[:, MOE_FF_TILE:] = wu_ref[...].astype(BF16)
        wdb_ref[...] = wd_ref[...].astype(BF16)

        def ffn_tile(m):
            xs = xb_ref[sub_rows(m), :]
            gu = _dot(xs, wgu_ref[...])
            hidden = (jax.nn.silu(gu[:, :MOE_FF_TILE]) * gu[:, MOE_FF_TILE:]).astype(BF16)
            return _dot(hidden, wdb_ref[...])

        def y_copy(m):
            return pltpu.make_async_copy(acc_ref.at[sub_rows(m), :], y_hbm.at[hbm_rows(m), :], out_sem)

        @pl.when(j == 0)
        def _():
            def x_copy(m):
                return pltpu.make_async_copy(x_hbm.at[hbm_rows(m), :], stage_ref.at[m % 2], in_sem.at[m % 2])

            x_copy(0).start()

            def body(m, carry):
                @pl.when(m + 1 < n_sub)
                def _():
                    x_copy(m + 1).start()

                x_copy(m).wait()
                xb_ref[sub_rows(m), :] = stage_ref[m % 2].astype(BF16)
                acc_ref[sub_rows(m), :] = ffn_tile(m)
                return carry

            lax.fori_loop(0, n_sub, body, 0)

        @pl.when(jnp.logical_and(j > 0, j < last))
        def _():
            def body(m, carry):
                acc_ref[sub_rows(m), :] += ffn_tile(m)
                return carry

            lax.fori_loop(0, n_sub, body, 0)

        @pl.when(j == last)
        def _():
            def body(m, carry):
                acc_ref[sub_rows(m), :] += ffn_tile(m)
                y_copy(m).start()
                return carry

            def drain(m, carry):
                y_copy(m).wait()
                return carry

            lax.fori_loop(0, n_sub, body, 0)
            lax.fori_loop(0, n_sub, drain, 0)


def _moe_grouped(xs, wg, wu, wd, visit_exp, visit_row0, visit_nsub, tail_sub):
    n_rows = xs.shape[0]
    n_visits = visit_exp.shape[0]
    d_ff = wg.shape[2]
    assert d_ff // MOE_FF_TILE >= 2
    rows = MOE_VISIT_SUBS * MOE_SUB
    w_in = lambda v, j, e, r, ns, tl: (e[v], 0, j)
    w_out = lambda v, j, e, r, ns, tl: (e[v], j, 0)
    return pl.pallas_call(
        _moe_kernel,
        grid_spec=pltpu.PrefetchScalarGridSpec(
            num_scalar_prefetch=4,
            grid=(n_visits, d_ff // MOE_FF_TILE),
            in_specs=[pl.BlockSpec(memory_space=pl.ANY),
                      pl.BlockSpec((None, D_MODEL, MOE_FF_TILE), w_in),
                      pl.BlockSpec((None, D_MODEL, MOE_FF_TILE), w_in),
                      pl.BlockSpec((None, MOE_FF_TILE, D_MODEL), w_out)],
            out_specs=pl.BlockSpec(memory_space=pl.ANY),
            scratch_shapes=[pltpu.VMEM((rows, D_MODEL), BF16), pltpu.VMEM((rows, D_MODEL), F32),
                            pltpu.VMEM((D_MODEL, 2 * MOE_FF_TILE), BF16),
                            pltpu.VMEM((MOE_FF_TILE, D_MODEL), BF16),
                            pltpu.VMEM((2, MOE_SUB, D_MODEL), F32),
                            pltpu.SemaphoreType.DMA((2,)), pltpu.SemaphoreType.DMA(())]),
        out_shape=jax.ShapeDtypeStruct((n_rows, D_MODEL), F32),
        compiler_params=_params("arbitrary", "arbitrary"),
        name="moe_grouped",
    )(visit_exp, visit_row0, visit_nsub, tail_sub, xs, wg, wu, wd)


def _combine_kernel(slot0_ref, slot1_ref, x_ref, gate_ref, y_hbm, y_grouped_hbm, g_ref, b_ref, o_ref, buf_ref, sem):
    t = pl.program_id(0)
    s = t % 2

    def issue_tile(tt, ss):
        def issue(i, carry):
            for u in range(8):
                for k, slot_ref in enumerate((slot0_ref, slot1_ref)):
                    pltpu.make_async_copy(y_hbm.at[pl.ds(slot_ref[tt * MOE_SUB + i * 8 + u], 1), :],
                                          buf_ref.at[ss, k, i, pl.ds(u, 1), :], sem.at[ss]).start()
            return carry

        lax.fori_loop(0, MOE_SUB // 8, issue, 0)

    @pl.when(t == 0)
    def _():
        issue_tile(0, 0)

    @pl.when(t + 1 < pl.num_programs(0))
    def _():
        issue_tile(t + 1, 1 - s)

    for k in range(TOP_K):
        pltpu.make_async_copy(y_grouped_hbm.at[pl.ds(0, MOE_SUB // 8)], buf_ref.at[s, k], sem.at[s]).wait()
    gate = gate_ref[...]
    y0 = buf_ref[s, 0].reshape(MOE_SUB, D_MODEL)
    y1 = buf_ref[s, 1].reshape(MOE_SUB, D_MODEL)
    ff = gate[:, 0:1] * y0 + gate[:, 1:2] * y1
    o_ref[...] = _layer_norm_rows(DN_ALPHA * x_ref[...] + ff, g_ref[...], b_ref[...])


def _combine_ln(x2d, route, slots, ys, g, b):
    n = x2d.shape[0]
    return pl.pallas_call(
        _combine_kernel,
        grid_spec=pltpu.PrefetchScalarGridSpec(
            num_scalar_prefetch=2,
            grid=(n // MOE_SUB,),
            in_specs=[pl.BlockSpec((MOE_SUB, D_MODEL), lambda t, s0, s1: (t, 0)),
                      pl.BlockSpec((MOE_SUB, LANES), lambda t, s0, s1: (t, 0)),
                      pl.BlockSpec(memory_space=pl.ANY), pl.BlockSpec(memory_space=pl.ANY),
                      pl.BlockSpec((1, D_MODEL), lambda t, s0, s1: (0, 0)),
                      pl.BlockSpec((1, D_MODEL), lambda t, s0, s1: (0, 0))],
            out_specs=pl.BlockSpec((MOE_SUB, D_MODEL), lambda t, s0, s1: (t, 0)),
            scratch_shapes=[pltpu.VMEM((2, TOP_K, MOE_SUB // 8, 8, D_MODEL), F32),
                            pltpu.SemaphoreType.DMA((2,))]),
        out_shape=jax.ShapeDtypeStruct((n, D_MODEL), F32),
        compiler_params=_params("arbitrary"),
        name="moe_combine_ln",
    )(slots[0], slots[1], x2d, route, ys, ys.reshape(ys.shape[0] // 8, 8, D_MODEL), g, b)


def _routing_tables(route_t, n):
    idx = route_t[2:4].astype(jnp.int32)
    expert = jnp.arange(N_EXPERTS, dtype=jnp.int32)[:, None]
    chosen = [idx[k][None, :] == expert for k in range(TOP_K)]
    onehot = jnp.logical_or(chosen[0], chosen[1]).astype(jnp.int32)
    rank = jnp.cumsum(onehot, axis=1) - onehot
    counts = jnp.sum(onehot, axis=1)
    padded = ((counts + MOE_SUB - 1) // MOE_SUB) * MOE_SUB
    ends = jnp.cumsum(padded)
    starts = ends - padded
    place = starts[:, None] + rank
    slot = [jnp.sum(jnp.where(chosen[k], place, 0), axis=0).astype(jnp.int32) for k in range(TOP_K)]

    n_rows = -(-(n * TOP_K + N_EXPERTS * (MOE_SUB - 1)) // MOE_SUB) * MOE_SUB

    visit_rows = MOE_VISIT_SUBS * MOE_SUB
    max_chunks = -(-n_rows // visit_rows)
    chunk = jnp.arange(max_chunks, dtype=jnp.int32)[None, :]
    left = padded[:, None] - chunk * visit_rows
    valid = (left > 0).reshape(-1)
    n_visits = n_rows // visit_rows + N_EXPERTS
    order = jnp.argsort(jnp.logical_not(valid), stable=True)[:n_visits]
    n_valid = jnp.sum(valid.astype(jnp.int32))
    live = jnp.arange(n_visits) < n_valid
    order = jnp.where(live, order, order[jnp.maximum(n_valid - 1, 0)])
    v_exp = (order // max_chunks).astype(jnp.int32)
    v_chunk = (order % max_chunks).astype(jnp.int32)
    v_row0 = jnp.where(live, starts[v_exp] + v_chunk * visit_rows, 0).astype(jnp.int32)
    v_nsub = jnp.where(live, jnp.minimum(left.reshape(-1)[order], visit_rows) // MOE_SUB, 0).astype(jnp.int32)
    tail_sub = (ends[N_EXPERTS - 1:] // MOE_SUB).astype(jnp.int32)
    return slot, ends.astype(jnp.int32), n_rows, (v_exp, v_row0, v_nsub, tail_sub)


def _router_operands(w_router, b_router):
    w_pad = jnp.zeros((D_MODEL, LANES), F32).at[:, :N_EXPERTS].set(w_router)
    w_hi = w_pad.astype(BF16)
    w_lo = (w_pad - w_hi.astype(F32)).astype(BF16)
    b_pad = jnp.zeros((1, LANES), F32).at[0, :N_EXPERTS].set(b_router)
    return w_hi, w_lo, b_pad


def _moe_ln(x2d, route, route_t, wg, wu, wd, g, b):
    n = x2d.shape[0]
    slots, ends, n_rows, visits = _routing_tables(route_t, n)
    xs = _dispatch_rows(x2d, slots, ends, n_rows)
    ys = _moe_grouped(xs, wg, wu, wd, *visits)
    return _combine_ln(x2d, route, slots, ys, g, b)


def _layout_in_proj(w_in, b_in):
    sizes = (MLSTM_W, MLSTM_W, MLSTM_W, MLSTM_W, MLSTM_HEADS, MLSTM_HEADS,
             ATTN_W, ATTN_KV_W, ATTN_KV_W, SGU_W, SGU_W)
    offs = np.concatenate([[0], np.cumsum(sizes)])
    seg = lambda a, i: a[int(offs[i]):int(offs[i + 1])]

    def build(a):
        tail = a.shape[1:]
        aq = seg(a, 6).reshape((ATTN_Q_HEADS, HEAD_DIM) + tail)[np.array(ATTN_HEAD_ORDER)]
        gates = jnp.concatenate([seg(a, 4), seg(a, 5), jnp.zeros((LANES - 2 * MLSTM_HEADS,) + tail, a.dtype)], 0)
        return jnp.concatenate([seg(a, 0), seg(a, 1), seg(a, 2), seg(a, 3), aq.reshape((ATTN_W,) + tail),
                                gates, seg(a, 7), seg(a, 8), seg(a, 9), seg(a, 10)], 0)

    return build(w_in.T).astype(BF16), build(b_in)[None, :]


def _rope_tables(positions):
    inv_freq = ROPE_THETA ** (-jnp.arange(0, ROPE_DIM, 2, dtype=F32) / ROPE_DIM)
    ang = inv_freq[None, :, None] * positions.astype(F32)[:, None, :]
    cos, sin = jnp.cos(ang), jnp.sin(ang)
    ones = jnp.ones((ang.shape[0], HEAD_DIM - ROPE_DIM, ang.shape[2]), F32)
    cos_head = jnp.concatenate([cos, cos, ones], 1)
    sin_head = jnp.concatenate([-sin, sin, 0.0 * ones], 1)
    cos_t = jnp.concatenate([cos_head, cos_head], 1)
    sin_t = jnp.concatenate([sin_head, sin_head], 1)
    return _to_token_major(cos_t, sin_t)


def _table_transpose_kernel(cos_ref, sin_ref, cos_out, sin_out):
    cos_out[...] = cos_ref[...].T
    sin_out[...] = sin_ref[...].T


def _to_token_major(cos_t, sin_t):
    b, lanes, s = cos_t.shape
    src = pl.BlockSpec((None, lanes, ROW_TILE), lambda i, c: (i, 0, c))
    dst = pl.BlockSpec((None, ROW_TILE, lanes), lambda i, c: (i, c, 0))
    shape = jax.ShapeDtypeStruct((b, s, lanes), F32)
    return pl.pallas_call(
        _table_transpose_kernel,
        grid=(b, s // ROW_TILE),
        in_specs=[src, src],
        out_specs=[dst, dst],
        out_shape=[shape, shape],
        compiler_params=_params("parallel", "parallel"),
        name="rope_tables",
    )(cos_t, sin_t)


def kernel(x, positions, w_in, b_in, conv_w, mlstm_norm_g, attn_sinks, sgu_w_s, sgu_b_s, sgu_norm_g, sgu_norm_b, w_out, ln1_g, ln1_b, ln2_g, ln2_b, ffn_w_gate, ffn_w_up, ffn_w_down, moe_w_router, moe_b_router, moe_w_gate, moe_w_up, moe_w_down):
    bsz, seq, _ = x.shape
    n = bsz * seq
    cos_t, sin_t = _rope_tables(positions)
    tril = jnp.tril(jnp.ones((CHUNK, CHUNK), bool))
    x2d = x.reshape(n, D_MODEL)
    for layer in range(DEPTH):
        w_p, b_p = _layout_in_proj(w_in[layer], b_in[layer])
        proj3 = _inproj(x2d, w_p, b_p).reshape(bsz, seq, D_PROJ_PAD)
        h_a = _mlstm(proj3, conv_w[layer], mlstm_norm_g[layer][None, :])
        h_b = _swa(proj3, cos_t, sin_t, attn_sinks[layer])
        w_tril = jnp.where(tril, sgu_w_s[layer], 0.0).astype(BF16)
        bias_tok = jnp.repeat(sgu_b_s[layer].T, HEAD_DIM, axis=1)
        h_c = _sgu(proj3, w_tril, bias_tok, sgu_norm_g[layer][None, :], sgu_norm_b[layer][None, :])
        wo = w_out[layer]
        wa = wo[:MLSTM_W].astype(BF16)
        wb = wo[MLSTM_W:MLSTM_W + ATTN_W].reshape(ATTN_Q_HEADS, HEAD_DIM, D_MODEL)[np.array(ATTN_HEAD_ORDER)]
        wb = wb.reshape(ATTN_W, D_MODEL).astype(BF16)
        wc = wo[MLSTM_W + ATTN_W:].astype(BF16)
        j = layer // 2
        dense = layer % 2 == 0
        mixed = _outproj_ln(h_a.reshape(n, MLSTM_W), h_b.reshape(n, ATTN_W), h_c.reshape(n, SGU_W), x2d,
                            wa, wb, wc, ln1_g[layer][None, :], ln1_b[layer][None, :],
                            router=None if dense else _router_operands(moe_w_router[j], moe_b_router[j]))
        g2, b2 = ln2_g[layer][None, :], ln2_b[layer][None, :]
        if dense:
            x2d = _ffn_ln(mixed, ffn_w_gate[j].astype(BF16), ffn_w_up[j].astype(BF16),
                          ffn_w_down[j].astype(BF16), g2, b2)
        else:
            x2d, route, route_t = mixed
            x2d = _moe_ln(x2d, route, route_t, moe_w_gate[j], moe_w_up[j], moe_w_down[j], g2, b2)
    return x2d.reshape(bsz, seq, D_MODEL)
```

```python
import jax
import jax.numpy as jnp
import numpy as np
from jax import lax
from jax.experimental import pallas as pl
from jax.experimental.pallas import tpu as pltpu

F32 = jnp.float32
BF16 = jnp.bfloat16

D_MODEL = 1024
HEAD_DIM = 64
LANES = 128
SUBLANES = 8
MLSTM_HEADS = 6
ATTN_Q_HEADS = 6
ATTN_KV_HEADS = 2
SGU_GROUPS = 4
MLSTM_W = MLSTM_HEADS * HEAD_DIM
ATTN_W = ATTN_Q_HEADS * HEAD_DIM
ATTN_KV_W = ATTN_KV_HEADS * HEAD_DIM
SGU_W = SGU_GROUPS * HEAD_DIM
CHUNK = 128
CONV_WIDTH = 4
ROPE_DIM = HEAD_DIM // 4
ROPE_THETA = 500000.0
N_EXPERTS = 8
TOP_K = 2
DEPTH = 2
DN_ALPHA = (2.0 * DEPTH) ** 0.25
LN_EPS = 1e-5

QK_OFF, V_OFF, O_OFF, AQ_OFF = 0, 768, 1152, 1536
GATE_OFF, AK_OFF, AV_OFF, SU_OFF, SV_OFF = 1920, 2048, 2176, 2304, 2560
D_PROJ_PAD = 2816
ATTN_HEAD_ORDER = (0, 3, 1, 4, 2, 5)

VMEM_LIMIT = 56 * 1024 * 1024

SGU_CHUNKS_PER_STEP = 4
MLSTM_CHUNKS_PER_STEP = 8
SWA_BLOCKS_PER_STEP = 2
ROW_TILE = 512
FFN_ROWS = 2048
FF_TILE = 256
MOE_SUB = 512
MOE_VISIT_SUBS = 9
MOE_FF_TILE = 512


def _params(*sem):
    return pltpu.CompilerParams(dimension_semantics=sem, vmem_limit_bytes=VMEM_LIMIT)


def _lane_lo(shape):
    return lax.broadcasted_iota(jnp.int32, shape, len(shape) - 1) < HEAD_DIM


def _layer_norm_rows(z, g, b):
    mu = jnp.mean(z, axis=-1, keepdims=True)
    zc = z - mu
    var = jnp.mean(zc * zc, axis=-1, keepdims=True)
    return zc * lax.rsqrt(var + LN_EPS) * g + b


def _half_layer_norm(x, lo):
    inv = 1.0 / HEAD_DIM
    s_lo = jnp.sum(jnp.where(lo, x, 0.0), axis=-1, keepdims=True)
    s_all = jnp.sum(x, axis=-1, keepdims=True)
    mu = jnp.where(lo, s_lo, s_all - s_lo) * inv
    xc = x - mu
    sq = xc * xc
    q_lo = jnp.sum(jnp.where(lo, sq, 0.0), axis=-1, keepdims=True)
    q_all = jnp.sum(sq, axis=-1, keepdims=True)
    var = jnp.where(lo, q_lo, q_all - q_lo) * inv
    return xc * lax.rsqrt(var + LN_EPS)


def _split3(a):
    h1 = a.astype(BF16)
    r1 = a - h1.astype(F32)
    h2 = r1.astype(BF16)
    r2 = r1 - h2.astype(F32)
    return h1, h2, r2.astype(BF16)


def _split3_masked(a):
    def top(x):
        bits = lax.bitcast_convert_type(x, jnp.int32) & jnp.int32(-65536)
        return lax.bitcast_convert_type(bits, F32)

    h1 = top(a)
    r1 = a - h1
    h2 = top(r1)
    return h1.astype(BF16), h2.astype(BF16), (r1 - h2).astype(BF16)


def _dot(a, b):
    return jnp.dot(a, b, preferred_element_type=F32)


def _inproj_kernel(x_ref, w_ref, b_ref, o_ref):
    o_ref[...] = _dot(x_ref[...].astype(BF16), w_ref[...]) + b_ref[...]


def _inproj(x2d, w, b):
    n = x2d.shape[0]
    return pl.pallas_call(
        _inproj_kernel,
        grid=(n // ROW_TILE,),
        in_specs=[pl.BlockSpec((ROW_TILE, D_MODEL), lambda i: (i, 0)),
                  pl.BlockSpec((D_MODEL, D_PROJ_PAD), lambda i: (0, 0)),
                  pl.BlockSpec((1, D_PROJ_PAD), lambda i: (0, 0))],
        out_specs=pl.BlockSpec((ROW_TILE, D_PROJ_PAD), lambda i: (i, 0)),
        out_shape=jax.ShapeDtypeStruct((n, D_PROJ_PAD), F32),
        compiler_params=_params("parallel"),
        name="inproj",
    )(x2d, w, b)


def _mlstm_kernel(qk_ref, v_ref, o_ref, gate_ref, cw_ref, g_ref, out_ref, buf_ref, ct_ref, m_ref):
    rows = MLSTM_CHUNKS_PER_STEP * CHUNK

    @pl.when(pl.program_id(1) == 0)
    def _():
        buf_ref[0:8, :] = jnp.zeros((8, 2 * MLSTM_W), F32)
        ct_ref[...] = jnp.zeros_like(ct_ref)
        m_ref[...] = jnp.zeros_like(m_ref)

    buf_ref[8:8 + rows, :] = qk_ref[...]
    ct = [ct_ref[p] for p in range(MLSTM_HEADS // 2)]
    m = [m_ref[h:h + 1, 0:1] for h in range(MLSTM_HEADS)]
    for i in range(MLSTM_CHUNKS_PER_STEP):
        blk = pl.ds(i * CHUNK, CHUNK)
        ct, m = _mlstm_chunk(i * CHUNK, buf_ref, v_ref.at[blk], o_ref.at[blk], gate_ref.at[blk], cw_ref, g_ref,
                             out_ref.at[blk], ct, m)
    buf_ref[0:8, :] = qk_ref[rows - 8:rows, :]
    for p in range(MLSTM_HEADS // 2):
        ct_ref[p] = ct[p]
    for h in range(MLSTM_HEADS):
        m_ref[h:h + 1, :] = jnp.broadcast_to(m[h], (1, LANES))


def _mlstm_chunk(row0, buf_ref, v_ref, o_ref, gate_ref, cw_ref, g_ref, out_ref, ct_in, m_in):
    cw = cw_ref[...]
    base = row0 + 8 - (CONV_WIDTH - 1)
    acc = cw[0:1, :] * buf_ref[base:base + CHUNK, :]
    for j in range(1, CONV_WIDTH):
        acc = acc + cw[j:j + 1, :] * buf_ref[base + j:base + j + CHUNK, :]
    qk = jax.nn.silu(acc)

    gates = gate_ref[...]
    lf = jax.nn.log_sigmoid(gates)
    row = lax.broadcasted_iota(jnp.int32, (CHUNK, CHUNK), 0)
    col = lax.broadcasted_iota(jnp.int32, (CHUNK, CHUNK), 1)
    causal = col <= row
    tri = causal.astype(BF16)
    l1, l2, l3 = _split3(lf)
    cum = _dot(tri, l1) + _dot(tri, l2) + _dot(tri, l3)
    gates_t = gates.T
    cum_t = cum.T
    cum_parts = _split3_masked(cum)

    lo = _lane_lo((CHUNK, LANES))
    ct_out = []
    m_out = []
    for p in range(MLSTM_HEADS // 2):
        sl = slice(p * LANES, (p + 1) * LANES)
        q_slab = qk[:, sl] * (HEAD_DIM ** -0.5)
        k_slab = qk[:, MLSTM_W + p * LANES:MLSTM_W + (p + 1) * LANES]
        kt_slab = k_slab.T
        v_slab = v_ref[:, sl]
        ct_pair = ct_in[p]
        ct_pair_b = ct_pair.astype(BF16)
        halves = []
        new_ct = []
        for half in range(2):
            h = 2 * p + half
            sel = lo if half == 0 else jnp.logical_not(lo)
            li_row = gates_t[h:h + 1, :]
            bc_row = cum_t[MLSTM_HEADS + h:MLSTM_HEADS + h + 1, :]
            pick = (row == MLSTM_HEADS + h).astype(BF16)
            bc_col = _dot(cum_parts[0], pick) + _dot(cum_parts[1], pick) + _dot(cum_parts[2], pick)
            b_tot = bc_row[:, CHUNK - 1:CHUNK]
            m_prev = m_in[h]

            dmat = jnp.where(causal, bc_col - bc_row + li_row, -jnp.inf)
            m_inter = bc_col + m_prev
            m_row = jnp.maximum(m_inter, jnp.max(dmat, axis=-1, keepdims=True))
            q_m = jnp.where(sel, q_slab, 0.0).astype(BF16)
            s = _dot(q_m, kt_slab.astype(BF16))
            pmat = (s * jnp.exp(dmat - m_row)).astype(BF16)
            w_inter = jnp.exp(m_inter - m_row)
            v_aug = jnp.where(sel, v_slab, 1.0).astype(BF16)
            halves.append((_dot(pmat, v_aug) + w_inter * _dot(q_m, ct_pair_b), jnp.exp(-m_row)))

            a_row = b_tot - bc_row + li_row
            m_loc = jnp.max(a_row, axis=-1, keepdims=True)
            w_row = jnp.exp(a_row - m_loc)
            kt_h = kt_slab[half * HEAD_DIM:(half + 1) * HEAD_DIM, :]
            ct_loc = _dot((kt_h * w_row).astype(BF16), v_aug)
            m_new = jnp.maximum(b_tot + m_prev, m_loc)
            s_old = jnp.exp(b_tot + m_prev - m_new)
            s_loc = jnp.exp(m_loc - m_new)
            new_ct.append(s_old * ct_pair[half * HEAD_DIM:(half + 1) * HEAD_DIM, :] + s_loc * ct_loc)
            m_out.append(m_new)

        ct_out.append(jnp.concatenate(new_ct, axis=0))
        (r_even, stab_even), (r_odd, stab_odd) = halves
        num = jnp.where(lo, r_even, r_odd)
        den = pltpu.roll(jnp.where(lo, r_odd, r_even), HEAD_DIM, 1)
        hh = num / jnp.maximum(jnp.abs(den), jnp.where(lo, stab_even, stab_odd))
        hn = _half_layer_norm(hh, lo) * g_ref[:, sl]
        out_ref[:, sl] = (hn * jax.nn.sigmoid(o_ref[:, sl])).astype(out_ref.dtype)
    return ct_out, m_out


def _mlstm(proj3, conv_w, norm_g):
    b, s, _ = proj3.shape
    rows = MLSTM_CHUNKS_PER_STEP * CHUNK
    assert s % rows == 0
    blk = lambda w, off: pl.BlockSpec((None, rows, w), lambda i, c: (i, c, off // w))
    return pl.pallas_call(
        _mlstm_kernel,
        grid=(b, s // rows),
        in_specs=[blk(2 * MLSTM_W, QK_OFF), blk(MLSTM_W, V_OFF), blk(MLSTM_W, O_OFF), blk(LANES, GATE_OFF),
                  pl.BlockSpec((CONV_WIDTH, 2 * MLSTM_W), lambda i, c: (0, 0)),
                  pl.BlockSpec((1, MLSTM_W), lambda i, c: (0, 0))],
        out_specs=pl.BlockSpec((None, rows, MLSTM_W), lambda i, c: (i, c, 0)),
        out_shape=jax.ShapeDtypeStruct((b, s, MLSTM_W), BF16),
        scratch_shapes=[pltpu.VMEM((8 + rows, 2 * MLSTM_W), F32),
                        pltpu.VMEM((MLSTM_HEADS // 2, LANES, LANES), F32),
                        pltpu.VMEM((8, LANES), F32)],
        compiler_params=_params("parallel", "arbitrary"),
        name="mlstm",
    )(proj3, proj3, proj3, proj3, conv_w, norm_g)


def _rope(x, cos_t, sin_t, first):
    return x * cos_t + jnp.where(first, pltpu.roll(x, LANES - ROPE_DIM // 2, 1),
                                 pltpu.roll(x, ROPE_DIM // 2, 1)) * sin_t


def _swa_kernel(sink_ref, q_ref, k_ref, v_ref, cos_ref, sin_ref, out_ref, kt_ref, vv_ref):
    first_step = pl.program_id(1) == 0

    @pl.when(first_step)
    def _():
        kt_ref[...] = jnp.zeros_like(kt_ref)
        vv_ref[...] = jnp.zeros_like(vv_ref)

    kt_prev = kt_ref[...]
    v_prev = vv_ref[...]
    for i in range(SWA_BLOCKS_PER_STEP):
        rows = pl.ds(i * CHUNK, CHUNK)
        hide_previous = jnp.where(first_step, 2 * CHUNK, 0) if i == 0 else 0
        kt_prev, v_prev = _swa_block(hide_previous, sink_ref, q_ref.at[rows], k_ref.at[rows], v_ref.at[rows],
                                     cos_ref.at[rows], sin_ref.at[rows], out_ref.at[rows], kt_prev, v_prev)
    kt_ref[...] = kt_prev
    vv_ref[...] = v_prev


def _swa_block(first_block_shift, sink_ref, q_ref, k_ref, v_ref, cos_ref, sin_ref, out_ref, kt_prev, v_prev):
    cos_t = cos_ref[...]
    sin_t = sin_ref[...]
    lane = lax.broadcasted_iota(jnp.int32, (CHUNK, LANES), 1)
    lo = lane < HEAD_DIM
    first = (lane % HEAD_DIM) < ROPE_DIM // 2

    kt_cur = _rope(k_ref[...], cos_t, sin_t, first).T.astype(BF16)
    v_cur = v_ref[...].astype(BF16)
    kt = jnp.concatenate([kt_prev, kt_cur], axis=1)
    vv = jnp.concatenate([v_prev, v_cur], axis=0)

    row = lax.broadcasted_iota(jnp.int32, (CHUNK, 2 * CHUNK), 0)
    col = lax.broadcasted_iota(jnp.int32, (CHUNK, 2 * CHUNK), 1)
    visible = jnp.logical_or(jnp.logical_and(col < CHUNK, col > row + first_block_shift),
                             jnp.logical_and(col >= CHUNK, col - CHUNK <= row))

    for j in range(ATTN_Q_HEADS // 2):
        sl = slice(j * LANES, (j + 1) * LANES)
        q_slab = _rope(q_ref[:, sl], cos_t, sin_t, first) * (HEAD_DIM ** -0.5)
        outs = []
        for half in range(2):
            sel = lo if half == 0 else jnp.logical_not(lo)
            sink = sink_ref[ATTN_HEAD_ORDER[2 * j + half]]
            q_m = jnp.where(sel, q_slab, 0.0).astype(BF16)
            s = jnp.where(visible, _dot(q_m, kt), -jnp.inf)
            mx = jnp.maximum(jnp.max(s, axis=-1, keepdims=True), sink)
            pexp = jnp.exp(s - mx)
            denom = jnp.sum(pexp, axis=-1, keepdims=True) + jnp.exp(sink - mx)
            outs.append(_dot(pexp.astype(BF16), vv) / denom)
        out_ref[:, sl] = jnp.where(lo, outs[0], outs[1]).astype(out_ref.dtype)

    return kt_cur, v_cur


def _swa(proj3, cos_t, sin_t, sinks):
    b, s, _ = proj3.shape
    rows = SWA_BLOCKS_PER_STEP * CHUNK
    assert s % rows == 0
    blk = lambda w, off: pl.BlockSpec((None, rows, w), lambda i, c: (i, c, off // w))
    tab =pl.BlockSpec((None, rows, LANES), lambda i, c: (i, c, 0))
    return pl.pallas_call(
        _swa_kernel,
        grid=(b, s // rows),
        in_specs=[pl.BlockSpec(memory_space=pltpu.SMEM),
                  blk(ATTN_W, AQ_OFF), blk(ATTN_KV_W, AK_OFF), blk(ATTN_KV_W, AV_OFF), tab, tab],
        out_specs=pl.BlockSpec((None, rows, ATTN_W), lambda i, c: (i, c, 0)),
        out_shape=jax.ShapeDtypeStruct((b, s, ATTN_W), BF16),
        scratch_shapes=[pltpu.VMEM((LANES, CHUNK), BF16), pltpu.VMEM((CHUNK, LANES), BF16)],
        compiler_params=_params("parallel", "arbitrary"),
        name="swa",
    )(sinks, proj3, proj3, proj3, cos_t, sin_t)


def _sgu_kernel(u_ref, v_ref, w_ref, bias_ref, g_ref, b_ref, out_ref):
    lo = _lane_lo((CHUNK, LANES))
    for c in range(SGU_CHUNKS_PER_STEP):
        rows = slice(c * CHUNK, (c + 1) * CHUNK)
        for j in range(SGU_GROUPS // 2):
            sl = slice(j * LANES, (j + 1) * LANES)
            u = jax.nn.gelu(u_ref[rows, sl])
            v = jax.nn.gelu(v_ref[rows, sl])
            vn = (_half_layer_norm(v, lo) * g_ref[:, sl] + b_ref[:, sl]).astype(BF16)
            mixed = jnp.where(lo, _dot(w_ref[2 * j], vn), _dot(w_ref[2 * j + 1], vn)) + bias_ref[:, sl]
            out_ref[rows, sl] = (u * mixed).astype(out_ref.dtype)


def _sgu(proj3, w_tril, bias_tok, norm_g, norm_b):
    b, s, _ = proj3.shape
    rows = SGU_CHUNKS_PER_STEP * CHUNK
    assert s % rows == 0
    blk = lambda w, off: pl.BlockSpec((None, rows, w), lambda i, c: (i, c, off // w))
    const2 = lambda shape: pl.BlockSpec(shape, lambda i, c: (0,) * len(shape))
    return pl.pallas_call(
        _sgu_kernel,
        grid=(b, s // rows),
        in_specs=[blk(SGU_W, SU_OFF), blk(SGU_W, SV_OFF), const2((SGU_GROUPS, CHUNK, CHUNK)),
                  const2((CHUNK, SGU_W)), const2((1, SGU_W)), const2((1, SGU_W))],
        out_specs=pl.BlockSpec((None, rows, SGU_W), lambda i, c: (i, c, 0)),
        out_shape=jax.ShapeDtypeStruct((b, s, SGU_W), BF16),
        compiler_params=_params("parallel", "parallel"),
        name="sgu",
    )(proj3, proj3, w_tril, bias_tok, norm_g, norm_b)


def _outproj_kernel(ha_ref, hb_ref, hc_ref, x_ref, wa_ref, wb_ref, wc_ref, g_ref, b_ref, *rest):
    mix = _dot(ha_ref[...], wa_ref[...]) + _dot(hb_ref[...], wb_ref[...]) + _dot(hc_ref[...], wc_ref[...])
    y = _layer_norm_rows(DN_ALPHA * x_ref[...] + mix, g_ref[...], b_ref[...])
    if len(rest) == 1:
        rest[0][...] = y
        return
    wh_ref, wl_ref, rb_ref, o_ref, route_ref, route_t_ref = rest
    o_ref[...] = y
    route = _route_rows(y, wh_ref[...], wl_ref[...], rb_ref[...])
    route_ref[...] = route
    route_t_ref[...] = route.T[0:8, :]


def _outproj_ln(ha, hb, hc, x2d, wa, wb, wc, g, b, router=None):
    n = x2d.shape[0]
    rows = lambda w: pl.BlockSpec((ROW_TILE, w), lambda i: (i, 0))
    const = lambda shape: pl.BlockSpec(shape, lambda i: (0, 0))
    in_specs = [rows(MLSTM_W), rows(ATTN_W), rows(SGU_W), rows(D_MODEL),
                const((MLSTM_W, D_MODEL)), const((ATTN_W, D_MODEL)), const((SGU_W, D_MODEL)),
                const((1, D_MODEL)), const((1, D_MODEL))]
    out_specs = [rows(D_MODEL)]
    out_shape = [jax.ShapeDtypeStruct((n, D_MODEL), F32)]
    operands = (ha, hb, hc, x2d, wa, wb, wc, g, b)
    if router is not None:
        in_specs += [const((D_MODEL, LANES)), const((D_MODEL, LANES)), const((1, LANES))]
        out_specs += [rows(LANES), pl.BlockSpec((8, ROW_TILE), lambda i: (0, i))]
        out_shape += [jax.ShapeDtypeStruct((n, LANES), F32), jax.ShapeDtypeStruct((8, n), F32)]
        operands += tuple(router)
    out = pl.pallas_call(
        _outproj_kernel,
        grid=(n // ROW_TILE,),
        in_specs=in_specs,
        out_specs=out_specs,
        out_shape=out_shape,
        compiler_params=_params("parallel"),
        name="outproj_ln" if router is None else "outproj_ln_route",
    )(*operands)
    return out[0] if router is None else tuple(out)


def _ffn_kernel(x_ref, wg_ref, wu_ref, wd_ref, g_ref, b_ref, o_ref, xb_ref, wgu_ref):
    j = pl.program_id(1)
    last = pl.num_programs(1) - 1
    n_sub = FFN_ROWS // ROW_TILE
    wgu_ref[:, :FF_TILE] = wg_ref[...]
    wgu_ref[:, FF_TILE:] = wu_ref[...]

    def sub_rows(m):
        return pl.ds(pl.multiple_of(m * ROW_TILE, ROW_TILE), ROW_TILE)

    def ffn_tile(m):
        gu = _dot(xb_ref[sub_rows(m), :], wgu_ref[...])
        hidden = (jax.nn.silu(gu[:, :FF_TILE]) * gu[:, FF_TILE:]).astype(BF16)
        return _dot(hidden, wd_ref[...])

    @pl.when(j == 0)
    def _():
        def body(m, carry):
            xb_ref[sub_rows(m), :] = x_ref[sub_rows(m), :].astype(BF16)
            o_ref[sub_rows(m), :] = ffn_tile(m)
            return carry

        lax.fori_loop(0, n_sub, body, 0)

    @pl.when(jnp.logical_and(j > 0, j < last))
    def _():
        def body(m, carry):
            o_ref[sub_rows(m), :] += ffn_tile(m)
            return carry

        lax.fori_loop(0, n_sub, body, 0)

    @pl.when(j == last)
    def _():
        def body(m, carry):
            rows = sub_rows(m)
            ff = o_ref[rows, :] + ffn_tile(m)
            o_ref[rows, :] = _layer_norm_rows(DN_ALPHA * x_ref[rows, :] + ff, g_ref[...], b_ref[...])
            return carry

        lax.fori_loop(0, n_sub, body, 0)


def _ffn_ln(x2d, wg, wu, wd, g, b):
    n = x2d.shape[0]
    d_ff = wd.shape[0]
    assert d_ff // FF_TILE >= 2
    tm = FFN_ROWS
    return pl.pallas_call(
        _ffn_kernel,
        grid=(n // tm, d_ff // FF_TILE),
        in_specs=[pl.BlockSpec((tm, D_MODEL), lambda i, j: (i, 0)),
                  pl.BlockSpec((D_MODEL, FF_TILE), lambda i, j: (0, j)),
                  pl.BlockSpec((D_MODEL, FF_TILE), lambda i, j: (0, j)),
                  pl.BlockSpec((FF_TILE, D_MODEL), lambda i, j: (j, 0)),
                  pl.BlockSpec((1, D_MODEL), lambda i, j: (0, 0)),
                  pl.BlockSpec((1, D_MODEL), lambda i, j: (0, 0))],
        out_specs=pl.BlockSpec((tm, D_MODEL), lambda i, j: (i, 0)),
        out_shape=jax.ShapeDtypeStruct((n, D_MODEL), F32),
        scratch_shapes=[pltpu.VMEM((tm, D_MODEL), BF16), pltpu.VMEM((D_MODEL, 2 * FF_TILE), BF16)],
        compiler_params=_params("parallel", "arbitrary"),
        name="ffn_ln",
    )(x2d, wg, wu, wd, g, b)


def _route_rows(x, w_hi, w_lo, bias):
    xh = x.astype(BF16)
    xl = (x - xh.astype(F32)).astype(BF16)
    logits = _dot(xh, w_hi) + _dot(xl, w_hi) + _dot(xh, w_lo) + bias
    lane = lax.broadcasted_iota(jnp.int32, logits.shape, 1)
    logits = jnp.where(lane < N_EXPERTS, logits, -jnp.inf)
    m1 = jnp.max(logits, axis=-1, keepdims=True)
    i1 = jnp.min(jnp.where(logits == m1, lane, LANES), axis=-1, keepdims=True)
    rest = jnp.where(lane == i1, -jnp.inf, logits)
    m2 = jnp.max(rest, axis=-1, keepdims=True)
    i2 = jnp.min(jnp.where(rest == m2, lane, LANES), axis=-1, keepdims=True)
    e2 = jnp.exp(m2 - m1)
    g1 = 1.0 / (1.0 + e2)
    g2 = e2 / (1.0 + e2)
    return jnp.where(lane == 0, g1, jnp.where(lane == 1, g2, jnp.where(
        lane == 2, i1.astype(F32), jnp.where(lane == 3, i2.astype(F32), 0.0))))


def _dispatch_kernel(slot0_ref, slot1_ref, ends_ref, x_ref, o_hbm, stage_ref, zero_ref, sem, zero_sem):
    t = pl.program_id(0)
    last = pl.num_programs(0) - 1
    s = t % 2

    def wait_tile(ss):
        for _ in range(TOP_K):
            pltpu.make_async_copy(zero_ref, o_hbm.at[pl.ds(0, MOE_SUB), :], sem.at[ss]).wait()

    @pl.when(t == 0)
    def _():
        zero_ref[...] = jnp.zeros_like(zero_ref)

        def zero_sub_tile(m, carry):
            dst = o_hbm.at[pl.ds(pl.multiple_of(m * MOE_SUB, MOE_SUB), MOE_SUB), :]
            cp = pltpu.make_async_copy(zero_ref, dst, zero_sem)
            cp.start()
            cp.wait()
            return carry

        for e in range(N_EXPERTS):
            @pl.when(ends_ref[e] >= MOE_SUB)
            def _():
                zero_sub_tile(ends_ref[e] // MOE_SUB - 1, 0)

        lax.fori_loop(ends_ref[N_EXPERTS - 1] // MOE_SUB, o_hbm.shape[0] // MOE_SUB, zero_sub_tile, 0)

    @pl.when(t >= 2)
    def _():
        wait_tile(s)

    stage_ref[s] = x_ref[...].reshape(MOE_SUB // SUBLANES, SUBLANES, D_MODEL)

    def issue(i, carry):
        for u in range(SUBLANES):
            for slot_ref in (slot0_ref, slot1_ref):
                pltpu.make_async_copy(stage_ref.at[s, i, pl.ds(u, 1), :],
                                      o_hbm.at[pl.ds(slot_ref[t * MOE_SUB + i * SUBLANES + u], 1), :], sem.at[s]).start()
        return carry

    lax.fori_loop(0, MOE_SUB // SUBLANES, issue, 0)

    @pl.when(t == last)
    def _():
        wait_tile(s)
        wait_tile(1 - s)


def _dispatch_rows(x2d, slots, ends, n_rows):
    n = x2d.shape[0]
    assert n // MOE_SUB >= 2
    return pl.pallas_call(
        _dispatch_kernel,
        grid_spec=pltpu.PrefetchScalarGridSpec(
            num_scalar_prefetch=3,
            grid=(n // MOE_SUB,),
            in_specs=[pl.BlockSpec((MOE_SUB, D_MODEL), lambda t, s0, s1, en: (t, 0))],
            out_specs=pl.BlockSpec(memory_space=pl.ANY),
            scratch_shapes=[pltpu.VMEM((2, MOE_SUB // SUBLANES, SUBLANES, D_MODEL), F32), pltpu.VMEM((MOE_SUB, D_MODEL), F32),
                            pltpu.SemaphoreType.DMA((2,)), pltpu.SemaphoreType.DMA(())]),
        out_shape=jax.ShapeDtypeStruct((n_rows, D_MODEL), F32),
        compiler_params=_params("arbitrary"),
        name="moe_dispatch",
    )(slots[0], slots[1], ends, x2d)


def _moe_kernel(exp_ref, row0_ref, nsub_ref, tail_ref, x_hbm, wg_ref, wu_ref, wd_ref, y_hbm,
                xb_ref, acc_ref, wgu_ref, wdb_ref, stage_ref, in_sem, out_sem):
    v = pl.program_id(0)
    j = pl.program_id(1)
    last = pl.num_programs(1) - 1
    n_sub = nsub_ref[v]
    row0 = row0_ref[v]

    def sub_rows(m):
        return pl.ds(pl.multiple_of(m * MOE_SUB, MOE_SUB), MOE_SUB)

    def hbm_rows(m):
        return pl.ds(pl.multiple_of(row0 + m * MOE_SUB, MOE_SUB), MOE_SUB)

    @pl.when(jnp.logical_and(v == 0, j == 0))
    def _():
        stage_ref[0] = jnp.zeros((MOE_SUB, D_MODEL), F32)

        def zero_sub_tile(m, carry):
            cp = pltpu.make_async_copy(stage_ref.at[0], y_hbm.at[sub_rows(m), :], out_sem)
            cp.start()
            cp.wait()
            return carry

        lax.fori_loop(tail_ref[0], y_hbm.shape[0] // MOE_SUB, zero_sub_tile, 0)

    @pl.when(n_sub > 0)
    def _():
        wgu_ref[:, :MOE_FF_TILE] = wg_ref[...].astype(BF16)
        wgu_ref[:, MOE_FF_TILE:] = wu_ref[...].astype(BF16)
        wdb_ref[...] = wd_ref[...].astype(BF16)

        def ffn_tile(m):
            xs = xb_ref[sub_rows(m), :]
            gu = _dot(xs, wgu_ref[...])
            hidden = (jax.nn.silu(gu[:, :MOE_FF_TILE]) * gu[:, MOE_FF_TILE:]).astype(BF16)
            return _dot(hidden, wdb_ref[...])

        def y_copy(m):
            return pltpu.make_async_copy(acc_ref.at[sub_rows(m), :], y_hbm.at[hbm_rows(m), :], out_sem)

        @pl.when(j == 0)
        def _():
            def x_copy(m):
                return pltpu.make_async_copy(x_hbm.at[hbm_rows(m), :], stage_ref.at[m % 2], in_sem.at[m % 2])

            x_copy(0).start()

            def body(m, carry):
                @pl.when(m + 1 < n_sub)
                def _():
                    x_copy(m + 1).start()

                x_copy(m).wait()
                xb_ref[sub_rows(m), :] = stage_ref[m % 2].astype(BF16)
                acc_ref[sub_rows(m), :] = ffn_tile(m)
                return carry

            lax.fori_loop(0, n_sub, body, 0)

        @pl.when(jnp.logical_and(j > 0, j < last))
        def _():
            def body(m, carry):
                acc_ref[sub_rows(m), :] += ffn_tile(m)
                return carry

            lax.fori_loop(0, n_sub, body, 0)

        @pl.when(j == last)
        def _():
            def body(m, carry):
                acc_ref[sub_rows(m), :] += ffn_tile(m)
                y_copy(m).start()
                return carry

            def drain(m, carry):
                y_copy(m).wait()
                return carry

            lax.fori_loop(0, n_sub, body, 0)
            lax.fori_loop(0, n_sub, drain, 0)


def _moe_grouped(xs, wg, wu, wd, visit_exp, visit_row0, visit_nsub, tail_sub):
    n_rows = xs.shape[0]
    n_visits = visit_exp.shape[0]
    d_ff = wg.shape[2]
    assert d_ff // MOE_FF_TILE >= 2
    rows = MOE_VISIT_SUBS * MOE_SUB
    w_in = lambda v, j, e, r, ns, tl: (e[v], 0, j)
    w_out = lambda v, j, e, r, ns, tl: (e[v], j, 0)
    return pl.pallas_call(
        _moe_kernel,
        grid_spec=pltpu.PrefetchScalarGridSpec(
            num_scalar_prefetch=4,
            grid=(n_visits, d_ff // MOE_FF_TILE),
            in_specs=[pl.BlockSpec(memory_space=pl.ANY),
                      pl.BlockSpec((None, D_MODEL, MOE_FF_TILE), w_in),
                      pl.BlockSpec((None, D_MODEL, MOE_FF_TILE), w_in),
                      pl.BlockSpec((None, MOE_FF_TILE, D_MODEL), w_out)],
            out_specs=pl.BlockSpec(memory_space=pl.ANY),
            scratch_shapes=[pltpu.VMEM((rows, D_MODEL), BF16), pltpu.VMEM((rows, D_MODEL), F32),
                            pltpu.VMEM((D_MODEL, 2 * MOE_FF_TILE), BF16),
                            pltpu.VMEM((MOE_FF_TILE, D_MODEL), BF16),
                            pltpu.VMEM((2, MOE_SUB, D_MODEL), F32),
                            pltpu.SemaphoreType.DMA((2,)), pltpu.SemaphoreType.DMA(())]),
        out_shape=jax.ShapeDtypeStruct((n_rows, D_MODEL), F32),
        compiler_params=_params("arbitrary", "arbitrary"),
        name="moe_grouped",
    )(visit_exp, visit_row0, visit_nsub, tail_sub, xs, wg, wu, wd)


def _combine_kernel(slot0_ref, slot1_ref, x_ref, gate_ref, y_hbm, y_grouped_hbm, g_ref, b_ref, o_ref, buf_ref, sem):
    t = pl.program_id(0)
    s = t % 2

    def issue_tile(tt, ss):
        def issue(i, carry):
            for u in range(SUBLANES):
                for k, slot_ref in enumerate((slot0_ref, slot1_ref)):
                    pltpu.make_async_copy(y_hbm.at[pl.ds(slot_ref[tt * MOE_SUB + i * SUBLANES + u], 1), :],
                                          buf_ref.at[ss, k, i, pl.ds(u, 1), :], sem.at[ss]).start()
            return carry

        lax.fori_loop(0, MOE_SUB // SUBLANES, issue, 0)

    @pl.when(t == 0)
    def _():
        issue_tile(0, 0)

    @pl.when(t + 1 < pl.num_programs(0))
    def _():
        issue_tile(t + 1, 1 - s)

    for k in range(TOP_K):
        pltpu.make_async_copy(y_grouped_hbm.at[pl.ds(0, MOE_SUB // SUBLANES)], buf_ref.at[s, k], sem.at[s]).wait()
    gate = gate_ref[...]
    y0 = buf_ref[s, 0].reshape(MOE_SUB, D_MODEL)
    y1 = buf_ref[s, 1].reshape(MOE_SUB, D_MODEL)
    ff = gate[:, 0:1] * y0 + gate[:, 1:2] * y1
    o_ref[...] = _layer_norm_rows(DN_ALPHA * x_ref[...] + ff, g_ref[...], b_ref[...])


def _combine_ln(x2d, route, slots, ys, g, b):
    n = x2d.shape[0]
    return pl.pallas_call(
        _combine_kernel,
        grid_spec=pltpu.PrefetchScalarGridSpec(
            num_scalar_prefetch=2,
            grid=(n // MOE_SUB,),
            in_specs=[pl.BlockSpec((MOE_SUB, D_MODEL), lambda t, s0, s1: (t, 0)),
                      pl.BlockSpec((MOE_SUB, LANES), lambda t, s0, s1: (t, 0)),
                      pl.BlockSpec(memory_space=pl.ANY), pl.BlockSpec(memory_space=pl.ANY),
                      pl.BlockSpec((1, D_MODEL), lambda t, s0, s1: (0, 0)),
                      pl.BlockSpec((1, D_MODEL), lambda t, s0, s1: (0, 0))],
            out_specs=pl.BlockSpec((MOE_SUB, D_MODEL), lambda t, s0, s1: (t, 0)),
            scratch_shapes=[pltpu.VMEM((2, TOP_K, MOE_SUB // SUBLANES, SUBLANES, D_MODEL), F32),
                            pltpu.SemaphoreType.DMA((2,))]),
        out_shape=jax.ShapeDtypeStruct((n, D_MODEL), F32),
        compiler_params=_params("arbitrary"),
        name="moe_combine_ln",
    )(slots[0], slots[1], x2d, route, ys, ys.reshape(ys.shape[0] // SUBLANES, SUBLANES, D_MODEL), g, b)


def _routing_tables(route_t, n):
    idx = route_t[2:4].astype(jnp.int32)
    expert = jnp.arange(N_EXPERTS, dtype=jnp.int32)[:, None]
    chosen = [idx[k][None, :] == expert for k in range(TOP_K)]
    onehot = jnp.logical_or(chosen[0], chosen[1]).astype(jnp.int32)
    rank = jnp.cumsum(onehot, axis=1) - onehot
    counts = jnp.sum(onehot, axis=1)
    padded = ((counts + MOE_SUB - 1) // MOE_SUB) * MOE_SUB
    ends = jnp.cumsum(padded)
    starts = ends - padded
    place = starts[:, None] + rank
    slot = [jnp.sum(jnp.where(chosen[k], place, 0), axis=0).astype(jnp.int32) for k in range(TOP_K)]

    n_rows = -(-(n * TOP_K + N_EXPERTS * (MOE_SUB - 1)) // MOE_SUB) * MOE_SUB

    visit_rows = MOE_VISIT_SUBS * MOE_SUB
    max_chunks = -(-n_rows // visit_rows)
    chunk = jnp.arange(max_chunks, dtype=jnp.int32)[None, :]
    left = padded[:, None] - chunk * visit_rows
    valid = (left > 0).reshape(-1)
    n_visits = n_rows // visit_rows + N_EXPERTS
    order = jnp.argsort(jnp.logical_not(valid), stable=True)[:n_visits]
    n_valid = jnp.sum(valid.astype(jnp.int32))
    live = jnp.arange(n_visits) < n_valid
    order = jnp.where(live, order, order[jnp.maximum(n_valid - 1, 0)])
    v_exp = (order // max_chunks).astype(jnp.int32)
    v_chunk = (order % max_chunks).astype(jnp.int32)
    v_row0 = jnp.where(live, starts[v_exp] + v_chunk * visit_rows, 0).astype(jnp.int32)
    v_nsub = jnp.where(live, jnp.minimum(left.reshape(-1)[order], visit_rows) // MOE_SUB, 0).astype(jnp.int32)
    tail_sub = (ends[N_EXPERTS - 1:] // MOE_SUB).astype(jnp.int32)
    return slot, ends.astype(jnp.int32), n_rows, (v_exp, v_row0, v_nsub, tail_sub)


def _router_operands(w_router, b_router):
    w_pad = jnp.zeros((D_MODEL, LANES), F32).at[:, :N_EXPERTS].set(w_router)
    w_hi = w_pad.astype(BF16)
    w_lo = (w_pad - w_hi.astype(F32)).astype(BF16)
    b_pad = jnp.zeros((1, LANES), F32).at[0, :N_EXPERTS].set(b_router)
    return w_hi, w_lo, b_pad


def _moe_ln(x2d, route, route_t, wg, wu, wd, g, b):
    n = x2d.shape[0]
    slots, ends, n_rows, visits = _routing_tables(route_t, n)
    xs = _dispatch_rows(x2d, slots, ends, n_rows)
    ys = _moe_grouped(xs, wg, wu, wd, *visits)
    return _combine_ln(x2d, route, slots, ys, g, b)


_SRC_GATES = 4 * MLSTM_W
_SRC_AQ = _SRC_GATES + 2 * MLSTM_HEADS
_SRC_AK = _SRC_AQ + ATTN_W
_IN_PROJ_MOVES = (
    ((0, 0, 4 * MLSTM_W),)
    + tuple((_SRC_AQ + h * HEAD_DIM, AQ_OFF + i * HEAD_DIM, HEAD_DIM) for i, h in enumerate(ATTN_HEAD_ORDER))
    + ((_SRC_AK, AK_OFF, ATTN_KV_W), (_SRC_AK + ATTN_KV_W, AV_OFF, ATTN_KV_W),
       (_SRC_AK + 2 * ATTN_KV_W, SU_OFF, SGU_W), (_SRC_AK + 2 * ATTN_KV_W + SGU_W, SV_OFF, SGU_W)))
D_PROJ = _SRC_AK + 2 * ATTN_KV_W + 2 * SGU_W


def _relayout_columns(src, dst_dtype):
    lead = src.shape[:-1]
    out = jnp.zeros(lead + (D_PROJ_PAD,), dst_dtype)
    for s, d, w in _IN_PROJ_MOVES + ((_SRC_GATES, GATE_OFF, 2 * MLSTM_HEADS),):
        out = out.at[..., d:d + w].set(src[..., s:s + w].astype(dst_dtype))
    return out


def _w_layout_kernel(w_ref, o_ref):
    for s, d, w in _IN_PROJ_MOVES:
        o_ref[:, d:d + w] = w_ref[:, s:s + w].astype(o_ref.dtype)
    pad = jnp.zeros((w_ref.shape[0], LANES - 2 * MLSTM_HEADS), F32)
    gates = jnp.concatenate([w_ref[:, _SRC_GATES:_SRC_GATES + 2 * MLSTM_HEADS], pad], axis=1)
    o_ref[:, GATE_OFF:GATE_OFF + LANES] = gates.astype(o_ref.dtype)


def _layout_in_proj(w_in, b_in):
    depth = w_in.shape[0]
    rows = D_MODEL // 4
    w_p = pl.pallas_call(
        _w_layout_kernel,
        grid=(depth, D_MODEL // rows),
        in_specs=[pl.BlockSpec((None, rows, D_PROJ), lambda l, i: (l, i, 0))],
        out_specs=pl.BlockSpec((None, rows, D_PROJ_PAD), lambda l, i: (l, i, 0)),
        out_shape=jax.ShapeDtypeStruct((depth, D_MODEL, D_PROJ_PAD), BF16),
        compiler_params=_params("parallel", "parallel"),
        name="w_in_layout",
    )(w_in)
    return w_p, _relayout_columns(b_in, F32)[:, None, :]


def _rope_tables(positions):
    inv_freq = ROPE_THETA ** (-jnp.arange(0, ROPE_DIM, 2, dtype=F32) / ROPE_DIM)
    ang = inv_freq[None, :, None] * positions.astype(F32)[:, None, :]
    cos, sin = jnp.cos(ang), jnp.sin(ang)
    ones = jnp.ones((ang.shape[0], HEAD_DIM - ROPE_DIM, ang.shape[2]), F32)
    cos_head = jnp.concatenate([cos, cos, ones], 1)
    sin_head = jnp.concatenate([-sin, sin, 0.0 * ones], 1)
    cos_t = jnp.concatenate([cos_head, cos_head], 1)
    sin_t = jnp.concatenate([sin_head, sin_head], 1)
    return _to_token_major(cos_t, sin_t)


def _table_transpose_kernel(cos_ref, sin_ref, cos_out, sin_out):
    cos_out[...] = cos_ref[...].T
    sin_out[...] = sin_ref[...].T


def _to_token_major(cos_t, sin_t):
    b, lanes, s = cos_t.shape
    src = pl.BlockSpec((None, lanes, ROW_TILE), lambda i, c: (i, 0, c))
    dst = pl.BlockSpec((None, ROW_TILE, lanes), lambda i, c: (i, c, 0))
    shape = jax.ShapeDtypeStruct((b, s, lanes), F32)
    return pl.pallas_call(
        _table_transpose_kernel,
        grid=(b, s // ROW_TILE),
        in_specs=[src, src],
        out_specs=[dst, dst],
        out_shape=[shape, shape],
        compiler_params=_params("parallel", "parallel"),
        name="rope_tables",
    )(cos_t, sin_t)


def kernel(x, positions, w_in, b_in, conv_w, mlstm_norm_g, attn_sinks, sgu_w_s, sgu_b_s, sgu_norm_g, sgu_norm_b, w_out, ln1_g, ln1_b, ln2_g, ln2_b, ffn_w_gate, ffn_w_up, ffn_w_down, moe_w_router, moe_b_router, moe_w_gate, moe_w_up, moe_w_down):
    bsz, seq, _ = x.shape
    n = bsz * seq
    cos_t, sin_t = _rope_tables(positions)
    tril = jnp.tril(jnp.ones((CHUNK, CHUNK), bool))
    x2d = x.reshape(n, D_MODEL)
    w_p, b_p = _layout_in_proj(w_in, b_in)
    for layer in range(DEPTH):
        proj3 = _inproj(x2d, w_p[layer], b_p[layer]).reshape(bsz, seq, D_PROJ_PAD)
        h_a = _mlstm(proj3, conv_w[layer], mlstm_norm_g[layer][None, :])
        h_b = _swa(proj3, cos_t, sin_t, attn_sinks[layer])
        w_tril = jnp.where(tril, sgu_w_s[layer], 0.0).astype(BF16)
        bias_tok = jnp.repeat(sgu_b_s[layer].T, HEAD_DIM, axis=1)
        h_c = _sgu(proj3, w_tril, bias_tok, sgu_norm_g[layer][None, :], sgu_norm_b[layer][None, :])
        wo = w_out[layer]
        wa = wo[:MLSTM_W].astype(BF16)
        wb = wo[MLSTM_W:MLSTM_W + ATTN_W].reshape(ATTN_Q_HEADS, HEAD_DIM, D_MODEL)[np.array(ATTN_HEAD_ORDER)]
        wb = wb.reshape(ATTN_W, D_MODEL).astype(BF16)
        wc = wo[MLSTM_W + ATTN_W:].astype(BF16)
        j = layer // 2
        dense = layer % 2 == 0
        mixed = _outproj_ln(h_a.reshape(n, MLSTM_W), h_b.reshape(n, ATTN_W), h_c.reshape(n, SGU_W), x2d,
                            wa, wb, wc, ln1_g[layer][None, :], ln1_b[layer][None, :],
                            router=None if dense else _router_operands(moe_w_router[j], moe_b_router[j]))
        g2, b2 = ln2_g[layer][None, :], ln2_b[layer][None, :]
        if dense:
            x2d = _ffn_ln(mixed, ffn_w_gate[j].astype(BF16), ffn_w_up[j].astype(BF16),
                          ffn_w_down[j].astype(BF16), g2, b2)
        else:
            x2d, route, route_t = mixed
            x2d = _moe_ln(x2d, route, route_t, moe_w_gate[j], moe_w_up[j], moe_w_down[j], g2, b2)
    return x2d.reshape(bsz, seq, D_MODEL)
```

```python
import jax
import jax.numpy as jnp
import numpy as np
from jax import lax
from jax.experimental import pallas as pl
from jax.experimental.pallas import tpu as pltpu

F32 = jnp.float32
BF16 = jnp.bfloat16

D_MODEL = 1024
HEAD_DIM = 64
LANES = 128
SUBLANES = 8
MLSTM_HEADS = 6
ATTN_Q_HEADS = 6
ATTN_KV_HEADS = 2
SGU_GROUPS = 4
MLSTM_W = MLSTM_HEADS * HEAD_DIM
ATTN_W = ATTN_Q_HEADS * HEAD_DIM
ATTN_KV_W = ATTN_KV_HEADS * HEAD_DIM
SGU_W = SGU_GROUPS * HEAD_DIM
CHUNK = 128
CONV_WIDTH = 4
ROPE_DIM = HEAD_DIM // 4
ROPE_THETA = 500000.0
N_EXPERTS = 8
TOP_K = 2
DEPTH = 2
DN_ALPHA = (2.0 * DEPTH) ** 0.25
LN_EPS = 1e-5

QK_OFF, V_OFF, O_OFF, AQ_OFF = 0, 768, 1152, 1536
GATE_OFF, AK_OFF, AV_OFF, SU_OFF, SV_OFF = 1920, 2048, 2176, 2304, 2560
D_PROJ_PAD = 2816
ATTN_HEAD_ORDER = (0, 3, 1, 4, 2, 5)

VMEM_LIMIT = 56 * 1024 * 1024

SGU_CHUNKS_PER_STEP = 4
MLSTM_CHUNKS_PER_STEP = 8
SWA_BLOCKS_PER_STEP = 2
ROW_TILE = 512
FFN_ROWS = 2048
FFN_SUB = 1024
FF_TILE = 256
MOE_SUB = 512
MOE_VISIT_SUBS = 9
MOE_FF_TILE = 512


def _params(*sem):
    return pltpu.CompilerParams(dimension_semantics=sem, vmem_limit_bytes=VMEM_LIMIT)


def _lane_lo(shape):
    return lax.broadcasted_iota(jnp.int32, shape, len(shape) - 1) < HEAD_DIM


def _layer_norm_rows(z, g, b):
    mu = jnp.mean(z, axis=-1, keepdims=True)
    zc = z - mu
    var = jnp.mean(zc * zc, axis=-1, keepdims=True)
    return zc * lax.rsqrt(var + LN_EPS) * g + b


def _half_layer_norm(x, lo):
    inv = 1.0 / HEAD_DIM
    s_lo = jnp.sum(jnp.where(lo, x, 0.0), axis=-1, keepdims=True)
    s_all = jnp.sum(x, axis=-1, keepdims=True)
    mu = jnp.where(lo, s_lo, s_all - s_lo) * inv
    xc = x - mu
    sq = xc * xc
    q_lo = jnp.sum(jnp.where(lo, sq, 0.0), axis=-1, keepdims=True)
    q_all = jnp.sum(sq, axis=-1, keepdims=True)
    var = jnp.where(lo, q_lo, q_all - q_lo) * inv
    return xc * lax.rsqrt(var + LN_EPS)


def _split3(a):
    h1 = a.astype(BF16)
    r1 = a - h1.astype(F32)
    h2 = r1.astype(BF16)
    r2 = r1 - h2.astype(F32)
    return h1, h2, r2.astype(BF16)


def _split3_masked(a):
    def top(x):
        bits = lax.bitcast_convert_type(x, jnp.int32) & jnp.int32(-65536)
        return lax.bitcast_convert_type(bits, F32)

    h1 = top(a)
    r1 = a - h1
    h2 = top(r1)
    return h1.astype(BF16), h2.astype(BF16), (r1 - h2).astype(BF16)


def _dot(a, b):
    return jnp.dot(a, b, preferred_element_type=F32)


def _inproj_kernel(x_ref, w_ref, b_ref, o_ref):
    o_ref[...] = _dot(x_ref[...].astype(BF16), w_ref[...]) + b_ref[...]


def _inproj(x2d, w, b):
    n = x2d.shape[0]
    return pl.pallas_call(
        _inproj_kernel,
        grid=(n // ROW_TILE,),
        in_specs=[pl.BlockSpec((ROW_TILE, D_MODEL), lambda i: (i, 0)),
                  pl.BlockSpec((D_MODEL, D_PROJ_PAD), lambda i: (0, 0)),
                  pl.BlockSpec((1, D_PROJ_PAD), lambda i: (0, 0))],
        out_specs=pl.BlockSpec((ROW_TILE, D_PROJ_PAD), lambda i: (i, 0)),
        out_shape=jax.ShapeDtypeStruct((n, D_PROJ_PAD), F32),
        compiler_params=_params("parallel"),
        name="inproj",
    )(x2d, w, b)


def _mlstm_kernel(qk_ref, v_ref, o_ref, gate_ref, cw_ref, g_ref, out_ref, buf_ref, ct_ref, m_ref):
    rows = MLSTM_CHUNKS_PER_STEP * CHUNK

    @pl.when(pl.program_id(1) == 0)
    def _():
        buf_ref[0:8, :] = jnp.zeros((8, 2 * MLSTM_W), F32)
        ct_ref[...] = jnp.zeros_like(ct_ref)
        m_ref[...] = jnp.zeros_like(m_ref)

    buf_ref[8:8 + rows, :] = qk_ref[...]
    ct = [ct_ref[p] for p in range(MLSTM_HEADS // 2)]
    m = [m_ref[h:h + 1, 0:1] for h in range(MLSTM_HEADS)]
    for i in range(MLSTM_CHUNKS_PER_STEP):
        blk = pl.ds(i * CHUNK, CHUNK)
        ct, m = _mlstm_chunk(i * CHUNK, buf_ref, v_ref.at[blk], o_ref.at[blk], gate_ref.at[blk], cw_ref, g_ref,
                             out_ref.at[blk], ct, m)
    buf_ref[0:8, :] = qk_ref[rows - 8:rows, :]
    for p in range(MLSTM_HEADS // 2):
        ct_ref[p] = ct[p]
    for h in range(MLSTM_HEADS):
        m_ref[h:h + 1, :] = jnp.broadcast_to(m[h], (1, LANES))


def _mlstm_chunk(row0, buf_ref, v_ref, o_ref, gate_ref, cw_ref, g_ref, out_ref, ct_in, m_in):
    cw = cw_ref[...]
    base = row0 + 8 - (CONV_WIDTH - 1)
    acc = cw[0:1, :] * buf_ref[base:base + CHUNK, :]
    for j in range(1, CONV_WIDTH):
        acc = acc + cw[j:j + 1, :] * buf_ref[base + j:base + j + CHUNK, :]
    qk = jax.nn.silu(acc)

    gates = gate_ref[...]
    lf = jax.nn.log_sigmoid(gates)
    row = lax.broadcasted_iota(jnp.int32, (CHUNK, CHUNK), 0)
    col = lax.broadcasted_iota(jnp.int32, (CHUNK, CHUNK), 1)
    causal = col <= row
    tri = causal.astype(BF16)
    l1, l2, l3 = _split3(lf)
    cum = _dot(tri, l1) + _dot(tri, l2) + _dot(tri, l3)
    gates_t = gates.T
    cum_t = cum.T
    cum_parts = _split3_masked(cum)

    lo = _lane_lo((CHUNK, LANES))
    ct_out = []
    m_out = []
    for p in range(MLSTM_HEADS // 2):
        sl = slice(p * LANES, (p + 1) * LANES)
        q_slab = qk[:, sl] * (HEAD_DIM ** -0.5)
        k_slab = qk[:, MLSTM_W + p * LANES:MLSTM_W + (p + 1) * LANES]
        kt_slab = k_slab.T
        v_slab = v_ref[:, sl]
        ct_pair = ct_in[p]
        ct_pair_b = ct_pair.astype(BF16)
        halves = []
        new_ct = []
        for half in range(2):
            h = 2 * p + half
            sel = lo if half == 0 else jnp.logical_not(lo)
            li_row = gates_t[h:h + 1, :]
            bc_row = cum_t[MLSTM_HEADS + h:MLSTM_HEADS + h + 1, :]
            pick = (row == MLSTM_HEADS + h).astype(BF16)
            bc_col = _dot(cum_parts[0], pick) + _dot(cum_parts[1], pick) + _dot(cum_parts[2], pick)
            b_tot = bc_row[:, CHUNK - 1:CHUNK]
            m_prev = m_in[h]

            dmat = jnp.where(causal, bc_col - bc_row + li_row, -jnp.inf)
            m_inter = bc_col + m_prev
            m_row = jnp.maximum(m_inter, jnp.max(dmat, axis=-1, keepdims=True))
            q_m = jnp.where(sel, q_slab, 0.0).astype(BF16)
            s = _dot(q_m, kt_slab.astype(BF16))
            pmat = (s * jnp.exp(dmat - m_row)).astype(BF16)
            w_inter = jnp.exp(m_inter - m_row)
            v_aug = jnp.where(sel, v_slab, 1.0).astype(BF16)
            halves.append((_dot(pmat, v_aug) + w_inter * _dot(q_m, ct_pair_b), jnp.exp(-m_row)))

            a_row = b_tot - bc_row + li_row
            m_loc = jnp.max(a_row, axis=-1, keepdims=True)
            w_row = jnp.exp(a_row - m_loc)
            kt_h = kt_slab[half * HEAD_DIM:(half + 1) * HEAD_DIM, :]
            ct_loc = _dot((kt_h * w_row).astype(BF16), v_aug)
            m_new = jnp.maximum(b_tot + m_prev, m_loc)
            s_old = jnp.exp(b_tot + m_prev - m_new)
            s_loc = jnp.exp(m_loc - m_new)
            new_ct.append(s_old * ct_pair[half * HEAD_DIM:(half + 1) * HEAD_DIM, :] + s_loc * ct_loc)
            m_out.append(m_new)

        ct_out.append(jnp.concatenate(new_ct, axis=0))
        (r_even, stab_even), (r_odd, stab_odd) = halves
        num = jnp.where(lo, r_even, r_odd)
        den = pltpu.roll(jnp.where(lo, r_odd, r_even), HEAD_DIM, 1)
        hh = num / jnp.maximum(jnp.abs(den), jnp.where(lo, stab_even, stab_odd))
        hn = _half_layer_norm(hh, lo) * g_ref[:, sl]
        out_ref[:, sl] = (hn * jax.nn.sigmoid(o_ref[:, sl])).astype(out_ref.dtype)
    return ct_out, m_out


def _mlstm(proj3, conv_w, norm_g):
    b, s, _ = proj3.shape
    rows = MLSTM_CHUNKS_PER_STEP * CHUNK
    assert s % rows == 0
    blk = lambda w, off: pl.BlockSpec((None, rows, w), lambda i, c: (i, c, off // w))
    return pl.pallas_call(
        _mlstm_kernel,
        grid=(b, s // rows),
        in_specs=[blk(2 * MLSTM_W, QK_OFF), blk(MLSTM_W, V_OFF), blk(MLSTM_W, O_OFF), blk(LANES, GATE_OFF),
                  pl.BlockSpec((CONV_WIDTH, 2 * MLSTM_W), lambda i, c: (0, 0)),
                  pl.BlockSpec((1, MLSTM_W), lambda i, c: (0, 0))],
        out_specs=pl.BlockSpec((None, rows, MLSTM_W), lambda i, c: (i, c, 0)),
        out_shape=jax.ShapeDtypeStruct((b, s, MLSTM_W), BF16),
        scratch_shapes=[pltpu.VMEM((8 + rows, 2 * MLSTM_W), F32),
                        pltpu.VMEM((MLSTM_HEADS // 2, LANES, LANES), F32),
                        pltpu.VMEM((8, LANES), F32)],
        compiler_params=_params("parallel", "arbitrary"),
        name="mlstm",
    )(proj3, proj3, proj3, proj3, conv_w, norm_g)


def _rope(x, cos_t, sin_t, first):
    return x * cos_t + jnp.where(first, pltpu.roll(x, LANES - ROPE_DIM // 2, 1),
                                 pltpu.roll(x, ROPE_DIM // 2, 1)) * sin_t


def _swa_kernel(sink_ref, q_ref, k_ref, v_ref, cos_ref, sin_ref, out_ref, kt_ref, vv_ref):
    first_step = pl.program_id(1) == 0

    @pl.when(first_step)
    def _():
        kt_ref[...] = jnp.zeros_like(kt_ref)
        vv_ref[...] = jnp.zeros_like(vv_ref)

    kt_prev = kt_ref[...]
    v_prev = vv_ref[...]
    for i in range(SWA_BLOCKS_PER_STEP):
        rows = pl.ds(i * CHUNK, CHUNK)
        hide_previous = jnp.where(first_step, 2 * CHUNK, 0) if i == 0 else 0
        kt_prev, v_prev = _swa_block(hide_previous, sink_ref, q_ref.at[rows], k_ref.at[rows], v_ref.at[rows],
                                     cos_ref.at[rows], sin_ref.at[rows], out_ref.at[rows], kt_prev, v_prev)
    kt_ref[...] = kt_prev
    vv_ref[...] = v_prev


def _swa_block(first_block_shift, sink_ref, q_ref, k_ref, v_ref, cos_ref, sin_ref, out_ref, kt_prev, v_prev):
    cos_t = cos_ref[...]
    sin_t = sin_ref[...]
    lane = lax.broadcasted_iota(jnp.int32, (CHUNK, LANES), 1)
    lo = lane < HEAD_DIM
    first = (lane % HEAD_DIM) < ROPE_DIM // 2

    kt_cur = _rope(k_ref[...], cos_t, sin_t, first).T.astype(BF16)
    v_cur = v_ref[...].astype(BF16)
    kt = jnp.concatenate([kt_prev, kt_cur], axis=1)
    vv = jnp.concatenate([v_prev, v_cur], axis=0)

    row = lax.broadcasted_iota(jnp.int32, (CHUNK, 2 * CHUNK), 0)
    col = lax.broadcasted_iota(jnp.int32, (CHUNK, 2 * CHUNK), 1)
    visible = jnp.logical_or(jnp.logical_and(col < CHUNK, col > row + first_block_shift),
                             jnp.logical_and(col >= CHUNK, col - CHUNK <= row))

    for j in range(ATTN_Q_HEADS // 2):
        sl = slice(j * LANES, (j + 1) * LANES)
        q_slab = _rope(q_ref[:, sl], cos_t, sin_t, first) * (HEAD_DIM ** -0.5)
        outs = []
        for half in range(2):
            sel = lo if half == 0 else jnp.logical_not(lo)
            sink = sink_ref[ATTN_HEAD_ORDER[2 * j + half]]
            q_m = jnp.where(sel, q_slab, 0.0).astype(BF16)
            s = jnp.where(visible, _dot(q_m, kt), -jnp.inf)
            mx = jnp.maximum(jnp.max(s, axis=-1, keepdims=True), sink)
            pexp = jnp.exp(s - mx)
            denom = jnp.sum(pexp, axis=-1, keepdims=True) + jnp.exp(sink - mx)
            outs.append(_dot(pexp.astype(BF16), vv) / denom)
        out_ref[:, sl] = jnp.where(lo, outs[0], outs[1]).astype(out_ref.dtype)

    return kt_cur, v_cur


def _swa(proj3, cos_t, sin_t, sinks):
    b, s, _ = proj3.shape
    rows = SWA_BLOCKS_PER_STEP * CHUNK
    assert s % rows == 0
    blk = lambda w, off: pl.BlockSpec((None, rows, w), lambda i, c: (i, c, off // w))
    tab =pl.BlockSpec((None, rows, LANES), lambda i, c: (i, c, 0))
    return pl.pallas_call(
        _swa_kernel,
        grid=(b, s // rows),
        in_specs=[pl.BlockSpec(memory_space=pltpu.SMEM),
                  blk(ATTN_W, AQ_OFF), blk(ATTN_KV_W, AK_OFF), blk(ATTN_KV_W, AV_OFF), tab, tab],
        out_specs=pl.BlockSpec((None, rows, ATTN_W), lambda i, c: (i, c, 0)),
        out_shape=jax.ShapeDtypeStruct((b, s, ATTN_W), BF16),
        scratch_shapes=[pltpu.VMEM((LANES, CHUNK), BF16), pltpu.VMEM((CHUNK, LANES), BF16)],
        compiler_params=_params("parallel", "arbitrary"),
        name="swa",
    )(sinks, proj3, proj3, proj3, cos_t, sin_t)


def _sgu_kernel(u_ref, v_ref, w_ref, bias_ref, g_ref, b_ref, out_ref):
    lo = _lane_lo((CHUNK, LANES))
    for c in range(SGU_CHUNKS_PER_STEP):
        rows = slice(c * CHUNK, (c + 1) * CHUNK)
        for j in range(SGU_GROUPS // 2):
            sl = slice(j * LANES, (j + 1) * LANES)
            u = jax.nn.gelu(u_ref[rows, sl])
            v = jax.nn.gelu(v_ref[rows, sl])
            vn = (_half_layer_norm(v, lo) * g_ref[:, sl] + b_ref[:, sl]).astype(BF16)
            mixed = jnp.where(lo, _dot(w_ref[2 * j], vn), _dot(w_ref[2 * j + 1], vn)) + bias_ref[:, sl]
            out_ref[rows, sl] = (u * mixed).astype(out_ref.dtype)


def _sgu(proj3, w_tril, bias_tok, norm_g, norm_b):
    b, s, _ = proj3.shape
    rows = SGU_CHUNKS_PER_STEP * CHUNK
    assert s % rows == 0
    blk = lambda w, off: pl.BlockSpec((None, rows, w), lambda i, c: (i, c, off // w))
    const2 = lambda shape: pl.BlockSpec(shape, lambda i, c: (0,) * len(shape))
    return pl.pallas_call(
        _sgu_kernel,
        grid=(b, s // rows),
        in_specs=[blk(SGU_W, SU_OFF), blk(SGU_W, SV_OFF), const2((SGU_GROUPS, CHUNK, CHUNK)),
                  const2((CHUNK, SGU_W)), const2((1, SGU_W)), const2((1, SGU_W))],
        out_specs=pl.BlockSpec((None, rows, SGU_W), lambda i, c: (i, c, 0)),
        out_shape=jax.ShapeDtypeStruct((b, s, SGU_W), BF16),
        compiler_params=_params("parallel", "parallel"),
        name="sgu",
    )(proj3, proj3, w_tril, bias_tok, norm_g, norm_b)


def _outproj_kernel(ha_ref, hb_ref, hc_ref, x_ref, wa_ref, wb_ref, wc_ref, g_ref, b_ref, *rest):
    mix = _dot(ha_ref[...], wa_ref[...]) + _dot(hb_ref[...], wb_ref[...]) + _dot(hc_ref[...], wc_ref[...])
    y = _layer_norm_rows(DN_ALPHA * x_ref[...] + mix, g_ref[...], b_ref[...])
    if len(rest) == 1:
        rest[0][...] = y
        return
    wh_ref, wl_ref, rb_ref, o_ref, route_ref, route_t_ref = rest
    o_ref[...] = y
    route = _route_rows(y, wh_ref[...], wl_ref[...], rb_ref[...])
    route_ref[...] = route
    route_t_ref[...] = route.T[0:8, :]


def _outproj_ln(ha, hb, hc, x2d, wa, wb, wc, g, b, router=None):
    n = x2d.shape[0]
    rows = lambda w: pl.BlockSpec((ROW_TILE, w), lambda i: (i, 0))
    const = lambda shape: pl.BlockSpec(shape, lambda i: (0, 0))
    in_specs = [rows(MLSTM_W), rows(ATTN_W), rows(SGU_W), rows(D_MODEL),
                const((MLSTM_W, D_MODEL)), const((ATTN_W, D_MODEL)), const((SGU_W, D_MODEL)),
                const((1, D_MODEL)), const((1, D_MODEL))]
    out_specs = [rows(D_MODEL)]
    out_shape = [jax.ShapeDtypeStruct((n, D_MODEL), F32)]
    operands = (ha, hb, hc, x2d, wa, wb, wc, g, b)
    if router is not None:
        in_specs += [const((D_MODEL, LANES)), const((D_MODEL, LANES)), const((1, LANES))]
        out_specs += [rows(LANES), pl.BlockSpec((8, ROW_TILE), lambda i: (0, i))]
        out_shape += [jax.ShapeDtypeStruct((n, LANES), F32), jax.ShapeDtypeStruct((8, n), F32)]
        operands += tuple(router)
    out = pl.pallas_call(
        _outproj_kernel,
        grid=(n // ROW_TILE,),
        in_specs=in_specs,
        out_specs=out_specs,
        out_shape=out_shape,
        compiler_params=_params("parallel"),
        name="outproj_ln" if router is None else "outproj_ln_route",
    )(*operands)
    return out[0] if router is None else tuple(out)


def _ffn_kernel(x_ref, wg_ref, wu_ref, wd_ref, g_ref, b_ref, o_ref, xb_ref, wgu_ref):
    j = pl.program_id(1)
    last = pl.num_programs(1) - 1
    n_sub = FFN_ROWS // FFN_SUB
    wgu_ref[:, :FF_TILE] = wg_ref[...]
    wgu_ref[:, FF_TILE:] = wu_ref[...]

    def sub_rows(m):
        return pl.ds(pl.multiple_of(m * FFN_SUB, FFN_SUB), FFN_SUB)

    def ffn_tile(m):
        gu = _dot(xb_ref[sub_rows(m), :], wgu_ref[...])
        hidden = (jax.nn.silu(gu[:, :FF_TILE]) * gu[:, FF_TILE:]).astype(BF16)
        return _dot(hidden, wd_ref[...])

    @pl.when(j == 0)
    def _():
        def body(m, carry):
            xb_ref[sub_rows(m), :] = x_ref[sub_rows(m), :].astype(BF16)
            o_ref[sub_rows(m), :] = ffn_tile(m)
            return carry

        lax.fori_loop(0, n_sub, body, 0)

    @pl.when(jnp.logical_and(j > 0, j < last))
    def _():
        def body(m, carry):
            o_ref[sub_rows(m), :] += ffn_tile(m)
            return carry

        lax.fori_loop(0, n_sub, body, 0)

    @pl.when(j == last)
    def _():
        def body(m, carry):
            rows = sub_rows(m)
            ff = o_ref[rows, :] + ffn_tile(m)
            o_ref[rows, :] = _layer_norm_rows(DN_ALPHA * x_ref[rows, :] + ff, g_ref[...], b_ref[...])
            return carry

        lax.fori_loop(0, n_sub, body, 0)


def _ffn_ln(x2d, wg, wu, wd, g, b):
    n = x2d.shape[0]
    d_ff = wd.shape[0]
    assert d_ff // FF_TILE >= 2
    tm = FFN_ROWS
    return pl.pallas_call(
        _ffn_kernel,
        grid=(n // tm, d_ff // FF_TILE),
        in_specs=[pl.BlockSpec((tm, D_MODEL), lambda i, j: (i, 0)),
                  pl.BlockSpec((D_MODEL, FF_TILE), lambda i, j: (0, j)),
                  pl.BlockSpec((D_MODEL, FF_TILE), lambda i, j: (0, j)),
                  pl.BlockSpec((FF_TILE, D_MODEL), lambda i, j: (j, 0)),
                  pl.BlockSpec((1, D_MODEL), lambda i, j: (0, 0)),
                  pl.BlockSpec((1, D_MODEL), lambda i, j: (0, 0))],
        out_specs=pl.BlockSpec((tm, D_MODEL), lambda i, j: (i, 0)),
        out_shape=jax.ShapeDtypeStruct((n, D_MODEL), F32),
        scratch_shapes=[pltpu.VMEM((tm, D_MODEL), BF16), pltpu.VMEM((D_MODEL, 2 * FF_TILE), BF16)],
        compiler_params=_params("parallel", "arbitrary"),
        name="ffn_ln",
    )(x2d, wg, wu, wd, g, b)


def _route_rows(x, w_hi, w_lo, bias):
    xh = x.astype(BF16)
    xl = (x - xh.astype(F32)).astype(BF16)
    logits = _dot(xh, w_hi) + _dot(xl, w_hi) + _dot(xh, w_lo) + bias
    lane = lax.broadcasted_iota(jnp.int32, logits.shape, 1)
    logits = jnp.where(lane < N_EXPERTS, logits, -jnp.inf)
    m1 = jnp.max(logits, axis=-1, keepdims=True)
    i1 = jnp.min(jnp.where(logits == m1, lane, LANES), axis=-1, keepdims=True)
    rest = jnp.where(lane == i1, -jnp.inf, logits)
    m2 = jnp.max(rest, axis=-1, keepdims=True)
    i2 = jnp.min(jnp.where(rest == m2, lane, LANES), axis=-1, keepdims=True)
    e2 = jnp.exp(m2 - m1)
    g1 = 1.0 / (1.0 + e2)
    g2 = e2 / (1.0 + e2)
    return jnp.where(lane == 0, g1, jnp.where(lane == 1, g2, jnp.where(
        lane == 2, i1.astype(F32), jnp.where(lane == 3, i2.astype(F32), 0.0))))


def _dispatch_kernel(slot0_ref, slot1_ref, ends_ref, x_ref, o_hbm, stage_ref, zero_ref, sem, zero_sem):
    t = pl.program_id(0)
    last = pl.num_programs(0) - 1
    s = t % 2

    def wait_tile(ss):
        for _ in range(TOP_K):
            pltpu.make_async_copy(zero_ref, o_hbm.at[pl.ds(0, MOE_SUB), :], sem.at[ss]).wait()

    @pl.when(t == 0)
    def _():
        zero_ref[...] = jnp.zeros_like(zero_ref)

        def zero_sub_tile(m, carry):
            dst = o_hbm.at[pl.ds(pl.multiple_of(m * MOE_SUB, MOE_SUB), MOE_SUB), :]
            cp = pltpu.make_async_copy(zero_ref, dst, zero_sem)
            cp.start()
            cp.wait()
            return carry

        for e in range(N_EXPERTS):
            @pl.when(ends_ref[e] >= MOE_SUB)
            def _():
                zero_sub_tile(ends_ref[e] // MOE_SUB - 1, 0)

        lax.fori_loop(ends_ref[N_EXPERTS - 1] // MOE_SUB, o_hbm.shape[0] // MOE_SUB, zero_sub_tile, 0)

    @pl.when(t >= 2)
    def _():
        wait_tile(s)

    stage_ref[s] = x_ref[...].reshape(MOE_SUB // SUBLANES, SUBLANES, D_MODEL)

    def issue(i, carry):
        for u in range(SUBLANES):
            for slot_ref in (slot0_ref, slot1_ref):
                pltpu.make_async_copy(stage_ref.at[s, i, pl.ds(u, 1), :],
                                      o_hbm.at[pl.ds(slot_ref[t * MOE_SUB + i * SUBLANES + u], 1), :], sem.at[s]).start()
        return carry

    lax.fori_loop(0, MOE_SUB // SUBLANES, issue, 0)

    @pl.when(t == last)
    def _():
        wait_tile(s)
        wait_tile(1 - s)


def _dispatch_rows(x2d, slots, ends, n_rows):
    n = x2d.shape[0]
    assert n // MOE_SUB >= 2
    return pl.pallas_call(
        _dispatch_kernel,
        grid_spec=pltpu.PrefetchScalarGridSpec(
            num_scalar_prefetch=3,
            grid=(n // MOE_SUB,),
            in_specs=[pl.BlockSpec((MOE_SUB, D_MODEL), lambda t, s0, s1, en: (t, 0))],
            out_specs=pl.BlockSpec(memory_space=pl.ANY),
            scratch_shapes=[pltpu.VMEM((2, MOE_SUB // SUBLANES, SUBLANES, D_MODEL), F32), pltpu.VMEM((MOE_SUB, D_MODEL), F32),
                            pltpu.SemaphoreType.DMA((2,)), pltpu.SemaphoreType.DMA(())]),
        out_shape=jax.ShapeDtypeStruct((n_rows, D_MODEL), F32),
        compiler_params=_params("arbitrary"),
        name="moe_dispatch",
    )(slots[0], slots[1], ends, x2d)


def _moe_kernel(exp_ref, row0_ref, nsub_ref, tail_ref, x_hbm, wg_ref, wu_ref, wd_ref, y_hbm,
                xb_ref, acc_ref, wgu_ref, wdb_ref, stage_ref, in_sem, out_sem):
    v = pl.program_id(0)
    j = pl.program_id(1)
    last = pl.num_programs(1) - 1
    n_sub = nsub_ref[v]
    row0 = row0_ref[v]

    def sub_rows(m):
        return pl.ds(pl.multiple_of(m * MOE_SUB, MOE_SUB), MOE_SUB)

    def hbm_rows(m):
        return pl.ds(pl.multiple_of(row0 + m * MOE_SUB, MOE_SUB), MOE_SUB)

    @pl.when(jnp.logical_and(v == 0, j == 0))
    def _():
        stage_ref[0] = jnp.zeros((MOE_SUB, D_MODEL), F32)

        def zero_sub_tile(m, carry):
            cp = pltpu.make_async_copy(stage_ref.at[0], y_hbm.at[sub_rows(m), :], out_sem)
            cp.start()
            cp.wait()
            return carry

        lax.fori_loop(tail_ref[0], y_hbm.shape[0] // MOE_SUB, zero_sub_tile, 0)

    @pl.when(n_sub > 0)
    def _():
        wgu_ref[:, :MOE_FF_TILE] = wg_ref[...].astype(BF16)
        wgu_ref[:, MOE_FF_TILE:] = wu_ref[...].astype(BF16)
        wdb_ref[...] = wd_ref[...].astype(BF16)

        def ffn_rows(rows):
            gu = _dot(xb_ref[rows, :], wgu_ref[...])
            hidden = (jax.nn.silu(gu[:, :MOE_FF_TILE]) * gu[:, MOE_FF_TILE:]).astype(BF16)
            return _dot(hidden, wdb_ref[...])

        def ffn_tile(m):
            return ffn_rows(sub_rows(m))

        def accumulate(after_sub_tile):
            def pair(k, carry):
                rows = pl.ds(pl.multiple_of(2 * k * MOE_SUB, 2 * MOE_SUB), 2 * MOE_SUB)
                acc_ref[rows, :] += ffn_rows(rows)
                after_sub_tile(2 * k)
                after_sub_tile(2 * k + 1)
                return carry

            lax.fori_loop(0, n_sub // 2, pair, 0)

            @pl.when(n_sub % 2 == 1)
            def _():
                acc_ref[sub_rows(n_sub - 1), :] += ffn_tile(n_sub - 1)
                after_sub_tile(n_sub - 1)

        def y_copy(m):
            return pltpu.make_async_copy(acc_ref.at[sub_rows(m), :], y_hbm.at[hbm_rows(m), :], out_sem)

        @pl.when(j == 0)
        def _():
            def x_copy(m):
                return pltpu.make_async_copy(x_hbm.at[hbm_rows(m), :], stage_ref.at[m % 2], in_sem.at[m % 2])

            x_copy(0).start()

            def body(m, carry):
                @pl.when(m + 1 < n_sub)
                def _():
                    x_copy(m + 1).start()

                x_copy(m).wait()
                xb_ref[sub_rows(m), :] = stage_ref[m % 2].astype(BF16)
                acc_ref[sub_rows(m), :] = ffn_tile(m)
                return carry

            lax.fori_loop(0, n_sub, body, 0)

        @pl.when(jnp.logical_and(j > 0, j < last))
        def _():
            accumulate(lambda m: None)

        @pl.when(j == last)
        def _():
            def drain(m, carry):
                y_copy(m).wait()
                return carry

            accumulate(lambda m: y_copy(m).start())
            lax.fori_loop(0, n_sub, drain, 0)


def _moe_grouped(xs, wg, wu, wd, visit_exp, visit_row0, visit_nsub, tail_sub):
    n_rows = xs.shape[0]
    n_visits = visit_exp.shape[0]
    d_ff = wg.shape[2]
    assert d_ff // MOE_FF_TILE >= 2
    rows = MOE_VISIT_SUBS * MOE_SUB
    w_in = lambda v, j, e, r, ns, tl: (e[v], 0, j)
    w_out = lambda v, j, e, r, ns, tl: (e[v], j, 0)
    return pl.pallas_call(
        _moe_kernel,
        grid_spec=pltpu.PrefetchScalarGridSpec(
            num_scalar_prefetch=4,
            grid=(n_visits, d_ff // MOE_FF_TILE),
            in_specs=[pl.BlockSpec(memory_space=pl.ANY),
                      pl.BlockSpec((None, D_MODEL, MOE_FF_TILE), w_in),
                      pl.BlockSpec((None, D_MODEL, MOE_FF_TILE), w_in),
                      pl.BlockSpec((None, MOE_FF_TILE, D_MODEL), w_out)],
            out_specs=pl.BlockSpec(memory_space=pl.ANY),
            scratch_shapes=[pltpu.VMEM((rows, D_MODEL), BF16), pltpu.VMEM((rows, D_MODEL), F32),
                            pltpu.VMEM((D_MODEL, 2 * MOE_FF_TILE), BF16),
                            pltpu.VMEM((MOE_FF_TILE, D_MODEL), BF16),
                            pltpu.VMEM((2, MOE_SUB, D_MODEL), F32),
                            pltpu.SemaphoreType.DMA((2,)), pltpu.SemaphoreType.DMA(())]),
        out_shape=jax.ShapeDtypeStruct((n_rows, D_MODEL), F32),
        compiler_params=_params("arbitrary", "arbitrary"),
        name="moe_grouped",
    )(visit_exp, visit_row0, visit_nsub, tail_sub, xs, wg, wu, wd)


def _combine_kernel(slot0_ref, slot1_ref, x_ref, gate_ref, y_hbm, y_grouped_hbm, g_ref, b_ref, o_ref, buf_ref, sem):
    t = pl.program_id(0)
    s = t % 2

    def issue_tile(tt, ss):
        def issue(i, carry):
            for u in range(SUBLANES):
                for k, slot_ref in enumerate((slot0_ref, slot1_ref)):
                    pltpu.make_async_copy(y_hbm.at[pl.ds(slot_ref[tt * MOE_SUB + i * SUBLANES + u], 1), :],
                                          buf_ref.at[ss, k, i, pl.ds(u, 1), :], sem.at[ss]).start()
            return carry

        lax.fori_loop(0, MOE_SUB // SUBLANES, issue, 0)

    @pl.when(t == 0)
    def _():
        issue_tile(0, 0)

    @pl.when(t + 1 < pl.num_programs(0))
    def _():
        issue_tile(t + 1, 1 - s)

    for k in range(TOP_K):
        pltpu.make_async_copy(y_grouped_hbm.at[pl.ds(0, MOE_SUB // SUBLANES)], buf_ref.at[s, k], sem.at[s]).wait()
    gate = gate_ref[...]
    y0 = buf_ref[s, 0].reshape(MOE_SUB, D_MODEL)
    y1 = buf_ref[s, 1].reshape(MOE_SUB, D_MODEL)
    ff = gate[:, 0:1] * y0 + gate[:, 1:2] * y1
    o_ref[...] = _layer_norm_rows(DN_ALPHA * x_ref[...] + ff, g_ref[...], b_ref[...])


def _combine_ln(x2d, route, slots, ys, g, b):
    n = x2d.shape[0]
    return pl.pallas_call(
        _combine_kernel,
        grid_spec=pltpu.PrefetchScalarGridSpec(
            num_scalar_prefetch=2,
            grid=(n // MOE_SUB,),
            in_specs=[pl.BlockSpec((MOE_SUB, D_MODEL), lambda t, s0, s1: (t, 0)),
                      pl.BlockSpec((MOE_SUB, LANES), lambda t, s0, s1: (t, 0)),
                      pl.BlockSpec(memory_space=pl.ANY), pl.BlockSpec(memory_space=pl.ANY),
                      pl.BlockSpec((1, D_MODEL), lambda t, s0, s1: (0, 0)),
                      pl.BlockSpec((1, D_MODEL), lambda t, s0, s1: (0, 0))],
            out_specs=pl.BlockSpec((MOE_SUB, D_MODEL), lambda t, s0, s1: (t, 0)),
            scratch_shapes=[pltpu.VMEM((2, TOP_K, MOE_SUB // SUBLANES, SUBLANES, D_MODEL), F32),
                            pltpu.SemaphoreType.DMA((2,))]),
        out_shape=jax.ShapeDtypeStruct((n, D_MODEL), F32),
        compiler_params=_params("arbitrary"),
        name="moe_combine_ln",
    )(slots[0], slots[1], x2d, route, ys, ys.reshape(ys.shape[0] // SUBLANES, SUBLANES, D_MODEL), g, b)


def _routing_tables(route_t, n):
    idx = route_t[2:4].astype(jnp.int32)
    expert = jnp.arange(N_EXPERTS, dtype=jnp.int32)[:, None]
    chosen = [idx[k][None, :] == expert for k in range(TOP_K)]
    onehot = jnp.logical_or(chosen[0], chosen[1]).astype(jnp.int32)
    rank = jnp.cumsum(onehot, axis=1) - onehot
    counts = jnp.sum(onehot, axis=1)
    padded = ((counts + MOE_SUB - 1) // MOE_SUB) * MOE_SUB
    ends = jnp.cumsum(padded)
    starts = ends - padded
    place = starts[:, None] + rank
    slot = [jnp.sum(jnp.where(chosen[k], place, 0), axis=0).astype(jnp.int32) for k in range(TOP_K)]

    n_rows = -(-(n * TOP_K + N_EXPERTS * (MOE_SUB - 1)) // MOE_SUB) * MOE_SUB

    visit_rows = MOE_VISIT_SUBS * MOE_SUB
    max_chunks = -(-n_rows // visit_rows)
    chunk = jnp.arange(max_chunks, dtype=jnp.int32)[None, :]
    left = padded[:, None] - chunk * visit_rows
    valid = (left > 0).reshape(-1)
    n_visits = n_rows // visit_rows + N_EXPERTS
    order = jnp.argsort(jnp.logical_not(valid), stable=True)[:n_visits]
    n_valid = jnp.sum(valid.astype(jnp.int32))
    live = jnp.arange(n_visits) < n_valid
    order = jnp.where(live, order, order[jnp.maximum(n_valid - 1, 0)])
    v_exp = (order // max_chunks).astype(jnp.int32)
    v_chunk = (order % max_chunks).astype(jnp.int32)
    v_row0 = jnp.where(live, starts[v_exp] + v_chunk * visit_rows, 0).astype(jnp.int32)
    v_nsub = jnp.where(live, jnp.minimum(left.reshape(-1)[order], visit_rows) // MOE_SUB, 0).astype(jnp.int32)
    tail_sub = (ends[N_EXPERTS - 1:] // MOE_SUB).astype(jnp.int32)
    return slot, ends.astype(jnp.int32), n_rows, (v_exp, v_row0, v_nsub, tail_sub)


def _router_operands(w_router, b_router):
    w_pad = jnp.zeros((D_MODEL, LANES), F32).at[:, :N_EXPERTS].set(w_router)
    w_hi = w_pad.astype(BF16)
    w_lo = (w_pad - w_hi.astype(F32)).astype(BF16)
    b_pad = jnp.zeros((1, LANES), F32).at[0, :N_EXPERTS].set(b_router)
    return w_hi, w_lo, b_pad


def _moe_ln(x2d, route, route_t, wg, wu, wd, g, b):
    n = x2d.shape[0]
    slots, ends, n_rows, visits = _routing_tables(route_t, n)
    xs = _dispatch_rows(x2d, slots, ends, n_rows)
    ys = _moe_grouped(xs, wg, wu, wd, *visits)
    return _combine_ln(x2d, route, slots, ys, g, b)


_SRC_GATES = 4 * MLSTM_W
_SRC_AQ = _SRC_GATES + 2 * MLSTM_HEADS
_SRC_AK = _SRC_AQ + ATTN_W
_IN_PROJ_MOVES = (
    ((0, 0, 4 * MLSTM_W),)
    + tuple((_SRC_AQ + h * HEAD_DIM, AQ_OFF + i * HEAD_DIM, HEAD_DIM) for i, h in enumerate(ATTN_HEAD_ORDER))
    + ((_SRC_AK, AK_OFF, ATTN_KV_W), (_SRC_AK + ATTN_KV_W, AV_OFF, ATTN_KV_W),
       (_SRC_AK + 2 * ATTN_KV_W, SU_OFF, SGU_W), (_SRC_AK + 2 * ATTN_KV_W + SGU_W, SV_OFF, SGU_W)))
D_PROJ = _SRC_AK + 2 * ATTN_KV_W + 2 * SGU_W


def _relayout_columns(src, dst_dtype):
    lead = src.shape[:-1]
    out = jnp.zeros(lead + (D_PROJ_PAD,), dst_dtype)
    for s, d, w in _IN_PROJ_MOVES + ((_SRC_GATES, GATE_OFF, 2 * MLSTM_HEADS),):
        out = out.at[..., d:d + w].set(src[..., s:s + w].astype(dst_dtype))
    return out


def _w_layout_kernel(w_ref, o_ref):
    for s, d, w in _IN_PROJ_MOVES:
        o_ref[:, d:d + w] = w_ref[:, s:s + w].astype(o_ref.dtype)
    pad = jnp.zeros((w_ref.shape[0], LANES - 2 * MLSTM_HEADS), F32)
    gates = jnp.concatenate([w_ref[:, _SRC_GATES:_SRC_GATES + 2 * MLSTM_HEADS], pad], axis=1)
    o_ref[:, GATE_OFF:GATE_OFF + LANES] = gates.astype(o_ref.dtype)


def _layout_in_proj(w_in, b_in):
    rows = D_MODEL // 4

    def one_layer(layer):
        return pl.pallas_call(
            _w_layout_kernel,
            grid=(D_MODEL // rows,),
            in_specs=[pl.BlockSpec((None, rows, D_PROJ), lambda i: (layer, i, 0))],
            out_specs=pl.BlockSpec((rows, D_PROJ_PAD), lambda i: (i, 0)),
            out_shape=jax.ShapeDtypeStruct((D_MODEL, D_PROJ_PAD), BF16),
            compiler_params=_params("parallel"),
            name="w_in_layout",
        )(w_in)

    return [one_layer(layer) for layer in range(w_in.shape[0])], _relayout_columns(b_in, F32)[:, None, :]


def _rope_tables(positions):
    inv_freq = ROPE_THETA ** (-jnp.arange(0, ROPE_DIM, 2, dtype=F32) / ROPE_DIM)
    ang = inv_freq[None, :, None] * positions.astype(F32)[:, None, :]
    cos, sin = jnp.cos(ang), jnp.sin(ang)
    ones = jnp.ones((ang.shape[0], HEAD_DIM - ROPE_DIM, ang.shape[2]), F32)
    cos_head = jnp.concatenate([cos, cos, ones], 1)
    sin_head = jnp.concatenate([-sin, sin, 0.0 * ones], 1)
    cos_t = jnp.concatenate([cos_head, cos_head], 1)
    sin_t = jnp.concatenate([sin_head, sin_head], 1)
    return _to_token_major(cos_t, sin_t)


def _table_transpose_kernel(cos_ref, sin_ref, cos_out, sin_out):
    cos_out[...] = cos_ref[...].T
    sin_out[...] = sin_ref[...].T


def _to_token_major(cos_t, sin_t):
    b, lanes, s = cos_t.shape
    src = pl.BlockSpec((None, lanes, ROW_TILE), lambda i, c: (i, 0, c))
    dst = pl.BlockSpec((None, ROW_TILE, lanes), lambda i, c: (i, c, 0))
    shape = jax.ShapeDtypeStruct((b, s, lanes), F32)
    return pl.pallas_call(
        _table_transpose_kernel,
        grid=(b, s // ROW_TILE),
        in_specs=[src, src],
        out_specs=[dst, dst],
        out_shape=[shape, shape],
        compiler_params=_params("parallel", "parallel"),
        name="rope_tables",
    )(cos_t, sin_t)


def kernel(x, positions, w_in, b_in, conv_w, mlstm_norm_g, attn_sinks, sgu_w_s, sgu_b_s, sgu_norm_g, sgu_norm_b, w_out, ln1_g, ln1_b, ln2_g, ln2_b, ffn_w_gate, ffn_w_up, ffn_w_down, moe_w_router, moe_b_router, moe_w_gate, moe_w_up, moe_w_down):
    bsz, seq, _ = x.shape
    n = bsz * seq
    cos_t, sin_t = _rope_tables(positions)
    tril = jnp.tril(jnp.ones((CHUNK, CHUNK), bool))
    x2d = x.reshape(n, D_MODEL)
    w_p, b_p = _layout_in_proj(w_in, b_in)
    for layer in range(DEPTH):
        proj3 = _inproj(x2d, w_p[layer], b_p[layer]).reshape(bsz, seq, D_PROJ_PAD)
        h_a = _mlstm(proj3, conv_w[layer], mlstm_norm_g[layer][None, :])
        h_b = _swa(proj3, cos_t, sin_t, attn_sinks[layer])
        w_tril = jnp.where(tril, sgu_w_s[layer], 0.0).astype(BF16)
        bias_tok = jnp.repeat(sgu_b_s[layer].T, HEAD_DIM, axis=1)
        h_c = _sgu(proj3, w_tril, bias_tok, sgu_norm_g[layer][None, :], sgu_norm_b[layer][None, :])
        wo = w_out[layer]
        wa = wo[:MLSTM_W].astype(BF16)
        wb = wo[MLSTM_W:MLSTM_W + ATTN_W].reshape(ATTN_Q_HEADS, HEAD_DIM, D_MODEL)[np.array(ATTN_HEAD_ORDER)]
        wb = wb.reshape(ATTN_W, D_MODEL).astype(BF16)
        wc = wo[MLSTM_W + ATTN_W:].astype(BF16)
        j = layer // 2
        dense = layer % 2 == 0
        mixed = _outproj_ln(h_a.reshape(n, MLSTM_W), h_b.reshape(n, ATTN_W), h_c.reshape(n, SGU_W), x2d,
                            wa, wb, wc, ln1_g[layer][None, :], ln1_b[layer][None, :],
                            router=None if dense else _router_operands(moe_w_router[j], moe_b_router[j]))
        g2, b2 = ln2_g[layer][None, :], ln2_b[layer][None, :]
        if dense:
            x2d = _ffn_ln(mixed, ffn_w_gate[j].astype(BF16), ffn_w_up[j].astype(BF16),
                          ffn_w_down[j].astype(BF16), g2, b2)
        else:
            x2d, route, route_t = mixed
            x2d = _moe_ln(x2d, route, route_t, moe_w_gate[j], moe_w_up[j], moe_w_down[j], g2, b2)
    return x2d.reshape(bsz, seq, D_MODEL)
```

```python
import jax
import jax.numpy as jnp
import numpy as np
from jax import lax
from jax.experimental import pallas as pl
from jax.experimental.pallas import tpu as pltpu

F32 = jnp.float32
BF16 = jnp.bfloat16

D_MODEL = 1024
HEAD_DIM = 64
LANES = 128
SUBLANES = 8
MLSTM_HEADS = 6
ATTN_Q_HEADS = 6
ATTN_KV_HEADS = 2
SGU_GROUPS = 4
MLSTM_W = MLSTM_HEADS * HEAD_DIM
ATTN_W = ATTN_Q_HEADS * HEAD_DIM
ATTN_KV_W = ATTN_KV_HEADS * HEAD_DIM
SGU_W = SGU_GROUPS * HEAD_DIM
CHUNK = 128
CONV_WIDTH = 4
ROPE_DIM = HEAD_DIM // 4
ROPE_THETA = 500000.0
N_EXPERTS = 8
TOP_K = 2
DEPTH = 2
DN_ALPHA = (2.0 * DEPTH) ** 0.25
LN_EPS = 1e-5

QK_OFF, V_OFF, O_OFF, AQ_OFF = 0, 768, 1152, 1536
GATE_OFF, AK_OFF, AV_OFF, SU_OFF, SV_OFF = 1920, 2048, 2176, 2304, 2560
D_PROJ_PAD = 2816
ATTN_HEAD_ORDER = (0, 3, 1, 4, 2, 5)

VMEM_LIMIT = 56 * 1024 * 1024

SGU_CHUNKS_PER_STEP = 4
MLSTM_CHUNKS_PER_STEP = 8
SWA_BLOCKS_PER_STEP = 2
ROW_TILE = 512
FFN_ROWS = 2048
FFN_SUB = 1024
FF_TILE = 256
MOE_SUB = 512
MOE_VISIT_SUBS = 9
MOE_FF_TILE = 512


def _params(*sem):
    return pltpu.CompilerParams(dimension_semantics=sem, vmem_limit_bytes=VMEM_LIMIT)


def _lane_lo(shape):
    return lax.broadcasted_iota(jnp.int32, shape, len(shape) - 1) < HEAD_DIM


def _layer_norm_rows(z, g, b):
    mu = jnp.mean(z, axis=-1, keepdims=True)
    zc = z - mu
    var = jnp.mean(zc * zc, axis=-1, keepdims=True)
    return zc * lax.rsqrt(var + LN_EPS) * g + b


def _half_layer_norm(x, lo):
    inv = 1.0 / HEAD_DIM
    s_lo = jnp.sum(jnp.where(lo, x, 0.0), axis=-1, keepdims=True)
    s_all = jnp.sum(x, axis=-1, keepdims=True)
    mu = jnp.where(lo, s_lo, s_all - s_lo) * inv
    xc = x - mu
    sq = xc * xc
    q_lo = jnp.sum(jnp.where(lo, sq, 0.0), axis=-1, keepdims=True)
    q_all = jnp.sum(sq, axis=-1, keepdims=True)
    var = jnp.where(lo, q_lo, q_all - q_lo) * inv
    return xc * lax.rsqrt(var + LN_EPS)


def _split3(a):
    h1 = a.astype(BF16)
    r1 = a - h1.astype(F32)
    h2 = r1.astype(BF16)
    r2 = r1 - h2.astype(F32)
    return h1, h2, r2.astype(BF16)


def _split3_masked(a):
    def top(x):
        bits = lax.bitcast_convert_type(x, jnp.int32) & jnp.int32(-65536)
        return lax.bitcast_convert_type(bits, F32)

    h1 = top(a)
    r1 = a - h1
    h2 = top(r1)
    return h1.astype(BF16), h2.astype(BF16), (r1 - h2).astype(BF16)


def _dot(a, b):
    return jnp.dot(a, b, preferred_element_type=F32)


def _inproj_kernel(x_ref, w_ref, b_ref, o_ref):
    o_ref[...] = _dot(x_ref[...].astype(BF16), w_ref[...]) + b_ref[...]


def _inproj(x2d, w, b):
    n = x2d.shape[0]
    return pl.pallas_call(
        _inproj_kernel,
        grid=(n // ROW_TILE,),
        in_specs=[pl.BlockSpec((ROW_TILE, D_MODEL), lambda i: (i, 0)),
                  pl.BlockSpec((D_MODEL, D_PROJ_PAD), lambda i: (0, 0)),
                  pl.BlockSpec((1, D_PROJ_PAD), lambda i: (0, 0))],
        out_specs=pl.BlockSpec((ROW_TILE, D_PROJ_PAD), lambda i: (i, 0)),
        out_shape=jax.ShapeDtypeStruct((n, D_PROJ_PAD), F32),
        compiler_params=_params("parallel"),
        name="inproj",
    )(x2d, w, b)


def _mlstm_kernel(qk_ref, v_ref, o_ref, gate_ref, cw_ref, g_ref, out_ref, buf_ref, ct_ref, m_ref):
    rows = MLSTM_CHUNKS_PER_STEP * CHUNK

    @pl.when(pl.program_id(1) == 0)
    def _():
        buf_ref[0:8, :] = jnp.zeros((8, 2 * MLSTM_W), F32)
        ct_ref[...] = jnp.zeros_like(ct_ref)
        m_ref[...] = jnp.zeros_like(m_ref)

    buf_ref[8:8 + rows, :] = qk_ref[...]
    ct = [ct_ref[p] for p in range(MLSTM_HEADS // 2)]
    m = [m_ref[h:h + 1, 0:1] for h in range(MLSTM_HEADS)]
    for i in range(MLSTM_CHUNKS_PER_STEP):
        blk = pl.ds(i * CHUNK, CHUNK)
        ct, m = _mlstm_chunk(i * CHUNK, buf_ref, v_ref.at[blk], o_ref.at[blk], gate_ref.at[blk], cw_ref, g_ref,
                             out_ref.at[blk], ct, m)
    buf_ref[0:8, :] = qk_ref[rows - 8:rows, :]
    for p in range(MLSTM_HEADS // 2):
        ct_ref[p] = ct[p]
    for h in range(MLSTM_HEADS):
        m_ref[h:h + 1, :] = jnp.broadcast_to(m[h], (1, LANES))


def _mlstm_chunk(row0, buf_ref, v_ref, o_ref, gate_ref, cw_ref, g_ref, out_ref, ct_in, m_in):
    cw = cw_ref[...]
    base = row0 + 8 - (CONV_WIDTH - 1)
    acc = cw[0:1, :] * buf_ref[base:base + CHUNK, :]
    for j in range(1, CONV_WIDTH):
        acc = acc + cw[j:j + 1, :] * buf_ref[base + j:base + j + CHUNK, :]
    qk = jax.nn.silu(acc)

    gates = gate_ref[...]
    lf = jax.nn.log_sigmoid(gates)
    row = lax.broadcasted_iota(jnp.int32, (CHUNK, CHUNK), 0)
    col = lax.broadcasted_iota(jnp.int32, (CHUNK, CHUNK), 1)
    causal = col <= row
    tri = causal.astype(BF16)
    l1, l2, l3 = _split3(lf)
    cum = _dot(tri, l1) + _dot(tri, l2) + _dot(tri, l3)
    gates_t = gates.T
    cum_t = cum.T
    cum_parts = _split3_masked(cum)

    lo = _lane_lo((CHUNK, LANES))
    ct_out = []
    m_out = []
    for p in range(MLSTM_HEADS // 2):
        sl = slice(p * LANES, (p + 1) * LANES)
        q_slab = qk[:, sl] * (HEAD_DIM ** -0.5)
        k_slab = qk[:, MLSTM_W + p * LANES:MLSTM_W + (p + 1) * LANES]
        kt_slab = k_slab.T
        v_slab = v_ref[:, sl]
        ct_pair = ct_in[p]
        ct_pair_b = ct_pair.astype(BF16)
        halves = []
        new_ct = []
        for half in range(2):
            h = 2 * p + half
            sel = lo if half == 0 else jnp.logical_not(lo)
            li_row = gates_t[h:h + 1, :]
            bc_row = cum_t[MLSTM_HEADS + h:MLSTM_HEADS + h + 1, :]
            pick = (row == MLSTM_HEADS + h).astype(BF16)
            bc_col = _dot(cum_parts[0], pick) + _dot(cum_parts[1], pick) + _dot(cum_parts[2], pick)
            b_tot = bc_row[:, CHUNK - 1:CHUNK]
            m_prev = m_in[h]

            dmat = jnp.where(causal, bc_col - bc_row + li_row, -jnp.inf)
            m_inter = bc_col + m_prev
            m_row = jnp.maximum(m_inter, jnp.max(dmat, axis=-1, keepdims=True))
            q_m = jnp.where(sel, q_slab, 0.0).astype(BF16)
            s = _dot(q_m, kt_slab.astype(BF16))
            pmat = (s * jnp.exp(dmat - m_row)).astype(BF16)
            w_inter = jnp.exp(m_inter - m_row)
            v_aug = jnp.where(sel, v_slab, 1.0).astype(BF16)
            halves.append((_dot(pmat, v_aug) + w_inter * _dot(q_m, ct_pair_b), jnp.exp(-m_row)))

            a_row = b_tot - bc_row + li_row
            m_loc = jnp.max(a_row, axis=-1, keepdims=True)
            w_row = jnp.exp(a_row - m_loc)
            kt_h = kt_slab[half * HEAD_DIM:(half + 1) * HEAD_DIM, :]
            ct_loc = _dot((kt_h * w_row).astype(BF16), v_aug)
            m_new = jnp.maximum(b_tot + m_prev, m_loc)
            s_old = jnp.exp(b_tot + m_prev - m_new)
            s_loc = jnp.exp(m_loc - m_new)
            new_ct.append(s_old * ct_pair[half * HEAD_DIM:(half + 1) * HEAD_DIM, :] + s_loc * ct_loc)
            m_out.append(m_new)

        ct_out.append(jnp.concatenate(new_ct, axis=0))
        (r_even, stab_even), (r_odd, stab_odd) = halves
        num = jnp.where(lo, r_even, r_odd)
        den = pltpu.roll(jnp.where(lo, r_odd, r_even), HEAD_DIM, 1)
        hh = num / jnp.maximum(jnp.abs(den), jnp.where(lo, stab_even, stab_odd))
        hn = _half_layer_norm(hh, lo) * g_ref[:, sl]
        out_ref[:, sl] = (hn * jax.nn.sigmoid(o_ref[:, sl])).astype(out_ref.dtype)
    return ct_out, m_out


def _mlstm(proj3, conv_w, norm_g):
    b, s, _ = proj3.shape
    rows = MLSTM_CHUNKS_PER_STEP * CHUNK
    assert s % rows == 0
    blk = lambda w, off: pl.BlockSpec((None, rows, w), lambda i, c: (i, c, off // w))
    return pl.pallas_call(
        _mlstm_kernel,
        grid=(b, s // rows),
        in_specs=[blk(2 * MLSTM_W, QK_OFF), blk(MLSTM_W, V_OFF), blk(MLSTM_W, O_OFF), blk(LANES, GATE_OFF),
                  pl.BlockSpec((CONV_WIDTH, 2 * MLSTM_W), lambda i, c: (0, 0)),
                  pl.BlockSpec((1, MLSTM_W), lambda i, c: (0, 0))],
        out_specs=pl.BlockSpec((None, rows, MLSTM_W), lambda i, c: (i, c, 0)),
        out_shape=jax.ShapeDtypeStruct((b, s, MLSTM_W), BF16),
        scratch_shapes=[pltpu.VMEM((8 + rows, 2 * MLSTM_W), F32),
                        pltpu.VMEM((MLSTM_HEADS // 2, LANES, LANES), F32),
                        pltpu.VMEM((8, LANES), F32)],
        compiler_params=_params("parallel", "arbitrary"),
        name="mlstm",
    )(proj3, proj3, proj3, proj3, conv_w, norm_g)


def _rope(x, cos_t, sin_t, first):
    return x * cos_t + jnp.where(first, pltpu.roll(x, LANES - ROPE_DIM // 2, 1),
                                 pltpu.roll(x, ROPE_DIM // 2, 1)) * sin_t


def _swa_kernel(sink_ref, q_ref, k_ref, v_ref, cos_ref, sin_ref, out_ref, kt_ref, vv_ref):
    first_step = pl.program_id(1) == 0

    @pl.when(first_step)
    def _():
        kt_ref[...] = jnp.zeros_like(kt_ref)
        vv_ref[...] = jnp.zeros_like(vv_ref)

    kt_prev = kt_ref[...]
    v_prev = vv_ref[...]
    for i in range(SWA_BLOCKS_PER_STEP):
        rows = pl.ds(i * CHUNK, CHUNK)
        hide_previous = jnp.where(first_step, 2 * CHUNK, 0) if i == 0 else 0
        kt_prev, v_prev = _swa_block(hide_previous, sink_ref, q_ref.at[rows], k_ref.at[rows], v_ref.at[rows],
                                     cos_ref.at[rows], sin_ref.at[rows], out_ref.at[rows], kt_prev, v_prev)
    kt_ref[...] = kt_prev
    vv_ref[...] = v_prev


def _swa_block(first_block_shift, sink_ref, q_ref, k_ref, v_ref, cos_ref, sin_ref, out_ref, kt_prev, v_prev):
    cos_t = cos_ref[...]
    sin_t = sin_ref[...]
    lane = lax.broadcasted_iota(jnp.int32, (CHUNK, LANES), 1)
    lo = lane < HEAD_DIM
    first = (lane % HEAD_DIM) < ROPE_DIM // 2

    kt_cur = _rope(k_ref[...], cos_t, sin_t, first).T.astype(BF16)
    v_cur = v_ref[...].astype(BF16)
    kt = jnp.concatenate([kt_prev, kt_cur], axis=1)
    vv = jnp.concatenate([v_prev, v_cur], axis=0)

    row = lax.broadcasted_iota(jnp.int32, (CHUNK, 2 * CHUNK), 0)
    col = lax.broadcasted_iota(jnp.int32, (CHUNK, 2 * CHUNK), 1)
    visible = jnp.logical_or(jnp.logical_and(col < CHUNK, col > row + first_block_shift),
                             jnp.logical_and(col >= CHUNK, col - CHUNK <= row))

    for j in range(ATTN_Q_HEADS // 2):
        sl = slice(j * LANES, (j + 1) * LANES)
        q_slab = _rope(q_ref[:, sl], cos_t, sin_t, first) * (HEAD_DIM ** -0.5)
        outs = []
        for half in range(2):
            sel = lo if half == 0 else jnp.logical_not(lo)
            sink = sink_ref[ATTN_HEAD_ORDER[2 * j + half]]
            q_m = jnp.where(sel, q_slab, 0.0).astype(BF16)
            s = jnp.where(visible, _dot(q_m, kt), -jnp.inf)
            mx = jnp.maximum(jnp.max(s, axis=-1, keepdims=True), sink)
            pexp = jnp.exp(s - mx)
            denom = jnp.sum(pexp, axis=-1, keepdims=True) + jnp.exp(sink - mx)
            outs.append(_dot(pexp.astype(BF16), vv) / denom)
        out_ref[:, sl] = jnp.where(lo, outs[0], outs[1]).astype(out_ref.dtype)

    return kt_cur, v_cur


def _swa(proj3, cos_t, sin_t, sinks):
    b, s, _ = proj3.shape
    rows = SWA_BLOCKS_PER_STEP * CHUNK
    assert s % rows == 0
    blk = lambda w, off: pl.BlockSpec((None, rows, w), lambda i, c: (i, c, off // w))
    tab =pl.BlockSpec((None, rows, LANES), lambda i, c: (i, c, 0))
    return pl.pallas_call(
        _swa_kernel,
        grid=(b, s // rows),
        in_specs=[pl.BlockSpec(memory_space=pltpu.SMEM),
                  blk(ATTN_W, AQ_OFF), blk(ATTN_KV_W, AK_OFF), blk(ATTN_KV_W, AV_OFF), tab, tab],
        out_specs=pl.BlockSpec((None, rows, ATTN_W), lambda i, c: (i, c, 0)),
        out_shape=jax.ShapeDtypeStruct((b, s, ATTN_W), BF16),
        scratch_shapes=[pltpu.VMEM((LANES, CHUNK), BF16), pltpu.VMEM((CHUNK, LANES), BF16)],
        compiler_params=_params("parallel", "arbitrary"),
        name="swa",
    )(sinks, proj3, proj3, proj3, cos_t, sin_t)


def _sgu_kernel(u_ref, v_ref, w_ref, bias_ref, g_ref, b_ref, out_ref):
    lo = _lane_lo((CHUNK, LANES))
    for c in range(SGU_CHUNKS_PER_STEP):
        rows = slice(c * CHUNK, (c + 1) * CHUNK)
        for j in range(SGU_GROUPS // 2):
            sl = slice(j * LANES, (j + 1) * LANES)
            u = jax.nn.gelu(u_ref[rows, sl])
            v = jax.nn.gelu(v_ref[rows, sl])
            vn = (_half_layer_norm(v, lo) * g_ref[:, sl] + b_ref[:, sl]).astype(BF16)
            mixed = jnp.where(lo, _dot(w_ref[2 * j], vn), _dot(w_ref[2 * j + 1], vn)) + bias_ref[:, sl]
            out_ref[rows, sl] = (u * mixed).astype(out_ref.dtype)


def _sgu(proj3, w_tril, bias_tok, norm_g, norm_b):
    b, s, _ = proj3.shape
    rows = SGU_CHUNKS_PER_STEP * CHUNK
    assert s % rows == 0
    blk = lambda w, off: pl.BlockSpec((None, rows, w), lambda i, c: (i, c, off // w))
    const2 = lambda shape: pl.BlockSpec(shape, lambda i, c: (0,) * len(shape))
    return pl.pallas_call(
        _sgu_kernel,
        grid=(b, s // rows),
        in_specs=[blk(SGU_W, SU_OFF), blk(SGU_W, SV_OFF), const2((SGU_GROUPS, CHUNK, CHUNK)),
                  const2((CHUNK, SGU_W)), const2((1, SGU_W)), const2((1, SGU_W))],
        out_specs=pl.BlockSpec((None, rows, SGU_W), lambda i, c: (i, c, 0)),
        out_shape=jax.ShapeDtypeStruct((b, s, SGU_W), BF16),
        compiler_params=_params("parallel", "parallel"),
        name="sgu",
    )(proj3, proj3, w_tril, bias_tok, norm_g, norm_b)


def _outproj_kernel(ha_ref, hb_ref, hc_ref, x_ref, w_ref, g_ref, b_ref, *rest):
    cat_ref = rest[-1]
    cat_ref[:, 0:MLSTM_W] = ha_ref[...]
    cat_ref[:, MLSTM_W:MLSTM_W + ATTN_W] = hb_ref[...]
    cat_ref[:, MLSTM_W + ATTN_W:] = hc_ref[...]
    mix = _dot(cat_ref[...], w_ref[...])
    y = _layer_norm_rows(DN_ALPHA * x_ref[...] + mix, g_ref[...], b_ref[...])
    rest = rest[:-1]
    if len(rest) == 1:
        rest[0][...] = y
        return
    w_split_ref, rb_ref, o_ref, route_ref, route_t_ref = rest
    o_ref[...] = y
    route_ref[...], route_t_ref[...] = _route_rows(y, w_split_ref[...], rb_ref[...])


def _outproj_ln(ha, hb, hc, x2d, w, g, b, router=None):
    n = x2d.shape[0]
    rows = lambda w: pl.BlockSpec((ROW_TILE, w), lambda i: (i, 0))
    const = lambda shape: pl.BlockSpec(shape, lambda i: (0, 0))
    in_specs = [rows(MLSTM_W), rows(ATTN_W), rows(SGU_W), rows(D_MODEL),
                const((D_MODEL, D_MODEL)), const((1, D_MODEL)), const((1, D_MODEL))]
    out_specs = [rows(D_MODEL)]
    out_shape = [jax.ShapeDtypeStruct((n, D_MODEL), F32)]
    operands = (ha, hb, hc, x2d, w, g, b)
    if router is not None:
        in_specs += [const((D_MODEL, 2 * LANES)), const((1, LANES))]
        out_specs += [rows(LANES), pl.BlockSpec((N_EXPERTS, ROW_TILE), lambda i: (0, i))]
        out_shape += [jax.ShapeDtypeStruct((n, LANES), F32), jax.ShapeDtypeStruct((N_EXPERTS, n), F32)]
        operands += tuple(router)
    out = pl.pallas_call(
        _outproj_kernel,
        grid=(n // ROW_TILE,),
        in_specs=in_specs,
        out_specs=out_specs,
        out_shape=out_shape,
        scratch_shapes=[pltpu.VMEM((ROW_TILE, D_MODEL), BF16)],
        compiler_params=_params("parallel"),
        name="outproj_ln" if router is None else "outproj_ln_route",
    )(*operands)
    return out[0] if router is None else tuple(out)


def _ffn_kernel(x_ref, wg_ref, wu_ref, wd_ref, g_ref, b_ref, o_ref, xb_ref, wgu_ref):
    j = pl.program_id(1)
    last = pl.num_programs(1) - 1
    n_sub = FFN_ROWS // FFN_SUB
    wgu_ref[:, :FF_TILE] = wg_ref[...]
    wgu_ref[:, FF_TILE:] = wu_ref[...]

    def sub_rows(m):
        return pl.ds(pl.multiple_of(m * FFN_SUB, FFN_SUB), FFN_SUB)

    def ffn_tile(m):
        gu = _dot(xb_ref[sub_rows(m), :], wgu_ref[...])
        hidden = (jax.nn.silu(gu[:, :FF_TILE]) * gu[:, FF_TILE:]).astype(BF16)
        return _dot(hidden, wd_ref[...])

    @pl.when(j == 0)
    def _():
        def body(m, carry):
            xb_ref[sub_rows(m), :] = x_ref[sub_rows(m), :].astype(BF16)
            o_ref[sub_rows(m), :] = ffn_tile(m)
            return carry

        lax.fori_loop(0, n_sub, body, 0)

    @pl.when(jnp.logical_and(j > 0, j < last))
    def _():
        def body(m, carry):
            o_ref[sub_rows(m), :] += ffn_tile(m)
            return carry

        lax.fori_loop(0, n_sub, body, 0)

    @pl.when(j == last)
    def _():
        def body(m, carry):
            rows = sub_rows(m)
            ff = o_ref[rows, :] + ffn_tile(m)
            o_ref[rows, :] = _layer_norm_rows(DN_ALPHA * x_ref[rows, :] + ff, g_ref[...], b_ref[...])
            return carry

        lax.fori_loop(0, n_sub, body, 0)


def _ffn_ln(x2d, wg, wu, wd, g, b):
    n = x2d.shape[0]
    d_ff = wd.shape[0]
    assert d_ff // FF_TILE >= 2
    tm = FFN_ROWS
    return pl.pallas_call(
        _ffn_kernel,
        grid=(n // tm, d_ff // FF_TILE),
        in_specs=[pl.BlockSpec((tm, D_MODEL), lambda i, j: (i, 0)),
                  pl.BlockSpec((D_MODEL, FF_TILE), lambda i, j: (0, j)),
                  pl.BlockSpec((D_MODEL, FF_TILE), lambda i, j: (0, j)),
                  pl.BlockSpec((FF_TILE, D_MODEL), lambda i, j: (j, 0)),
                  pl.BlockSpec((1, D_MODEL), lambda i, j: (0, 0)),
                  pl.BlockSpec((1, D_MODEL), lambda i, j: (0, 0))],
        out_specs=pl.BlockSpec((tm, D_MODEL), lambda i, j: (i, 0)),
        out_shape=jax.ShapeDtypeStruct((n, D_MODEL), F32),
        scratch_shapes=[pltpu.VMEM((tm, D_MODEL), BF16), pltpu.VMEM((D_MODEL, 2 * FF_TILE), BF16)],
        compiler_params=_params("parallel", "arbitrary"),
        name="ffn_ln",
    )(x2d, wg, wu, wd, g, b)


def _route_rows(x, w_split, bias):
    n = x.shape[0]
    xh = x.astype(BF16)
    xl = (x - xh.astype(F32)).astype(BF16)
    prod = _dot(jnp.concatenate([xh, xl], axis=0), w_split)
    logits = prod[:n, :LANES] + prod[:n, LANES:] + prod[n:, :LANES] + prod[n:, LANES:] + bias
    lt = logits.T[0:N_EXPERTS, :]
    expert = lax.broadcasted_iota(jnp.int32, lt.shape, 0)
    m1 = jnp.max(lt, axis=0, keepdims=True)
    i1 = jnp.min(jnp.where(lt == m1, expert, N_EXPERTS), axis=0, keepdims=True)
    rest = jnp.where(expert == i1, -jnp.inf, lt)
    m2 = jnp.max(rest, axis=0, keepdims=True)
    i2 = jnp.min(jnp.where(rest == m2, expert, N_EXPERTS), axis=0, keepdims=True)
    e2 = jnp.exp(m2 - m1)
    g1 = 1.0 / (1.0 + e2)
    g2 = e2 / (1.0 + e2)
    route_t = jnp.where(expert == 0, g1, jnp.where(expert == 1, g2, jnp.where(
        expert == 2, i1.astype(F32), jnp.where(expert == 3, i2.astype(F32), 0.0))))
    padded = jnp.concatenate([route_t, jnp.zeros((LANES - N_EXPERTS, n), F32)], axis=0)
    return padded.T, route_t


def _dispatch_kernel(slot0_ref, slot1_ref, ends_ref, x_ref, o_hbm, stage_ref, zero_ref, sem, zero_sem):
    t = pl.program_id(0)
    last = pl.num_programs(0) - 1
    s = t % 2

    def wait_tile(ss):
        for _ in range(TOP_K):
            pltpu.make_async_copy(zero_ref, o_hbm.at[pl.ds(0, MOE_SUB), :], sem.at[ss]).wait()

    @pl.when(t == 0)
    def _():
        zero_ref[...] = jnp.zeros_like(zero_ref)

        def zero_sub_tile(m, carry):
            dst = o_hbm.at[pl.ds(pl.multiple_of(m * MOE_SUB, MOE_SUB), MOE_SUB), :]
            cp = pltpu.make_async_copy(zero_ref, dst, zero_sem)
            cp.start()
            cp.wait()
            return carry

        for e in range(N_EXPERTS):
            @pl.when(ends_ref[e] >= MOE_SUB)
            def _():
                zero_sub_tile(ends_ref[e] // MOE_SUB - 1, 0)

        lax.fori_loop(ends_ref[N_EXPERTS - 1] // MOE_SUB, o_hbm.shape[0] // MOE_SUB, zero_sub_tile, 0)

    @pl.when(t >= 2)
    def _():
        wait_tile(s)

    stage_ref[s] = x_ref[...].reshape(MOE_SUB // SUBLANES, SUBLANES, D_MODEL)

    def issue(i, carry):
        for u in range(SUBLANES):
            for slot_ref in (slot0_ref, slot1_ref):
                pltpu.make_async_copy(stage_ref.at[s, i, pl.ds(u, 1), :],
                                      o_hbm.at[pl.ds(slot_ref[t * MOE_SUB + i * SUBLANES + u], 1), :], sem.at[s]).start()
        return carry

    lax.fori_loop(0, MOE_SUB // SUBLANES, issue, 0)

    @pl.when(t == last)
    def _():
        wait_tile(s)
        wait_tile(1 - s)


def _dispatch_rows(x2d, slots, ends, n_rows):
    n = x2d.shape[0]
    assert n // MOE_SUB >= 2
    return pl.pallas_call(
        _dispatch_kernel,
        grid_spec=pltpu.PrefetchScalarGridSpec(
            num_scalar_prefetch=3,
            grid=(n // MOE_SUB,),
            in_specs=[pl.BlockSpec((MOE_SUB, D_MODEL), lambda t, s0, s1, en: (t, 0))],
            out_specs=pl.BlockSpec(memory_space=pl.ANY),
            scratch_shapes=[pltpu.VMEM((2, MOE_SUB // SUBLANES, SUBLANES, D_MODEL), F32), pltpu.VMEM((MOE_SUB, D_MODEL), F32),
                            pltpu.SemaphoreType.DMA((2,)), pltpu.SemaphoreType.DMA(())]),
        out_shape=jax.ShapeDtypeStruct((n_rows, D_MODEL), F32),
        compiler_params=_params("arbitrary"),
        name="moe_dispatch",
    )(slots[0], slots[1], ends, x2d)


def _moe_kernel(exp_ref, row0_ref, nsub_ref, tail_ref, x_hbm, wg_ref, wu_ref, wd_ref, y_hbm,
                xb_ref, acc_ref, wgu_ref, wdb_ref, stage_ref, in_sem, out_sem):
    v = pl.program_id(0)
    j = pl.program_id(1)
    last = pl.num_programs(1) - 1
    n_sub = nsub_ref[v]
    row0 = row0_ref[v]

    def sub_rows(m):
        return pl.ds(pl.multiple_of(m * MOE_SUB, MOE_SUB), MOE_SUB)

    def hbm_rows(m):
        return pl.ds(pl.multiple_of(row0 + m * MOE_SUB, MOE_SUB), MOE_SUB)

    @pl.when(jnp.logical_and(v == 0, j == 0))
    def _():
        stage_ref[0] = jnp.zeros((MOE_SUB, D_MODEL), F32)

        def zero_sub_tile(m, carry):
            cp = pltpu.make_async_copy(stage_ref.at[0], y_hbm.at[sub_rows(m), :], out_sem)
            cp.start()
            cp.wait()
            return carry

        lax.fori_loop(tail_ref[0], y_hbm.shape[0] // MOE_SUB, zero_sub_tile, 0)

    @pl.when(n_sub > 0)
    def _():
        wgu_ref[:, :MOE_FF_TILE] = wg_ref[...].astype(BF16)
        wgu_ref[:, MOE_FF_TILE:] = wu_ref[...].astype(BF16)
        wdb_ref[...] = wd_ref[...].astype(BF16)

        def ffn_rows(rows):
            gu = _dot(xb_ref[rows, :], wgu_ref[...])
            hidden = (jax.nn.silu(gu[:, :MOE_FF_TILE]) * gu[:, MOE_FF_TILE:]).astype(BF16)
            return _dot(hidden, wdb_ref[...])

        def ffn_tile(m):
            return ffn_rows(sub_rows(m))

        def accumulate(after_sub_tile):
            def pair(k, carry):
                rows = pl.ds(pl.multiple_of(2 * k * MOE_SUB, 2 * MOE_SUB), 2 * MOE_SUB)
                acc_ref[rows, :] += ffn_rows(rows)
                after_sub_tile(2 * k)
                after_sub_tile(2 * k + 1)
                return carry

            lax.fori_loop(0, n_sub // 2, pair, 0)

            @pl.when(n_sub % 2 == 1)
            def _():
                acc_ref[sub_rows(n_sub - 1), :] += ffn_tile(n_sub - 1)
                after_sub_tile(n_sub - 1)

        def y_copy(m):
            return pltpu.make_async_copy(acc_ref.at[sub_rows(m), :], y_hbm.at[hbm_rows(m), :], out_sem)

        @pl.when(j == 0)
        def _():
            def x_copy(m):
                return pltpu.make_async_copy(x_hbm.at[hbm_rows(m), :], stage_ref.at[m % 2], in_sem.at[m % 2])

            x_copy(0).start()

            def body(m, carry):
                @pl.when(m + 1 < n_sub)
                def _():
                    x_copy(m + 1).start()

                x_copy(m).wait()
                xb_ref[sub_rows(m), :] = stage_ref[m % 2].astype(BF16)
                acc_ref[sub_rows(m), :] = ffn_tile(m)
                return carry

            lax.fori_loop(0, n_sub, body, 0)

        @pl.when(jnp.logical_and(j > 0, j < last))
        def _():
            accumulate(lambda m: None)

        @pl.when(j == last)
        def _():
            def drain(m, carry):
                y_copy(m).wait()
                return carry

            accumulate(lambda m: y_copy(m).start())
            lax.fori_loop(0, n_sub, drain, 0)


def _moe_grouped(xs, wg, wu, wd, visit_exp, visit_row0, visit_nsub, tail_sub):
    n_rows = xs.shape[0]
    n_visits = visit_exp.shape[0]
    d_ff = wg.shape[2]
    assert d_ff // MOE_FF_TILE >= 2
    rows = MOE_VISIT_SUBS * MOE_SUB
    w_in = lambda v, j, e, r, ns, tl: (e[v], 0, j)
    w_out = lambda v, j, e, r, ns, tl: (e[v], j, 0)
    return pl.pallas_call(
        _moe_kernel,
        grid_spec=pltpu.PrefetchScalarGridSpec(
            num_scalar_prefetch=4,
            grid=(n_visits, d_ff // MOE_FF_TILE),
            in_specs=[pl.BlockSpec(memory_space=pl.ANY),
                      pl.BlockSpec((None, D_MODEL, MOE_FF_TILE), w_in),
                      pl.BlockSpec((None, D_MODEL, MOE_FF_TILE), w_in),
                      pl.BlockSpec((None, MOE_FF_TILE, D_MODEL), w_out)],
            out_specs=pl.BlockSpec(memory_space=pl.ANY),
            scratch_shapes=[pltpu.VMEM((rows, D_MODEL), BF16), pltpu.VMEM((rows, D_MODEL), F32),
                            pltpu.VMEM((D_MODEL, 2 * MOE_FF_TILE), BF16),
                            pltpu.VMEM((MOE_FF_TILE, D_MODEL), BF16),
                            pltpu.VMEM((2, MOE_SUB, D_MODEL), F32),
                            pltpu.SemaphoreType.DMA((2,)), pltpu.SemaphoreType.DMA(())]),
        out_shape=jax.ShapeDtypeStruct((n_rows, D_MODEL), F32),
        compiler_params=_params("arbitrary", "arbitrary"),
        name="moe_grouped",
    )(visit_exp, visit_row0, visit_nsub, tail_sub, xs, wg, wu, wd)


def _combine_kernel(slot0_ref, slot1_ref, x_ref, gate_ref, y_hbm, y_grouped_hbm, g_ref, b_ref, o_ref, buf_ref, sem):
    t = pl.program_id(0)
    s = t % 2

    def issue_tile(tt, ss):
        def issue(i, carry):
            for u in range(SUBLANES):
                for k, slot_ref in enumerate((slot0_ref, slot1_ref)):
                    pltpu.make_async_copy(y_hbm.at[pl.ds(slot_ref[tt * MOE_SUB + i * SUBLANES + u], 1), :],
                                          buf_ref.at[ss, k, i, pl.ds(u, 1), :], sem.at[ss]).start()
            return carry

        lax.fori_loop(0, MOE_SUB // SUBLANES, issue, 0)

    @pl.when(t == 0)
    def _():
        issue_tile(0, 0)

    @pl.when(t + 1 < pl.num_programs(0))
    def _():
        issue_tile(t + 1, 1 - s)

    for k in range(TOP_K):
        pltpu.make_async_copy(y_grouped_hbm.at[pl.ds(0, MOE_SUB // SUBLANES)], buf_ref.at[s, k], sem.at[s]).wait()
    gate = gate_ref[...]
    y0 = buf_ref[s, 0].reshape(MOE_SUB, D_MODEL)
    y1 = buf_ref[s, 1].reshape(MOE_SUB, D_MODEL)
    ff = gate[:, 0:1] * y0 + gate[:, 1:2] * y1
    o_ref[...] = _layer_norm_rows(DN_ALPHA * x_ref[...] + ff, g_ref[...], b_ref[...])


def _combine_ln(x2d, route, slots, ys, g, b):
    n = x2d.shape[0]
    return pl.pallas_call(
        _combine_kernel,
        grid_spec=pltpu.PrefetchScalarGridSpec(
            num_scalar_prefetch=2,
            grid=(n // MOE_SUB,),
            in_specs=[pl.BlockSpec((MOE_SUB, D_MODEL), lambda t, s0, s1: (t, 0)),
                      pl.BlockSpec((MOE_SUB, LANES), lambda t, s0, s1: (t, 0)),
                      pl.BlockSpec(memory_space=pl.ANY), pl.BlockSpec(memory_space=pl.ANY),
                      pl.BlockSpec((1, D_MODEL), lambda t, s0, s1: (0, 0)),
                      pl.BlockSpec((1, D_MODEL), lambda t, s0, s1: (0, 0))],
            out_specs=pl.BlockSpec((MOE_SUB, D_MODEL), lambda t, s0, s1: (t, 0)),
            scratch_shapes=[pltpu.VMEM((2, TOP_K, MOE_SUB // SUBLANES, SUBLANES, D_MODEL), F32),
                            pltpu.SemaphoreType.DMA((2,))]),
        out_shape=jax.ShapeDtypeStruct((n, D_MODEL), F32),
        compiler_params=_params("arbitrary"),
        name="moe_combine_ln",
    )(slots[0], slots[1], x2d, route, ys, ys.reshape(ys.shape[0] // SUBLANES, SUBLANES, D_MODEL), g, b)


def _routing_tables(route_t, n):
    idx = route_t[2:4].astype(jnp.int32)
    expert = jnp.arange(N_EXPERTS, dtype=jnp.int32)[:, None]
    chosen = [idx[k][None, :] == expert for k in range(TOP_K)]
    onehot = jnp.logical_or(chosen[0], chosen[1]).astype(jnp.int32)
    rank = jnp.cumsum(onehot, axis=1) - onehot
    counts = jnp.sum(onehot, axis=1)
    padded = ((counts + MOE_SUB - 1) // MOE_SUB) * MOE_SUB
    ends = jnp.cumsum(padded)
    starts = ends - padded
    place = starts[:, None] + rank
    slot = [jnp.sum(jnp.where(chosen[k], place, 0), axis=0).astype(jnp.int32) for k in range(TOP_K)]

    n_rows = -(-(n * TOP_K + N_EXPERTS * (MOE_SUB - 1)) // MOE_SUB) * MOE_SUB

    visit_rows = MOE_VISIT_SUBS * MOE_SUB
    max_chunks = -(-n_rows // visit_rows)
    chunk = jnp.arange(max_chunks, dtype=jnp.int32)[None, :]
    left = padded[:, None] - chunk * visit_rows
    valid = (left > 0).reshape(-1)
    n_visits = n_rows // visit_rows + N_EXPERTS
    order = jnp.argsort(jnp.logical_not(valid), stable=True)[:n_visits]
    n_valid = jnp.sum(valid.astype(jnp.int32))
    live = jnp.arange(n_visits) < n_valid
    order = jnp.where(live, order, order[jnp.maximum(n_valid - 1, 0)])
    v_exp = (order // max_chunks).astype(jnp.int32)
    v_chunk = (order % max_chunks).astype(jnp.int32)
    v_row0 = jnp.where(live, starts[v_exp] + v_chunk * visit_rows, 0).astype(jnp.int32)
    v_nsub = jnp.where(live, jnp.minimum(left.reshape(-1)[order], visit_rows) // MOE_SUB, 0).astype(jnp.int32)
    tail_sub = (ends[N_EXPERTS - 1:] // MOE_SUB).astype(jnp.int32)
    return slot, ends.astype(jnp.int32), n_rows, (v_exp, v_row0, v_nsub, tail_sub)


def _router_operands(w_router, b_router):
    w_pad = jnp.zeros((D_MODEL, LANES), F32).at[:, :N_EXPERTS].set(w_router)
    w_hi = w_pad.astype(BF16)
    w_lo = (w_pad - w_hi.astype(F32)).astype(BF16)
    b_pad = jnp.zeros((1, LANES), F32).at[0, :N_EXPERTS].set(b_router)
    return jnp.concatenate([w_hi, w_lo], axis=1), b_pad


def _moe_ln(x2d, route, route_t, wg, wu, wd, g, b):
    n = x2d.shape[0]
    slots, ends, n_rows, visits = _routing_tables(route_t, n)
    xs = _dispatch_rows(x2d, slots, ends, n_rows)
    ys = _moe_grouped(xs, wg, wu, wd, *visits)
    return _combine_ln(x2d, route, slots, ys, g, b)


_SRC_GATES = 4 * MLSTM_W
_SRC_AQ = _SRC_GATES + 2 * MLSTM_HEADS
_SRC_AK = _SRC_AQ + ATTN_W
_IN_PROJ_MOVES = (
    ((0, 0, 4 * MLSTM_W),)
    + tuple((_SRC_AQ + h * HEAD_DIM, AQ_OFF + i * HEAD_DIM, HEAD_DIM) for i, h in enumerate(ATTN_HEAD_ORDER))
    + ((_SRC_AK, AK_OFF, ATTN_KV_W), (_SRC_AK + ATTN_KV_W, AV_OFF, ATTN_KV_W),
       (_SRC_AK + 2 * ATTN_KV_W, SU_OFF, SGU_W), (_SRC_AK + 2 * ATTN_KV_W + SGU_W, SV_OFF, SGU_W)))
D_PROJ = _SRC_AK + 2 * ATTN_KV_W + 2 * SGU_W


def _relayout_columns(src, dst_dtype):
    lead = src.shape[:-1]
    out = jnp.zeros(lead + (D_PROJ_PAD,), dst_dtype)
    for s, d, w in _IN_PROJ_MOVES + ((_SRC_GATES, GATE_OFF, 2 * MLSTM_HEADS),):
        out = out.at[..., d:d + w].set(src[..., s:s + w].astype(dst_dtype))
    return out


def _w_layout_kernel(w_ref, o_ref):
    for s, d, w in _IN_PROJ_MOVES:
        o_ref[:, d:d + w] = w_ref[:, s:s + w].astype(o_ref.dtype)
    pad = jnp.zeros((w_ref.shape[0], LANES - 2 * MLSTM_HEADS), F32)
    gates = jnp.concatenate([w_ref[:, _SRC_GATES:_SRC_GATES + 2 * MLSTM_HEADS], pad], axis=1)
    o_ref[:, GATE_OFF:GATE_OFF + LANES] = gates.astype(o_ref.dtype)


def _layout_in_proj(w_in, b_in):
    rows = D_MODEL // 4

    def one_layer(layer):
        return pl.pallas_call(
            _w_layout_kernel,
            grid=(D_MODEL // rows,),
            in_specs=[pl.BlockSpec((None, rows, D_PROJ), lambda i: (layer, i, 0))],
            out_specs=pl.BlockSpec((rows, D_PROJ_PAD), lambda i: (i, 0)),
            out_shape=jax.ShapeDtypeStruct((D_MODEL, D_PROJ_PAD), BF16),
            compiler_params=_params("parallel"),
            name="w_in_layout",
        )(w_in)

    return [one_layer(layer) for layer in range(w_in.shape[0])], _relayout_columns(b_in, F32)[:, None, :]


def _rope_tables(positions):
    inv_freq = ROPE_THETA ** (-jnp.arange(0, ROPE_DIM, 2, dtype=F32) / ROPE_DIM)
    ang = inv_freq[None, :, None] * positions.astype(F32)[:, None, :]
    cos, sin = jnp.cos(ang), jnp.sin(ang)
    ones = jnp.ones((ang.shape[0], HEAD_DIM - ROPE_DIM, ang.shape[2]), F32)
    cos_head = jnp.concatenate([cos, cos, ones], 1)
    sin_head = jnp.concatenate([-sin, sin, 0.0 * ones], 1)
    cos_t = jnp.concatenate([cos_head, cos_head], 1)
    sin_t = jnp.concatenate([sin_head, sin_head], 1)
    return _to_token_major(cos_t, sin_t)


def _table_transpose_kernel(cos_ref, sin_ref, cos_out, sin_out):
    cos_out[...] = cos_ref[...].T
    sin_out[...] = sin_ref[...].T


def _to_token_major(cos_t, sin_t):
    b, lanes, s = cos_t.shape
    src = pl.BlockSpec((None, lanes, ROW_TILE), lambda i, c: (i, 0, c))
    dst = pl.BlockSpec((None, ROW_TILE, lanes), lambda i, c: (i, c, 0))
    shape = jax.ShapeDtypeStruct((b, s, lanes), F32)
    return pl.pallas_call(
        _table_transpose_kernel,
        grid=(b, s // ROW_TILE),
        in_specs=[src, src],
        out_specs=[dst, dst],
        out_shape=[shape, shape],
        compiler_params=_params("parallel", "parallel"),
        name="rope_tables",
    )(cos_t, sin_t)


def kernel(x, positions, w_in, b_in, conv_w, mlstm_norm_g, attn_sinks, sgu_w_s, sgu_b_s, sgu_norm_g, sgu_norm_b, w_out, ln1_g, ln1_b, ln2_g, ln2_b, ffn_w_gate, ffn_w_up, ffn_w_down, moe_w_router, moe_b_router, moe_w_gate, moe_w_up, moe_w_down):
    bsz, seq, _ = x.shape
    n = bsz * seq
    cos_t, sin_t = _rope_tables(positions)
    tril = jnp.tril(jnp.ones((CHUNK, CHUNK), bool))
    x2d = x.reshape(n, D_MODEL)
    w_p, b_p = _layout_in_proj(w_in, b_in)
    for layer in range(DEPTH):
        proj3 = _inproj(x2d, w_p[layer], b_p[layer]).reshape(bsz, seq, D_PROJ_PAD)
        h_a = _mlstm(proj3, conv_w[layer], mlstm_norm_g[layer][None, :])
        h_b = _swa(proj3, cos_t, sin_t, attn_sinks[layer])
        w_tril = jnp.where(tril, sgu_w_s[layer], 0.0).astype(BF16)
        bias_tok = jnp.repeat(sgu_b_s[layer].T, HEAD_DIM, axis=1)
        h_c = _sgu(proj3, w_tril, bias_tok, sgu_norm_g[layer][None, :], sgu_norm_b[layer][None, :])
        wo = w_out[layer]
        wb = wo[MLSTM_W:MLSTM_W + ATTN_W].reshape(ATTN_Q_HEADS, HEAD_DIM, D_MODEL)[np.array(ATTN_HEAD_ORDER)]
        wo = jnp.concatenate([wo[:MLSTM_W], wb.reshape(ATTN_W, D_MODEL), wo[MLSTM_W + ATTN_W:]], 0).astype(BF16)
        j = layer // 2
        dense = layer % 2 == 0
        mixed = _outproj_ln(h_a.reshape(n, MLSTM_W), h_b.reshape(n, ATTN_W), h_c.reshape(n, SGU_W), x2d,
                            wo, ln1_g[layer][None, :], ln1_b[layer][None, :],
                            router=None if dense else _router_operands(moe_w_router[j], moe_b_router[j]))
        g2, b2 = ln2_g[layer][None, :], ln2_b[layer][None, :]
        if dense:
            x2d = _ffn_ln(mixed, ffn_w_gate[j].astype(BF16), ffn_w_up[j].astype(BF16),
                          ffn_w_down[j].astype(BF16), g2, b2)
        else:
            x2d, route, route_t = mixed
            x2d = _moe_ln(x2d, route, route_t, moe_w_gate[j], moe_w_up[j], moe_w_down[j], g2, b2)
    return x2d.reshape(bsz, seq, D_MODEL)
```

```python
import jax
import jax.numpy as jnp
import numpy as np
from jax import lax
from jax.experimental import pallas as pl
from jax.experimental.pallas import tpu as pltpu

F32 = jnp.float32
BF16 = jnp.bfloat16

D_MODEL = 1024
HEAD_DIM = 64
LANES = 128
SUBLANES = 8
MLSTM_HEADS = 6
ATTN_Q_HEADS = 6
ATTN_KV_HEADS = 2
SGU_GROUPS = 4
MLSTM_W = MLSTM_HEADS * HEAD_DIM
ATTN_W = ATTN_Q_HEADS * HEAD_DIM
ATTN_KV_W = ATTN_KV_HEADS * HEAD_DIM
SGU_W = SGU_GROUPS * HEAD_DIM
CHUNK = 128
CONV_WIDTH = 4
ROPE_DIM = HEAD_DIM // 4
ROPE_THETA = 500000.0
N_EXPERTS = 8
TOP_K = 2
DEPTH = 2
DN_ALPHA = (2.0 * DEPTH) ** 0.25
LN_EPS = 1e-5

QK_OFF, V_OFF, O_OFF, AQ_OFF = 0, 768, 1152, 1536
GATE_OFF, AK_OFF, AV_OFF, SU_OFF, SV_OFF = 1920, 2048, 2176, 2304, 2560
D_PROJ_PAD = 2816
ATTN_HEAD_ORDER = (0, 3, 1, 4, 2, 5)

VMEM_LIMIT = 56 * 1024 * 1024

MLSTM_CHUNKS_PER_STEP = 8
SWA_BLOCKS_PER_STEP = 2
ROW_TILE = 512
FFN_ROWS = 2048
FFN_SUB = 1024
FF_TILE = 256
MOE_SUB = 512
MOE_VISIT_SUBS = 9
MOE_FF_TILE = 512


def _params(*sem):
    return pltpu.CompilerParams(dimension_semantics=sem, vmem_limit_bytes=VMEM_LIMIT)


def _lane_lo(shape):
    return lax.broadcasted_iota(jnp.int32, shape, len(shape) - 1) < HEAD_DIM


def _layer_norm_rows(z, g, b):
    mu = jnp.mean(z, axis=-1, keepdims=True)
    zc = z - mu
    var = jnp.mean(zc * zc, axis=-1, keepdims=True)
    return zc * lax.rsqrt(var + LN_EPS) * g + b


def _half_layer_norm(x, lo):
    inv = 1.0 / HEAD_DIM
    s_lo = jnp.sum(jnp.where(lo, x, 0.0), axis=-1, keepdims=True)
    s_all = jnp.sum(x, axis=-1, keepdims=True)
    mu = jnp.where(lo, s_lo, s_all - s_lo) * inv
    xc = x - mu
    sq = xc * xc
    q_lo = jnp.sum(jnp.where(lo, sq, 0.0), axis=-1, keepdims=True)
    q_all = jnp.sum(sq, axis=-1, keepdims=True)
    var = jnp.where(lo, q_lo, q_all - q_lo) * inv
    return xc * lax.rsqrt(var + LN_EPS)


def _split3(a):
    h1 = a.astype(BF16)
    r1 = a - h1.astype(F32)
    h2 = r1.astype(BF16)
    r2 = r1 - h2.astype(F32)
    return h1, h2, r2.astype(BF16)


def _split3_masked(a):
    def top(x):
        bits = lax.bitcast_convert_type(x, jnp.int32) & jnp.int32(-65536)
        return lax.bitcast_convert_type(bits, F32)

    h1 = top(a)
    r1 = a - h1
    h2 = top(r1)
    return h1.astype(BF16), h2.astype(BF16), (r1 - h2).astype(BF16)


def _dot(a, b):
    return jnp.dot(a, b, preferred_element_type=F32)


def _inproj_kernel(x_ref, w_ref, b_ref, o_ref):
    o_ref[...] = _dot(x_ref[...].astype(BF16), w_ref[...]) + b_ref[...]


def _inproj(x2d, w, b):
    n = x2d.shape[0]
    return pl.pallas_call(
        _inproj_kernel,
        grid=(n // ROW_TILE,),
        in_specs=[pl.BlockSpec((ROW_TILE, D_MODEL), lambda i: (i, 0)),
                  pl.BlockSpec((D_MODEL, D_PROJ_PAD), lambda i: (0, 0)),
                  pl.BlockSpec((1, D_PROJ_PAD), lambda i: (0, 0))],
        out_specs=pl.BlockSpec((ROW_TILE, D_PROJ_PAD), lambda i: (i, 0)),
        out_shape=jax.ShapeDtypeStruct((n, D_PROJ_PAD), F32),
        compiler_params=_params("parallel"),
        name="inproj",
    )(x2d, w, b)


def _mlstm_kernel(qk_ref, v_ref, o_ref, gate_ref, cw_ref, g_ref, out_ref, buf_ref, ct_ref, m_ref):
    rows = MLSTM_CHUNKS_PER_STEP * CHUNK

    @pl.when(pl.program_id(1) == 0)
    def _():
        buf_ref[0:8, :] = jnp.zeros((8, 2 * MLSTM_W), F32)
        ct_ref[...] = jnp.zeros_like(ct_ref)
        m_ref[...] = jnp.zeros_like(m_ref)

    buf_ref[8:8 + rows, :] = qk_ref[...]
    ct = [ct_ref[p] for p in range(MLSTM_HEADS // 2)]
    m = [m_ref[h:h + 1, 0:1] for h in range(MLSTM_HEADS)]
    for i in range(MLSTM_CHUNKS_PER_STEP):
        blk = pl.ds(i * CHUNK, CHUNK)
        ct, m = _mlstm_chunk(i * CHUNK, buf_ref, v_ref.at[blk], o_ref.at[blk], gate_ref.at[blk], cw_ref, g_ref,
                             out_ref.at[blk], ct, m)
    buf_ref[0:8, :] = qk_ref[rows - 8:rows, :]
    for p in range(MLSTM_HEADS // 2):
        ct_ref[p] = ct[p]
    for h in range(MLSTM_HEADS):
        m_ref[h:h + 1, :] = jnp.broadcast_to(m[h], (1, LANES))


def _mlstm_chunk(row0, buf_ref, v_ref, o_ref, gate_ref, cw_ref, g_ref, out_ref, ct_in, m_in):
    cw = cw_ref[...]
    base = row0 + 8 - (CONV_WIDTH - 1)
    acc = cw[0:1, :] * buf_ref[base:base + CHUNK, :]
    for j in range(1, CONV_WIDTH):
        acc = acc + cw[j:j + 1, :] * buf_ref[base + j:base + j + CHUNK, :]
    qk = jax.nn.silu(acc)

    gates = gate_ref[...]
    lf = jax.nn.log_sigmoid(gates)
    row = lax.broadcasted_iota(jnp.int32, (CHUNK, CHUNK), 0)
    col = lax.broadcasted_iota(jnp.int32, (CHUNK, CHUNK), 1)
    causal = col <= row
    tri = causal.astype(BF16)
    l1, l2, l3 = _split3(lf)
    cum = _dot(tri, l1) + _dot(tri, l2) + _dot(tri, l3)
    gates_t = gates.T
    cum_t = cum.T
    cum_parts = _split3_masked(cum)

    lo = _lane_lo((CHUNK, LANES))
    ct_out = []
    m_out = []
    for p in range(MLSTM_HEADS // 2):
        sl = slice(p * LANES, (p + 1) * LANES)
        q_slab = qk[:, sl] * (HEAD_DIM ** -0.5)
        k_slab = qk[:, MLSTM_W + p * LANES:MLSTM_W + (p + 1) * LANES]
        kt_slab = k_slab.T
        v_slab = v_ref[:, sl]
        ct_pair = ct_in[p]
        ct_pair_b = ct_pair.astype(BF16)
        halves = []
        new_ct = []
        for half in range(2):
            h = 2 * p + half
            sel = lo if half == 0 else jnp.logical_not(lo)
            li_row = gates_t[h:h + 1, :]
            bc_row = cum_t[MLSTM_HEADS + h:MLSTM_HEADS + h + 1, :]
            pick = (row == MLSTM_HEADS + h).astype(BF16)
            bc_col = _dot(cum_parts[0], pick) + _dot(cum_parts[1], pick) + _dot(cum_parts[2], pick)
            b_tot = bc_row[:, CHUNK - 1:CHUNK]
            m_prev = m_in[h]

            dmat = jnp.where(causal, bc_col - bc_row + li_row, -jnp.inf)
            m_inter = bc_col + m_prev
            m_row = jnp.maximum(m_inter, jnp.max(dmat, axis=-1, keepdims=True))
            q_m = jnp.where(sel, q_slab, 0.0).astype(BF16)
            s = _dot(q_m, kt_slab.astype(BF16))
            pmat = (s * jnp.exp(dmat - m_row)).astype(BF16)
            w_inter = jnp.exp(m_inter - m_row)
            v_aug = jnp.where(sel, v_slab, 1.0).astype(BF16)
            halves.append((_dot(pmat, v_aug) + w_inter * _dot(q_m, ct_pair_b), jnp.exp(-m_row)))

            a_row = b_tot - bc_row + li_row
            m_loc = jnp.max(a_row, axis=-1, keepdims=True)
            w_row = jnp.exp(a_row - m_loc)
            kt_h = kt_slab[half * HEAD_DIM:(half + 1) * HEAD_DIM, :]
            ct_loc = _dot((kt_h * w_row).astype(BF16), v_aug)
            m_new = jnp.maximum(b_tot + m_prev, m_loc)
            s_old = jnp.exp(b_tot + m_prev - m_new)
            s_loc = jnp.exp(m_loc - m_new)
            new_ct.append(s_old * ct_pair[half * HEAD_DIM:(half + 1) * HEAD_DIM, :] + s_loc * ct_loc)
            m_out.append(m_new)

        ct_out.append(jnp.concatenate(new_ct, axis=0))
        (r_even, stab_even), (r_odd, stab_odd) = halves
        num = jnp.where(lo, r_even, r_odd)
        den = pltpu.roll(jnp.where(lo, r_odd, r_even), HEAD_DIM, 1)
        hh = num / jnp.maximum(jnp.abs(den), jnp.where(lo, stab_even, stab_odd))
        hn = _half_layer_norm(hh, lo) * g_ref[:, sl]
        out_ref[:, sl] = (hn * jax.nn.sigmoid(o_ref[:, sl])).astype(out_ref.dtype)
    return ct_out, m_out


def _mlstm(proj3, conv_w, norm_g):
    b, s, _ = proj3.shape
    rows = MLSTM_CHUNKS_PER_STEP * CHUNK
    assert s % rows == 0
    blk = lambda w, off: pl.BlockSpec((None, rows, w), lambda i, c: (i, c, off // w))
    return pl.pallas_call(
        _mlstm_kernel,
        grid=(b, s // rows),
        in_specs=[blk(2 * MLSTM_W, QK_OFF), blk(MLSTM_W, V_OFF), blk(MLSTM_W, O_OFF), blk(LANES, GATE_OFF),
                  pl.BlockSpec((CONV_WIDTH, 2 * MLSTM_W), lambda i, c: (0, 0)),
                  pl.BlockSpec((1, MLSTM_W), lambda i, c: (0, 0))],
        out_specs=pl.BlockSpec((None, rows, MLSTM_W), lambda i, c: (i, c, 0)),
        out_shape=jax.ShapeDtypeStruct((b, s, MLSTM_W), BF16),
        scratch_shapes=[pltpu.VMEM((8 + rows, 2 * MLSTM_W), F32),
                        pltpu.VMEM((MLSTM_HEADS // 2, LANES, LANES), F32),
                        pltpu.VMEM((8, LANES), F32)],
        compiler_params=_params("parallel", "arbitrary"),
        name="mlstm",
    )(proj3, proj3, proj3, proj3, conv_w, norm_g)


def _rope(x, cos_t, sin_t, first):
    return x * cos_t + jnp.where(first, pltpu.roll(x, LANES - ROPE_DIM // 2, 1),
                                 pltpu.roll(x, ROPE_DIM // 2, 1)) * sin_t


def _swa_kernel(sink_ref, q_ref, k_ref, v_ref, cos_ref, sin_ref, su_ref, sv_ref, ws_ref, bias_ref, sg_ref, sb_ref,
                out_ref, gated_ref, kt_ref, vv_ref):
    first_step = pl.program_id(1) == 0

    @pl.when(first_step)
    def _():
        kt_ref[...] = jnp.zeros_like(kt_ref)
        vv_ref[...] = jnp.zeros_like(vv_ref)

    kt_prev = kt_ref[...]
    v_prev = vv_ref[...]
    for i in range(SWA_BLOCKS_PER_STEP):
        rows = pl.ds(i * CHUNK, CHUNK)
        hide_previous = jnp.where(first_step, 2 * CHUNK, 0) if i == 0 else 0
        kt_prev, v_prev = _swa_block(hide_previous, sink_ref, q_ref.at[rows], k_ref.at[rows], v_ref.at[rows],
                                     cos_ref.at[rows], sin_ref.at[rows], out_ref.at[rows], kt_prev, v_prev)
        _sgu_chunk(slice(i * CHUNK, (i + 1) * CHUNK), su_ref, sv_ref, ws_ref, bias_ref, sg_ref, sb_ref, gated_ref)
    kt_ref[...] = kt_prev
    vv_ref[...] = v_prev


def _swa_block(first_block_shift, sink_ref, q_ref, k_ref, v_ref, cos_ref, sin_ref, out_ref, kt_prev, v_prev):
    cos_t = cos_ref[...]
    sin_t = sin_ref[...]
    lane = lax.broadcasted_iota(jnp.int32, (CHUNK, LANES), 1)
    lo = lane < HEAD_DIM
    first = (lane % HEAD_DIM) < ROPE_DIM // 2

    kt_cur = _rope(k_ref[...], cos_t, sin_t, first).T.astype(BF16)
    v_cur = v_ref[...].astype(BF16)
    kt = jnp.concatenate([kt_prev, kt_cur], axis=1)
    vv = jnp.concatenate([v_prev, v_cur], axis=0)

    row = lax.broadcasted_iota(jnp.int32, (CHUNK, 2 * CHUNK), 0)
    col = lax.broadcasted_iota(jnp.int32, (CHUNK, 2 * CHUNK), 1)
    visible = jnp.logical_or(jnp.logical_and(col < CHUNK, col > row + first_block_shift),
                             jnp.logical_and(col >= CHUNK, col - CHUNK <= row))

    for j in range(ATTN_Q_HEADS // 2):
        sl = slice(j * LANES, (j + 1) * LANES)
        q_slab = _rope(q_ref[:, sl], cos_t, sin_t, first) * (HEAD_DIM ** -0.5)
        outs = []
        for half in range(2):
            sel = lo if half == 0 else jnp.logical_not(lo)
            sink = sink_ref[ATTN_HEAD_ORDER[2 * j + half]]
            q_m = jnp.where(sel, q_slab, 0.0).astype(BF16)
            s = jnp.where(visible, _dot(q_m, kt), -jnp.inf)
            mx = jnp.maximum(jnp.max(s, axis=-1, keepdims=True), sink)
            pexp = jnp.exp(s - mx)
            denom = jnp.sum(pexp, axis=-1, keepdims=True) + jnp.exp(sink - mx)
            outs.append(_dot(pexp.astype(BF16), vv) / denom)
        out_ref[:, sl] = jnp.where(lo, outs[0], outs[1]).astype(out_ref.dtype)

    return kt_cur, v_cur


def _swa_sgu(proj3, cos_t, sin_t, sinks, w_tril, bias_tok, norm_g, norm_b):
    b, s, _ = proj3.shape
    rows = SWA_BLOCKS_PER_STEP * CHUNK
    assert s % rows == 0
    blk = lambda w, off: pl.BlockSpec((None, rows, w), lambda i, c: (i, c, off // w))
    tab = pl.BlockSpec((None, rows, LANES), lambda i, c: (i, c, 0))
    const2 = lambda shape: pl.BlockSpec(shape, lambda i, c: (0,) * len(shape))
    out = lambda w: pl.BlockSpec((None, rows, w), lambda i, c: (i, c, 0))
    return pl.pallas_call(
        _swa_kernel,
        grid=(b, s // rows),
        in_specs=[pl.BlockSpec(memory_space=pltpu.SMEM),
                  blk(ATTN_W, AQ_OFF), blk(ATTN_KV_W, AK_OFF), blk(ATTN_KV_W, AV_OFF), tab, tab,
                  blk(SGU_W, SU_OFF), blk(SGU_W, SV_OFF), const2((SGU_GROUPS, CHUNK, CHUNK)),
                  const2((CHUNK, SGU_W)), const2((1, SGU_W)), const2((1, SGU_W))],
        out_specs=[out(ATTN_W), out(SGU_W)],
        out_shape=[jax.ShapeDtypeStruct((b, s, ATTN_W), BF16), jax.ShapeDtypeStruct((b, s, SGU_W), BF16)],
        scratch_shapes=[pltpu.VMEM((LANES, CHUNK), BF16), pltpu.VMEM((CHUNK, LANES), BF16)],
        compiler_params=_params("parallel", "arbitrary"),
        name="swa_sgu",
    )(sinks, proj3, proj3, proj3, cos_t, sin_t, proj3, proj3, w_tril, bias_tok, norm_g, norm_b)


def _sgu_chunk(rows, u_ref, v_ref, w_ref, bias_ref, g_ref, b_ref, out_ref):
    lo = _lane_lo((CHUNK, LANES))
    for j in range(SGU_GROUPS // 2):
        sl = slice(j * LANES, (j + 1) * LANES)
        u = jax.nn.gelu(u_ref[rows, sl])
        v = jax.nn.gelu(v_ref[rows, sl])
        vn = (_half_layer_norm(v, lo) * g_ref[:, sl] + b_ref[:, sl]).astype(BF16)
        mixed = jnp.where(lo, _dot(w_ref[2 * j], vn), _dot(w_ref[2 * j + 1], vn)) + bias_ref[:, sl]
        out_ref[rows, sl] = (u * mixed).astype(out_ref.dtype)


def _outproj_kernel(ha_ref, hb_ref, hc_ref, x_ref, w_ref, g_ref, b_ref, *rest):
    cat_ref = rest[-1]
    cat_ref[:, 0:MLSTM_W] = ha_ref[...]
    cat_ref[:, MLSTM_W:MLSTM_W + ATTN_W] = hb_ref[...]
    cat_ref[:, MLSTM_W + ATTN_W:] = hc_ref[...]
    mix = _dot(cat_ref[...], w_ref[...])
    y = _layer_norm_rows(DN_ALPHA * x_ref[...] + mix, g_ref[...], b_ref[...])
    rest = rest[:-1]
    if len(rest) == 1:
        rest[0][...] = y
        return
    w_split_ref, rb_ref, o_ref, route_ref, route_t_ref = rest
    o_ref[...] = y
    route_ref[...], route_t_ref[...] = _route_rows(y, w_split_ref[...], rb_ref[...])


def _outproj_ln(ha, hb, hc, x2d, w, g, b, router=None):
    n = x2d.shape[0]
    rows = lambda w: pl.BlockSpec((ROW_TILE, w), lambda i: (i, 0))
    const = lambda shape: pl.BlockSpec(shape, lambda i: (0, 0))
    in_specs = [rows(MLSTM_W), rows(ATTN_W), rows(SGU_W), rows(D_MODEL),
                const((D_MODEL, D_MODEL)), const((1, D_MODEL)), const((1, D_MODEL))]
    out_specs = [rows(D_MODEL)]
    out_shape = [jax.ShapeDtypeStruct((n, D_MODEL), F32)]
    operands = (ha, hb, hc, x2d, w, g, b)
    if router is not None:
        in_specs += [const((D_MODEL, 2 * LANES)), const((1, LANES))]
        out_specs += [rows(LANES), pl.BlockSpec((N_EXPERTS, ROW_TILE), lambda i: (0, i))]
        out_shape += [jax.ShapeDtypeStruct((n, LANES), F32), jax.ShapeDtypeStruct((N_EXPERTS, n), F32)]
        operands += tuple(router)
    out = pl.pallas_call(
        _outproj_kernel,
        grid=(n // ROW_TILE,),
        in_specs=in_specs,
        out_specs=out_specs,
        out_shape=out_shape,
        scratch_shapes=[pltpu.VMEM((ROW_TILE, D_MODEL), BF16)],
        compiler_params=_params("parallel"),
        name="outproj_ln" if router is None else "outproj_ln_route",
    )(*operands)
    return out[0] if router is None else tuple(out)


def _ffn_kernel(x_ref, wg_ref, wu_ref, wd_ref, g_ref, b_ref, o_ref, xb_ref, wgu_ref):
    j = pl.program_id(1)
    last = pl.num_programs(1) - 1
    n_sub = FFN_ROWS // FFN_SUB
    wgu_ref[:, :FF_TILE] = wg_ref[...]
    wgu_ref[:, FF_TILE:] = wu_ref[...]

    def sub_rows(m):
        return pl.ds(pl.multiple_of(m * FFN_SUB, FFN_SUB), FFN_SUB)

    def ffn_tile(m):
        gu = _dot(xb_ref[sub_rows(m), :], wgu_ref[...])
        hidden = (jax.nn.silu(gu[:, :FF_TILE]) * gu[:, FF_TILE:]).astype(BF16)
        return _dot(hidden, wd_ref[...])

    @pl.when(j == 0)
    def _():
        def body(m, carry):
            xb_ref[sub_rows(m), :] = x_ref[sub_rows(m), :].astype(BF16)
            o_ref[sub_rows(m), :] = ffn_tile(m)
            return carry

        lax.fori_loop(0, n_sub, body, 0)

    @pl.when(jnp.logical_and(j > 0, j < last))
    def _():
        def body(m, carry):
            o_ref[sub_rows(m), :] += ffn_tile(m)
            return carry

        lax.fori_loop(0, n_sub, body, 0)

    @pl.when(j == last)
    def _():
        def body(m, carry):
            rows = sub_rows(m)
            ff = o_ref[rows, :] + ffn_tile(m)
            o_ref[rows, :] = _layer_norm_rows(DN_ALPHA * x_ref[rows, :] + ff, g_ref[...], b_ref[...])
            return carry

        lax.fori_loop(0, n_sub, body, 0)


def _ffn_ln(x2d, wg, wu, wd, g, b):
    n = x2d.shape[0]
    d_ff = wd.shape[0]
    assert d_ff // FF_TILE >= 2
    tm = FFN_ROWS
    return pl.pallas_call(
        _ffn_kernel,
        grid=(n // tm, d_ff // FF_TILE),
        in_specs=[pl.BlockSpec((tm, D_MODEL), lambda i, j: (i, 0)),
                  pl.BlockSpec((D_MODEL, FF_TILE), lambda i, j: (0, j)),
                  pl.BlockSpec((D_MODEL, FF_TILE), lambda i, j: (0, j)),
                  pl.BlockSpec((FF_TILE, D_MODEL), lambda i, j: (j, 0)),
                  pl.BlockSpec((1, D_MODEL), lambda i, j: (0, 0)),
                  pl.BlockSpec((1, D_MODEL), lambda i, j: (0, 0))],
        out_specs=pl.BlockSpec((tm, D_MODEL), lambda i, j: (i, 0)),
        out_shape=jax.ShapeDtypeStruct((n, D_MODEL), F32),
        scratch_shapes=[pltpu.VMEM((tm, D_MODEL), BF16), pltpu.VMEM((D_MODEL, 2 * FF_TILE), BF16)],
        compiler_params=_params("parallel", "arbitrary"),
        name="ffn_ln",
    )(x2d, wg, wu, wd, g, b)


def _route_rows(x, w_split, bias):
    n = x.shape[0]
    xh = x.astype(BF16)
    xl = (x - xh.astype(F32)).astype(BF16)
    prod = _dot(jnp.concatenate([xh, xl], axis=0), w_split)
    logits = prod[:n, :LANES] + prod[:n, LANES:] + prod[n:, :LANES] + prod[n:, LANES:] + bias
    lt = logits.T[0:N_EXPERTS, :]
    expert = lax.broadcasted_iota(jnp.int32, lt.shape, 0)
    m1 = jnp.max(lt, axis=0, keepdims=True)
    i1 = jnp.min(jnp.where(lt == m1, expert, N_EXPERTS), axis=0, keepdims=True)
    rest = jnp.where(expert == i1, -jnp.inf, lt)
    m2 = jnp.max(rest, axis=0, keepdims=True)
    i2 = jnp.min(jnp.where(rest == m2, expert, N_EXPERTS), axis=0, keepdims=True)
    e2 = jnp.exp(m2 - m1)
    g1 = 1.0 / (1.0 + e2)
    g2 = e2 / (1.0 + e2)
    route_t = jnp.where(expert == 0, g1, jnp.where(expert == 1, g2, jnp.where(
        expert == 2, i1.astype(F32), jnp.where(expert == 3, i2.astype(F32), 0.0))))
    padded = jnp.concatenate([route_t, jnp.zeros((LANES - N_EXPERTS, n), F32)], axis=0)
    return padded.T, route_t


TOKEN_TILE = (SUBLANES, D_MODEL // SUBLANES)


def _dispatch_kernel(slot0_ref, slot1_ref, ends_ref, x_ref, o_hbm, stage_ref, zero_ref, sem, zero_sem):
    t = pl.program_id(0)
    last = pl.num_programs(0) - 1
    s = t % 2

    def wait_tile(ss):
        for _ in range(TOP_K):
            pltpu.make_async_copy(zero_ref, o_hbm.at[pl.ds(0, MOE_SUB)], sem.at[ss]).wait()

    @pl.when(t == 0)
    def _():
        zero_ref[...] = jnp.zeros_like(zero_ref)

        def zero_sub_tile(m, carry):
            dst = o_hbm.at[pl.ds(pl.multiple_of(m * MOE_SUB, MOE_SUB), MOE_SUB)]
            cp = pltpu.make_async_copy(zero_ref, dst, zero_sem)
            cp.start()
            cp.wait()
            return carry

        for e in range(N_EXPERTS):
            @pl.when(ends_ref[e] >= MOE_SUB)
            def _():
                zero_sub_tile(ends_ref[e] // MOE_SUB - 1, 0)

        lax.fori_loop(ends_ref[N_EXPERTS - 1] // MOE_SUB, o_hbm.shape[0] // MOE_SUB, zero_sub_tile, 0)

    @pl.when(t >= 2)
    def _():
        wait_tile(s)

    stage_ref[s] = x_ref[...].reshape((MOE_SUB,) + TOKEN_TILE)

    def issue(i, carry):
        for u in range(SUBLANES):
            r = i * SUBLANES + u
            for slot_ref in (slot0_ref, slot1_ref):
                pltpu.make_async_copy(stage_ref.at[s, r], o_hbm.at[slot_ref[t * MOE_SUB + r]], sem.at[s]).start()
        return carry

    lax.fori_loop(0, MOE_SUB // SUBLANES, issue, 0)

    @pl.when(t == last)
    def _():
        wait_tile(s)
        wait_tile(1 - s)


def _dispatch_rows(x2d, slots, ends, n_rows):
    n = x2d.shape[0]
    assert n // MOE_SUB >= 2
    return pl.pallas_call(
        _dispatch_kernel,
        grid_spec=pltpu.PrefetchScalarGridSpec(
            num_scalar_prefetch=3,
            grid=(n // MOE_SUB,),
            in_specs=[pl.BlockSpec((MOE_SUB, D_MODEL), lambda t, s0, s1, en: (t, 0))],
            out_specs=pl.BlockSpec(memory_space=pl.ANY),
            scratch_shapes=[pltpu.VMEM((2, MOE_SUB) + TOKEN_TILE, F32), pltpu.VMEM((MOE_SUB,) + TOKEN_TILE, F32),
                            pltpu.SemaphoreType.DMA((2,)), pltpu.SemaphoreType.DMA(())]),
        out_shape=jax.ShapeDtypeStruct((n_rows,) + TOKEN_TILE, F32),
        compiler_params=_params("arbitrary"),
        name="moe_dispatch",
    )(slots[0], slots[1], ends, x2d)


def _moe_kernel(exp_ref, row0_ref, nsub_ref, tail_ref, x_hbm, wg_ref, wu_ref, wd_ref, y_hbm,
                xb_ref, acc_ref, wgu_ref, wdb_ref, stage_ref, in_sem, out_sem):
    v = pl.program_id(0)
    j = pl.program_id(1)
    last = pl.num_programs(1) - 1
    n_sub = nsub_ref[v]
    row0 = row0_ref[v]

    def sub_rows(m):
        return pl.ds(pl.multiple_of(m * MOE_SUB, MOE_SUB), MOE_SUB)

    def hbm_rows(m):
        return pl.ds(pl.multiple_of(row0 + m * MOE_SUB, MOE_SUB), MOE_SUB)

    @pl.when(jnp.logical_and(v == 0, j == 0))
    def _():
        stage_ref[0] = jnp.zeros((MOE_SUB,) + TOKEN_TILE, F32)

        def zero_sub_tile(m, carry):
            cp = pltpu.make_async_copy(stage_ref.at[0], y_hbm.at[sub_rows(m)], out_sem.at[0])
            cp.start()
            cp.wait()
            return carry

        lax.fori_loop(tail_ref[0], y_hbm.shape[0] // MOE_SUB, zero_sub_tile, 0)

    @pl.when(n_sub > 0)
    def _():
        wgu_ref[:, :MOE_FF_TILE] = wg_ref[...].astype(BF16)
        wgu_ref[:, MOE_FF_TILE:] = wu_ref[...].astype(BF16)
        wdb_ref[...] = wd_ref[...].astype(BF16)

        def ffn_rows(rows):
            gu = _dot(xb_ref[rows, :], wgu_ref[...])
            hidden = (jax.nn.silu(gu[:, :MOE_FF_TILE]) * gu[:, MOE_FF_TILE:]).astype(BF16)
            return _dot(hidden, wdb_ref[...])

        def ffn_tile(m):
            return ffn_rows(sub_rows(m))

        def accumulate(after_sub_tile):
            def pair(k, carry):
                rows = pl.ds(pl.multiple_of(2 * k * MOE_SUB, 2 * MOE_SUB), 2 * MOE_SUB)
                acc_ref[rows, :] += ffn_rows(rows)
                after_sub_tile(2 * k)
                after_sub_tile(2 * k + 1)
                return carry

            lax.fori_loop(0, n_sub // 2, pair, 0)

            @pl.when(n_sub % 2 == 1)
            def _():
                acc_ref[sub_rows(n_sub - 1), :] += ffn_tile(n_sub - 1)
                after_sub_tile(n_sub - 1)

        def y_copy(m):
            return pltpu.make_async_copy(stage_ref.at[m % 2], y_hbm.at[hbm_rows(m)], out_sem.at[m % 2])

        def send_sub_tile(m):
            @pl.when(m >= 2)
            def _():
                y_copy(m - 2).wait()

            stage_ref[m % 2] = acc_ref[sub_rows(m), :].reshape((MOE_SUB,) + TOKEN_TILE)
            y_copy(m).start()

        @pl.when(j == 0)
        def _():
            def x_copy(m):
                return pltpu.make_async_copy(x_hbm.at[hbm_rows(m)], stage_ref.at[m % 2], in_sem.at[m % 2])

            x_copy(0).start()

            def body(m, carry):
                @pl.when(m + 1 < n_sub)
                def _():
                    x_copy(m + 1).start()

                x_copy(m).wait()
                xb_ref[sub_rows(m), :] = stage_ref[m % 2].reshape(MOE_SUB, D_MODEL).astype(BF16)
                acc_ref[sub_rows(m), :] = ffn_tile(m)
                return carry

            lax.fori_loop(0, n_sub, body, 0)

        @pl.when(jnp.logical_and(j > 0, j < last))
        def _():
            accumulate(lambda m: None)

        @pl.when(j == last)
        def _():
            accumulate(send_sub_tile)

            @pl.when(n_sub >= 2)
            def _():
                y_copy(n_sub - 2).wait()

            y_copy(n_sub - 1).wait()


def _moe_grouped(xs, wg, wu, wd, visit_exp, visit_row0, visit_nsub, tail_sub):
    n_rows = xs.shape[0]
    n_visits = visit_exp.shape[0]
    d_ff = wg.shape[2]
    assert d_ff // MOE_FF_TILE >= 2
    rows = MOE_VISIT_SUBS * MOE_SUB
    w_in = lambda v, j, e, r, ns, tl: (e[v], 0, j)
    w_out = lambda v, j, e, r, ns, tl: (e[v], j, 0)
    return pl.pallas_call(
        _moe_kernel,
        grid_spec=pltpu.PrefetchScalarGridSpec(
            num_scalar_prefetch=4,
            grid=(n_visits, d_ff // MOE_FF_TILE),
            in_specs=[pl.BlockSpec(memory_space=pl.ANY),
                      pl.BlockSpec((None, D_MODEL, MOE_FF_TILE), w_in),
                      pl.BlockSpec((None, D_MODEL, MOE_FF_TILE), w_in),
                      pl.BlockSpec((None, MOE_FF_TILE, D_MODEL), w_out)],
            out_specs=pl.BlockSpec(memory_space=pl.ANY),
            scratch_shapes=[pltpu.VMEM((rows, D_MODEL), BF16), pltpu.VMEM((rows, D_MODEL), F32),
                            pltpu.VMEM((D_MODEL, 2 * MOE_FF_TILE), BF16),
                            pltpu.VMEM((MOE_FF_TILE, D_MODEL), BF16),
                            pltpu.VMEM((2, MOE_SUB) + TOKEN_TILE, F32),
                            pltpu.SemaphoreType.DMA((2,)), pltpu.SemaphoreType.DMA((2,))]),
        out_shape=jax.ShapeDtypeStruct((n_rows,) + TOKEN_TILE, F32),
        compiler_params=_params("arbitrary", "arbitrary"),
        name="moe_grouped",
    )(visit_exp, visit_row0, visit_nsub, tail_sub, xs, wg, wu, wd)


def _combine_kernel(slot0_ref, slot1_ref, x_ref, gate_ref, y_hbm, g_ref, b_ref, o_ref, buf_ref, sem):
    t = pl.program_id(0)
    s = t % 2

    def issue_tile(tt, ss):
        def issue(i, carry):
            for u in range(SUBLANES):
                r = i * SUBLANES + u
                for k, slot_ref in enumerate((slot0_ref, slot1_ref)):
                    pltpu.make_async_copy(y_hbm.at[slot_ref[tt * MOE_SUB + r]], buf_ref.at[ss, k, r],
                                          sem.at[ss]).start()
            return carry

        lax.fori_loop(0, MOE_SUB // SUBLANES, issue, 0)

    @pl.when(t == 0)
    def _():
        issue_tile(0, 0)

    @pl.when(t + 1 < pl.num_programs(0))
    def _():
        issue_tile(t + 1, 1 - s)

    for k in range(TOP_K):
        pltpu.make_async_copy(y_hbm.at[pl.ds(0, MOE_SUB)], buf_ref.at[s, k], sem.at[s]).wait()
    gate = gate_ref[...]
    y0 = buf_ref[s, 0].reshape(MOE_SUB, D_MODEL)
    y1 = buf_ref[s, 1].reshape(MOE_SUB, D_MODEL)
    ff = gate[:, 0:1] * y0 + gate[:, 1:2] * y1
    o_ref[...] = _layer_norm_rows(DN_ALPHA * x_ref[...] + ff, g_ref[...], b_ref[...])


def _combine_ln(x2d, route, slots, ys, g, b):
    n = x2d.shape[0]
    return pl.pallas_call(
        _combine_kernel,
        grid_spec=pltpu.PrefetchScalarGridSpec(
            num_scalar_prefetch=2,
            grid=(n // MOE_SUB,),
            in_specs=[pl.BlockSpec((MOE_SUB, D_MODEL), lambda t, s0, s1: (t, 0)),
                      pl.BlockSpec((MOE_SUB, LANES), lambda t, s0, s1: (t, 0)),
                      pl.BlockSpec(memory_space=pl.ANY),
                      pl.BlockSpec((1, D_MODEL), lambda t, s0, s1: (0, 0)),
                      pl.BlockSpec((1, D_MODEL), lambda t, s0, s1: (0, 0))],
            out_specs=pl.BlockSpec((MOE_SUB, D_MODEL), lambda t, s0, s1: (t, 0)),
            scratch_shapes=[pltpu.VMEM((2, TOP_K, MOE_SUB) + TOKEN_TILE, F32), pltpu.SemaphoreType.DMA((2,))]),
        out_shape=jax.ShapeDtypeStruct((n, D_MODEL), F32),
        compiler_params=_params("arbitrary"),
        name="moe_combine_ln",
    )(slots[0], slots[1], x2d, route, ys, g, b)


def _routing_tables(route_t, n):
    idx = route_t[2:4].astype(jnp.int32)
    expert = jnp.arange(N_EXPERTS, dtype=jnp.int32)[:, None]
    chosen = [idx[k][None, :] == expert for k in range(TOP_K)]
    onehot = jnp.logical_or(chosen[0], chosen[1]).astype(jnp.int32)
    rank = jnp.cumsum(onehot, axis=1) - onehot
    counts = jnp.sum(onehot, axis=1)
    padded = ((counts + MOE_SUB - 1) // MOE_SUB) * MOE_SUB
    ends = jnp.cumsum(padded)
    starts = ends - padded
    place = starts[:, None] + rank
    slot = [jnp.sum(jnp.where(chosen[k], place, 0), axis=0).astype(jnp.int32) for k in range(TOP_K)]

    n_rows = -(-(n * TOP_K + N_EXPERTS * (MOE_SUB - 1)) // MOE_SUB) * MOE_SUB

    visit_rows = MOE_VISIT_SUBS * MOE_SUB
    max_chunks = -(-n_rows // visit_rows)
    chunk = jnp.arange(max_chunks, dtype=jnp.int32)[None, :]
    left = padded[:, None] - chunk * visit_rows
    valid = (left > 0).reshape(-1)
    n_visits = n_rows // visit_rows + N_EXPERTS
    order = jnp.argsort(jnp.logical_not(valid), stable=True)[:n_visits]
    n_valid = jnp.sum(valid.astype(jnp.int32))
    live = jnp.arange(n_visits) < n_valid
    order = jnp.where(live, order, order[jnp.maximum(n_valid - 1, 0)])
    v_exp = (order // max_chunks).astype(jnp.int32)
    v_chunk = (order % max_chunks).astype(jnp.int32)
    v_row0 = jnp.where(live, starts[v_exp] + v_chunk * visit_rows, 0).astype(jnp.int32)
    v_nsub = jnp.where(live, jnp.minimum(left.reshape(-1)[order], visit_rows) // MOE_SUB, 0).astype(jnp.int32)
    tail_sub = (ends[N_EXPERTS - 1:] // MOE_SUB).astype(jnp.int32)
    return slot, ends.astype(jnp.int32), n_rows, (v_exp, v_row0, v_nsub, tail_sub)


def _router_operands(w_router, b_router):
    w_pad = jnp.zeros((D_MODEL, LANES), F32).at[:, :N_EXPERTS].set(w_router)
    w_hi = w_pad.astype(BF16)
    w_lo = (w_pad - w_hi.astype(F32)).astype(BF16)
    b_pad = jnp.zeros((1, LANES), F32).at[0, :N_EXPERTS].set(b_router)
    return jnp.concatenate([w_hi, w_lo], axis=1), b_pad


def _moe_ln(x2d, route, route_t, wg, wu, wd, g, b):
    n = x2d.shape[0]
    slots, ends, n_rows, visits = _routing_tables(route_t, n)
    xs = _dispatch_rows(x2d, slots, ends, n_rows)
    ys = _moe_grouped(xs, wg, wu, wd, *visits)
    return _combine_ln(x2d, route, slots, ys, g, b)


_SRC_GATES = 4 * MLSTM_W
_SRC_AQ = _SRC_GATES + 2 * MLSTM_HEADS
_SRC_AK = _SRC_AQ + ATTN_W
_IN_PROJ_MOVES = (
    ((0, 0, 4 * MLSTM_W),)
    + tuple((_SRC_AQ + h * HEAD_DIM, AQ_OFF + i * HEAD_DIM, HEAD_DIM) for i, h in enumerate(ATTN_HEAD_ORDER))
    + ((_SRC_AK, AK_OFF, ATTN_KV_W), (_SRC_AK + ATTN_KV_W, AV_OFF, ATTN_KV_W),
       (_SRC_AK + 2 * ATTN_KV_W, SU_OFF, SGU_W), (_SRC_AK + 2 * ATTN_KV_W + SGU_W, SV_OFF, SGU_W)))
D_PROJ = _SRC_AK + 2 * ATTN_KV_W + 2 * SGU_W


def _relayout_columns(src, dst_dtype):
    lead = src.shape[:-1]
    out = jnp.zeros(lead + (D_PROJ_PAD,), dst_dtype)
    for s, d, w in _IN_PROJ_MOVES + ((_SRC_GATES, GATE_OFF, 2 * MLSTM_HEADS),):
        out = out.at[..., d:d + w].set(src[..., s:s + w].astype(dst_dtype))
    return out


def _w_layout_kernel(w_ref, o_ref):
    for s, d, w in _IN_PROJ_MOVES:
        o_ref[:, d:d + w] = w_ref[:, s:s + w].astype(o_ref.dtype)
    pad = jnp.zeros((w_ref.shape[0], LANES - 2 * MLSTM_HEADS), F32)
    gates = jnp.concatenate([w_ref[:, _SRC_GATES:_SRC_GATES + 2 * MLSTM_HEADS], pad], axis=1)
    o_ref[:, GATE_OFF:GATE_OFF + LANES] = gates.astype(o_ref.dtype)


def _layout_in_proj(w_in, b_in):
    rows = D_MODEL // 4

    def one_layer(layer):
        return pl.pallas_call(
            _w_layout_kernel,
            grid=(D_MODEL // rows,),
            in_specs=[pl.BlockSpec((None, rows, D_PROJ), lambda i: (layer, i, 0))],
            out_specs=pl.BlockSpec((rows, D_PROJ_PAD), lambda i: (i, 0)),
            out_shape=jax.ShapeDtypeStruct((D_MODEL, D_PROJ_PAD), BF16),
            compiler_params=_params("parallel"),
            name="w_in_layout",
        )(w_in)

    return [one_layer(layer) for layer in range(w_in.shape[0])], _relayout_columns(b_in, F32)[:, None, :]


def _rope_tables(positions):
    inv_freq = ROPE_THETA ** (-jnp.arange(0, ROPE_DIM, 2, dtype=F32) / ROPE_DIM)
    ang = inv_freq[None, :, None] * positions.astype(F32)[:, None, :]
    cos, sin = jnp.cos(ang), jnp.sin(ang)
    ones = jnp.ones((ang.shape[0], HEAD_DIM - ROPE_DIM, ang.shape[2]), F32)
    cos_head = jnp.concatenate([cos, cos, ones], 1)
    sin_head = jnp.concatenate([-sin, sin, 0.0 * ones], 1)
    cos_t = jnp.concatenate([cos_head, cos_head], 1)
    sin_t = jnp.concatenate([sin_head, sin_head], 1)
    return _to_token_major(cos_t, sin_t)


def _table_transpose_kernel(cos_ref, sin_ref, cos_out, sin_out):
    cos_out[...] = cos_ref[...].T
    sin_out[...] = sin_ref[...].T


def _to_token_major(cos_t, sin_t):
    b, lanes, s = cos_t.shape
    src = pl.BlockSpec((None, lanes, ROW_TILE), lambda i, c: (i, 0, c))
    dst = pl.BlockSpec((None, ROW_TILE, lanes), lambda i, c: (i, c, 0))
    shape = jax.ShapeDtypeStruct((b, s, lanes), F32)
    return pl.pallas_call(
        _table_transpose_kernel,
        grid=(b, s // ROW_TILE),
        in_specs=[src, src],
        out_specs=[dst, dst],
        out_shape=[shape, shape],
        compiler_params=_params("parallel", "parallel"),
        name="rope_tables",
    )(cos_t, sin_t)


def kernel(x, positions, w_in, b_in, conv_w, mlstm_norm_g, attn_sinks, sgu_w_s, sgu_b_s, sgu_norm_g, sgu_norm_b, w_out, ln1_g, ln1_b, ln2_g, ln2_b, ffn_w_gate, ffn_w_up, ffn_w_down, moe_w_router, moe_b_router, moe_w_gate, moe_w_up, moe_w_down):
    bsz, seq, _ = x.shape
    n = bsz * seq
    cos_t, sin_t = _rope_tables(positions)
    tril = jnp.tril(jnp.ones((CHUNK, CHUNK), bool))
    x2d = x.reshape(n, D_MODEL)
    w_p, b_p = _layout_in_proj(w_in, b_in)
    for layer in range(DEPTH):
        proj3 = _inproj(x2d, w_p[layer], b_p[layer]).reshape(bsz, seq, D_PROJ_PAD)
        h_a = _mlstm(proj3, conv_w[layer], mlstm_norm_g[layer][None, :])
        w_tril = jnp.where(tril, sgu_w_s[layer], 0.0).astype(BF16)
        bias_tok = jnp.repeat(sgu_b_s[layer].T, HEAD_DIM, axis=1)
        h_b, h_c = _swa_sgu(proj3, cos_t, sin_t, attn_sinks[layer], w_tril, bias_tok,
                            sgu_norm_g[layer][None, :], sgu_norm_b[layer][None, :])
        wo = w_out[layer]
        wb = wo[MLSTM_W:MLSTM_W + ATTN_W].reshape(ATTN_Q_HEADS, HEAD_DIM, D_MODEL)[np.array(ATTN_HEAD_ORDER)]
        wo = jnp.concatenate([wo[:MLSTM_W], wb.reshape(ATTN_W, D_MODEL), wo[MLSTM_W + ATTN_W:]], 0).astype(BF16)
        j = layer // 2
        dense = layer % 2 == 0
        mixed = _outproj_ln(h_a.reshape(n, MLSTM_W), h_b.reshape(n, ATTN_W), h_c.reshape(n, SGU_W), x2d,
                            wo, ln1_g[layer][None, :], ln1_b[layer][None, :],
                            router=None if dense else _router_operands(moe_w_router[j], moe_b_router[j]))
        g2, b2 = ln2_g[layer][None, :], ln2_b[layer][None, :]
        if dense:
            x2d = _ffn_ln(mixed, ffn_w_gate[j].astype(BF16), ffn_w_up[j].astype(BF16),
                          ffn_w_down[j].astype(BF16), g2, b2)
        else:
            x2d, route, route_t = mixed
            x2d = _moe_ln(x2d, route, route_t, moe_w_gate[j], moe_w_up[j], moe_w_down[j], g2, b2)
    return x2d.reshape(bsz, seq, D_MODEL)
```

```python
import jax
import jax.numpy as jnp
import numpy as np
from jax import lax
from jax.experimental import pallas as pl
from jax.experimental.pallas import tpu as pltpu

F32 = jnp.float32
BF16 = jnp.bfloat16

D_MODEL = 1024
HEAD_DIM = 64
LANES = 128
SUBLANES = 8
MLSTM_HEADS = 6
ATTN_Q_HEADS = 6
ATTN_KV_HEADS = 2
SGU_GROUPS = 4
MLSTM_W = MLSTM_HEADS * HEAD_DIM
ATTN_W = ATTN_Q_HEADS * HEAD_DIM
ATTN_KV_W = ATTN_KV_HEADS * HEAD_DIM
SGU_W = SGU_GROUPS * HEAD_DIM
CHUNK = 128
CONV_WIDTH = 4
ROPE_DIM = HEAD_DIM // 4
ROPE_THETA = 500000.0
N_EXPERTS = 8
TOP_K = 2
DEPTH = 2
DN_ALPHA = (2.0 * DEPTH) ** 0.25
LN_EPS = 1e-5

QK_OFF, V_OFF, O_OFF, AQ_OFF = 0, 768, 1152, 1536
GATE_OFF, AK_OFF, AV_OFF, SU_OFF, SV_OFF = 1920, 2048, 2176, 2304, 2560
D_PROJ_PAD = 2816
ATTN_HEAD_ORDER = (0, 3, 1, 4, 2, 5)

VMEM_LIMIT = 56 * 1024 * 1024

MLSTM_CHUNKS_PER_STEP = 8
SWA_BLOCKS_PER_STEP = 2
ROW_TILE = 512
FFN_ROWS = 2048
FFN_SUB = 1024
FF_TILE = 256
MOE_SUB = 512
MOE_VISIT_SUBS = 9
MOE_FF_TILE = 512


def _params(*sem):
    return pltpu.CompilerParams(dimension_semantics=sem, vmem_limit_bytes=VMEM_LIMIT)


def _lane_lo(shape):
    return lax.broadcasted_iota(jnp.int32, shape, len(shape) - 1) < HEAD_DIM


def _layer_norm_rows(z, g, b):
    mu = jnp.mean(z, axis=-1, keepdims=True)
    zc = z - mu
    var = jnp.mean(zc * zc, axis=-1, keepdims=True)
    return zc * lax.rsqrt(var + LN_EPS) * g + b


def _half_layer_norm(x, lo):
    inv = 1.0 / HEAD_DIM
    s_lo = jnp.sum(jnp.where(lo, x, 0.0), axis=-1, keepdims=True)
    s_all = jnp.sum(x, axis=-1, keepdims=True)
    mu = jnp.where(lo, s_lo, s_all - s_lo) * inv
    xc = x - mu
    sq = xc * xc
    q_lo = jnp.sum(jnp.where(lo, sq, 0.0), axis=-1, keepdims=True)
    q_all = jnp.sum(sq, axis=-1, keepdims=True)
    var = jnp.where(lo, q_lo, q_all - q_lo) * inv
    return xc * lax.rsqrt(var + LN_EPS)


def _split3(a):
    h1 = a.astype(BF16)
    r1 = a - h1.astype(F32)
    h2 = r1.astype(BF16)
    r2 = r1 - h2.astype(F32)
    return h1, h2, r2.astype(BF16)


def _split3_masked(a):
    def top(x):
        bits = lax.bitcast_convert_type(x, jnp.int32) & jnp.int32(-65536)
        return lax.bitcast_convert_type(bits, F32)

    h1 = top(a)
    r1 = a - h1
    h2 = top(r1)
    return h1.astype(BF16), h2.astype(BF16), (r1 - h2).astype(BF16)


def _dot(a, b):
    return jnp.dot(a, b, preferred_element_type=F32)


def _inproj_kernel(x_ref, w_ref, b_ref, o_ref):
    o_ref[...] = _dot(x_ref[...].astype(BF16), w_ref[...]) + b_ref[...]


def _inproj(x2d, w, b):
    n = x2d.shape[0]
    return pl.pallas_call(
        _inproj_kernel,
        grid=(n // ROW_TILE,),
        in_specs=[pl.BlockSpec((ROW_TILE, D_MODEL), lambda i: (i, 0)),
                  pl.BlockSpec((D_MODEL, D_PROJ_PAD), lambda i: (0, 0)),
                  pl.BlockSpec((1, D_PROJ_PAD), lambda i: (0, 0))],
        out_specs=pl.BlockSpec((ROW_TILE, D_PROJ_PAD), lambda i: (i, 0)),
        out_shape=jax.ShapeDtypeStruct((n, D_PROJ_PAD), F32),
        compiler_params=_params("parallel"),
        name="inproj",
    )(x2d, w, b)


def _mlstm_kernel(qk_ref, v_ref, o_ref, gate_ref, cw_ref, g_ref, out_ref, buf_ref, ct_ref, m_ref):
    rows = MLSTM_CHUNKS_PER_STEP * CHUNK

    @pl.when(pl.program_id(1) == 0)
    def _():
        buf_ref[0:8, :] = jnp.zeros((8, 2 * MLSTM_W), F32)
        ct_ref[...] = jnp.zeros_like(ct_ref)
        m_ref[...] = jnp.zeros_like(m_ref)

    buf_ref[8:8 + rows, :] = qk_ref[...]
    ct = [ct_ref[p] for p in range(MLSTM_HEADS // 2)]
    m = [m_ref[h:h + 1, 0:1] for h in range(MLSTM_HEADS)]
    for i in range(MLSTM_CHUNKS_PER_STEP):
        blk = pl.ds(i * CHUNK, CHUNK)
        ct, m = _mlstm_chunk(i * CHUNK, buf_ref, v_ref.at[blk], o_ref.at[blk], gate_ref.at[blk], cw_ref, g_ref,
                             out_ref.at[blk], ct, m)
    buf_ref[0:8, :] = qk_ref[rows - 8:rows, :]
    for p in range(MLSTM_HEADS // 2):
        ct_ref[p] = ct[p]
    for h in range(MLSTM_HEADS):
        m_ref[h:h + 1, :] = jnp.broadcast_to(m[h], (1, LANES))


def _mlstm_chunk(row0, buf_ref, v_ref, o_ref, gate_ref, cw_ref, g_ref, out_ref, ct_in, m_in):
    cw = cw_ref[...]
    base = row0 + 8 - (CONV_WIDTH - 1)
    acc = cw[0:1, :] * buf_ref[base:base + CHUNK, :]
    for j in range(1, CONV_WIDTH):
        acc = acc + cw[j:j + 1, :] * buf_ref[base + j:base + j + CHUNK, :]
    qk = jax.nn.silu(acc)

    gates = gate_ref[...]
    lf = jax.nn.log_sigmoid(gates)
    row = lax.broadcasted_iota(jnp.int32, (CHUNK, CHUNK), 0)
    col = lax.broadcasted_iota(jnp.int32, (CHUNK, CHUNK), 1)
    causal = col <= row
    tri = causal.astype(BF16)
    l1, l2, l3 = _split3(lf)
    cum = _dot(tri, l1) + _dot(tri, l2) + _dot(tri, l3)
    gates_t = gates.T
    cum_t = cum.T
    cum_parts = _split3_masked(cum)

    lo = _lane_lo((CHUNK, LANES))
    ct_out = []
    m_out = []
    for p in range(MLSTM_HEADS // 2):
        sl = slice(p * LANES, (p + 1) * LANES)
        q_slab = qk[:, sl] * (HEAD_DIM ** -0.5)
        k_slab = qk[:, MLSTM_W + p * LANES:MLSTM_W + (p + 1) * LANES]
        kt_slab = k_slab.T
        v_slab = v_ref[:, sl]
        ct_pair = ct_in[p]
        ct_pair_b = ct_pair.astype(BF16)
        halves = []
        new_ct = []
        for half in range(2):
            h = 2 * p + half
            sel = lo if half == 0 else jnp.logical_not(lo)
            li_row = gates_t[h:h + 1, :]
            bc_row = cum_t[MLSTM_HEADS + h:MLSTM_HEADS + h + 1, :]
            pick = (row == MLSTM_HEADS + h).astype(BF16)
            bc_col = _dot(cum_parts[0], pick) + _dot(cum_parts[1], pick) + _dot(cum_parts[2], pick)
            b_tot = bc_row[:, CHUNK - 1:CHUNK]
            m_prev = m_in[h]

            dmat = jnp.where(causal, bc_col - bc_row + li_row, -jnp.inf)
            m_inter = bc_col + m_prev
            m_row = jnp.maximum(m_inter, jnp.max(dmat, axis=-1, keepdims=True))
            q_m = jnp.where(sel, q_slab, 0.0).astype(BF16)
            s = _dot(q_m, kt_slab.astype(BF16))
            pmat = (s * jnp.exp(dmat - m_row)).astype(BF16)
            w_inter = jnp.exp(m_inter - m_row)
            v_aug = jnp.where(sel, v_slab, 1.0).astype(BF16)
            halves.append((_dot(pmat, v_aug) + w_inter * _dot(q_m, ct_pair_b), jnp.exp(-m_row)))

            a_row = b_tot - bc_row + li_row
            m_loc = jnp.max(a_row, axis=-1, keepdims=True)
            w_row = jnp.exp(a_row - m_loc)
            kt_h = kt_slab[half * HEAD_DIM:(half + 1) * HEAD_DIM, :]
            ct_loc = _dot((kt_h * w_row).astype(BF16), v_aug)
            m_new = jnp.maximum(b_tot + m_prev, m_loc)
            s_old = jnp.exp(b_tot + m_prev - m_new)
            s_loc = jnp.exp(m_loc - m_new)
            new_ct.append(s_old * ct_pair[half * HEAD_DIM:(half + 1) * HEAD_DIM, :] + s_loc * ct_loc)
            m_out.append(m_new)

        ct_out.append(jnp.concatenate(new_ct, axis=0))
        (r_even, stab_even), (r_odd, stab_odd) = halves
        num = jnp.where(lo, r_even, r_odd)
        den = pltpu.roll(jnp.where(lo, r_odd, r_even), HEAD_DIM, 1)
        hh = num / jnp.maximum(jnp.abs(den), jnp.where(lo, stab_even, stab_odd))
        hn = _half_layer_norm(hh, lo) * g_ref[:, sl]
        out_ref[:, sl] = (hn * jax.nn.sigmoid(o_ref[:, sl])).astype(out_ref.dtype)
    return ct_out, m_out


def _mlstm(proj3, conv_w, norm_g):
    b, s, _ = proj3.shape
    rows = MLSTM_CHUNKS_PER_STEP * CHUNK
    assert s % rows == 0
    blk = lambda w, off: pl.BlockSpec((None, rows, w), lambda i, c: (i, c, off // w))
    return pl.pallas_call(
        _mlstm_kernel,
        grid=(b, s // rows),
        in_specs=[blk(2 * MLSTM_W, QK_OFF), blk(MLSTM_W, V_OFF), blk(MLSTM_W, O_OFF), blk(LANES, GATE_OFF),
                  pl.BlockSpec((CONV_WIDTH, 2 * MLSTM_W), lambda i, c: (0, 0)),
                  pl.BlockSpec((1, MLSTM_W), lambda i, c: (0, 0))],
        out_specs=pl.BlockSpec((None, rows, MLSTM_W), lambda i, c: (i, c, 0)),
        out_shape=jax.ShapeDtypeStruct((b, s, MLSTM_W), BF16),
        scratch_shapes=[pltpu.VMEM((8 + rows, 2 * MLSTM_W), F32),
                        pltpu.VMEM((MLSTM_HEADS // 2, LANES, LANES), F32),
                        pltpu.VMEM((8, LANES), F32)],
        compiler_params=_params("parallel", "arbitrary"),
        name="mlstm",
    )(proj3, proj3, proj3, proj3, conv_w, norm_g)


def _rope(x, cos_t, sin_t, first):
    return x * cos_t + jnp.where(first, pltpu.roll(x, LANES - ROPE_DIM // 2, 1),
                                 pltpu.roll(x, ROPE_DIM // 2, 1)) * sin_t


def _swa_kernel(sink_ref, q_ref, k_ref, v_ref, cos_ref, sin_ref, su_ref, sv_ref, ws_ref, bias_ref, sg_ref, sb_ref,
                out_ref, gated_ref, kt_ref, vv_ref):
    first_step = pl.program_id(1) == 0

    @pl.when(first_step)
    def _():
        kt_ref[...] = jnp.zeros_like(kt_ref)
        vv_ref[...] = jnp.zeros_like(vv_ref)

    kt_prev = kt_ref[...]
    v_prev = vv_ref[...]
    for i in range(SWA_BLOCKS_PER_STEP):
        rows = pl.ds(i * CHUNK, CHUNK)
        hide_previous = jnp.where(first_step, 2 * CHUNK, 0) if i == 0 else 0
        kt_prev, v_prev = _swa_block(hide_previous, sink_ref, q_ref.at[rows], k_ref.at[rows], v_ref.at[rows],
                                     cos_ref.at[rows], sin_ref.at[rows], out_ref.at[rows], kt_prev, v_prev)
        _sgu_chunk(slice(i * CHUNK, (i + 1) * CHUNK), su_ref, sv_ref, ws_ref, bias_ref, sg_ref, sb_ref, gated_ref)
    kt_ref[...] = kt_prev
    vv_ref[...] = v_prev


def _swa_block(first_block_shift, sink_ref, q_ref, k_ref, v_ref, cos_ref, sin_ref, out_ref, kt_prev, v_prev):
    cos_t = cos_ref[...]
    sin_t = sin_ref[...]
    lane = lax.broadcasted_iota(jnp.int32, (CHUNK, LANES), 1)
    lo = lane < HEAD_DIM
    first = (lane % HEAD_DIM) < ROPE_DIM // 2

    kt_cur = _rope(k_ref[...], cos_t, sin_t, first).T.astype(BF16)
    v_cur = v_ref[...].astype(BF16)
    kt = jnp.concatenate([kt_prev, kt_cur], axis=1)
    vv = jnp.concatenate([v_prev, v_cur], axis=0)

    row = lax.broadcasted_iota(jnp.int32, (CHUNK, 2 * CHUNK), 0)
    col = lax.broadcasted_iota(jnp.int32, (CHUNK, 2 * CHUNK), 1)
    visible = jnp.logical_or(jnp.logical_and(col < CHUNK, col > row + first_block_shift),
                             jnp.logical_and(col >= CHUNK, col - CHUNK <= row))

    for j in range(ATTN_Q_HEADS // 2):
        sl = slice(j * LANES, (j + 1) * LANES)
        q_slab = _rope(q_ref[:, sl], cos_t, sin_t, first) * (HEAD_DIM ** -0.5)
        outs = []
        for half in range(2):
            sel = lo if half == 0 else jnp.logical_not(lo)
            sink = sink_ref[ATTN_HEAD_ORDER[2 * j + half]]
            q_m = jnp.where(sel, q_slab, 0.0).astype(BF16)
            s = jnp.where(visible, _dot(q_m, kt), -jnp.inf)
            mx = jnp.maximum(jnp.max(s, axis=-1, keepdims=True), sink)
            pexp = jnp.exp(s - mx)
            denom = jnp.sum(pexp, axis=-1, keepdims=True) + jnp.exp(sink - mx)
            outs.append(_dot(pexp.astype(BF16), vv) / denom)
        out_ref[:, sl] = jnp.where(lo, outs[0], outs[1]).astype(out_ref.dtype)

    return kt_cur, v_cur


def _swa_sgu(proj3, cos_t, sin_t, sinks, w_tril, bias_tok, norm_g, norm_b):
    b, s, _ = proj3.shape
    rows = SWA_BLOCKS_PER_STEP * CHUNK
    assert s % rows == 0
    blk = lambda w, off: pl.BlockSpec((None, rows, w), lambda i, c: (i, c, off // w))
    tab = pl.BlockSpec((None, rows, LANES), lambda i, c: (i, c, 0))
    const2 = lambda shape: pl.BlockSpec(shape, lambda i, c: (0,) * len(shape))
    out = lambda w: pl.BlockSpec((None, rows, w), lambda i, c: (i, c, 0))
    return pl.pallas_call(
        _swa_kernel,
        grid=(b, s // rows),
        in_specs=[pl.BlockSpec(memory_space=pltpu.SMEM),
                  blk(ATTN_W, AQ_OFF), blk(ATTN_KV_W, AK_OFF), blk(ATTN_KV_W, AV_OFF), tab, tab,
                  blk(SGU_W, SU_OFF), blk(SGU_W, SV_OFF), const2((SGU_GROUPS, CHUNK, CHUNK)),
                  const2((CHUNK, SGU_W)), const2((1, SGU_W)), const2((1, SGU_W))],
        out_specs=[out(ATTN_W), out(SGU_W)],
        out_shape=[jax.ShapeDtypeStruct((b, s, ATTN_W), BF16), jax.ShapeDtypeStruct((b, s, SGU_W), BF16)],
        scratch_shapes=[pltpu.VMEM((LANES, CHUNK), BF16), pltpu.VMEM((CHUNK, LANES), BF16)],
        compiler_params=_params("parallel", "arbitrary"),
        name="swa_sgu",
    )(sinks, proj3, proj3, proj3, cos_t, sin_t, proj3, proj3, w_tril, bias_tok, norm_g, norm_b)


def _sgu_chunk(rows, u_ref, v_ref, w_ref, bias_ref, g_ref, b_ref, out_ref):
    lo = _lane_lo((CHUNK, LANES))
    for j in range(SGU_GROUPS // 2):
        sl = slice(j * LANES, (j + 1) * LANES)
        u = jax.nn.gelu(u_ref[rows, sl])
        v = jax.nn.gelu(v_ref[rows, sl])
        vn = (_half_layer_norm(v, lo) * g_ref[:, sl] + b_ref[:, sl]).astype(BF16)
        mixed = jnp.where(lo, _dot(w_ref[2 * j], vn), _dot(w_ref[2 * j + 1], vn)) + bias_ref[:, sl]
        out_ref[rows, sl] = (u * mixed).astype(out_ref.dtype)


def _outproj_kernel(ha_ref, hb_ref, hc_ref, x_ref, w_ref, g_ref, b_ref, *rest):
    cat_ref = rest[-1]
    cat_ref[:, 0:MLSTM_W] = ha_ref[...]
    cat_ref[:, MLSTM_W:MLSTM_W + ATTN_W] = hb_ref[...]
    cat_ref[:, MLSTM_W + ATTN_W:] = hc_ref[...]
    mix = _dot(cat_ref[...], w_ref[...])
    y = _layer_norm_rows(DN_ALPHA * x_ref[...] + mix, g_ref[...], b_ref[...])
    rest = rest[:-1]
    if len(rest) == 1:
        rest[0][...] = y
        return
    w_split_ref, rb_ref, o_ref, route_ref, route_t_ref = rest
    o_ref[...] = y
    route_ref[...], route_t_ref[...] = _route_rows(y, w_split_ref[...], rb_ref[...])


def _outproj_ln(ha, hb, hc, x2d, w, g, b, router=None):
    n = x2d.shape[0]
    rows = lambda w: pl.BlockSpec((ROW_TILE, w), lambda i: (i, 0))
    const = lambda shape: pl.BlockSpec(shape, lambda i: (0, 0))
    in_specs = [rows(MLSTM_W), rows(ATTN_W), rows(SGU_W), rows(D_MODEL),
                const((D_MODEL, D_MODEL)), const((1, D_MODEL)), const((1, D_MODEL))]
    out_specs = [rows(D_MODEL)]
    out_shape = [jax.ShapeDtypeStruct((n, D_MODEL), F32)]
    operands = (ha, hb, hc, x2d, w, g, b)
    if router is not None:
        in_specs += [const((D_MODEL, 2 * LANES)), const((1, LANES))]
        out_specs += [rows(LANES), pl.BlockSpec((N_EXPERTS, ROW_TILE), lambda i: (0, i))]
        out_shape += [jax.ShapeDtypeStruct((n, LANES), F32), jax.ShapeDtypeStruct((N_EXPERTS, n), F32)]
        operands += tuple(router)
    out = pl.pallas_call(
        _outproj_kernel,
        grid=(n // ROW_TILE,),
        in_specs=in_specs,
        out_specs=out_specs,
        out_shape=out_shape,
        scratch_shapes=[pltpu.VMEM((ROW_TILE, D_MODEL), BF16)],
        compiler_params=_params("parallel"),
        name="outproj_ln" if router is None else "outproj_ln_route",
    )(*operands)
    return out[0] if router is None else tuple(out)


def _ffn_kernel(x_ref, wg_ref, wu_ref, wd_ref, g_ref, b_ref, o_ref, xb_ref, wgu_ref):
    j = pl.program_id(1)
    last = pl.num_programs(1) - 1
    n_sub = FFN_ROWS // FFN_SUB
    wgu_ref[:, :FF_TILE] = wg_ref[...]
    wgu_ref[:, FF_TILE:] = wu_ref[...]

    def sub_rows(m):
        return pl.ds(pl.multiple_of(m * FFN_SUB, FFN_SUB), FFN_SUB)

    def ffn_tile(m):
        gu = _dot(xb_ref[sub_rows(m), :], wgu_ref[...])
        hidden = (jax.nn.silu(gu[:, :FF_TILE]) * gu[:, FF_TILE:]).astype(BF16)
        return _dot(hidden, wd_ref[...])

    @pl.when(j == 0)
    def _():
        def body(m, carry):
            xb_ref[sub_rows(m), :] = x_ref[sub_rows(m), :].astype(BF16)
            o_ref[sub_rows(m), :] = ffn_tile(m)
            return carry

        lax.fori_loop(0, n_sub, body, 0)

    @pl.when(jnp.logical_and(j > 0, j < last))
    def _():
        def body(m, carry):
            o_ref[sub_rows(m), :] += ffn_tile(m)
            return carry

        lax.fori_loop(0, n_sub, body, 0)

    @pl.when(j == last)
    def _():
        def body(m, carry):
            rows = sub_rows(m)
            ff = o_ref[rows, :] + ffn_tile(m)
            o_ref[rows, :] = _layer_norm_rows(DN_ALPHA * x_ref[rows, :] + ff, g_ref[...], b_ref[...])
            return carry

        lax.fori_loop(0, n_sub, body, 0)


def _ffn_ln(x2d, wg, wu, wd, g, b):
    n = x2d.shape[0]
    d_ff = wd.shape[0]
    assert d_ff // FF_TILE >= 2
    tm = FFN_ROWS
    return pl.pallas_call(
        _ffn_kernel,
        grid=(n // tm, d_ff // FF_TILE),
        in_specs=[pl.BlockSpec((tm, D_MODEL), lambda i, j: (i, 0)),
                  pl.BlockSpec((D_MODEL, FF_TILE), lambda i, j: (0, j)),
                  pl.BlockSpec((D_MODEL, FF_TILE), lambda i, j: (0, j)),
                  pl.BlockSpec((FF_TILE, D_MODEL), lambda i, j: (j, 0)),
                  pl.BlockSpec((1, D_MODEL), lambda i, j: (0, 0)),
                  pl.BlockSpec((1, D_MODEL), lambda i, j: (0, 0))],
        out_specs=pl.BlockSpec((tm, D_MODEL), lambda i, j: (i, 0)),
        out_shape=jax.ShapeDtypeStruct((n, D_MODEL), F32),
        scratch_shapes=[pltpu.VMEM((tm, D_MODEL), BF16), pltpu.VMEM((D_MODEL, 2 * FF_TILE), BF16)],
        compiler_params=_params("parallel", "arbitrary"),
        name="ffn_ln",
    )(x2d, wg, wu, wd, g, b)


def _route_rows(x, w_split, bias):
    n = x.shape[0]
    xh = x.astype(BF16)
    xl = (x - xh.astype(F32)).astype(BF16)
    prod = _dot(jnp.concatenate([xh, xl], axis=0), w_split)
    logits = prod[:n, :LANES] + prod[:n, LANES:] + prod[n:, :LANES] + prod[n:, LANES:] + bias
    lt = logits.T[0:N_EXPERTS, :]
    expert = lax.broadcasted_iota(jnp.int32, lt.shape, 0)
    m1 = jnp.max(lt, axis=0, keepdims=True)
    i1 = jnp.min(jnp.where(lt == m1, expert, N_EXPERTS), axis=0, keepdims=True)
    rest = jnp.where(expert == i1, -jnp.inf, lt)
    m2 = jnp.max(rest, axis=0, keepdims=True)
    i2 = jnp.min(jnp.where(rest == m2, expert, N_EXPERTS), axis=0, keepdims=True)
    e2 = jnp.exp(m2 - m1)
    g1 = 1.0 / (1.0 + e2)
    g2 = e2 / (1.0 + e2)
    route_t = jnp.where(expert == 0, g1, jnp.where(expert == 1, g2, jnp.where(
        expert == 2, i1.astype(F32), jnp.where(expert == 3, i2.astype(F32), 0.0))))
    padded = jnp.concatenate([route_t, jnp.zeros((LANES - N_EXPERTS, n), F32)], axis=0)
    return padded.T, route_t


TOKEN_TILE = (SUBLANES, D_MODEL // SUBLANES)


def _dispatch_kernel(slot0_ref, slot1_ref, ends_ref, x_ref, o_hbm, stage_ref, zero_ref, sem, zero_sem):
    t = pl.program_id(0)
    last = pl.num_programs(0) - 1
    s = t % 2

    def wait_tile(ss):
        for _ in range(TOP_K):
            pltpu.make_async_copy(zero_ref, o_hbm.at[pl.ds(0, MOE_SUB)], sem.at[ss]).wait()

    @pl.when(t == 0)
    def _():
        zero_ref[...] = jnp.zeros_like(zero_ref)

        def zero_sub_tile(m, carry):
            dst = o_hbm.at[pl.ds(pl.multiple_of(m * MOE_SUB, MOE_SUB), MOE_SUB)]
            cp = pltpu.make_async_copy(zero_ref, dst, zero_sem)
            cp.start()
            cp.wait()
            return carry

        for e in range(N_EXPERTS):
            @pl.when(ends_ref[e] >= MOE_SUB)
            def _():
                zero_sub_tile(ends_ref[e] // MOE_SUB - 1, 0)

        lax.fori_loop(ends_ref[N_EXPERTS - 1] // MOE_SUB, o_hbm.shape[0] // MOE_SUB, zero_sub_tile, 0)

    @pl.when(t >= 2)
    def _():
        wait_tile(s)

    stage_ref[s] = x_ref[...].reshape((MOE_SUB,) + TOKEN_TILE)

    def issue(i, carry):
        for u in range(SUBLANES):
            r = i * SUBLANES + u
            for k, slot_ref in enumerate((slot0_ref, slot1_ref)):
                pltpu.make_async_copy(stage_ref.at[s, r], o_hbm.at[slot_ref[t * MOE_SUB + r]],
                                      sem.at[s]).start(priority=k)
        return carry

    lax.fori_loop(0, MOE_SUB // SUBLANES, issue, 0)

    @pl.when(t == last)
    def _():
        wait_tile(s)
        wait_tile(1 - s)


def _dispatch_rows(x2d, slots, ends, n_rows):
    n = x2d.shape[0]
    assert n // MOE_SUB >= 2
    return pl.pallas_call(
        _dispatch_kernel,
        grid_spec=pltpu.PrefetchScalarGridSpec(
            num_scalar_prefetch=3,
            grid=(n // MOE_SUB,),
            in_specs=[pl.BlockSpec((MOE_SUB, D_MODEL), lambda t, s0, s1, en: (t, 0))],
            out_specs=pl.BlockSpec(memory_space=pl.ANY),
            scratch_shapes=[pltpu.VMEM((2, MOE_SUB) + TOKEN_TILE, F32), pltpu.VMEM((MOE_SUB,) + TOKEN_TILE, F32),
                            pltpu.SemaphoreType.DMA((2,)), pltpu.SemaphoreType.DMA(())]),
        out_shape=jax.ShapeDtypeStruct((n_rows,) + TOKEN_TILE, F32),
        compiler_params=_params("arbitrary"),
        name="moe_dispatch",
    )(slots[0], slots[1], ends, x2d)


def _moe_kernel(exp_ref, row0_ref, nsub_ref, tail_ref, x_hbm, wg_ref, wu_ref, wd_ref, y_hbm,
                xb_ref, acc_ref, wgu_ref, wdb_ref, stage_ref, in_sem, out_sem):
    v = pl.program_id(0)
    j = pl.program_id(1)
    last = pl.num_programs(1) - 1
    n_sub = nsub_ref[v]
    row0 = row0_ref[v]

    def sub_rows(m):
        return pl.ds(pl.multiple_of(m * MOE_SUB, MOE_SUB), MOE_SUB)

    def hbm_rows(m):
        return pl.ds(pl.multiple_of(row0 + m * MOE_SUB, MOE_SUB), MOE_SUB)

    @pl.when(jnp.logical_and(v == 0, j == 0))
    def _():
        stage_ref[0] = jnp.zeros((MOE_SUB,) + TOKEN_TILE, F32)

        def zero_sub_tile(m, carry):
            cp = pltpu.make_async_copy(stage_ref.at[0], y_hbm.at[sub_rows(m)], out_sem.at[0])
            cp.start()
            cp.wait()
            return carry

        lax.fori_loop(tail_ref[0], y_hbm.shape[0] // MOE_SUB, zero_sub_tile, 0)

    @pl.when(n_sub > 0)
    def _():
        wgu_ref[:, :MOE_FF_TILE] = wg_ref[...].astype(BF16)
        wgu_ref[:, MOE_FF_TILE:] = wu_ref[...].astype(BF16)
        wdb_ref[...] = wd_ref[...].astype(BF16)

        def ffn_rows(rows):
            gu = _dot(xb_ref[rows, :], wgu_ref[...])
            hidden = (jax.nn.silu(gu[:, :MOE_FF_TILE]) * gu[:, MOE_FF_TILE:]).astype(BF16)
            return _dot(hidden, wdb_ref[...])

        def ffn_tile(m):
            return ffn_rows(sub_rows(m))

        def accumulate(after_sub_tile):
            def pair(k, carry):
                rows = pl.ds(pl.multiple_of(2 * k * MOE_SUB, 2 * MOE_SUB), 2 * MOE_SUB)
                acc_ref[rows, :] += ffn_rows(rows)
                after_sub_tile(2 * k)
                after_sub_tile(2 * k + 1)
                return carry

            lax.fori_loop(0, n_sub // 2, pair, 0)

            @pl.when(n_sub % 2 == 1)
            def _():
                acc_ref[sub_rows(n_sub - 1), :] += ffn_tile(n_sub - 1)
                after_sub_tile(n_sub - 1)

        def y_copy(m):
            return pltpu.make_async_copy(stage_ref.at[m % 2], y_hbm.at[hbm_rows(m)], out_sem.at[m % 2])

        def send_sub_tile(m):
            @pl.when(m >= 2)
            def _():
                y_copy(m - 2).wait()

            stage_ref[m % 2] = acc_ref[sub_rows(m), :].reshape((MOE_SUB,) + TOKEN_TILE)
            y_copy(m).start()

        @pl.when(j == 0)
        def _():
            def x_copy(m):
                return pltpu.make_async_copy(x_hbm.at[hbm_rows(m)], stage_ref.at[m % 2], in_sem.at[m % 2])

            x_copy(0).start()

            def body(m, carry):
                @pl.when(m + 1 < n_sub)
                def _():
                    x_copy(m + 1).start()

                x_copy(m).wait()
                xb_ref[sub_rows(m), :] = stage_ref[m % 2].reshape(MOE_SUB, D_MODEL).astype(BF16)
                acc_ref[sub_rows(m), :] = ffn_tile(m)
                return carry

            lax.fori_loop(0, n_sub, body, 0)

        @pl.when(jnp.logical_and(j > 0, j < last))
        def _():
            accumulate(lambda m: None)

        @pl.when(j == last)
        def _():
            accumulate(send_sub_tile)

            @pl.when(n_sub >= 2)
            def _():
                y_copy(n_sub - 2).wait()

            y_copy(n_sub - 1).wait()


def _moe_grouped(xs, wg, wu, wd, visit_exp, visit_row0, visit_nsub, tail_sub):
    n_rows = xs.shape[0]
    n_visits = visit_exp.shape[0]
    d_ff = wg.shape[2]
    assert d_ff // MOE_FF_TILE >= 2
    rows = MOE_VISIT_SUBS * MOE_SUB
    w_in = lambda v, j, e, r, ns, tl: (e[v], 0, j)
    w_out = lambda v, j, e, r, ns, tl: (e[v], j, 0)
    return pl.pallas_call(
        _moe_kernel,
        grid_spec=pltpu.PrefetchScalarGridSpec(
            num_scalar_prefetch=4,
            grid=(n_visits, d_ff // MOE_FF_TILE),
            in_specs=[pl.BlockSpec(memory_space=pl.ANY),
                      pl.BlockSpec((None, D_MODEL, MOE_FF_TILE), w_in),
                      pl.BlockSpec((None, D_MODEL, MOE_FF_TILE), w_in),
                      pl.BlockSpec((None, MOE_FF_TILE, D_MODEL), w_out)],
            out_specs=pl.BlockSpec(memory_space=pl.ANY),
            scratch_shapes=[pltpu.VMEM((rows, D_MODEL), BF16), pltpu.VMEM((rows, D_MODEL), F32),
                            pltpu.VMEM((D_MODEL, 2 * MOE_FF_TILE), BF16),
                            pltpu.VMEM((MOE_FF_TILE, D_MODEL), BF16),
                            pltpu.VMEM((2, MOE_SUB) + TOKEN_TILE, F32),
                            pltpu.SemaphoreType.DMA((2,)), pltpu.SemaphoreType.DMA((2,))]),
        out_shape=jax.ShapeDtypeStruct((n_rows,) + TOKEN_TILE, F32),
        compiler_params=_params("arbitrary", "arbitrary"),
        name="moe_grouped",
    )(visit_exp, visit_row0, visit_nsub, tail_sub, xs, wg, wu, wd)


def _combine_kernel(slot0_ref, slot1_ref, x_ref, gate_ref, y_hbm, g_ref, b_ref, o_ref, buf_ref, sem):
    t = pl.program_id(0)
    s = t % 2

    def issue_tile(tt, ss):
        def issue(i, carry):
            for u in range(SUBLANES):
                r = i * SUBLANES + u
                for k, slot_ref in enumerate((slot0_ref, slot1_ref)):
                    pltpu.make_async_copy(y_hbm.at[slot_ref[tt * MOE_SUB + r]], buf_ref.at[ss, k, r],
                                          sem.at[ss]).start(priority=k)
            return carry

        lax.fori_loop(0, MOE_SUB // SUBLANES, issue, 0)

    @pl.when(t == 0)
    def _():
        issue_tile(0, 0)

    @pl.when(t + 1 < pl.num_programs(0))
    def _():
        issue_tile(t + 1, 1 - s)

    for k in range(TOP_K):
        pltpu.make_async_copy(y_hbm.at[pl.ds(0, MOE_SUB)], buf_ref.at[s, k], sem.at[s]).wait()
    gate = gate_ref[...]
    y0 = buf_ref[s, 0].reshape(MOE_SUB, D_MODEL)
    y1 = buf_ref[s, 1].reshape(MOE_SUB, D_MODEL)
    ff = gate[:, 0:1] * y0 + gate[:, 1:2] * y1
    o_ref[...] = _layer_norm_rows(DN_ALPHA * x_ref[...] + ff, g_ref[...], b_ref[...])


def _combine_ln(x2d, route, slots, ys, g, b):
    n = x2d.shape[0]
    return pl.pallas_call(
        _combine_kernel,
        grid_spec=pltpu.PrefetchScalarGridSpec(
            num_scalar_prefetch=2,
            grid=(n // MOE_SUB,),
            in_specs=[pl.BlockSpec((MOE_SUB, D_MODEL), lambda t, s0, s1: (t, 0)),
                      pl.BlockSpec((MOE_SUB, LANES), lambda t, s0, s1: (t, 0)),
                      pl.BlockSpec(memory_space=pl.ANY),
                      pl.BlockSpec((1, D_MODEL), lambda t, s0, s1: (0, 0)),
                      pl.BlockSpec((1, D_MODEL), lambda t, s0, s1: (0, 0))],
            out_specs=pl.BlockSpec((MOE_SUB, D_MODEL), lambda t, s0, s1: (t, 0)),
            scratch_shapes=[pltpu.VMEM((2, TOP_K, MOE_SUB) + TOKEN_TILE, F32), pltpu.SemaphoreType.DMA((2,))]),
        out_shape=jax.ShapeDtypeStruct((n, D_MODEL), F32),
        compiler_params=_params("arbitrary"),
        name="moe_combine_ln",
    )(slots[0], slots[1], x2d, route, ys, g, b)


def _routing_tables(route_t, n):
    idx = route_t[2:4].astype(jnp.int32)
    expert = jnp.arange(N_EXPERTS, dtype=jnp.int32)[:, None]
    chosen = [idx[k][None, :] == expert for k in range(TOP_K)]
    onehot = jnp.logical_or(chosen[0], chosen[1]).astype(jnp.int32)
    rank = jnp.cumsum(onehot, axis=1) - onehot
    counts = jnp.sum(onehot, axis=1)
    padded = ((counts + MOE_SUB - 1) // MOE_SUB) * MOE_SUB
    ends = jnp.cumsum(padded)
    starts = ends - padded
    place = starts[:, None] + rank
    slot = [jnp.sum(jnp.where(chosen[k], place, 0), axis=0).astype(jnp.int32) for k in range(TOP_K)]

    n_rows = -(-(n * TOP_K + N_EXPERTS * (MOE_SUB - 1)) // MOE_SUB) * MOE_SUB

    visit_rows = MOE_VISIT_SUBS * MOE_SUB
    max_chunks = -(-n_rows // visit_rows)
    chunk = jnp.arange(max_chunks, dtype=jnp.int32)[None, :]
    left = padded[:, None] - chunk * visit_rows
    valid = (left > 0).reshape(-1)
    n_visits = n_rows // visit_rows + N_EXPERTS
    order = jnp.argsort(jnp.logical_not(valid), stable=True)[:n_visits]
    n_valid = jnp.sum(valid.astype(jnp.int32))
    live = jnp.arange(n_visits) < n_valid
    order = jnp.where(live, order, order[jnp.maximum(n_valid - 1, 0)])
    v_exp = (order // max_chunks).astype(jnp.int32)
    v_chunk = (order % max_chunks).astype(jnp.int32)
    v_row0 = jnp.where(live, starts[v_exp] + v_chunk * visit_rows, 0).astype(jnp.int32)
    v_nsub = jnp.where(live, jnp.minimum(left.reshape(-1)[order], visit_rows) // MOE_SUB, 0).astype(jnp.int32)
    tail_sub = (ends[N_EXPERTS - 1:] // MOE_SUB).astype(jnp.int32)
    return slot, ends.astype(jnp.int32), n_rows, (v_exp, v_row0, v_nsub, tail_sub)


def _router_operands(w_router, b_router):
    w_pad = jnp.zeros((D_MODEL, LANES), F32).at[:, :N_EXPERTS].set(w_router)
    w_hi = w_pad.astype(BF16)
    w_lo = (w_pad - w_hi.astype(F32)).astype(BF16)
    b_pad = jnp.zeros((1, LANES), F32).at[0, :N_EXPERTS].set(b_router)
    return jnp.concatenate([w_hi, w_lo], axis=1), b_pad


def _moe_ln(x2d, route, route_t, wg, wu, wd, g, b):
    n = x2d.shape[0]
    slots, ends, n_rows, visits = _routing_tables(route_t, n)
    xs = _dispatch_rows(x2d, slots, ends, n_rows)
    ys = _moe_grouped(xs, wg, wu, wd, *visits)
    return _combine_ln(x2d, route, slots, ys, g, b)


_SRC_GATES = 4 * MLSTM_W
_SRC_AQ = _SRC_GATES + 2 * MLSTM_HEADS
_SRC_AK = _SRC_AQ + ATTN_W
_IN_PROJ_MOVES = (
    ((0, 0, 4 * MLSTM_W),)
    + tuple((_SRC_AQ + h * HEAD_DIM, AQ_OFF + i * HEAD_DIM, HEAD_DIM) for i, h in enumerate(ATTN_HEAD_ORDER))
    + ((_SRC_AK, AK_OFF, ATTN_KV_W), (_SRC_AK + ATTN_KV_W, AV_OFF, ATTN_KV_W),
       (_SRC_AK + 2 * ATTN_KV_W, SU_OFF, SGU_W), (_SRC_AK + 2 * ATTN_KV_W + SGU_W, SV_OFF, SGU_W)))
D_PROJ = _SRC_AK + 2 * ATTN_KV_W + 2 * SGU_W


def _relayout_columns(src, dst_dtype):
    lead = src.shape[:-1]
    out = jnp.zeros(lead + (D_PROJ_PAD,), dst_dtype)
    for s, d, w in _IN_PROJ_MOVES + ((_SRC_GATES, GATE_OFF, 2 * MLSTM_HEADS),):
        out = out.at[..., d:d + w].set(src[..., s:s + w].astype(dst_dtype))
    return out


def _w_layout_kernel(w_ref, o_ref):
    for s, d, w in _IN_PROJ_MOVES:
        o_ref[:, d:d + w] = w_ref[:, s:s + w].astype(o_ref.dtype)
    pad = jnp.zeros((w_ref.shape[0], LANES - 2 * MLSTM_HEADS), F32)
    gates = jnp.concatenate([w_ref[:, _SRC_GATES:_SRC_GATES + 2 * MLSTM_HEADS], pad], axis=1)
    o_ref[:, GATE_OFF:GATE_OFF + LANES] = gates.astype(o_ref.dtype)


def _layout_in_proj(w_in, b_in):
    rows = D_MODEL // 4

    def one_layer(layer):
        return pl.pallas_call(
            _w_layout_kernel,
            grid=(D_MODEL // rows,),
            in_specs=[pl.BlockSpec((None, rows, D_PROJ), lambda i: (layer, i, 0))],
            out_specs=pl.BlockSpec((rows, D_PROJ_PAD), lambda i: (i, 0)),
            out_shape=jax.ShapeDtypeStruct((D_MODEL, D_PROJ_PAD), BF16),
            compiler_params=_params("parallel"),
            name="w_in_layout",
        )(w_in)

    return [one_layer(layer) for layer in range(w_in.shape[0])], _relayout_columns(b_in, F32)[:, None, :]


def _rope_tables(positions):
    inv_freq = ROPE_THETA ** (-jnp.arange(0, ROPE_DIM, 2, dtype=F32) / ROPE_DIM)
    ang = inv_freq[None, :, None] * positions.astype(F32)[:, None, :]
    cos, sin = jnp.cos(ang), jnp.sin(ang)
    ones = jnp.ones((ang.shape[0], HEAD_DIM - ROPE_DIM, ang.shape[2]), F32)
    cos_head = jnp.concatenate([cos, cos, ones], 1)
    sin_head = jnp.concatenate([-sin, sin, 0.0 * ones], 1)
    cos_t = jnp.concatenate([cos_head, cos_head], 1)
    sin_t = jnp.concatenate([sin_head, sin_head], 1)
    return _to_token_major(cos_t, sin_t)


def _table_transpose_kernel(cos_ref, sin_ref, cos_out, sin_out):
    cos_out[...] = cos_ref[...].T
    sin_out[...] = sin_ref[...].T


def _to_token_major(cos_t, sin_t):
    b, lanes, s = cos_t.shape
    src = pl.BlockSpec((None, lanes, ROW_TILE), lambda i, c: (i, 0, c))
    dst = pl.BlockSpec((None, ROW_TILE, lanes), lambda i, c: (i, c, 0))
    shape = jax.ShapeDtypeStruct((b, s, lanes), F32)
    return pl.pallas_call(
        _table_transpose_kernel,
        grid=(b, s // ROW_TILE),
        in_specs=[src, src],
        out_specs=[dst, dst],
        out_shape=[shape, shape],
        compiler_params=_params("parallel", "parallel"),
        name="rope_tables",
    )(cos_t, sin_t)


def kernel(x, positions, w_in, b_in, conv_w, mlstm_norm_g, attn_sinks, sgu_w_s, sgu_b_s, sgu_norm_g, sgu_norm_b, w_out, ln1_g, ln1_b, ln2_g, ln2_b, ffn_w_gate, ffn_w_up, ffn_w_down, moe_w_router, moe_b_router, moe_w_gate, moe_w_up, moe_w_down):
    bsz, seq, _ = x.shape
    n = bsz * seq
    cos_t, sin_t = _rope_tables(positions)
    tril = jnp.tril(jnp.ones((CHUNK, CHUNK), bool))
    x2d = x.reshape(n, D_MODEL)
    w_p, b_p = _layout_in_proj(w_in, b_in)
    for layer in range(DEPTH):
        proj3 = _inproj(x2d, w_p[layer], b_p[layer]).reshape(bsz, seq, D_PROJ_PAD)
        h_a = _mlstm(proj3, conv_w[layer], mlstm_norm_g[layer][None, :])
        w_tril = jnp.where(tril, sgu_w_s[layer], 0.0).astype(BF16)
        bias_tok = jnp.repeat(sgu_b_s[layer].T, HEAD_DIM, axis=1)
        h_b, h_c = _swa_sgu(proj3, cos_t, sin_t, attn_sinks[layer], w_tril, bias_tok,
                            sgu_norm_g[layer][None, :], sgu_norm_b[layer][None, :])
        wo = w_out[layer]
        wb = wo[MLSTM_W:MLSTM_W + ATTN_W].reshape(ATTN_Q_HEADS, HEAD_DIM, D_MODEL)[np.array(ATTN_HEAD_ORDER)]
        wo = jnp.concatenate([wo[:MLSTM_W], wb.reshape(ATTN_W, D_MODEL), wo[MLSTM_W + ATTN_W:]], 0).astype(BF16)
        j = layer // 2
        dense = layer % 2 == 0
        mixed = _outproj_ln(h_a.reshape(n, MLSTM_W), h_b.reshape(n, ATTN_W), h_c.reshape(n, SGU_W), x2d,
                            wo, ln1_g[layer][None, :], ln1_b[layer][None, :],
                            router=None if dense else _router_operands(moe_w_router[j], moe_b_router[j]))
        g2, b2 = ln2_g[layer][None, :], ln2_b[layer][None, :]
        if dense:
            x2d = _ffn_ln(mixed, ffn_w_gate[j].astype(BF16), ffn_w_up[j].astype(BF16),
                          ffn_w_down[j].astype(BF16), g2, b2)
        else:
            x2d, route, route_t = mixed
            x2d = _moe_ln(x2d, route, route_t, moe_w_gate[j], moe_w_up[j], moe_w_down[j], g2, b2)
    return x2d.reshape(bsz, seq, D_MODEL)
```

```python
import jax
import jax.numpy as jnp
import numpy as np
from jax import lax
from jax.experimental import pallas as pl
from jax.experimental.pallas import tpu as pltpu

F32 = jnp.float32
BF16 = jnp.bfloat16

D_MODEL = 1024
HEAD_DIM = 64
LANES = 128
SUBLANES = 8
MLSTM_HEADS = 6
ATTN_Q_HEADS = 6
ATTN_KV_HEADS = 2
SGU_GROUPS = 4
MLSTM_W = MLSTM_HEADS * HEAD_DIM
ATTN_W = ATTN_Q_HEADS * HEAD_DIM
ATTN_KV_W = ATTN_KV_HEADS * HEAD_DIM
SGU_W = SGU_GROUPS * HEAD_DIM
CHUNK = 128
CONV_WIDTH = 4
ROPE_DIM = HEAD_DIM // 4
ROPE_THETA = 500000.0
N_EXPERTS = 8
TOP_K = 2
DEPTH = 2
DN_ALPHA = (2.0 * DEPTH) ** 0.25
LN_EPS = 1e-5

QK_OFF, V_OFF, O_OFF, AQ_OFF = 0, 768, 1152, 1536
GATE_OFF, AK_OFF, AV_OFF, SU_OFF, SV_OFF = 1920, 2048, 2176, 2304, 2560
D_PROJ_PAD = 2816
ATTN_HEAD_ORDER = (0, 3, 1, 4, 2, 5)

VMEM_LIMIT = 56 * 1024 * 1024

MLSTM_CHUNKS_PER_STEP = 8
SWA_BLOCKS_PER_STEP = 2
ROW_TILE = 512
FFN_ROWS = 2048
FFN_SUB = 1024
FF_TILE = 256
MOE_SUB = 512
MOE_VISIT_SUBS = 9
MOE_FF_TILE = 512


def _params(*sem):
    return pltpu.CompilerParams(dimension_semantics=sem, vmem_limit_bytes=VMEM_LIMIT)


def _lane_lo(shape):
    return lax.broadcasted_iota(jnp.int32, shape, len(shape) - 1) < HEAD_DIM


def _layer_norm_rows(z, g, b):
    mu = jnp.mean(z, axis=-1, keepdims=True)
    zc = z - mu
    var = jnp.mean(zc * zc, axis=-1, keepdims=True)
    return zc * lax.rsqrt(var + LN_EPS) * g + b


def _half_layer_norm(x, lo):
    inv = 1.0 / HEAD_DIM
    s_lo = jnp.sum(jnp.where(lo, x, 0.0), axis=-1, keepdims=True)
    s_all = jnp.sum(x, axis=-1, keepdims=True)
    mu = jnp.where(lo, s_lo, s_all - s_lo) * inv
    xc = x - mu
    sq = xc * xc
    q_lo = jnp.sum(jnp.where(lo, sq, 0.0), axis=-1, keepdims=True)
    q_all = jnp.sum(sq, axis=-1, keepdims=True)
    var = jnp.where(lo, q_lo, q_all - q_lo) * inv
    return xc * lax.rsqrt(var + LN_EPS)


def _split3(a):
    h1 = a.astype(BF16)
    r1 = a - h1.astype(F32)
    h2 = r1.astype(BF16)
    r2 = r1 - h2.astype(F32)
    return h1, h2, r2.astype(BF16)


def _split3_masked(a):
    def top(x):
        bits = lax.bitcast_convert_type(x, jnp.int32) & jnp.int32(-65536)
        return lax.bitcast_convert_type(bits, F32)

    h1 = top(a)
    r1 = a - h1
    h2 = top(r1)
    return h1.astype(BF16), h2.astype(BF16), (r1 - h2).astype(BF16)


def _dot(a, b):
    return jnp.dot(a, b, preferred_element_type=F32)


def _inproj_kernel(x_ref, w_ref, b_ref, o_ref):
    o_ref[...] = _dot(x_ref[...].astype(BF16), w_ref[...]) + b_ref[...]


def _inproj(x2d, w, b):
    n = x2d.shape[0]
    return pl.pallas_call(
        _inproj_kernel,
        grid=(n // ROW_TILE,),
        in_specs=[pl.BlockSpec((ROW_TILE, D_MODEL), lambda i: (i, 0)),
                  pl.BlockSpec((D_MODEL, D_PROJ_PAD), lambda i: (0, 0)),
                  pl.BlockSpec((1, D_PROJ_PAD), lambda i: (0, 0))],
        out_specs=pl.BlockSpec((ROW_TILE, D_PROJ_PAD), lambda i: (i, 0)),
        out_shape=jax.ShapeDtypeStruct((n, D_PROJ_PAD), F32),
        compiler_params=_params("parallel"),
        name="inproj",
    )(x2d, w, b)


def _mlstm_kernel(qk_ref, v_ref, o_ref, gate_ref, cw_ref, g_ref, out_ref, buf_ref, ct_ref, m_ref):
    rows = MLSTM_CHUNKS_PER_STEP * CHUNK

    @pl.when(pl.program_id(1) == 0)
    def _():
        buf_ref[0:8, :] = jnp.zeros((8, 2 * MLSTM_W), F32)
        ct_ref[...] = jnp.zeros_like(ct_ref)
        m_ref[...] = jnp.zeros_like(m_ref)

    buf_ref[8:8 + rows, :] = qk_ref[...]
    ct = [ct_ref[p] for p in range(MLSTM_HEADS // 2)]
    m = [m_ref[h:h + 1, 0:1] for h in range(MLSTM_HEADS)]
    for i in range(MLSTM_CHUNKS_PER_STEP):
        blk = pl.ds(i * CHUNK, CHUNK)
        ct, m = _mlstm_chunk(i * CHUNK, buf_ref, v_ref.at[blk], o_ref.at[blk], gate_ref.at[blk], cw_ref, g_ref,
                             out_ref.at[blk], ct, m)
    buf_ref[0:8, :] = qk_ref[rows - 8:rows, :]
    for p in range(MLSTM_HEADS // 2):
        ct_ref[p] = ct[p]
    for h in range(MLSTM_HEADS):
        m_ref[h:h + 1, :] = jnp.broadcast_to(m[h], (1, LANES))


def _mlstm_chunk(row0, buf_ref, v_ref, o_ref, gate_ref, cw_ref, g_ref, out_ref, ct_in, m_in):
    cw = cw_ref[...]
    base = row0 + 8 - (CONV_WIDTH - 1)
    acc = cw[0:1, :] * buf_ref[base:base + CHUNK, :]
    for j in range(1, CONV_WIDTH):
        acc = acc + cw[j:j + 1, :] * buf_ref[base + j:base + j + CHUNK, :]
    qk = jax.nn.silu(acc)

    gates = gate_ref[...]
    lf = jax.nn.log_sigmoid(gates)
    row = lax.broadcasted_iota(jnp.int32, (CHUNK, CHUNK), 0)
    col = lax.broadcasted_iota(jnp.int32, (CHUNK, CHUNK), 1)
    causal = col <= row
    tri = causal.astype(BF16)
    l1, l2, l3 = _split3(lf)
    cum = _dot(tri, l1) + _dot(tri, l2) + _dot(tri, l3)
    gates_t = gates.T
    cum_t = cum.T
    cum_parts = _split3_masked(cum)

    lo = _lane_lo((CHUNK, LANES))
    ct_out = []
    m_out = []
    for p in range(MLSTM_HEADS // 2):
        sl = slice(p * LANES, (p + 1) * LANES)
        q_slab = qk[:, sl] * (HEAD_DIM ** -0.5)
        k_slab = qk[:, MLSTM_W + p * LANES:MLSTM_W + (p + 1) * LANES]
        kt_slab = k_slab.T
        v_slab = v_ref[:, sl]
        ct_pair = ct_in[p]
        ct_pair_b = ct_pair.astype(BF16)
        halves = []
        new_ct = []
        for half in range(2):
            h = 2 * p + half
            sel = lo if half == 0 else jnp.logical_not(lo)
            li_row = gates_t[h:h + 1, :]
            bc_row = cum_t[MLSTM_HEADS + h:MLSTM_HEADS + h + 1, :]
            pick = (row == MLSTM_HEADS + h).astype(BF16)
            bc_col = _dot(cum_parts[0], pick) + _dot(cum_parts[1], pick) + _dot(cum_parts[2], pick)
            b_tot = bc_row[:, CHUNK - 1:CHUNK]
            m_prev = m_in[h]

            dmat = jnp.where(causal, bc_col - bc_row + li_row, -jnp.inf)
            m_inter = bc_col + m_prev
            m_row = jnp.maximum(m_inter, jnp.max(dmat, axis=-1, keepdims=True))
            q_m = jnp.where(sel, q_slab, 0.0).astype(BF16)
            s = _dot(q_m, kt_slab.astype(BF16))
            pmat = (s * jnp.exp(dmat - m_row)).astype(BF16)
            w_inter = jnp.exp(m_inter - m_row)
            v_aug = jnp.where(sel, v_slab, 1.0).astype(BF16)
            halves.append((_dot(pmat, v_aug) + w_inter * _dot(q_m, ct_pair_b), jnp.exp(-m_row)))

            a_row = b_tot - bc_row + li_row
            m_loc = jnp.max(a_row, axis=-1, keepdims=True)
            w_row = jnp.exp(a_row - m_loc)
            kt_h = kt_slab[half * HEAD_DIM:(half + 1) * HEAD_DIM, :]
            ct_loc = _dot((kt_h * w_row).astype(BF16), v_aug)
            m_new = jnp.maximum(b_tot + m_prev, m_loc)
            s_old = jnp.exp(b_tot + m_prev - m_new)
            s_loc = jnp.exp(m_loc - m_new)
            new_ct.append(s_old * ct_pair[half * HEAD_DIM:(half + 1) * HEAD_DIM, :] + s_loc * ct_loc)
            m_out.append(m_new)

        ct_out.append(jnp.concatenate(new_ct, axis=0))
        (r_even, stab_even), (r_odd, stab_odd) = halves
        num = jnp.where(lo, r_even, r_odd)
        den = pltpu.roll(jnp.where(lo, r_odd, r_even), HEAD_DIM, 1)
        hh = num / jnp.maximum(jnp.abs(den), jnp.where(lo, stab_even, stab_odd))
        hn = _half_layer_norm(hh, lo) * g_ref[:, sl]
        out_ref[:, sl] = (hn * jax.nn.sigmoid(o_ref[:, sl])).astype(out_ref.dtype)
    return ct_out, m_out


def _mlstm(proj3, conv_w, norm_g):
    b, s, _ = proj3.shape
    rows = MLSTM_CHUNKS_PER_STEP * CHUNK
    assert s % rows == 0
    blk = lambda w, off: pl.BlockSpec((None, rows, w), lambda i, c: (i, c, off // w))
    return pl.pallas_call(
        _mlstm_kernel,
        grid=(b, s // rows),
        in_specs=[blk(2 * MLSTM_W, QK_OFF), blk(MLSTM_W, V_OFF), blk(MLSTM_W, O_OFF), blk(LANES, GATE_OFF),
                  pl.BlockSpec((CONV_WIDTH, 2 * MLSTM_W), lambda i, c: (0, 0)),
                  pl.BlockSpec((1, MLSTM_W), lambda i, c: (0, 0))],
        out_specs=pl.BlockSpec((None, rows, MLSTM_W), lambda i, c: (i, c, 0)),
        out_shape=jax.ShapeDtypeStruct((b, s, MLSTM_W), BF16),
        scratch_shapes=[pltpu.VMEM((8 + rows, 2 * MLSTM_W), F32),
                        pltpu.VMEM((MLSTM_HEADS // 2, LANES, LANES), F32),
                        pltpu.VMEM((8, LANES), F32)],
        compiler_params=_params("parallel", "arbitrary"),
        name="mlstm",
    )(proj3, proj3, proj3, proj3, conv_w, norm_g)


def _rope(x, cos_t, sin_t, first):
    return x * cos_t + jnp.where(first, pltpu.roll(x, LANES - ROPE_DIM // 2, 1),
                                 pltpu.roll(x, ROPE_DIM // 2, 1)) * sin_t


def _swa_kernel(sink_ref, q_ref, k_ref, v_ref, cos_ref, sin_ref, su_ref, sv_ref, ws_ref, bias_ref, sg_ref, sb_ref,
                out_ref, gated_ref, kt_ref, vv_ref):
    first_step = pl.program_id(1) == 0

    @pl.when(first_step)
    def _():
        kt_ref[...] = jnp.zeros_like(kt_ref)
        vv_ref[...] = jnp.zeros_like(vv_ref)

    kt_prev = kt_ref[...]
    v_prev = vv_ref[...]
    for i in range(SWA_BLOCKS_PER_STEP):
        rows = pl.ds(i * CHUNK, CHUNK)
        hide_previous = jnp.where(first_step, 2 * CHUNK, 0) if i == 0 else 0
        kt_prev, v_prev = _swa_block(hide_previous, sink_ref, q_ref.at[rows], k_ref.at[rows], v_ref.at[rows],
                                     cos_ref.at[rows], sin_ref.at[rows], out_ref.at[rows], kt_prev, v_prev)
        _sgu_chunk(slice(i * CHUNK, (i + 1) * CHUNK), su_ref, sv_ref, ws_ref, bias_ref, sg_ref, sb_ref, gated_ref)
    kt_ref[...] = kt_prev
    vv_ref[...] = v_prev


def _swa_block(first_block_shift, sink_ref, q_ref, k_ref, v_ref, cos_ref, sin_ref, out_ref, kt_prev, v_prev):
    cos_t = cos_ref[...]
    sin_t = sin_ref[...]
    lane = lax.broadcasted_iota(jnp.int32, (CHUNK, LANES), 1)
    lo = lane < HEAD_DIM
    first = (lane % HEAD_DIM) < ROPE_DIM // 2

    kt_cur = _rope(k_ref[...], cos_t, sin_t, first).T.astype(BF16)
    v_cur = v_ref[...].astype(BF16)
    kt = jnp.concatenate([kt_prev, kt_cur], axis=1)
    vv = jnp.concatenate([v_prev, v_cur], axis=0)

    row = lax.broadcasted_iota(jnp.int32, (CHUNK, 2 * CHUNK), 0)
    col = lax.broadcasted_iota(jnp.int32, (CHUNK, 2 * CHUNK), 1)
    visible = jnp.logical_or(jnp.logical_and(col < CHUNK, col > row + first_block_shift),
                             jnp.logical_and(col >= CHUNK, col - CHUNK <= row))

    for j in range(ATTN_Q_HEADS // 2):
        sl = slice(j * LANES, (j + 1) * LANES)
        q_slab = _rope(q_ref[:, sl], cos_t, sin_t, first) * (HEAD_DIM ** -0.5)
        outs = []
        for half in range(2):
            sel = lo if half == 0 else jnp.logical_not(lo)
            sink = sink_ref[ATTN_HEAD_ORDER[2 * j + half]]
            q_m = jnp.where(sel, q_slab, 0.0).astype(BF16)
            s = jnp.where(visible, _dot(q_m, kt), -jnp.inf)
            mx = jnp.maximum(jnp.max(s, axis=-1, keepdims=True), sink)
            pexp = jnp.exp(s - mx)
            denom = jnp.sum(pexp, axis=-1, keepdims=True) + jnp.exp(sink - mx)
            outs.append(_dot(pexp.astype(BF16), vv) / denom)
        out_ref[:, sl] = jnp.where(lo, outs[0], outs[1]).astype(out_ref.dtype)

    return kt_cur, v_cur


def _swa_sgu(proj3, cos_t, sin_t, sinks, w_tril, bias_tok, norm_g, norm_b):
    b, s, _ = proj3.shape
    rows = SWA_BLOCKS_PER_STEP * CHUNK
    assert s % rows == 0
    blk = lambda w, off: pl.BlockSpec((None, rows, w), lambda i, c: (i, c, off // w))
    tab = pl.BlockSpec((None, rows, LANES), lambda i, c: (i, c, 0))
    const2 = lambda shape: pl.BlockSpec(shape, lambda i, c: (0,) * len(shape))
    out = lambda w: pl.BlockSpec((None, rows, w), lambda i, c: (i, c, 0))
    return pl.pallas_call(
        _swa_kernel,
        grid=(b, s // rows),
        in_specs=[pl.BlockSpec(memory_space=pltpu.SMEM),
                  blk(ATTN_W, AQ_OFF), blk(ATTN_KV_W, AK_OFF), blk(ATTN_KV_W, AV_OFF), tab, tab,
                  blk(SGU_W, SU_OFF), blk(SGU_W, SV_OFF), const2((SGU_GROUPS, CHUNK, CHUNK)),
                  const2((CHUNK, SGU_W)), const2((1, SGU_W)), const2((1, SGU_W))],
        out_specs=[out(ATTN_W), out(SGU_W)],
        out_shape=[jax.ShapeDtypeStruct((b, s, ATTN_W), BF16), jax.ShapeDtypeStruct((b, s, SGU_W), BF16)],
        scratch_shapes=[pltpu.VMEM((LANES, CHUNK), BF16), pltpu.VMEM((CHUNK, LANES), BF16)],
        compiler_params=_params("parallel", "arbitrary"),
        name="swa_sgu",
    )(sinks, proj3, proj3, proj3, cos_t, sin_t, proj3, proj3, w_tril, bias_tok, norm_g, norm_b)


def _sgu_chunk(rows, u_ref, v_ref, w_ref, bias_ref, g_ref, b_ref, out_ref):
    lo = _lane_lo((CHUNK, LANES))
    for j in range(SGU_GROUPS // 2):
        sl = slice(j * LANES, (j + 1) * LANES)
        u = jax.nn.gelu(u_ref[rows, sl])
        v = jax.nn.gelu(v_ref[rows, sl])
        vn = (_half_layer_norm(v, lo) * g_ref[:, sl] + b_ref[:, sl]).astype(BF16)
        mixed = jnp.where(lo, _dot(w_ref[2 * j], vn), _dot(w_ref[2 * j + 1], vn)) + bias_ref[:, sl]
        out_ref[rows, sl] = (u * mixed).astype(out_ref.dtype)


def _outproj_kernel(ha_ref, hb_ref, hc_ref, x_ref, w_ref, g_ref, b_ref, *rest):
    cat_ref = rest[-1]
    cat_ref[:, 0:MLSTM_W] = ha_ref[...]
    cat_ref[:, MLSTM_W:MLSTM_W + ATTN_W] = hb_ref[...]
    cat_ref[:, MLSTM_W + ATTN_W:] = hc_ref[...]
    mix = _dot(cat_ref[...], w_ref[...])
    y = _layer_norm_rows(DN_ALPHA * x_ref[...] + mix, g_ref[...], b_ref[...])
    rest = rest[:-1]
    if len(rest) == 1:
        rest[0][...] = y
        return
    w_split_ref, rb_ref, o_ref, route_ref, route_t_ref = rest
    o_ref[...] = y
    route_ref[...], route_t_ref[...] = _route_rows(y, w_split_ref[...], rb_ref[...])


def _outproj_ln(ha, hb, hc, x2d, w, g, b, router=None):
    n = x2d.shape[0]
    rows = lambda w: pl.BlockSpec((ROW_TILE, w), lambda i: (i, 0))
    const = lambda shape: pl.BlockSpec(shape, lambda i: (0, 0))
    in_specs = [rows(MLSTM_W), rows(ATTN_W), rows(SGU_W), rows(D_MODEL),
                const((D_MODEL, D_MODEL)), const((1, D_MODEL)), const((1, D_MODEL))]
    out_specs = [rows(D_MODEL)]
    out_shape = [jax.ShapeDtypeStruct((n, D_MODEL), F32)]
    operands = (ha, hb, hc, x2d, w, g, b)
    if router is not None:
        in_specs += [const((D_MODEL, 2 * LANES)), const((1, LANES))]
        out_specs += [rows(LANES), pl.BlockSpec((N_EXPERTS, ROW_TILE), lambda i: (0, i))]
        out_shape += [jax.ShapeDtypeStruct((n, LANES), F32), jax.ShapeDtypeStruct((N_EXPERTS, n), F32)]
        operands += tuple(router)
    out = pl.pallas_call(
        _outproj_kernel,
        grid=(n // ROW_TILE,),
        in_specs=in_specs,
        out_specs=out_specs,
        out_shape=out_shape,
        scratch_shapes=[pltpu.VMEM((ROW_TILE, D_MODEL), BF16)],
        compiler_params=_params("parallel"),
        name="outproj_ln" if router is None else "outproj_ln_route",
    )(*operands)
    return out[0] if router is None else tuple(out)


def _ffn_kernel(x_ref, wg_ref, wu_ref, wd_ref, g_ref, b_ref, o_ref, xb_ref, wgu_ref):
    j = pl.program_id(1)
    last = pl.num_programs(1) - 1
    n_sub = FFN_ROWS // FFN_SUB
    wgu_ref[:, :FF_TILE] = wg_ref[...]
    wgu_ref[:, FF_TILE:] = wu_ref[...]

    def sub_rows(m):
        return pl.ds(pl.multiple_of(m * FFN_SUB, FFN_SUB), FFN_SUB)

    def ffn_tile(m):
        gu = _dot(xb_ref[sub_rows(m), :], wgu_ref[...])
        hidden = (jax.nn.silu(gu[:, :FF_TILE]) * gu[:, FF_TILE:]).astype(BF16)
        return _dot(hidden, wd_ref[...])

    @pl.when(j == 0)
    def _():
        def body(m, carry):
            xb_ref[sub_rows(m), :] = x_ref[sub_rows(m), :].astype(BF16)
            o_ref[sub_rows(m), :] = ffn_tile(m)
            return carry

        lax.fori_loop(0, n_sub, body, 0)

    @pl.when(jnp.logical_and(j > 0, j < last))
    def _():
        def body(m, carry):
            o_ref[sub_rows(m), :] += ffn_tile(m)
            return carry

        lax.fori_loop(0, n_sub, body, 0)

    @pl.when(j == last)
    def _():
        def body(m, carry):
            rows = sub_rows(m)
            ff = o_ref[rows, :] + ffn_tile(m)
            o_ref[rows, :] = _layer_norm_rows(DN_ALPHA * x_ref[rows, :] + ff, g_ref[...], b_ref[...])
            return carry

        lax.fori_loop(0, n_sub, body, 0)


def _ffn_ln(x2d, wg, wu, wd, g, b):
    n = x2d.shape[0]
    d_ff = wd.shape[0]
    assert d_ff // FF_TILE >= 2
    tm = FFN_ROWS
    return pl.pallas_call(
        _ffn_kernel,
        grid=(n // tm, d_ff // FF_TILE),
        in_specs=[pl.BlockSpec((tm, D_MODEL), lambda i, j: (i, 0)),
                  pl.BlockSpec((D_MODEL, FF_TILE), lambda i, j: (0, j)),
                  pl.BlockSpec((D_MODEL, FF_TILE), lambda i, j: (0, j)),
                  pl.BlockSpec((FF_TILE, D_MODEL), lambda i, j: (j, 0)),
                  pl.BlockSpec((1, D_MODEL), lambda i, j: (0, 0)),
                  pl.BlockSpec((1, D_MODEL), lambda i, j: (0, 0))],
        out_specs=pl.BlockSpec((tm, D_MODEL), lambda i, j: (i, 0)),
        out_shape=jax.ShapeDtypeStruct((n, D_MODEL), F32),
        scratch_shapes=[pltpu.VMEM((tm, D_MODEL), BF16), pltpu.VMEM((D_MODEL, 2 * FF_TILE), BF16)],
        compiler_params=_params("parallel", "arbitrary"),
        name="ffn_ln",
    )(x2d, wg, wu, wd, g, b)


def _route_rows(x, w_split, bias):
    n = x.shape[0]
    xh = x.astype(BF16)
    xl = (x - xh.astype(F32)).astype(BF16)
    prod = _dot(jnp.concatenate([xh, xl], axis=0), w_split)
    logits = prod[:n, :LANES] + prod[:n, LANES:] + prod[n:, :LANES] + prod[n:, LANES:] + bias
    lt = logits.T[0:N_EXPERTS, :]
    expert = lax.broadcasted_iota(jnp.int32, lt.shape, 0)
    m1 = jnp.max(lt, axis=0, keepdims=True)
    i1 = jnp.min(jnp.where(lt == m1, expert, N_EXPERTS), axis=0, keepdims=True)
    rest = jnp.where(expert == i1, -jnp.inf, lt)
    m2 = jnp.max(rest, axis=0, keepdims=True)
    i2 = jnp.min(jnp.where(rest == m2, expert, N_EXPERTS), axis=0, keepdims=True)
    e2 = jnp.exp(m2 - m1)
    g1 = 1.0 / (1.0 + e2)
    g2 = e2 / (1.0 + e2)
    route_t = jnp.where(expert == 0, g1, jnp.where(expert == 1, g2, jnp.where(
        expert == 2, i1.astype(F32), jnp.where(expert == 3, i2.astype(F32), 0.0))))
    padded = jnp.concatenate([route_t, jnp.zeros((LANES - N_EXPERTS, n), F32)], axis=0)
    return padded.T, route_t


TOKEN_TILE = (SUBLANES, D_MODEL // SUBLANES)


def _dispatch_kernel(slot0_ref, slot1_ref, ends_ref, x_ref, o_hbm, stage_ref, zero_ref, sem, zero_sem):
    t = pl.program_id(0)
    last = pl.num_programs(0) - 1
    s = t % 2

    def wait_tile(ss):
        for _ in range(TOP_K):
            pltpu.make_async_copy(zero_ref, o_hbm.at[pl.ds(0, MOE_SUB)], sem.at[ss]).wait()

    @pl.when(t == 0)
    def _():
        zero_ref[...] = jnp.zeros_like(zero_ref)

        def zero_sub_tile(m, carry):
            dst = o_hbm.at[pl.ds(pl.multiple_of(m * MOE_SUB, MOE_SUB), MOE_SUB)]
            cp = pltpu.make_async_copy(zero_ref, dst, zero_sem)
            cp.start()
            cp.wait()
            return carry

        for e in range(N_EXPERTS):
            @pl.when(ends_ref[e] >= MOE_SUB)
            def _():
                zero_sub_tile(ends_ref[e] // MOE_SUB - 1, 0)

        lax.fori_loop(ends_ref[N_EXPERTS - 1] // MOE_SUB, o_hbm.shape[0] // MOE_SUB, zero_sub_tile, 0)

    @pl.when(t >= 2)
    def _():
        wait_tile(s)

    stage_ref[s] = x_ref[...].reshape((MOE_SUB,) + TOKEN_TILE)

    def issue(i, carry):
        for u in range(SUBLANES):
            r = i * SUBLANES + u
            for k, slot_ref in enumerate((slot0_ref, slot1_ref)):
                pltpu.make_async_copy(stage_ref.at[s, r], o_hbm.at[slot_ref[t * MOE_SUB + r]],
                                      sem.at[s]).start(priority=k)
        return carry

    lax.fori_loop(0, MOE_SUB // SUBLANES, issue, 0)

    @pl.when(t == last)
    def _():
        wait_tile(s)
        wait_tile(1 - s)


def _dispatch_rows(x2d, slots, ends, n_rows):
    n = x2d.shape[0]
    assert n // MOE_SUB >= 2
    return pl.pallas_call(
        _dispatch_kernel,
        grid_spec=pltpu.PrefetchScalarGridSpec(
            num_scalar_prefetch=3,
            grid=(n // MOE_SUB,),
            in_specs=[pl.BlockSpec((MOE_SUB, D_MODEL), lambda t, s0, s1, en: (t, 0))],
            out_specs=pl.BlockSpec(memory_space=pl.ANY),
            scratch_shapes=[pltpu.VMEM((2, MOE_SUB) + TOKEN_TILE, F32), pltpu.VMEM((MOE_SUB,) + TOKEN_TILE, F32),
                            pltpu.SemaphoreType.DMA((2,)), pltpu.SemaphoreType.DMA(())]),
        out_shape=jax.ShapeDtypeStruct((n_rows,) + TOKEN_TILE, F32),
        compiler_params=_params("arbitrary"),
        name="moe_dispatch",
    )(slots[0], slots[1], ends, x2d)


def _moe_kernel(exp_ref, row0_ref, nsub_ref, tail_ref, x_hbm, wg_ref, wu_ref, wd_ref, y_hbm,
                xb_ref, acc_ref, wgu_ref, wdb_ref, stage_ref, in_sem, out_sem):
    v = pl.program_id(0)
    j = pl.program_id(1)
    last = pl.num_programs(1) - 1
    n_sub = nsub_ref[v]
    row0 = row0_ref[v]

    def sub_rows(m):
        return pl.ds(pl.multiple_of(m * MOE_SUB, MOE_SUB), MOE_SUB)

    def hbm_rows(m):
        return pl.ds(pl.multiple_of(row0 + m * MOE_SUB, MOE_SUB), MOE_SUB)

    @pl.when(jnp.logical_and(v == 0, j == 0))
    def _():
        stage_ref[0] = jnp.zeros((MOE_SUB,) + TOKEN_TILE, F32)

        def zero_sub_tile(m, carry):
            cp = pltpu.make_async_copy(stage_ref.at[0], y_hbm.at[sub_rows(m)], out_sem.at[0])
            cp.start()
            cp.wait()
            return carry

        lax.fori_loop(tail_ref[0], y_hbm.shape[0] // MOE_SUB, zero_sub_tile, 0)

    @pl.when(n_sub > 0)
    def _():
        wgu_ref[:, :MOE_FF_TILE] = wg_ref[...].astype(BF16)
        wgu_ref[:, MOE_FF_TILE:] = wu_ref[...].astype(BF16)
        wdb_ref[...] = wd_ref[...].astype(BF16)

        def ffn_rows(rows):
            gu = _dot(xb_ref[rows, :], wgu_ref[...])
            hidden = (jax.nn.silu(gu[:, :MOE_FF_TILE]) * gu[:, MOE_FF_TILE:]).astype(BF16)
            return _dot(hidden, wdb_ref[...])

        def ffn_tile(m):
            return ffn_rows(sub_rows(m))

        def accumulate(after_sub_tile):
            def pair(k, carry):
                rows = pl.ds(pl.multiple_of(2 * k * MOE_SUB, 2 * MOE_SUB), 2 * MOE_SUB)
                acc_ref[rows, :] += ffn_rows(rows)
                after_sub_tile(2 * k)
                after_sub_tile(2 * k + 1)
                return carry

            lax.fori_loop(0, n_sub // 2, pair, 0)

            @pl.when(n_sub % 2 == 1)
            def _():
                acc_ref[sub_rows(n_sub - 1), :] += ffn_tile(n_sub - 1)
                after_sub_tile(n_sub - 1)

        def y_copy(m):
            return pltpu.make_async_copy(stage_ref.at[m % 2], y_hbm.at[hbm_rows(m)], out_sem.at[m % 2])

        def send_sub_tile(m):
            @pl.when(m >= 2)
            def _():
                y_copy(m - 2).wait()

            stage_ref[m % 2] = acc_ref[sub_rows(m), :].reshape((MOE_SUB,) + TOKEN_TILE)
            y_copy(m).start()

        @pl.when(j == 0)
        def _():
            def x_copy(m):
                return pltpu.make_async_copy(x_hbm.at[hbm_rows(m)], stage_ref.at[m % 2], in_sem.at[m % 2])

            x_copy(0).start()

            def body(m, carry):
                @pl.when(m + 1 < n_sub)
                def _():
                    x_copy(m + 1).start()

                x_copy(m).wait()
                xb_ref[sub_rows(m), :] = stage_ref[m % 2].reshape(MOE_SUB, D_MODEL).astype(BF16)
                acc_ref[sub_rows(m), :] = ffn_tile(m)
                return carry

            lax.fori_loop(0, n_sub, body, 0)

        @pl.when(jnp.logical_and(j > 0, j < last))
        def _():
            accumulate(lambda m: None)

        @pl.when(j == last)
        def _():
            accumulate(send_sub_tile)

            @pl.when(n_sub >= 2)
            def _():
                y_copy(n_sub - 2).wait()

            y_copy(n_sub - 1).wait()


def _moe_grouped(xs, wg, wu, wd, visit_exp, visit_row0, visit_nsub, tail_sub):
    n_rows = xs.shape[0]
    n_visits = visit_exp.shape[0]
    d_ff = wg.shape[2]
    assert d_ff // MOE_FF_TILE >= 2
    rows = MOE_VISIT_SUBS * MOE_SUB
    w_in = lambda v, j, e, r, ns, tl: (e[v], 0, j)
    w_out = lambda v, j, e, r, ns, tl: (e[v], j, 0)
    return pl.pallas_call(
        _moe_kernel,
        grid_spec=pltpu.PrefetchScalarGridSpec(
            num_scalar_prefetch=4,
            grid=(n_visits, d_ff // MOE_FF_TILE),
            in_specs=[pl.BlockSpec(memory_space=pl.ANY),
                      pl.BlockSpec((None, D_MODEL, MOE_FF_TILE), w_in),
                      pl.BlockSpec((None, D_MODEL, MOE_FF_TILE), w_in),
                      pl.BlockSpec((None, MOE_FF_TILE, D_MODEL), w_out)],
            out_specs=pl.BlockSpec(memory_space=pl.ANY),
            scratch_shapes=[pltpu.VMEM((rows, D_MODEL), BF16), pltpu.VMEM((rows, D_MODEL), F32),
                            pltpu.VMEM((D_MODEL, 2 * MOE_FF_TILE), BF16),
                            pltpu.VMEM((MOE_FF_TILE, D_MODEL), BF16),
                            pltpu.VMEM((2, MOE_SUB) + TOKEN_TILE, F32),
                            pltpu.SemaphoreType.DMA((2,)), pltpu.SemaphoreType.DMA((2,))]),
        out_shape=jax.ShapeDtypeStruct((n_rows,) + TOKEN_TILE, F32),
        compiler_params=_params("arbitrary", "arbitrary"),
        name="moe_grouped",
    )(visit_exp, visit_row0, visit_nsub, tail_sub, xs, wg, wu, wd)


def _combine_kernel(slot0_ref, slot1_ref, x_ref, gate_ref, y_hbm, g_ref, b_ref, o_ref, buf_ref, sem):
    t = pl.program_id(0)
    s = t % 2

    def issue_tile(tt, ss):
        def issue(i, carry):
            for u in range(SUBLANES):
                r = i * SUBLANES + u
                for k, slot_ref in enumerate((slot0_ref, slot1_ref)):
                    pltpu.make_async_copy(y_hbm.at[slot_ref[tt * MOE_SUB + r]], buf_ref.at[ss, k, r],
                                          sem.at[ss]).start(priority=k)
            return carry

        lax.fori_loop(0, MOE_SUB // SUBLANES, issue, 0)

    @pl.when(t == 0)
    def _():
        issue_tile(0, 0)

    @pl.when(t + 1 < pl.num_programs(0))
    def _():
        issue_tile(t + 1, 1 - s)

    for k in range(TOP_K):
        pltpu.make_async_copy(y_hbm.at[pl.ds(0, MOE_SUB)], buf_ref.at[s, k], sem.at[s]).wait()
    gate = gate_ref[...]
    y0 = buf_ref[s, 0].reshape(MOE_SUB, D_MODEL)
    y1 = buf_ref[s, 1].reshape(MOE_SUB, D_MODEL)
    ff = gate[:, 0:1] * y0 + gate[:, 1:2] * y1
    o_ref[...] = _layer_norm_rows(DN_ALPHA * x_ref[...] + ff, g_ref[...], b_ref[...])


def _combine_ln(x2d, route, slots, ys, g, b):
    n = x2d.shape[0]
    return pl.pallas_call(
        _combine_kernel,
        grid_spec=pltpu.PrefetchScalarGridSpec(
            num_scalar_prefetch=2,
            grid=(n // MOE_SUB,),
            in_specs=[pl.BlockSpec((MOE_SUB, D_MODEL), lambda t, s0, s1: (t, 0)),
                      pl.BlockSpec((MOE_SUB, LANES), lambda t, s0, s1: (t, 0)),
                      pl.BlockSpec(memory_space=pl.ANY),
                      pl.BlockSpec((1, D_MODEL), lambda t, s0, s1: (0, 0)),
                      pl.BlockSpec((1, D_MODEL), lambda t, s0, s1: (0, 0))],
            out_specs=pl.BlockSpec((MOE_SUB, D_MODEL), lambda t, s0, s1: (t, 0)),
            scratch_shapes=[pltpu.VMEM((2, TOP_K, MOE_SUB) + TOKEN_TILE, F32), pltpu.SemaphoreType.DMA((2,))]),
        out_shape=jax.ShapeDtypeStruct((n, D_MODEL), F32),
        compiler_params=_params("arbitrary"),
        name="moe_combine_ln",
    )(slots[0], slots[1], x2d, route, ys, g, b)


def _routing_tables(route_t, n):
    idx = route_t[2:4].astype(jnp.int32)
    expert = jnp.arange(N_EXPERTS, dtype=jnp.int32)[:, None]
    chosen = [idx[k][None, :] == expert for k in range(TOP_K)]
    onehot = jnp.logical_or(chosen[0], chosen[1]).astype(jnp.int32)
    rank = jnp.cumsum(onehot, axis=1) - onehot
    counts = jnp.sum(onehot, axis=1)
    padded = ((counts + MOE_SUB - 1) // MOE_SUB) * MOE_SUB
    ends = jnp.cumsum(padded)
    starts = ends - padded
    place = starts[:, None] + rank
    slot = [jnp.sum(jnp.where(chosen[k], place, 0), axis=0).astype(jnp.int32) for k in range(TOP_K)]

    n_rows = -(-(n * TOP_K + N_EXPERTS * (MOE_SUB - 1)) // MOE_SUB) * MOE_SUB

    visit_rows = MOE_VISIT_SUBS * MOE_SUB
    max_chunks = -(-n_rows // visit_rows)
    chunk = jnp.arange(max_chunks, dtype=jnp.int32)[None, :]
    left = padded[:, None] - chunk * visit_rows
    valid = (left > 0).reshape(-1)
    n_visits = n_rows // visit_rows + N_EXPERTS
    order = jnp.argsort(jnp.logical_not(valid), stable=True)[:n_visits]
    n_valid = jnp.sum(valid.astype(jnp.int32))
    live = jnp.arange(n_visits) < n_valid
    order = jnp.where(live, order, order[jnp.maximum(n_valid - 1, 0)])
    v_exp = (order // max_chunks).astype(jnp.int32)
    v_chunk = (order % max_chunks).astype(jnp.int32)
    v_row0 = jnp.where(live, starts[v_exp] + v_chunk * visit_rows, 0).astype(jnp.int32)
    v_nsub = jnp.where(live, jnp.minimum(left.reshape(-1)[order], visit_rows) // MOE_SUB, 0).astype(jnp.int32)
    tail_sub = (ends[N_EXPERTS - 1:] // MOE_SUB).astype(jnp.int32)
    return slot, ends.astype(jnp.int32), n_rows, (v_exp, v_row0, v_nsub, tail_sub)


def _router_operands(w_router, b_router):
    w_pad = jnp.zeros((D_MODEL, LANES), F32).at[:, :N_EXPERTS].set(w_router)
    w_hi = w_pad.astype(BF16)
    w_lo = (w_pad - w_hi.astype(F32)).astype(BF16)
    b_pad = jnp.zeros((1, LANES), F32).at[0, :N_EXPERTS].set(b_router)
    return jnp.concatenate([w_hi, w_lo], axis=1), b_pad


def _moe_ln(x2d, route, route_t, wg, wu, wd, g, b):
    n = x2d.shape[0]
    slots, ends, n_rows, visits = _routing_tables(route_t, n)
    xs = _dispatch_rows(x2d, slots, ends, n_rows)
    ys = _moe_grouped(xs, wg, wu, wd, *visits)
    return _combine_ln(x2d, route, slots, ys, g, b)


_SRC_GATES = 4 * MLSTM_W
_SRC_AQ = _SRC_GATES + 2 * MLSTM_HEADS
_SRC_AK = _SRC_AQ + ATTN_W
_IN_PROJ_MOVES = (
    ((0, 0, 4 * MLSTM_W),)
    + tuple((_SRC_AQ + h * HEAD_DIM, AQ_OFF + i * HEAD_DIM, HEAD_DIM) for i, h in enumerate(ATTN_HEAD_ORDER))
    + ((_SRC_AK, AK_OFF, ATTN_KV_W), (_SRC_AK + ATTN_KV_W, AV_OFF, ATTN_KV_W),
       (_SRC_AK + 2 * ATTN_KV_W, SU_OFF, SGU_W), (_SRC_AK + 2 * ATTN_KV_W + SGU_W, SV_OFF, SGU_W)))
D_PROJ = _SRC_AK + 2 * ATTN_KV_W + 2 * SGU_W


def _relayout_columns(src, dst_dtype):
    source = np.full((D_PROJ_PAD,), D_PROJ, np.int32)
    for s, d, w in _IN_PROJ_MOVES + ((_SRC_GATES, GATE_OFF, 2 * MLSTM_HEADS),):
        source[d:d + w] = np.arange(s, s + w)
    padded = jnp.concatenate([src, jnp.zeros(src.shape[:-1] + (1,), src.dtype)], axis=-1)
    return jnp.take(padded, source, axis=-1).astype(dst_dtype)


def _w_layout_kernel(w_ref, o_ref):
    for s, d, w in _IN_PROJ_MOVES:
        o_ref[:, d:d + w] = w_ref[:, s:s + w].astype(o_ref.dtype)
    pad = jnp.zeros((w_ref.shape[0], LANES - 2 * MLSTM_HEADS), F32)
    gates = jnp.concatenate([w_ref[:, _SRC_GATES:_SRC_GATES + 2 * MLSTM_HEADS], pad], axis=1)
    o_ref[:, GATE_OFF:GATE_OFF + LANES] = gates.astype(o_ref.dtype)


def _layout_in_proj(w_in, b_in):
    rows = D_MODEL // 4

    def one_layer(layer):
        return pl.pallas_call(
            _w_layout_kernel,
            grid=(D_MODEL // rows,),
            in_specs=[pl.BlockSpec((None, rows, D_PROJ), lambda i: (layer, i, 0))],
            out_specs=pl.BlockSpec((rows, D_PROJ_PAD), lambda i: (i, 0)),
            out_shape=jax.ShapeDtypeStruct((D_MODEL, D_PROJ_PAD), BF16),
            compiler_params=_params("parallel"),
            name="w_in_layout",
        )(w_in)

    return [one_layer(layer) for layer in range(w_in.shape[0])], _relayout_columns(b_in, F32)[:, None, :]


def _rope_tables(positions):
    inv_freq = ROPE_THETA ** (-jnp.arange(0, ROPE_DIM, 2, dtype=F32) / ROPE_DIM)
    ang = inv_freq[None, :, None] * positions.astype(F32)[:, None, :]
    cos, sin = jnp.cos(ang), jnp.sin(ang)
    ones = jnp.ones((ang.shape[0], HEAD_DIM - ROPE_DIM, ang.shape[2]), F32)
    cos_head = jnp.concatenate([cos, cos, ones], 1)
    sin_head = jnp.concatenate([-sin, sin, 0.0 * ones], 1)
    cos_t = jnp.concatenate([cos_head, cos_head], 1)
    sin_t = jnp.concatenate([sin_head, sin_head], 1)
    return _to_token_major(cos_t, sin_t)


def _table_transpose_kernel(cos_ref, sin_ref, cos_out, sin_out):
    cos_out[...] = cos_ref[...].T
    sin_out[...] = sin_ref[...].T


def _to_token_major(cos_t, sin_t):
    b, lanes, s = cos_t.shape
    rows = min(s, FFN_ROWS)
    src = pl.BlockSpec((None, lanes, rows), lambda i, c: (i, 0, c))
    dst = pl.BlockSpec((None, rows, lanes), lambda i, c: (i, c, 0))
    shape = jax.ShapeDtypeStruct((b, s, lanes), F32)
    return pl.pallas_call(
        _table_transpose_kernel,
        grid=(b, s // rows),
        in_specs=[src, src],
        out_specs=[dst, dst],
        out_shape=[shape, shape],
        compiler_params=_params("parallel", "parallel"),
        name="rope_tables",
    )(cos_t, sin_t)


def kernel(x, positions, w_in, b_in, conv_w, mlstm_norm_g, attn_sinks, sgu_w_s, sgu_b_s, sgu_norm_g, sgu_norm_b, w_out, ln1_g, ln1_b, ln2_g, ln2_b, ffn_w_gate, ffn_w_up, ffn_w_down, moe_w_router, moe_b_router, moe_w_gate, moe_w_up, moe_w_down):
    bsz, seq, _ = x.shape
    n = bsz * seq
    cos_t, sin_t = _rope_tables(positions)
    tril = jnp.tril(jnp.ones((CHUNK, CHUNK), bool))
    x2d = x.reshape(n, D_MODEL)
    w_p, b_p = _layout_in_proj(w_in, b_in)
    for layer in range(DEPTH):
        proj3 = _inproj(x2d, w_p[layer], b_p[layer]).reshape(bsz, seq, D_PROJ_PAD)
        h_a = _mlstm(proj3, conv_w[layer], mlstm_norm_g[layer][None, :])
        w_tril = jnp.where(tril, sgu_w_s[layer], 0.0).astype(BF16)
        bias_tok = jnp.repeat(sgu_b_s[layer].T, HEAD_DIM, axis=1)
        h_b, h_c = _swa_sgu(proj3, cos_t, sin_t, attn_sinks[layer], w_tril, bias_tok,
                            sgu_norm_g[layer][None, :], sgu_norm_b[layer][None, :])
        wo = w_out[layer]
        wb = wo[MLSTM_W:MLSTM_W + ATTN_W].reshape(ATTN_Q_HEADS, HEAD_DIM, D_MODEL)[np.array(ATTN_HEAD_ORDER)]
        wo = jnp.concatenate([wo[:MLSTM_W], wb.reshape(ATTN_W, D_MODEL), wo[MLSTM_W + ATTN_W:]], 0).astype(BF16)
        j = layer // 2
        dense = layer % 2 == 0
        mixed = _outproj_ln(h_a.reshape(n, MLSTM_W), h_b.reshape(n, ATTN_W), h_c.reshape(n, SGU_W), x2d,
                            wo, ln1_g[layer][None, :], ln1_b[layer][None, :],
                            router=None if dense else _router_operands(moe_w_router[j], moe_b_router[j]))
        g2, b2 = ln2_g[layer][None, :], ln2_b[layer][None, :]
        if dense:
            x2d = _ffn_ln(mixed, ffn_w_gate[j].astype(BF16), ffn_w_up[j].astype(BF16),
                          ffn_w_down[j].astype(BF16), g2, b2)
        else:
            x2d, route, route_t = mixed
            x2d = _moe_ln(x2d, route, route_t, moe_w_gate[j], moe_w_up[j], moe_w_down[j], g2, b2)
    return x2d.reshape(bsz, seq, D_MODEL)
```

```python
import jax
import jax.numpy as jnp
import numpy as np
from jax import lax
from jax.experimental import pallas as pl
from jax.experimental.pallas import tpu as pltpu

F32 = jnp.float32
BF16 = jnp.bfloat16

D_MODEL = 1024
HEAD_DIM = 64
LANES = 128
SUBLANES = 8
MLSTM_HEADS = 6
ATTN_Q_HEADS = 6
ATTN_KV_HEADS = 2
SGU_GROUPS = 4
MLSTM_W = MLSTM_HEADS * HEAD_DIM
ATTN_W = ATTN_Q_HEADS * HEAD_DIM
ATTN_KV_W = ATTN_KV_HEADS * HEAD_DIM
SGU_W = SGU_GROUPS * HEAD_DIM
CHUNK = 128
CONV_WIDTH = 4
ROPE_DIM = HEAD_DIM // 4
ROPE_THETA = 500000.0
N_EXPERTS = 8
TOP_K = 2
DEPTH = 2
DN_ALPHA = (2.0 * DEPTH) ** 0.25
LN_EPS = 1e-5

QK_OFF, V_OFF, O_OFF, AQ_OFF = 0, 768, 1152, 1536
GATE_OFF, AK_OFF, AV_OFF, SU_OFF, SV_OFF = 1920, 2048, 2176, 2304, 2560
D_PROJ_PAD = 2816
ATTN_HEAD_ORDER = (0, 3, 1, 4, 2, 5)

VMEM_LIMIT = 56 * 1024 * 1024

MLSTM_CHUNKS_PER_STEP = 8
SWA_BLOCKS_PER_STEP = 2
ROW_TILE = 512
FFN_ROWS = 2048
FFN_SUB = 1024
FF_TILE = 256
MOE_SUB = 512
MOE_VISIT_SUBS = 9
MOE_FF_TILE = 512


def _params(*sem):
    return pltpu.CompilerParams(dimension_semantics=sem, vmem_limit_bytes=VMEM_LIMIT)


def _lane_lo(shape):
    return lax.broadcasted_iota(jnp.int32, shape, len(shape) - 1) < HEAD_DIM


def _layer_norm_rows(z, g, b):
    mu = jnp.mean(z, axis=-1, keepdims=True)
    zc = z - mu
    var = jnp.mean(zc * zc, axis=-1, keepdims=True)
    return zc * lax.rsqrt(var + LN_EPS) * g + b


def _half_layer_norm(x, lo):
    inv = 1.0 / HEAD_DIM
    s_lo = jnp.sum(jnp.where(lo, x, 0.0), axis=-1, keepdims=True)
    s_all = jnp.sum(x, axis=-1, keepdims=True)
    mu = jnp.where(lo, s_lo, s_all - s_lo) * inv
    xc = x - mu
    sq = xc * xc
    q_lo = jnp.sum(jnp.where(lo, sq, 0.0), axis=-1, keepdims=True)
    q_all = jnp.sum(sq, axis=-1, keepdims=True)
    var = jnp.where(lo, q_lo, q_all - q_lo) * inv
    return xc * lax.rsqrt(var + LN_EPS)


def _split3(a):
    h1 = a.astype(BF16)
    r1 = a - h1.astype(F32)
    h2 = r1.astype(BF16)
    r2 = r1 - h2.astype(F32)
    return h1, h2, r2.astype(BF16)


def _split3_masked(a):
    def top(x):
        bits = lax.bitcast_convert_type(x, jnp.int32) & jnp.int32(-65536)
        return lax.bitcast_convert_type(bits, F32)

    h1 = top(a)
    r1 = a - h1
    h2 = top(r1)
    return h1.astype(BF16), h2.astype(BF16), (r1 - h2).astype(BF16)


def _dot(a, b):
    return jnp.dot(a, b, preferred_element_type=F32)


def _inproj_kernel(x_ref, w_ref, b_ref, o_ref):
    o_ref[...] = _dot(x_ref[...].astype(BF16), w_ref[...]) + b_ref[...]


def _inproj(x2d, w, b):
    n = x2d.shape[0]
    return pl.pallas_call(
        _inproj_kernel,
        grid=(n // ROW_TILE,),
        in_specs=[pl.BlockSpec((ROW_TILE, D_MODEL), lambda i: (i, 0)),
                  pl.BlockSpec((D_MODEL, D_PROJ_PAD), lambda i: (0, 0)),
                  pl.BlockSpec((1, D_PROJ_PAD), lambda i: (0, 0))],
        out_specs=pl.BlockSpec((ROW_TILE, D_PROJ_PAD), lambda i: (i, 0)),
        out_shape=jax.ShapeDtypeStruct((n, D_PROJ_PAD), F32),
        compiler_params=_params("parallel"),
        name="inproj",
    )(x2d, w, b)


def _mlstm_kernel(qk_ref, v_ref, o_ref, gate_ref, cw_ref, g_ref, out_ref, buf_ref, ct_ref, m_ref):
    rows = MLSTM_CHUNKS_PER_STEP * CHUNK

    @pl.when(pl.program_id(1) == 0)
    def _():
        buf_ref[0:8, :] = jnp.zeros((8, 2 * MLSTM_W), F32)
        ct_ref[...] = jnp.zeros_like(ct_ref)
        m_ref[...] = jnp.zeros_like(m_ref)

    buf_ref[8:8 + rows, :] = qk_ref[...]
    ct = [ct_ref[p] for p in range(MLSTM_HEADS // 2)]
    m = [m_ref[h:h + 1, 0:1] for h in range(MLSTM_HEADS)]
    for i in range(MLSTM_CHUNKS_PER_STEP):
        blk = pl.ds(i * CHUNK, CHUNK)
        ct, m = _mlstm_chunk(i * CHUNK, buf_ref, v_ref.at[blk], o_ref.at[blk], gate_ref.at[blk], cw_ref, g_ref,
                             out_ref.at[blk], ct, m)
    buf_ref[0:8, :] = qk_ref[rows - 8:rows, :]
    for p in range(MLSTM_HEADS // 2):
        ct_ref[p] = ct[p]
    for h in range(MLSTM_HEADS):
        m_ref[h:h + 1, :] = jnp.broadcast_to(m[h], (1, LANES))


def _mlstm_chunk(row0, buf_ref, v_ref, o_ref, gate_ref, cw_ref, g_ref, out_ref, ct_in, m_in):
    cw = cw_ref[...]
    base = row0 + 8 - (CONV_WIDTH - 1)
    acc = cw[0:1, :] * buf_ref[base:base + CHUNK, :]
    for j in range(1, CONV_WIDTH):
        acc = acc + cw[j:j + 1, :] * buf_ref[base + j:base + j + CHUNK, :]
    qk = jax.nn.silu(acc)

    gates = gate_ref[...]
    lf = jax.nn.log_sigmoid(gates)
    row = lax.broadcasted_iota(jnp.int32, (CHUNK, CHUNK), 0)
    col = lax.broadcasted_iota(jnp.int32, (CHUNK, CHUNK), 1)
    causal = col <= row
    tri = causal.astype(BF16)
    l1, l2, l3 = _split3(lf)
    cum = _dot(tri, l1) + _dot(tri, l2) + _dot(tri, l3)
    gates_t = gates.T
    cum_t = cum.T
    cum_parts = _split3_masked(cum)

    lo = _lane_lo((CHUNK, LANES))
    ct_out = []
    m_out = []
    for p in range(MLSTM_HEADS // 2):
        sl = slice(p * LANES, (p + 1) * LANES)
        q_slab = qk[:, sl] * (HEAD_DIM ** -0.5)
        k_slab = qk[:, MLSTM_W + p * LANES:MLSTM_W + (p + 1) * LANES]
        kt_slab = k_slab.T
        v_slab = v_ref[:, sl]
        ct_pair = ct_in[p]
        ct_pair_b = ct_pair.astype(BF16)
        halves = []
        new_ct = []
        for half in range(2):
            h = 2 * p + half
            sel = lo if half == 0 else jnp.logical_not(lo)
            li_row = gates_t[h:h + 1, :]
            bc_row = cum_t[MLSTM_HEADS + h:MLSTM_HEADS + h + 1, :]
            pick = (row == MLSTM_HEADS + h).astype(BF16)
            bc_col = _dot(cum_parts[0], pick) + _dot(cum_parts[1], pick) + _dot(cum_parts[2], pick)
            b_tot = bc_row[:, CHUNK - 1:CHUNK]
            m_prev = m_in[h]

            dmat = jnp.where(causal, bc_col - bc_row + li_row, -jnp.inf)
            m_inter = bc_col + m_prev
            m_row = jnp.maximum(m_inter, jnp.max(dmat, axis=-1, keepdims=True))
            q_m = jnp.where(sel, q_slab, 0.0).astype(BF16)
            s = _dot(q_m, kt_slab.astype(BF16))
            pmat = (s * jnp.exp(dmat - m_row)).astype(BF16)
            w_inter = jnp.exp(m_inter - m_row)
            v_aug = jnp.where(sel, v_slab, 1.0).astype(BF16)
            halves.append((_dot(pmat, v_aug) + w_inter * _dot(q_m, ct_pair_b), jnp.exp(-m_row)))

            a_row = b_tot - bc_row + li_row
            m_loc = jnp.max(a_row, axis=-1, keepdims=True)
            w_row = jnp.exp(a_row - m_loc)
            kt_h = kt_slab[half * HEAD_DIM:(half + 1) * HEAD_DIM, :]
            ct_loc = _dot((kt_h * w_row).astype(BF16), v_aug)
            m_new = jnp.maximum(b_tot + m_prev, m_loc)
            s_old = jnp.exp(b_tot + m_prev - m_new)
            s_loc = jnp.exp(m_loc - m_new)
            new_ct.append(s_old * ct_pair[half * HEAD_DIM:(half + 1) * HEAD_DIM, :] + s_loc * ct_loc)
            m_out.append(m_new)

        ct_out.append(jnp.concatenate(new_ct, axis=0))
        (r_even, stab_even), (r_odd, stab_odd) = halves
        num = jnp.where(lo, r_even, r_odd)
        den = pltpu.roll(jnp.where(lo, r_odd, r_even), HEAD_DIM, 1)
        hh = num / jnp.maximum(jnp.abs(den), jnp.where(lo, stab_even, stab_odd))
        hn = _half_layer_norm(hh, lo) * g_ref[:, sl]
        out_ref[:, sl] = (hn * jax.nn.sigmoid(o_ref[:, sl])).astype(out_ref.dtype)
    return ct_out, m_out


def _mlstm(proj3, conv_w, norm_g):
    b, s, _ = proj3.shape
    rows = MLSTM_CHUNKS_PER_STEP * CHUNK
    assert s % rows == 0
    blk = lambda w, off: pl.BlockSpec((None, rows, w), lambda i, c: (i, c, off // w))
    return pl.pallas_call(
        _mlstm_kernel,
        grid=(b, s // rows),
        in_specs=[blk(2 * MLSTM_W, QK_OFF), blk(MLSTM_W, V_OFF), blk(MLSTM_W, O_OFF), blk(LANES, GATE_OFF),
                  pl.BlockSpec((CONV_WIDTH, 2 * MLSTM_W), lambda i, c: (0, 0)),
                  pl.BlockSpec((1, MLSTM_W), lambda i, c: (0, 0))],
        out_specs=pl.BlockSpec((None, rows, MLSTM_W), lambda i, c: (i, c, 0)),
        out_shape=jax.ShapeDtypeStruct((b, s, MLSTM_W), BF16),
        scratch_shapes=[pltpu.VMEM((8 + rows, 2 * MLSTM_W), F32),
                        pltpu.VMEM((MLSTM_HEADS // 2, LANES, LANES), F32),
                        pltpu.VMEM((8, LANES), F32)],
        compiler_params=_params("parallel", "arbitrary"),
        name="mlstm",
    )(proj3, proj3, proj3, proj3, conv_w, norm_g)


def _rope(x, cos_t, sin_t, first):
    return x * cos_t + jnp.where(first, pltpu.roll(x, LANES - ROPE_DIM // 2, 1),
                                 pltpu.roll(x, ROPE_DIM // 2, 1)) * sin_t


def _swa_kernel(sink_ref, q_ref, k_ref, v_ref, cos_ref, sin_ref, su_ref, sv_ref, ws_ref, bias_ref, sg_ref, sb_ref,
                out_ref, gated_ref, kt_ref, vv_ref):
    first_step = pl.program_id(1) == 0

    @pl.when(first_step)
    def _():
        kt_ref[...] = jnp.zeros_like(kt_ref)
        vv_ref[...] = jnp.zeros_like(vv_ref)

    kt_prev = kt_ref[...]
    v_prev = vv_ref[...]
    for i in range(SWA_BLOCKS_PER_STEP):
        rows = pl.ds(i * CHUNK, CHUNK)
        hide_previous = jnp.where(first_step, 2 * CHUNK, 0) if i == 0 else 0
        kt_prev, v_prev = _swa_block(hide_previous, sink_ref, q_ref.at[rows], k_ref.at[rows], v_ref.at[rows],
                                     cos_ref.at[rows], sin_ref.at[rows], out_ref.at[rows], kt_prev, v_prev)
        _sgu_chunk(slice(i * CHUNK, (i + 1) * CHUNK), su_ref, sv_ref, ws_ref, bias_ref, sg_ref, sb_ref, gated_ref)
    kt_ref[...] = kt_prev
    vv_ref[...] = v_prev


def _swa_block(first_block_shift, sink_ref, q_ref, k_ref, v_ref, cos_ref, sin_ref, out_ref, kt_prev, v_prev):
    cos_t = cos_ref[...]
    sin_t = sin_ref[...]
    lane = lax.broadcasted_iota(jnp.int32, (CHUNK, LANES), 1)
    lo = lane < HEAD_DIM
    first = (lane % HEAD_DIM) < ROPE_DIM // 2

    kt_cur = _rope(k_ref[...], cos_t, sin_t, first).T.astype(BF16)
    v_cur = v_ref[...].astype(BF16)
    kt = jnp.concatenate([kt_prev, kt_cur], axis=1)
    vv = jnp.concatenate([v_prev, v_cur], axis=0)

    row = lax.broadcasted_iota(jnp.int32, (CHUNK, 2 * CHUNK), 0)
    col = lax.broadcasted_iota(jnp.int32, (CHUNK, 2 * CHUNK), 1)
    visible = jnp.logical_or(jnp.logical_and(col < CHUNK, col > row + first_block_shift),
                             jnp.logical_and(col >= CHUNK, col - CHUNK <= row))

    for j in range(ATTN_Q_HEADS // 2):
        sl = slice(j * LANES, (j + 1) * LANES)
        q_slab = _rope(q_ref[:, sl], cos_t, sin_t, first) * (HEAD_DIM ** -0.5)
        outs = []
        for half in range(2):
            sel = lo if half == 0 else jnp.logical_not(lo)
            sink = sink_ref[ATTN_HEAD_ORDER[2 * j + half]]
            q_m = jnp.where(sel, q_slab, 0.0).astype(BF16)
            s = jnp.where(visible, _dot(q_m, kt), -jnp.inf)
            mx = jnp.maximum(jnp.max(s, axis=-1, keepdims=True), sink)
            pexp = jnp.exp(s - mx)
            denom = jnp.sum(pexp, axis=-1, keepdims=True) + jnp.exp(sink - mx)
            outs.append(_dot(pexp.astype(BF16), vv) / denom)
        out_ref[:, sl] = jnp.where(lo, outs[0], outs[1]).astype(out_ref.dtype)

    return kt_cur, v_cur


def _swa_sgu(proj3, cos_t, sin_t, sinks, w_tril, bias_tok, norm_g, norm_b):
    b, s, _ = proj3.shape
    rows = SWA_BLOCKS_PER_STEP * CHUNK
    assert s % rows == 0
    blk = lambda w, off: pl.BlockSpec((None, rows, w), lambda i, c: (i, c, off // w))
    tab = pl.BlockSpec((None, rows, LANES), lambda i, c: (i, c, 0))
    const2 = lambda shape: pl.BlockSpec(shape, lambda i, c: (0,) * len(shape))
    out = lambda w: pl.BlockSpec((None, rows, w), lambda i, c: (i, c, 0))
    return pl.pallas_call(
        _swa_kernel,
        grid=(b, s // rows),
        in_specs=[pl.BlockSpec(memory_space=pltpu.SMEM),
                  blk(ATTN_W, AQ_OFF), blk(ATTN_KV_W, AK_OFF), blk(ATTN_KV_W, AV_OFF), tab, tab,
                  blk(SGU_W, SU_OFF), blk(SGU_W, SV_OFF), const2((SGU_GROUPS, CHUNK, CHUNK)),
                  const2((CHUNK, SGU_W)), const2((1, SGU_W)), const2((1, SGU_W))],
        out_specs=[out(ATTN_W), out(SGU_W)],
        out_shape=[jax.ShapeDtypeStruct((b, s, ATTN_W), BF16), jax.ShapeDtypeStruct((b, s, SGU_W), BF16)],
        scratch_shapes=[pltpu.VMEM((LANES, CHUNK), BF16), pltpu.VMEM((CHUNK, LANES), BF16)],
        compiler_params=_params("parallel", "arbitrary"),
        name="swa_sgu",
    )(sinks, proj3, proj3, proj3, cos_t, sin_t, proj3, proj3, w_tril, bias_tok, norm_g, norm_b)


def _sgu_chunk(rows, u_ref, v_ref, w_ref, bias_ref, g_ref, b_ref, out_ref):
    lo = _lane_lo((CHUNK, LANES))
    for j in range(SGU_GROUPS // 2):
        sl = slice(j * LANES, (j + 1) * LANES)
        u = jax.nn.gelu(u_ref[rows, sl])
        v = jax.nn.gelu(v_ref[rows, sl])
        vn = (_half_layer_norm(v, lo) * g_ref[:, sl] + b_ref[:, sl]).astype(BF16)
        mixed = jnp.where(lo, _dot(w_ref[2 * j], vn), _dot(w_ref[2 * j + 1], vn)) + bias_ref[:, sl]
        out_ref[rows, sl] = (u * mixed).astype(out_ref.dtype)


def _outproj_kernel(ha_ref, hb_ref, hc_ref, x_ref, w_ref, g_ref, b_ref, *rest):
    cat_ref = rest[-1]
    cat_ref[:, 0:MLSTM_W] = ha_ref[...]
    cat_ref[:, MLSTM_W:MLSTM_W + ATTN_W] = hb_ref[...]
    cat_ref[:, MLSTM_W + ATTN_W:] = hc_ref[...]
    mix = _dot(cat_ref[...], w_ref[...])
    y = _layer_norm_rows(DN_ALPHA * x_ref[...] + mix, g_ref[...], b_ref[...])
    rest = rest[:-1]
    if len(rest) == 1:
        rest[0][...] = y
        return
    w_split_ref, rb_ref, o_ref, route_ref, route_t_ref = rest
    o_ref[...] = y
    route_ref[...], route_t_ref[...] = _route_rows(y, w_split_ref[...], rb_ref[...])


def _outproj_ln(ha, hb, hc, x2d, w, g, b, router=None):
    n = x2d.shape[0]
    rows = lambda w: pl.BlockSpec((ROW_TILE, w), lambda i: (i, 0))
    const = lambda shape: pl.BlockSpec(shape, lambda i: (0, 0))
    in_specs = [rows(MLSTM_W), rows(ATTN_W), rows(SGU_W), rows(D_MODEL),
                const((D_MODEL, D_MODEL)), const((1, D_MODEL)), const((1, D_MODEL))]
    out_specs = [rows(D_MODEL)]
    out_shape = [jax.ShapeDtypeStruct((n, D_MODEL), F32)]
    operands = (ha, hb, hc, x2d, w, g, b)
    if router is not None:
        in_specs += [const((D_MODEL, 2 * LANES)), const((1, LANES))]
        out_specs += [rows(LANES), pl.BlockSpec((N_EXPERTS, ROW_TILE), lambda i: (0, i))]
        out_shape += [jax.ShapeDtypeStruct((n, LANES), F32), jax.ShapeDtypeStruct((N_EXPERTS, n), F32)]
        operands += tuple(router)
    out = pl.pallas_call(
        _outproj_kernel,
        grid=(n // ROW_TILE,),
        in_specs=in_specs,
        out_specs=out_specs,
        out_shape=out_shape,
        scratch_shapes=[pltpu.VMEM((ROW_TILE, D_MODEL), BF16)],
        compiler_params=_params("parallel"),
        name="outproj_ln" if router is None else "outproj_ln_route",
    )(*operands)
    return out[0] if router is None else tuple(out)


def _ffn_kernel(x_ref, wg_ref, wu_ref, wd_ref, g_ref, b_ref, o_ref, xb_ref, wgu_ref):
    j = pl.program_id(1)
    last = pl.num_programs(1) - 1
    n_sub = FFN_ROWS // FFN_SUB
    wgu_ref[:, :FF_TILE] = wg_ref[...]
    wgu_ref[:, FF_TILE:] = wu_ref[...]

    def sub_rows(m):
        return pl.ds(pl.multiple_of(m * FFN_SUB, FFN_SUB), FFN_SUB)

    def ffn_tile(m):
        gu = _dot(xb_ref[sub_rows(m), :], wgu_ref[...])
        hidden = (jax.nn.silu(gu[:, :FF_TILE]) * gu[:, FF_TILE:]).astype(BF16)
        return _dot(hidden, wd_ref[...])

    @pl.when(j == 0)
    def _():
        def body(m, carry):
            xb_ref[sub_rows(m), :] = x_ref[sub_rows(m), :].astype(BF16)
            o_ref[sub_rows(m), :] = ffn_tile(m)
            return carry

        lax.fori_loop(0, n_sub, body, 0)

    @pl.when(jnp.logical_and(j > 0, j < last))
    def _():
        def body(m, carry):
            o_ref[sub_rows(m), :] += ffn_tile(m)
            return carry

        lax.fori_loop(0, n_sub, body, 0)

    @pl.when(j == last)
    def _():
        def body(m, carry):
            rows = sub_rows(m)
            ff = o_ref[rows, :] + ffn_tile(m)
            o_ref[rows, :] = _layer_norm_rows(DN_ALPHA * x_ref[rows, :] + ff, g_ref[...], b_ref[...])
            return carry

        lax.fori_loop(0, n_sub, body, 0)


def _ffn_ln(x2d, wg, wu, wd, g, b):
    n = x2d.shape[0]
    d_ff = wd.shape[0]
    assert d_ff // FF_TILE >= 2
    tm = FFN_ROWS
    return pl.pallas_call(
        _ffn_kernel,
        grid=(n // tm, d_ff // FF_TILE),
        in_specs=[pl.BlockSpec((tm, D_MODEL), lambda i, j: (i, 0)),
                  pl.BlockSpec((D_MODEL, FF_TILE), lambda i, j: (0, j)),
                  pl.BlockSpec((D_MODEL, FF_TILE), lambda i, j: (0, j)),
                  pl.BlockSpec((FF_TILE, D_MODEL), lambda i, j: (j, 0)),
                  pl.BlockSpec((1, D_MODEL), lambda i, j: (0, 0)),
                  pl.BlockSpec((1, D_MODEL), lambda i, j: (0, 0))],
        out_specs=pl.BlockSpec((tm, D_MODEL), lambda i, j: (i, 0)),
        out_shape=jax.ShapeDtypeStruct((n, D_MODEL), F32),
        scratch_shapes=[pltpu.VMEM((tm, D_MODEL), BF16), pltpu.VMEM((D_MODEL, 2 * FF_TILE), BF16)],
        compiler_params=_params("parallel", "arbitrary"),
        name="ffn_ln",
    )(x2d, wg, wu, wd, g, b)


def _route_rows(x, w_split, bias):
    n = x.shape[0]
    xh = x.astype(BF16)
    xl = (x - xh.astype(F32)).astype(BF16)
    prod = _dot(jnp.concatenate([xh, xl], axis=0), w_split)
    logits = prod[:n, :LANES] + prod[:n, LANES:] + prod[n:, :LANES] + prod[n:, LANES:] + bias
    lt = logits.T[0:N_EXPERTS, :]
    expert = lax.broadcasted_iota(jnp.int32, lt.shape, 0)
    m1 = jnp.max(lt, axis=0, keepdims=True)
    i1 = jnp.min(jnp.where(lt == m1, expert, N_EXPERTS), axis=0, keepdims=True)
    rest = jnp.where(expert == i1, -jnp.inf, lt)
    m2 = jnp.max(rest, axis=0, keepdims=True)
    i2 = jnp.min(jnp.where(rest == m2, expert, N_EXPERTS), axis=0, keepdims=True)
    e2 = jnp.exp(m2 - m1)
    g1 = 1.0 / (1.0 + e2)
    g2 = e2 / (1.0 + e2)
    route_t = jnp.where(expert == 0, g1, jnp.where(expert == 1, g2, jnp.where(
        expert == 2, i1.astype(F32), jnp.where(expert == 3, i2.astype(F32), 0.0))))
    padded = jnp.concatenate([route_t, jnp.zeros((LANES - N_EXPERTS, n), F32)], axis=0)
    return padded.T, route_t


TOKEN_TILE = (SUBLANES, D_MODEL // SUBLANES)


def _dispatch_kernel(slot0_ref, slot1_ref, ends_ref, x_ref, o_hbm, stage_ref, zero_ref, sem, zero_sem):
    t = pl.program_id(0)
    last = pl.num_programs(0) - 1
    s = t % 2

    def wait_tile(ss):
        for _ in range(TOP_K):
            pltpu.make_async_copy(zero_ref, o_hbm.at[pl.ds(0, MOE_SUB)], sem.at[ss]).wait()

    @pl.when(t == 0)
    def _():
        zero_ref[...] = jnp.zeros_like(zero_ref)

        def zero_sub_tile(m, carry):
            dst = o_hbm.at[pl.ds(pl.multiple_of(m * MOE_SUB, MOE_SUB), MOE_SUB)]
            cp = pltpu.make_async_copy(zero_ref, dst, zero_sem)
            cp.start()
            cp.wait()
            return carry

        for e in range(N_EXPERTS):
            @pl.when(ends_ref[e] >= MOE_SUB)
            def _():
                zero_sub_tile(ends_ref[e] // MOE_SUB - 1, 0)

        lax.fori_loop(ends_ref[N_EXPERTS - 1] // MOE_SUB, o_hbm.shape[0] // MOE_SUB, zero_sub_tile, 0)

    @pl.when(t >= 2)
    def _():
        wait_tile(s)

    stage_ref[s] = x_ref[...].reshape((MOE_SUB,) + TOKEN_TILE)

    def issue(i, carry):
        for u in range(SUBLANES):
            r = i * SUBLANES + u
            for k, slot_ref in enumerate((slot0_ref, slot1_ref)):
                pltpu.make_async_copy(stage_ref.at[s, r], o_hbm.at[slot_ref[t * MOE_SUB + r]],
                                      sem.at[s]).start(priority=k)
        return carry

    lax.fori_loop(0, MOE_SUB // SUBLANES, issue, 0)

    @pl.when(t == last)
    def _():
        wait_tile(s)
        wait_tile(1 - s)


def _dispatch_rows(x2d, slots, ends, n_rows):
    n = x2d.shape[0]
    assert n // MOE_SUB >= 2
    return pl.pallas_call(
        _dispatch_kernel,
        grid_spec=pltpu.PrefetchScalarGridSpec(
            num_scalar_prefetch=3,
            grid=(n // MOE_SUB,),
            in_specs=[pl.BlockSpec((MOE_SUB, D_MODEL), lambda t, s0, s1, en: (t, 0))],
            out_specs=pl.BlockSpec(memory_space=pl.ANY),
            scratch_shapes=[pltpu.VMEM((2, MOE_SUB) + TOKEN_TILE, F32), pltpu.VMEM((MOE_SUB,) + TOKEN_TILE, F32),
                            pltpu.SemaphoreType.DMA((2,)), pltpu.SemaphoreType.DMA(())]),
        out_shape=jax.ShapeDtypeStruct((n_rows,) + TOKEN_TILE, F32),
        compiler_params=_params("arbitrary"),
        name="moe_dispatch",
    )(slots[0], slots[1], ends, x2d)


def _moe_kernel(exp_ref, row0_ref, nsub_ref, tail_ref, x_hbm, wg_ref, wu_ref, wd_ref, y_hbm,
                xb_ref, acc_ref, wgu_ref, wdb_ref, stage_ref, in_sem, out_sem):
    v = pl.program_id(0)
    j = pl.program_id(1)
    last = pl.num_programs(1) - 1
    n_sub = nsub_ref[v]
    row0 = row0_ref[v]

    def sub_rows(m):
        return pl.ds(pl.multiple_of(m * MOE_SUB, MOE_SUB), MOE_SUB)

    def hbm_rows(m):
        return pl.ds(pl.multiple_of(row0 + m * MOE_SUB, MOE_SUB), MOE_SUB)

    @pl.when(jnp.logical_and(v == 0, j == 0))
    def _():
        stage_ref[0] = jnp.zeros((MOE_SUB,) + TOKEN_TILE, F32)

        def zero_sub_tile(m, carry):
            cp = pltpu.make_async_copy(stage_ref.at[0], y_hbm.at[sub_rows(m)], out_sem.at[0])
            cp.start()
            cp.wait()
            return carry

        lax.fori_loop(tail_ref[0], y_hbm.shape[0] // MOE_SUB, zero_sub_tile, 0)

    @pl.when(n_sub > 0)
    def _():
        wgu_ref[:, :MOE_FF_TILE] = wg_ref[...].astype(BF16)
        wgu_ref[:, MOE_FF_TILE:] = wu_ref[...].astype(BF16)
        wdb_ref[...] = wd_ref[...].astype(BF16)

        def ffn_rows(rows):
            gu = _dot(xb_ref[rows, :], wgu_ref[...])
            hidden = (jax.nn.silu(gu[:, :MOE_FF_TILE]) * gu[:, MOE_FF_TILE:]).astype(BF16)
            return _dot(hidden, wdb_ref[...])

        def ffn_tile(m):
            return ffn_rows(sub_rows(m))

        def accumulate(after_sub_tile):
            def pair(k, carry):
                rows = pl.ds(pl.multiple_of(2 * k * MOE_SUB, 2 * MOE_SUB), 2 * MOE_SUB)
                acc_ref[rows, :] += ffn_rows(rows)
                after_sub_tile(2 * k)
                after_sub_tile(2 * k + 1)
                return carry

            lax.fori_loop(0, n_sub // 2, pair, 0)

            @pl.when(n_sub % 2 == 1)
            def _():
                acc_ref[sub_rows(n_sub - 1), :] += ffn_tile(n_sub - 1)
                after_sub_tile(n_sub - 1)

        def y_copy(m):
            return pltpu.make_async_copy(stage_ref.at[m % 2], y_hbm.at[hbm_rows(m)], out_sem.at[m % 2])

        def send_sub_tile(m):
            @pl.when(m >= 2)
            def _():
                y_copy(m - 2).wait()

            stage_ref[m % 2] = acc_ref[sub_rows(m), :].reshape((MOE_SUB,) + TOKEN_TILE)
            y_copy(m).start()

        @pl.when(j == 0)
        def _():
            def x_copy(m):
                return pltpu.make_async_copy(x_hbm.at[hbm_rows(m)], stage_ref.at[m % 2], in_sem.at[m % 2])

            def convert(m):
                xb_ref[sub_rows(m), :] = stage_ref[m % 2].reshape(MOE_SUB, D_MODEL).astype(BF16)

            x_copy(0).start()

            @pl.when(n_sub > 1)
            def _():
                x_copy(1).start()

            x_copy(0).wait()
            convert(0)

            def body(m, carry):
                @pl.when(m + 2 < n_sub)
                def _():
                    x_copy(m + 2).start()

                x_copy(m + 1).wait()
                acc_ref[sub_rows(m), :] = ffn_tile(m)
                convert(m + 1)
                return carry

            lax.fori_loop(0, n_sub - 1, body, 0)
            acc_ref[sub_rows(n_sub - 1), :] = ffn_tile(n_sub - 1)

        @pl.when(jnp.logical_and(j > 0, j < last))
        def _():
            accumulate(lambda m: None)

        @pl.when(j == last)
        def _():
            accumulate(send_sub_tile)

            @pl.when(n_sub >= 2)
            def _():
                y_copy(n_sub - 2).wait()

            y_copy(n_sub - 1).wait()


def _moe_grouped(xs, wg, wu, wd, visit_exp, visit_row0, visit_nsub, tail_sub):
    n_rows = xs.shape[0]
    n_visits = visit_exp.shape[0]
    d_ff = wg.shape[2]
    assert d_ff // MOE_FF_TILE >= 2
    rows = MOE_VISIT_SUBS * MOE_SUB
    w_in = lambda v, j, e, r, ns, tl: (e[v], 0, j)
    w_out = lambda v, j, e, r, ns, tl: (e[v], j, 0)
    return pl.pallas_call(
        _moe_kernel,
        grid_spec=pltpu.PrefetchScalarGridSpec(
            num_scalar_prefetch=4,
            grid=(n_visits, d_ff // MOE_FF_TILE),
            in_specs=[pl.BlockSpec(memory_space=pl.ANY),
                      pl.BlockSpec((None, D_MODEL, MOE_FF_TILE), w_in),
                      pl.BlockSpec((None, D_MODEL, MOE_FF_TILE), w_in),
                      pl.BlockSpec((None, MOE_FF_TILE, D_MODEL), w_out)],
            out_specs=pl.BlockSpec(memory_space=pl.ANY),
            scratch_shapes=[pltpu.VMEM((rows, D_MODEL), BF16), pltpu.VMEM((rows, D_MODEL), F32),
                            pltpu.VMEM((D_MODEL, 2 * MOE_FF_TILE), BF16),
                            pltpu.VMEM((MOE_FF_TILE, D_MODEL), BF16),
                            pltpu.VMEM((2, MOE_SUB) + TOKEN_TILE, F32),
                            pltpu.SemaphoreType.DMA((2,)), pltpu.SemaphoreType.DMA((2,))]),
        out_shape=jax.ShapeDtypeStruct((n_rows,) + TOKEN_TILE, F32),
        compiler_params=_params("arbitrary", "arbitrary"),
        name="moe_grouped",
    )(visit_exp, visit_row0, visit_nsub, tail_sub, xs, wg, wu, wd)


def _combine_kernel(slot0_ref, slot1_ref, x_ref, gate_ref, y_hbm, g_ref, b_ref, o_ref, buf_ref, sem):
    t = pl.program_id(0)
    s = t % 2

    def issue_tile(tt, ss):
        def issue(i, carry):
            for u in range(SUBLANES):
                r = i * SUBLANES + u
                for k, slot_ref in enumerate((slot0_ref, slot1_ref)):
                    pltpu.make_async_copy(y_hbm.at[slot_ref[tt * MOE_SUB + r]], buf_ref.at[ss, k, r],
                                          sem.at[ss]).start(priority=k)
            return carry

        lax.fori_loop(0, MOE_SUB // SUBLANES, issue, 0)

    @pl.when(t == 0)
    def _():
        issue_tile(0, 0)

    @pl.when(t + 1 < pl.num_programs(0))
    def _():
        issue_tile(t + 1, 1 - s)

    for k in range(TOP_K):
        pltpu.make_async_copy(y_hbm.at[pl.ds(0, MOE_SUB)], buf_ref.at[s, k], sem.at[s]).wait()
    gate = gate_ref[...]
    y0 = buf_ref[s, 0].reshape(MOE_SUB, D_MODEL)
    y1 = buf_ref[s, 1].reshape(MOE_SUB, D_MODEL)
    ff = gate[:, 0:1] * y0 + gate[:, 1:2] * y1
    o_ref[...] = _layer_norm_rows(DN_ALPHA * x_ref[...] + ff, g_ref[...], b_ref[...])


def _combine_ln(x2d, route, slots, ys, g, b):
    n = x2d.shape[0]
    return pl.pallas_call(
        _combine_kernel,
        grid_spec=pltpu.PrefetchScalarGridSpec(
            num_scalar_prefetch=2,
            grid=(n // MOE_SUB,),
            in_specs=[pl.BlockSpec((MOE_SUB, D_MODEL), lambda t, s0, s1: (t, 0)),
                      pl.BlockSpec((MOE_SUB, LANES), lambda t, s0, s1: (t, 0)),
                      pl.BlockSpec(memory_space=pl.ANY),
                      pl.BlockSpec((1, D_MODEL), lambda t, s0, s1: (0, 0)),
                      pl.BlockSpec((1, D_MODEL), lambda t, s0, s1: (0, 0))],
            out_specs=pl.BlockSpec((MOE_SUB, D_MODEL), lambda t, s0, s1: (t, 0)),
            scratch_shapes=[pltpu.VMEM((2, TOP_K, MOE_SUB) + TOKEN_TILE, F32), pltpu.SemaphoreType.DMA((2,))]),
        out_shape=jax.ShapeDtypeStruct((n, D_MODEL), F32),
        compiler_params=_params("arbitrary"),
        name="moe_combine_ln",
    )(slots[0], slots[1], x2d, route, ys, g, b)


def _routing_tables(route_t, n):
    idx = route_t[2:4].astype(jnp.int32)
    expert = jnp.arange(N_EXPERTS, dtype=jnp.int32)[:, None]
    chosen = [idx[k][None, :] == expert for k in range(TOP_K)]
    onehot = jnp.logical_or(chosen[0], chosen[1]).astype(jnp.int32)
    rank = jnp.cumsum(onehot, axis=1) - onehot
    counts = jnp.sum(onehot, axis=1)
    padded = ((counts + MOE_SUB - 1) // MOE_SUB) * MOE_SUB
    ends = jnp.cumsum(padded)
    starts = ends - padded
    place = starts[:, None] + rank
    slot = [jnp.sum(jnp.where(chosen[k], place, 0), axis=0).astype(jnp.int32) for k in range(TOP_K)]

    n_rows = -(-(n * TOP_K + N_EXPERTS * (MOE_SUB - 1)) // MOE_SUB) * MOE_SUB

    visit_rows = MOE_VISIT_SUBS * MOE_SUB
    max_chunks = -(-n_rows // visit_rows)
    chunk = jnp.arange(max_chunks, dtype=jnp.int32)[None, :]
    left = padded[:, None] - chunk * visit_rows
    valid = (left > 0).reshape(-1)
    n_visits = n_rows // visit_rows + N_EXPERTS
    order = jnp.argsort(jnp.logical_not(valid), stable=True)[:n_visits]
    n_valid = jnp.sum(valid.astype(jnp.int32))
    live = jnp.arange(n_visits) < n_valid
    order = jnp.where(live, order, order[jnp.maximum(n_valid - 1, 0)])
    v_exp = (order // max_chunks).astype(jnp.int32)
    v_chunk = (order % max_chunks).astype(jnp.int32)
    v_row0 = jnp.where(live, starts[v_exp] + v_chunk * visit_rows, 0).astype(jnp.int32)
    v_nsub = jnp.where(live, jnp.minimum(left.reshape(-1)[order], visit_rows) // MOE_SUB, 0).astype(jnp.int32)
    tail_sub = (ends[N_EXPERTS - 1:] // MOE_SUB).astype(jnp.int32)
    return slot, ends.astype(jnp.int32), n_rows, (v_exp, v_row0, v_nsub, tail_sub)


def _router_operands(w_router, b_router):
    w_pad = jnp.zeros((D_MODEL, LANES), F32).at[:, :N_EXPERTS].set(w_router)
    w_hi = w_pad.astype(BF16)
    w_lo = (w_pad - w_hi.astype(F32)).astype(BF16)
    b_pad = jnp.zeros((1, LANES), F32).at[0, :N_EXPERTS].set(b_router)
    return jnp.concatenate([w_hi, w_lo], axis=1), b_pad


def _moe_ln(x2d, route, route_t, wg, wu, wd, g, b):
    n = x2d.shape[0]
    slots, ends, n_rows, visits = _routing_tables(route_t, n)
    xs = _dispatch_rows(x2d, slots, ends, n_rows)
    ys = _moe_grouped(xs, wg, wu, wd, *visits)
    return _combine_ln(x2d, route, slots, ys, g, b)


_SRC_GATES = 4 * MLSTM_W
_SRC_AQ = _SRC_GATES + 2 * MLSTM_HEADS
_SRC_AK = _SRC_AQ + ATTN_W
_IN_PROJ_MOVES = (
    ((0, 0, 4 * MLSTM_W),)
    + tuple((_SRC_AQ + h * HEAD_DIM, AQ_OFF + i * HEAD_DIM, HEAD_DIM) for i, h in enumerate(ATTN_HEAD_ORDER))
    + ((_SRC_AK, AK_OFF, ATTN_KV_W), (_SRC_AK + ATTN_KV_W, AV_OFF, ATTN_KV_W),
       (_SRC_AK + 2 * ATTN_KV_W, SU_OFF, SGU_W), (_SRC_AK + 2 * ATTN_KV_W + SGU_W, SV_OFF, SGU_W)))
D_PROJ = _SRC_AK + 2 * ATTN_KV_W + 2 * SGU_W


def _relayout_columns(src, dst_dtype):
    source = np.full((D_PROJ_PAD,), D_PROJ, np.int32)
    for s, d, w in _IN_PROJ_MOVES + ((_SRC_GATES, GATE_OFF, 2 * MLSTM_HEADS),):
        source[d:d + w] = np.arange(s, s + w)
    padded = jnp.concatenate([src, jnp.zeros(src.shape[:-1] + (1,), src.dtype)], axis=-1)
    return jnp.take(padded, source, axis=-1).astype(dst_dtype)


def _w_layout_kernel(w_ref, o_ref):
    for s, d, w in _IN_PROJ_MOVES:
        o_ref[:, d:d + w] = w_ref[:, s:s + w].astype(o_ref.dtype)
    pad = jnp.zeros((w_ref.shape[0], LANES - 2 * MLSTM_HEADS), F32)
    gates = jnp.concatenate([w_ref[:, _SRC_GATES:_SRC_GATES + 2 * MLSTM_HEADS], pad], axis=1)
    o_ref[:, GATE_OFF:GATE_OFF + LANES] = gates.astype(o_ref.dtype)


def _layout_in_proj(w_in, b_in):
    rows = D_MODEL // 4

    def one_layer(layer):
        return pl.pallas_call(
            _w_layout_kernel,
            grid=(D_MODEL // rows,),
            in_specs=[pl.BlockSpec((None, rows, D_PROJ), lambda i: (layer, i, 0))],
            out_specs=pl.BlockSpec((rows, D_PROJ_PAD), lambda i: (i, 0)),
            out_shape=jax.ShapeDtypeStruct((D_MODEL, D_PROJ_PAD), BF16),
            compiler_params=_params("parallel"),
            name="w_in_layout",
        )(w_in)

    return [one_layer(layer) for layer in range(w_in.shape[0])], _relayout_columns(b_in, F32)[:, None, :]


def _rope_tables(positions):
    inv_freq = ROPE_THETA ** (-jnp.arange(0, ROPE_DIM, 2, dtype=F32) / ROPE_DIM)
    ang = inv_freq[None, :, None] * positions.astype(F32)[:, None, :]
    cos, sin = jnp.cos(ang), jnp.sin(ang)
    ones = jnp.ones((ang.shape[0], HEAD_DIM - ROPE_DIM, ang.shape[2]), F32)
    cos_head = jnp.concatenate([cos, cos, ones], 1)
    sin_head = jnp.concatenate([-sin, sin, 0.0 * ones], 1)
    cos_t = jnp.concatenate([cos_head, cos_head], 1)
    sin_t = jnp.concatenate([sin_head, sin_head], 1)
    return _to_token_major(cos_t, sin_t)


def _table_transpose_kernel(cos_ref, sin_ref, cos_out, sin_out):
    cos_out[...] = cos_ref[...].T
    sin_out[...] = sin_ref[...].T


def _to_token_major(cos_t, sin_t):
    b, lanes, s = cos_t.shape
    rows = min(s, FFN_ROWS)
    src = pl.BlockSpec((None, lanes, rows), lambda i, c: (i, 0, c))
    dst = pl.BlockSpec((None, rows, lanes), lambda i, c: (i, c, 0))
    shape = jax.ShapeDtypeStruct((b, s, lanes), F32)
    return pl.pallas_call(
        _table_transpose_kernel,
        grid=(b, s // rows),
        in_specs=[src, src],
        out_specs=[dst, dst],
        out_shape=[shape, shape],
        compiler_params=_params("parallel", "parallel"),
        name="rope_tables",
    )(cos_t, sin_t)


def kernel(x, positions, w_in, b_in, conv_w, mlstm_norm_g, attn_sinks, sgu_w_s, sgu_b_s, sgu_norm_g, sgu_norm_b, w_out, ln1_g, ln1_b, ln2_g, ln2_b, ffn_w_gate, ffn_w_up, ffn_w_down, moe_w_router, moe_b_router, moe_w_gate, moe_w_up, moe_w_down):
    bsz, seq, _ = x.shape
    n = bsz * seq
    cos_t, sin_t = _rope_tables(positions)
    tril = jnp.tril(jnp.ones((CHUNK, CHUNK), bool))
    x2d = x.reshape(n, D_MODEL)
    w_p, b_p = _layout_in_proj(w_in, b_in)
    for layer in range(DEPTH):
        proj3 = _inproj(x2d, w_p[layer], b_p[layer]).reshape(bsz, seq, D_PROJ_PAD)
        h_a = _mlstm(proj3, conv_w[layer], mlstm_norm_g[layer][None, :])
        w_tril = jnp.where(tril, sgu_w_s[layer], 0.0).astype(BF16)
        bias_tok = jnp.repeat(sgu_b_s[layer].T, HEAD_DIM, axis=1)
        h_b, h_c = _swa_sgu(proj3, cos_t, sin_t, attn_sinks[layer], w_tril, bias_tok,
                            sgu_norm_g[layer][None, :], sgu_norm_b[layer][None, :])
        wo = w_out[layer]
        wb = wo[MLSTM_W:MLSTM_W + ATTN_W].reshape(ATTN_Q_HEADS, HEAD_DIM, D_MODEL)[np.array(ATTN_HEAD_ORDER)]
        wo = jnp.concatenate([wo[:MLSTM_W], wb.reshape(ATTN_W, D_MODEL), wo[MLSTM_W + ATTN_W:]], 0).astype(BF16)
        j = layer // 2
        dense = layer % 2 == 0
        mixed = _outproj_ln(h_a.reshape(n, MLSTM_W), h_b.reshape(n, ATTN_W), h_c.reshape(n, SGU_W), x2d,
                            wo, ln1_g[layer][None, :], ln1_b[layer][None, :],
                            router=None if dense else _router_operands(moe_w_router[j], moe_b_router[j]))
        g2, b2 = ln2_g[layer][None, :], ln2_b[layer][None, :]
        if dense:
            x2d = _ffn_ln(mixed, ffn_w_gate[j].astype(BF16), ffn_w_up[j].astype(BF16),
                          ffn_w_down[j].astype(BF16), g2, b2)
        else:
            x2d, route, route_t = mixed
            x2d = _moe_ln(x2d, route, route_t, moe_w_gate[j], moe_w_up[j], moe_w_down[j], g2, b2)
    return x2d.reshape(bsz, seq, D_MODEL)
```

```python
import jax
import jax.numpy as jnp
import numpy as np
from jax import lax
from jax.experimental import pallas as pl
from jax.experimental.pallas import tpu as pltpu

F32 = jnp.float32
BF16 = jnp.bfloat16

D_MODEL = 1024
HEAD_DIM = 64
LANES = 128
SUBLANES = 8
MLSTM_HEADS = 6
ATTN_Q_HEADS = 6
ATTN_KV_HEADS = 2
SGU_GROUPS = 4
MLSTM_W = MLSTM_HEADS * HEAD_DIM
ATTN_W = ATTN_Q_HEADS * HEAD_DIM
ATTN_KV_W = ATTN_KV_HEADS * HEAD_DIM
SGU_W = SGU_GROUPS * HEAD_DIM
CHUNK = 128
CONV_WIDTH = 4
ROPE_DIM = HEAD_DIM // 4
ROPE_THETA = 500000.0
N_EXPERTS = 8
TOP_K = 2
DEPTH = 2
DN_ALPHA = (2.0 * DEPTH) ** 0.25
LN_EPS = 1e-5

QK_OFF, V_OFF, O_OFF, AQ_OFF = 0, 768, 1152, 1536
GATE_OFF, AK_OFF, AV_OFF, SU_OFF, SV_OFF = 1920, 2048, 2176, 2304, 2560
D_PROJ_PAD = 2816
ATTN_HEAD_ORDER = (0, 3, 1, 4, 2, 5)

VMEM_LIMIT = 56 * 1024 * 1024

MLSTM_CHUNKS_PER_STEP = 8
SWA_BLOCKS_PER_STEP = 2
ROW_TILE = 512
OUTPROJ_ROWS = 1024
FFN_ROWS = 2048
FFN_SUB = 2048
FF_TILE = 256
MOE_SUB = 512
MOE_VISIT_SUBS = 9
MOE_FF_TILE = 512


def _params(*sem):
    return pltpu.CompilerParams(dimension_semantics=sem, vmem_limit_bytes=VMEM_LIMIT)


def _lane_lo(shape):
    return lax.broadcasted_iota(jnp.int32, shape, len(shape) - 1) < HEAD_DIM


def _layer_norm_rows(z, g, b):
    mu = jnp.mean(z, axis=-1, keepdims=True)
    zc = z - mu
    var = jnp.mean(zc * zc, axis=-1, keepdims=True)
    return zc * lax.rsqrt(var + LN_EPS) * g + b


def _half_layer_norm(x, lo):
    inv = 1.0 / HEAD_DIM
    s_lo = jnp.sum(jnp.where(lo, x, 0.0), axis=-1, keepdims=True)
    s_all = jnp.sum(x, axis=-1, keepdims=True)
    mu = jnp.where(lo, s_lo, s_all - s_lo) * inv
    xc = x - mu
    sq = xc * xc
    q_lo = jnp.sum(jnp.where(lo, sq, 0.0), axis=-1, keepdims=True)
    q_all = jnp.sum(sq, axis=-1, keepdims=True)
    var = jnp.where(lo, q_lo, q_all - q_lo) * inv
    return xc * lax.rsqrt(var + LN_EPS)


def _split3(a):
    h1 = a.astype(BF16)
    r1 = a - h1.astype(F32)
    h2 = r1.astype(BF16)
    r2 = r1 - h2.astype(F32)
    return h1, h2, r2.astype(BF16)


def _split3_masked(a):
    def top(x):
        bits = lax.bitcast_convert_type(x, jnp.int32) & jnp.int32(-65536)
        return lax.bitcast_convert_type(bits, F32)

    h1 = top(a)
    r1 = a - h1
    h2 = top(r1)
    return h1.astype(BF16), h2.astype(BF16), (r1 - h2).astype(BF16)


def _dot(a, b):
    return jnp.dot(a, b, preferred_element_type=F32)


def _inproj_kernel(x_ref, w_ref, b_ref, o_ref):
    o_ref[...] = _dot(x_ref[...].astype(BF16), w_ref[...]) + b_ref[...]


def _inproj(x2d, w, b):
    n = x2d.shape[0]
    return pl.pallas_call(
        _inproj_kernel,
        grid=(n // ROW_TILE,),
        in_specs=[pl.BlockSpec((ROW_TILE, D_MODEL), lambda i: (i, 0)),
                  pl.BlockSpec((D_MODEL, D_PROJ_PAD), lambda i: (0, 0)),
                  pl.BlockSpec((1, D_PROJ_PAD), lambda i: (0, 0))],
        out_specs=pl.BlockSpec((ROW_TILE, D_PROJ_PAD), lambda i: (i, 0)),
        out_shape=jax.ShapeDtypeStruct((n, D_PROJ_PAD), F32),
        compiler_params=_params("parallel"),
        name="inproj",
    )(x2d, w, b)


def _mlstm_kernel(qk_ref, v_ref, o_ref, gate_ref, cw_ref, g_ref, out_ref, buf_ref, ct_ref, m_ref):
    rows = MLSTM_CHUNKS_PER_STEP * CHUNK

    @pl.when(pl.program_id(1) == 0)
    def _():
        buf_ref[0:8, :] = jnp.zeros((8, 2 * MLSTM_W), F32)
        ct_ref[...] = jnp.zeros_like(ct_ref)
        m_ref[...] = jnp.zeros_like(m_ref)

    buf_ref[8:8 + rows, :] = qk_ref[...]
    ct = [ct_ref[p] for p in range(MLSTM_HEADS // 2)]
    m = [m_ref[h:h + 1, 0:1] for h in range(MLSTM_HEADS)]
    for i in range(MLSTM_CHUNKS_PER_STEP):
        blk = pl.ds(i * CHUNK, CHUNK)
        ct, m = _mlstm_chunk(i * CHUNK, buf_ref, v_ref.at[blk], o_ref.at[blk], gate_ref.at[blk], cw_ref, g_ref,
                             out_ref.at[blk], ct, m)
    buf_ref[0:8, :] = qk_ref[rows - 8:rows, :]
    for p in range(MLSTM_HEADS // 2):
        ct_ref[p] = ct[p]
    for h in range(MLSTM_HEADS):
        m_ref[h:h + 1, :] = jnp.broadcast_to(m[h], (1, LANES))


def _mlstm_chunk(row0, buf_ref, v_ref, o_ref, gate_ref, cw_ref, g_ref, out_ref, ct_in, m_in):
    cw = cw_ref[...]
    base = row0 + 8 - (CONV_WIDTH - 1)
    acc = cw[0:1, :] * buf_ref[base:base + CHUNK, :]
    for j in range(1, CONV_WIDTH):
        acc = acc + cw[j:j + 1, :] * buf_ref[base + j:base + j + CHUNK, :]
    qk = jax.nn.silu(acc)

    gates = gate_ref[...]
    lf = jax.nn.log_sigmoid(gates)
    row = lax.broadcasted_iota(jnp.int32, (CHUNK, CHUNK), 0)
    col = lax.broadcasted_iota(jnp.int32, (CHUNK, CHUNK), 1)
    causal = col <= row
    tri = causal.astype(BF16)
    l1, l2, l3 = _split3(lf)
    cum = _dot(tri, l1) + _dot(tri, l2) + _dot(tri, l3)
    gates_t = gates.T
    cum_t = cum.T
    cum_parts = _split3_masked(cum)

    lo = _lane_lo((CHUNK, LANES))
    ct_out = []
    m_out = []
    for p in range(MLSTM_HEADS // 2):
        sl = slice(p * LANES, (p + 1) * LANES)
        q_slab = qk[:, sl] * (HEAD_DIM ** -0.5)
        k_slab = qk[:, MLSTM_W + p * LANES:MLSTM_W + (p + 1) * LANES]
        kt_slab = k_slab.T
        v_slab = v_ref[:, sl]
        ct_pair = ct_in[p]
        ct_pair_b = ct_pair.astype(BF16)
        halves = []
        new_ct = []
        for half in range(2):
            h = 2 * p + half
            sel = lo if half == 0 else jnp.logical_not(lo)
            li_row = gates_t[h:h + 1, :]
            bc_row = cum_t[MLSTM_HEADS + h:MLSTM_HEADS + h + 1, :]
            pick = (row == MLSTM_HEADS + h).astype(BF16)
            bc_col = _dot(cum_parts[0], pick) + _dot(cum_parts[1], pick) + _dot(cum_parts[2], pick)
            b_tot = bc_row[:, CHUNK - 1:CHUNK]
            m_prev = m_in[h]

            dmat = jnp.where(causal, bc_col - bc_row + li_row, -jnp.inf)
            m_inter = bc_col + m_prev
            m_row = jnp.maximum(m_inter, jnp.max(dmat, axis=-1, keepdims=True))
            q_m = jnp.where(sel, q_slab, 0.0).astype(BF16)
            s = _dot(q_m, kt_slab.astype(BF16))
            pmat = (s * jnp.exp(dmat - m_row)).astype(BF16)
            w_inter = jnp.exp(m_inter - m_row)
            v_aug = jnp.where(sel, v_slab, 1.0).astype(BF16)
            halves.append((_dot(pmat, v_aug) + w_inter * _dot(q_m, ct_pair_b), jnp.exp(-m_row)))

            a_row = b_tot - bc_row + li_row
            m_loc = jnp.max(a_row, axis=-1, keepdims=True)
            w_row = jnp.exp(a_row - m_loc)
            kt_h = kt_slab[half * HEAD_DIM:(half + 1) * HEAD_DIM, :]
            ct_loc = _dot((kt_h * w_row).astype(BF16), v_aug)
            m_new = jnp.maximum(b_tot + m_prev, m_loc)
            s_old = jnp.exp(b_tot + m_prev - m_new)
            s_loc = jnp.exp(m_loc - m_new)
            new_ct.append(s_old * ct_pair[half * HEAD_DIM:(half + 1) * HEAD_DIM, :] + s_loc * ct_loc)
            m_out.append(m_new)

        ct_out.append(jnp.concatenate(new_ct, axis=0))
        (r_even, stab_even), (r_odd, stab_odd) = halves
        num = jnp.where(lo, r_even, r_odd)
        den = pltpu.roll(jnp.where(lo, r_odd, r_even), HEAD_DIM, 1)
        hh = num / jnp.maximum(jnp.abs(den), jnp.where(lo, stab_even, stab_odd))
        hn = _half_layer_norm(hh, lo) * g_ref[:, sl]
        out_ref[:, sl] = (hn * jax.nn.sigmoid(o_ref[:, sl])).astype(out_ref.dtype)
    return ct_out, m_out


def _mlstm(proj3, conv_w, norm_g):
    b, s, _ = proj3.shape
    rows = MLSTM_CHUNKS_PER_STEP * CHUNK
    assert s % rows == 0
    blk = lambda w, off: pl.BlockSpec((None, rows, w), lambda i, c: (i, c, off // w))
    return pl.pallas_call(
        _mlstm_kernel,
        grid=(b, s // rows),
        in_specs=[blk(2 * MLSTM_W, QK_OFF), blk(MLSTM_W, V_OFF), blk(MLSTM_W, O_OFF), blk(LANES, GATE_OFF),
                  pl.BlockSpec((CONV_WIDTH, 2 * MLSTM_W), lambda i, c: (0, 0)),
                  pl.BlockSpec((1, MLSTM_W), lambda i, c: (0, 0))],
        out_specs=pl.BlockSpec((None, rows, MLSTM_W), lambda i, c: (i, c, 0)),
        out_shape=jax.ShapeDtypeStruct((b, s, MLSTM_W), BF16),
        scratch_shapes=[pltpu.VMEM((8 + rows, 2 * MLSTM_W), F32),
                        pltpu.VMEM((MLSTM_HEADS // 2, LANES, LANES), F32),
                        pltpu.VMEM((8, LANES), F32)],
        compiler_params=_params("parallel", "arbitrary"),
        name="mlstm",
    )(proj3, proj3, proj3, proj3, conv_w, norm_g)


def _rope(x, cos_t, sin_t, first):
    return x * cos_t + jnp.where(first, pltpu.roll(x, LANES - ROPE_DIM // 2, 1),
                                 pltpu.roll(x, ROPE_DIM // 2, 1)) * sin_t


def _swa_kernel(sink_ref, q_ref, k_ref, v_ref, cos_ref, sin_ref, su_ref, sv_ref, ws_ref, bias_ref, sg_ref, sb_ref,
                out_ref, gated_ref, kt_ref, vv_ref):
    first_step = pl.program_id(1) == 0

    @pl.when(first_step)
    def _():
        kt_ref[...] = jnp.zeros_like(kt_ref)
        vv_ref[...] = jnp.zeros_like(vv_ref)

    kt_prev = kt_ref[...]
    v_prev = vv_ref[...]
    for i in range(SWA_BLOCKS_PER_STEP):
        rows = pl.ds(i * CHUNK, CHUNK)
        hide_previous = jnp.where(first_step, 2 * CHUNK, 0) if i == 0 else 0
        kt_prev, v_prev = _swa_block(hide_previous, sink_ref, q_ref.at[rows], k_ref.at[rows], v_ref.at[rows],
                                     cos_ref.at[rows], sin_ref.at[rows], out_ref.at[rows], kt_prev, v_prev)
        _sgu_chunk(slice(i * CHUNK, (i + 1) * CHUNK), su_ref, sv_ref, ws_ref, bias_ref, sg_ref, sb_ref, gated_ref)
    kt_ref[...] = kt_prev
    vv_ref[...] = v_prev


def _swa_block(first_block_shift, sink_ref, q_ref, k_ref, v_ref, cos_ref, sin_ref, out_ref, kt_prev, v_prev):
    cos_t = cos_ref[...]
    sin_t = sin_ref[...]
    lane = lax.broadcasted_iota(jnp.int32, (CHUNK, LANES), 1)
    lo = lane < HEAD_DIM
    first = (lane % HEAD_DIM) < ROPE_DIM // 2

    kt_cur = _rope(k_ref[...], cos_t, sin_t, first).T.astype(BF16)
    v_cur = v_ref[...].astype(BF16)
    kt = jnp.concatenate([kt_prev, kt_cur], axis=1)
    vv = jnp.concatenate([v_prev, v_cur], axis=0)

    row = lax.broadcasted_iota(jnp.int32, (CHUNK, 2 * CHUNK), 0)
    col = lax.broadcasted_iota(jnp.int32, (CHUNK, 2 * CHUNK), 1)
    visible = jnp.logical_or(jnp.logical_and(col < CHUNK, col > row + first_block_shift),
                             jnp.logical_and(col >= CHUNK, col - CHUNK <= row))

    for j in range(ATTN_Q_HEADS // 2):
        sl = slice(j * LANES, (j + 1) * LANES)
        q_slab = _rope(q_ref[:, sl], cos_t, sin_t, first) * (HEAD_DIM ** -0.5)
        outs = []
        for half in range(2):
            sel = lo if half == 0 else jnp.logical_not(lo)
            sink = sink_ref[ATTN_HEAD_ORDER[2 * j + half]]
            q_m = jnp.where(sel, q_slab, 0.0).astype(BF16)
            s = jnp.where(visible, _dot(q_m, kt), -jnp.inf)
            mx = jnp.maximum(jnp.max(s, axis=-1, keepdims=True), sink)
            pexp = jnp.exp(s - mx)
            denom = jnp.sum(pexp, axis=-1, keepdims=True) + jnp.exp(sink - mx)
            outs.append((_dot(pexp.astype(BF16), vv), denom))
        (num_lo, den_lo), (num_hi, den_hi) = outs
        out_ref[:, sl] = (jnp.where(lo, num_lo, num_hi) / jnp.where(lo, den_lo, den_hi)).astype(out_ref.dtype)

    return kt_cur, v_cur


def _swa_sgu(proj3, cos_t, sin_t, sinks, w_tril, bias_tok, norm_g, norm_b):
    b, s, _ = proj3.shape
    rows = SWA_BLOCKS_PER_STEP * CHUNK
    assert s % rows == 0
    blk = lambda w, off: pl.BlockSpec((None, rows, w), lambda i, c: (i, c, off // w))
    tab = pl.BlockSpec((None, rows, LANES), lambda i, c: (i, c, 0))
    const2 = lambda shape: pl.BlockSpec(shape, lambda i, c: (0,) * len(shape))
    out = lambda w: pl.BlockSpec((None, rows, w), lambda i, c: (i, c, 0))
    return pl.pallas_call(
        _swa_kernel,
        grid=(b, s // rows),
        in_specs=[pl.BlockSpec(memory_space=pltpu.SMEM),
                  blk(ATTN_W, AQ_OFF), blk(ATTN_KV_W, AK_OFF), blk(ATTN_KV_W, AV_OFF), tab, tab,
                  blk(SGU_W, SU_OFF), blk(SGU_W, SV_OFF), const2((SGU_GROUPS, CHUNK, CHUNK)),
                  const2((CHUNK, SGU_W)), const2((1, SGU_W)), const2((1, SGU_W))],
        out_specs=[out(ATTN_W), out(SGU_W)],
        out_shape=[jax.ShapeDtypeStruct((b, s, ATTN_W), BF16), jax.ShapeDtypeStruct((b, s, SGU_W), BF16)],
        scratch_shapes=[pltpu.VMEM((LANES, CHUNK), BF16), pltpu.VMEM((CHUNK, LANES), BF16)],
        compiler_params=_params("parallel", "arbitrary"),
        name="swa_sgu",
    )(sinks, proj3, proj3, proj3, cos_t, sin_t, proj3, proj3, w_tril, bias_tok, norm_g, norm_b)


def _sgu_chunk(rows, u_ref, v_ref, w_ref, bias_ref, g_ref, b_ref, out_ref):
    lo = _lane_lo((CHUNK, LANES))
    for j in range(SGU_GROUPS // 2):
        sl = slice(j * LANES, (j + 1) * LANES)
        u = jax.nn.gelu(u_ref[rows, sl])
        v = jax.nn.gelu(v_ref[rows, sl])
        vn = (_half_layer_norm(v, lo) * g_ref[:, sl] + b_ref[:, sl]).astype(BF16)
        mixed = jnp.where(lo, _dot(w_ref[2 * j], vn), _dot(w_ref[2 * j + 1], vn)) + bias_ref[:, sl]
        out_ref[rows, sl] = (u * mixed).astype(out_ref.dtype)


def _outproj_kernel(ha_ref, hb_ref, hc_ref, x_ref, w_ref, g_ref, b_ref, *rest):
    cat_ref = rest[-1]
    cat_ref[:, 0:MLSTM_W] = ha_ref[...]
    cat_ref[:, MLSTM_W:MLSTM_W + ATTN_W] = hb_ref[...]
    cat_ref[:, MLSTM_W + ATTN_W:] = hc_ref[...]
    mix = _dot(cat_ref[...], w_ref[...])
    y = _layer_norm_rows(DN_ALPHA * x_ref[...] + mix, g_ref[...], b_ref[...])
    rest = rest[:-1]
    if len(rest) == 1:
        rest[0][...] = y
        return
    w_split_ref, rb_ref, o_ref, route_ref, route_t_ref = rest
    o_ref[...] = y
    route_ref[...], route_t_ref[...] = _route_rows(y, w_split_ref[...], rb_ref[...])


def _outproj_ln(ha, hb, hc, x2d, w, g, b, router=None):
    n = x2d.shape[0]
    rows = lambda w: pl.BlockSpec((OUTPROJ_ROWS, w), lambda i: (i, 0))
    const = lambda shape: pl.BlockSpec(shape, lambda i: (0, 0))
    in_specs = [rows(MLSTM_W), rows(ATTN_W), rows(SGU_W), rows(D_MODEL),
                const((D_MODEL, D_MODEL)), const((1, D_MODEL)), const((1, D_MODEL))]
    out_specs = [rows(D_MODEL)]
    out_shape = [jax.ShapeDtypeStruct((n, D_MODEL), F32)]
    operands = (ha, hb, hc, x2d, w, g, b)
    if router is not None:
        in_specs += [const((D_MODEL, 2 * LANES)), const((1, LANES))]
        out_specs += [rows(LANES), pl.BlockSpec((N_EXPERTS, OUTPROJ_ROWS), lambda i: (0, i))]
        out_shape += [jax.ShapeDtypeStruct((n, LANES), F32), jax.ShapeDtypeStruct((N_EXPERTS, n), F32)]
        operands += tuple(router)
    out = pl.pallas_call(
        _outproj_kernel,
        grid=(n // OUTPROJ_ROWS,),
        in_specs=in_specs,
        out_specs=out_specs,
        out_shape=out_shape,
        scratch_shapes=[pltpu.VMEM((OUTPROJ_ROWS, D_MODEL), BF16)],
        compiler_params=_params("parallel"),
        name="outproj_ln" if router is None else "outproj_ln_route",
    )(*operands)
    return out[0] if router is None else tuple(out)


def _ffn_kernel(x_ref, wg_ref, wu_ref, wd_ref, g_ref, b_ref, o_ref, xb_ref, wgu_ref):
    j = pl.program_id(1)
    last = pl.num_programs(1) - 1
    n_sub = FFN_ROWS // FFN_SUB
    wgu_ref[:, :FF_TILE] = wg_ref[...]
    wgu_ref[:, FF_TILE:] = wu_ref[...]

    def sub_rows(m):
        return pl.ds(pl.multiple_of(m * FFN_SUB, FFN_SUB), FFN_SUB)

    def ffn_tile(m):
        gu = _dot(xb_ref[sub_rows(m), :], wgu_ref[...])
        hidden = (jax.nn.silu(gu[:, :FF_TILE]) * gu[:, FF_TILE:]).astype(BF16)
        return _dot(hidden, wd_ref[...])

    @pl.when(j == 0)
    def _():
        def body(m, carry):
            xb_ref[sub_rows(m), :] = x_ref[sub_rows(m), :].astype(BF16)
            o_ref[sub_rows(m), :] = ffn_tile(m)
            return carry

        lax.fori_loop(0, n_sub, body, 0)

    @pl.when(jnp.logical_and(j > 0, j < last))
    def _():
        def body(m, carry):
            o_ref[sub_rows(m), :] += ffn_tile(m)
            return carry

        lax.fori_loop(0, n_sub, body, 0)

    @pl.when(j == last)
    def _():
        def body(m, carry):
            rows = sub_rows(m)
            ff = o_ref[rows, :] + ffn_tile(m)
            o_ref[rows, :] = _layer_norm_rows(DN_ALPHA * x_ref[rows, :] + ff, g_ref[...], b_ref[...])
            return carry

        lax.fori_loop(0, n_sub, body, 0)


def _ffn_ln(x2d, wg, wu, wd, g, b):
    n = x2d.shape[0]
    d_ff = wd.shape[0]
    assert d_ff // FF_TILE >= 2
    tm = FFN_ROWS
    return pl.pallas_call(
        _ffn_kernel,
        grid=(n // tm, d_ff // FF_TILE),
        in_specs=[pl.BlockSpec((tm, D_MODEL), lambda i, j: (i, 0)),
                  pl.BlockSpec((D_MODEL, FF_TILE), lambda i, j: (0, j)),
                  pl.BlockSpec((D_MODEL, FF_TILE), lambda i, j: (0, j)),
                  pl.BlockSpec((FF_TILE, D_MODEL), lambda i, j: (j, 0)),
                  pl.BlockSpec((1, D_MODEL), lambda i, j: (0, 0)),
                  pl.BlockSpec((1, D_MODEL), lambda i, j: (0, 0))],
        out_specs=pl.BlockSpec((tm, D_MODEL), lambda i, j: (i, 0)),
        out_shape=jax.ShapeDtypeStruct((n, D_MODEL), F32),
        scratch_shapes=[pltpu.VMEM((tm, D_MODEL), BF16), pltpu.VMEM((D_MODEL, 2 * FF_TILE), BF16)],
        compiler_params=_params("parallel", "arbitrary"),
        name="ffn_ln",
    )(x2d, wg, wu, wd, g, b)


def _route_rows(x, w_split, bias):
    n = x.shape[0]
    xh = x.astype(BF16)
    xl = (x - xh.astype(F32)).astype(BF16)
    prod = _dot(jnp.concatenate([xh, xl], axis=0), w_split)
    logits = prod[:n, :LANES] + prod[:n, LANES:] + prod[n:, :LANES] + prod[n:, LANES:] + bias
    lt = logits.T[0:N_EXPERTS, :]
    expert = lax.broadcasted_iota(jnp.int32, lt.shape, 0)
    m1 = jnp.max(lt, axis=0, keepdims=True)
    i1 = jnp.min(jnp.where(lt == m1, expert, N_EXPERTS), axis=0, keepdims=True)
    rest = jnp.where(expert == i1, -jnp.inf, lt)
    m2 = jnp.max(rest, axis=0, keepdims=True)
    i2 = jnp.min(jnp.where(rest == m2, expert, N_EXPERTS), axis=0, keepdims=True)
    e2 = jnp.exp(m2 - m1)
    g1 = 1.0 / (1.0 + e2)
    g2 = e2 / (1.0 + e2)
    route_t = jnp.where(expert == 0, g1, jnp.where(expert == 1, g2, jnp.where(
        expert == 2, i1.astype(F32), jnp.where(expert == 3, i2.astype(F32), 0.0))))
    padded = jnp.concatenate([route_t, jnp.zeros((LANES - N_EXPERTS, n), F32)], axis=0)
    return padded.T, route_t


TOKEN_TILE = (SUBLANES, D_MODEL // SUBLANES)


def _dispatch_kernel(slot0_ref, slot1_ref, ends_ref, x_ref, o_hbm, stage_ref, zero_ref, sem, zero_sem):
    t = pl.program_id(0)
    last = pl.num_programs(0) - 1
    s = t % 2

    def wait_tile(ss):
        for _ in range(TOP_K):
            pltpu.make_async_copy(zero_ref, o_hbm.at[pl.ds(0, MOE_SUB)], sem.at[ss]).wait()

    @pl.when(t == 0)
    def _():
        zero_ref[...] = jnp.zeros_like(zero_ref)

        def zero_sub_tile(m, carry):
            dst = o_hbm.at[pl.ds(pl.multiple_of(m * MOE_SUB, MOE_SUB), MOE_SUB)]
            cp = pltpu.make_async_copy(zero_ref, dst, zero_sem)
            cp.start()
            cp.wait()
            return carry

        for e in range(N_EXPERTS):
            @pl.when(ends_ref[e] >= MOE_SUB)
            def _():
                zero_sub_tile(ends_ref[e] // MOE_SUB - 1, 0)

        lax.fori_loop(ends_ref[N_EXPERTS - 1] // MOE_SUB, o_hbm.shape[0] // MOE_SUB, zero_sub_tile, 0)

    @pl.when(t >= 2)
    def _():
        wait_tile(s)

    stage_ref[s] = x_ref[...].reshape((MOE_SUB,) + TOKEN_TILE)

    def issue(i, carry):
        for u in range(SUBLANES):
            r = i * SUBLANES + u
            for k, slot_ref in enumerate((slot0_ref, slot1_ref)):
                pltpu.make_async_copy(stage_ref.at[s, r], o_hbm.at[slot_ref[t * MOE_SUB + r]],
                                      sem.at[s]).start(priority=k)
        return carry

    lax.fori_loop(0, MOE_SUB // SUBLANES, issue, 0)

    @pl.when(t == last)
    def _():
        wait_tile(s)
        wait_tile(1 - s)


def _dispatch_rows(x2d, slots, ends, n_rows):
    n = x2d.shape[0]
    assert n // MOE_SUB >= 2
    return pl.pallas_call(
        _dispatch_kernel,
        grid_spec=pltpu.PrefetchScalarGridSpec(
            num_scalar_prefetch=3,
            grid=(n // MOE_SUB,),
            in_specs=[pl.BlockSpec((MOE_SUB, D_MODEL), lambda t, s0, s1, en: (t, 0))],
            out_specs=pl.BlockSpec(memory_space=pl.ANY),
            scratch_shapes=[pltpu.VMEM((2, MOE_SUB) + TOKEN_TILE, F32), pltpu.VMEM((MOE_SUB,) + TOKEN_TILE, F32),
                            pltpu.SemaphoreType.DMA((2,)), pltpu.SemaphoreType.DMA(())]),
        out_shape=jax.ShapeDtypeStruct((n_rows,) + TOKEN_TILE, F32),
        compiler_params=_params("arbitrary"),
        name="moe_dispatch",
    )(slots[0], slots[1], ends, x2d)


def _moe_kernel(exp_ref, row0_ref, nsub_ref, tail_ref, x_hbm, wg_ref, wu_ref, wd_ref, y_hbm,
                xb_ref, acc_ref, wgu_ref, wdb_ref, stage_ref, in_sem, out_sem):
    v = pl.program_id(0)
    j = pl.program_id(1)
    last = pl.num_programs(1) - 1
    n_sub = nsub_ref[v]
    row0 = row0_ref[v]

    def sub_rows(m):
        return pl.ds(pl.multiple_of(m * MOE_SUB, MOE_SUB), MOE_SUB)

    def hbm_rows(m):
        return pl.ds(pl.multiple_of(row0 + m * MOE_SUB, MOE_SUB), MOE_SUB)

    @pl.when(jnp.logical_and(v == 0, j == 0))
    def _():
        stage_ref[0] = jnp.zeros((MOE_SUB,) + TOKEN_TILE, F32)

        def zero_sub_tile(m, carry):
            cp = pltpu.make_async_copy(stage_ref.at[0], y_hbm.at[sub_rows(m)], out_sem.at[0])
            cp.start()
            cp.wait()
            return carry

        lax.fori_loop(tail_ref[0], y_hbm.shape[0] // MOE_SUB, zero_sub_tile, 0)

    @pl.when(n_sub > 0)
    def _():
        wgu_ref[:, :MOE_FF_TILE] = wg_ref[...].astype(BF16)
        wgu_ref[:, MOE_FF_TILE:] = wu_ref[...].astype(BF16)
        wdb_ref[...] = wd_ref[...].astype(BF16)

        def ffn_rows(rows):
            gu = _dot(xb_ref[rows, :], wgu_ref[...])
            hidden = (jax.nn.silu(gu[:, :MOE_FF_TILE]) * gu[:, MOE_FF_TILE:]).astype(BF16)
            return _dot(hidden, wdb_ref[...])

        def ffn_tile(m):
            return ffn_rows(sub_rows(m))

        def accumulate(after_sub_tile):
            def pair(k, carry):
                rows = pl.ds(pl.multiple_of(2 * k * MOE_SUB, 2 * MOE_SUB), 2 * MOE_SUB)
                acc_ref[rows, :] += ffn_rows(rows)
                after_sub_tile(2 * k)
                after_sub_tile(2 * k + 1)
                return carry

            lax.fori_loop(0, n_sub // 2, pair, 0)

            @pl.when(n_sub % 2 == 1)
            def _():
                acc_ref[sub_rows(n_sub - 1), :] += ffn_tile(n_sub - 1)
                after_sub_tile(n_sub - 1)

        def y_copy(m):
            return pltpu.make_async_copy(stage_ref.at[m % 2], y_hbm.at[hbm_rows(m)], out_sem.at[m % 2])

        def send_sub_tile(m):
            @pl.when(m >= 2)
            def _():
                y_copy(m - 2).wait()

            stage_ref[m % 2] = acc_ref[sub_rows(m), :].reshape((MOE_SUB,) + TOKEN_TILE)
            y_copy(m).start()

        @pl.when(j == 0)
        def _():
            def x_copy(m):
                return pltpu.make_async_copy(x_hbm.at[hbm_rows(m)], stage_ref.at[m % 2], in_sem.at[m % 2])

            x_copy(0).start()

            def body(m, carry):
                @pl.when(m + 1 < n_sub)
                def _():
                    x_copy(m + 1).start()

                x_copy(m).wait()
                xb_ref[sub_rows(m), :] = stage_ref[m % 2].reshape(MOE_SUB, D_MODEL).astype(BF16)
                acc_ref[sub_rows(m), :] = ffn_tile(m)
                return carry

            lax.fori_loop(0, n_sub, body, 0)

        @pl.when(jnp.logical_and(j > 0, j < last))
        def _():
            accumulate(lambda m: None)

        @pl.when(j == last)
        def _():
            accumulate(send_sub_tile)

            @pl.when(n_sub >= 2)
            def _():
                y_copy(n_sub - 2).wait()

            y_copy(n_sub - 1).wait()


def _moe_grouped(xs, wg, wu, wd, visit_exp, visit_row0, visit_nsub, tail_sub):
    n_rows = xs.shape[0]
    n_visits = visit_exp.shape[0]
    d_ff = wg.shape[2]
    assert d_ff // MOE_FF_TILE >= 2
    rows = MOE_VISIT_SUBS * MOE_SUB
    w_in = lambda v, j, e, r, ns, tl: (e[v], 0, j)
    w_out = lambda v, j, e, r, ns, tl: (e[v], j, 0)
    return pl.pallas_call(
        _moe_kernel,
        grid_spec=pltpu.PrefetchScalarGridSpec(
            num_scalar_prefetch=4,
            grid=(n_visits, d_ff // MOE_FF_TILE),
            in_specs=[pl.BlockSpec(memory_space=pl.ANY),
                      pl.BlockSpec((None, D_MODEL, MOE_FF_TILE), w_in),
                      pl.BlockSpec((None, D_MODEL, MOE_FF_TILE), w_in),
                      pl.BlockSpec((None, MOE_FF_TILE, D_MODEL), w_out)],
            out_specs=pl.BlockSpec(memory_space=pl.ANY),
            scratch_shapes=[pltpu.VMEM((rows, D_MODEL), BF16), pltpu.VMEM((rows, D_MODEL), F32),
                            pltpu.VMEM((D_MODEL, 2 * MOE_FF_TILE), BF16),
                            pltpu.VMEM((MOE_FF_TILE, D_MODEL), BF16),
                            pltpu.VMEM((2, MOE_SUB) + TOKEN_TILE, F32),
                            pltpu.SemaphoreType.DMA((2,)), pltpu.SemaphoreType.DMA((2,))]),
        out_shape=jax.ShapeDtypeStruct((n_rows,) + TOKEN_TILE, F32),
        compiler_params=_params("arbitrary", "arbitrary"),
        name="moe_grouped",
    )(visit_exp, visit_row0, visit_nsub, tail_sub, xs, wg, wu, wd)


def _combine_kernel(slot0_ref, slot1_ref, x_ref, gate_ref, y_hbm, g_ref, b_ref, o_ref, buf_ref, sem):
    t = pl.program_id(0)
    s = t % 2

    def issue_tile(tt, ss):
        def issue(i, carry):
            for u in range(SUBLANES):
                r = i * SUBLANES + u
                for k, slot_ref in enumerate((slot0_ref, slot1_ref)):
                    pltpu.make_async_copy(y_hbm.at[slot_ref[tt * MOE_SUB + r]], buf_ref.at[ss, k, r],
                                          sem.at[ss]).start(priority=k)
            return carry

        lax.fori_loop(0, MOE_SUB // SUBLANES, issue, 0)

    @pl.when(t == 0)
    def _():
        issue_tile(0, 0)

    @pl.when(t + 1 < pl.num_programs(0))
    def _():
        issue_tile(t + 1, 1 - s)

    for k in range(TOP_K):
        pltpu.make_async_copy(y_hbm.at[pl.ds(0, MOE_SUB)], buf_ref.at[s, k], sem.at[s]).wait()
    gate = gate_ref[...]
    y0 = buf_ref[s, 0].reshape(MOE_SUB, D_MODEL)
    y1 = buf_ref[s, 1].reshape(MOE_SUB, D_MODEL)
    ff = gate[:, 0:1] * y0 + gate[:, 1:2] * y1
    o_ref[...] = _layer_norm_rows(DN_ALPHA * x_ref[...] + ff, g_ref[...], b_ref[...])


def _combine_ln(x2d, route, slots, ys, g, b):
    n = x2d.shape[0]
    return pl.pallas_call(
        _combine_kernel,
        grid_spec=pltpu.PrefetchScalarGridSpec(
            num_scalar_prefetch=2,
            grid=(n // MOE_SUB,),
            in_specs=[pl.BlockSpec((MOE_SUB, D_MODEL), lambda t, s0, s1: (t, 0)),
                      pl.BlockSpec((MOE_SUB, LANES), lambda t, s0, s1: (t, 0)),
                      pl.BlockSpec(memory_space=pl.ANY),
                      pl.BlockSpec((1, D_MODEL), lambda t, s0, s1: (0, 0)),
                      pl.BlockSpec((1, D_MODEL), lambda t, s0, s1: (0, 0))],
            out_specs=pl.BlockSpec((MOE_SUB, D_MODEL), lambda t, s0, s1: (t, 0)),
            scratch_shapes=[pltpu.VMEM((2, TOP_K, MOE_SUB) + TOKEN_TILE, F32), pltpu.SemaphoreType.DMA((2,))]),
        out_shape=jax.ShapeDtypeStruct((n, D_MODEL), F32),
        compiler_params=_params("arbitrary"),
        name="moe_combine_ln",
    )(slots[0], slots[1], x2d, route, ys, g, b)


def _routing_tables(route_t, n):
    idx = route_t[2:4].astype(jnp.int32)
    expert = jnp.arange(N_EXPERTS, dtype=jnp.int32)[:, None]
    chosen = [idx[k][None, :] == expert for k in range(TOP_K)]
    onehot = jnp.logical_or(chosen[0], chosen[1]).astype(jnp.int32)
    rank = jnp.cumsum(onehot, axis=1) - onehot
    counts = jnp.sum(onehot, axis=1)
    padded = ((counts + MOE_SUB - 1) // MOE_SUB) * MOE_SUB
    ends = jnp.cumsum(padded)
    starts = ends - padded
    place = starts[:, None] + rank
    slot = [jnp.sum(jnp.where(chosen[k], place, 0), axis=0).astype(jnp.int32) for k in range(TOP_K)]

    n_rows = -(-(n * TOP_K + N_EXPERTS * (MOE_SUB - 1)) // MOE_SUB) * MOE_SUB

    visit_rows = MOE_VISIT_SUBS * MOE_SUB
    max_chunks = -(-n_rows // visit_rows)
    chunk = jnp.arange(max_chunks, dtype=jnp.int32)[None, :]
    left = padded[:, None] - chunk * visit_rows
    valid = (left > 0).reshape(-1)
    n_visits = n_rows // visit_rows + N_EXPERTS
    order = jnp.argsort(jnp.logical_not(valid), stable=True)[:n_visits]
    n_valid = jnp.sum(valid.astype(jnp.int32))
    live = jnp.arange(n_visits) < n_valid
    order = jnp.where(live, order, order[jnp.maximum(n_valid - 1, 0)])
    v_exp = (order // max_chunks).astype(jnp.int32)
    v_chunk = (order % max_chunks).astype(jnp.int32)
    v_row0 = jnp.where(live, starts[v_exp] + v_chunk * visit_rows, 0).astype(jnp.int32)
    v_nsub = jnp.where(live, jnp.minimum(left.reshape(-1)[order], visit_rows) // MOE_SUB, 0).astype(jnp.int32)
    tail_sub = (ends[N_EXPERTS - 1:] // MOE_SUB).astype(jnp.int32)
    return slot, ends.astype(jnp.int32), n_rows, (v_exp, v_row0, v_nsub, tail_sub)


def _router_operands(w_router, b_router):
    w_pad = jnp.zeros((D_MODEL, LANES), F32).at[:, :N_EXPERTS].set(w_router)
    w_hi = w_pad.astype(BF16)
    w_lo = (w_pad - w_hi.astype(F32)).astype(BF16)
    b_pad = jnp.zeros((1, LANES), F32).at[0, :N_EXPERTS].set(b_router)
    return jnp.concatenate([w_hi, w_lo], axis=1), b_pad


def _moe_ln(x2d, route, route_t, wg, wu, wd, g, b):
    n = x2d.shape[0]
    slots, ends, n_rows, visits = _routing_tables(route_t, n)
    xs = _dispatch_rows(x2d, slots, ends, n_rows)
    ys = _moe_grouped(xs, wg, wu, wd, *visits)
    return _combine_ln(x2d, route, slots, ys, g, b)


_SRC_GATES = 4 * MLSTM_W
_SRC_AQ = _SRC_GATES + 2 * MLSTM_HEADS
_SRC_AK = _SRC_AQ + ATTN_W
_IN_PROJ_MOVES = (
    ((0, 0, 4 * MLSTM_W),)
    + tuple((_SRC_AQ + h * HEAD_DIM, AQ_OFF + i * HEAD_DIM, HEAD_DIM) for i, h in enumerate(ATTN_HEAD_ORDER))
    + ((_SRC_AK, AK_OFF, ATTN_KV_W), (_SRC_AK + ATTN_KV_W, AV_OFF, ATTN_KV_W),
       (_SRC_AK + 2 * ATTN_KV_W, SU_OFF, SGU_W), (_SRC_AK + 2 * ATTN_KV_W + SGU_W, SV_OFF, SGU_W)))
D_PROJ = _SRC_AK + 2 * ATTN_KV_W + 2 * SGU_W


def _relayout_columns(src, dst_dtype):
    source = np.full((D_PROJ_PAD,), D_PROJ, np.int32)
    for s, d, w in _IN_PROJ_MOVES + ((_SRC_GATES, GATE_OFF, 2 * MLSTM_HEADS),):
        source[d:d + w] = np.arange(s, s + w)
    padded = jnp.concatenate([src, jnp.zeros(src.shape[:-1] + (1,), src.dtype)], axis=-1)
    return jnp.take(padded, source, axis=-1).astype(dst_dtype)


def _w_layout_kernel(w_ref, o_ref):
    for s, d, w in _IN_PROJ_MOVES:
        o_ref[:, d:d + w] = w_ref[:, s:s + w].astype(o_ref.dtype)
    pad = jnp.zeros((w_ref.shape[0], LANES - 2 * MLSTM_HEADS), F32)
    gates = jnp.concatenate([w_ref[:, _SRC_GATES:_SRC_GATES + 2 * MLSTM_HEADS], pad], axis=1)
    o_ref[:, GATE_OFF:GATE_OFF + LANES] = gates.astype(o_ref.dtype)


def _layout_in_proj(w_in, b_in):
    rows = D_MODEL // 4

    def one_layer(layer):
        return pl.pallas_call(
            _w_layout_kernel,
            grid=(D_MODEL // rows,),
            in_specs=[pl.BlockSpec((None, rows, D_PROJ), lambda i: (layer, i, 0))],
            out_specs=pl.BlockSpec((rows, D_PROJ_PAD), lambda i: (i, 0)),
            out_shape=jax.ShapeDtypeStruct((D_MODEL, D_PROJ_PAD), BF16),
            compiler_params=_params("parallel"),
            name="w_in_layout",
        )(w_in)

    return [one_layer(layer) for layer in range(w_in.shape[0])], _relayout_columns(b_in, F32)[:, None, :]


def _rope_tables(positions):
    inv_freq = ROPE_THETA ** (-jnp.arange(0, ROPE_DIM, 2, dtype=F32) / ROPE_DIM)
    ang = inv_freq[None, :, None] * positions.astype(F32)[:, None, :]
    cos, sin = jnp.cos(ang), jnp.sin(ang)
    ones = jnp.ones((ang.shape[0], HEAD_DIM - ROPE_DIM, ang.shape[2]), F32)
    cos_head = jnp.concatenate([cos, cos, ones], 1)
    sin_head = jnp.concatenate([-sin, sin, 0.0 * ones], 1)
    cos_t = jnp.concatenate([cos_head, cos_head], 1)
    sin_t = jnp.concatenate([sin_head, sin_head], 1)
    return _to_token_major(cos_t, sin_t)


def _table_transpose_kernel(cos_ref, sin_ref, cos_out, sin_out):
    cos_out[...] = cos_ref[...].T
    sin_out[...] = sin_ref[...].T


def _to_token_major(cos_t, sin_t):
    b, lanes, s = cos_t.shape
    rows = min(s, FFN_ROWS)
    src = pl.BlockSpec((None, lanes, rows), lambda i, c: (i, 0, c))
    dst = pl.BlockSpec((None, rows, lanes), lambda i, c: (i, c, 0))
    shape = jax.ShapeDtypeStruct((b, s, lanes), F32)
    return pl.pallas_call(
        _table_transpose_kernel,
        grid=(b, s // rows),
        in_specs=[src, src],
        out_specs=[dst, dst],
        out_shape=[shape, shape],
        compiler_params=_params("parallel", "parallel"),
        name="rope_tables",
    )(cos_t, sin_t)


def kernel(x, positions, w_in, b_in, conv_w, mlstm_norm_g, attn_sinks, sgu_w_s, sgu_b_s, sgu_norm_g, sgu_norm_b, w_out, ln1_g, ln1_b, ln2_g, ln2_b, ffn_w_gate, ffn_w_up, ffn_w_down, moe_w_router, moe_b_router, moe_w_gate, moe_w_up, moe_w_down):
    bsz, seq, _ = x.shape
    n = bsz * seq
    cos_t, sin_t = _rope_tables(positions)
    tril = jnp.tril(jnp.ones((CHUNK, CHUNK), bool))
    x2d = x.reshape(n, D_MODEL)
    w_p, b_p = _layout_in_proj(w_in, b_in)
    for layer in range(DEPTH):
        proj3 = _inproj(x2d, w_p[layer], b_p[layer]).reshape(bsz, seq, D_PROJ_PAD)
        h_a = _mlstm(proj3, conv_w[layer], mlstm_norm_g[layer][None, :])
        w_tril = jnp.where(tril, sgu_w_s[layer], 0.0).astype(BF16)
        bias_tok = jnp.repeat(sgu_b_s[layer].T, HEAD_DIM, axis=1)
        h_b, h_c = _swa_sgu(proj3, cos_t, sin_t, attn_sinks[layer], w_tril, bias_tok,
                            sgu_norm_g[layer][None, :], sgu_norm_b[layer][None, :])
        wo = w_out[layer]
        wb = wo[MLSTM_W:MLSTM_W + ATTN_W].reshape(ATTN_Q_HEADS, HEAD_DIM, D_MODEL)[np.array(ATTN_HEAD_ORDER)]
        wo = jnp.concatenate([wo[:MLSTM_W], wb.reshape(ATTN_W, D_MODEL), wo[MLSTM_W + ATTN_W:]], 0).astype(BF16)
        j = layer // 2
        dense = layer % 2 == 0
        mixed = _outproj_ln(h_a.reshape(n, MLSTM_W), h_b.reshape(n, ATTN_W), h_c.reshape(n, SGU_W), x2d,
                            wo, ln1_g[layer][None, :], ln1_b[layer][None, :],
                            router=None if dense else _router_operands(moe_w_router[j], moe_b_router[j]))
        g2, b2 = ln2_g[layer][None, :], ln2_b[layer][None, :]
        if dense:
            x2d = _ffn_ln(mixed, ffn_w_gate[j].astype(BF16), ffn_w_up[j].astype(BF16),
                          ffn_w_down[j].astype(BF16), g2, b2)
        else:
            x2d, route, route_t = mixed
            x2d = _moe_ln(x2d, route, route_t, moe_w_gate[j], moe_w_up[j], moe_w_down[j], g2, b2)
    return x2d.reshape(bsz, seq, D_MODEL)
```

```python
import jax
import jax.numpy as jnp
import numpy as np
from jax import lax
from jax.experimental import pallas as pl
from jax.experimental.pallas import tpu as pltpu

F32 = jnp.float32
BF16 = jnp.bfloat16

D_MODEL = 1024
HEAD_DIM = 64
LANES = 128
SUBLANES = 8
MLSTM_HEADS = 6
ATTN_Q_HEADS = 6
ATTN_KV_HEADS = 2
SGU_GROUPS = 4
MLSTM_W = MLSTM_HEADS * HEAD_DIM
ATTN_W = ATTN_Q_HEADS * HEAD_DIM
ATTN_KV_W = ATTN_KV_HEADS * HEAD_DIM
SGU_W = SGU_GROUPS * HEAD_DIM
CHUNK = 128
CONV_WIDTH = 4
ROPE_DIM = HEAD_DIM // 4
ROPE_THETA = 500000.0
N_EXPERTS = 8
TOP_K = 2
DEPTH = 2
DN_ALPHA = (2.0 * DEPTH) ** 0.25
LN_EPS = 1e-5

QK_OFF, V_OFF, O_OFF, AQ_OFF = 0, 768, 1152, 1536
GATE_OFF, AK_OFF, AV_OFF, SU_OFF, SV_OFF = 1920, 2048, 2176, 2304, 2560
D_PROJ_PAD = 2816
ATTN_HEAD_ORDER = (0, 3, 1, 4, 2, 5)

VMEM_LIMIT = 56 * 1024 * 1024

MLSTM_CHUNKS_PER_STEP = 8
SWA_BLOCKS_PER_STEP = 2
ROW_TILE = 1024
OUTPROJ_ROWS = 1024
FFN_ROWS = 2048
FFN_SUB = 2048
FF_TILE = 256
MOE_SUB = 512
MOE_VISIT_SUBS = 9
MOE_FF_TILE = 512


def _params(*sem):
    return pltpu.CompilerParams(dimension_semantics=sem, vmem_limit_bytes=VMEM_LIMIT)


def _lane_lo(shape):
    return lax.broadcasted_iota(jnp.int32, shape, len(shape) - 1) < HEAD_DIM


def _layer_norm_rows(z, g, b):
    mu = jnp.mean(z, axis=-1, keepdims=True)
    zc = z - mu
    var = jnp.mean(zc * zc, axis=-1, keepdims=True)
    return zc * lax.rsqrt(var + LN_EPS) * g + b


def _half_layer_norm(x, lo):
    inv = 1.0 / HEAD_DIM
    s_lo = jnp.sum(jnp.where(lo, x, 0.0), axis=-1, keepdims=True)
    s_all = jnp.sum(x, axis=-1, keepdims=True)
    mu = jnp.where(lo, s_lo, s_all - s_lo) * inv
    xc = x - mu
    sq = xc * xc
    q_lo = jnp.sum(jnp.where(lo, sq, 0.0), axis=-1, keepdims=True)
    q_all = jnp.sum(sq, axis=-1, keepdims=True)
    var = jnp.where(lo, q_lo, q_all - q_lo) * inv
    return xc * lax.rsqrt(var + LN_EPS)


def _split3(a):
    h1 = a.astype(BF16)
    r1 = a - h1.astype(F32)
    h2 = r1.astype(BF16)
    r2 = r1 - h2.astype(F32)
    return h1, h2, r2.astype(BF16)


def _split3_masked(a):
    def top(x):
        bits = lax.bitcast_convert_type(x, jnp.int32) & jnp.int32(-65536)
        return lax.bitcast_convert_type(bits, F32)

    h1 = top(a)
    r1 = a - h1
    h2 = top(r1)
    return h1.astype(BF16), h2.astype(BF16), (r1 - h2).astype(BF16)


def _dot(a, b):
    return jnp.dot(a, b, preferred_element_type=F32)


def _inproj_kernel(x_ref, w_ref, b_ref, o_ref):
    o_ref[...] = _dot(x_ref[...].astype(BF16), w_ref[...]) + b_ref[...]


def _inproj(x2d, w, b):
    n = x2d.shape[0]
    return pl.pallas_call(
        _inproj_kernel,
        grid=(n // ROW_TILE,),
        in_specs=[pl.BlockSpec((ROW_TILE, D_MODEL), lambda i: (i, 0)),
                  pl.BlockSpec((D_MODEL, D_PROJ_PAD), lambda i: (0, 0)),
                  pl.BlockSpec((1, D_PROJ_PAD), lambda i: (0, 0))],
        out_specs=pl.BlockSpec((ROW_TILE, D_PROJ_PAD), lambda i: (i, 0)),
        out_shape=jax.ShapeDtypeStruct((n, D_PROJ_PAD), F32),
        compiler_params=_params("parallel"),
        name="inproj",
    )(x2d, w, b)


def _mlstm_kernel(qk_ref, v_ref, o_ref, gate_ref, cw_ref, g_ref, out_ref, buf_ref, ct_ref, m_ref):
    rows = MLSTM_CHUNKS_PER_STEP * CHUNK

    @pl.when(pl.program_id(1) == 0)
    def _():
        buf_ref[0:8, :] = jnp.zeros((8, 2 * MLSTM_W), F32)
        ct_ref[...] = jnp.zeros_like(ct_ref)
        m_ref[...] = jnp.zeros_like(m_ref)

    buf_ref[8:8 + rows, :] = qk_ref[...]
    ct = [ct_ref[p] for p in range(MLSTM_HEADS // 2)]
    m = [m_ref[h:h + 1, 0:1] for h in range(MLSTM_HEADS)]
    for i in range(MLSTM_CHUNKS_PER_STEP):
        blk = pl.ds(i * CHUNK, CHUNK)
        ct, m = _mlstm_chunk(i * CHUNK, buf_ref, v_ref.at[blk], o_ref.at[blk], gate_ref.at[blk], cw_ref, g_ref,
                             out_ref.at[blk], ct, m)
    buf_ref[0:8, :] = qk_ref[rows - 8:rows, :]
    for p in range(MLSTM_HEADS // 2):
        ct_ref[p] = ct[p]
    for h in range(MLSTM_HEADS):
        m_ref[h:h + 1, :] = jnp.broadcast_to(m[h], (1, LANES))


def _mlstm_chunk(row0, buf_ref, v_ref, o_ref, gate_ref, cw_ref, g_ref, out_ref, ct_in, m_in):
    cw = cw_ref[...]
    base = row0 + 8 - (CONV_WIDTH - 1)
    acc = cw[0:1, :] * buf_ref[base:base + CHUNK, :]
    for j in range(1, CONV_WIDTH):
        acc = acc + cw[j:j + 1, :] * buf_ref[base + j:base + j + CHUNK, :]
    qk = jax.nn.silu(acc)

    gates = gate_ref[...]
    lf = jax.nn.log_sigmoid(gates)
    row = lax.broadcasted_iota(jnp.int32, (CHUNK, CHUNK), 0)
    col = lax.broadcasted_iota(jnp.int32, (CHUNK, CHUNK), 1)
    causal = col <= row
    tri = causal.astype(BF16)
    l1, l2, l3 = _split3(lf)
    cum = _dot(tri, l1) + _dot(tri, l2) + _dot(tri, l3)
    gates_t = gates.T
    cum_t = cum.T
    cum_parts = _split3_masked(cum)

    lo = _lane_lo((CHUNK, LANES))
    ct_out = []
    m_out = []
    for p in range(MLSTM_HEADS // 2):
        sl = slice(p * LANES, (p + 1) * LANES)
        q_slab = qk[:, sl] * (HEAD_DIM ** -0.5)
        k_slab = qk[:, MLSTM_W + p * LANES:MLSTM_W + (p + 1) * LANES]
        kt_slab = k_slab.T
        v_slab = v_ref[:, sl]
        ct_pair = ct_in[p]
        ct_pair_b = ct_pair.astype(BF16)
        halves = []
        new_ct = []
        for half in range(2):
            h = 2 * p + half
            sel = lo if half == 0 else jnp.logical_not(lo)
            li_row = gates_t[h:h + 1, :]
            bc_row = cum_t[MLSTM_HEADS + h:MLSTM_HEADS + h + 1, :]
            pick = (row == MLSTM_HEADS + h).astype(BF16)
            bc_col = _dot(cum_parts[0], pick) + _dot(cum_parts[1], pick) + _dot(cum_parts[2], pick)
            b_tot = bc_row[:, CHUNK - 1:CHUNK]
            m_prev = m_in[h]

            dmat = jnp.where(causal, bc_col - bc_row + li_row, -jnp.inf)
            m_inter = bc_col + m_prev
            m_row = jnp.maximum(m_inter, jnp.max(dmat, axis=-1, keepdims=True))
            q_m = jnp.where(sel, q_slab, 0.0).astype(BF16)
            s = _dot(q_m, kt_slab.astype(BF16))
            pmat = (s * jnp.exp(dmat - m_row)).astype(BF16)
            w_inter = jnp.exp(m_inter - m_row)
            v_aug = jnp.where(sel, v_slab, 1.0).astype(BF16)
            halves.append((_dot(pmat, v_aug) + w_inter * _dot(q_m, ct_pair_b), jnp.exp(-m_row)))

            a_row = b_tot - bc_row + li_row
            m_loc = jnp.max(a_row, axis=-1, keepdims=True)
            w_row = jnp.exp(a_row - m_loc)
            kt_h = kt_slab[half * HEAD_DIM:(half + 1) * HEAD_DIM, :]
            ct_loc = _dot((kt_h * w_row).astype(BF16), v_aug)
            m_new = jnp.maximum(b_tot + m_prev, m_loc)
            s_old = jnp.exp(b_tot + m_prev - m_new)
            s_loc = jnp.exp(m_loc - m_new)
            new_ct.append(s_old * ct_pair[half * HEAD_DIM:(half + 1) * HEAD_DIM, :] + s_loc * ct_loc)
            m_out.append(m_new)

        ct_out.append(jnp.concatenate(new_ct, axis=0))
        (r_even, stab_even), (r_odd, stab_odd) = halves
        num = jnp.where(lo, r_even, r_odd)
        den = pltpu.roll(jnp.where(lo, r_odd, r_even), HEAD_DIM, 1)
        hh = num / jnp.maximum(jnp.abs(den), jnp.where(lo, stab_even, stab_odd))
        hn = _half_layer_norm(hh, lo) * g_ref[:, sl]
        out_ref[:, sl] = (hn * jax.nn.sigmoid(o_ref[:, sl])).astype(out_ref.dtype)
    return ct_out, m_out


def _mlstm(proj3, conv_w, norm_g):
    b, s, _ = proj3.shape
    rows = MLSTM_CHUNKS_PER_STEP * CHUNK
    assert s % rows == 0
    blk = lambda w, off: pl.BlockSpec((None, rows, w), lambda i, c: (i, c, off // w))
    return pl.pallas_call(
        _mlstm_kernel,
        grid=(b, s // rows),
        in_specs=[blk(2 * MLSTM_W, QK_OFF), blk(MLSTM_W, V_OFF), blk(MLSTM_W, O_OFF), blk(LANES, GATE_OFF),
                  pl.BlockSpec((CONV_WIDTH, 2 * MLSTM_W), lambda i, c: (0, 0)),
                  pl.BlockSpec((1, MLSTM_W), lambda i, c: (0, 0))],
        out_specs=pl.BlockSpec((None, rows, MLSTM_W), lambda i, c: (i, c, 0)),
        out_shape=jax.ShapeDtypeStruct((b, s, MLSTM_W), BF16),
        scratch_shapes=[pltpu.VMEM((8 + rows, 2 * MLSTM_W), F32),
                        pltpu.VMEM((MLSTM_HEADS // 2, LANES, LANES), F32),
                        pltpu.VMEM((8, LANES), F32)],
        compiler_params=_params("parallel", "arbitrary"),
        name="mlstm",
    )(proj3, proj3, proj3, proj3, conv_w, norm_g)


def _rope(x, cos_t, sin_t, first):
    return x * cos_t + jnp.where(first, pltpu.roll(x, LANES - ROPE_DIM // 2, 1),
                                 pltpu.roll(x, ROPE_DIM // 2, 1)) * sin_t


def _swa_kernel(sink_ref, q_ref, k_ref, v_ref, cos_ref, sin_ref, su_ref, sv_ref, ws_ref, bias_ref, sg_ref, sb_ref,
                out_ref, gated_ref, kt_ref, vv_ref):
    first_step = pl.program_id(1) == 0

    @pl.when(first_step)
    def _():
        kt_ref[...] = jnp.zeros_like(kt_ref)
        vv_ref[...] = jnp.zeros_like(vv_ref)

    kt_prev = kt_ref[...]
    v_prev = vv_ref[...]
    for i in range(SWA_BLOCKS_PER_STEP):
        rows = pl.ds(i * CHUNK, CHUNK)
        hide_previous = jnp.where(first_step, 2 * CHUNK, 0) if i == 0 else 0
        kt_prev, v_prev = _swa_block(hide_previous, sink_ref, q_ref.at[rows], k_ref.at[rows], v_ref.at[rows],
                                     cos_ref.at[rows], sin_ref.at[rows], out_ref.at[rows], kt_prev, v_prev)
        _sgu_chunk(slice(i * CHUNK, (i + 1) * CHUNK), su_ref, sv_ref, ws_ref, bias_ref, sg_ref, sb_ref, gated_ref)
    kt_ref[...] = kt_prev
    vv_ref[...] = v_prev


def _swa_block(first_block_shift, sink_ref, q_ref, k_ref, v_ref, cos_ref, sin_ref, out_ref, kt_prev, v_prev):
    cos_t = cos_ref[...]
    sin_t = sin_ref[...]
    lane = lax.broadcasted_iota(jnp.int32, (CHUNK, LANES), 1)
    lo = lane < HEAD_DIM
    first = (lane % HEAD_DIM) < ROPE_DIM // 2

    kt_cur = _rope(k_ref[...], cos_t, sin_t, first).T.astype(BF16)
    v_cur = v_ref[...].astype(BF16)
    kt = jnp.concatenate([kt_prev, kt_cur], axis=1)
    vv = jnp.concatenate([v_prev, v_cur], axis=0)

    row = lax.broadcasted_iota(jnp.int32, (CHUNK, 2 * CHUNK), 0)
    col = lax.broadcasted_iota(jnp.int32, (CHUNK, 2 * CHUNK), 1)
    visible = jnp.logical_or(jnp.logical_and(col < CHUNK, col > row + first_block_shift),
                             jnp.logical_and(col >= CHUNK, col - CHUNK <= row))

    for j in range(ATTN_Q_HEADS // 2):
        sl = slice(j * LANES, (j + 1) * LANES)
        q_slab = _rope(q_ref[:, sl], cos_t, sin_t, first) * (HEAD_DIM ** -0.5)
        outs = []
        for half in range(2):
            sel = lo if half == 0 else jnp.logical_not(lo)
            sink = sink_ref[ATTN_HEAD_ORDER[2 * j + half]]
            q_m = jnp.where(sel, q_slab, 0.0).astype(BF16)
            s = jnp.where(visible, _dot(q_m, kt), -jnp.inf)
            mx = jnp.maximum(jnp.max(s, axis=-1, keepdims=True), sink)
            pexp = jnp.exp(s - mx)
            denom = jnp.sum(pexp, axis=-1, keepdims=True) + jnp.exp(sink - mx)
            outs.append((_dot(pexp.astype(BF16), vv), denom))
        (num_lo, den_lo), (num_hi, den_hi) = outs
        out_ref[:, sl] = (jnp.where(lo, num_lo, num_hi) / jnp.where(lo, den_lo, den_hi)).astype(out_ref.dtype)

    return kt_cur, v_cur


def _swa_sgu(proj3, cos_t, sin_t, sinks, w_tril, bias_tok, norm_g, norm_b):
    b, s, _ = proj3.shape
    rows = SWA_BLOCKS_PER_STEP * CHUNK
    assert s % rows == 0
    blk = lambda w, off: pl.BlockSpec((None, rows, w), lambda i, c: (i, c, off // w))
    tab = pl.BlockSpec((None, rows, LANES), lambda i, c: (i, c, 0))
    const2 = lambda shape: pl.BlockSpec(shape, lambda i, c: (0,) * len(shape))
    out = lambda w: pl.BlockSpec((None, rows, w), lambda i, c: (i, c, 0))
    return pl.pallas_call(
        _swa_kernel,
        grid=(b, s // rows),
        in_specs=[pl.BlockSpec(memory_space=pltpu.SMEM),
                  blk(ATTN_W, AQ_OFF), blk(ATTN_KV_W, AK_OFF), blk(ATTN_KV_W, AV_OFF), tab, tab,
                  blk(SGU_W, SU_OFF), blk(SGU_W, SV_OFF), const2((SGU_GROUPS, CHUNK, CHUNK)),
                  const2((CHUNK, SGU_W)), const2((1, SGU_W)), const2((1, SGU_W))],
        out_specs=[out(ATTN_W), out(SGU_W)],
        out_shape=[jax.ShapeDtypeStruct((b, s, ATTN_W), BF16), jax.ShapeDtypeStruct((b, s, SGU_W), BF16)],
        scratch_shapes=[pltpu.VMEM((LANES, CHUNK), BF16), pltpu.VMEM((CHUNK, LANES), BF16)],
        compiler_params=_params("parallel", "arbitrary"),
        name="swa_sgu",
    )(sinks, proj3, proj3, proj3, cos_t, sin_t, proj3, proj3, w_tril, bias_tok, norm_g, norm_b)


def _sgu_chunk(rows, u_ref, v_ref, w_ref, bias_ref, g_ref, b_ref, out_ref):
    lo = _lane_lo((CHUNK, LANES))
    for j in range(SGU_GROUPS // 2):
        sl = slice(j * LANES, (j + 1) * LANES)
        u = jax.nn.gelu(u_ref[rows, sl])
        v = jax.nn.gelu(v_ref[rows, sl])
        vn = (_half_layer_norm(v, lo) * g_ref[:, sl] + b_ref[:, sl]).astype(BF16)
        mixed = jnp.where(lo, _dot(w_ref[2 * j], vn), _dot(w_ref[2 * j + 1], vn)) + bias_ref[:, sl]
        out_ref[rows, sl] = (u * mixed).astype(out_ref.dtype)


def _outproj_kernel(ha_ref, hb_ref, hc_ref, x_ref, w_ref, g_ref, b_ref, *rest):
    cat_ref = rest[-1]
    cat_ref[:, 0:MLSTM_W] = ha_ref[...]
    cat_ref[:, MLSTM_W:MLSTM_W + ATTN_W] = hb_ref[...]
    cat_ref[:, MLSTM_W + ATTN_W:] = hc_ref[...]
    mix = _dot(cat_ref[...], w_ref[...])
    y = _layer_norm_rows(DN_ALPHA * x_ref[...] + mix, g_ref[...], b_ref[...])
    rest = rest[:-1]
    if len(rest) == 1:
        rest[0][...] = y
        return
    w_split_ref, rb_ref, o_ref, route_ref, route_t_ref = rest
    o_ref[...] = y
    route_ref[...], route_t_ref[...] = _route_rows(y, w_split_ref[...], rb_ref[...])


def _outproj_ln(ha, hb, hc, x2d, w, g, b, router=None):
    n = x2d.shape[0]
    rows = lambda w: pl.BlockSpec((OUTPROJ_ROWS, w), lambda i: (i, 0))
    const = lambda shape: pl.BlockSpec(shape, lambda i: (0, 0))
    in_specs = [rows(MLSTM_W), rows(ATTN_W), rows(SGU_W), rows(D_MODEL),
                const((D_MODEL, D_MODEL)), const((1, D_MODEL)), const((1, D_MODEL))]
    out_specs = [rows(D_MODEL)]
    out_shape = [jax.ShapeDtypeStruct((n, D_MODEL), F32)]
    operands = (ha, hb, hc, x2d, w, g, b)
    if router is not None:
        in_specs += [const((D_MODEL, 2 * LANES)), const((1, LANES))]
        out_specs += [rows(LANES), pl.BlockSpec((N_EXPERTS, OUTPROJ_ROWS), lambda i: (0, i))]
        out_shape += [jax.ShapeDtypeStruct((n, LANES), F32), jax.ShapeDtypeStruct((N_EXPERTS, n), F32)]
        operands += tuple(router)
    out = pl.pallas_call(
        _outproj_kernel,
        grid=(n // OUTPROJ_ROWS,),
        in_specs=in_specs,
        out_specs=out_specs,
        out_shape=out_shape,
        scratch_shapes=[pltpu.VMEM((OUTPROJ_ROWS, D_MODEL), BF16)],
        compiler_params=_params("parallel"),
        name="outproj_ln" if router is None else "outproj_ln_route",
    )(*operands)
    return out[0] if router is None else tuple(out)


def _ffn_kernel(x_ref, wg_ref, wu_ref, wd_ref, g_ref, b_ref, o_ref, xb_ref, wgu_ref):
    j = pl.program_id(1)
    last = pl.num_programs(1) - 1
    n_sub = FFN_ROWS // FFN_SUB
    wgu_ref[:, :FF_TILE] = wg_ref[...]
    wgu_ref[:, FF_TILE:] = wu_ref[...]

    def sub_rows(m):
        return pl.ds(pl.multiple_of(m * FFN_SUB, FFN_SUB), FFN_SUB)

    def ffn_tile(m):
        gu = _dot(xb_ref[sub_rows(m), :], wgu_ref[...])
        hidden = (jax.nn.silu(gu[:, :FF_TILE]) * gu[:, FF_TILE:]).astype(BF16)
        return _dot(hidden, wd_ref[...])

    @pl.when(j == 0)
    def _():
        def body(m, carry):
            xb_ref[sub_rows(m), :] = x_ref[sub_rows(m), :].astype(BF16)
            o_ref[sub_rows(m), :] = ffn_tile(m)
            return carry

        lax.fori_loop(0, n_sub, body, 0)

    @pl.when(jnp.logical_and(j > 0, j < last))
    def _():
        def body(m, carry):
            o_ref[sub_rows(m), :] += ffn_tile(m)
            return carry

        lax.fori_loop(0, n_sub, body, 0)

    @pl.when(j == last)
    def _():
        def body(m, carry):
            rows = sub_rows(m)
            ff = o_ref[rows, :] + ffn_tile(m)
            o_ref[rows, :] = _layer_norm_rows(DN_ALPHA * x_ref[rows, :] + ff, g_ref[...], b_ref[...])
            return carry

        lax.fori_loop(0, n_sub, body, 0)


def _ffn_ln(x2d, wg, wu, wd, g, b):
    n = x2d.shape[0]
    d_ff = wd.shape[0]
    assert d_ff // FF_TILE >= 2
    tm = FFN_ROWS
    return pl.pallas_call(
        _ffn_kernel,
        grid=(n // tm, d_ff // FF_TILE),
        in_specs=[pl.BlockSpec((tm, D_MODEL), lambda i, j: (i, 0)),
                  pl.BlockSpec((D_MODEL, FF_TILE), lambda i, j: (0, j)),
                  pl.BlockSpec((D_MODEL, FF_TILE), lambda i, j: (0, j)),
                  pl.BlockSpec((FF_TILE, D_MODEL), lambda i, j: (j, 0)),
                  pl.BlockSpec((1, D_MODEL), lambda i, j: (0, 0)),
                  pl.BlockSpec((1, D_MODEL), lambda i, j: (0, 0))],
        out_specs=pl.BlockSpec((tm, D_MODEL), lambda i, j: (i, 0)),
        out_shape=jax.ShapeDtypeStruct((n, D_MODEL), F32),
        scratch_shapes=[pltpu.VMEM((tm, D_MODEL), BF16), pltpu.VMEM((D_MODEL, 2 * FF_TILE), BF16)],
        compiler_params=_params("parallel", "arbitrary"),
        name="ffn_ln",
    )(x2d, wg, wu, wd, g, b)


def _route_rows(x, w_split, bias):
    n = x.shape[0]
    xh = x.astype(BF16)
    xl = (x - xh.astype(F32)).astype(BF16)
    prod = _dot(jnp.concatenate([xh, xl], axis=0), w_split)
    logits = prod[:n, :LANES] + prod[:n, LANES:] + prod[n:, :LANES] + prod[n:, LANES:] + bias
    lt = logits.T[0:N_EXPERTS, :]
    expert = lax.broadcasted_iota(jnp.int32, lt.shape, 0)
    m1 = jnp.max(lt, axis=0, keepdims=True)
    i1 = jnp.min(jnp.where(lt == m1, expert, N_EXPERTS), axis=0, keepdims=True)
    rest = jnp.where(expert == i1, -jnp.inf, lt)
    m2 = jnp.max(rest, axis=0, keepdims=True)
    i2 = jnp.min(jnp.where(rest == m2, expert, N_EXPERTS), axis=0, keepdims=True)
    e2 = jnp.exp(m2 - m1)
    g1 = 1.0 / (1.0 + e2)
    g2 = e2 / (1.0 + e2)
    route_t = jnp.where(expert == 0, g1, jnp.where(expert == 1, g2, jnp.where(
        expert == 2, i1.astype(F32), jnp.where(expert == 3, i2.astype(F32), 0.0))))
    padded = jnp.concatenate([route_t, jnp.zeros((LANES - N_EXPERTS, n), F32)], axis=0)
    return padded.T, route_t


TOKEN_TILE = (SUBLANES, D_MODEL // SUBLANES)


def _dispatch_kernel(slot0_ref, slot1_ref, ends_ref, x_ref, o_hbm, stage_ref, zero_ref, sem, zero_sem):
    t = pl.program_id(0)
    last = pl.num_programs(0) - 1
    s = t % 2

    def wait_tile(ss):
        for _ in range(TOP_K):
            pltpu.make_async_copy(zero_ref, o_hbm.at[pl.ds(0, MOE_SUB)], sem.at[ss]).wait()

    @pl.when(t == 0)
    def _():
        zero_ref[...] = jnp.zeros_like(zero_ref)

        def zero_sub_tile(m, carry):
            dst = o_hbm.at[pl.ds(pl.multiple_of(m * MOE_SUB, MOE_SUB), MOE_SUB)]
            cp = pltpu.make_async_copy(zero_ref, dst, zero_sem)
            cp.start()
            cp.wait()
            return carry

        for e in range(N_EXPERTS):
            @pl.when(ends_ref[e] >= MOE_SUB)
            def _():
                zero_sub_tile(ends_ref[e] // MOE_SUB - 1, 0)

        lax.fori_loop(ends_ref[N_EXPERTS - 1] // MOE_SUB, o_hbm.shape[0] // MOE_SUB, zero_sub_tile, 0)

    @pl.when(t >= 2)
    def _():
        wait_tile(s)

    stage_ref[s] = x_ref[...].reshape((MOE_SUB,) + TOKEN_TILE)

    def issue(i, carry):
        for u in range(SUBLANES):
            r = i * SUBLANES + u
            for k, slot_ref in enumerate((slot0_ref, slot1_ref)):
                pltpu.make_async_copy(stage_ref.at[s, r], o_hbm.at[slot_ref[t * MOE_SUB + r]],
                                      sem.at[s]).start(priority=k)
        return carry

    lax.fori_loop(0, MOE_SUB // SUBLANES, issue, 0)

    @pl.when(t == last)
    def _():
        wait_tile(s)
        wait_tile(1 - s)


def _dispatch_rows(x2d, slots, ends, n_rows):
    n = x2d.shape[0]
    assert n // MOE_SUB >= 2
    return pl.pallas_call(
        _dispatch_kernel,
        grid_spec=pltpu.PrefetchScalarGridSpec(
            num_scalar_prefetch=3,
            grid=(n // MOE_SUB,),
            in_specs=[pl.BlockSpec((MOE_SUB, D_MODEL), lambda t, s0, s1, en: (t, 0))],
            out_specs=pl.BlockSpec(memory_space=pl.ANY),
            scratch_shapes=[pltpu.VMEM((2, MOE_SUB) + TOKEN_TILE, F32), pltpu.VMEM((MOE_SUB,) + TOKEN_TILE, F32),
                            pltpu.SemaphoreType.DMA((2,)), pltpu.SemaphoreType.DMA(())]),
        out_shape=jax.ShapeDtypeStruct((n_rows,) + TOKEN_TILE, F32),
        compiler_params=_params("arbitrary"),
        name="moe_dispatch",
    )(slots[0], slots[1], ends, x2d)


def _moe_kernel(exp_ref, row0_ref, nsub_ref, tail_ref, x_hbm, wg_ref, wu_ref, wd_ref, y_hbm,
                xb_ref, acc_ref, wgu_ref, wdb_ref, stage_ref, in_sem, out_sem):
    v = pl.program_id(0)
    j = pl.program_id(1)
    last = pl.num_programs(1) - 1
    n_sub = nsub_ref[v]
    row0 = row0_ref[v]

    def sub_rows(m):
        return pl.ds(pl.multiple_of(m * MOE_SUB, MOE_SUB), MOE_SUB)

    def hbm_rows(m):
        return pl.ds(pl.multiple_of(row0 + m * MOE_SUB, MOE_SUB), MOE_SUB)

    @pl.when(jnp.logical_and(v == 0, j == 0))
    def _():
        stage_ref[0] = jnp.zeros((MOE_SUB,) + TOKEN_TILE, F32)

        def zero_sub_tile(m, carry):
            cp = pltpu.make_async_copy(stage_ref.at[0], y_hbm.at[sub_rows(m)], out_sem.at[0])
            cp.start()
            cp.wait()
            return carry

        lax.fori_loop(tail_ref[0], y_hbm.shape[0] // MOE_SUB, zero_sub_tile, 0)

    @pl.when(n_sub > 0)
    def _():
        wgu_ref[:, :MOE_FF_TILE] = wg_ref[...].astype(BF16)
        wgu_ref[:, MOE_FF_TILE:] = wu_ref[...].astype(BF16)
        wdb_ref[...] = wd_ref[...].astype(BF16)

        def ffn_rows(rows):
            gu = _dot(xb_ref[rows, :], wgu_ref[...])
            hidden = (jax.nn.silu(gu[:, :MOE_FF_TILE]) * gu[:, MOE_FF_TILE:]).astype(BF16)
            return _dot(hidden, wdb_ref[...])

        def ffn_tile(m):
            return ffn_rows(sub_rows(m))

        def accumulate(after_sub_tile):
            def pair(k, carry):
                rows = pl.ds(pl.multiple_of(2 * k * MOE_SUB, 2 * MOE_SUB), 2 * MOE_SUB)
                acc_ref[rows, :] += ffn_rows(rows)
                after_sub_tile(2 * k)
                after_sub_tile(2 * k + 1)
                return carry

            lax.fori_loop(0, n_sub // 2, pair, 0)

            @pl.when(n_sub % 2 == 1)
            def _():
                acc_ref[sub_rows(n_sub - 1), :] += ffn_tile(n_sub - 1)
                after_sub_tile(n_sub - 1)

        def y_copy(m):
            return pltpu.make_async_copy(stage_ref.at[m % 2], y_hbm.at[hbm_rows(m)], out_sem.at[m % 2])

        def send_sub_tile(m):
            @pl.when(m >= 2)
            def _():
                y_copy(m - 2).wait()

            stage_ref[m % 2] = acc_ref[sub_rows(m), :].reshape((MOE_SUB,) + TOKEN_TILE)
            y_copy(m).start()

        @pl.when(j == 0)
        def _():
            def x_copy(m):
                return pltpu.make_async_copy(x_hbm.at[hbm_rows(m)], stage_ref.at[m % 2], in_sem.at[m % 2])

            x_copy(0).start()

            def body(m, carry):
                @pl.when(m + 1 < n_sub)
                def _():
                    x_copy(m + 1).start()

                x_copy(m).wait()
                xb_ref[sub_rows(m), :] = stage_ref[m % 2].reshape(MOE_SUB, D_MODEL).astype(BF16)
                acc_ref[sub_rows(m), :] = ffn_tile(m)
                return carry

            lax.fori_loop(0, n_sub, body, 0)

        @pl.when(jnp.logical_and(j > 0, j < last))
        def _():
            accumulate(lambda m: None)

        @pl.when(j == last)
        def _():
            accumulate(send_sub_tile)

            @pl.when(n_sub >= 2)
            def _():
                y_copy(n_sub - 2).wait()

            y_copy(n_sub - 1).wait()


def _moe_grouped(xs, wg, wu, wd, visit_exp, visit_row0, visit_nsub, tail_sub):
    n_rows = xs.shape[0]
    n_visits = visit_exp.shape[0]
    d_ff = wg.shape[2]
    assert d_ff // MOE_FF_TILE >= 2
    rows = MOE_VISIT_SUBS * MOE_SUB
    w_in = lambda v, j, e, r, ns, tl: (e[v], 0, j)
    w_out = lambda v, j, e, r, ns, tl: (e[v], j, 0)
    return pl.pallas_call(
        _moe_kernel,
        grid_spec=pltpu.PrefetchScalarGridSpec(
            num_scalar_prefetch=4,
            grid=(n_visits, d_ff // MOE_FF_TILE),
            in_specs=[pl.BlockSpec(memory_space=pl.ANY),
                      pl.BlockSpec((None, D_MODEL, MOE_FF_TILE), w_in),
                      pl.BlockSpec((None, D_MODEL, MOE_FF_TILE), w_in),
                      pl.BlockSpec((None, MOE_FF_TILE, D_MODEL), w_out)],
            out_specs=pl.BlockSpec(memory_space=pl.ANY),
            scratch_shapes=[pltpu.VMEM((rows, D_MODEL), BF16), pltpu.VMEM((rows, D_MODEL), F32),
                            pltpu.VMEM((D_MODEL, 2 * MOE_FF_TILE), BF16),
                            pltpu.VMEM((MOE_FF_TILE, D_MODEL), BF16),
                            pltpu.VMEM((2, MOE_SUB) + TOKEN_TILE, F32),
                            pltpu.SemaphoreType.DMA((2,)), pltpu.SemaphoreType.DMA((2,))]),
        out_shape=jax.ShapeDtypeStruct((n_rows,) + TOKEN_TILE, F32),
        compiler_params=_params("arbitrary", "arbitrary"),
        name="moe_grouped",
    )(visit_exp, visit_row0, visit_nsub, tail_sub, xs, wg, wu, wd)


def _combine_kernel(slot0_ref, slot1_ref, x_ref, gate_ref, y_hbm, g_ref, b_ref, o_ref, buf_ref, sem):
    t = pl.program_id(0)
    s = t % 2

    def issue_tile(tt, ss):
        def issue(i, carry):
            for u in range(SUBLANES):
                r = i * SUBLANES + u
                for k, slot_ref in enumerate((slot0_ref, slot1_ref)):
                    pltpu.make_async_copy(y_hbm.at[slot_ref[tt * MOE_SUB + r]], buf_ref.at[ss, k, r],
                                          sem.at[ss]).start(priority=k)
            return carry

        lax.fori_loop(0, MOE_SUB // SUBLANES, issue, 0)

    @pl.when(t == 0)
    def _():
        issue_tile(0, 0)

    @pl.when(t + 1 < pl.num_programs(0))
    def _():
        issue_tile(t + 1, 1 - s)

    for k in range(TOP_K):
        pltpu.make_async_copy(y_hbm.at[pl.ds(0, MOE_SUB)], buf_ref.at[s, k], sem.at[s]).wait()
    gate = gate_ref[...]
    y0 = buf_ref[s, 0].reshape(MOE_SUB, D_MODEL)
    y1 = buf_ref[s, 1].reshape(MOE_SUB, D_MODEL)
    ff = gate[:, 0:1] * y0 + gate[:, 1:2] * y1
    o_ref[...] = _layer_norm_rows(DN_ALPHA * x_ref[...] + ff, g_ref[...], b_ref[...])


def _combine_ln(x2d, route, slots, ys, g, b):
    n = x2d.shape[0]
    return pl.pallas_call(
        _combine_kernel,
        grid_spec=pltpu.PrefetchScalarGridSpec(
            num_scalar_prefetch=2,
            grid=(n // MOE_SUB,),
            in_specs=[pl.BlockSpec((MOE_SUB, D_MODEL), lambda t, s0, s1: (t, 0)),
                      pl.BlockSpec((MOE_SUB, LANES), lambda t, s0, s1: (t, 0)),
                      pl.BlockSpec(memory_space=pl.ANY),
                      pl.BlockSpec((1, D_MODEL), lambda t, s0, s1: (0, 0)),
                      pl.BlockSpec((1, D_MODEL), lambda t, s0, s1: (0, 0))],
            out_specs=pl.BlockSpec((MOE_SUB, D_MODEL), lambda t, s0, s1: (t, 0)),
            scratch_shapes=[pltpu.VMEM((2, TOP_K, MOE_SUB) + TOKEN_TILE, F32), pltpu.SemaphoreType.DMA((2,))]),
        out_shape=jax.ShapeDtypeStruct((n, D_MODEL), F32),
        compiler_params=_params("arbitrary"),
        name="moe_combine_ln",
    )(slots[0], slots[1], x2d, route, ys, g, b)


def _routing_tables(route_t, n):
    idx = route_t[2:4].astype(jnp.int32)
    expert = jnp.arange(N_EXPERTS, dtype=jnp.int32)[:, None]
    chosen = [idx[k][None, :] == expert for k in range(TOP_K)]
    onehot = jnp.logical_or(chosen[0], chosen[1]).astype(jnp.int32)
    rank = jnp.cumsum(onehot, axis=1) - onehot
    counts = jnp.sum(onehot, axis=1)
    padded = ((counts + MOE_SUB - 1) // MOE_SUB) * MOE_SUB
    ends = jnp.cumsum(padded)
    starts = ends - padded
    place = starts[:, None] + rank
    slot = [jnp.sum(jnp.where(chosen[k], place, 0), axis=0).astype(jnp.int32) for k in range(TOP_K)]

    n_rows = -(-(n * TOP_K + N_EXPERTS * (MOE_SUB - 1)) // MOE_SUB) * MOE_SUB

    visit_rows = MOE_VISIT_SUBS * MOE_SUB
    max_chunks = -(-n_rows // visit_rows)
    chunk = jnp.arange(max_chunks, dtype=jnp.int32)[None, :]
    left = padded[:, None] - chunk * visit_rows
    valid = (left > 0).reshape(-1)
    n_visits = n_rows // visit_rows + N_EXPERTS
    order = jnp.argsort(jnp.logical_not(valid), stable=True)[:n_visits]
    n_valid = jnp.sum(valid.astype(jnp.int32))
    live = jnp.arange(n_visits) < n_valid
    order = jnp.where(live, order, order[jnp.maximum(n_valid - 1, 0)])
    v_exp = (order // max_chunks).astype(jnp.int32)
    v_chunk = (order % max_chunks).astype(jnp.int32)
    v_row0 = jnp.where(live, starts[v_exp] + v_chunk * visit_rows, 0).astype(jnp.int32)
    v_nsub = jnp.where(live, jnp.minimum(left.reshape(-1)[order], visit_rows) // MOE_SUB, 0).astype(jnp.int32)
    tail_sub = (ends[N_EXPERTS - 1:] // MOE_SUB).astype(jnp.int32)
    return slot, ends.astype(jnp.int32), n_rows, (v_exp, v_row0, v_nsub, tail_sub)


def _router_operands(w_router, b_router):
    w_pad = jnp.zeros((D_MODEL, LANES), F32).at[:, :N_EXPERTS].set(w_router)
    w_hi = w_pad.astype(BF16)
    w_lo = (w_pad - w_hi.astype(F32)).astype(BF16)
    b_pad = jnp.zeros((1, LANES), F32).at[0, :N_EXPERTS].set(b_router)
    return jnp.concatenate([w_hi, w_lo], axis=1), b_pad


def _moe_ln(x2d, route, route_t, wg, wu, wd, g, b):
    n = x2d.shape[0]
    slots, ends, n_rows, visits = _routing_tables(route_t, n)
    xs = _dispatch_rows(x2d, slots, ends, n_rows)
    ys = _moe_grouped(xs, wg, wu, wd, *visits)
    return _combine_ln(x2d, route, slots, ys, g, b)


_SRC_GATES = 4 * MLSTM_W
_SRC_AQ = _SRC_GATES + 2 * MLSTM_HEADS
_SRC_AK = _SRC_AQ + ATTN_W
_IN_PROJ_MOVES = (
    ((0, 0, 4 * MLSTM_W),)
    + tuple((_SRC_AQ + h * HEAD_DIM, AQ_OFF + i * HEAD_DIM, HEAD_DIM) for i, h in enumerate(ATTN_HEAD_ORDER))
    + ((_SRC_AK, AK_OFF, ATTN_KV_W), (_SRC_AK + ATTN_KV_W, AV_OFF, ATTN_KV_W),
       (_SRC_AK + 2 * ATTN_KV_W, SU_OFF, SGU_W), (_SRC_AK + 2 * ATTN_KV_W + SGU_W, SV_OFF, SGU_W)))
D_PROJ = _SRC_AK + 2 * ATTN_KV_W + 2 * SGU_W


def _relayout_columns(src, dst_dtype):
    source = np.full((D_PROJ_PAD,), D_PROJ, np.int32)
    for s, d, w in _IN_PROJ_MOVES + ((_SRC_GATES, GATE_OFF, 2 * MLSTM_HEADS),):
        source[d:d + w] = np.arange(s, s + w)
    padded = jnp.concatenate([src, jnp.zeros(src.shape[:-1] + (1,), src.dtype)], axis=-1)
    return jnp.take(padded, source, axis=-1).astype(dst_dtype)


def _w_layout_kernel(w_ref, o_ref):
    for s, d, w in _IN_PROJ_MOVES:
        o_ref[:, d:d + w] = w_ref[:, s:s + w].astype(o_ref.dtype)
    pad = jnp.zeros((w_ref.shape[0], LANES - 2 * MLSTM_HEADS), F32)
    gates = jnp.concatenate([w_ref[:, _SRC_GATES:_SRC_GATES + 2 * MLSTM_HEADS], pad], axis=1)
    o_ref[:, GATE_OFF:GATE_OFF + LANES] = gates.astype(o_ref.dtype)


def _layout_in_proj(w_in, b_in):
    rows = D_MODEL // 4

    def one_layer(layer):
        return pl.pallas_call(
            _w_layout_kernel,
            grid=(D_MODEL // rows,),
            in_specs=[pl.BlockSpec((None, rows, D_PROJ), lambda i: (layer, i, 0))],
            out_specs=pl.BlockSpec((rows, D_PROJ_PAD), lambda i: (i, 0)),
            out_shape=jax.ShapeDtypeStruct((D_MODEL, D_PROJ_PAD), BF16),
            compiler_params=_params("parallel"),
            name="w_in_layout",
        )(w_in)

    return [one_layer(layer) for layer in range(w_in.shape[0])], _relayout_columns(b_in, F32)[:, None, :]


def _rope_tables(positions):
    inv_freq = ROPE_THETA ** (-jnp.arange(0, ROPE_DIM, 2, dtype=F32) / ROPE_DIM)
    ang = inv_freq[None, :, None] * positions.astype(F32)[:, None, :]
    cos, sin = jnp.cos(ang), jnp.sin(ang)
    ones = jnp.ones((ang.shape[0], HEAD_DIM - ROPE_DIM, ang.shape[2]), F32)
    cos_head = jnp.concatenate([cos, cos, ones], 1)
    sin_head = jnp.concatenate([-sin, sin, 0.0 * ones], 1)
    cos_t = jnp.concatenate([cos_head, cos_head], 1)
    sin_t = jnp.concatenate([sin_head, sin_head], 1)
    return _to_token_major(cos_t, sin_t)


def _table_transpose_kernel(cos_ref, sin_ref, cos_out, sin_out):
    cos_out[...] = cos_ref[...].T
    sin_out[...] = sin_ref[...].T


def _to_token_major(cos_t, sin_t):
    b, lanes, s = cos_t.shape
    rows = min(s, FFN_ROWS)
    src = pl.BlockSpec((None, lanes, rows), lambda i, c: (i, 0, c))
    dst = pl.BlockSpec((None, rows, lanes), lambda i, c: (i, c, 0))
    shape = jax.ShapeDtypeStruct((b, s, lanes), F32)
    return pl.pallas_call(
        _table_transpose_kernel,
        grid=(b, s // rows),
        in_specs=[src, src],
        out_specs=[dst, dst],
        out_shape=[shape, shape],
        compiler_params=_params("parallel", "parallel"),
        name="rope_tables",
    )(cos_t, sin_t)


def kernel(x, positions, w_in, b_in, conv_w, mlstm_norm_g, attn_sinks, sgu_w_s, sgu_b_s, sgu_norm_g, sgu_norm_b, w_out, ln1_g, ln1_b, ln2_g, ln2_b, ffn_w_gate, ffn_w_up, ffn_w_down, moe_w_router, moe_b_router, moe_w_gate, moe_w_up, moe_w_down):
    bsz, seq, _ = x.shape
    n = bsz * seq
    cos_t, sin_t = _rope_tables(positions)
    tril = jnp.tril(jnp.ones((CHUNK, CHUNK), bool))
    x2d = x.reshape(n, D_MODEL)
    w_p, b_p = _layout_in_proj(w_in, b_in)
    for layer in range(DEPTH):
        proj3 = _inproj(x2d, w_p[layer], b_p[layer]).reshape(bsz, seq, D_PROJ_PAD)
        h_a = _mlstm(proj3, conv_w[layer], mlstm_norm_g[layer][None, :])
        w_tril = jnp.where(tril, sgu_w_s[layer], 0.0).astype(BF16)
        bias_tok = jnp.repeat(sgu_b_s[layer].T, HEAD_DIM, axis=1)
        h_b, h_c = _swa_sgu(proj3, cos_t, sin_t, attn_sinks[layer], w_tril, bias_tok,
                            sgu_norm_g[layer][None, :], sgu_norm_b[layer][None, :])
        wo = w_out[layer]
        wb = wo[MLSTM_W:MLSTM_W + ATTN_W].reshape(ATTN_Q_HEADS, HEAD_DIM, D_MODEL)[np.array(ATTN_HEAD_ORDER)]
        wo = jnp.concatenate([wo[:MLSTM_W], wb.reshape(ATTN_W, D_MODEL), wo[MLSTM_W + ATTN_W:]], 0).astype(BF16)
        j = layer // 2
        dense = layer % 2 == 0
        mixed = _outproj_ln(h_a.reshape(n, MLSTM_W), h_b.reshape(n, ATTN_W), h_c.reshape(n, SGU_W), x2d,
                            wo, ln1_g[layer][None, :], ln1_b[layer][None, :],
                            router=None if dense else _router_operands(moe_w_router[j], moe_b_router[j]))
        g2, b2 = ln2_g[layer][None, :], ln2_b[layer][None, :]
        if dense:
            x2d = _ffn_ln(mixed, ffn_w_gate[j].astype(BF16), ffn_w_up[j].astype(BF16),
                          ffn_w_down[j].astype(BF16), g2, b2)
        else:
            x2d, route, route_t = mixed
            x2d = _moe_ln(x2d, route, route_t, moe_w_gate[j], moe_w_up[j], moe_w_down[j], g2, b2)
    return x2d.reshape(bsz, seq, D_MODEL)
```

```python
import jax
import jax.numpy as jnp
import numpy as np
from jax import lax
from jax.experimental import pallas as pl
from jax.experimental.pallas import tpu as pltpu

F32 = jnp.float32
BF16 = jnp.bfloat16

D_MODEL = 1024
HEAD_DIM = 64
LANES = 128
SUBLANES = 8
MLSTM_HEADS = 6
ATTN_Q_HEADS = 6
ATTN_KV_HEADS = 2
SGU_GROUPS = 4
MLSTM_W = MLSTM_HEADS * HEAD_DIM
ATTN_W = ATTN_Q_HEADS * HEAD_DIM
ATTN_KV_W = ATTN_KV_HEADS * HEAD_DIM
SGU_W = SGU_GROUPS * HEAD_DIM
CHUNK = 128
CONV_WIDTH = 4
ROPE_DIM = HEAD_DIM // 4
ROPE_THETA = 500000.0
N_EXPERTS = 8
TOP_K = 2
DEPTH = 2
DN_ALPHA = (2.0 * DEPTH) ** 0.25
LN_EPS = 1e-5

QK_OFF, V_OFF, O_OFF, AQ_OFF = 0, 768, 1152, 1536
GATE_OFF, AK_OFF, AV_OFF, SU_OFF, SV_OFF = 1920, 2048, 2176, 2304, 2560
D_PROJ_PAD = 2816
ATTN_HEAD_ORDER = (0, 3, 1, 4, 2, 5)

VMEM_LIMIT = 56 * 1024 * 1024

MLSTM_CHUNKS_PER_STEP = 8
SWA_BLOCKS_PER_STEP = 2
ROW_TILE = 1024
OUTPROJ_ROWS = 1024
FFN_ROWS = 2048
FFN_SUB = 2048
FF_TILE = 256
MOE_SUB = 512
MOE_VISIT_SUBS = 9
MOE_FF_TILE = 512


def _params(*sem):
    return pltpu.CompilerParams(dimension_semantics=sem, vmem_limit_bytes=VMEM_LIMIT)


def _lane_lo(shape):
    return lax.broadcasted_iota(jnp.int32, shape, len(shape) - 1) < HEAD_DIM


def _layer_norm_rows(z, g, b):
    mu = jnp.mean(z, axis=-1, keepdims=True)
    zc = z - mu
    var = jnp.mean(zc * zc, axis=-1, keepdims=True)
    return zc * lax.rsqrt(var + LN_EPS) * g + b


def _half_layer_norm(x, lo):
    inv = 1.0 / HEAD_DIM
    s_lo = jnp.sum(jnp.where(lo, x, 0.0), axis=-1, keepdims=True)
    s_all = jnp.sum(x, axis=-1, keepdims=True)
    mu = jnp.where(lo, s_lo, s_all - s_lo) * inv
    xc = x - mu
    sq = xc * xc
    q_lo = jnp.sum(jnp.where(lo, sq, 0.0), axis=-1, keepdims=True)
    q_all = jnp.sum(sq, axis=-1, keepdims=True)
    var = jnp.where(lo, q_lo, q_all - q_lo) * inv
    return xc * lax.rsqrt(var + LN_EPS)


def _split3(a):
    h1 = a.astype(BF16)
    r1 = a - h1.astype(F32)
    h2 = r1.astype(BF16)
    r2 = r1 - h2.astype(F32)
    return h1, h2, r2.astype(BF16)


def _split3_masked(a):
    def top(x):
        bits = lax.bitcast_convert_type(x, jnp.int32) & jnp.int32(-65536)
        return lax.bitcast_convert_type(bits, F32)

    h1 = top(a)
    r1 = a - h1
    h2 = top(r1)
    return h1.astype(BF16), h2.astype(BF16), (r1 - h2).astype(BF16)


def _dot(a, b):
    return jnp.dot(a, b, preferred_element_type=F32)


def _inproj_kernel(x_ref, w_ref, b_ref, o_ref):
    o_ref[...] = _dot(x_ref[...].astype(BF16), w_ref[...]) + b_ref[...]


def _inproj(x2d, w, b):
    n = x2d.shape[0]
    return pl.pallas_call(
        _inproj_kernel,
        grid=(n // ROW_TILE,),
        in_specs=[pl.BlockSpec((ROW_TILE, D_MODEL), lambda i: (i, 0)),
                  pl.BlockSpec((D_MODEL, D_PROJ_PAD), lambda i: (0, 0)),
                  pl.BlockSpec((1, D_PROJ_PAD), lambda i: (0, 0))],
        out_specs=pl.BlockSpec((ROW_TILE, D_PROJ_PAD), lambda i: (i, 0)),
        out_shape=jax.ShapeDtypeStruct((n, D_PROJ_PAD), F32),
        compiler_params=_params("parallel"),
        name="inproj",
    )(x2d, w, b)


def _mlstm_kernel(qk_ref, v_ref, o_ref, gate_ref, cw_ref, g_ref, out_ref, buf_ref, ct_ref, m_ref):
    rows = MLSTM_CHUNKS_PER_STEP * CHUNK

    @pl.when(pl.program_id(1) == 0)
    def _():
        buf_ref[0:8, :] = jnp.zeros((8, 2 * MLSTM_W), F32)
        ct_ref[...] = jnp.zeros_like(ct_ref)
        m_ref[...] = jnp.zeros_like(m_ref)

    buf_ref[8:8 + rows, :] = qk_ref[...]
    ct = [ct_ref[p] for p in range(MLSTM_HEADS // 2)]
    m = [m_ref[h:h + 1, 0:1] for h in range(MLSTM_HEADS)]
    for i in range(MLSTM_CHUNKS_PER_STEP):
        blk = pl.ds(i * CHUNK, CHUNK)
        ct, m = _mlstm_chunk(i * CHUNK, buf_ref, v_ref.at[blk], o_ref.at[blk], gate_ref.at[blk], cw_ref, g_ref,
                             out_ref.at[blk], ct, m)
    buf_ref[0:8, :] = qk_ref[rows - 8:rows, :]
    for p in range(MLSTM_HEADS // 2):
        ct_ref[p] = ct[p]
    for h in range(MLSTM_HEADS):
        m_ref[h:h + 1, :] = jnp.broadcast_to(m[h], (1, LANES))


def _mlstm_chunk(row0, buf_ref, v_ref, o_ref, gate_ref, cw_ref, g_ref, out_ref, ct_in, m_in):
    cw = cw_ref[...]
    base = row0 + 8 - (CONV_WIDTH - 1)
    acc = cw[0:1, :] * buf_ref[base:base + CHUNK, :]
    for j in range(1, CONV_WIDTH):
        acc = acc + cw[j:j + 1, :] * buf_ref[base + j:base + j + CHUNK, :]
    qk = jax.nn.silu(acc)

    gates = gate_ref[...]
    lf = jax.nn.log_sigmoid(gates)
    row = lax.broadcasted_iota(jnp.int32, (CHUNK, CHUNK), 0)
    col = lax.broadcasted_iota(jnp.int32, (CHUNK, CHUNK), 1)
    causal = col <= row
    tri = causal.astype(BF16)
    l1, l2, l3 = _split3(lf)
    cum = _dot(tri, l1) + _dot(tri, l2) + _dot(tri, l3)
    gates_t = gates.T
    cum_t = cum.T
    cum_parts = _split3_masked(cum)

    lo = _lane_lo((CHUNK, LANES))
    ct_out = []
    m_out = []
    for p in range(MLSTM_HEADS // 2):
        sl = slice(p * LANES, (p + 1) * LANES)
        q_slab = qk[:, sl] * (HEAD_DIM ** -0.5)
        k_slab = qk[:, MLSTM_W + p * LANES:MLSTM_W + (p + 1) * LANES]
        kt_slab = k_slab.T
        v_slab = v_ref[:, sl]
        ct_pair = ct_in[p]
        ct_pair_b = ct_pair.astype(BF16)
        halves = []
        new_ct = []
        for half in range(2):
            h = 2 * p + half
            sel = lo if half == 0 else jnp.logical_not(lo)
            li_row = gates_t[h:h + 1, :]
            bc_row = cum_t[MLSTM_HEADS + h:MLSTM_HEADS + h + 1, :]
            pick = (row == MLSTM_HEADS + h).astype(BF16)
            bc_col = _dot(cum_parts[0], pick) + _dot(cum_parts[1], pick) + _dot(cum_parts[2], pick)
            b_tot = bc_row[:, CHUNK - 1:CHUNK]
            m_prev = m_in[h]

            dmat = jnp.where(causal, bc_col - bc_row + li_row, -jnp.inf)
            m_inter = bc_col + m_prev
            m_row = jnp.maximum(m_inter, jnp.max(dmat, axis=-1, keepdims=True))
            q_m = jnp.where(sel, q_slab, 0.0).astype(BF16)
            s = _dot(q_m, kt_slab.astype(BF16))
            pmat = (s * jnp.exp(dmat - m_row)).astype(BF16)
            w_inter = jnp.exp(m_inter - m_row)
            v_aug = jnp.where(sel, v_slab, 1.0).astype(BF16)
            halves.append((_dot(pmat, v_aug) + w_inter * _dot(q_m, ct_pair_b), jnp.exp(-m_row)))

            a_row = b_tot - bc_row + li_row
            m_loc = jnp.max(a_row, axis=-1, keepdims=True)
            w_row = jnp.exp(a_row - m_loc)
            kt_h = kt_slab[half * HEAD_DIM:(half + 1) * HEAD_DIM, :]
            ct_loc = _dot((kt_h * w_row).astype(BF16), v_aug)
            m_new = jnp.maximum(b_tot + m_prev, m_loc)
            s_old = jnp.exp(b_tot + m_prev - m_new)
            s_loc = jnp.exp(m_loc - m_new)
            new_ct.append(s_old * ct_pair[half * HEAD_DIM:(half + 1) * HEAD_DIM, :] + s_loc * ct_loc)
            m_out.append(m_new)

        ct_out.append(jnp.concatenate(new_ct, axis=0))
        (r_even, stab_even), (r_odd, stab_odd) = halves
        num = jnp.where(lo, r_even, r_odd)
        den = pltpu.roll(jnp.where(lo, r_odd, r_even), HEAD_DIM, 1)
        hh = num / jnp.maximum(jnp.abs(den), jnp.where(lo, stab_even, stab_odd))
        hn = _half_layer_norm(hh, lo) * g_ref[:, sl]
        out_ref[:, sl] = (hn * jax.nn.sigmoid(o_ref[:, sl])).astype(out_ref.dtype)
    return ct_out, m_out


def _mlstm(proj3, conv_w, norm_g):
    b, s, _ = proj3.shape
    rows = MLSTM_CHUNKS_PER_STEP * CHUNK
    assert s % rows == 0
    blk = lambda w, off: pl.BlockSpec((None, rows, w), lambda i, c: (i, c, off // w))
    return pl.pallas_call(
        _mlstm_kernel,
        grid=(b, s // rows),
        in_specs=[blk(2 * MLSTM_W, QK_OFF), blk(MLSTM_W, V_OFF), blk(MLSTM_W, O_OFF), blk(LANES, GATE_OFF),
                  pl.BlockSpec((CONV_WIDTH, 2 * MLSTM_W), lambda i, c: (0, 0)),
                  pl.BlockSpec((1, MLSTM_W), lambda i, c: (0, 0))],
        out_specs=pl.BlockSpec((None, rows, MLSTM_W), lambda i, c: (i, c, 0)),
        out_shape=jax.ShapeDtypeStruct((b, s, MLSTM_W), BF16),
        scratch_shapes=[pltpu.VMEM((8 + rows, 2 * MLSTM_W), F32),
                        pltpu.VMEM((MLSTM_HEADS // 2, LANES, LANES), F32),
                        pltpu.VMEM((8, LANES), F32)],
        compiler_params=_params("parallel", "arbitrary"),
        name="mlstm",
    )(proj3, proj3, proj3, proj3, conv_w, norm_g)


def _rope(x, cos_t, sin_t, first):
    return x * cos_t + jnp.where(first, pltpu.roll(x, LANES - ROPE_DIM // 2, 1),
                                 pltpu.roll(x, ROPE_DIM // 2, 1)) * sin_t


def _swa_kernel(sink_ref, q_ref, k_ref, v_ref, cos_ref, sin_ref, su_ref, sv_ref, ws_ref, bias_ref, sg_ref, sb_ref,
                out_ref, gated_ref, kt_ref, vv_ref):
    first_step = pl.program_id(1) == 0

    @pl.when(first_step)
    def _():
        kt_ref[...] = jnp.zeros_like(kt_ref)
        vv_ref[...] = jnp.zeros_like(vv_ref)

    kt_prev = kt_ref[...]
    v_prev = vv_ref[...]
    for i in range(SWA_BLOCKS_PER_STEP):
        rows = pl.ds(i * CHUNK, CHUNK)
        hide_previous = jnp.where(first_step, 2 * CHUNK, 0) if i == 0 else 0
        kt_prev, v_prev = _swa_block(hide_previous, sink_ref, q_ref.at[rows], k_ref.at[rows], v_ref.at[rows],
                                     cos_ref.at[rows], sin_ref.at[rows], out_ref.at[rows], kt_prev, v_prev)
        _sgu_chunk(slice(i * CHUNK, (i + 1) * CHUNK), su_ref, sv_ref, ws_ref, bias_ref, sg_ref, sb_ref, gated_ref)
    kt_ref[...] = kt_prev
    vv_ref[...] = v_prev


def _swa_block(first_block_shift, sink_ref, q_ref, k_ref, v_ref, cos_ref, sin_ref, out_ref, kt_prev, v_prev):
    cos_t = cos_ref[...]
    sin_t = sin_ref[...]
    lane = lax.broadcasted_iota(jnp.int32, (CHUNK, LANES), 1)
    lo = lane < HEAD_DIM
    first = (lane % HEAD_DIM) < ROPE_DIM // 2

    kt_cur = _rope(k_ref[...], cos_t, sin_t, first).T.astype(BF16)
    v_cur = v_ref[...].astype(BF16)
    kt = jnp.concatenate([kt_prev, kt_cur], axis=1)
    vv = jnp.concatenate([v_prev, v_cur], axis=0)

    row = lax.broadcasted_iota(jnp.int32, (CHUNK, 2 * CHUNK), 0)
    col = lax.broadcasted_iota(jnp.int32, (CHUNK, 2 * CHUNK), 1)
    visible = jnp.logical_or(jnp.logical_and(col < CHUNK, col > row + first_block_shift),
                             jnp.logical_and(col >= CHUNK, col - CHUNK <= row))

    for j in range(ATTN_Q_HEADS // 2):
        sl = slice(j * LANES, (j + 1) * LANES)
        q_slab = _rope(q_ref[:, sl], cos_t, sin_t, first) * (HEAD_DIM ** -0.5)
        outs = []
        for half in range(2):
            sel = lo if half == 0 else jnp.logical_not(lo)
            sink = sink_ref[ATTN_HEAD_ORDER[2 * j + half]]
            q_m = jnp.where(sel, q_slab, 0.0).astype(BF16)
            s = jnp.where(visible, _dot(q_m, kt), -jnp.inf)
            mx = jnp.maximum(jnp.max(s, axis=-1, keepdims=True), sink)
            pexp = jnp.exp(s - mx)
            denom = jnp.sum(pexp, axis=-1, keepdims=True) + jnp.exp(sink - mx)
            outs.append((_dot(pexp.astype(BF16), vv), denom))
        (num_lo, den_lo), (num_hi, den_hi) = outs
        out_ref[:, sl] = (jnp.where(lo, num_lo, num_hi) / jnp.where(lo, den_lo, den_hi)).astype(out_ref.dtype)

    return kt_cur, v_cur


def _swa_sgu(proj3, cos_t, sin_t, sinks, w_tril, bias_tok, norm_g, norm_b):
    b, s, _ = proj3.shape
    rows = SWA_BLOCKS_PER_STEP * CHUNK
    assert s % rows == 0
    blk = lambda w, off: pl.BlockSpec((None, rows, w), lambda i, c: (i, c, off // w))
    tab = pl.BlockSpec((None, rows, LANES), lambda i, c: (i, c, 0))
    const2 = lambda shape: pl.BlockSpec(shape, lambda i, c: (0,) * len(shape))
    out = lambda w: pl.BlockSpec((None, rows, w), lambda i, c: (i, c, 0))
    return pl.pallas_call(
        _swa_kernel,
        grid=(b, s // rows),
        in_specs=[pl.BlockSpec(memory_space=pltpu.SMEM),
                  blk(ATTN_W, AQ_OFF), blk(ATTN_KV_W, AK_OFF), blk(ATTN_KV_W, AV_OFF), tab, tab,
                  blk(SGU_W, SU_OFF), blk(SGU_W, SV_OFF), const2((SGU_GROUPS, CHUNK, CHUNK)),
                  const2((CHUNK, SGU_W)), const2((1, SGU_W)), const2((1, SGU_W))],
        out_specs=[out(ATTN_W), out(SGU_W)],
        out_shape=[jax.ShapeDtypeStruct((b, s, ATTN_W), BF16), jax.ShapeDtypeStruct((b, s, SGU_W), BF16)],
        scratch_shapes=[pltpu.VMEM((LANES, CHUNK), BF16), pltpu.VMEM((CHUNK, LANES), BF16)],
        compiler_params=_params("parallel", "arbitrary"),
        name="swa_sgu",
    )(sinks, proj3, proj3, proj3, cos_t, sin_t, proj3, proj3, w_tril, bias_tok, norm_g, norm_b)


def _sgu_chunk(rows, u_ref, v_ref, w_ref, bias_ref, g_ref, b_ref, out_ref):
    lo = _lane_lo((CHUNK, LANES))
    for j in range(SGU_GROUPS // 2):
        sl = slice(j * LANES, (j + 1) * LANES)
        u = jax.nn.gelu(u_ref[rows, sl])
        v = jax.nn.gelu(v_ref[rows, sl])
        vn = (_half_layer_norm(v, lo) * g_ref[:, sl] + b_ref[:, sl]).astype(BF16)
        mixed = jnp.where(lo, _dot(w_ref[2 * j], vn), _dot(w_ref[2 * j + 1], vn)) + bias_ref[:, sl]
        out_ref[rows, sl] = (u * mixed).astype(out_ref.dtype)


def _outproj_kernel(ha_ref, hb_ref, hc_ref, x_ref, w_ref, g_ref, b_ref, *rest):
    cat_ref = rest[-1]
    cat_ref[:, 0:MLSTM_W] = ha_ref[...]
    cat_ref[:, MLSTM_W:MLSTM_W + ATTN_W] = hb_ref[...]
    cat_ref[:, MLSTM_W + ATTN_W:] = hc_ref[...]
    mix = _dot(cat_ref[...], w_ref[...])
    y = _layer_norm_rows(DN_ALPHA * x_ref[...] + mix, g_ref[...], b_ref[...])
    rest = rest[:-1]
    if len(rest) == 1:
        rest[0][...] = y
        return
    w_split_ref, rb_ref, o_ref, route_ref, route_t_ref = rest
    o_ref[...] = y
    route_ref[...], route_t_ref[...] = _route_rows(y, w_split_ref[...], rb_ref[...])


def _outproj_ln(ha, hb, hc, x2d, w, g, b, router=None):
    n = x2d.shape[0]
    rows = lambda w: pl.BlockSpec((OUTPROJ_ROWS, w), lambda i: (i, 0))
    const = lambda shape: pl.BlockSpec(shape, lambda i: (0, 0))
    in_specs = [rows(MLSTM_W), rows(ATTN_W), rows(SGU_W), rows(D_MODEL),
                const((D_MODEL, D_MODEL)), const((1, D_MODEL)), const((1, D_MODEL))]
    out_specs = [rows(D_MODEL)]
    out_shape = [jax.ShapeDtypeStruct((n, D_MODEL), F32)]
    operands = (ha, hb, hc, x2d, w, g, b)
    if router is not None:
        in_specs += [const((D_MODEL, 2 * LANES)), const((1, LANES))]
        out_specs += [rows(LANES), pl.BlockSpec((N_EXPERTS, OUTPROJ_ROWS), lambda i: (0, i))]
        out_shape += [jax.ShapeDtypeStruct((n, LANES), F32), jax.ShapeDtypeStruct((N_EXPERTS, n), F32)]
        operands += tuple(router)
    out = pl.pallas_call(
        _outproj_kernel,
        grid=(n // OUTPROJ_ROWS,),
        in_specs=in_specs,
        out_specs=out_specs,
        out_shape=out_shape,
        scratch_shapes=[pltpu.VMEM((OUTPROJ_ROWS, D_MODEL), BF16)],
        compiler_params=_params("parallel"),
        name="outproj_ln" if router is None else "outproj_ln_route",
    )(*operands)
    return out[0] if router is None else tuple(out)


def _ffn_kernel(x_ref, wg_ref, wu_ref, wd_ref, g_ref, b_ref, o_ref, xb_ref, wgu_ref):
    j = pl.program_id(1)
    last = pl.num_programs(1) - 1
    n_sub = FFN_ROWS // FFN_SUB
    wgu_ref[:, :FF_TILE] = wg_ref[...]
    wgu_ref[:, FF_TILE:] = wu_ref[...]

    def sub_rows(m):
        return pl.ds(pl.multiple_of(m * FFN_SUB, FFN_SUB), FFN_SUB)

    def ffn_tile(m):
        gu = _dot(xb_ref[sub_rows(m), :], wgu_ref[...])
        hidden = (jax.nn.silu(gu[:, :FF_TILE]) * gu[:, FF_TILE:]).astype(BF16)
        return _dot(hidden, wd_ref[...])

    @pl.when(j == 0)
    def _():
        def body(m, carry):
            xb_ref[sub_rows(m), :] = x_ref[sub_rows(m), :].astype(BF16)
            o_ref[sub_rows(m), :] = ffn_tile(m)
            return carry

        lax.fori_loop(0, n_sub, body, 0)

    @pl.when(jnp.logical_and(j > 0, j < last))
    def _():
        def body(m, carry):
            o_ref[sub_rows(m), :] += ffn_tile(m)
            return carry

        lax.fori_loop(0, n_sub, body, 0)

    @pl.when(j == last)
    def _():
        def body(m, carry):
            rows = sub_rows(m)
            ff = o_ref[rows, :] + ffn_tile(m)
            o_ref[rows, :] = _layer_norm_rows(DN_ALPHA * x_ref[rows, :] + ff, g_ref[...], b_ref[...])
            return carry

        lax.fori_loop(0, n_sub, body, 0)


def _ffn_ln(x2d, wg, wu, wd, g, b):
    n = x2d.shape[0]
    d_ff = wd.shape[0]
    assert d_ff // FF_TILE >= 2
    tm = FFN_ROWS
    return pl.pallas_call(
        _ffn_kernel,
        grid=(n // tm, d_ff // FF_TILE),
        in_specs=[pl.BlockSpec((tm, D_MODEL), lambda i, j: (i, 0)),
                  pl.BlockSpec((D_MODEL, FF_TILE), lambda i, j: (0, j)),
                  pl.BlockSpec((D_MODEL, FF_TILE), lambda i, j: (0, j)),
                  pl.BlockSpec((FF_TILE, D_MODEL), lambda i, j: (j, 0)),
                  pl.BlockSpec((1, D_MODEL), lambda i, j: (0, 0)),
                  pl.BlockSpec((1, D_MODEL), lambda i, j: (0, 0))],
        out_specs=pl.BlockSpec((tm, D_MODEL), lambda i, j: (i, 0)),
        out_shape=jax.ShapeDtypeStruct((n, D_MODEL), F32),
        scratch_shapes=[pltpu.VMEM((tm, D_MODEL), BF16), pltpu.VMEM((D_MODEL, 2 * FF_TILE), BF16)],
        compiler_params=_params("parallel", "arbitrary"),
        name="ffn_ln",
    )(x2d, wg, wu, wd, g, b)


def _route_rows(x, w_split, bias):
    n = x.shape[0]
    xh = x.astype(BF16)
    xl = (x - xh.astype(F32)).astype(BF16)
    prod = _dot(jnp.concatenate([xh, xl], axis=0), w_split)
    logits = prod[:n, :LANES] + prod[:n, LANES:] + prod[n:, :LANES] + prod[n:, LANES:] + bias
    lt = logits.T[0:N_EXPERTS, :]
    expert = lax.broadcasted_iota(jnp.int32, lt.shape, 0)
    m1 = jnp.max(lt, axis=0, keepdims=True)
    i1 = jnp.min(jnp.where(lt == m1, expert, N_EXPERTS), axis=0, keepdims=True)
    rest = jnp.where(expert == i1, -jnp.inf, lt)
    m2 = jnp.max(rest, axis=0, keepdims=True)
    i2 = jnp.min(jnp.where(rest == m2, expert, N_EXPERTS), axis=0, keepdims=True)
    e2 = jnp.exp(m2 - m1)
    g1 = 1.0 / (1.0 + e2)
    g2 = e2 / (1.0 + e2)
    route_t = jnp.where(expert == 0, g1, jnp.where(expert == 1, g2, jnp.where(
        expert == 2, i1.astype(F32), jnp.where(expert == 3, i2.astype(F32), 0.0))))
    padded = jnp.concatenate([route_t, jnp.zeros((LANES - N_EXPERTS, n), F32)], axis=0)
    return padded.T, route_t


TOKEN_TILE = (SUBLANES, D_MODEL // SUBLANES)


def _dispatch_kernel(slot0_ref, slot1_ref, ends_ref, x_ref, o_hbm, stage_ref, zero_ref, sem, zero_sem):
    t = pl.program_id(0)
    last = pl.num_programs(0) - 1
    s = t % 2

    def wait_tile(ss):
        for _ in range(TOP_K):
            pltpu.make_async_copy(zero_ref, o_hbm.at[pl.ds(0, MOE_SUB)], sem.at[ss]).wait()

    @pl.when(t == 0)
    def _():
        zero_ref[...] = jnp.zeros_like(zero_ref)

        def zero_sub_tile(m, carry):
            dst = o_hbm.at[pl.ds(pl.multiple_of(m * MOE_SUB, MOE_SUB), MOE_SUB)]
            cp = pltpu.make_async_copy(zero_ref, dst, zero_sem)
            cp.start()
            cp.wait()
            return carry

        for e in range(N_EXPERTS):
            @pl.when(ends_ref[e] >= MOE_SUB)
            def _():
                zero_sub_tile(ends_ref[e] // MOE_SUB - 1, 0)

        lax.fori_loop(ends_ref[N_EXPERTS - 1] // MOE_SUB, o_hbm.shape[0] // MOE_SUB, zero_sub_tile, 0)

    @pl.when(t >= 2)
    def _():
        wait_tile(s)

    stage_ref[s] = x_ref[...].reshape((MOE_SUB,) + TOKEN_TILE)

    def issue(i, carry):
        for u in range(SUBLANES):
            r = i * SUBLANES + u
            for k, slot_ref in enumerate((slot0_ref, slot1_ref)):
                pltpu.make_async_copy(stage_ref.at[s, r], o_hbm.at[slot_ref[t * MOE_SUB + r]],
                                      sem.at[s]).start(priority=k)
        return carry

    lax.fori_loop(0, MOE_SUB // SUBLANES, issue, 0)

    @pl.when(t == last)
    def _():
        wait_tile(s)
        wait_tile(1 - s)


def _dispatch_rows(x2d, slots, ends, n_rows):
    n = x2d.shape[0]
    assert n // MOE_SUB >= 2
    return pl.pallas_call(
        _dispatch_kernel,
        grid_spec=pltpu.PrefetchScalarGridSpec(
            num_scalar_prefetch=3,
            grid=(n // MOE_SUB,),
            in_specs=[pl.BlockSpec((MOE_SUB, D_MODEL), lambda t, s0, s1, en: (t, 0))],
            out_specs=pl.BlockSpec(memory_space=pl.ANY),
            scratch_shapes=[pltpu.VMEM((2, MOE_SUB) + TOKEN_TILE, F32), pltpu.VMEM((MOE_SUB,) + TOKEN_TILE, F32),
                            pltpu.SemaphoreType.DMA((2,)), pltpu.SemaphoreType.DMA(())]),
        out_shape=jax.ShapeDtypeStruct((n_rows,) + TOKEN_TILE, F32),
        compiler_params=_params("arbitrary"),
        name="moe_dispatch",
    )(slots[0], slots[1], ends, x2d)


def _moe_kernel(exp_ref, row0_ref, nsub_ref, tail_ref, x_hbm, wg_ref, wu_ref, wd_ref, y_hbm,
                xb_ref, acc_ref, wgu_ref, wdb_ref, stage_ref, in_sem, out_sem):
    v = pl.program_id(0)
    j = pl.program_id(1)
    last = pl.num_programs(1) - 1
    n_sub = nsub_ref[v]
    row0 = row0_ref[v]

    def sub_rows(m):
        return pl.ds(pl.multiple_of(m * MOE_SUB, MOE_SUB), MOE_SUB)

    def hbm_rows(m):
        return pl.ds(pl.multiple_of(row0 + m * MOE_SUB, MOE_SUB), MOE_SUB)

    @pl.when(jnp.logical_and(v == 0, j == 0))
    def _():
        stage_ref[0] = jnp.zeros((MOE_SUB,) + TOKEN_TILE, F32)

        def zero_sub_tile(m, carry):
            cp = pltpu.make_async_copy(stage_ref.at[0], y_hbm.at[sub_rows(m)], out_sem.at[0])
            cp.start()
            cp.wait()
            return carry

        lax.fori_loop(tail_ref[0], y_hbm.shape[0] // MOE_SUB, zero_sub_tile, 0)

    @pl.when(n_sub > 0)
    def _():
        wgu_ref[:, :MOE_FF_TILE] = wg_ref[...].astype(BF16)
        wgu_ref[:, MOE_FF_TILE:] = wu_ref[...].astype(BF16)
        wdb_ref[...] = wd_ref[...].astype(BF16)

        def ffn_rows(rows):
            gu = _dot(xb_ref[rows, :], wgu_ref[...])
            hidden = (jax.nn.silu(gu[:, :MOE_FF_TILE]) * gu[:, MOE_FF_TILE:]).astype(BF16)
            return _dot(hidden, wdb_ref[...])

        def ffn_tile(m):
            return ffn_rows(sub_rows(m))

        def accumulate(after_sub_tile):
            def pair(k, carry):
                rows = pl.ds(pl.multiple_of(2 * k * MOE_SUB, 2 * MOE_SUB), 2 * MOE_SUB)
                acc_ref[rows, :] += ffn_rows(rows)
                after_sub_tile(2 * k)
                after_sub_tile(2 * k + 1)
                return carry

            lax.fori_loop(0, n_sub // 2, pair, 0)

            @pl.when(n_sub % 2 == 1)
            def _():
                acc_ref[sub_rows(n_sub - 1), :] += ffn_tile(n_sub - 1)
                after_sub_tile(n_sub - 1)

        def y_copy(m):
            return pltpu.make_async_copy(stage_ref.at[m % 2], y_hbm.at[hbm_rows(m)], out_sem.at[m % 2])

        def send_sub_tile(m):
            @pl.when(m >= 2)
            def _():
                y_copy(m - 2).wait()

            stage_ref[m % 2] = acc_ref[sub_rows(m), :].reshape((MOE_SUB,) + TOKEN_TILE)
            y_copy(m).start()

        @pl.when(j == 0)
        def _():
            def x_copy(m):
                return pltpu.make_async_copy(x_hbm.at[hbm_rows(m)], stage_ref.at[m % 2], in_sem.at[m % 2])

            x_copy(0).start()

            def body(m, carry):
                @pl.when(m + 1 < n_sub)
                def _():
                    x_copy(m + 1).start()

                x_copy(m).wait()
                xb_ref[sub_rows(m), :] = stage_ref[m % 2].reshape(MOE_SUB, D_MODEL).astype(BF16)
                acc_ref[sub_rows(m), :] = ffn_tile(m)
                return carry

            lax.fori_loop(0, n_sub, body, 0)

        @pl.when(jnp.logical_and(j > 0, j < last))
        def _():
            accumulate(lambda m: None)

        @pl.when(j == last)
        def _():
            accumulate(send_sub_tile)

            @pl.when(n_sub >= 2)
            def _():
                y_copy(n_sub - 2).wait()

            y_copy(n_sub - 1).wait()


def _moe_grouped(xs, wg, wu, wd, visit_exp, visit_row0, visit_nsub, tail_sub):
    n_rows = xs.shape[0]
    n_visits = visit_exp.shape[0]
    d_ff = wg.shape[2]
    assert d_ff // MOE_FF_TILE >= 2
    rows = MOE_VISIT_SUBS * MOE_SUB
    w_in = lambda v, j, e, r, ns, tl: (e[v], 0, j)
    w_out = lambda v, j, e, r, ns, tl: (e[v], j, 0)
    return pl.pallas_call(
        _moe_kernel,
        grid_spec=pltpu.PrefetchScalarGridSpec(
            num_scalar_prefetch=4,
            grid=(n_visits, d_ff // MOE_FF_TILE),
            in_specs=[pl.BlockSpec(memory_space=pl.ANY),
                      pl.BlockSpec((None, D_MODEL, MOE_FF_TILE), w_in),
                      pl.BlockSpec((None, D_MODEL, MOE_FF_TILE), w_in),
                      pl.BlockSpec((None, MOE_FF_TILE, D_MODEL), w_out)],
            out_specs=pl.BlockSpec(memory_space=pl.ANY),
            scratch_shapes=[pltpu.VMEM((rows, D_MODEL), BF16), pltpu.VMEM((rows, D_MODEL), F32),
                            pltpu.VMEM((D_MODEL, 2 * MOE_FF_TILE), BF16),
                            pltpu.VMEM((MOE_FF_TILE, D_MODEL), BF16),
                            pltpu.VMEM((2, MOE_SUB) + TOKEN_TILE, F32),
                            pltpu.SemaphoreType.DMA((2,)), pltpu.SemaphoreType.DMA((2,))]),
        out_shape=jax.ShapeDtypeStruct((n_rows,) + TOKEN_TILE, F32),
        compiler_params=_params("arbitrary", "arbitrary"),
        name="moe_grouped",
    )(visit_exp, visit_row0, visit_nsub, tail_sub, xs, wg, wu, wd)


def _combine_kernel(slot0_ref, slot1_ref, x_ref, gate_ref, y_hbm, g_ref, b_ref, o_ref, buf_ref, sem):
    t = pl.program_id(0)
    s = t % 2

    def issue_tile(tt, ss):
        def issue(i, carry):
            for u in range(SUBLANES):
                r = i * SUBLANES + u
                for k, slot_ref in enumerate((slot0_ref, slot1_ref)):
                    pltpu.make_async_copy(y_hbm.at[slot_ref[tt * MOE_SUB + r]], buf_ref.at[ss, k, r],
                                          sem.at[ss]).start(priority=k)
            return carry

        lax.fori_loop(0, MOE_SUB // SUBLANES, issue, 0)

    @pl.when(t == 0)
    def _():
        issue_tile(0, 0)

    @pl.when(t + 1 < pl.num_programs(0))
    def _():
        issue_tile(t + 1, 1 - s)

    for k in range(TOP_K):
        pltpu.make_async_copy(y_hbm.at[pl.ds(0, MOE_SUB)], buf_ref.at[s, k], sem.at[s]).wait()
    gate = gate_ref[...]
    y0 = buf_ref[s, 0].reshape(MOE_SUB, D_MODEL)
    y1 = buf_ref[s, 1].reshape(MOE_SUB, D_MODEL)
    ff = gate[:, 0:1] * y0 + gate[:, 1:2] * y1
    o_ref[...] = _layer_norm_rows(DN_ALPHA * x_ref[...] + ff, g_ref[...], b_ref[...])


def _combine_ln(x2d, route, slots, ys, g, b):
    n = x2d.shape[0]
    return pl.pallas_call(
        _combine_kernel,
        grid_spec=pltpu.PrefetchScalarGridSpec(
            num_scalar_prefetch=2,
            grid=(n // MOE_SUB,),
            in_specs=[pl.BlockSpec((MOE_SUB, D_MODEL), lambda t, s0, s1: (t, 0)),
                      pl.BlockSpec((MOE_SUB, LANES), lambda t, s0, s1: (t, 0)),
                      pl.BlockSpec(memory_space=pl.ANY),
                      pl.BlockSpec((1, D_MODEL), lambda t, s0, s1: (0, 0)),
                      pl.BlockSpec((1, D_MODEL), lambda t, s0, s1: (0, 0))],
            out_specs=pl.BlockSpec((MOE_SUB, D_MODEL), lambda t, s0, s1: (t, 0)),
            scratch_shapes=[pltpu.VMEM((2, TOP_K, MOE_SUB) + TOKEN_TILE, F32), pltpu.SemaphoreType.DMA((2,))]),
        out_shape=jax.ShapeDtypeStruct((n, D_MODEL), F32),
        compiler_params=_params("arbitrary"),
        name="moe_combine_ln",
    )(slots[0], slots[1], x2d, route, ys, g, b)


def _routing_tables(route_t, n):
    idx = route_t[2:4].astype(jnp.int32)
    expert = jnp.arange(N_EXPERTS, dtype=jnp.int32)[:, None]
    chosen = [idx[k][None, :] == expert for k in range(TOP_K)]
    onehot = jnp.logical_or(chosen[0], chosen[1]).astype(jnp.int32)
    rank = jnp.cumsum(onehot, axis=1) - onehot
    counts = jnp.sum(onehot, axis=1)
    padded = ((counts + MOE_SUB - 1) // MOE_SUB) * MOE_SUB
    ends = jnp.cumsum(padded)
    starts = ends - padded
    place = starts[:, None] + rank
    slot = [jnp.sum(jnp.where(chosen[k], place, 0), axis=0).astype(jnp.int32) for k in range(TOP_K)]

    n_rows = -(-(n * TOP_K + N_EXPERTS * (MOE_SUB - 1)) // MOE_SUB) * MOE_SUB

    visit_rows = MOE_VISIT_SUBS * MOE_SUB
    max_chunks = -(-n_rows // visit_rows)
    chunk = jnp.arange(max_chunks, dtype=jnp.int32)[None, :]
    left = padded[:, None] - chunk * visit_rows
    valid = (left > 0).reshape(-1)
    n_visits = n_rows // visit_rows + N_EXPERTS
    order = jnp.argsort(jnp.logical_not(valid), stable=True)[:n_visits]
    n_valid = jnp.sum(valid.astype(jnp.int32))
    live = jnp.arange(n_visits) < n_valid
    order = jnp.where(live, order, order[jnp.maximum(n_valid - 1, 0)])
    v_exp = (order // max_chunks).astype(jnp.int32)
    v_chunk = (order % max_chunks).astype(jnp.int32)
    v_row0 = jnp.where(live, starts[v_exp] + v_chunk * visit_rows, 0).astype(jnp.int32)
    v_nsub = jnp.where(live, jnp.minimum(left.reshape(-1)[order], visit_rows) // MOE_SUB, 0).astype(jnp.int32)
    tail_sub = (ends[N_EXPERTS - 1:] // MOE_SUB).astype(jnp.int32)
    return slot, ends.astype(jnp.int32), n_rows, (v_exp, v_row0, v_nsub, tail_sub)


def _router_operands(w_router, b_router):
    w_pad = jnp.zeros((D_MODEL, LANES), F32).at[:, :N_EXPERTS].set(w_router)
    w_hi = w_pad.astype(BF16)
    w_lo = (w_pad - w_hi.astype(F32)).astype(BF16)
    b_pad = jnp.zeros((1, LANES), F32).at[0, :N_EXPERTS].set(b_router)
    return jnp.concatenate([w_hi, w_lo], axis=1), b_pad


def _moe_ln(x2d, route, route_t, wg, wu, wd, g, b):
    n = x2d.shape[0]
    slots, ends, n_rows, visits = _routing_tables(route_t, n)
    xs = _dispatch_rows(x2d, slots, ends, n_rows)
    ys = _moe_grouped(xs, wg, wu, wd, *visits)
    return _combine_ln(x2d, route, slots, ys, g, b)


_SRC_GATES = 4 * MLSTM_W
_SRC_AQ = _SRC_GATES + 2 * MLSTM_HEADS
_SRC_AK = _SRC_AQ + ATTN_W
_IN_PROJ_MOVES = (
    ((0, 0, 4 * MLSTM_W),)
    + tuple((_SRC_AQ + h * HEAD_DIM, AQ_OFF + i * HEAD_DIM, HEAD_DIM) for i, h in enumerate(ATTN_HEAD_ORDER))
    + ((_SRC_AK, AK_OFF, ATTN_KV_W), (_SRC_AK + ATTN_KV_W, AV_OFF, ATTN_KV_W),
       (_SRC_AK + 2 * ATTN_KV_W, SU_OFF, SGU_W), (_SRC_AK + 2 * ATTN_KV_W + SGU_W, SV_OFF, SGU_W)))
D_PROJ = _SRC_AK + 2 * ATTN_KV_W + 2 * SGU_W


def _relayout_columns(src, dst_dtype):
    source = np.full((D_PROJ_PAD,), D_PROJ, np.int32)
    for s, d, w in _IN_PROJ_MOVES + ((_SRC_GATES, GATE_OFF, 2 * MLSTM_HEADS),):
        source[d:d + w] = np.arange(s, s + w)
    padded = jnp.concatenate([src, jnp.zeros(src.shape[:-1] + (1,), src.dtype)], axis=-1)
    return jnp.take(padded, source, axis=-1).astype(dst_dtype)


def _w_layout_kernel(w_ref, o_ref):
    for s, d, w in _IN_PROJ_MOVES:
        o_ref[:, d:d + w] = w_ref[:, s:s + w].astype(o_ref.dtype)
    pad = jnp.zeros((w_ref.shape[0], LANES - 2 * MLSTM_HEADS), w_ref.dtype)
    gates = jnp.concatenate([w_ref[:, _SRC_GATES:_SRC_GATES + 2 * MLSTM_HEADS], pad], axis=1)
    o_ref[:, GATE_OFF:GATE_OFF + LANES] = gates.astype(o_ref.dtype)


def _layout_in_proj(w_in, b_in):
    rows = D_MODEL // 4
    w_in_b = w_in.astype(BF16)

    def one_layer(layer):
        return pl.pallas_call(
            _w_layout_kernel,
            grid=(D_MODEL // rows,),
            in_specs=[pl.BlockSpec((None, rows, D_PROJ), lambda i: (layer, i, 0))],
            out_specs=pl.BlockSpec((rows, D_PROJ_PAD), lambda i: (i, 0)),
            out_shape=jax.ShapeDtypeStruct((D_MODEL, D_PROJ_PAD), BF16),
            compiler_params=_params("parallel"),
            name="w_in_layout",
        )(w_in_b)

    return [one_layer(layer) for layer in range(w_in.shape[0])], _relayout_columns(b_in, F32)[:, None, :]


def _rope_tables(positions):
    inv_freq = ROPE_THETA ** (-jnp.arange(0, ROPE_DIM, 2, dtype=F32) / ROPE_DIM)
    ang = inv_freq[None, :, None] * positions.astype(F32)[:, None, :]
    cos, sin = jnp.cos(ang), jnp.sin(ang)
    ones = jnp.ones((ang.shape[0], HEAD_DIM - ROPE_DIM, ang.shape[2]), F32)
    cos_head = jnp.concatenate([cos, cos, ones], 1)
    sin_head = jnp.concatenate([-sin, sin, 0.0 * ones], 1)
    cos_t = jnp.concatenate([cos_head, cos_head], 1)
    sin_t = jnp.concatenate([sin_head, sin_head], 1)
    return _to_token_major(cos_t, sin_t)


def _table_transpose_kernel(cos_ref, sin_ref, cos_out, sin_out):
    cos_out[...] = cos_ref[...].T
    sin_out[...] = sin_ref[...].T


def _to_token_major(cos_t, sin_t):
    b, lanes, s = cos_t.shape
    rows = min(s, FFN_ROWS)
    src = pl.BlockSpec((None, lanes, rows), lambda i, c: (i, 0, c))
    dst = pl.BlockSpec((None, rows, lanes), lambda i, c: (i, c, 0))
    shape = jax.ShapeDtypeStruct((b, s, lanes), F32)
    return pl.pallas_call(
        _table_transpose_kernel,
        grid=(b, s // rows),
        in_specs=[src, src],
        out_specs=[dst, dst],
        out_shape=[shape, shape],
        compiler_params=_params("parallel", "parallel"),
        name="rope_tables",
    )(cos_t, sin_t)


def kernel(x, positions, w_in, b_in, conv_w, mlstm_norm_g, attn_sinks, sgu_w_s, sgu_b_s, sgu_norm_g, sgu_norm_b, w_out, ln1_g, ln1_b, ln2_g, ln2_b, ffn_w_gate, ffn_w_up, ffn_w_down, moe_w_router, moe_b_router, moe_w_gate, moe_w_up, moe_w_down):
    bsz, seq, _ = x.shape
    n = bsz * seq
    cos_t, sin_t = _rope_tables(positions)
    tril = jnp.tril(jnp.ones((CHUNK, CHUNK), bool))
    x2d = x.reshape(n, D_MODEL)
    w_p, b_p = _layout_in_proj(w_in, b_in)
    for layer in range(DEPTH):
        proj3 = _inproj(x2d, w_p[layer], b_p[layer]).reshape(bsz, seq, D_PROJ_PAD)
        h_a = _mlstm(proj3, conv_w[layer], mlstm_norm_g[layer][None, :])
        w_tril = jnp.where(tril, sgu_w_s[layer], 0.0).astype(BF16)
        bias_tok = jnp.repeat(sgu_b_s[layer].T, HEAD_DIM, axis=1)
        h_b, h_c = _swa_sgu(proj3, cos_t, sin_t, attn_sinks[layer], w_tril, bias_tok,
                            sgu_norm_g[layer][None, :], sgu_norm_b[layer][None, :])
        wo = w_out[layer]
        wb = wo[MLSTM_W:MLSTM_W + ATTN_W].reshape(ATTN_Q_HEADS, HEAD_DIM, D_MODEL)[np.array(ATTN_HEAD_ORDER)]
        wo = jnp.concatenate([wo[:MLSTM_W], wb.reshape(ATTN_W, D_MODEL), wo[MLSTM_W + ATTN_W:]], 0).astype(BF16)
        j = layer // 2
        dense = layer % 2 == 0
        mixed = _outproj_ln(h_a.reshape(n, MLSTM_W), h_b.reshape(n, ATTN_W), h_c.reshape(n, SGU_W), x2d,
                            wo, ln1_g[layer][None, :], ln1_b[layer][None, :],
                            router=None if dense else _router_operands(moe_w_router[j], moe_b_router[j]))
        g2, b2 = ln2_g[layer][None, :], ln2_b[layer][None, :]
        if dense:
            x2d = _ffn_ln(mixed, ffn_w_gate[j].astype(BF16), ffn_w_up[j].astype(BF16),
                          ffn_w_down[j].astype(BF16), g2, b2)
        else:
            x2d, route, route_t = mixed
            x2d = _moe_ln(x2d, route, route_t, moe_w_gate[j], moe_w_up[j], moe_w_down[j], g2, b2)
    return x2d.reshape(bsz, seq, D_MODEL)
```
